```python
import math
import jax, jax.numpy as jnp
from jax import lax
import numpy as np

D_MODEL = 1024
BATCH = 8
SEQ = 2048
DEPTH = 2
DEC_BATCH = 128
DEC_SEQ = 8
PAST_LEN = 16384
PAGE_SIZE = 128

POOL_WIDTH = D_MODEL // 2
POOL_WINDOWS = (2, 4, 8, 16)
POOL_GROUPS = len(POOL_WINDOWS)
POOL_GROUP_DIM = POOL_WIDTH // POOL_GROUPS
POOL_BUF = max(POOL_WINDOWS) - 1
RET_WIDTH = D_MODEL - POOL_WIDTH
RET_HEADS = 4
RET_HEAD_DIM = RET_WIDTH // RET_HEADS
RET_CHUNK = 128
ROPE_BASE = 10000.0
IN_WIDTH = POOL_WIDTH + 4 * RET_WIDTH
N_EXPERT_GROUPS = 4
EXPERTS_PER_GROUP = 4
N_EXPERTS = N_EXPERT_GROUPS * EXPERTS_PER_GROUP
EXPERT_TOP_K = 2
D_EXPERT = D_MODEL // 4
RMS_EPS = 1e-6
GN_EPS = 1e-5

kernel_name = "hybrid_pool_retention_hmoe_step"


def rmsnorm(x, g):
    xf = x.astype(jnp.float32)
    y = xf * lax.rsqrt(jnp.mean(xf * xf, axis=-1, keepdims=True) + RMS_EPS)
    return (y * g.astype(jnp.float32)).astype(x.dtype)


def pool_mix(u, prev, n_prev, w_grp, scale):
    B, L, P = u.shape
    full = jnp.concatenate([prev, u], axis=1).astype(jnp.float32)
    c = lax.cumsum(full, axis=1)
    c = jnp.concatenate([jnp.zeros((B, 1, P), jnp.float32), c], axis=1)
    t_idx = jnp.arange(L)
    uf = u.astype(jnp.float32)
    outs = []
    for j, w in enumerate(POOL_WINDOWS):
        sl = slice(j * POOL_GROUP_DIM, (j + 1) * POOL_GROUP_DIM)
        s = c[:, POOL_BUF + 1:, sl] - c[:, POOL_BUF + 1 - w:POOL_BUF + 1 - w + L, sl]
        cnt = jnp.minimum(w, n_prev + t_idx + 1).astype(jnp.float32)
        outs.append(s / cnt[None, :, None] - uf[..., sl])
    d = jnp.concatenate(outs, axis=-1).reshape(B, L, POOL_GROUPS, POOL_GROUP_DIM)
    y = jnp.einsum('blgc,gce->blge', d, w_grp.astype(jnp.float32)).reshape(B, L, P)
    return y * scale.astype(jnp.float32)


def rope(x, pos):
    half = x.shape[-1] // 2
    inv = ROPE_BASE ** (-jnp.arange(half, dtype=jnp.float32) / half)
    ang = pos.astype(jnp.float32)[:, None] * inv[None, :]
    cos = jnp.cos(ang)[None, :, None, :]
    sin = jnp.sin(ang)[None, :, None, :]
    x1, x2 = x[..., :half], x[..., half:]
    return jnp.concatenate([x1 * cos - x2 * sin, x1 * sin + x2 * cos], axis=-1)


def retention(q, k, v, s0):
    B, H, L, D = q.shape
    C = RET_CHUNK if L % RET_CHUNK == 0 else L
    n = L // C
    log_g = jnp.log(1.0 - 2.0 ** (-5.0 - jnp.arange(H, dtype=jnp.float32)))
    i = jnp.arange(C, dtype=jnp.float32)
    diff = i[:, None] - i[None, :]
    decay_mat = jnp.where(diff[None] >= 0,
                          jnp.exp(jnp.maximum(diff, 0.0)[None] * log_g[:, None, None]), 0.0)
    q_decay = jnp.exp((i + 1.0)[None, :] * log_g[:, None])[..., None]
    k_decay = jnp.exp((C - 1.0 - i)[None, :] * log_g[:, None])[..., None]
    chunk_decay = jnp.exp(C * log_g)[:, None, None]

    def to_chunks(a):
        return a.reshape(B, H, n, C, D).transpose(2, 0, 1, 3, 4)

    def step(s, qkv):
        qc, kc, vc = qkv
        scores = jnp.einsum('bhid,bhjd->bhij', qc, kc) * decay_mat
        o = (jnp.einsum('bhij,bhjd->bhid', scores, vc)
             + jnp.einsum('bhid,bhde->bhie', qc * q_decay, s))
        s_new = s * chunk_decay + jnp.einsum('bhjd,bhje->bhde', kc * k_decay, vc)
        return s_new, o

    s_fin, o = lax.scan(step, s0, (to_chunks(q), to_chunks(k), to_chunks(v)))
    o = o.transpose(1, 2, 0, 3, 4).reshape(B, H, L, D)
    return o, s_fin


def mixer_block(x, pool_prev, n_prev, ret_s0, pos, norm_g, w_in, w_pool, pool_scale, gn_g, w_out):
    B, L, _ = x.shape
    h = rmsnorm(x, norm_g)
    z = h @ w_in
    P, R = POOL_WIDTH, RET_WIDTH
    u, q, k, v, g = jnp.split(z, [P, P + R, P + 2 * R, P + 3 * R], axis=-1)
    pool_out = pool_mix(u, pool_prev, n_prev, w_pool, pool_scale)
    new_buf = jnp.concatenate([pool_prev, u.astype(pool_prev.dtype)], axis=1)[:, -POOL_BUF:]
    shp = (B, L, RET_HEADS, RET_HEAD_DIM)
    qh = rope(q.reshape(shp).astype(jnp.float32), pos)
    kh = rope(k.reshape(shp).astype(jnp.float32), pos) * (RET_HEAD_DIM ** -0.5)
    vh = v.reshape(shp).astype(jnp.float32)
    o, s_fin = retention(qh.transpose(0, 2, 1, 3), kh.transpose(0, 2, 1, 3),
                         vh.transpose(0, 2, 1, 3), ret_s0.astype(jnp.float32))
    o = o.transpose(0, 2, 1, 3)
    mu = jnp.mean(o, axis=-1, keepdims=True)
    var = jnp.mean(jnp.square(o - mu), axis=-1, keepdims=True)
    o = ((o - mu) * lax.rsqrt(var + GN_EPS)).reshape(B, L, R) * gn_g.astype(jnp.float32)
    ret_out = jax.nn.silu(g.astype(jnp.float32)) * o
    mix = jnp.concatenate([pool_out, ret_out], axis=-1).astype(x.dtype) @ w_out
    return x + mix, new_buf, s_fin.astype(ret_s0.dtype)


def hier_moe(x, norm_g, w_rg, b_rg, w_re, b_re, w_gate, w_up, w_down):
    B, L, D = x.shape
    hf = rmsnorm(x, norm_g).reshape(B * L, D)
    T = B * L
    grp_logits = (hf @ w_rg).astype(jnp.float32) + b_rg.astype(jnp.float32)
    grp_prob = jax.nn.softmax(grp_logits, axis=-1)
    _, g_sel = lax.top_k(grp_logits, 1)
    exp_logits = ((hf @ w_re).astype(jnp.float32) + b_re.astype(jnp.float32)).reshape(
        T, N_EXPERT_GROUPS, EXPERTS_PER_GROUP)
    in_grp = jnp.take_along_axis(exp_logits, g_sel[:, :, None], axis=1)[:, 0]
    top_v, top_i = lax.top_k(in_grp, EXPERT_TOP_K)
    top_w = jax.nn.softmax(top_v, axis=-1) * jnp.take_along_axis(grp_prob, g_sel, axis=1)
    expert_idx = g_sel * EXPERTS_PER_GROUP + top_i
    gates = jnp.sum(jax.nn.one_hot(expert_idx, N_EXPERTS, dtype=jnp.float32) * top_w[..., None], axis=1)
    y = jnp.zeros((T, D), jnp.float32)
    for e in range(N_EXPERTS):
        act = jax.nn.silu((hf @ w_gate[e]).astype(jnp.float32)) * (hf @ w_up[e]).astype(jnp.float32)
        y = y + (act * gates[:, e:e + 1]).astype(x.dtype) @ w_down[e]
    return x + y.reshape(B, L, D).astype(x.dtype)


def setup_inputs(seed: int = 0) -> dict:
    key = jax.random.key(seed)
    ks = jax.random.split(key, 20)
    f32 = jnp.float32
    nrm = lambda k, s, sc: jax.random.normal(k, s, f32) * sc
    return {
        "x_prompt": nrm(ks[0], (BATCH, SEQ, D_MODEL), 1.0),
        "x_sample": nrm(ks[1], (DEC_BATCH, DEC_SEQ, D_MODEL), 1.0),
        "cache_pool": nrm(ks[2], (DEPTH, DEC_BATCH, POOL_BUF, POOL_WIDTH), 1.0),
        "state_ret": nrm(ks[3], (DEPTH, DEC_BATCH, RET_HEADS, RET_HEAD_DIM, RET_HEAD_DIM), 0.5),
        "norm_mix": 1.0 + nrm(ks[4], (DEPTH, D_MODEL), 0.02),
        "w_in": nrm(ks[5], (DEPTH, D_MODEL, IN_WIDTH), D_MODEL ** -0.5),
        "w_pool": nrm(ks[6], (DEPTH, POOL_GROUPS, POOL_GROUP_DIM, POOL_GROUP_DIM), POOL_GROUP_DIM ** -0.5),
        "pool_scale": 1.0 + nrm(ks[7], (DEPTH, POOL_WIDTH), 0.1),
        "ret_gn": 1.0 + nrm(ks[8], (DEPTH, RET_WIDTH), 0.02),
        "w_out": nrm(ks[9], (DEPTH, D_MODEL, D_MODEL), D_MODEL ** -0.5),
        "norm_ffn": 1.0 + nrm(ks[10], (DEPTH, D_MODEL), 0.02),
        "w_router_group": nrm(ks[11], (DEPTH, D_MODEL, N_EXPERT_GROUPS), D_MODEL ** -0.5),
        "b_router_group": nrm(ks[12], (DEPTH, N_EXPERT_GROUPS), 0.01),
        "w_router_expert": nrm(ks[13], (DEPTH, D_MODEL, N_EXPERTS), D_MODEL ** -0.5),
        "b_router_expert": nrm(ks[14], (DEPTH, N_EXPERTS), 0.01),
        "w_gate": nrm(ks[15], (DEPTH, N_EXPERTS, D_MODEL, D_EXPERT), D_MODEL ** -0.5),
        "w_up": nrm(ks[16], (DEPTH, N_EXPERTS, D_MODEL, D_EXPERT), D_MODEL ** -0.5),
        "w_down": nrm(ks[17], (DEPTH, N_EXPERTS, D_EXPERT, D_MODEL), D_EXPERT ** -0.5),
        "norm_final": 1.0 + nrm(ks[18], (D_MODEL,), 0.02),
    }


def reference(x_prompt, x_sample, cache_pool, state_ret, norm_mix, w_in, w_pool, pool_scale,
              ret_gn, w_out, norm_ffn, w_router_group, b_router_group, w_router_expert,
              b_router_expert, w_gate, w_up, w_down, norm_final):
    yp, ys = x_prompt, x_sample
    Bp, Lp, _ = x_prompt.shape
    Ls = x_sample.shape[1]
    pos_p = jnp.arange(Lp)
    pos_s = PAST_LEN + jnp.arange(Ls)
    n_prev_s = min(POOL_BUF, PAST_LEN)
    pool_p, ret_p, pool_s, ret_s = [], [], [], []
    for l in range(DEPTH):
        zero_buf = jnp.zeros((Bp, POOL_BUF, POOL_WIDTH), x_prompt.dtype)
        zero_state = jnp.zeros((Bp, RET_HEADS, RET_HEAD_DIM, RET_HEAD_DIM), x_prompt.dtype)
        mix_w = (norm_mix[l], w_in[l], w_pool[l], pool_scale[l], ret_gn[l], w_out[l])
        yp, bp, sp = mixer_block(yp, zero_buf, 0, zero_state, pos_p, *mix_w)
        ys, bs, ss = mixer_block(ys, cache_pool[l], n_prev_s, state_ret[l], pos_s, *mix_w)
        moe_w = (norm_ffn[l], w_router_group[l], b_router_group[l], w_router_expert[l],
                 b_router_expert[l], w_gate[l], w_up[l], w_down[l])
        yp = hier_moe(yp, *moe_w)
        ys = hier_moe(ys, *moe_w)
        pool_p.append(bp); ret_p.append(sp); pool_s.append(bs); ret_s.append(ss)
    y_prompt = rmsnorm(yp, norm_final)
    y_sample = rmsnorm(ys, norm_final)
    return (y_prompt, y_sample, jnp.stack(pool_p), jnp.stack(ret_p), jnp.stack(pool_s), jnp.stack(ret_s))
```

```python
import functools

import jax
import jax.numpy as jnp
from jax import lax
from jax.experimental import pallas as pl
from jax.experimental.pallas import tpu as pltpu

F32 = jnp.float32
BF16 = jnp.bfloat16

D_MODEL = 1024
POOL_WIDTH = 512
POOL_WINDOWS = (2, 4, 8, 16)
POOL_GROUP_DIM = 128
POOL_BUF = 15
POOL_HIST = 16
RET_WIDTH = 512
RET_HEADS = 4
RET_HEAD_DIM = 128
RET_CHUNK = 128
ROPE_BASE = 10000.0
IN_WIDTH = POOL_WIDTH + 4 * RET_WIDTH
N_EXPERT_GROUPS = 4
EXPERTS_PER_GROUP = 4
N_EXPERTS = 16
D_EXPERT = 256
RMS_EPS = 1e-6
GN_EPS = 1e-5
PAST_LEN = 16384

ROUTER_LANES = 128
EXPERT_LANE0 = N_EXPERT_GROUPS
CHUNK = 128
SAMPLE_SEQS = 16
VMEM_LIMIT = 56 * 1024 * 1024


def _dot(a, b):
    return jnp.dot(a, b, preferred_element_type=F32)


def _dot_nt(a, b):
    return lax.dot_general(a, b, (((1,), (1,)), ((), ())), preferred_element_type=F32)


def _rmsnorm(x, g):
    ms = jnp.mean(x * x, axis=-1, keepdims=True)
    return x * lax.rsqrt(ms + RMS_EPS) * g


def _pool_mix(ubuf, rows, t_first, n_prev, wpool_ref, pscale):
    ns = ubuf.shape[0]
    t = t_first + lax.broadcasted_iota(jnp.int32, (1, rows, POOL_GROUP_DIM), 1)
    outs = []
    for j, w in enumerate(POOL_WINDOWS):
        lanes = slice(j * POOL_GROUP_DIM, (j + 1) * POOL_GROUP_DIM)
        uj = ubuf[:, POOL_HIST:POOL_HIST + rows, lanes]
        acc = uj
        for i in range(1, w):
            acc = acc + ubuf[:, POOL_HIST - i:POOL_HIST - i + rows, lanes]
        cnt = jnp.minimum(w, n_prev + t + 1).astype(F32)
        d = (acc / cnt - uj).reshape(ns * rows, POOL_GROUP_DIM)
        outs.append(_dot(d.astype(BF16), wpool_ref[j]))
    return jnp.concatenate(outs, axis=-1) * pscale


def _rope(xh, cos2, sin2):
    return xh * cos2 + pltpu.roll(xh, RET_HEAD_DIM // 2, 1) * sin2


def _group_norm(o):
    mu = jnp.mean(o, axis=-1, keepdims=True)
    c = o - mu
    var = jnp.mean(c * c, axis=-1, keepdims=True)
    return c * lax.rsqrt(var + GN_EPS)


def _silu(x):
    return x * (1.0 / (1.0 + jnp.exp(-x)))


def _head(a, h):
    return a[:, h * RET_HEAD_DIM:(h + 1) * RET_HEAD_DIM]


def _qkvg(z):
    p, r = POOL_WIDTH, RET_WIDTH
    return z[:, p:p + r], z[:, p + r:p + 2 * r], z[:, p + 2 * r:p + 3 * r], z[:, p + 3 * r:p + 4 * r]


def _mixer_prompt_kernel(x_ref, cos_ref, sin_ref, ng_ref, win_ref, wpool_ref, pscale_ref, gn_ref,
                         wout_ref, dmat_ref, qdec_ref, kdec_ref, cdec_ref,
                         y_ref, tail_ref, sfin_ref, ubuf, s_scr, ret_scr, *, rows):
    c = pl.program_id(1)

    @pl.when(c == 0)
    def _():
        ubuf[:, 0:POOL_HIST, :] = jnp.zeros((1, POOL_HIST, POOL_WIDTH), F32)
        s_scr[...] = jnp.zeros_like(s_scr)

    x = x_ref[0]
    hn = _rmsnorm(x, ng_ref[...]).astype(BF16)
    z = _dot(hn, win_ref[...])
    ubuf[0, POOL_HIST:POOL_HIST + rows, :] = z[:, :POOL_WIDTH]
    pool_out = _pool_mix(ubuf, rows, c * rows, 0, wpool_ref, pscale_ref[...])
    tail_ref[...] = ubuf[:, rows:rows + POOL_HIST, :]
    ubuf[:, 0:POOL_HIST, :] = ubuf[:, rows:rows + POOL_HIST, :]

    q_all, k_all, v_all, g_all = _qkvg(z)
    scale = RET_HEAD_DIM ** -0.5
    for ci in range(rows // CHUNK):
        rs = slice(ci * CHUNK, (ci + 1) * CHUNK)
        cos2 = cos_ref[rs, :]
        sin2 = sin_ref[rs, :]
        for h in range(RET_HEADS):
            q = _rope(_head(q_all, h)[rs], cos2, sin2)
            k = _rope(_head(k_all, h)[rs], cos2, sin2) * scale
            v = _head(v_all, h)[rs]
            vb = v.astype(BF16)
            s_old = s_scr[h]
            scores = _dot_nt(q.astype(BF16), k.astype(BF16)) * dmat_ref[h]
            qd = q * _head(qdec_ref[...], h)
            o = _dot(scores.astype(BF16), vb) + _dot(qd.astype(BF16), s_old.astype(BF16))
            kd = k * _head(kdec_ref[...], h)
            s_scr[h] = s_old * cdec_ref[h] + _dot(kd.T.astype(BF16), vb)
            on = _group_norm(o) * _head(gn_ref[...], h)
            ret_scr[rs, h * RET_HEAD_DIM:(h + 1) * RET_HEAD_DIM] = _silu(_head(g_all, h)[rs]) * on

    mix_in = jnp.concatenate([pool_out, ret_scr[...]], axis=-1).astype(BF16)
    y_ref[0] = x + _dot(mix_in, wout_ref[...])
    sfin_ref[0] = s_scr[...]


def _mixer_sample_kernel(x_ref, prev_ref, s0_ref, cos_ref, sin_ref, ng_ref, win_ref, wpool_ref,
                         pscale_ref, gn_ref, wout_ref, dmat_ref, qdec_ref, kdec_ref, cdec_ref,
                         y_ref, tail_ref, sfin_ref, ubuf, ret_scr, *, seq_len):
    ns = SAMPLE_SEQS
    rows = ns * seq_len
    x = x_ref[...].reshape(rows, D_MODEL)
    hn = _rmsnorm(x, ng_ref[...]).astype(BF16)
    z = _dot(hn, win_ref[...])
    ubuf[:, 0:POOL_HIST, :] = prev_ref[...]
    ubuf[:, POOL_HIST:POOL_HIST + seq_len, :] = z[:, :POOL_WIDTH].reshape(ns, seq_len, POOL_WIDTH)
    pool_out = _pool_mix(ubuf, seq_len, 0, POOL_BUF, wpool_ref, pscale_ref[...])
    tail_ref[...] = ubuf[:, seq_len:seq_len + POOL_HIST, :]

    q_all, k_all, v_all, g_all = _qkvg(z)
    scale = RET_HEAD_DIM ** -0.5
    cos2 = cos_ref[...]
    sin2 = sin_ref[...]
    tok_seq = lax.broadcasted_iota(jnp.int32, (RET_HEAD_DIM, rows), 1) // seq_len
    for h in range(RET_HEADS):
        q = _rope(_head(q_all, h), cos2, sin2)
        k = _rope(_head(k_all, h), cos2, sin2) * scale
        vb = _head(v_all, h).astype(BF16)
        s_old = s0_ref[:, h]
        scores = _dot_nt(q.astype(BF16), k.astype(BF16)) * dmat_ref[h]
        qd = (q * _head(qdec_ref[...], h)).astype(BF16).reshape(ns, seq_len, RET_HEAD_DIM)
        o_state = jnp.einsum('bid,bde->bie', qd, s_old.astype(BF16), preferred_element_type=F32)
        o = _dot(scores.astype(BF16), vb) + o_state.reshape(rows, RET_HEAD_DIM)
        kdt = (k * _head(kdec_ref[...], h)).T
        lhs = jnp.concatenate(
            [jnp.where(tok_seq == b, kdt, 0.0).astype(BF16) for b in range(ns)], axis=0)
        upd = _dot(lhs, vb).reshape(ns, RET_HEAD_DIM, RET_HEAD_DIM)
        sfin_ref[:, h] = s_old * cdec_ref[h] + upd
        on = _group_norm(o) * _head(gn_ref[...], h)
        ret_scr[:, h * RET_HEAD_DIM:(h + 1) * RET_HEAD_DIM] = _silu(_head(g_all, h)) * on

    mix_in = jnp.concatenate([pool_out, ret_scr[...]], axis=-1).astype(BF16)
    y_ref[...] = (x + _dot(mix_in, wout_ref[...])).reshape(ns, seq_len, D_MODEL)


def _route(hn, wr_hi_ref, wr_lo_ref, br_ref):
    h_hi = hn.astype(BF16)
    h_lo = (hn - h_hi.astype(F32)).astype(BF16)
    lg = (_dot(h_hi, wr_hi_ref[...]) + _dot(h_lo, wr_hi_ref[...]) + _dot(h_hi, wr_lo_ref[...])
          + br_ref[...])
    lane = lax.broadcasted_iota(jnp.int32, lg.shape, 1)
    neg = jnp.float32(-jnp.inf)
    big = jnp.int32(ROUTER_LANES)
    gmask = lane < N_EXPERT_GROUPS
    gmax = jnp.max(jnp.where(gmask, lg, neg), axis=-1, keepdims=True)
    g_sel = jnp.min(jnp.where(gmask & (lg == gmax), lane, big), axis=-1, keepdims=True)
    p_sel = 1.0 / jnp.sum(jnp.where(gmask, jnp.exp(lg - gmax), 0.0), axis=-1, keepdims=True)
    e_lo = EXPERT_LANE0 + g_sel * EXPERTS_PER_GROUP
    emask = (lane >= e_lo) & (lane < e_lo + EXPERTS_PER_GROUP)
    v1 = jnp.max(jnp.where(emask, lg, neg), axis=-1, keepdims=True)
    i1 = jnp.min(jnp.where(emask & (lg == v1), lane, big), axis=-1, keepdims=True)
    emask2 = emask & (lane != i1)
    v2 = jnp.max(jnp.where(emask2, lg, neg), axis=-1, keepdims=True)
    i2 = jnp.min(jnp.where(emask2 & (lg == v2), lane, big), axis=-1, keepdims=True)
    e2 = jnp.exp(v2 - v1)
    w1 = p_sel / (1.0 + e2)
    w2 = p_sel * e2 / (1.0 + e2)
    return jnp.where(lane == i1, w1, 0.0) + jnp.where(lane == i2, w2, 0.0)


def _moe_dense_kernel(x_ref, ng_ref, wr_hi_ref, wr_lo_ref, br_ref, wgu_ref, wd_ref, nf_ref,
                      y_ref, h_scr, gate_scr, acc_scr, *, final_norm):
    e = pl.program_id(1)

    @pl.when(e == 0)
    def _():
        hn = _rmsnorm(x_ref[...], ng_ref[...])
        h_scr[...] = hn.astype(BF16)
        gate_scr[...] = _route(hn, wr_hi_ref, wr_lo_ref, br_ref)
        acc_scr[...] = jnp.zeros_like(acc_scr)

    lane = lax.broadcasted_iota(jnp.int32, gate_scr.shape, 1)
    gate = jnp.sum(jnp.where(lane == EXPERT_LANE0 + e, gate_scr[...], 0.0), axis=-1, keepdims=True)
    gu = _dot(h_scr[...], wgu_ref[0])
    act = _silu(gu[:, :D_EXPERT]) * gu[:, D_EXPERT:]
    acc_scr[...] += _dot((act * gate).astype(BF16), wd_ref[0])

    @pl.when(e == N_EXPERTS - 1)
    def _():
        y = x_ref[...] + acc_scr[...]
        if final_norm:
            y = _rmsnorm(y, nf_ref[...])
        y_ref[...] = y


def _const_spec(shape):
    return pl.BlockSpec(shape, lambda *_: (0,) * len(shape))


def _rope_tables(pos):
    half = RET_HEAD_DIM // 2
    inv = ROPE_BASE ** (-jnp.arange(half, dtype=F32) / half)
    ang = pos.astype(F32)[:, None] * inv[None, :]
    cos, sin = jnp.cos(ang), jnp.sin(ang)
    return jnp.concatenate([cos, cos], -1), jnp.concatenate([-sin, sin], -1)


def _decay_consts(c, reps):
    log_g = jnp.log(1.0 - 2.0 ** (-5.0 - jnp.arange(RET_HEADS, dtype=F32)))
    i = jnp.arange(c, dtype=F32)
    diff = i[:, None] - i[None, :]
    dmat = jnp.where(diff[None] >= 0, jnp.exp(jnp.maximum(diff, 0.0)[None] * log_g[:, None, None]), 0.0)
    qdec = jnp.exp((i + 1.0)[None, :] * log_g[:, None])
    kdec = jnp.exp((c - 1.0 - i)[None, :] * log_g[:, None])
    cdec = jnp.exp(c * log_g)
    eye = jnp.eye(reps, dtype=F32)
    dmat = jnp.einsum('ab,hij->haibj', eye, dmat).reshape(RET_HEADS, reps * c, reps * c)
    lanes = lambda t: jnp.repeat(jnp.tile(t, (1, reps)).T, RET_HEAD_DIM, axis=1)
    cdec = jnp.broadcast_to(cdec[:, None, None], (RET_HEADS, 1, RET_HEAD_DIM))
    return dmat, lanes(qdec), lanes(kdec), cdec


def _mixer_weights(norm_g, w_in, w_pool, pool_scale, gn_g, w_out):
    return (norm_g.reshape(1, D_MODEL), w_in.astype(BF16), w_pool.astype(BF16),
            pool_scale.reshape(1, POOL_WIDTH), gn_g.reshape(1, RET_WIDTH), w_out.astype(BF16))


_MIXER_WEIGHT_SPECS = [
    _const_spec((1, D_MODEL)), _const_spec((D_MODEL, IN_WIDTH)),
    _const_spec((len(POOL_WINDOWS), POOL_GROUP_DIM, POOL_GROUP_DIM)),
    _const_spec((1, POOL_WIDTH)), _const_spec((1, RET_WIDTH)), _const_spec((D_MODEL, D_MODEL)),
]
_DECAY_SPECS = [
    _const_spec((RET_HEADS, CHUNK, CHUNK)), _const_spec((CHUNK, RET_WIDTH)),
    _const_spec((CHUNK, RET_WIDTH)), _const_spec((RET_HEADS, 1, RET_HEAD_DIM)),
]


def _mixer_prompt(x, weights, rows):
    b, l, _ = x.shape
    assert l % rows == 0 and rows % CHUNK == 0
    cos2, sin2 = _rope_tables(jnp.arange(l))
    decay = _decay_consts(RET_CHUNK, 1)
    tok = lambda i, c: (i, c, 0)
    seq = lambda i, c: (i, 0, 0)
    y, tail, s_fin = pl.pallas_call(
        functools.partial(_mixer_prompt_kernel, rows=rows),
        grid=(b, l // rows),
        in_specs=[pl.BlockSpec((1, rows, D_MODEL), tok),
                  pl.BlockSpec((rows, RET_HEAD_DIM), lambda i, c: (c, 0)),
                  pl.BlockSpec((rows, RET_HEAD_DIM), lambda i, c: (c, 0)),
                  *_MIXER_WEIGHT_SPECS, *_DECAY_SPECS],
        out_specs=[pl.BlockSpec((1, rows, D_MODEL), tok),
                   pl.BlockSpec((1, POOL_HIST, POOL_WIDTH), seq),
                   pl.BlockSpec((1, RET_HEADS, RET_HEAD_DIM, RET_HEAD_DIM), lambda i, c: (i, 0, 0, 0))],
        out_shape=[jax.ShapeDtypeStruct(x.shape, F32),
                   jax.ShapeDtypeStruct((b, POOL_HIST, POOL_WIDTH), F32),
                   jax.ShapeDtypeStruct((b, RET_HEADS, RET_HEAD_DIM, RET_HEAD_DIM), F32)],
        scratch_shapes=[pltpu.VMEM((1, POOL_HIST + rows, POOL_WIDTH), F32),
                        pltpu.VMEM((RET_HEADS, RET_HEAD_DIM, RET_HEAD_DIM), F32),
                        pltpu.VMEM((rows, RET_WIDTH), F32)],
        compiler_params=pltpu.CompilerParams(
            dimension_semantics=("arbitrary", "arbitrary"), vmem_limit_bytes=VMEM_LIMIT),
        name="mixer_prompt",
    )(x, cos2, sin2, *weights, *decay)
    return y, tail[:, POOL_HIST - POOL_BUF:], s_fin


def _mixer_sample(x, pool_prev, s0, weights):
    b, l, _ = x.shape
    ns = SAMPLE_SEQS
    assert ns * l == CHUNK and b % ns == 0
    cos2, sin2 = _rope_tables(PAST_LEN + jnp.arange(l))
    cos2, sin2 = jnp.tile(cos2, (ns, 1)), jnp.tile(sin2, (ns, 1))
    decay = _decay_consts(l, ns)
    prev = jnp.pad(pool_prev, ((0, 0), (POOL_HIST - POOL_BUF, 0), (0, 0)))
    seq3 = lambda i: (i, 0, 0)
    seq4 = lambda i: (i, 0, 0, 0)
    state_spec = pl.BlockSpec((ns, RET_HEADS, RET_HEAD_DIM, RET_HEAD_DIM), seq4)
    y, tail, s_fin = pl.pallas_call(
        functools.partial(_mixer_sample_kernel, seq_len=l),
        grid=(b // ns,),
        in_specs=[pl.BlockSpec((ns, l, D_MODEL), seq3),
                  pl.BlockSpec((ns, POOL_HIST, POOL_WIDTH), seq3),
                  state_spec,
                  _const_spec((CHUNK, RET_HEAD_DIM)), _const_spec((CHUNK, RET_HEAD_DIM)),
                  *_MIXER_WEIGHT_SPECS, *_DECAY_SPECS],
        out_specs=[pl.BlockSpec((ns, l, D_MODEL), seq3),
                   pl.BlockSpec((ns, POOL_HIST, POOL_WIDTH), seq3),
                   state_spec],
        out_shape=[jax.ShapeDtypeStruct(x.shape, F32),
                   jax.ShapeDtypeStruct((b, POOL_HIST, POOL_WIDTH), F32),
                   jax.ShapeDtypeStruct(s0.shape, F32)],
        scratch_shapes=[pltpu.VMEM((ns, POOL_HIST + l, POOL_WIDTH), F32),
                        pltpu.VMEM((CHUNK, RET_WIDTH), F32)],
        compiler_params=pltpu.CompilerParams(
            dimension_semantics=("arbitrary",), vmem_limit_bytes=VMEM_LIMIT),
        name="mixer_sample",
    )(x, prev, s0, cos2, sin2, *weights, *decay)
    return y, tail[:, POOL_HIST - POOL_BUF:], s_fin


def _moe_weights(norm_g, w_rg, b_rg, w_re, b_re, w_gate, w_up, w_down):
    pad = ROUTER_LANES - N_EXPERT_GROUPS - N_EXPERTS
    wr = jnp.pad(jnp.concatenate([w_rg, w_re], axis=1), ((0, 0), (0, pad)))
    br = jnp.pad(jnp.concatenate([b_rg, b_re]), (0, pad)).reshape(1, ROUTER_LANES)
    wr_hi = wr.astype(BF16)
    wr_lo = (wr - wr_hi.astype(F32)).astype(BF16)
    wgu = jnp.concatenate([w_gate, w_up], axis=-1).astype(BF16)
    return norm_g.reshape(1, D_MODEL), wr_hi, wr_lo, br, wgu, w_down.astype(BF16)


def _moe_dense(x, weights, norm_final, final_norm, rows):
    t = x.shape[0]
    assert t % rows == 0
    tok = lambda i, e: (i, 0)
    return pl.pallas_call(
        functools.partial(_moe_dense_kernel, final_norm=final_norm),
        grid=(t // rows, N_EXPERTS),
        in_specs=[pl.BlockSpec((rows, D_MODEL), tok),
                  _const_spec((1, D_MODEL)),
                  _const_spec((D_MODEL, ROUTER_LANES)), _const_spec((D_MODEL, ROUTER_LANES)),
                  _const_spec((1, ROUTER_LANES)),
                  pl.BlockSpec((1, D_MODEL, 2 * D_EXPERT), lambda i, e: (e, 0, 0)),
                  pl.BlockSpec((1, D_EXPERT, D_MODEL), lambda i, e: (e, 0, 0)),
                  _const_spec((1, D_MODEL))],
        out_specs=pl.BlockSpec((rows, D_MODEL), tok),
        out_shape=jax.ShapeDtypeStruct(x.shape, F32),
        scratch_shapes=[pltpu.VMEM((rows, D_MODEL), BF16),
                        pltpu.VMEM((rows, ROUTER_LANES), F32),
                        pltpu.VMEM((rows, D_MODEL), F32)],
        compiler_params=pltpu.CompilerParams(
            dimension_semantics=("arbitrary", "arbitrary"), vmem_limit_bytes=VMEM_LIMIT),
        name="moe_dense",
    )(x, *weights, norm_final.reshape(1, D_MODEL))


def kernel(x_prompt, x_sample, cache_pool, state_ret, norm_mix, w_in, w_pool, pool_scale, ret_gn, w_out, norm_ffn, w_router_group, b_router_group, w_router_expert, b_router_expert, w_gate, w_up, w_down, norm_final):
    depth = norm_mix.shape[0]
    yp, ys = x_prompt, x_sample
    pool_p, ret_p, pool_s, ret_s = [], [], [], []
    for l in range(depth):
        mix_w = _mixer_weights(norm_mix[l], w_in[l], w_pool[l], pool_scale[l], ret_gn[l], w_out[l])
        yp, bp, sp = _mixer_prompt(yp, mix_w, rows=512)
        ys, bs, ss = _mixer_sample(ys, cache_pool[l], state_ret[l], mix_w)
        moe_w = _moe_weights(norm_ffn[l], w_router_group[l], b_router_group[l], w_router_expert[l],
                             b_router_expert[l], w_gate[l], w_up[l], w_down[l])
        last = l == depth - 1
        yp = _moe_dense(yp.reshape(-1, D_MODEL), moe_w, norm_final, last, rows=1024).reshape(yp.shape)
        ys = _moe_dense(ys.reshape(-1, D_MODEL), moe_w, norm_final, last, rows=1024).reshape(ys.shape)
        pool_p.append(bp); ret_p.append(sp); pool_s.append(bs); ret_s.append(ss)
    return (yp, ys, jnp.stack(pool_p), jnp.stack(ret_p), jnp.stack(pool_s), jnp.stack(ret_s))
```

```python
import dataclasses
import functools

import jax
import jax.numpy as jnp
from jax import lax
from jax.experimental import pallas as pl
from jax.experimental.pallas import tpu as pltpu
from jax.experimental.pallas import tpu_sc as plsc

F32 = jnp.float32
BF16 = jnp.bfloat16

D_MODEL = 1024
POOL_WIDTH = 512
POOL_WINDOWS = (2, 4, 8, 16)
POOL_GROUP_DIM = 128
POOL_BUF = 15
POOL_HIST = 16
RET_WIDTH = 512
RET_HEADS = 4
RET_HEAD_DIM = 128
RET_CHUNK = 128
ROPE_BASE = 10000.0
IN_WIDTH = POOL_WIDTH + 4 * RET_WIDTH
N_EXPERT_GROUPS = 4
EXPERTS_PER_GROUP = 4
N_EXPERTS = 16
D_EXPERT = 256
RMS_EPS = 1e-6
GN_EPS = 1e-5
PAST_LEN = 16384

ROUTER_LANES = 128
EXPERT_LANE0 = N_EXPERT_GROUPS
PAIRS_PER_GROUP = 6
N_CLASSES = N_EXPERT_GROUPS * PAIRS_PER_GROUP
PAIR_LO = (0, 0, 0, 1, 1, 2)
PAIR_HI = (1, 2, 3, 2, 3, 3)
META_W_LO, META_W_HI, META_CLS, META_RANK = 0, 1, 2, 3
ROUTE_TILE = 1024
PAIR_TILE = 256
PACK_W = D_MODEL // 4
SC_CORES, SC_SUBCORES, SC_LANES = 2, 16, 16
SC_WINDOW = 128
PRECISE_TAIL_STEPS = 1
CHUNK = 128
SAMPLE_SEQS = 16
VMEM_LIMIT = 56 * 1024 * 1024


def _dot(a, b):
    return jnp.dot(a, b, preferred_element_type=F32)


def _dot_nt(a, b):
    return lax.dot_general(a, b, (((1,), (1,)), ((), ())), preferred_element_type=F32)


def _split(a):
    hi = a.astype(BF16)
    return hi, (a - hi.astype(F32)).astype(BF16)


def _mm(a, b, precise, nt=False):
    dot = _dot_nt if nt else _dot
    if precise:
        b_hi, b_lo = b if isinstance(b, tuple) else _split(b)
        a_hi, a_lo = _split(a)
        return dot(a_hi, b_hi) + dot(a_lo, b_hi) + dot(a_hi, b_lo)
    return dot(a.astype(BF16), b[0] if isinstance(b, tuple) else b.astype(BF16))


def _rmsnorm(x, g):
    ms = jnp.mean(x * x, axis=-1, keepdims=True)
    return x * lax.rsqrt(ms + RMS_EPS) * g


def _pool_mix(ubuf, rows, t_first, n_prev, wpool_refs, pscale, precise=False):
    ns = ubuf.shape[0]
    t = t_first + lax.broadcasted_iota(jnp.int32, (1, rows, POOL_GROUP_DIM), 1)
    outs = []
    for j, w in enumerate(POOL_WINDOWS):
        lanes = slice(j * POOL_GROUP_DIM, (j + 1) * POOL_GROUP_DIM)
        uj = ubuf[:, POOL_HIST:POOL_HIST + rows, lanes]
        acc = uj
        for i in range(1, w):
            acc = acc + ubuf[:, POOL_HIST - i:POOL_HIST - i + rows, lanes]
        cnt = jnp.minimum(w, n_prev + t + 1).astype(F32)
        d = (acc / cnt - uj).reshape(ns * rows, POOL_GROUP_DIM)
        outs.append(_mm(d, tuple(w[j] for w in wpool_refs), precise))
    return jnp.concatenate(outs, axis=-1) * pscale


def _rope(xh, cos2, sin2):
    return xh * cos2 + pltpu.roll(xh, RET_HEAD_DIM // 2, 1) * sin2


def _group_norm(o):
    mu = jnp.mean(o, axis=-1, keepdims=True)
    c = o - mu
    var = jnp.mean(c * c, axis=-1, keepdims=True)
    return c * lax.rsqrt(var + GN_EPS)


def _silu(x):
    return x * (1.0 / (1.0 + jnp.exp(-x)))


def _head(a, h):
    return a[:, h * RET_HEAD_DIM:(h + 1) * RET_HEAD_DIM]


def _qkvg(z):
    p, r = POOL_WIDTH, RET_WIDTH
    return z[:, p:p + r], z[:, p + r:p + 2 * r], z[:, p + 2 * r:p + 3 * r], z[:, p + 3 * r:p + 4 * r]


def _mixer_prompt_kernel(x_ref, ya_ref, yb_ref, cos_ref, sin_ref, ng_ref, win_hi_ref, win_lo_ref,
                         wpool_hi_ref, wpool_lo_ref, pscale_ref, gn_ref, wout_hi_ref, wout_lo_ref,
                         dmat_ref, qdec_ref, kdec_ref, cdec_ref,
                         y_ref, tail_ref, sfin_ref, ubuf, s_scr, ret_scr, *, rows, moe_in,
                         precise_tail):
    c = pl.program_id(1)

    @pl.when(c == 0)
    def _():
        ubuf[:, 0:POOL_HIST, :] = jnp.zeros((1, POOL_HIST, POOL_WIDTH), F32)
        s_scr[...] = jnp.zeros_like(s_scr)

    def step(precise):
        x = x_ref[0]
        if moe_in:
            x = x + _unpack_row(ya_ref[0], yb_ref[0])
        hn = _rmsnorm(x, ng_ref[...])
        z = _mm(hn, (win_hi_ref[...], win_lo_ref[...]), precise)
        ubuf[0, POOL_HIST:POOL_HIST + rows, :] = z[:, :POOL_WIDTH]
        pool_out = _pool_mix(ubuf, rows, c * rows, 0, (wpool_hi_ref, wpool_lo_ref),
                             pscale_ref[...], precise)
        tail_ref[...] = ubuf[:, rows:rows + POOL_HIST, :]
        ubuf[:, 0:POOL_HIST, :] = ubuf[:, rows:rows + POOL_HIST, :]

        q_all, k_all, v_all, g_all = _qkvg(z)
        scale = RET_HEAD_DIM ** -0.5
        for ci in range(rows // CHUNK):
            rs = slice(ci * CHUNK, (ci + 1) * CHUNK)
            cos2 = cos_ref[rs, :]
            sin2 = sin_ref[rs, :]
            for h in range(RET_HEADS):
                q = _rope(_head(q_all, h)[rs], cos2, sin2)
                k = _rope(_head(k_all, h)[rs], cos2, sin2) * scale
                v = _split(_head(v_all, h)[rs]) if precise else _head(v_all, h)[rs]
                s_old = s_scr[h]
                scores = _mm(q, k, precise, nt=True) * dmat_ref[h]
                qd = q * _head(qdec_ref[...], h)
                o = _mm(scores, v, precise) + _mm(qd, s_old, precise)
                kd = k * _head(kdec_ref[...], h)
                s_scr[h] = s_old * cdec_ref[h] + _mm(kd.T, v, precise)
                on = _group_norm(o) * _head(gn_ref[...], h)
                ret_scr[rs, h * RET_HEAD_DIM:(h + 1) * RET_HEAD_DIM] = _silu(_head(g_all, h)[rs]) * on

        mix_in = jnp.concatenate([pool_out, ret_scr[...]], axis=-1)
        y_ref[0] = x + _mm(mix_in, (wout_hi_ref[...], wout_lo_ref[...]), precise)
        sfin_ref[0] = s_scr[...]

    if precise_tail:
        first_precise = pl.num_programs(1) - precise_tail
        pl.when(c >= first_precise)(lambda: step(True))
        pl.when(c < first_precise)(lambda: step(False))
    else:
        step(False)


def _mixer_sample_kernel(x_ref, prev_ref, s0_ref, cos_ref, sin_ref, ng_ref, win_ref, wpool_ref,
                         pscale_ref, gn_ref, wout_ref, dmat_ref, qdec_ref, kdec_ref, cdec_ref,
                         y_ref, tail_ref, sfin_ref, ubuf, ret_scr, *, seq_len):
    ns = SAMPLE_SEQS
    rows = ns * seq_len
    x = x_ref[...].reshape(rows, D_MODEL)
    hn = _rmsnorm(x, ng_ref[...]).astype(BF16)
    z = _dot(hn, win_ref[...])
    ubuf[:, 0:POOL_HIST, :] = prev_ref[...]
    ubuf[:, POOL_HIST:POOL_HIST + seq_len, :] = z[:, :POOL_WIDTH].reshape(ns, seq_len, POOL_WIDTH)
    pool_out = _pool_mix(ubuf, seq_len, 0, POOL_BUF, (wpool_ref,), pscale_ref[...])
    tail_ref[...] = ubuf[:, seq_len:seq_len + POOL_HIST, :]

    q_all, k_all, v_all, g_all = _qkvg(z)
    scale = RET_HEAD_DIM ** -0.5
    cos2 = cos_ref[...]
    sin2 = sin_ref[...]
    tok_seq = lax.broadcasted_iota(jnp.int32, (RET_HEAD_DIM, rows), 1) // seq_len
    for h in range(RET_HEADS):
        q = _rope(_head(q_all, h), cos2, sin2)
        k = _rope(_head(k_all, h), cos2, sin2) * scale
        vb = _head(v_all, h).astype(BF16)
        s_old = s0_ref[:, h]
        scores = _dot_nt(q.astype(BF16), k.astype(BF16)) * dmat_ref[h]
        qd = (q * _head(qdec_ref[...], h)).astype(BF16).reshape(ns, seq_len, RET_HEAD_DIM)
        o_state = jnp.einsum('bid,bde->bie', qd, s_old.astype(BF16), preferred_element_type=F32)
        o = _dot(scores.astype(BF16), vb) + o_state.reshape(rows, RET_HEAD_DIM)
        kdt = (k * _head(kdec_ref[...], h)).T
        lhs = jnp.concatenate(
            [jnp.where(tok_seq == b, kdt, 0.0).astype(BF16) for b in range(ns)], axis=0)
        upd = _dot(lhs, vb).reshape(ns, RET_HEAD_DIM, RET_HEAD_DIM)
        sfin_ref[:, h] = s_old * cdec_ref[h] + upd
        on = _group_norm(o) * _head(gn_ref[...], h)
        ret_scr[:, h * RET_HEAD_DIM:(h + 1) * RET_HEAD_DIM] = _silu(_head(g_all, h)) * on

    mix_in = jnp.concatenate([pool_out, ret_scr[...]], axis=-1).astype(BF16)
    y_ref[...] = (x + _dot(mix_in, wout_ref[...])).reshape(ns, seq_len, D_MODEL)


def _route_select(hn, wr_hi_ref, wr_lo_ref, br_ref):
    h_hi = hn.astype(BF16)
    h_lo = (hn - h_hi.astype(F32)).astype(BF16)
    lg = (_dot(h_hi, wr_hi_ref[...]) + _dot(h_lo, wr_hi_ref[...]) + _dot(h_hi, wr_lo_ref[...])
          + br_ref[...])
    lane = lax.broadcasted_iota(jnp.int32, lg.shape, 1)
    neg = jnp.float32(-jnp.inf)
    big = jnp.int32(ROUTER_LANES)
    gmask = lane < N_EXPERT_GROUPS
    gmax = jnp.max(jnp.where(gmask, lg, neg), axis=-1, keepdims=True)
    g_sel = jnp.min(jnp.where(gmask & (lg == gmax), lane, big), axis=-1, keepdims=True)
    p_sel = 1.0 / jnp.sum(jnp.where(gmask, jnp.exp(lg - gmax), 0.0), axis=-1, keepdims=True)
    e_lo = EXPERT_LANE0 + g_sel * EXPERTS_PER_GROUP
    emask = (lane >= e_lo) & (lane < e_lo + EXPERTS_PER_GROUP)
    v1 = jnp.max(jnp.where(emask, lg, neg), axis=-1, keepdims=True)
    i1 = jnp.min(jnp.where(emask & (lg == v1), lane, big), axis=-1, keepdims=True)
    emask2 = emask & (lane != i1)
    v2 = jnp.max(jnp.where(emask2, lg, neg), axis=-1, keepdims=True)
    i2 = jnp.min(jnp.where(emask2 & (lg == v2), lane, big), axis=-1, keepdims=True)
    e2 = jnp.exp(v2 - v1)
    w1 = p_sel / (1.0 + e2)
    w2 = p_sel * e2 / (1.0 + e2)
    return g_sel, i1, i2, w1, w2


def _route(hn, wr_hi_ref, wr_lo_ref, br_ref):
    _, i1, i2, w1, w2 = _route_select(hn, wr_hi_ref, wr_lo_ref, br_ref)
    lane = lax.broadcasted_iota(jnp.int32, (hn.shape[0], ROUTER_LANES), 1)
    return jnp.where(lane == i1, w1, 0.0) + jnp.where(lane == i2, w2, 0.0)


def _pack_bf16_pair(a, b):
    ua = pltpu.bitcast(a.astype(BF16).astype(F32), jnp.uint32)
    ub = pltpu.bitcast(b.astype(BF16).astype(F32), jnp.uint32)
    return pltpu.bitcast((ua >> 16) | (ub & jnp.uint32(0xFFFF0000)), jnp.int32)


def _unpack_bf16_pair(w):
    u = pltpu.bitcast(w, jnp.uint32)
    return pltpu.bitcast(u << 16, F32), pltpu.bitcast(u & jnp.uint32(0xFFFF0000), F32)


def _pack_row(y):
    q = D_MODEL // 4
    return _pack_bf16_pair(y[:, 0:q], y[:, q:2 * q]), _pack_bf16_pair(y[:, 2 * q:3 * q], y[:, 3 * q:])


def _unpack_row(wa, wb):
    return jnp.concatenate([*_unpack_bf16_pair(wa), *_unpack_bf16_pair(wb)], axis=-1)


def _moe_route_kernel(x_ref, ng_ref, wr_hi_ref, wr_lo_ref, br_ref, ha_ref, hb_ref, meta_ref,
                      count_ref, tri_scr, carry_scr):
    i = pl.program_id(0)
    rows = x_ref.shape[0]

    @pl.when(i == 0)
    def _():
        r = lax.broadcasted_iota(jnp.int32, (rows, rows), 0)
        c = lax.broadcasted_iota(jnp.int32, (rows, rows), 1)
        tri_scr[...] = jnp.where(c < r, 1.0, 0.0).astype(BF16)
        carry_scr[...] = jnp.zeros_like(carry_scr)

    hn = _rmsnorm(x_ref[...], ng_ref[...])
    ha_ref[...], hb_ref[...] = _pack_row(hn)
    g_sel, i1, i2, w1, w2 = _route_select(hn, wr_hi_ref, wr_lo_ref, br_ref)
    first = EXPERT_LANE0 + g_sel * EXPERTS_PER_GROUP
    lo = jnp.minimum(i1, i2) - first
    hi = jnp.maximum(i1, i2) - first
    cls = g_sel * PAIRS_PER_GROUP + (lo * (7 - lo)) // 2 + (hi - lo - 1)
    w_lo = jnp.where(i1 < i2, w1, w2)
    w_hi = jnp.where(i1 < i2, w2, w1)
    lane = lax.broadcasted_iota(jnp.int32, (rows, ROUTER_LANES), 1)
    onehot = jnp.where(lane == cls, 1.0, 0.0)
    before = _dot(tri_scr[...], onehot.astype(BF16)) + carry_scr[...]
    rank = jnp.sum(onehot * before, axis=-1, keepdims=True)
    carry_scr[...] += jnp.sum(onehot, axis=0, keepdims=True)
    meta_ref[...] = (jnp.where(lane == META_W_LO, w_lo, 0.0) + jnp.where(lane == META_W_HI, w_hi, 0.0)
                     + jnp.where(lane == META_CLS, cls.astype(F32), 0.0)
                     + jnp.where(lane == META_RANK, rank, 0.0))
    count_ref[...] = carry_scr[...]


def _moe_pair_kernel(ea_ref, eb_ref, ha_ref, hb_ref, meta_ref, wgu_a_ref, wgu_b_ref, wd_a_ref,
                     wd_b_ref, ya_ref, yb_ref):
    del ea_ref, eb_ref
    h = _unpack_row(ha_ref[...], hb_ref[...]).astype(BF16)
    meta = meta_ref[...]
    y = None
    for wgu_ref, wd_ref, lane in ((wgu_a_ref, wd_a_ref, META_W_LO), (wgu_b_ref, wd_b_ref, META_W_HI)):
        gu = _dot(h, wgu_ref[0])
        act = _silu(gu[:, :D_EXPERT]) * gu[:, D_EXPERT:]
        part = _dot((act * meta[:, lane:lane + 1]).astype(BF16), wd_ref[0])
        y = part if y is None else y + part
    ya_ref[...], yb_ref[...] = _pack_row(y)


def _final_norm_kernel(x_ref, ya_ref, yb_ref, g_ref, o_ref):
    o_ref[...] = _rmsnorm(x_ref[...] + _unpack_row(ya_ref[...], yb_ref[...]), g_ref[...])


def _moe_dense_kernel(x_ref, ng_ref, wr_hi_ref, wr_lo_ref, br_ref, wgu_ref, wd_ref, nf_ref,
                      y_ref, h_scr, gate_scr, acc_scr, *, final_norm):
    e = pl.program_id(1)

    @pl.when(e == 0)
    def _():
        hn = _rmsnorm(x_ref[...], ng_ref[...])
        h_scr[...] = hn.astype(BF16)
        gate_scr[...] = _route(hn, wr_hi_ref, wr_lo_ref, br_ref)
        acc_scr[...] = jnp.zeros_like(acc_scr)

    lane = lax.broadcasted_iota(jnp.int32, gate_scr.shape, 1)
    gate = jnp.sum(jnp.where(lane == EXPERT_LANE0 + e, gate_scr[...], 0.0), axis=-1, keepdims=True)
    gu = _dot(h_scr[...], wgu_ref[0])
    act = _silu(gu[:, :D_EXPERT]) * gu[:, D_EXPERT:]
    acc_scr[...] += _dot((act * gate).astype(BF16), wd_ref[0])

    @pl.when(e == N_EXPERTS - 1)
    def _():
        y = x_ref[...] + acc_scr[...]
        if final_norm:
            y = _rmsnorm(y, nf_ref[...])
        y_ref[...] = y


def _const_spec(shape):
    return pl.BlockSpec(shape, lambda *_: (0,) * len(shape))


def _rope_tables(pos):
    half = RET_HEAD_DIM // 2
    inv = ROPE_BASE ** (-jnp.arange(half, dtype=F32) / half)
    ang = pos.astype(F32)[:, None] * inv[None, :]
    cos, sin = jnp.cos(ang), jnp.sin(ang)
    return jnp.concatenate([cos, cos], -1), jnp.concatenate([-sin, sin], -1)


def _decay_consts(c, reps):
    log_g = jnp.log(1.0 - 2.0 ** (-5.0 - jnp.arange(RET_HEADS, dtype=F32)))
    i = jnp.arange(c, dtype=F32)
    diff = i[:, None] - i[None, :]
    dmat = jnp.where(diff[None] >= 0, jnp.exp(jnp.maximum(diff, 0.0)[None] * log_g[:, None, None]), 0.0)
    qdec = jnp.exp((i + 1.0)[None, :] * log_g[:, None])
    kdec = jnp.exp((c - 1.0 - i)[None, :] * log_g[:, None])
    cdec = jnp.exp(c * log_g)
    eye = jnp.eye(reps, dtype=F32)
    dmat = jnp.einsum('ab,hij->haibj', eye, dmat).reshape(RET_HEADS, reps * c, reps * c)
    lanes = lambda t: jnp.repeat(jnp.tile(t, (1, reps)).T, RET_HEAD_DIM, axis=1)
    cdec = jnp.broadcast_to(cdec[:, None, None], (RET_HEADS, 1, RET_HEAD_DIM))
    return dmat, lanes(qdec), lanes(kdec), cdec


def _split_weight(w):
    hi = w.astype(BF16)
    return hi, (w - hi.astype(F32)).astype(BF16)


def _mixer_weights(norm_g, w_in, w_pool, pool_scale, gn_g, w_out):
    return (norm_g.reshape(1, D_MODEL), *_split_weight(w_in), *_split_weight(w_pool),
            pool_scale.reshape(1, POOL_WIDTH), gn_g.reshape(1, RET_WIDTH), *_split_weight(w_out))


def _hi_only(weights):
    norm_g, win_hi, _, wpool_hi, _, pscale, gn_g, wout_hi, _ = weights
    return norm_g, win_hi, wpool_hi, pscale, gn_g, wout_hi


def _resident_spec(shape):
    return pl.BlockSpec(shape, lambda *_: (0,) * len(shape), pipeline_mode=pl.Buffered(1))


_W_IN_SPEC = _resident_spec((D_MODEL, IN_WIDTH))
_W_POOL_SPEC = _const_spec((len(POOL_WINDOWS), POOL_GROUP_DIM, POOL_GROUP_DIM))
_W_OUT_SPEC = _resident_spec((D_MODEL, D_MODEL))
_MIXER_WEIGHT_SPECS = [
    _const_spec((1, D_MODEL)), _W_IN_SPEC, _W_POOL_SPEC,
    _const_spec((1, POOL_WIDTH)), _const_spec((1, RET_WIDTH)), _W_OUT_SPEC,
]
_MIXER_SPLIT_WEIGHT_SPECS = [
    _const_spec((1, D_MODEL)), _W_IN_SPEC, _W_IN_SPEC, _W_POOL_SPEC, _W_POOL_SPEC,
    _const_spec((1, POOL_WIDTH)), _const_spec((1, RET_WIDTH)), _W_OUT_SPEC, _W_OUT_SPEC,
]
_DECAY_SPECS = [
    _const_spec((RET_HEADS, CHUNK, CHUNK)), _const_spec((CHUNK, RET_WIDTH)),
    _const_spec((CHUNK, RET_WIDTH)), _const_spec((RET_HEADS, 1, RET_HEAD_DIM)),
]


def _mixer_prompt(x, moe_y, weights, rows, precise_tail):
    b, l, _ = x.shape
    assert l % rows == 0 and rows % CHUNK == 0
    cos2, sin2 = _rope_tables(jnp.arange(l))
    decay = _decay_consts(RET_CHUNK, 1)
    tok = lambda i, c: (i, c, 0)
    seq = lambda i, c: (i, 0, 0)
    if moe_y is None:
        ya = yb = jnp.zeros((1, rows, PACK_W), jnp.int32)
        y_spec = _const_spec((1, rows, PACK_W))
    else:
        ya, yb = (t.reshape(b, l, PACK_W) for t in moe_y)
        y_spec = pl.BlockSpec((1, rows, PACK_W), tok)
    y, tail, s_fin = pl.pallas_call(
        functools.partial(_mixer_prompt_kernel, rows=rows, moe_in=moe_y is not None,
                          precise_tail=precise_tail),
        grid=(b, l // rows),
        in_specs=[pl.BlockSpec((1, rows, D_MODEL), tok), y_spec, y_spec,
                  pl.BlockSpec((rows, RET_HEAD_DIM), lambda i, c: (c, 0)),
                  pl.BlockSpec((rows, RET_HEAD_DIM), lambda i, c: (c, 0)),
                  *_MIXER_SPLIT_WEIGHT_SPECS, *_DECAY_SPECS],
        out_specs=[pl.BlockSpec((1, rows, D_MODEL), tok),
                   pl.BlockSpec((1, POOL_HIST, POOL_WIDTH), seq),
                   pl.BlockSpec((1, RET_HEADS, RET_HEAD_DIM, RET_HEAD_DIM), lambda i, c: (i, 0, 0, 0))],
        out_shape=[jax.ShapeDtypeStruct(x.shape, F32),
                   jax.ShapeDtypeStruct((b, POOL_HIST, POOL_WIDTH), F32),
                   jax.ShapeDtypeStruct((b, RET_HEADS, RET_HEAD_DIM, RET_HEAD_DIM), F32)],
        scratch_shapes=[pltpu.VMEM((1, POOL_HIST + rows, POOL_WIDTH), F32),
                        pltpu.VMEM((RET_HEADS, RET_HEAD_DIM, RET_HEAD_DIM), F32),
                        pltpu.VMEM((rows, RET_WIDTH), F32)],
        compiler_params=pltpu.CompilerParams(
            dimension_semantics=("arbitrary", "arbitrary"), vmem_limit_bytes=VMEM_LIMIT),
        name="mixer_prompt",
    )(x, ya, yb, cos2, sin2, *weights, *decay)
    return y, tail[:, POOL_HIST - POOL_BUF:], s_fin


def _mixer_sample(x, pool_prev, s0, weights):
    b, l, _ = x.shape
    ns = SAMPLE_SEQS
    assert ns * l == CHUNK and b % ns == 0
    cos2, sin2 = _rope_tables(PAST_LEN + jnp.arange(l))
    cos2, sin2 = jnp.tile(cos2, (ns, 1)), jnp.tile(sin2, (ns, 1))
    decay = _decay_consts(l, ns)
    prev = jnp.pad(pool_prev, ((0, 0), (POOL_HIST - POOL_BUF, 0), (0, 0)))
    seq3 = lambda i: (i, 0, 0)
    seq4 = lambda i: (i, 0, 0, 0)
    state_spec = pl.BlockSpec((ns, RET_HEADS, RET_HEAD_DIM, RET_HEAD_DIM), seq4)
    y, tail, s_fin = pl.pallas_call(
        functools.partial(_mixer_sample_kernel, seq_len=l),
        grid=(b // ns,),
        in_specs=[pl.BlockSpec((ns, l, D_MODEL), seq3),
                  pl.BlockSpec((ns, POOL_HIST, POOL_WIDTH), seq3),
                  state_spec,
                  _const_spec((CHUNK, RET_HEAD_DIM)), _const_spec((CHUNK, RET_HEAD_DIM)),
                  *_MIXER_WEIGHT_SPECS, *_DECAY_SPECS],
        out_specs=[pl.BlockSpec((ns, l, D_MODEL), seq3),
                   pl.BlockSpec((ns, POOL_HIST, POOL_WIDTH), seq3),
                   state_spec],
        out_shape=[jax.ShapeDtypeStruct(x.shape, F32),
                   jax.ShapeDtypeStruct((b, POOL_HIST, POOL_WIDTH), F32),
                   jax.ShapeDtypeStruct(s0.shape, F32)],
        scratch_shapes=[pltpu.VMEM((ns, POOL_HIST + l, POOL_WIDTH), F32),
                        pltpu.VMEM((CHUNK, RET_WIDTH), F32)],
        compiler_params=pltpu.CompilerParams(
            dimension_semantics=("arbitrary",), vmem_limit_bytes=VMEM_LIMIT),
        name="mixer_sample",
    )(x, prev, s0, cos2, sin2, *_hi_only(weights), *decay)
    return y, tail[:, POOL_HIST - POOL_BUF:], s_fin


def _moe_weights(norm_g, w_rg, b_rg, w_re, b_re, w_gate, w_up, w_down):
    pad = ROUTER_LANES - N_EXPERT_GROUPS - N_EXPERTS
    wr = jnp.pad(jnp.concatenate([w_rg, w_re], axis=1), ((0, 0), (0, pad)))
    br = jnp.pad(jnp.concatenate([b_rg, b_re]), (0, pad)).reshape(1, ROUTER_LANES)
    wr_hi = wr.astype(BF16)
    wr_lo = (wr - wr_hi.astype(F32)).astype(BF16)
    wgu = jnp.concatenate([w_gate, w_up], axis=-1).astype(BF16)
    return norm_g.reshape(1, D_MODEL), wr_hi, wr_lo, br, wgu, w_down.astype(BF16)


def _moe_dense(x, weights, norm_final, final_norm, rows):
    t = x.shape[0]
    assert t % rows == 0
    tok = lambda i, e: (i, 0)
    return pl.pallas_call(
        functools.partial(_moe_dense_kernel, final_norm=final_norm),
        grid=(t // rows, N_EXPERTS),
        in_specs=[pl.BlockSpec((rows, D_MODEL), tok),
                  _const_spec((1, D_MODEL)),
                  _const_spec((D_MODEL, ROUTER_LANES)), _const_spec((D_MODEL, ROUTER_LANES)),
                  _const_spec((1, ROUTER_LANES)),
                  pl.BlockSpec((1, D_MODEL, 2 * D_EXPERT), lambda i, e: (e, 0, 0)),
                  pl.BlockSpec((1, D_EXPERT, D_MODEL), lambda i, e: (e, 0, 0)),
                  _const_spec((1, D_MODEL))],
        out_specs=pl.BlockSpec((rows, D_MODEL), tok),
        out_shape=jax.ShapeDtypeStruct(x.shape, F32),
        scratch_shapes=[pltpu.VMEM((rows, D_MODEL), BF16),
                        pltpu.VMEM((rows, ROUTER_LANES), F32),
                        pltpu.VMEM((rows, D_MODEL), F32)],
        compiler_params=pltpu.CompilerParams(
            dimension_semantics=("arbitrary", "arbitrary"), vmem_limit_bytes=VMEM_LIMIT),
        name="moe_dense",
    )(x, *weights, norm_final.reshape(1, D_MODEL))


def _sc_mesh():
    return plsc.VectorSubcoreMesh(core_axis_name="core", subcore_axis_name="subcore",
                                  num_cores=SC_CORES, num_subcores=SC_SUBCORES)


def _sc_gather(tables, idx):
    n = idx.shape[0]
    assert n % SC_WINDOW == 0
    nt = len(tables)

    def body(*refs):
        i_hbm = refs[nt]
        for t_hbm, o_hbm in zip(refs[:nt], refs[nt + 1:]):
            def gather_window(i_vmem, o_vmem, t_hbm=t_hbm):
                pltpu.sync_copy(t_hbm.at[i_vmem.at[0]], o_vmem)

            pltpu.emit_pipeline(
                gather_window, grid=(n // SC_WINDOW,),
                in_specs=[pl.BlockSpec((1, SC_WINDOW), lambda i: (0, i))],
                out_specs=[pl.BlockSpec((SC_WINDOW, t_hbm.shape[1]), lambda i: (i, 0))],
                core_axis_name=("core", "subcore"),
                dimension_semantics=(pltpu.PARALLEL,),
            )(i_hbm, o_hbm)

    out_type = tuple(jax.ShapeDtypeStruct((n, t.shape[1]), t.dtype) for t in tables)
    return pl.kernel(body, out_type=out_type, mesh=_sc_mesh(), name="sc_gather")(
        *tables, idx.reshape(1, n))


def _sc_invert(pos, n_slots):
    t = pos.shape[0]
    workers = SC_CORES * SC_SUBCORES
    per = n_slots // workers
    assert n_slots % (workers * SC_LANES) == 0 and t % SC_LANES == 0 and t & (t - 1) == 0
    params = pltpu.CompilerParams()
    if "needs_layout_passes" in pltpu.CompilerParams.__dataclass_fields__:
        params = dataclasses.replace(params, needs_layout_passes=False)

    def body(pos_hbm, out_hbm, pos_v, tok_v):
        wid = lax.axis_index("subcore") * SC_CORES + lax.axis_index("core")
        pltpu.sync_copy(pos_hbm, pos_v)

        @pl.loop(0, n_slots, step=SC_LANES)
        def _(i):
            tok_v[pl.ds(i, SC_LANES)] = (lax.iota(jnp.int32, SC_LANES) + i) & (t - 1)

        @pl.loop(0, t, step=SC_LANES)
        def _(i):
            plsc.store_scatter(tok_v, [pos_v[pl.ds(i, SC_LANES)]], lax.iota(jnp.int32, SC_LANES) + i)

        off = pl.multiple_of(wid * per, SC_LANES)
        pltpu.sync_copy(tok_v.at[pl.ds(off, per)], out_hbm.at[pl.ds(off, per)])

    return pl.kernel(body, out_type=jax.ShapeDtypeStruct((n_slots,), jnp.int32), mesh=_sc_mesh(),
                     scratch_types=[pltpu.VMEM((t,), jnp.int32), pltpu.VMEM((n_slots,), jnp.int32)],
                     compiler_params=params, name="sc_invert")(pos)


def _moe_sparse(x, weights):
    norm_g, wr_hi, wr_lo, br, wgu, wd = weights
    t = x.shape[0]
    n_slots = t + N_CLASSES * PAIR_TILE
    n_tiles = n_slots // PAIR_TILE
    tok = lambda i: (i, 0)
    ha, hb, meta, counts = pl.pallas_call(
        _moe_route_kernel,
        grid=(t // ROUTE_TILE,),
        in_specs=[pl.BlockSpec((ROUTE_TILE, D_MODEL), tok), _const_spec((1, D_MODEL)),
                  _const_spec((D_MODEL, ROUTER_LANES)), _const_spec((D_MODEL, ROUTER_LANES)),
                  _const_spec((1, ROUTER_LANES))],
        out_specs=[pl.BlockSpec((ROUTE_TILE, PACK_W), tok), pl.BlockSpec((ROUTE_TILE, PACK_W), tok),
                   pl.BlockSpec((ROUTE_TILE, ROUTER_LANES), tok), _const_spec((1, ROUTER_LANES))],
        out_shape=[jax.ShapeDtypeStruct((t, PACK_W), jnp.int32),
                   jax.ShapeDtypeStruct((t, PACK_W), jnp.int32),
                   jax.ShapeDtypeStruct((t, ROUTER_LANES), F32),
                   jax.ShapeDtypeStruct((1, ROUTER_LANES), F32)],
        scratch_shapes=[pltpu.VMEM((ROUTE_TILE, ROUTE_TILE), BF16),
                        pltpu.VMEM((1, ROUTER_LANES), F32)],
        compiler_params=pltpu.CompilerParams(
            dimension_semantics=("arbitrary",), vmem_limit_bytes=VMEM_LIMIT),
        name="moe_route",
    )(x, norm_g, wr_hi, wr_lo, br)

    cnt = counts[0, :N_CLASSES].astype(jnp.int32)
    padded = (cnt + PAIR_TILE - 1) // PAIR_TILE * PAIR_TILE
    ends = jnp.cumsum(padded)
    starts = ends - padded
    cls = meta[:, META_CLS].astype(jnp.int32)
    pos = starts[cls] + meta[:, META_RANK].astype(jnp.int32)
    tile_cls = jnp.minimum(
        jnp.sum(ends[None, :] <= (jnp.arange(n_tiles) * PAIR_TILE)[:, None], axis=1), N_CLASSES - 1)
    first = (tile_cls // PAIRS_PER_GROUP) * EXPERTS_PER_GROUP
    tile_ea = (first + jnp.asarray(PAIR_LO, jnp.int32)[tile_cls % PAIRS_PER_GROUP]).astype(jnp.int32)
    tile_eb = (first + jnp.asarray(PAIR_HI, jnp.int32)[tile_cls % PAIRS_PER_GROUP]).astype(jnp.int32)

    slot_tok = _sc_invert(pos, n_slots)
    hsa, hsb, metas = _sc_gather((ha, hb, meta), slot_tok)

    row = lambda i, ea, eb: (i, 0)
    ysa, ysb = pl.pallas_call(
        _moe_pair_kernel,
        grid_spec=pltpu.PrefetchScalarGridSpec(
            num_scalar_prefetch=2, grid=(n_tiles,),
            in_specs=[pl.BlockSpec((PAIR_TILE, PACK_W), row), pl.BlockSpec((PAIR_TILE, PACK_W), row),
                      pl.BlockSpec((PAIR_TILE, ROUTER_LANES), row),
                      pl.BlockSpec((1, D_MODEL, 2 * D_EXPERT), lambda i, ea, eb: (ea[i], 0, 0)),
                      pl.BlockSpec((1, D_MODEL, 2 * D_EXPERT), lambda i, ea, eb: (eb[i], 0, 0)),
                      pl.BlockSpec((1, D_EXPERT, D_MODEL), lambda i, ea, eb: (ea[i], 0, 0)),
                      pl.BlockSpec((1, D_EXPERT, D_MODEL), lambda i, ea, eb: (eb[i], 0, 0))],
            out_specs=[pl.BlockSpec((PAIR_TILE, PACK_W), row), pl.BlockSpec((PAIR_TILE, PACK_W), row)]),
        out_shape=[jax.ShapeDtypeStruct((n_slots, PACK_W), jnp.int32),
                   jax.ShapeDtypeStruct((n_slots, PACK_W), jnp.int32)],
        compiler_params=pltpu.CompilerParams(
            dimension_semantics=("arbitrary",), vmem_limit_bytes=VMEM_LIMIT),
        name="moe_pair",
    )(tile_ea, tile_eb, hsa, hsb, metas, wgu, wgu, wd, wd)
    return _sc_gather((ysa, ysb), pos)


def _final_norm(x, moe_y, g, rows):
    t = x.shape[0]
    tok = lambda i: (i, 0)
    return pl.pallas_call(
        _final_norm_kernel,
        grid=(t // rows,),
        in_specs=[pl.BlockSpec((rows, D_MODEL), tok), pl.BlockSpec((rows, PACK_W), tok),
                  pl.BlockSpec((rows, PACK_W), tok), _const_spec((1, D_MODEL))],
        out_specs=pl.BlockSpec((rows, D_MODEL), tok),
        out_shape=jax.ShapeDtypeStruct(x.shape, F32),
        compiler_params=pltpu.CompilerParams(
            dimension_semantics=("arbitrary",), vmem_limit_bytes=VMEM_LIMIT),
        name="final_norm",
    )(x, *moe_y, g.reshape(1, D_MODEL))


def kernel(x_prompt, x_sample, cache_pool, state_ret, norm_mix, w_in, w_pool, pool_scale, ret_gn, w_out, norm_ffn, w_router_group, b_router_group, w_router_expert, b_router_expert, w_gate, w_up, w_down, norm_final):
    depth = norm_mix.shape[0]
    yp, ys = x_prompt, x_sample
    moe_p = None
    pool_p, ret_p, pool_s, ret_s = [], [], [], []
    for l in range(depth):
        mix_w = _mixer_weights(norm_mix[l], w_in[l], w_pool[l], pool_scale[l], ret_gn[l], w_out[l])
        yp, bp, sp = _mixer_prompt(yp, moe_p, mix_w, rows=512,
                                   precise_tail=PRECISE_TAIL_STEPS if l < depth - 1 else 0)
        ys, bs, ss = _mixer_sample(ys, cache_pool[l], state_ret[l], mix_w)
        moe_w = _moe_weights(norm_ffn[l], w_router_group[l], b_router_group[l], w_router_expert[l],
                             b_router_expert[l], w_gate[l], w_up[l], w_down[l])
        last = l == depth - 1
        moe_p = _moe_sparse(yp.reshape(-1, D_MODEL), moe_w)
        ys = _moe_dense(ys.reshape(-1, D_MODEL), moe_w, norm_final, last, rows=1024).reshape(ys.shape)
        pool_p.append(bp); ret_p.append(sp); pool_s.append(bs); ret_s.append(ss)
    yp = _final_norm(yp.reshape(-1, D_MODEL), moe_p, norm_final, rows=1024).reshape(yp.shape)
    return (yp, ys, jnp.stack(pool_p), jnp.stack(ret_p), jnp.stack(pool_s), jnp.stack(ret_s))
```

```python
import dataclasses
import functools

import jax
import jax.numpy as jnp
import numpy as np
from jax import lax
from jax.experimental import pallas as pl
from jax.experimental.pallas import tpu as pltpu
from jax.experimental.pallas import tpu_sc as plsc

F32 = jnp.float32
BF16 = jnp.bfloat16
I32 = jnp.int32

D_MODEL = 1024
POOL_WIDTH = 512
POOL_WINDOWS = (2, 4, 8, 16)
POOL_GROUP_DIM = 128
POOL_BUF = 15
POOL_HIST = 16
RET_WIDTH = 512
RET_HEADS = 4
RET_HEAD_DIM = 128
RET_CHUNK = 128
ROPE_BASE = 10000.0
IN_WIDTH = POOL_WIDTH + 4 * RET_WIDTH
N_EXPERT_GROUPS = 4
EXPERTS_PER_GROUP = 4
N_EXPERTS = 16
D_EXPERT = 256
RMS_EPS = 1e-6
GN_EPS = 1e-5
PAST_LEN = 16384

LANES = 128
SUBLANES = 8
EXPERT_LANE0 = 8
PAIRS_PER_GROUP = 6
N_CLASSES = N_EXPERT_GROUPS * PAIRS_PER_GROUP
CLASS_ROWS = 32
PAIR_LO = (0, 0, 0, 1, 1, 2)
PAIR_HI = (1, 2, 3, 2, 3, 3)
GATE_LO, GATE_HI = 0, 1
ROUTE_TILE = 1024
COUNT_BLOCK = 256
PAIR_TILE = 256
PACK_W = D_MODEL // 4
SC_CORES, SC_SUBCORES, SC_LANES = 2, 16, 16
SC_WINDOW = 128
PRECISE_TAIL_STEPS = 1
CHUNK = 128
SAMPLE_SEQS = 16
VMEM_LIMIT = 56 * 1024 * 1024


def _dot(a, b):
    return jnp.dot(a, b, preferred_element_type=F32)


def _dot_nt(a, b):
    return lax.dot_general(a, b, (((1,), (1,)), ((), ())), preferred_element_type=F32)


def _bf16_round_bits(u):
    return (u + jnp.uint32(0x7FFF) + ((u >> 16) & jnp.uint32(1))) & jnp.uint32(0xFFFF0000)


def _split(a):
    hi = pltpu.bitcast(_bf16_round_bits(pltpu.bitcast(a, jnp.uint32)), F32)
    return hi.astype(BF16), (a - hi).astype(BF16)


def _mm(a, b, precise, nt=False):
    dot = _dot_nt if nt else _dot
    if precise:
        b_hi, b_lo = b if isinstance(b, tuple) else _split(b)
        a_hi, a_lo = _split(a)
        return dot(a_hi, b_hi) + dot(a_lo, b_hi) + dot(a_hi, b_lo)
    return dot(a.astype(BF16), b[0] if isinstance(b, tuple) else b.astype(BF16))


def _rmsnorm(x, g):
    ms = jnp.mean(x * x, axis=-1, keepdims=True)
    return x * lax.rsqrt(ms + RMS_EPS) * g


def _pool_mix(ubuf, rows, t_first, n_prev, wpool_refs, pscale, precise=False):
    ns = ubuf.shape[0]
    t = t_first + lax.broadcasted_iota(I32, (1, rows, POOL_GROUP_DIM), 1)
    outs = []
    for j, w in enumerate(POOL_WINDOWS):
        lanes = slice(j * POOL_GROUP_DIM, (j + 1) * POOL_GROUP_DIM)
        uj = ubuf[:, POOL_HIST:POOL_HIST + rows, lanes]
        acc = uj
        for i in range(1, w):
            acc = acc + ubuf[:, POOL_HIST - i:POOL_HIST - i + rows, lanes]
        cnt = jnp.minimum(w, n_prev + t + 1).astype(F32)
        d = (acc / cnt - uj).reshape(ns * rows, POOL_GROUP_DIM)
        outs.append(_mm(d, tuple(w[j] for w in wpool_refs), precise))
    return jnp.concatenate(outs, axis=-1) * pscale


def _rope(xh, cos2, sin2):
    return xh * cos2 + pltpu.roll(xh, RET_HEAD_DIM // 2, 1) * sin2


def _group_norm(o):
    mu = jnp.mean(o, axis=-1, keepdims=True)
    c = o - mu
    var = jnp.mean(c * c, axis=-1, keepdims=True)
    return c * lax.rsqrt(var + GN_EPS)


def _silu(x):
    return x * (1.0 / (1.0 + jnp.exp(-x)))


def _head(a, h):
    return a[:, h * RET_HEAD_DIM:(h + 1) * RET_HEAD_DIM]


def _qkvg(z):
    p, r = POOL_WIDTH, RET_WIDTH
    return z[:, p:p + r], z[:, p + r:p + 2 * r], z[:, p + 2 * r:p + 3 * r], z[:, p + 3 * r:p + 4 * r]


def _pack_bf16_pair(a, b):
    ua = pltpu.bitcast(a.astype(BF16).astype(F32), jnp.uint32)
    ub = pltpu.bitcast(b.astype(BF16).astype(F32), jnp.uint32)
    return pltpu.bitcast((ua >> 16) | (ub & jnp.uint32(0xFFFF0000)), I32)


def _unpack_bf16_pair(w):
    u = pltpu.bitcast(w, jnp.uint32)
    return pltpu.bitcast(u << 16, F32), pltpu.bitcast(u & jnp.uint32(0xFFFF0000), F32)


def _pack_row(y):
    q = PACK_W
    return _pack_bf16_pair(y[:, 0:q], y[:, q:2 * q]), _pack_bf16_pair(y[:, 2 * q:3 * q], y[:, 3 * q:])


def _unpack_row(wa, wb):
    return jnp.concatenate([*_unpack_bf16_pair(wa), *_unpack_bf16_pair(wb)], axis=-1)


def _zero_other_layers(ref, layer):
    for j in range(ref.shape[0]):
        if j != layer:
            ref[j] = jnp.zeros(ref.shape[1:], ref.dtype)


def _mixer_prompt_kernel(*refs, rows, moe_in, precise_tail, layer, has_prev):
    (x_ref, ya_ref, yb_ref, cos_ref, sin_ref, ng_ref, win_hi_ref, win_lo_ref, wpool_hi_ref,
     wpool_lo_ref, pscale_ref, gn_ref, wout_hi_ref, wout_lo_ref, dmat_ref, qdec_ref, kdec_ref,
     cdec_ref) = refs[:18]
    y_ref, tail_ref, sfin_ref, ubuf, s_scr, ret_scr = refs[18 + 2 * has_prev:]
    c = pl.program_id(1)
    if not has_prev:
        _zero_other_layers(tail_ref, layer)
        _zero_other_layers(sfin_ref, layer)
        tail_ref, sfin_ref = tail_ref.at[layer], sfin_ref.at[layer]

    @pl.when(c == 0)
    def _():
        ubuf[:, 0:POOL_HIST, :] = jnp.zeros((1, POOL_HIST, POOL_WIDTH), F32)
        s_scr[...] = jnp.zeros_like(s_scr)

    def step(precise):
        x = x_ref[0]
        if moe_in:
            x = x + _unpack_row(ya_ref[0], yb_ref[0])
        hn = _rmsnorm(x, ng_ref[...])
        z = _mm(hn, (win_hi_ref[...], win_lo_ref[...]), precise)
        ubuf[0, POOL_HIST:POOL_HIST + rows, :] = z[:, :POOL_WIDTH]
        pool_out = _pool_mix(ubuf, rows, c * rows, 0, (wpool_hi_ref, wpool_lo_ref),
                             pscale_ref[...], precise)
        tail_ref[...] = ubuf[:, rows:rows + POOL_HIST, :]
        ubuf[:, 0:POOL_HIST, :] = ubuf[:, rows:rows + POOL_HIST, :]

        q_all, k_all, v_all, g_all = _qkvg(z)
        scale = RET_HEAD_DIM ** -0.5
        for ci in range(rows // CHUNK):
            rs = slice(ci * CHUNK, (ci + 1) * CHUNK)
            cos2 = cos_ref[rs, :]
            sin2 = sin_ref[rs, :]
            for h in range(RET_HEADS):
                q = _rope(_head(q_all, h)[rs], cos2, sin2)
                k = _rope(_head(k_all, h)[rs], cos2, sin2) * scale
                v = _split(_head(v_all, h)[rs]) if precise else _head(v_all, h)[rs]
                s_old = s_scr[h]
                scores = _mm(q, k, precise, nt=True) * dmat_ref[h]
                qd = q * _head(qdec_ref[...], h)
                o = _mm(scores, v, precise) + _mm(qd, s_old, precise)
                kd = k * _head(kdec_ref[...], h)
                s_scr[h] = s_old * cdec_ref[h] + _mm(kd.T, v, precise)
                on = _group_norm(o) * _head(gn_ref[...], h)
                ret_scr[rs, h * RET_HEAD_DIM:(h + 1) * RET_HEAD_DIM] = _silu(_head(g_all, h)[rs]) * on

        mix_in = jnp.concatenate([pool_out, ret_scr[...]], axis=-1)
        y_ref[0] = x + _mm(mix_in, (wout_hi_ref[...], wout_lo_ref[...]), precise)
        sfin_ref[0] = s_scr[...]

    if precise_tail:
        first_precise = pl.num_programs(1) - precise_tail
        pl.when(c >= first_precise)(lambda: step(True))
        pl.when(c < first_precise)(lambda: step(False))
    else:
        step(False)


def _mixer_sample_kernel(*refs, seq_len, layer, has_prev):
    (x_ref, prev_ref, s0_ref, cos_ref, sin_ref, ng_ref, win_ref, wpool_ref, pscale_ref, gn_ref,
     wout_ref, dmat_ref, qdec_ref, kdec_ref, cdec_ref) = refs[:15]
    y_ref, tail_ref, sfin_ref, ubuf, ret_scr = refs[15 + 2 * has_prev:]
    if not has_prev:
        _zero_other_layers(tail_ref, layer)
        _zero_other_layers(sfin_ref, layer)
        tail_ref, sfin_ref = tail_ref.at[layer], sfin_ref.at[layer]
    ns = SAMPLE_SEQS
    rows = ns * seq_len
    x = x_ref[...].reshape(rows, D_MODEL)
    hn = _rmsnorm(x, ng_ref[...]).astype(BF16)
    z = _dot(hn, win_ref[...])
    ubuf[:, 0:POOL_HIST, :] = prev_ref[...]
    ubuf[:, POOL_HIST:POOL_HIST + seq_len, :] = z[:, :POOL_WIDTH].reshape(ns, seq_len, POOL_WIDTH)
    pool_out = _pool_mix(ubuf, seq_len, 0, POOL_BUF, (wpool_ref,), pscale_ref[...])
    tail_ref[...] = ubuf[:, seq_len:seq_len + POOL_HIST, :]

    q_all, k_all, v_all, g_all = _qkvg(z)
    scale = RET_HEAD_DIM ** -0.5
    cos2 = cos_ref[...]
    sin2 = sin_ref[...]
    tok_seq = lax.broadcasted_iota(I32, (RET_HEAD_DIM, rows), 1) // seq_len
    for h in range(RET_HEADS):
        q = _rope(_head(q_all, h), cos2, sin2)
        k = _rope(_head(k_all, h), cos2, sin2) * scale
        vb = _head(v_all, h).astype(BF16)
        s_old = s0_ref[:, h]
        scores = _dot_nt(q.astype(BF16), k.astype(BF16)) * dmat_ref[h]
        qd = (q * _head(qdec_ref[...], h)).astype(BF16).reshape(ns, seq_len, RET_HEAD_DIM)
        o_state = jnp.einsum('bid,bde->bie', qd, s_old.astype(BF16), preferred_element_type=F32)
        o = _dot(scores.astype(BF16), vb) + o_state.reshape(rows, RET_HEAD_DIM)
        kdt = (k * _head(kdec_ref[...], h)).T
        lhs = jnp.concatenate(
            [jnp.where(tok_seq == b, kdt, 0.0).astype(BF16) for b in range(ns)], axis=0)
        upd = _dot(lhs, vb).reshape(ns, RET_HEAD_DIM, RET_HEAD_DIM)
        sfin_ref[:, h] = s_old * cdec_ref[h] + upd
        on = _group_norm(o) * _head(gn_ref[...], h)
        ret_scr[:, h * RET_HEAD_DIM:(h + 1) * RET_HEAD_DIM] = _silu(_head(g_all, h)) * on

    mix_in = jnp.concatenate([pool_out, ret_scr[...]], axis=-1).astype(BF16)
    y_ref[...] = (x + _dot(mix_in, wout_ref[...])).reshape(ns, seq_len, D_MODEL)


def _router_logits(hn, wr_cat_ref, wr_hi_ref, br_ref):
    h_hi, h_lo = _split(hn)
    part = _dot(h_hi, wr_cat_ref[...])
    return part[:, :LANES] + part[:, LANES:] + _dot(h_lo, wr_hi_ref[...]) + br_ref[...]


def _select(group_lg, expert_lg, gidx, eidx, axis):
    red = dict(axis=axis, keepdims=True)
    neg = jnp.float32(-jnp.inf)
    gmax = jnp.max(group_lg, **red)
    g_sel = jnp.min(jnp.where(group_lg == gmax, gidx, N_EXPERT_GROUPS), **red)
    p_sel = 1.0 / jnp.sum(jnp.exp(group_lg - gmax), **red)
    emask = (eidx >> 2) == g_sel
    v1 = jnp.max(jnp.where(emask, expert_lg, neg), **red)
    i1 = jnp.min(jnp.where(emask & (expert_lg == v1), eidx, N_EXPERTS), **red)
    emask2 = emask & (eidx != i1)
    v2 = jnp.max(jnp.where(emask2, expert_lg, neg), **red)
    i2 = jnp.min(jnp.where(emask2 & (expert_lg == v2), eidx, N_EXPERTS), **red)
    e2 = jnp.exp(v2 - v1)
    return g_sel, i1, i2, p_sel / (1.0 + e2), p_sel * e2 / (1.0 + e2)


def _moe_route_kernel(x_ref, ng_ref, wr_cat_ref, wr_hi_ref, br_ref, ha_ref, hb_ref, gate_ref,
                      idx_ref, count_ref, carry_scr):
    i = pl.program_id(0)
    rows = x_ref.shape[0]

    @pl.when(i == 0)
    def _():
        carry_scr[...] = jnp.zeros_like(carry_scr)

    hn = _rmsnorm(x_ref[...], ng_ref[...])
    ha_ref[...], hb_ref[...] = _pack_row(hn)
    lgt = _router_logits(hn, wr_cat_ref, wr_hi_ref, br_ref).T
    neg = jnp.float32(-jnp.inf)
    gidx = lax.broadcasted_iota(I32, (SUBLANES, rows), 0)
    eidx = lax.broadcasted_iota(I32, (N_EXPERTS, rows), 0)
    group_lg = jnp.where(gidx < N_EXPERT_GROUPS, lgt[0:SUBLANES], neg)
    expert_lg = lgt[EXPERT_LANE0:EXPERT_LANE0 + N_EXPERTS]
    g_sel, i1, i2, w1, w2 = _select(group_lg, expert_lg, gidx, eidx, 0)
    lo = jnp.minimum(i1, i2) - g_sel * EXPERTS_PER_GROUP
    hi = jnp.maximum(i1, i2) - g_sel * EXPERTS_PER_GROUP
    cls = g_sel * PAIRS_PER_GROUP + ((lo * (7 - lo)) >> 1) + (hi - lo - 1)
    w_lo = jnp.where(i1 < i2, w1, w2)
    w_hi = jnp.where(i1 < i2, w2, w1)

    crow = lax.broadcasted_iota(I32, (CLASS_ROWS, rows), 0)
    onehot = jnp.where(crow == cls, 1.0, 0.0)
    n_blk = rows // COUNT_BLOCK
    blocks = [onehot[:, j * COUNT_BLOCK:(j + 1) * COUNT_BLOCK] for j in range(n_blk)]
    r = lax.broadcasted_iota(I32, (COUNT_BLOCK, COUNT_BLOCK), 0)
    c = lax.broadcasted_iota(I32, (COUNT_BLOCK, COUNT_BLOCK), 1)
    upper = jnp.where(r < c, 1.0, 0.0).astype(BF16)
    within = _dot(jnp.concatenate(blocks, axis=0).astype(BF16), upper)
    carry = carry_scr[:, 0:1]
    ranks = []
    for j in range(n_blk):
        before = within[j * CLASS_ROWS:(j + 1) * CLASS_ROWS] + carry
        ranks.append(jnp.sum(blocks[j] * before, axis=0, keepdims=True))
        carry = carry + jnp.sum(blocks[j], axis=1, keepdims=True)
    rank = jnp.concatenate(ranks, axis=1)
    carry_scr[...] = jnp.broadcast_to(carry, carry_scr.shape)
    count_ref[...] = carry_scr[...]

    row8 = lax.broadcasted_iota(I32, (SUBLANES, rows), 0)
    idx_ref[...] = jnp.where(row8 == 0, cls, jnp.where(row8 == 1, rank.astype(I32), 0))
    rowl = lax.broadcasted_iota(I32, (LANES, rows), 0)
    gate_ref[...] = jnp.where(rowl == GATE_LO, w_lo, jnp.where(rowl == GATE_HI, w_hi, 0.0)).T


def _moe_pair_kernel(ea_ref, eb_ref, used_ref, ha_ref, hb_ref, gate_ref, wg_a_ref, wu_a_ref,
                     wd_a_ref, wg_b_ref, wu_b_ref, wd_b_ref, ya_ref, yb_ref, wgu_scr, wd_scr):
    i = pl.program_id(0)
    prev = jnp.maximum(i - 1, 0)
    new_class = (i == 0) | (ea_ref[i] != ea_ref[prev]) | (eb_ref[i] != eb_ref[prev])

    @pl.when(new_class)
    def _():
        for s, (wg_ref, wu_ref, wd_ref) in enumerate(((wg_a_ref, wu_a_ref, wd_a_ref),
                                                      (wg_b_ref, wu_b_ref, wd_b_ref))):
            wgu_scr[s, :, :D_EXPERT] = wg_ref[0].astype(BF16)
            wgu_scr[s, :, D_EXPERT:] = wu_ref[0].astype(BF16)
            wd_scr[s] = wd_ref[0].astype(BF16)

    @pl.when(i < used_ref[0])
    def _():
        h = _unpack_row(ha_ref[...], hb_ref[...]).astype(BF16)
        gates = gate_ref[...]
        y = None
        for s, lane in enumerate((GATE_LO, GATE_HI)):
            gu = _dot(h, wgu_scr[s])
            act = _silu(gu[:, :D_EXPERT]) * gu[:, D_EXPERT:]
            part = _dot((act * gates[:, lane:lane + 1]).astype(BF16), wd_scr[s])
            y = part if y is None else y + part
        ya_ref[...], yb_ref[...] = _pack_row(y)


def _final_norm_kernel(x_ref, ya_ref, yb_ref, g_ref, o_ref):
    o_ref[...] = _rmsnorm(x_ref[...] + _unpack_row(ya_ref[...], yb_ref[...]), g_ref[...])


def _moe_dense_kernel(x_ref, ng_ref, wr_cat_ref, wr_hi_ref, br_ref, wg_ref, wu_ref, wd_ref, nf_ref,
                      y_ref, h_scr, gate_scr, acc_scr, *, final_norm):
    e = pl.program_id(1)

    @pl.when(e == 0)
    def _():
        hn = _rmsnorm(x_ref[...], ng_ref[...])
        h_scr[...] = hn.astype(BF16)
        lg = _router_logits(hn, wr_cat_ref, wr_hi_ref, br_ref)
        lane = lax.broadcasted_iota(I32, lg.shape, 1)
        neg = jnp.float32(-jnp.inf)
        group_lg = jnp.where(lane < N_EXPERT_GROUPS, lg, neg)
        is_expert = (lane >= EXPERT_LANE0) & (lane < EXPERT_LANE0 + N_EXPERTS)
        _, i1, i2, w1, w2 = _select(group_lg, jnp.where(is_expert, lg, neg), lane,
                                    jnp.where(is_expert, lane - EXPERT_LANE0, N_EXPERTS), 1)
        gate_scr[...] = (jnp.where(lane == i1 + EXPERT_LANE0, w1, 0.0)
                         + jnp.where(lane == i2 + EXPERT_LANE0, w2, 0.0))
        acc_scr[...] = jnp.zeros_like(acc_scr)

    lane = lax.broadcasted_iota(I32, gate_scr.shape, 1)
    gate = jnp.sum(jnp.where(lane == EXPERT_LANE0 + e, gate_scr[...], 0.0), axis=-1, keepdims=True)
    h = h_scr[...]
    act = _silu(_dot(h, wg_ref[0].astype(BF16))) * _dot(h, wu_ref[0].astype(BF16))
    acc_scr[...] += _dot((act * gate).astype(BF16), wd_ref[0].astype(BF16))

    @pl.when(e == N_EXPERTS - 1)
    def _():
        y = x_ref[...] + acc_scr[...]
        if final_norm:
            y = _rmsnorm(y, nf_ref[...])
        y_ref[...] = y


def _const_spec(shape):
    return pl.BlockSpec(shape, lambda *_: (0,) * len(shape))


def _resident_spec(shape):
    return pl.BlockSpec(shape, lambda *_: (0,) * len(shape), pipeline_mode=pl.Buffered(1))


_ANY_SPEC = pl.BlockSpec(memory_space=pl.ANY)


def _rope_tables(pos):
    half = RET_HEAD_DIM // 2
    inv = np.float32(ROPE_BASE) ** (-np.arange(half, dtype=np.float32) / np.float32(half))
    ang = pos.astype(np.float32)[:, None] * inv[None, :]
    cos, sin = np.cos(ang), np.sin(ang)
    return np.concatenate([cos, cos], -1), np.concatenate([-sin, sin], -1)


def _decay_consts(c, reps):
    f = np.float32
    log_g = np.log(f(1.0) - f(2.0) ** (f(-5.0) - np.arange(RET_HEADS, dtype=f)))
    i = np.arange(c, dtype=f)
    diff = i[:, None] - i[None, :]
    dmat = np.where(diff[None] >= 0, np.exp(np.maximum(diff, f(0))[None] * log_g[:, None, None]), f(0))
    qdec = np.exp((i + f(1))[None, :] * log_g[:, None])
    kdec = np.exp((f(c) - f(1) - i)[None, :] * log_g[:, None])
    cdec = np.exp(f(c) * log_g)
    dmat = np.einsum('ab,hij->haibj', np.eye(reps, dtype=f), dmat).reshape(RET_HEADS, reps * c, reps * c)
    lanes = lambda t: np.repeat(np.tile(t, (1, reps)).T, RET_HEAD_DIM, axis=1)
    cdec = np.broadcast_to(cdec[:, None, None], (RET_HEADS, 1, RET_HEAD_DIM))
    return tuple(np.ascontiguousarray(a, dtype=f) for a in (dmat, lanes(qdec), lanes(kdec), cdec))


def _split_weight(w):
    hi = lax.bitcast_convert_type(_bf16_round_bits(lax.bitcast_convert_type(w, jnp.uint32)), F32)
    return hi.astype(BF16), (w - hi).astype(BF16)


_W_IN_BLOCK = (None, D_MODEL, IN_WIDTH)
_W_POOL_BLOCK = (None, len(POOL_WINDOWS), POOL_GROUP_DIM, POOL_GROUP_DIM)
_W_OUT_BLOCK = (None, D_MODEL, D_MODEL)
_DECAY_SPECS = [
    _const_spec((RET_HEADS, CHUNK, CHUNK)), _const_spec((CHUNK, RET_WIDTH)),
    _const_spec((CHUNK, RET_WIDTH)), _const_spec((RET_HEADS, 1, RET_HEAD_DIM)),
]


def _layer_spec(block, layer, resident=False):
    index_map = lambda *_: (layer,) + (0,) * (len(block) - 1)
    if resident:
        return pl.BlockSpec(block, index_map, pipeline_mode=pl.Buffered(1))
    return pl.BlockSpec(block, index_map)


def _mixer_weight_specs(layer, split):
    n = 2 if split else 1
    row = lambda width: _layer_spec((None, 1, width), layer)
    return [row(D_MODEL), *[_layer_spec(_W_IN_BLOCK, layer, True)] * n,
            *[_layer_spec(_W_POOL_BLOCK, layer)] * n, row(POOL_WIDTH), row(RET_WIDTH),
            *[_layer_spec(_W_OUT_BLOCK, layer, True)] * n]


def _chain_out_spec(block, depth, layer, has_prev):
    zeros = (0,) * (len(block) - 1)
    if has_prev:
        return pl.BlockSpec((None,) + block, lambda i, *_: (layer, i) + zeros)
    return pl.BlockSpec((depth,) + block, lambda i, *_: (0, i) + zeros)


def _mixer_prompt(x, moe_y, weights, prev_out, layer, rows, precise_tail):
    b, l, _ = x.shape
    depth = weights[0].shape[0]
    assert l % rows == 0 and rows % CHUNK == 0
    cos2, sin2 = _rope_tables(np.arange(l))
    decay = _decay_consts(RET_CHUNK, 1)
    tok = lambda i, c: (i, c, 0)
    if moe_y is None:
        ya = yb = jnp.zeros((1, rows, PACK_W), I32)
        y_spec = _const_spec((1, rows, PACK_W))
    else:
        ya, yb = (t.reshape(b, l, PACK_W) for t in moe_y)
        y_spec = pl.BlockSpec((1, rows, PACK_W), tok)
    has_prev = prev_out is not None
    n_in = 18
    y, tails, states = pl.pallas_call(
        functools.partial(_mixer_prompt_kernel, rows=rows, moe_in=moe_y is not None,
                          precise_tail=precise_tail, layer=layer, has_prev=has_prev),
        grid=(b, l // rows),
        in_specs=[pl.BlockSpec((1, rows, D_MODEL), tok), y_spec, y_spec,
                  pl.BlockSpec((rows, RET_HEAD_DIM), lambda i, c: (c, 0)),
                  pl.BlockSpec((rows, RET_HEAD_DIM), lambda i, c: (c, 0)),
                  *_mixer_weight_specs(layer, True), *_DECAY_SPECS,
                  *([_ANY_SPEC, _ANY_SPEC] if has_prev else [])],
        out_specs=[pl.BlockSpec((1, rows, D_MODEL), tok),
                   _chain_out_spec((1, POOL_HIST, POOL_WIDTH), depth, layer, has_prev),
                   _chain_out_spec((1, RET_HEADS, RET_HEAD_DIM, RET_HEAD_DIM), depth, layer, has_prev)],
        out_shape=[jax.ShapeDtypeStruct(x.shape, F32),
                   jax.ShapeDtypeStruct((depth, b, POOL_HIST, POOL_WIDTH), F32),
                   jax.ShapeDtypeStruct((depth, b, RET_HEADS, RET_HEAD_DIM, RET_HEAD_DIM), F32)],
        input_output_aliases={n_in: 1, n_in + 1: 2} if has_prev else {},
        scratch_shapes=[pltpu.VMEM((1, POOL_HIST + rows, POOL_WIDTH), F32),
                        pltpu.VMEM((RET_HEADS, RET_HEAD_DIM, RET_HEAD_DIM), F32),
                        pltpu.VMEM((rows, RET_WIDTH), F32)],
        compiler_params=pltpu.CompilerParams(
            dimension_semantics=("arbitrary", "arbitrary"), vmem_limit_bytes=VMEM_LIMIT),
        name="mixer_prompt",
    )(x, ya, yb, cos2, sin2, *weights, *decay, *(prev_out if has_prev else ()))
    return y, (tails, states)


def _mixer_sample(x, pool_prev, s0, weights, prev_out, layer):
    b, l, _ = x.shape
    depth = s0.shape[0]
    ns = SAMPLE_SEQS
    assert ns * l == CHUNK and b % ns == 0
    cos2, sin2 = _rope_tables(PAST_LEN + np.arange(l))
    cos2, sin2 = np.tile(cos2, (ns, 1)), np.tile(sin2, (ns, 1))
    decay = _decay_consts(l, ns)
    seq3 = lambda i: (i, 0, 0)
    state_block = (ns, RET_HEADS, RET_HEAD_DIM, RET_HEAD_DIM)
    tail_block = (ns, POOL_HIST, POOL_WIDTH)
    has_prev = prev_out is not None
    n_in = 15
    y, tails, states = pl.pallas_call(
        functools.partial(_mixer_sample_kernel, seq_len=l, layer=layer, has_prev=has_prev),
        grid=(b // ns,),
        in_specs=[pl.BlockSpec((ns, l, D_MODEL), seq3),
                  pl.BlockSpec((None,) + tail_block, lambda i: (layer, i, 0, 0)),
                  pl.BlockSpec((None,) + state_block, lambda i: (layer, i, 0, 0, 0)),
                  _const_spec((CHUNK, RET_HEAD_DIM)), _const_spec((CHUNK, RET_HEAD_DIM)),
                  *_mixer_weight_specs(layer, False), *_DECAY_SPECS,
                  *([_ANY_SPEC, _ANY_SPEC] if has_prev else [])],
        out_specs=[pl.BlockSpec((ns, l, D_MODEL), seq3),
                   _chain_out_spec(tail_block, depth, layer, has_prev),
                   _chain_out_spec(state_block, depth, layer, has_prev)],
        out_shape=[jax.ShapeDtypeStruct(x.shape, F32),
                   jax.ShapeDtypeStruct((depth, b, POOL_HIST, POOL_WIDTH), F32),
                   jax.ShapeDtypeStruct(s0.shape, F32)],
        input_output_aliases={n_in: 1, n_in + 1: 2} if has_prev else {},
        scratch_shapes=[pltpu.VMEM((ns, POOL_HIST + l, POOL_WIDTH), F32),
                        pltpu.VMEM((CHUNK, RET_WIDTH), F32)],
        compiler_params=pltpu.CompilerParams(
            dimension_semantics=("arbitrary",), vmem_limit_bytes=VMEM_LIMIT),
        name="mixer_sample",
    )(x, pool_prev, s0, cos2, sin2, *weights, *decay, *(prev_out if has_prev else ()))
    return y, (tails, states)


def _router_weights(w_rg, b_rg, w_re, b_re):
    depth = w_rg.shape[0]
    gap = EXPERT_LANE0 - N_EXPERT_GROUPS
    rest = LANES - EXPERT_LANE0 - N_EXPERTS
    wr = jnp.concatenate([w_rg, jnp.zeros((depth, D_MODEL, gap), F32), w_re,
                          jnp.zeros((depth, D_MODEL, rest), F32)], axis=-1)
    br = jnp.concatenate([b_rg, jnp.zeros((depth, gap), F32), b_re, jnp.zeros((depth, rest), F32)],
                         axis=-1).reshape(depth, 1, LANES)
    wr_hi, wr_lo = _split_weight(wr)
    return jnp.concatenate([wr_hi, wr_lo], axis=-1), wr_hi, br


def _router_specs(layer):
    return [_layer_spec((None, D_MODEL, 2 * LANES), layer), _layer_spec((None, D_MODEL, LANES), layer),
            _layer_spec((None, 1, LANES), layer)]


def _sc_mesh():
    return plsc.VectorSubcoreMesh(core_axis_name="core", subcore_axis_name="subcore",
                                  num_cores=SC_CORES, num_subcores=SC_SUBCORES)


def _sc_params():
    params = pltpu.CompilerParams()
    if "needs_layout_passes" in pltpu.CompilerParams.__dataclass_fields__:
        params = dataclasses.replace(params, needs_layout_passes=False)
    return params


def _sc_gather(tables, idx):
    n = idx.shape[0]
    assert n % SC_WINDOW == 0
    nt = len(tables)

    def body(*refs):
        i_hbm = refs[nt]
        for t_hbm, o_hbm in zip(refs[:nt], refs[nt + 1:]):
            def gather_window(i_vmem, o_vmem, t_hbm=t_hbm):
                pltpu.sync_copy(t_hbm.at[i_vmem.at[0]], o_vmem)

            pltpu.emit_pipeline(
                gather_window, grid=(n // SC_WINDOW,),
                in_specs=[pl.BlockSpec((1, SC_WINDOW), lambda i: (0, i))],
                out_specs=[pl.BlockSpec((SC_WINDOW, t_hbm.shape[1]), lambda i: (i, 0))],
                core_axis_name=("core", "subcore"),
                dimension_semantics=(pltpu.PARALLEL,),
            )(i_hbm, o_hbm)

    out_type = tuple(jax.ShapeDtypeStruct((n, t.shape[1]), t.dtype) for t in tables)
    return pl.kernel(body, out_type=out_type, mesh=_sc_mesh(), name="sc_gather")(
        *tables, idx.reshape(1, n))


def _sc_slots(cls, rank, starts, n_slots):
    t = cls.shape[0]
    workers = SC_CORES * SC_SUBCORES
    slot_per, tok_per = n_slots // workers, t // workers
    assert n_slots % (workers * SC_LANES) == 0 and t % (workers * SC_LANES) == 0 and t & (t - 1) == 0

    def body(cls_hbm, rank_hbm, starts_hbm, pos_hbm, slot_hbm, cls_v, rank_v, starts_v, pos_v, slot_v):
        wid = lax.axis_index("subcore") * SC_CORES + lax.axis_index("core")
        pltpu.sync_copy(cls_hbm, cls_v)
        pltpu.sync_copy(rank_hbm, rank_v)
        pltpu.sync_copy(starts_hbm, starts_v)

        @pl.loop(0, n_slots, step=SC_LANES)
        def _(i):
            slot_v[pl.ds(i, SC_LANES)] = (lax.iota(I32, SC_LANES) + i) & (t - 1)

        @pl.loop(0, t, step=SC_LANES)
        def _(i):
            at = pl.ds(i, SC_LANES)
            pos = plsc.load_gather(starts_v, [cls_v[at]]) + rank_v[at]
            pos_v[at] = pos
            plsc.store_scatter(slot_v, [pos], lax.iota(I32, SC_LANES) + i)

        tok_off = pl.multiple_of(wid * tok_per, SC_LANES)
        pltpu.sync_copy(pos_v.at[pl.ds(tok_off, tok_per)], pos_hbm.at[pl.ds(tok_off, tok_per)])
        slot_off = pl.multiple_of(wid * slot_per, SC_LANES)
        pltpu.sync_copy(slot_v.at[pl.ds(slot_off, slot_per)], slot_hbm.at[pl.ds(slot_off, slot_per)])

    return pl.kernel(
        body, mesh=_sc_mesh(), compiler_params=_sc_params(), name="sc_slots",
        out_type=(jax.ShapeDtypeStruct((t,), I32), jax.ShapeDtypeStruct((n_slots,), I32)),
        scratch_types=[pltpu.VMEM((t,), I32), pltpu.VMEM((t,), I32), pltpu.VMEM((CLASS_ROWS,), I32),
                       pltpu.VMEM((t,), I32), pltpu.VMEM((n_slots,), I32)],
    )(cls, rank, starts)


def _moe_sparse(x, norm_g, router, w_gate, w_up, w_down, layer):
    t = x.shape[0]
    n_slots = t + N_CLASSES * PAIR_TILE
    n_tiles = n_slots // PAIR_TILE
    tok = lambda i: (i, 0)
    ha, hb, gates, idx, counts = pl.pallas_call(
        _moe_route_kernel,
        grid=(t // ROUTE_TILE,),
        in_specs=[pl.BlockSpec((ROUTE_TILE, D_MODEL), tok), _layer_spec((None, 1, D_MODEL), layer),
                  *_router_specs(layer)],
        out_specs=[pl.BlockSpec((ROUTE_TILE, PACK_W), tok), pl.BlockSpec((ROUTE_TILE, PACK_W), tok),
                   pl.BlockSpec((ROUTE_TILE, LANES), tok),
                   pl.BlockSpec((SUBLANES, ROUTE_TILE), lambda i: (0, i)),
                   _const_spec((CLASS_ROWS, LANES))],
        out_shape=[jax.ShapeDtypeStruct((t, PACK_W), I32), jax.ShapeDtypeStruct((t, PACK_W), I32),
                   jax.ShapeDtypeStruct((t, LANES), F32), jax.ShapeDtypeStruct((SUBLANES, t), I32),
                   jax.ShapeDtypeStruct((CLASS_ROWS, LANES), F32)],
        scratch_shapes=[pltpu.VMEM((CLASS_ROWS, LANES), F32)],
        compiler_params=pltpu.CompilerParams(
            dimension_semantics=("arbitrary",), vmem_limit_bytes=VMEM_LIMIT),
        name="moe_route",
    )(x, norm_g, *router)

    cnt = counts[:, 0].astype(I32)
    padded = (cnt + PAIR_TILE - 1) // PAIR_TILE * PAIR_TILE
    ends = jnp.cumsum(padded)
    starts = ends - padded
    tile_cls = jnp.minimum(
        jnp.sum(ends[None, :N_CLASSES] <= (jnp.arange(n_tiles, dtype=I32) * PAIR_TILE)[:, None], axis=1),
        N_CLASSES - 1).astype(I32)
    first = (tile_cls // PAIRS_PER_GROUP) * EXPERTS_PER_GROUP
    tile_ea = first + jnp.asarray(PAIR_LO, I32)[tile_cls % PAIRS_PER_GROUP]
    tile_eb = first + jnp.asarray(PAIR_HI, I32)[tile_cls % PAIRS_PER_GROUP]
    used = (ends[N_CLASSES - 1:N_CLASSES] // PAIR_TILE).astype(I32)

    pos, slot_tok = _sc_slots(idx[0], idx[1], starts, n_slots)
    hsa, hsb, gate_s = _sc_gather((ha, hb, gates), slot_tok)

    row = lambda i, ea, eb, nu: (jnp.minimum(i, nu[0] - 1), 0)
    w_spec = lambda shape, which: pl.BlockSpec(
        (None, 1) + shape, lambda i, ea, eb, nu: (layer, (ea, eb)[which][i], 0, 0))
    gu_shape, d_shape = (D_MODEL, D_EXPERT), (D_EXPERT, D_MODEL)
    ysa, ysb = pl.pallas_call(
        _moe_pair_kernel,
        grid_spec=pltpu.PrefetchScalarGridSpec(
            num_scalar_prefetch=3, grid=(n_tiles,),
            in_specs=[pl.BlockSpec((PAIR_TILE, PACK_W), row), pl.BlockSpec((PAIR_TILE, PACK_W), row),
                      pl.BlockSpec((PAIR_TILE, LANES), row),
                      w_spec(gu_shape, 0), w_spec(gu_shape, 0), w_spec(d_shape, 0),
                      w_spec(gu_shape, 1), w_spec(gu_shape, 1), w_spec(d_shape, 1)],
            out_specs=[pl.BlockSpec((PAIR_TILE, PACK_W), row), pl.BlockSpec((PAIR_TILE, PACK_W), row)],
            scratch_shapes=[pltpu.VMEM((2, D_MODEL, 2 * D_EXPERT), BF16),
                            pltpu.VMEM((2, D_EXPERT, D_MODEL), BF16)]),
        out_shape=[jax.ShapeDtypeStruct((n_slots, PACK_W), I32),
                   jax.ShapeDtypeStruct((n_slots, PACK_W), I32)],
        compiler_params=pltpu.CompilerParams(
            dimension_semantics=("arbitrary",), vmem_limit_bytes=VMEM_LIMIT),
        name="moe_pair",
    )(tile_ea, tile_eb, used, hsa, hsb, gate_s, w_gate, w_up, w_down, w_gate, w_up, w_down)
    return _sc_gather((ysa, ysb), pos)


def _final_norm(x, moe_y, g, rows):
    t = x.shape[0]
    tok = lambda i: (i, 0)
    return pl.pallas_call(
        _final_norm_kernel,
        grid=(t // rows,),
        in_specs=[pl.BlockSpec((rows, D_MODEL), tok), pl.BlockSpec((rows, PACK_W), tok),
                  pl.BlockSpec((rows, PACK_W), tok), _const_spec((1, D_MODEL))],
        out_specs=pl.BlockSpec((rows, D_MODEL), tok),
        out_shape=jax.ShapeDtypeStruct(x.shape, F32),
        compiler_params=pltpu.CompilerParams(
            dimension_semantics=("arbitrary",), vmem_limit_bytes=VMEM_LIMIT),
        name="final_norm",
    )(x, *moe_y, g.reshape(1, D_MODEL))


def _moe_dense(x, norm_g, router, w_gate, w_up, w_down, norm_final, layer, final_norm, rows):
    t = x.shape[0]
    assert t % rows == 0
    tok = lambda i, e: (i, 0)
    w_spec = lambda shape: pl.BlockSpec((None, 1) + shape, lambda i, e: (layer, e, 0, 0))
    return pl.pallas_call(
        functools.partial(_moe_dense_kernel, final_norm=final_norm),
        grid=(t // rows, N_EXPERTS),
        in_specs=[pl.BlockSpec((rows, D_MODEL), tok), _layer_spec((None, 1, D_MODEL), layer),
                  *_router_specs(layer),
                  w_spec((D_MODEL, D_EXPERT)), w_spec((D_MODEL, D_EXPERT)), w_spec((D_EXPERT, D_MODEL)),
                  _const_spec((1, D_MODEL))],
        out_specs=pl.BlockSpec((rows, D_MODEL), tok),
        out_shape=jax.ShapeDtypeStruct(x.shape, F32),
        scratch_shapes=[pltpu.VMEM((rows, D_MODEL), BF16),
                        pltpu.VMEM((rows, LANES), F32),
                        pltpu.VMEM((rows, D_MODEL), F32)],
        compiler_params=pltpu.CompilerParams(
            dimension_semantics=("arbitrary", "arbitrary"), vmem_limit_bytes=VMEM_LIMIT),
        name="moe_dense",
    )(x, norm_g, *router, w_gate, w_up, w_down, norm_final.reshape(1, D_MODEL))


def kernel(x_prompt, x_sample, cache_pool, state_ret, norm_mix, w_in, w_pool, pool_scale, ret_gn, w_out, norm_ffn, w_router_group, b_router_group, w_router_expert, b_router_expert, w_gate, w_up, w_down, norm_final):
    depth = norm_mix.shape[0]
    row = lambda a: a.reshape(depth, 1, a.shape[-1])
    mix_split = (row(norm_mix), *_split_weight(w_in), *_split_weight(w_pool), row(pool_scale),
                 row(ret_gn), *_split_weight(w_out))
    mix_hi = tuple(mix_split[i] for i in (0, 1, 3, 5, 6, 7))
    router = _router_weights(w_router_group, b_router_group, w_router_expert, b_router_expert)
    norm_ffn = row(norm_ffn)
    pool_prev = jnp.pad(cache_pool, ((0, 0), (0, 0), (POOL_HIST - POOL_BUF, 0), (0, 0)))

    yp, ys = x_prompt, x_sample
    moe_p = None
    out_p = out_s = None
    for l in range(depth):
        yp, out_p = _mixer_prompt(yp, moe_p, mix_split, out_p, l, rows=512,
                                  precise_tail=PRECISE_TAIL_STEPS if l < depth - 1 else 0)
        ys, out_s = _mixer_sample(ys, pool_prev, state_ret, mix_hi, out_s, l)
        moe_p = _moe_sparse(yp.reshape(-1, D_MODEL), norm_ffn, router, w_gate, w_up, w_down, l)
        ys = _moe_dense(ys.reshape(-1, D_MODEL), norm_ffn, router, w_gate, w_up, w_down, norm_final,
                        l, l == depth - 1, rows=1024).reshape(ys.shape)
    yp = _final_norm(yp.reshape(-1, D_MODEL), moe_p, norm_final, rows=1024).reshape(yp.shape)
    tail = POOL_HIST - POOL_BUF
    return (yp, ys, out_p[0][:, :, tail:], out_p[1], out_s[0][:, :, tail:], out_s[1])
```

```python
import dataclasses
import functools

import jax
import jax.numpy as jnp
import numpy as np
from jax import lax
from jax.experimental import pallas as pl
from jax.experimental.pallas import tpu as pltpu
from jax.experimental.pallas import tpu_sc as plsc

F32 = jnp.float32
BF16 = jnp.bfloat16
I32 = jnp.int32

D_MODEL = 1024
POOL_WIDTH = 512
POOL_WINDOWS = (2, 4, 8, 16)
POOL_GROUP_DIM = 128
POOL_BUF = 15
POOL_HIST = 16
RET_WIDTH = 512
RET_HEADS = 4
RET_HEAD_DIM = 128
RET_CHUNK = 128
ROPE_BASE = 10000.0
IN_WIDTH = POOL_WIDTH + 4 * RET_WIDTH
N_EXPERT_GROUPS = 4
EXPERTS_PER_GROUP = 4
N_EXPERTS = 16
D_EXPERT = 256
RMS_EPS = 1e-6
GN_EPS = 1e-5
PAST_LEN = 16384

LANES = 128
SUBLANES = 8
EXPERT_LANE0 = 8
PAIRS_PER_GROUP = 6
N_CLASSES = N_EXPERT_GROUPS * PAIRS_PER_GROUP
CLASS_ROWS = 32
PAIR_LO = (0, 0, 0, 1, 1, 2)
PAIR_HI = (1, 2, 3, 2, 3, 3)
GATE_LO, GATE_HI = 0, 1
ROUTE_TILE = 1024
COUNT_BLOCK = 256
PAIR_TILE = 256
PACK_W = D_MODEL // 4
SC_CORES, SC_SUBCORES, SC_LANES = 2, 16, 16
SC_WINDOW = 128
PRECISE_TAIL_STEPS = 1
CHUNK = 128
SAMPLE_SEQS = 16
VMEM_LIMIT = 56 * 1024 * 1024


def _dot(a, b):
    return jnp.dot(a, b, preferred_element_type=F32)


def _dot_nt(a, b):
    return lax.dot_general(a, b, (((1,), (1,)), ((), ())), preferred_element_type=F32)


def _bf16_round_bits(u):
    return (u + jnp.uint32(0x7FFF) + ((u >> 16) & jnp.uint32(1))) & jnp.uint32(0xFFFF0000)


def _split(a):
    hi = pltpu.bitcast(_bf16_round_bits(pltpu.bitcast(a, jnp.uint32)), F32)
    return hi.astype(BF16), (a - hi).astype(BF16)


def _mm(a, b, precise, nt=False):
    dot = _dot_nt if nt else _dot
    if precise:
        b_hi, b_lo = b if isinstance(b, tuple) else _split(b)
        a_hi, a_lo = _split(a)
        return dot(a_hi, b_hi) + dot(a_lo, b_hi) + dot(a_hi, b_lo)
    return dot(a.astype(BF16), b[0] if isinstance(b, tuple) else b.astype(BF16))


def _rmsnorm(x, g):
    ms = jnp.mean(x * x, axis=-1, keepdims=True)
    return x * lax.rsqrt(ms + RMS_EPS) * g


def _pool_mix(ubuf, rows, t_first, n_prev, wpool_refs, pscale, precise=False):
    ns = ubuf.shape[0]
    t = t_first + lax.broadcasted_iota(I32, (1, rows, POOL_GROUP_DIM), 1)
    outs = []
    for j, w in enumerate(POOL_WINDOWS):
        lanes = slice(j * POOL_GROUP_DIM, (j + 1) * POOL_GROUP_DIM)
        uj = ubuf[:, POOL_HIST:POOL_HIST + rows, lanes]
        acc = uj
        for i in range(1, w):
            acc = acc + ubuf[:, POOL_HIST - i:POOL_HIST - i + rows, lanes]
        cnt = jnp.minimum(w, n_prev + t + 1).astype(F32)
        d = (acc / cnt - uj).reshape(ns * rows, POOL_GROUP_DIM)
        outs.append(_mm(d, tuple(w[j] for w in wpool_refs), precise))
    return jnp.concatenate(outs, axis=-1) * pscale


def _rope(xh, cos2, sin2):
    return xh * cos2 + pltpu.roll(xh, RET_HEAD_DIM // 2, 1) * sin2


def _group_norm(o):
    mu = jnp.mean(o, axis=-1, keepdims=True)
    c = o - mu
    var = jnp.mean(c * c, axis=-1, keepdims=True)
    return c * lax.rsqrt(var + GN_EPS)


def _silu(x):
    return x * (1.0 / (1.0 + jnp.exp(-x)))


def _head(a, h):
    return a[:, h * RET_HEAD_DIM:(h + 1) * RET_HEAD_DIM]


def _qkvg(z):
    p, r = POOL_WIDTH, RET_WIDTH
    return z[:, p:p + r], z[:, p + r:p + 2 * r], z[:, p + 2 * r:p + 3 * r], z[:, p + 3 * r:p + 4 * r]


def _pack_bf16_pair(a, b):
    ua = pltpu.bitcast(a.astype(BF16).astype(F32), jnp.uint32)
    ub = pltpu.bitcast(b.astype(BF16).astype(F32), jnp.uint32)
    return pltpu.bitcast((ua >> 16) | (ub & jnp.uint32(0xFFFF0000)), I32)


def _unpack_bf16_pair(w):
    u = pltpu.bitcast(w, jnp.uint32)
    return pltpu.bitcast(u << 16, F32), pltpu.bitcast(u & jnp.uint32(0xFFFF0000), F32)


def _pack_row(y):
    q = PACK_W
    return _pack_bf16_pair(y[:, 0:q], y[:, q:2 * q]), _pack_bf16_pair(y[:, 2 * q:3 * q], y[:, 3 * q:])


def _unpack_row(wa, wb):
    return jnp.concatenate([*_unpack_bf16_pair(wa), *_unpack_bf16_pair(wb)], axis=-1)


def _zero_other_layers(ref, layer):
    for j in range(ref.shape[0]):
        if j != layer:
            ref[j] = jnp.zeros(ref.shape[1:], ref.dtype)


def _mixer_prompt_kernel(*refs, rows, moe_in, precise_tail, layer, has_prev):
    (x_ref, ya_ref, yb_ref, cos_ref, sin_ref, ng_ref, win_hi_ref, win_lo_ref, wpool_hi_ref,
     wpool_lo_ref, pscale_ref, gn_ref, wout_hi_ref, wout_lo_ref, dmat_ref, qdec_ref, kdec_ref,
     cdec_ref) = refs[:18]
    y_ref, tail_ref, sfin_ref, ubuf, s_scr, ret_scr = refs[18 + 2 * has_prev:]
    c = pl.program_id(1)
    if not has_prev:
        _zero_other_layers(tail_ref, layer)
        _zero_other_layers(sfin_ref, layer)
        tail_ref, sfin_ref = tail_ref.at[layer], sfin_ref.at[layer]

    @pl.when(c == 0)
    def _():
        ubuf[:, 0:POOL_HIST, :] = jnp.zeros((1, POOL_HIST, POOL_WIDTH), F32)
        s_scr[...] = jnp.zeros_like(s_scr)

    def step(precise):
        x = x_ref[0]
        if moe_in:
            x = x + _unpack_row(ya_ref[0], yb_ref[0])
        hn = _rmsnorm(x, ng_ref[...])
        z = _mm(hn, (win_hi_ref[...], win_lo_ref[...]), precise)
        ubuf[0, POOL_HIST:POOL_HIST + rows, :] = z[:, :POOL_WIDTH]
        pool_out = _pool_mix(ubuf, rows, c * rows, 0, (wpool_hi_ref, wpool_lo_ref),
                             pscale_ref[...], precise)
        tail_ref[...] = ubuf[:, rows:rows + POOL_HIST, :]
        ubuf[:, 0:POOL_HIST, :] = ubuf[:, rows:rows + POOL_HIST, :]

        q_all, k_all, v_all, g_all = _qkvg(z)
        scale = RET_HEAD_DIM ** -0.5
        for ci in range(rows // CHUNK):
            rs = slice(ci * CHUNK, (ci + 1) * CHUNK)
            cos2 = cos_ref[rs, :]
            sin2 = sin_ref[rs, :]
            for h in range(RET_HEADS):
                q = _rope(_head(q_all, h)[rs], cos2, sin2)
                k = _rope(_head(k_all, h)[rs], cos2, sin2) * scale
                v = _split(_head(v_all, h)[rs]) if precise else _head(v_all, h)[rs]
                s_old = s_scr[h]
                scores = _mm(q, k, precise, nt=True) * dmat_ref[h]
                qd = q * _head(qdec_ref[...], h)
                o = _mm(scores, v, precise) + _mm(qd, s_old, precise)
                kd = k * _head(kdec_ref[...], h)
                s_scr[h] = s_old * cdec_ref[h] + _mm(kd.T, v, precise)
                on = _group_norm(o) * _head(gn_ref[...], h)
                ret_scr[rs, h * RET_HEAD_DIM:(h + 1) * RET_HEAD_DIM] = _silu(_head(g_all, h)[rs]) * on

        mix_in = jnp.concatenate([pool_out, ret_scr[...]], axis=-1)
        y_ref[0] = x + _mm(mix_in, (wout_hi_ref[...], wout_lo_ref[...]), precise)
        sfin_ref[0] = s_scr[...]

    if precise_tail:
        first_precise = pl.num_programs(1) - precise_tail
        pl.when(c >= first_precise)(lambda: step(True))
        pl.when(c < first_precise)(lambda: step(False))
    else:
        step(False)


def _mixer_sample_kernel(*refs, seq_len, layer, has_prev):
    (x_ref, prev_ref, s0_ref, cos_ref, sin_ref, ng_ref, win_ref, wpool_ref, pscale_ref, gn_ref,
     wout_ref, dmat_ref, qdec_ref, kdec_ref, cdec_ref) = refs[:15]
    y_ref, tail_ref, sfin_ref, ubuf, ret_scr = refs[16 + 2 * has_prev:]
    if not has_prev:
        _zero_other_layers(tail_ref, layer)
        _zero_other_layers(sfin_ref, layer)
        tail_ref, sfin_ref = tail_ref.at[layer], sfin_ref.at[layer]
    ns = SAMPLE_SEQS
    rows = ns * seq_len
    x = x_ref[...].reshape(rows, D_MODEL)
    hn = _rmsnorm(x, ng_ref[...]).astype(BF16)
    z = _dot(hn, win_ref[...])
    ubuf[:, 0:POOL_HIST, :] = prev_ref[...]
    ubuf[:, POOL_HIST:POOL_HIST + seq_len, :] = z[:, :POOL_WIDTH].reshape(ns, seq_len, POOL_WIDTH)
    pool_out = _pool_mix(ubuf, seq_len, 0, POOL_BUF, (wpool_ref,), pscale_ref[...])
    tail_ref[...] = ubuf[:, seq_len:seq_len + POOL_HIST, :]

    q_all, k_all, v_all, g_all = _qkvg(z)
    scale = RET_HEAD_DIM ** -0.5
    cos2 = cos_ref[...]
    sin2 = sin_ref[...]
    tok_seq = lax.broadcasted_iota(I32, (RET_HEAD_DIM, rows), 1) // seq_len
    for h in range(RET_HEADS):
        q = _rope(_head(q_all, h), cos2, sin2)
        k = _rope(_head(k_all, h), cos2, sin2) * scale
        vb = _head(v_all, h).astype(BF16)
        s_old = s0_ref[:, h]
        scores = _dot_nt(q.astype(BF16), k.astype(BF16)) * dmat_ref[h]
        qd = (q * _head(qdec_ref[...], h)).astype(BF16).reshape(ns, seq_len, RET_HEAD_DIM)
        o_state = jnp.einsum('bid,bde->bie', qd, s_old.astype(BF16), preferred_element_type=F32)
        o = _dot(scores.astype(BF16), vb) + o_state.reshape(rows, RET_HEAD_DIM)
        kdt = (k * _head(kdec_ref[...], h)).T
        lhs = jnp.concatenate(
            [jnp.where(tok_seq == b, kdt, 0.0).astype(BF16) for b in range(ns)], axis=0)
        upd = _dot(lhs, vb).reshape(ns, RET_HEAD_DIM, RET_HEAD_DIM)
        sfin_ref[:, h] = s_old * cdec_ref[h] + upd
        on = _group_norm(o) * _head(gn_ref[...], h)
        ret_scr[:, h * RET_HEAD_DIM:(h + 1) * RET_HEAD_DIM] = _silu(_head(g_all, h)) * on

    mix_in = jnp.concatenate([pool_out, ret_scr[...]], axis=-1).astype(BF16)
    y_ref[...] = (x + _dot(mix_in, wout_ref[...])).reshape(ns, seq_len, D_MODEL)


def _router_logits(hn, wr_cat_ref, wr_hi_ref, br_ref):
    h_hi, h_lo = _split(hn)
    part = _dot(h_hi, wr_cat_ref[...])
    return part[:, :LANES] + part[:, LANES:] + _dot(h_lo, wr_hi_ref[...]) + br_ref[...]


def _select(group_lg, expert_lg, gidx, eidx, axis):
    red = dict(axis=axis, keepdims=True)
    neg = jnp.float32(-jnp.inf)
    gmax = jnp.max(group_lg, **red)
    g_sel = jnp.min(jnp.where(group_lg == gmax, gidx, N_EXPERT_GROUPS), **red)
    p_sel = 1.0 / jnp.sum(jnp.exp(group_lg - gmax), **red)
    emask = (eidx >> 2) == g_sel
    v1 = jnp.max(jnp.where(emask, expert_lg, neg), **red)
    i1 = jnp.min(jnp.where(emask & (expert_lg == v1), eidx, N_EXPERTS), **red)
    emask2 = emask & (eidx != i1)
    v2 = jnp.max(jnp.where(emask2, expert_lg, neg), **red)
    i2 = jnp.min(jnp.where(emask2 & (expert_lg == v2), eidx, N_EXPERTS), **red)
    e2 = jnp.exp(v2 - v1)
    return g_sel, i1, i2, p_sel / (1.0 + e2), p_sel * e2 / (1.0 + e2)


def _moe_route_kernel(x_ref, ng_ref, wr_cat_ref, wr_hi_ref, br_ref, ha_ref, hb_ref, gate_ref,
                      idx_ref, count_ref, carry_scr):
    i = pl.program_id(0)
    rows = x_ref.shape[0]

    @pl.when(i == 0)
    def _():
        carry_scr[...] = jnp.zeros_like(carry_scr)

    hn = _rmsnorm(x_ref[...], ng_ref[...])
    ha_ref[...], hb_ref[...] = _pack_row(hn)
    lgt = _router_logits(hn, wr_cat_ref, wr_hi_ref, br_ref).T
    neg = jnp.float32(-jnp.inf)
    gidx = lax.broadcasted_iota(I32, (SUBLANES, rows), 0)
    eidx = lax.broadcasted_iota(I32, (N_EXPERTS, rows), 0)
    group_lg = jnp.where(gidx < N_EXPERT_GROUPS, lgt[0:SUBLANES], neg)
    expert_lg = lgt[EXPERT_LANE0:EXPERT_LANE0 + N_EXPERTS]
    g_sel, i1, i2, w1, w2 = _select(group_lg, expert_lg, gidx, eidx, 0)
    lo = jnp.minimum(i1, i2) - g_sel * EXPERTS_PER_GROUP
    hi = jnp.maximum(i1, i2) - g_sel * EXPERTS_PER_GROUP
    cls = g_sel * PAIRS_PER_GROUP + ((lo * (7 - lo)) >> 1) + (hi - lo - 1)
    w_lo = jnp.where(i1 < i2, w1, w2)
    w_hi = jnp.where(i1 < i2, w2, w1)

    crow = lax.broadcasted_iota(I32, (CLASS_ROWS, rows), 0)
    onehot = jnp.where(crow == cls, 1.0, 0.0)
    n_blk = rows // COUNT_BLOCK
    blocks = [onehot[:, j * COUNT_BLOCK:(j + 1) * COUNT_BLOCK] for j in range(n_blk)]
    r = lax.broadcasted_iota(I32, (COUNT_BLOCK, COUNT_BLOCK), 0)
    c = lax.broadcasted_iota(I32, (COUNT_BLOCK, COUNT_BLOCK), 1)
    upper = jnp.where(r < c, 1.0, 0.0).astype(BF16)
    within = _dot(jnp.concatenate(blocks, axis=0).astype(BF16), upper)
    carry = carry_scr[:, 0:1]
    ranks = []
    for j in range(n_blk):
        before = within[j * CLASS_ROWS:(j + 1) * CLASS_ROWS] + carry
        ranks.append(jnp.sum(blocks[j] * before, axis=0, keepdims=True))
        carry = carry + jnp.sum(blocks[j], axis=1, keepdims=True)
    rank = jnp.concatenate(ranks, axis=1)
    carry_scr[...] = jnp.broadcast_to(carry, carry_scr.shape)
    count_ref[...] = carry_scr[...]

    row8 = lax.broadcasted_iota(I32, (SUBLANES, rows), 0)
    idx_ref[...] = jnp.where(row8 == 0, cls, jnp.where(row8 == 1, rank.astype(I32), 0))
    rowl = lax.broadcasted_iota(I32, (LANES, rows), 0)
    gate_ref[...] = jnp.where(rowl == GATE_LO, w_lo, jnp.where(rowl == GATE_HI, w_hi, 0.0)).T


def _moe_pair_kernel(ea_ref, eb_ref, used_ref, ha_ref, hb_ref, gate_ref, wg_a_ref, wu_a_ref,
                     wd_a_ref, wg_b_ref, wu_b_ref, wd_b_ref, after_ref, ya_ref, yb_ref, wgu_scr,
                     wd_scr):
    del after_ref
    i = pl.program_id(0)
    prev = jnp.maximum(i - 1, 0)
    new_class = (i == 0) | (ea_ref[i] != ea_ref[prev]) | (eb_ref[i] != eb_ref[prev])

    @pl.when(new_class)
    def _():
        for s, (wg_ref, wu_ref, wd_ref) in enumerate(((wg_a_ref, wu_a_ref, wd_a_ref),
                                                      (wg_b_ref, wu_b_ref, wd_b_ref))):
            wgu_scr[s, :, :D_EXPERT] = wg_ref[0].astype(BF16)
            wgu_scr[s, :, D_EXPERT:] = wu_ref[0].astype(BF16)
            wd_scr[s] = wd_ref[0].astype(BF16)

    @pl.when(i < used_ref[0])
    def _():
        h = _unpack_row(ha_ref[...], hb_ref[...]).astype(BF16)
        gates = gate_ref[...]
        y = None
        for s, lane in enumerate((GATE_LO, GATE_HI)):
            gu = _dot(h, wgu_scr[s])
            act = _silu(gu[:, :D_EXPERT]) * gu[:, D_EXPERT:]
            part = _dot((act * gates[:, lane:lane + 1]).astype(BF16), wd_scr[s])
            y = part if y is None else y + part
        ya_ref[...], yb_ref[...] = _pack_row(y)


def _final_norm_kernel(x_ref, ya_ref, yb_ref, g_ref, o_ref):
    o_ref[...] = _rmsnorm(x_ref[...] + _unpack_row(ya_ref[...], yb_ref[...]), g_ref[...])


def _moe_dense_kernel(x_ref, ng_ref, wr_cat_ref, wr_hi_ref, br_ref, wg_ref, wu_ref, wd_ref, nf_ref,
                      y_ref, h_scr, gate_scr, acc_scr, *, final_norm):
    e = pl.program_id(1)

    @pl.when(e == 0)
    def _():
        hn = _rmsnorm(x_ref[...], ng_ref[...])
        h_scr[...] = hn.astype(BF16)
        lg = _router_logits(hn, wr_cat_ref, wr_hi_ref, br_ref)
        lane = lax.broadcasted_iota(I32, lg.shape, 1)
        neg = jnp.float32(-jnp.inf)
        group_lg = jnp.where(lane < N_EXPERT_GROUPS, lg, neg)
        is_expert = (lane >= EXPERT_LANE0) & (lane < EXPERT_LANE0 + N_EXPERTS)
        _, i1, i2, w1, w2 = _select(group_lg, jnp.where(is_expert, lg, neg), lane,
                                    jnp.where(is_expert, lane - EXPERT_LANE0, N_EXPERTS), 1)
        gate_scr[...] = (jnp.where(lane == i1 + EXPERT_LANE0, w1, 0.0)
                         + jnp.where(lane == i2 + EXPERT_LANE0, w2, 0.0))
        acc_scr[...] = jnp.zeros_like(acc_scr)

    lane = lax.broadcasted_iota(I32, gate_scr.shape, 1)
    gate = jnp.sum(jnp.where(lane == EXPERT_LANE0 + e, gate_scr[...], 0.0), axis=-1, keepdims=True)
    h = h_scr[...]
    act = _silu(_dot(h, wg_ref[0].astype(BF16))) * _dot(h, wu_ref[0].astype(BF16))
    acc_scr[...] += _dot((act * gate).astype(BF16), wd_ref[0].astype(BF16))

    @pl.when(e == N_EXPERTS - 1)
    def _():
        y = x_ref[...] + acc_scr[...]
        if final_norm:
            y = _rmsnorm(y, nf_ref[...])
        y_ref[...] = y


def _const_spec(shape):
    return pl.BlockSpec(shape, lambda *_: (0,) * len(shape))


def _resident_spec(shape):
    return pl.BlockSpec(shape, lambda *_: (0,) * len(shape), pipeline_mode=pl.Buffered(1))


_ANY_SPEC = pl.BlockSpec(memory_space=pl.ANY)


def _rope_tables(pos):
    half = RET_HEAD_DIM // 2
    inv = np.float32(ROPE_BASE) ** (-np.arange(half, dtype=np.float32) / np.float32(half))
    ang = pos.astype(np.float32)[:, None] * inv[None, :]
    cos, sin = np.cos(ang), np.sin(ang)
    return np.concatenate([cos, cos], -1), np.concatenate([-sin, sin], -1)


def _decay_consts(c, reps):
    f = np.float32
    log_g = np.log(f(1.0) - f(2.0) ** (f(-5.0) - np.arange(RET_HEADS, dtype=f)))
    i = np.arange(c, dtype=f)
    diff = i[:, None] - i[None, :]
    dmat = np.where(diff[None] >= 0, np.exp(np.maximum(diff, f(0))[None] * log_g[:, None, None]), f(0))
    qdec = np.exp((i + f(1))[None, :] * log_g[:, None])
    kdec = np.exp((f(c) - f(1) - i)[None, :] * log_g[:, None])
    cdec = np.exp(f(c) * log_g)
    dmat = np.einsum('ab,hij->haibj', np.eye(reps, dtype=f), dmat).reshape(RET_HEADS, reps * c, reps * c)
    lanes = lambda t: np.repeat(np.tile(t, (1, reps)).T, RET_HEAD_DIM, axis=1)
    cdec = np.broadcast_to(cdec[:, None, None], (RET_HEADS, 1, RET_HEAD_DIM))
    return tuple(np.ascontiguousarray(a, dtype=f) for a in (dmat, lanes(qdec), lanes(kdec), cdec))


def _split_weight(w):
    hi = lax.bitcast_convert_type(_bf16_round_bits(lax.bitcast_convert_type(w, jnp.uint32)), F32)
    return hi.astype(BF16), (w - hi).astype(BF16)


_W_IN_BLOCK = (None, D_MODEL, IN_WIDTH)
_W_POOL_BLOCK = (None, len(POOL_WINDOWS), POOL_GROUP_DIM, POOL_GROUP_DIM)
_W_OUT_BLOCK = (None, D_MODEL, D_MODEL)
_DECAY_SPECS = [
    _const_spec((RET_HEADS, CHUNK, CHUNK)), _const_spec((CHUNK, RET_WIDTH)),
    _const_spec((CHUNK, RET_WIDTH)), _const_spec((RET_HEADS, 1, RET_HEAD_DIM)),
]


def _layer_spec(block, layer, resident=False):
    index_map = lambda *_: (layer,) + (0,) * (len(block) - 1)
    if resident:
        return pl.BlockSpec(block, index_map, pipeline_mode=pl.Buffered(1))
    return pl.BlockSpec(block, index_map)


def _mixer_weight_specs(layer, split):
    n = 2 if split else 1
    row = lambda width: _layer_spec((None, 1, width), layer)
    return [row(D_MODEL), *[_layer_spec(_W_IN_BLOCK, layer, True)] * n,
            *[_layer_spec(_W_POOL_BLOCK, layer)] * n, row(POOL_WIDTH), row(RET_WIDTH),
            *[_layer_spec(_W_OUT_BLOCK, layer, True)] * n]


def _chain_out_spec(block, depth, layer, has_prev):
    zeros = (0,) * (len(block) - 1)
    if has_prev:
        return pl.BlockSpec((None,) + block, lambda i, *_: (layer, i) + zeros)
    return pl.BlockSpec((depth,) + block, lambda i, *_: (0, i) + zeros)


def _mixer_prompt(x, moe_y, weights, prev_out, layer, rows, precise_tail):
    b, l, _ = x.shape
    depth = weights[0].shape[0]
    assert l % rows == 0 and rows % CHUNK == 0
    cos2, sin2 = _rope_tables(np.arange(l))
    decay = _decay_consts(RET_CHUNK, 1)
    tok = lambda i, c: (i, c, 0)
    if moe_y is None:
        ya = yb = jnp.zeros((1, rows, PACK_W), I32)
        y_spec = _const_spec((1, rows, PACK_W))
    else:
        ya, yb = (t.reshape(b, l, PACK_W) for t in moe_y)
        y_spec = pl.BlockSpec((1, rows, PACK_W), tok)
    has_prev = prev_out is not None
    n_in = 18
    y, tails, states = pl.pallas_call(
        functools.partial(_mixer_prompt_kernel, rows=rows, moe_in=moe_y is not None,
                          precise_tail=precise_tail, layer=layer, has_prev=has_prev),
        grid=(b, l // rows),
        in_specs=[pl.BlockSpec((1, rows, D_MODEL), tok), y_spec, y_spec,
                  pl.BlockSpec((rows, RET_HEAD_DIM), lambda i, c: (c, 0)),
                  pl.BlockSpec((rows, RET_HEAD_DIM), lambda i, c: (c, 0)),
                  *_mixer_weight_specs(layer, True), *_DECAY_SPECS,
                  *([_ANY_SPEC, _ANY_SPEC] if has_prev else [])],
        out_specs=[pl.BlockSpec((1, rows, D_MODEL), tok),
                   _chain_out_spec((1, POOL_HIST, POOL_WIDTH), depth, layer, has_prev),
                   _chain_out_spec((1, RET_HEADS, RET_HEAD_DIM, RET_HEAD_DIM), depth, layer, has_prev)],
        out_shape=[jax.ShapeDtypeStruct(x.shape, F32),
                   jax.ShapeDtypeStruct((depth, b, POOL_HIST, POOL_WIDTH), F32),
                   jax.ShapeDtypeStruct((depth, b, RET_HEADS, RET_HEAD_DIM, RET_HEAD_DIM), F32)],
        input_output_aliases={n_in: 1, n_in + 1: 2} if has_prev else {},
        scratch_shapes=[pltpu.VMEM((1, POOL_HIST + rows, POOL_WIDTH), F32),
                        pltpu.VMEM((RET_HEADS, RET_HEAD_DIM, RET_HEAD_DIM), F32),
                        pltpu.VMEM((rows, RET_WIDTH), F32)],
        compiler_params=pltpu.CompilerParams(
            dimension_semantics=("arbitrary", "arbitrary"), vmem_limit_bytes=VMEM_LIMIT),
        name="mixer_prompt",
    )(x, ya, yb, cos2, sin2, *weights, *decay, *(prev_out if has_prev else ()))
    return y, (tails, states)


def _mixer_sample(x, pool_prev, s0, weights, prev_out, layer, after):
    b, l, _ = x.shape
    depth = s0.shape[0]
    ns = SAMPLE_SEQS
    assert ns * l == CHUNK and b % ns == 0
    cos2, sin2 = _rope_tables(PAST_LEN + np.arange(l))
    cos2, sin2 = np.tile(cos2, (ns, 1)), np.tile(sin2, (ns, 1))
    decay = _decay_consts(l, ns)
    seq3 = lambda i: (i, 0, 0)
    state_block = (ns, RET_HEADS, RET_HEAD_DIM, RET_HEAD_DIM)
    tail_block = (ns, POOL_HIST, POOL_WIDTH)
    has_prev = prev_out is not None
    n_in = 16
    y, tails, states = pl.pallas_call(
        functools.partial(_mixer_sample_kernel, seq_len=l, layer=layer, has_prev=has_prev),
        grid=(b // ns,),
        in_specs=[pl.BlockSpec((ns, l, D_MODEL), seq3),
                  pl.BlockSpec((None,) + tail_block, lambda i: (layer, i, 0, 0)),
                  pl.BlockSpec((None,) + state_block, lambda i: (layer, i, 0, 0, 0)),
                  _const_spec((CHUNK, RET_HEAD_DIM)), _const_spec((CHUNK, RET_HEAD_DIM)),
                  *_mixer_weight_specs(layer, False), *_DECAY_SPECS, _ANY_SPEC,
                  *([_ANY_SPEC, _ANY_SPEC] if has_prev else [])],
        out_specs=[pl.BlockSpec((ns, l, D_MODEL), seq3),
                   _chain_out_spec(tail_block, depth, layer, has_prev),
                   _chain_out_spec(state_block, depth, layer, has_prev)],
        out_shape=[jax.ShapeDtypeStruct(x.shape, F32),
                   jax.ShapeDtypeStruct((depth, b, POOL_HIST, POOL_WIDTH), F32),
                   jax.ShapeDtypeStruct(s0.shape, F32)],
        input_output_aliases={n_in: 1, n_in + 1: 2} if has_prev else {},
        scratch_shapes=[pltpu.VMEM((ns, POOL_HIST + l, POOL_WIDTH), F32),
                        pltpu.VMEM((CHUNK, RET_WIDTH), F32)],
        compiler_params=pltpu.CompilerParams(
            dimension_semantics=("arbitrary",), vmem_limit_bytes=VMEM_LIMIT),
        name="mixer_sample",
    )(x, pool_prev, s0, cos2, sin2, *weights, *decay, after, *(prev_out if has_prev else ()))
    return y, (tails, states)


def _router_weights(w_rg, b_rg, w_re, b_re):
    depth = w_rg.shape[0]
    gap = EXPERT_LANE0 - N_EXPERT_GROUPS
    rest = LANES - EXPERT_LANE0 - N_EXPERTS
    wr = jnp.concatenate([w_rg, jnp.zeros((depth, D_MODEL, gap), F32), w_re,
                          jnp.zeros((depth, D_MODEL, rest), F32)], axis=-1)
    br = jnp.concatenate([b_rg, jnp.zeros((depth, gap), F32), b_re, jnp.zeros((depth, rest), F32)],
                         axis=-1).reshape(depth, 1, LANES)
    wr_hi, wr_lo = _split_weight(wr)
    return jnp.concatenate([wr_hi, wr_lo], axis=-1), wr_hi, br


def _router_specs(layer):
    return [_layer_spec((None, D_MODEL, 2 * LANES), layer), _layer_spec((None, D_MODEL, LANES), layer),
            _layer_spec((None, 1, LANES), layer)]


def _sc_mesh():
    return plsc.VectorSubcoreMesh(core_axis_name="core", subcore_axis_name="subcore",
                                  num_cores=SC_CORES, num_subcores=SC_SUBCORES)


def _sc_params():
    params = pltpu.CompilerParams()
    if "needs_layout_passes" in pltpu.CompilerParams.__dataclass_fields__:
        params = dataclasses.replace(params, needs_layout_passes=False)
    return params


def _sc_gather(tables, idx):
    n = idx.shape[0]
    assert n % SC_WINDOW == 0
    nt = len(tables)

    def body(*refs):
        i_hbm = refs[nt]
        for t_hbm, o_hbm in zip(refs[:nt], refs[nt + 1:]):
            def gather_window(i_vmem, o_vmem, t_hbm=t_hbm):
                pltpu.sync_copy(t_hbm.at[i_vmem.at[0]], o_vmem)

            pltpu.emit_pipeline(
                gather_window, grid=(n // SC_WINDOW,),
                in_specs=[pl.BlockSpec((1, SC_WINDOW), lambda i: (0, i))],
                out_specs=[pl.BlockSpec((SC_WINDOW, t_hbm.shape[1]), lambda i: (i, 0))],
                core_axis_name=("core", "subcore"),
                dimension_semantics=(pltpu.PARALLEL,),
            )(i_hbm, o_hbm)

    out_type = tuple(jax.ShapeDtypeStruct((n, t.shape[1]), t.dtype) for t in tables)
    return pl.kernel(body, out_type=out_type, mesh=_sc_mesh(), name="sc_gather")(
        *tables, idx.reshape(1, n))


def _sc_slots(cls, rank, starts, n_slots):
    t = cls.shape[0]
    workers = SC_CORES * SC_SUBCORES
    slot_per, tok_per = n_slots // workers, t // workers
    assert n_slots % (workers * SC_LANES) == 0 and t % (workers * SC_LANES) == 0 and t & (t - 1) == 0

    def body(cls_hbm, rank_hbm, starts_hbm, pos_hbm, slot_hbm, cls_v, rank_v, starts_v, pos_v, slot_v):
        wid = lax.axis_index("subcore") * SC_CORES + lax.axis_index("core")
        pltpu.sync_copy(cls_hbm, cls_v)
        pltpu.sync_copy(rank_hbm, rank_v)
        pltpu.sync_copy(starts_hbm, starts_v)

        @pl.loop(0, n_slots, step=SC_LANES)
        def _(i):
            slot_v[pl.ds(i, SC_LANES)] = (lax.iota(I32, SC_LANES) + i) & (t - 1)

        @pl.loop(0, t, step=SC_LANES)
        def _(i):
            at = pl.ds(i, SC_LANES)
            pos = plsc.load_gather(starts_v, [cls_v[at]]) + rank_v[at]
            pos_v[at] = pos
            plsc.store_scatter(slot_v, [pos], lax.iota(I32, SC_LANES) + i)

        tok_off = pl.multiple_of(wid * tok_per, SC_LANES)
        pltpu.sync_copy(pos_v.at[pl.ds(tok_off, tok_per)], pos_hbm.at[pl.ds(tok_off, tok_per)])
        slot_off = pl.multiple_of(wid * slot_per, SC_LANES)
        pltpu.sync_copy(slot_v.at[pl.ds(slot_off, slot_per)], slot_hbm.at[pl.ds(slot_off, slot_per)])

    return pl.kernel(
        body, mesh=_sc_mesh(), compiler_params=_sc_params(), name="sc_slots",
        out_type=(jax.ShapeDtypeStruct((t,), I32), jax.ShapeDtypeStruct((n_slots,), I32)),
        scratch_types=[pltpu.VMEM((t,), I32), pltpu.VMEM((t,), I32), pltpu.VMEM((CLASS_ROWS,), I32),
                       pltpu.VMEM((t,), I32), pltpu.VMEM((n_slots,), I32)],
    )(cls, rank, starts)


def _moe_route(x, norm_g, router, layer):
    t = x.shape[0]
    tok = lambda i: (i, 0)
    return pl.pallas_call(
        _moe_route_kernel,
        grid=(t // ROUTE_TILE,),
        in_specs=[pl.BlockSpec((ROUTE_TILE, D_MODEL), tok), _layer_spec((None, 1, D_MODEL), layer),
                  *_router_specs(layer)],
        out_specs=[pl.BlockSpec((ROUTE_TILE, PACK_W), tok), pl.BlockSpec((ROUTE_TILE, PACK_W), tok),
                   pl.BlockSpec((ROUTE_TILE, LANES), tok),
                   pl.BlockSpec((SUBLANES, ROUTE_TILE), lambda i: (0, i)),
                   _const_spec((CLASS_ROWS, LANES))],
        out_shape=[jax.ShapeDtypeStruct((t, PACK_W), I32), jax.ShapeDtypeStruct((t, PACK_W), I32),
                   jax.ShapeDtypeStruct((t, LANES), F32), jax.ShapeDtypeStruct((SUBLANES, t), I32),
                   jax.ShapeDtypeStruct((CLASS_ROWS, LANES), F32)],
        scratch_shapes=[pltpu.VMEM((CLASS_ROWS, LANES), F32)],
        compiler_params=pltpu.CompilerParams(
            dimension_semantics=("arbitrary",), vmem_limit_bytes=VMEM_LIMIT),
        name="moe_route",
    )(x, norm_g, *router)


def _moe_experts(routed, w_gate, w_up, w_down, layer, after):
    ha, hb, gates, idx, counts = routed
    t = ha.shape[0]
    n_slots = t + N_CLASSES * PAIR_TILE
    n_tiles = n_slots // PAIR_TILE

    cnt = counts[:, 0].astype(I32)
    padded = (cnt + PAIR_TILE - 1) // PAIR_TILE * PAIR_TILE
    ends = jnp.cumsum(padded)
    starts = ends - padded
    tile_cls = jnp.minimum(
        jnp.sum(ends[None, :N_CLASSES] <= (jnp.arange(n_tiles, dtype=I32) * PAIR_TILE)[:, None], axis=1),
        N_CLASSES - 1).astype(I32)
    first = (tile_cls // PAIRS_PER_GROUP) * EXPERTS_PER_GROUP
    tile_ea = first + jnp.asarray(PAIR_LO, I32)[tile_cls % PAIRS_PER_GROUP]
    tile_eb = first + jnp.asarray(PAIR_HI, I32)[tile_cls % PAIRS_PER_GROUP]
    used = (ends[N_CLASSES - 1:N_CLASSES] // PAIR_TILE).astype(I32)

    pos, slot_tok = _sc_slots(idx[0], idx[1], starts, n_slots)
    hsa, hsb, gate_s = _sc_gather((ha, hb, gates), slot_tok)

    row = lambda i, ea, eb, nu: (jnp.minimum(i, nu[0] - 1), 0)
    w_spec = lambda shape, which: pl.BlockSpec(
        (None, 1) + shape, lambda i, ea, eb, nu: (layer, (ea, eb)[which][i], 0, 0))
    gu_shape, d_shape = (D_MODEL, D_EXPERT), (D_EXPERT, D_MODEL)
    ysa, ysb = pl.pallas_call(
        _moe_pair_kernel,
        grid_spec=pltpu.PrefetchScalarGridSpec(
            num_scalar_prefetch=3, grid=(n_tiles,),
            in_specs=[pl.BlockSpec((PAIR_TILE, PACK_W), row), pl.BlockSpec((PAIR_TILE, PACK_W), row),
                      pl.BlockSpec((PAIR_TILE, LANES), row),
                      w_spec(gu_shape, 0), w_spec(gu_shape, 0), w_spec(d_shape, 0),
                      w_spec(gu_shape, 1), w_spec(gu_shape, 1), w_spec(d_shape, 1), _ANY_SPEC],
            out_specs=[pl.BlockSpec((PAIR_TILE, PACK_W), row), pl.BlockSpec((PAIR_TILE, PACK_W), row)],
            scratch_shapes=[pltpu.VMEM((2, D_MODEL, 2 * D_EXPERT), BF16),
                            pltpu.VMEM((2, D_EXPERT, D_MODEL), BF16)]),
        out_shape=[jax.ShapeDtypeStruct((n_slots, PACK_W), I32),
                   jax.ShapeDtypeStruct((n_slots, PACK_W), I32)],
        compiler_params=pltpu.CompilerParams(
            dimension_semantics=("arbitrary",), vmem_limit_bytes=VMEM_LIMIT),
        name="moe_pair",
    )(tile_ea, tile_eb, used, hsa, hsb, gate_s, w_gate, w_up, w_down, w_gate, w_up, w_down, after)
    return _sc_gather((ysa, ysb), pos)


def _final_norm(x, moe_y, g, rows):
    t = x.shape[0]
    tok = lambda i: (i, 0)
    return pl.pallas_call(
        _final_norm_kernel,
        grid=(t // rows,),
        in_specs=[pl.BlockSpec((rows, D_MODEL), tok), pl.BlockSpec((rows, PACK_W), tok),
                  pl.BlockSpec((rows, PACK_W), tok), _const_spec((1, D_MODEL))],
        out_specs=pl.BlockSpec((rows, D_MODEL), tok),
        out_shape=jax.ShapeDtypeStruct(x.shape, F32),
        compiler_params=pltpu.CompilerParams(
            dimension_semantics=("arbitrary",), vmem_limit_bytes=VMEM_LIMIT),
        name="final_norm",
    )(x, *moe_y, g.reshape(1, D_MODEL))


def _moe_dense(x, norm_g, router, w_gate, w_up, w_down, norm_final, layer, final_norm, rows):
    t = x.shape[0]
    assert t % rows == 0
    tok = lambda i, e: (i, 0)
    w_spec = lambda shape: pl.BlockSpec((None, 1) + shape, lambda i, e: (layer, e, 0, 0))
    return pl.pallas_call(
        functools.partial(_moe_dense_kernel, final_norm=final_norm),
        grid=(t // rows, N_EXPERTS),
        in_specs=[pl.BlockSpec((rows, D_MODEL), tok), _layer_spec((None, 1, D_MODEL), layer),
                  *_router_specs(layer),
                  w_spec((D_MODEL, D_EXPERT)), w_spec((D_MODEL, D_EXPERT)), w_spec((D_EXPERT, D_MODEL)),
                  _const_spec((1, D_MODEL))],
        out_specs=pl.BlockSpec((rows, D_MODEL), tok),
        out_shape=jax.ShapeDtypeStruct(x.shape, F32),
        scratch_shapes=[pltpu.VMEM((rows, D_MODEL), BF16),
                        pltpu.VMEM((rows, LANES), F32),
                        pltpu.VMEM((rows, D_MODEL), F32)],
        compiler_params=pltpu.CompilerParams(
            dimension_semantics=("arbitrary", "arbitrary"), vmem_limit_bytes=VMEM_LIMIT),
        name="moe_dense",
    )(x, norm_g, *router, w_gate, w_up, w_down, norm_final.reshape(1, D_MODEL))


def kernel(x_prompt, x_sample, cache_pool, state_ret, norm_mix, w_in, w_pool, pool_scale, ret_gn, w_out, norm_ffn, w_router_group, b_router_group, w_router_expert, b_router_expert, w_gate, w_up, w_down, norm_final):
    depth = norm_mix.shape[0]
    row = lambda a: a.reshape(depth, 1, a.shape[-1])
    mix_split = (row(norm_mix), *_split_weight(w_in), *_split_weight(w_pool), row(pool_scale),
                 row(ret_gn), *_split_weight(w_out))
    mix_hi = tuple(mix_split[i] for i in (0, 1, 3, 5, 6, 7))
    router = _router_weights(w_router_group, b_router_group, w_router_expert, b_router_expert)
    norm_ffn = row(norm_ffn)
    pool_prev = jnp.pad(cache_pool, ((0, 0), (0, 0), (POOL_HIST - POOL_BUF, 0), (0, 0)))

    yp, ys = x_prompt, x_sample
    moe_p = None
    out_p = out_s = None
    for l in range(depth):
        yp, out_p = _mixer_prompt(yp, moe_p, mix_split, out_p, l, rows=512,
                                  precise_tail=PRECISE_TAIL_STEPS if l < depth - 1 else 0)
        routed = _moe_route(yp.reshape(-1, D_MODEL), norm_ffn, router, l)
        ys, out_s = _mixer_sample(ys, pool_prev, state_ret, mix_hi, out_s, l, after=routed[-1])
        ys = _moe_dense(ys.reshape(-1, D_MODEL), norm_ffn, router, w_gate, w_up, w_down, norm_final,
                        l, l == depth - 1, rows=1024).reshape(ys.shape)
        moe_p = _moe_experts(routed, w_gate, w_up, w_down, l, after=ys)
    yp = _final_norm(yp.reshape(-1, D_MODEL), moe_p, norm_final, rows=1024).reshape(yp.shape)
    tail = POOL_HIST - POOL_BUF
    return (yp, ys, out_p[0][:, :, tail:], out_p[1], out_s[0][:, :, tail:], out_s[1])
```

```python
import dataclasses
import functools

import jax
import jax.numpy as jnp
import numpy as np
from jax import lax
from jax.experimental import pallas as pl
from jax.experimental.pallas import tpu as pltpu
from jax.experimental.pallas import tpu_sc as plsc

F32 = jnp.float32
BF16 = jnp.bfloat16
I32 = jnp.int32

D_MODEL = 1024
POOL_WIDTH = 512
POOL_WINDOWS = (2, 4, 8, 16)
POOL_GROUP_DIM = 128
POOL_BUF = 15
POOL_HIST = 16
RET_WIDTH = 512
RET_HEADS = 4
RET_HEAD_DIM = 128
RET_CHUNK = 128
ROPE_BASE = 10000.0
IN_WIDTH = POOL_WIDTH + 4 * RET_WIDTH
N_EXPERT_GROUPS = 4
EXPERTS_PER_GROUP = 4
N_EXPERTS = 16
D_EXPERT = 256
RMS_EPS = 1e-6
GN_EPS = 1e-5
PAST_LEN = 16384

LANES = 128
SUBLANES = 8
EXPERT_LANE0 = 8
PAIRS_PER_GROUP = 6
N_CLASSES = N_EXPERT_GROUPS * PAIRS_PER_GROUP
CLASS_ROWS = 32
PAIR_LO = (0, 0, 0, 1, 1, 2)
PAIR_HI = (1, 2, 3, 2, 3, 3)
GATE_LO, GATE_HI = 0, 1
ROUTE_TILE = 1024
COUNT_BLOCK = 256
PAIR_TILE = 256
PACK_W = D_MODEL // 4
SC_CORES, SC_SUBCORES, SC_LANES = 2, 16, 16
SC_WINDOW = 128
SC_UNROLL = 8
PRECISE_TAIL_STEPS = 1
CHUNK = 128
SAMPLE_SEQS = 16
VMEM_LIMIT = 56 * 1024 * 1024


def _dot(a, b):
    return jnp.dot(a, b, preferred_element_type=F32)


def _dot_nt(a, b):
    return lax.dot_general(a, b, (((1,), (1,)), ((), ())), preferred_element_type=F32)


def _bf16_round_bits(u):
    return (u + jnp.uint32(0x7FFF) + ((u >> 16) & jnp.uint32(1))) & jnp.uint32(0xFFFF0000)


def _split(a):
    hi = pltpu.bitcast(_bf16_round_bits(pltpu.bitcast(a, jnp.uint32)), F32)
    return hi.astype(BF16), (a - hi).astype(BF16)


def _mm(a, b, precise, nt=False):
    dot = _dot_nt if nt else _dot
    if precise:
        b_hi, b_lo = b if isinstance(b, tuple) else _split(b)
        a_hi, a_lo = _split(a)
        return dot(a_hi, b_hi) + dot(a_lo, b_hi) + dot(a_hi, b_lo)
    return dot(a.astype(BF16), b[0] if isinstance(b, tuple) else b.astype(BF16))


def _rmsnorm(x, g):
    ms = jnp.mean(x * x, axis=-1, keepdims=True)
    return x * lax.rsqrt(ms + RMS_EPS) * g


def _pool_mix(ubuf, rows, t_first, n_prev, wpool_refs, pscale, precise=False):
    ns = ubuf.shape[0]
    t = t_first + lax.broadcasted_iota(I32, (1, rows, POOL_GROUP_DIM), 1)
    outs = []
    for j, w in enumerate(POOL_WINDOWS):
        lanes = slice(j * POOL_GROUP_DIM, (j + 1) * POOL_GROUP_DIM)
        uj = ubuf[:, POOL_HIST:POOL_HIST + rows, lanes]
        acc = uj
        for i in range(1, w):
            acc = acc + ubuf[:, POOL_HIST - i:POOL_HIST - i + rows, lanes]
        cnt = jnp.minimum(w, n_prev + t + 1).astype(F32)
        d = (acc / cnt - uj).reshape(ns * rows, POOL_GROUP_DIM)
        outs.append(_mm(d, tuple(w[j] for w in wpool_refs), precise))
    return jnp.concatenate(outs, axis=-1) * pscale


def _rope(xh, cos2, sin2):
    return xh * cos2 + pltpu.roll(xh, RET_HEAD_DIM // 2, 1) * sin2


def _group_norm(o):
    mu = jnp.mean(o, axis=-1, keepdims=True)
    c = o - mu
    var = jnp.mean(c * c, axis=-1, keepdims=True)
    return c * lax.rsqrt(var + GN_EPS)


def _silu(x):
    return x * (1.0 / (1.0 + jnp.exp(-x)))


def _head(a, h):
    return a[:, h * RET_HEAD_DIM:(h + 1) * RET_HEAD_DIM]


def _qkvg(z):
    p, r = POOL_WIDTH, RET_WIDTH
    return z[:, p:p + r], z[:, p + r:p + 2 * r], z[:, p + 2 * r:p + 3 * r], z[:, p + 3 * r:p + 4 * r]


def _pack_bf16_pair(a, b):
    ua = pltpu.bitcast(a.astype(BF16).astype(F32), jnp.uint32)
    ub = pltpu.bitcast(b.astype(BF16).astype(F32), jnp.uint32)
    return pltpu.bitcast((ua >> 16) | (ub & jnp.uint32(0xFFFF0000)), I32)


def _unpack_bf16_pair(w):
    u = pltpu.bitcast(w, jnp.uint32)
    return pltpu.bitcast(u << 16, F32), pltpu.bitcast(u & jnp.uint32(0xFFFF0000), F32)


def _pack_row(y):
    q = PACK_W
    return _pack_bf16_pair(y[:, 0:q], y[:, q:2 * q]), _pack_bf16_pair(y[:, 2 * q:3 * q], y[:, 3 * q:])


def _unpack_row(wa, wb):
    return jnp.concatenate([*_unpack_bf16_pair(wa), *_unpack_bf16_pair(wb)], axis=-1)


def _zero_other_layers(ref, layer):
    for j in range(ref.shape[0]):
        if j != layer:
            ref[j] = jnp.zeros(ref.shape[1:], ref.dtype)


def _mixer_prompt_kernel(*refs, rows, moe_in, precise_tail, layer, has_prev):
    (x_ref, ya_ref, yb_ref, cos_ref, sin_ref, ng_ref, win_hi_ref, win_lo_ref, wpool_hi_ref,
     wpool_lo_ref, pscale_ref, gn_ref, wout_hi_ref, wout_lo_ref, dmat_ref, qdec_ref, kdec_ref,
     cdec_ref) = refs[:18]
    y_ref, tail_ref, sfin_ref, ubuf, s_scr, ret_scr = refs[18 + 2 * has_prev:]
    c = pl.program_id(1)
    if not has_prev:
        _zero_other_layers(tail_ref, layer)
        _zero_other_layers(sfin_ref, layer)
        tail_ref, sfin_ref = tail_ref.at[layer], sfin_ref.at[layer]

    @pl.when(c == 0)
    def _():
        ubuf[:, 0:POOL_HIST, :] = jnp.zeros((1, POOL_HIST, POOL_WIDTH), F32)
        s_scr[...] = jnp.zeros_like(s_scr)

    def step(precise):
        x = x_ref[0]
        if moe_in:
            x = x + _unpack_row(ya_ref[0], yb_ref[0])
        hn = _rmsnorm(x, ng_ref[...])
        z = _mm(hn, (win_hi_ref[...], win_lo_ref[...]), precise)
        ubuf[0, POOL_HIST:POOL_HIST + rows, :] = z[:, :POOL_WIDTH]
        pool_out = _pool_mix(ubuf, rows, c * rows, 0, (wpool_hi_ref, wpool_lo_ref),
                             pscale_ref[...], precise)
        tail_ref[...] = ubuf[:, rows + POOL_HIST - POOL_BUF:rows + POOL_HIST, :]
        ubuf[:, 0:POOL_HIST, :] = ubuf[:, rows:rows + POOL_HIST, :]

        q_all, k_all, v_all, g_all = _qkvg(z)
        scale = RET_HEAD_DIM ** -0.5
        for ci in range(rows // CHUNK):
            rs = slice(ci * CHUNK, (ci + 1) * CHUNK)
            cos2 = cos_ref[rs, :]
            sin2 = sin_ref[rs, :]
            for h in range(RET_HEADS):
                q = _rope(_head(q_all, h)[rs], cos2, sin2)
                k = _rope(_head(k_all, h)[rs], cos2, sin2) * scale
                v = _split(_head(v_all, h)[rs]) if precise else _head(v_all, h)[rs]
                s_old = s_scr[h]
                scores = _mm(q, k, precise, nt=True) * dmat_ref[h]
                qd = q * _head(qdec_ref[...], h)
                o = _mm(scores, v, precise) + _mm(qd, s_old, precise)
                kd = k * _head(kdec_ref[...], h)
                s_scr[h] = s_old * cdec_ref[h] + _mm(kd.T, v, precise)
                on = _group_norm(o) * _head(gn_ref[...], h)
                ret_scr[rs, h * RET_HEAD_DIM:(h + 1) * RET_HEAD_DIM] = _silu(_head(g_all, h)[rs]) * on

        mix_in = jnp.concatenate([pool_out, ret_scr[...]], axis=-1)
        y_ref[0] = x + _mm(mix_in, (wout_hi_ref[...], wout_lo_ref[...]), precise)
        sfin_ref[0] = s_scr[...]

    if precise_tail:
        first_precise = pl.num_programs(1) - precise_tail
        pl.when(c >= first_precise)(lambda: step(True))
        pl.when(c < first_precise)(lambda: step(False))
    else:
        step(False)


def _mixer_sample_kernel(*refs, seq_len, layer, has_prev):
    (x_ref, prev_ref, s0_ref, cos_ref, sin_ref, ng_ref, win_ref, wpool_ref, pscale_ref, gn_ref,
     wout_ref, dmat_ref, qdec_ref, kdec_ref, cdec_ref) = refs[:15]
    y_ref, tail_ref, sfin_ref, ubuf, ret_scr = refs[16 + 2 * has_prev:]
    if not has_prev:
        _zero_other_layers(tail_ref, layer)
        _zero_other_layers(sfin_ref, layer)
        tail_ref, sfin_ref = tail_ref.at[layer], sfin_ref.at[layer]
    ns = SAMPLE_SEQS
    rows = ns * seq_len
    x = x_ref[...].reshape(rows, D_MODEL)
    hn = _rmsnorm(x, ng_ref[...]).astype(BF16)
    z = _dot(hn, win_ref[...])
    ubuf[:, POOL_HIST - POOL_BUF:POOL_HIST, :] = prev_ref[...]
    ubuf[:, POOL_HIST:POOL_HIST + seq_len, :] = z[:, :POOL_WIDTH].reshape(ns, seq_len, POOL_WIDTH)
    pool_out = _pool_mix(ubuf, seq_len, 0, POOL_BUF, (wpool_ref,), pscale_ref[...])
    tail_ref[...] = ubuf[:, seq_len + POOL_HIST - POOL_BUF:seq_len + POOL_HIST, :]

    q_all, k_all, v_all, g_all = _qkvg(z)
    scale = RET_HEAD_DIM ** -0.5
    cos2 = cos_ref[...]
    sin2 = sin_ref[...]
    tok_seq = lax.broadcasted_iota(I32, (RET_HEAD_DIM, rows), 1) // seq_len
    for h in range(RET_HEADS):
        q = _rope(_head(q_all, h), cos2, sin2)
        k = _rope(_head(k_all, h), cos2, sin2) * scale
        vb = _head(v_all, h).astype(BF16)
        s_old = s0_ref[:, h]
        scores = _dot_nt(q.astype(BF16), k.astype(BF16)) * dmat_ref[h]
        qd = (q * _head(qdec_ref[...], h)).astype(BF16).reshape(ns, seq_len, RET_HEAD_DIM)
        o_state = jnp.einsum('bid,bde->bie', qd, s_old.astype(BF16), preferred_element_type=F32)
        o = _dot(scores.astype(BF16), vb) + o_state.reshape(rows, RET_HEAD_DIM)
        kdt = (k * _head(kdec_ref[...], h)).T
        lhs = jnp.concatenate(
            [jnp.where(tok_seq == b, kdt, 0.0).astype(BF16) for b in range(ns)], axis=0)
        upd = _dot(lhs, vb).reshape(ns, RET_HEAD_DIM, RET_HEAD_DIM)
        sfin_ref[:, h] = s_old * cdec_ref[h] + upd
        on = _group_norm(o) * _head(gn_ref[...], h)
        ret_scr[:, h * RET_HEAD_DIM:(h + 1) * RET_HEAD_DIM] = _silu(_head(g_all, h)) * on

    mix_in = jnp.concatenate([pool_out, ret_scr[...]], axis=-1).astype(BF16)
    y_ref[...] = (x + _dot(mix_in, wout_ref[...])).reshape(ns, seq_len, D_MODEL)


def _router_logits(hn, wr_cat_ref, wr_hi_ref, br_ref):
    h_hi, h_lo = _split(hn)
    part = _dot(h_hi, wr_cat_ref[...])
    return part[:, :LANES] + part[:, LANES:] + _dot(h_lo, wr_hi_ref[...]) + br_ref[...]


def _select(group_lg, expert_lg, gidx, eidx, axis):
    red = dict(axis=axis, keepdims=True)
    neg = jnp.float32(-jnp.inf)
    gmax = jnp.max(group_lg, **red)
    g_sel = jnp.min(jnp.where(group_lg == gmax, gidx, N_EXPERT_GROUPS), **red)
    p_sel = 1.0 / jnp.sum(jnp.exp(group_lg - gmax), **red)
    emask = (eidx >> 2) == g_sel
    v1 = jnp.max(jnp.where(emask, expert_lg, neg), **red)
    i1 = jnp.min(jnp.where(emask & (expert_lg == v1), eidx, N_EXPERTS), **red)
    emask2 = emask & (eidx != i1)
    v2 = jnp.max(jnp.where(emask2, expert_lg, neg), **red)
    i2 = jnp.min(jnp.where(emask2 & (expert_lg == v2), eidx, N_EXPERTS), **red)
    e2 = jnp.exp(v2 - v1)
    return g_sel, i1, i2, p_sel / (1.0 + e2), p_sel * e2 / (1.0 + e2)


def _moe_route_kernel(x_ref, ng_ref, wr_cat_ref, wr_hi_ref, br_ref, ha_ref, hb_ref, gate_ref,
                      idx_ref, count_ref, carry_scr):
    i = pl.program_id(0)
    rows = x_ref.shape[0]

    @pl.when(i == 0)
    def _():
        carry_scr[...] = jnp.zeros_like(carry_scr)

    hn = _rmsnorm(x_ref[...], ng_ref[...])
    ha_ref[...], hb_ref[...] = _pack_row(hn)
    lgt = _router_logits(hn, wr_cat_ref, wr_hi_ref, br_ref).T
    neg = jnp.float32(-jnp.inf)
    gidx = lax.broadcasted_iota(I32, (SUBLANES, rows), 0)
    eidx = lax.broadcasted_iota(I32, (N_EXPERTS, rows), 0)
    group_lg = jnp.where(gidx < N_EXPERT_GROUPS, lgt[0:SUBLANES], neg)
    expert_lg = lgt[EXPERT_LANE0:EXPERT_LANE0 + N_EXPERTS]
    g_sel, i1, i2, w1, w2 = _select(group_lg, expert_lg, gidx, eidx, 0)
    lo = jnp.minimum(i1, i2) - g_sel * EXPERTS_PER_GROUP
    hi = jnp.maximum(i1, i2) - g_sel * EXPERTS_PER_GROUP
    cls = g_sel * PAIRS_PER_GROUP + ((lo * (7 - lo)) >> 1) + (hi - lo - 1)
    w_lo = jnp.where(i1 < i2, w1, w2)
    w_hi = jnp.where(i1 < i2, w2, w1)

    crow = lax.broadcasted_iota(I32, (CLASS_ROWS, rows), 0)
    onehot = jnp.where(crow == cls, 1.0, 0.0)
    n_blk = rows // COUNT_BLOCK
    blocks = [onehot[:, j * COUNT_BLOCK:(j + 1) * COUNT_BLOCK] for j in range(n_blk)]
    r = lax.broadcasted_iota(I32, (COUNT_BLOCK, COUNT_BLOCK), 0)
    c = lax.broadcasted_iota(I32, (COUNT_BLOCK, COUNT_BLOCK), 1)
    upper = jnp.where(r < c, 1.0, 0.0).astype(BF16)
    within = _dot(jnp.concatenate(blocks, axis=0).astype(BF16), upper)
    carry = carry_scr[:, 0:1]
    ranks = []
    for j in range(n_blk):
        before = within[j * CLASS_ROWS:(j + 1) * CLASS_ROWS] + carry
        ranks.append(jnp.sum(blocks[j] * before, axis=0, keepdims=True))
        carry = carry + jnp.sum(blocks[j], axis=1, keepdims=True)
    rank = jnp.concatenate(ranks, axis=1)
    carry_scr[...] = jnp.broadcast_to(carry, carry_scr.shape)
    count_ref[...] = carry_scr[...]

    row8 = lax.broadcasted_iota(I32, (SUBLANES, rows), 0)
    idx_ref[...] = jnp.where(row8 == 0, cls, jnp.where(row8 == 1, rank.astype(I32), 0))
    rowl = lax.broadcasted_iota(I32, (LANES, rows), 0)
    gate_ref[...] = jnp.where(rowl == GATE_LO, w_lo, jnp.where(rowl == GATE_HI, w_hi, 0.0)).T


def _moe_pair_kernel(ea_ref, eb_ref, used_ref, ha_ref, hb_ref, gate_ref, wg_a_ref, wu_a_ref,
                     wd_a_ref, wg_b_ref, wu_b_ref, wd_b_ref, after_ref, ya_ref, yb_ref, wgu_scr,
                     wd_scr):
    del after_ref
    i = pl.program_id(0)
    prev = jnp.maximum(i - 1, 0)
    new_class = (i == 0) | (ea_ref[i] != ea_ref[prev]) | (eb_ref[i] != eb_ref[prev])

    @pl.when(new_class)
    def _():
        for s, (wg_ref, wu_ref, wd_ref) in enumerate(((wg_a_ref, wu_a_ref, wd_a_ref),
                                                      (wg_b_ref, wu_b_ref, wd_b_ref))):
            wgu_scr[s, :, :D_EXPERT] = wg_ref[0].astype(BF16)
            wgu_scr[s, :, D_EXPERT:] = wu_ref[0].astype(BF16)
            wd_scr[s] = wd_ref[0].astype(BF16)

    @pl.when(i < used_ref[0])
    def _():
        h = _unpack_row(ha_ref[...], hb_ref[...]).astype(BF16)
        gates = gate_ref[...]
        y = None
        for s, lane in enumerate((GATE_LO, GATE_HI)):
            gu = _dot(h, wgu_scr[s])
            act = _silu(gu[:, :D_EXPERT]) * gu[:, D_EXPERT:]
            part = _dot((act * gates[:, lane:lane + 1]).astype(BF16), wd_scr[s])
            y = part if y is None else y + part
        ya_ref[...], yb_ref[...] = _pack_row(y)


def _final_norm_kernel(x_ref, ya_ref, yb_ref, g_ref, o_ref):
    o_ref[...] = _rmsnorm(x_ref[...] + _unpack_row(ya_ref[...], yb_ref[...]), g_ref[...])


def _moe_dense_kernel(x_ref, ng_ref, wr_cat_ref, wr_hi_ref, br_ref, wg_ref, wu_ref, wd_ref, nf_ref,
                      y_ref, h_scr, gate_scr, acc_scr, *, final_norm):
    e = pl.program_id(1)

    @pl.when(e == 0)
    def _():
        hn = _rmsnorm(x_ref[...], ng_ref[...])
        h_scr[...] = hn.astype(BF16)
        lg = _router_logits(hn, wr_cat_ref, wr_hi_ref, br_ref)
        lane = lax.broadcasted_iota(I32, lg.shape, 1)
        neg = jnp.float32(-jnp.inf)
        group_lg = jnp.where(lane < N_EXPERT_GROUPS, lg, neg)
        is_expert = (lane >= EXPERT_LANE0) & (lane < EXPERT_LANE0 + N_EXPERTS)
        _, i1, i2, w1, w2 = _select(group_lg, jnp.where(is_expert, lg, neg), lane,
                                    jnp.where(is_expert, lane - EXPERT_LANE0, N_EXPERTS), 1)
        gate_scr[...] = (jnp.where(lane == i1 + EXPERT_LANE0, w1, 0.0)
                         + jnp.where(lane == i2 + EXPERT_LANE0, w2, 0.0))
        acc_scr[...] = jnp.zeros_like(acc_scr)

    lane = lax.broadcasted_iota(I32, gate_scr.shape, 1)
    gate = jnp.sum(jnp.where(lane == EXPERT_LANE0 + e, gate_scr[...], 0.0), axis=-1, keepdims=True)
    h = h_scr[...]
    act = _silu(_dot(h, wg_ref[0].astype(BF16))) * _dot(h, wu_ref[0].astype(BF16))
    acc_scr[...] += _dot((act * gate).astype(BF16), wd_ref[0].astype(BF16))

    @pl.when(e == N_EXPERTS - 1)
    def _():
        y = x_ref[...] + acc_scr[...]
        if final_norm:
            y = _rmsnorm(y, nf_ref[...])
        y_ref[...] = y


def _const_spec(shape):
    return pl.BlockSpec(shape, lambda *_: (0,) * len(shape))


def _resident_spec(shape):
    return pl.BlockSpec(shape, lambda *_: (0,) * len(shape), pipeline_mode=pl.Buffered(1))


_ANY_SPEC = pl.BlockSpec(memory_space=pl.ANY)


def _rope_tables(pos):
    half = RET_HEAD_DIM // 2
    inv = np.float32(ROPE_BASE) ** (-np.arange(half, dtype=np.float32) / np.float32(half))
    ang = pos.astype(np.float32)[:, None] * inv[None, :]
    cos, sin = np.cos(ang), np.sin(ang)
    return np.concatenate([cos, cos], -1), np.concatenate([-sin, sin], -1)


def _decay_consts(c, reps):
    f = np.float32
    log_g = np.log(f(1.0) - f(2.0) ** (f(-5.0) - np.arange(RET_HEADS, dtype=f)))
    i = np.arange(c, dtype=f)
    diff = i[:, None] - i[None, :]
    dmat = np.where(diff[None] >= 0, np.exp(np.maximum(diff, f(0))[None] * log_g[:, None, None]), f(0))
    qdec = np.exp((i + f(1))[None, :] * log_g[:, None])
    kdec = np.exp((f(c) - f(1) - i)[None, :] * log_g[:, None])
    cdec = np.exp(f(c) * log_g)
    dmat = np.einsum('ab,hij->haibj', np.eye(reps, dtype=f), dmat).reshape(RET_HEADS, reps * c, reps * c)
    lanes = lambda t: np.repeat(np.tile(t, (1, reps)).T, RET_HEAD_DIM, axis=1)
    cdec = np.broadcast_to(cdec[:, None, None], (RET_HEADS, 1, RET_HEAD_DIM))
    return tuple(np.ascontiguousarray(a, dtype=f) for a in (dmat, lanes(qdec), lanes(kdec), cdec))


def _split_weight(w):
    hi = lax.bitcast_convert_type(_bf16_round_bits(lax.bitcast_convert_type(w, jnp.uint32)), F32)
    return hi.astype(BF16), (w - hi).astype(BF16)


_W_IN_BLOCK = (None, D_MODEL, IN_WIDTH)
_W_POOL_BLOCK = (None, len(POOL_WINDOWS), POOL_GROUP_DIM, POOL_GROUP_DIM)
_W_OUT_BLOCK = (None, D_MODEL, D_MODEL)
_DECAY_SPECS = [
    _const_spec((RET_HEADS, CHUNK, CHUNK)), _const_spec((CHUNK, RET_WIDTH)),
    _const_spec((CHUNK, RET_WIDTH)), _const_spec((RET_HEADS, 1, RET_HEAD_DIM)),
]


def _layer_spec(block, layer, resident=False):
    index_map = lambda *_: (layer,) + (0,) * (len(block) - 1)
    if resident:
        return pl.BlockSpec(block, index_map, pipeline_mode=pl.Buffered(1))
    return pl.BlockSpec(block, index_map)


def _mixer_weight_specs(layer, split):
    n = 2 if split else 1
    row = lambda width: _layer_spec((None, 1, width), layer)
    return [row(D_MODEL), *[_layer_spec(_W_IN_BLOCK, layer, True)] * n,
            *[_layer_spec(_W_POOL_BLOCK, layer)] * n, row(POOL_WIDTH), row(RET_WIDTH),
            *[_layer_spec(_W_OUT_BLOCK, layer, True)] * n]


def _chain_out_spec(block, depth, layer, has_prev):
    zeros = (0,) * (len(block) - 1)
    if has_prev:
        return pl.BlockSpec((None,) + block, lambda i, *_: (layer, i) + zeros)
    return pl.BlockSpec((depth,) + block, lambda i, *_: (0, i) + zeros)


def _mixer_prompt(x, moe_y, weights, prev_out, layer, rows, precise_tail):
    b, l, _ = x.shape
    depth = weights[0].shape[0]
    assert l % rows == 0 and rows % CHUNK == 0
    cos2, sin2 = _rope_tables(np.arange(l))
    decay = _decay_consts(RET_CHUNK, 1)
    tok = lambda i, c: (i, c, 0)
    if moe_y is None:
        ya = yb = jnp.zeros((1, rows, PACK_W), I32)
        y_spec = _const_spec((1, rows, PACK_W))
    else:
        ya, yb = (t.reshape(b, l, PACK_W) for t in moe_y)
        y_spec = pl.BlockSpec((1, rows, PACK_W), tok)
    has_prev = prev_out is not None
    n_in = 18
    y, tails, states = pl.pallas_call(
        functools.partial(_mixer_prompt_kernel, rows=rows, moe_in=moe_y is not None,
                          precise_tail=precise_tail, layer=layer, has_prev=has_prev),
        grid=(b, l // rows),
        in_specs=[pl.BlockSpec((1, rows, D_MODEL), tok), y_spec, y_spec,
                  pl.BlockSpec((rows, RET_HEAD_DIM), lambda i, c: (c, 0)),
                  pl.BlockSpec((rows, RET_HEAD_DIM), lambda i, c: (c, 0)),
                  *_mixer_weight_specs(layer, True), *_DECAY_SPECS,
                  *([_ANY_SPEC, _ANY_SPEC] if has_prev else [])],
        out_specs=[pl.BlockSpec((1, rows, D_MODEL), tok),
                   _chain_out_spec((1, POOL_BUF, POOL_WIDTH), depth, layer, has_prev),
                   _chain_out_spec((1, RET_HEADS, RET_HEAD_DIM, RET_HEAD_DIM), depth, layer, has_prev)],
        out_shape=[jax.ShapeDtypeStruct(x.shape, F32),
                   jax.ShapeDtypeStruct((depth, b, POOL_BUF, POOL_WIDTH), F32),
                   jax.ShapeDtypeStruct((depth, b, RET_HEADS, RET_HEAD_DIM, RET_HEAD_DIM), F32)],
        input_output_aliases={n_in: 1, n_in + 1: 2} if has_prev else {},
        scratch_shapes=[pltpu.VMEM((1, POOL_HIST + rows, POOL_WIDTH), F32),
                        pltpu.VMEM((RET_HEADS, RET_HEAD_DIM, RET_HEAD_DIM), F32),
                        pltpu.VMEM((rows, RET_WIDTH), F32)],
        compiler_params=pltpu.CompilerParams(
            dimension_semantics=("arbitrary", "arbitrary"), vmem_limit_bytes=VMEM_LIMIT),
        name="mixer_prompt",
    )(x, ya, yb, cos2, sin2, *weights, *decay, *(prev_out if has_prev else ()))
    return y, (tails, states)


def _mixer_sample(x, pool_prev, s0, weights, prev_out, layer, after):
    b, l, _ = x.shape
    depth = s0.shape[0]
    ns = SAMPLE_SEQS
    assert ns * l == CHUNK and b % ns == 0
    cos2, sin2 = _rope_tables(PAST_LEN + np.arange(l))
    cos2, sin2 = np.tile(cos2, (ns, 1)), np.tile(sin2, (ns, 1))
    decay = _decay_consts(l, ns)
    seq3 = lambda i: (i, 0, 0)
    state_block = (ns, RET_HEADS, RET_HEAD_DIM, RET_HEAD_DIM)
    tail_block = (ns, POOL_BUF, POOL_WIDTH)
    has_prev = prev_out is not None
    n_in = 16
    y, tails, states = pl.pallas_call(
        functools.partial(_mixer_sample_kernel, seq_len=l, layer=layer, has_prev=has_prev),
        grid=(b // ns,),
        in_specs=[pl.BlockSpec((ns, l, D_MODEL), seq3),
                  pl.BlockSpec((None,) + tail_block, lambda i: (layer, i, 0, 0)),
                  pl.BlockSpec((None,) + state_block, lambda i: (layer, i, 0, 0, 0)),
                  _const_spec((CHUNK, RET_HEAD_DIM)), _const_spec((CHUNK, RET_HEAD_DIM)),
                  *_mixer_weight_specs(layer, False), *_DECAY_SPECS, _ANY_SPEC,
                  *([_ANY_SPEC, _ANY_SPEC] if has_prev else [])],
        out_specs=[pl.BlockSpec((ns, l, D_MODEL), seq3),
                   _chain_out_spec(tail_block, depth, layer, has_prev),
                   _chain_out_spec(state_block, depth, layer, has_prev)],
        out_shape=[jax.ShapeDtypeStruct(x.shape, F32),
                   jax.ShapeDtypeStruct((depth, b, POOL_BUF, POOL_WIDTH), F32),
                   jax.ShapeDtypeStruct(s0.shape, F32)],
        input_output_aliases={n_in: 1, n_in + 1: 2} if has_prev else {},
        scratch_shapes=[pltpu.VMEM((ns, POOL_HIST + l, POOL_WIDTH), F32),
                        pltpu.VMEM((CHUNK, RET_WIDTH), F32)],
        compiler_params=pltpu.CompilerParams(
            dimension_semantics=("arbitrary",), vmem_limit_bytes=VMEM_LIMIT),
        name="mixer_sample",
    )(x, pool_prev, s0, cos2, sin2, *weights, *decay, after, *(prev_out if has_prev else ()))
    return y, (tails, states)


def _router_weights(w_rg, b_rg, w_re, b_re):
    depth = w_rg.shape[0]
    gap = EXPERT_LANE0 - N_EXPERT_GROUPS
    rest = LANES - EXPERT_LANE0 - N_EXPERTS
    wr = jnp.concatenate([w_rg, jnp.zeros((depth, D_MODEL, gap), F32), w_re,
                          jnp.zeros((depth, D_MODEL, rest), F32)], axis=-1)
    br = jnp.concatenate([b_rg, jnp.zeros((depth, gap), F32), b_re, jnp.zeros((depth, rest), F32)],
                         axis=-1).reshape(depth, 1, LANES)
    wr_hi, wr_lo = _split_weight(wr)
    return jnp.concatenate([wr_hi, wr_lo], axis=-1), wr_hi, br


def _router_specs(layer):
    return [_layer_spec((None, D_MODEL, 2 * LANES), layer), _layer_spec((None, D_MODEL, LANES), layer),
            _layer_spec((None, 1, LANES), layer)]


def _sc_mesh():
    return plsc.VectorSubcoreMesh(core_axis_name="core", subcore_axis_name="subcore",
                                  num_cores=SC_CORES, num_subcores=SC_SUBCORES)


def _sc_params():
    params = pltpu.CompilerParams()
    if "needs_layout_passes" in pltpu.CompilerParams.__dataclass_fields__:
        params = dataclasses.replace(params, needs_layout_passes=False)
    return params


def _sc_gather(tables, idx):
    n = idx.shape[0]
    assert n % SC_WINDOW == 0
    nt = len(tables)

    def body(*refs):
        i_hbm = refs[nt]
        for t_hbm, o_hbm in zip(refs[:nt], refs[nt + 1:]):
            def gather_window(i_vmem, o_vmem, t_hbm=t_hbm):
                pltpu.sync_copy(t_hbm.at[i_vmem.at[0]], o_vmem)

            pltpu.emit_pipeline(
                gather_window, grid=(n // SC_WINDOW,),
                in_specs=[pl.BlockSpec((1, SC_WINDOW), lambda i: (0, i))],
                out_specs=[pl.BlockSpec((SC_WINDOW, t_hbm.shape[1]), lambda i: (i, 0))],
                core_axis_name=("core", "subcore"),
                dimension_semantics=(pltpu.PARALLEL,),
            )(i_hbm, o_hbm)

    out_type = tuple(jax.ShapeDtypeStruct((n, t.shape[1]), t.dtype) for t in tables)
    return pl.kernel(body, out_type=out_type, mesh=_sc_mesh(), name="sc_gather")(
        *tables, idx.reshape(1, n))


def _sc_slots(cls, rank, starts, n_slots):
    t = cls.shape[0]
    workers = SC_CORES * SC_SUBCORES
    slot_per, tok_per = n_slots // workers, t // workers
    assert n_slots % (workers * SC_LANES) == 0 and t % (workers * SC_LANES) == 0 and t & (t - 1) == 0
    assert n_slots % (SC_LANES * SC_UNROLL) == 0 and t % (SC_LANES * SC_UNROLL) == 0

    def body(cls_hbm, rank_hbm, starts_hbm, pos_hbm, slot_hbm, cls_v, rank_v, starts_v, pos_v, slot_v):
        wid = lax.axis_index("subcore") * SC_CORES + lax.axis_index("core")
        pltpu.sync_copy(cls_hbm, cls_v)
        pltpu.sync_copy(rank_hbm, rank_v)
        pltpu.sync_copy(starts_hbm, starts_v)

        lane = lax.iota(I32, SC_LANES)
        span = SC_LANES * SC_UNROLL

        @pl.loop(0, n_slots, step=span)
        def _(i):
            for u in range(SC_UNROLL):
                j = i + u * SC_LANES
                slot_v[pl.ds(j, SC_LANES)] = (lane + j) & (t - 1)

        @pl.loop(0, t, step=span)
        def _(i):
            for u in range(SC_UNROLL):
                j = i + u * SC_LANES
                at = pl.ds(j, SC_LANES)
                pos = plsc.load_gather(starts_v, [cls_v[at]]) + rank_v[at]
                pos_v[at] = pos
                plsc.store_scatter(slot_v, [pos], lane + j)

        tok_off = pl.multiple_of(wid * tok_per, SC_LANES)
        pltpu.sync_copy(pos_v.at[pl.ds(tok_off, tok_per)], pos_hbm.at[pl.ds(tok_off, tok_per)])
        slot_off = pl.multiple_of(wid * slot_per, SC_LANES)
        pltpu.sync_copy(slot_v.at[pl.ds(slot_off, slot_per)], slot_hbm.at[pl.ds(slot_off, slot_per)])

    return pl.kernel(
        body, mesh=_sc_mesh(), compiler_params=_sc_params(), name="sc_slots",
        out_type=(jax.ShapeDtypeStruct((t,), I32), jax.ShapeDtypeStruct((n_slots,), I32)),
        scratch_types=[pltpu.VMEM((t,), I32), pltpu.VMEM((t,), I32), pltpu.VMEM((CLASS_ROWS,), I32),
                       pltpu.VMEM((t,), I32), pltpu.VMEM((n_slots,), I32)],
    )(cls, rank, starts)


def _moe_route(x, norm_g, router, layer):
    t = x.shape[0]
    tok = lambda i: (i, 0)
    return pl.pallas_call(
        _moe_route_kernel,
        grid=(t // ROUTE_TILE,),
        in_specs=[pl.BlockSpec((ROUTE_TILE, D_MODEL), tok), _layer_spec((None, 1, D_MODEL), layer),
                  *_router_specs(layer)],
        out_specs=[pl.BlockSpec((ROUTE_TILE, PACK_W), tok), pl.BlockSpec((ROUTE_TILE, PACK_W), tok),
                   pl.BlockSpec((ROUTE_TILE, LANES), tok),
                   pl.BlockSpec((SUBLANES, ROUTE_TILE), lambda i: (0, i)),
                   _const_spec((CLASS_ROWS, LANES))],
        out_shape=[jax.ShapeDtypeStruct((t, PACK_W), I32), jax.ShapeDtypeStruct((t, PACK_W), I32),
                   jax.ShapeDtypeStruct((t, LANES), F32), jax.ShapeDtypeStruct((SUBLANES, t), I32),
                   jax.ShapeDtypeStruct((CLASS_ROWS, LANES), F32)],
        scratch_shapes=[pltpu.VMEM((CLASS_ROWS, LANES), F32)],
        compiler_params=pltpu.CompilerParams(
            dimension_semantics=("arbitrary",), vmem_limit_bytes=VMEM_LIMIT),
        name="moe_route",
    )(x, norm_g, *router)


def _moe_experts(routed, w_gate, w_up, w_down, layer, after):
    ha, hb, gates, idx, counts = routed
    t = ha.shape[0]
    n_slots = t + N_CLASSES * PAIR_TILE
    n_tiles = n_slots // PAIR_TILE

    cnt = counts[:, 0].astype(I32)
    padded = (cnt + PAIR_TILE - 1) // PAIR_TILE * PAIR_TILE
    ends = jnp.cumsum(padded)
    starts = ends - padded
    tile_cls = jnp.minimum(
        jnp.sum(ends[None, :N_CLASSES] <= (jnp.arange(n_tiles, dtype=I32) * PAIR_TILE)[:, None], axis=1),
        N_CLASSES - 1).astype(I32)
    first = (tile_cls // PAIRS_PER_GROUP) * EXPERTS_PER_GROUP
    tile_ea = first + jnp.asarray(PAIR_LO, I32)[tile_cls % PAIRS_PER_GROUP]
    tile_eb = first + jnp.asarray(PAIR_HI, I32)[tile_cls % PAIRS_PER_GROUP]
    used = (ends[N_CLASSES - 1:N_CLASSES] // PAIR_TILE).astype(I32)

    pos, slot_tok = _sc_slots(idx[0], idx[1], starts, n_slots)
    hsa, hsb, gate_s = _sc_gather((ha, hb, gates), slot_tok)

    row = lambda i, ea, eb, nu: (jnp.minimum(i, nu[0] - 1), 0)
    w_spec = lambda shape, which: pl.BlockSpec(
        (None, 1) + shape, lambda i, ea, eb, nu: (layer, (ea, eb)[which][i], 0, 0))
    gu_shape, d_shape = (D_MODEL, D_EXPERT), (D_EXPERT, D_MODEL)
    ysa, ysb = pl.pallas_call(
        _moe_pair_kernel,
        grid_spec=pltpu.PrefetchScalarGridSpec(
            num_scalar_prefetch=3, grid=(n_tiles,),
            in_specs=[pl.BlockSpec((PAIR_TILE, PACK_W), row), pl.BlockSpec((PAIR_TILE, PACK_W), row),
                      pl.BlockSpec((PAIR_TILE, LANES), row),
                      w_spec(gu_shape, 0), w_spec(gu_shape, 0), w_spec(d_shape, 0),
                      w_spec(gu_shape, 1), w_spec(gu_shape, 1), w_spec(d_shape, 1), _ANY_SPEC],
            out_specs=[pl.BlockSpec((PAIR_TILE, PACK_W), row), pl.BlockSpec((PAIR_TILE, PACK_W), row)],
            scratch_shapes=[pltpu.VMEM((2, D_MODEL, 2 * D_EXPERT), BF16),
                            pltpu.VMEM((2, D_EXPERT, D_MODEL), BF16)]),
        out_shape=[jax.ShapeDtypeStruct((n_slots, PACK_W), I32),
                   jax.ShapeDtypeStruct((n_slots, PACK_W), I32)],
        compiler_params=pltpu.CompilerParams(
            dimension_semantics=("arbitrary",), vmem_limit_bytes=VMEM_LIMIT),
        name="moe_pair",
    )(tile_ea, tile_eb, used, hsa, hsb, gate_s, w_gate, w_up, w_down, w_gate, w_up, w_down, after)
    return _sc_gather((ysa, ysb), pos)


def _final_norm(x, moe_y, g, rows):
    t = x.shape[0]
    tok = lambda i: (i, 0)
    return pl.pallas_call(
        _final_norm_kernel,
        grid=(t // rows,),
        in_specs=[pl.BlockSpec((rows, D_MODEL), tok), pl.BlockSpec((rows, PACK_W), tok),
                  pl.BlockSpec((rows, PACK_W), tok), _const_spec((1, D_MODEL))],
        out_specs=pl.BlockSpec((rows, D_MODEL), tok),
        out_shape=jax.ShapeDtypeStruct(x.shape, F32),
        compiler_params=pltpu.CompilerParams(
            dimension_semantics=("arbitrary",), vmem_limit_bytes=VMEM_LIMIT),
        name="final_norm",
    )(x, *moe_y, g.reshape(1, D_MODEL))


def _moe_dense(x, norm_g, router, w_gate, w_up, w_down, norm_final, layer, final_norm, rows):
    t = x.shape[0]
    assert t % rows == 0
    tok = lambda i, e: (i, 0)
    w_spec = lambda shape: pl.BlockSpec((None, 1) + shape, lambda i, e: (layer, e, 0, 0))
    return pl.pallas_call(
        functools.partial(_moe_dense_kernel, final_norm=final_norm),
        grid=(t // rows, N_EXPERTS),
        in_specs=[pl.BlockSpec((rows, D_MODEL), tok), _layer_spec((None, 1, D_MODEL), layer),
                  *_router_specs(layer),
                  w_spec((D_MODEL, D_EXPERT)), w_spec((D_MODEL, D_EXPERT)), w_spec((D_EXPERT, D_MODEL)),
                  _const_spec((1, D_MODEL))],
        out_specs=pl.BlockSpec((rows, D_MODEL), tok),
        out_shape=jax.ShapeDtypeStruct(x.shape, F32),
        scratch_shapes=[pltpu.VMEM((rows, D_MODEL), BF16),
                        pltpu.VMEM((rows, LANES), F32),
                        pltpu.VMEM((rows, D_MODEL), F32)],
        compiler_params=pltpu.CompilerParams(
            dimension_semantics=("arbitrary", "arbitrary"), vmem_limit_bytes=VMEM_LIMIT),
        name="moe_dense",
    )(x, norm_g, *router, w_gate, w_up, w_down, norm_final.reshape(1, D_MODEL))


def kernel(x_prompt, x_sample, cache_pool, state_ret, norm_mix, w_in, w_pool, pool_scale, ret_gn, w_out, norm_ffn, w_router_group, b_router_group, w_router_expert, b_router_expert, w_gate, w_up, w_down, norm_final):
    depth = norm_mix.shape[0]
    row = lambda a: a.reshape(depth, 1, a.shape[-1])
    mix_split = (row(norm_mix), *_split_weight(w_in), *_split_weight(w_pool), row(pool_scale),
                 row(ret_gn), *_split_weight(w_out))
    mix_hi = tuple(mix_split[i] for i in (0, 1, 3, 5, 6, 7))
    router = _router_weights(w_router_group, b_router_group, w_router_expert, b_router_expert)
    norm_ffn = row(norm_ffn)

    yp, ys = x_prompt, x_sample
    moe_p = None
    out_p = out_s = None
    for l in range(depth):
        yp, out_p = _mixer_prompt(yp, moe_p, mix_split, out_p, l, rows=512,
                                  precise_tail=PRECISE_TAIL_STEPS if l < depth - 1 else 0)
        routed = _moe_route(yp.reshape(-1, D_MODEL), norm_ffn, router, l)
        ys, out_s = _mixer_sample(ys, cache_pool, state_ret, mix_hi, out_s, l, after=routed[-1])
        ys = _moe_dense(ys.reshape(-1, D_MODEL), norm_ffn, router, w_gate, w_up, w_down, norm_final,
                        l, l == depth - 1, rows=1024).reshape(ys.shape)
        moe_p = _moe_experts(routed, w_gate, w_up, w_down, l, after=ys)
    yp = _final_norm(yp.reshape(-1, D_MODEL), moe_p, norm_final, rows=1024).reshape(yp.shape)
    return (yp, ys, *out_p, *out_s)
```

```python
import dataclasses
import functools

import jax
import jax.numpy as jnp
import numpy as np
from jax import lax
from jax.experimental import pallas as pl
from jax.experimental.pallas import tpu as pltpu
from jax.experimental.pallas import tpu_sc as plsc

F32 = jnp.float32
BF16 = jnp.bfloat16
I32 = jnp.int32

D_MODEL = 1024
POOL_WIDTH = 512
POOL_WINDOWS = (2, 4, 8, 16)
POOL_GROUP_DIM = 128
POOL_BUF = 15
POOL_HIST = 16
RET_WIDTH = 512
RET_HEADS = 4
RET_HEAD_DIM = 128
RET_CHUNK = 128
ROPE_BASE = 10000.0
IN_WIDTH = POOL_WIDTH + 4 * RET_WIDTH
N_EXPERT_GROUPS = 4
EXPERTS_PER_GROUP = 4
N_EXPERTS = 16
D_EXPERT = 256
RMS_EPS = 1e-6
GN_EPS = 1e-5
PAST_LEN = 16384

LANES = 128
SUBLANES = 8
EXPERT_LANE0 = 8
PAIRS_PER_GROUP = 6
N_CLASSES = N_EXPERT_GROUPS * PAIRS_PER_GROUP
CLASS_ROWS = 32
PAIR_LO = (0, 0, 0, 1, 1, 2)
PAIR_HI = (1, 2, 3, 2, 3, 3)
GATE_LO, GATE_HI = 0, 1
ROUTE_TILE = 1024
COUNT_BLOCK = 256
PAIR_TILE = 256
PACK_W = D_MODEL // 4
SC_CORES, SC_SUBCORES, SC_LANES = 2, 16, 16
SC_WINDOW = 128
SC_UNROLL = 8
PROJ_PIECE = 256
PRECISE_TAIL_STEPS = 1
CHUNK = 128
SAMPLE_SEQS = 16
VMEM_LIMIT = 56 * 1024 * 1024


def _dot(a, b):
    return jnp.dot(a, b, preferred_element_type=F32)


def _dot_nt(a, b):
    return lax.dot_general(a, b, (((1,), (1,)), ((), ())), preferred_element_type=F32)


def _bf16_round_bits(u):
    return (u + jnp.uint32(0x7FFF) + ((u >> 16) & jnp.uint32(1))) & jnp.uint32(0xFFFF0000)


def _split(a):
    hi = pltpu.bitcast(_bf16_round_bits(pltpu.bitcast(a, jnp.uint32)), F32)
    return hi.astype(BF16), (a - hi).astype(BF16)


def _mm(a, b, precise, nt=False):
    dot = _dot_nt if nt else _dot
    if precise:
        b_hi, b_lo = b if isinstance(b, tuple) else _split(b)
        a_hi, a_lo = _split(a)
        return dot(a_hi, b_hi) + dot(a_lo, b_hi) + dot(a_hi, b_lo)
    return dot(a.astype(BF16), b[0] if isinstance(b, tuple) else b.astype(BF16))


def _rmsnorm(x, g):
    ms = jnp.mean(x * x, axis=-1, keepdims=True)
    return x * lax.rsqrt(ms + RMS_EPS) * g


def _pool_mix(ubuf, rows, t_first, n_prev, wpool_refs, pscale, precise=False):
    ns = ubuf.shape[0]
    t = t_first + lax.broadcasted_iota(I32, (1, rows, POOL_GROUP_DIM), 1)
    outs = []
    for j, w in enumerate(POOL_WINDOWS):
        lanes = slice(j * POOL_GROUP_DIM, (j + 1) * POOL_GROUP_DIM)
        uj = ubuf[:, POOL_HIST:POOL_HIST + rows, lanes]
        acc = uj
        for i in range(1, w):
            acc = acc + ubuf[:, POOL_HIST - i:POOL_HIST - i + rows, lanes]
        cnt = jnp.minimum(w, n_prev + t + 1).astype(F32)
        d = (acc / cnt - uj).reshape(ns * rows, POOL_GROUP_DIM)
        outs.append(_mm(d, tuple(w[j] for w in wpool_refs), precise))
    return jnp.concatenate(outs, axis=-1) * pscale


def _rope(xh, cos2, sin2):
    return xh * cos2 + pltpu.roll(xh, RET_HEAD_DIM // 2, 1) * sin2


def _group_norm(o):
    mu = jnp.mean(o, axis=-1, keepdims=True)
    c = o - mu
    var = jnp.mean(c * c, axis=-1, keepdims=True)
    return c * lax.rsqrt(var + GN_EPS)


def _silu(x):
    return x * (1.0 / (1.0 + jnp.exp(-x)))


def _head(a, h):
    return a[:, h * RET_HEAD_DIM:(h + 1) * RET_HEAD_DIM]


def _qkvg(z):
    p, r = POOL_WIDTH, RET_WIDTH
    return z[:, p:p + r], z[:, p + r:p + 2 * r], z[:, p + 2 * r:p + 3 * r], z[:, p + 3 * r:p + 4 * r]


def _pack_bf16_pair(a, b):
    ua = pltpu.bitcast(a.astype(BF16).astype(F32), jnp.uint32)
    ub = pltpu.bitcast(b.astype(BF16).astype(F32), jnp.uint32)
    return pltpu.bitcast((ua >> 16) | (ub & jnp.uint32(0xFFFF0000)), I32)


def _unpack_bf16_pair(w):
    u = pltpu.bitcast(w, jnp.uint32)
    return pltpu.bitcast(u << 16, F32), pltpu.bitcast(u & jnp.uint32(0xFFFF0000), F32)


def _pack_row(y):
    q = PACK_W
    return _pack_bf16_pair(y[:, 0:q], y[:, q:2 * q]), _pack_bf16_pair(y[:, 2 * q:3 * q], y[:, 3 * q:])


def _unpack_row(wa, wb):
    return jnp.concatenate([*_unpack_bf16_pair(wa), *_unpack_bf16_pair(wb)], axis=-1)


def _zero_other_layers(ref, layer):
    for j in range(ref.shape[0]):
        if j != layer:
            ref[j] = jnp.zeros(ref.shape[1:], ref.dtype)


def _mixer_prompt_kernel(*refs, rows, steps, moe_in, precise_tail, layer, has_prev):
    (x_ref, ya_ref, yb_ref, xn_ref, yan_ref, ybn_ref, cos_ref, sin_ref, ng_ref, win_hi_ref,
     win_lo_ref, wpool_hi_ref, wpool_lo_ref, pscale_ref, gn_ref, wout_hi_ref, wout_lo_ref,
     dmat_ref, qdec_ref, kdec_ref, cdec_ref) = refs[:21]
    y_ref, tail_ref, sfin_ref, ubuf, s_scr, ret_scr, z0, z1 = refs[21 + 2 * has_prev:]
    i, c = pl.program_id(0), pl.program_id(1)
    half = rows // 2
    if not has_prev:
        _zero_other_layers(tail_ref, layer)
        _zero_other_layers(sfin_ref, layer)
        tail_ref, sfin_ref = tail_ref.at[layer], sfin_ref.at[layer]

    @pl.when(c == 0)
    def _():
        ubuf[:, 0:POOL_HIST, :] = jnp.zeros((1, POOL_HIST, POOL_WIDTH), F32)
        s_scr[...] = jnp.zeros_like(s_scr)

    def load_x(x_r, ya_r, yb_r, rs):
        x = x_r[0, rs, :]
        if moe_in:
            x = x + _unpack_row(ya_r[0, rs, :], yb_r[0, rs, :])
        return x

    def project(x, z_ref, precise):
        a = _rmsnorm(x, ng_ref[...])
        a = _split(a) if precise else (a.astype(BF16),)
        for j in range(IN_WIDTH // PROJ_PIECE):
            cs = slice(j * PROJ_PIECE, (j + 1) * PROJ_PIECE)
            piece = _dot(a[0], win_hi_ref[:, cs])
            if precise:
                piece = piece + _dot(a[1], win_hi_ref[:, cs]) + _dot(a[0], win_lo_ref[:, cs])
            z_ref[:, cs] = piece
            yield

    def mix_half(x, z_ref, row0, precise, pieces):
        ubuf[0, POOL_HIST:POOL_HIST + half, :] = z_ref[:, :POOL_WIDTH]
        pool_out = _pool_mix(ubuf, half, c * rows + row0, 0, (wpool_hi_ref, wpool_lo_ref),
                             pscale_ref[...], precise)
        if row0:
            tail_ref[...] = ubuf[:, half + POOL_HIST - POOL_BUF:half + POOL_HIST, :]
        ubuf[:, 0:POOL_HIST, :] = ubuf[:, half:half + POOL_HIST, :]
        next(pieces, None)
        scale = RET_HEAD_DIM ** -0.5
        for ci in range(half // CHUNK):
            rs = slice(ci * CHUNK, (ci + 1) * CHUNK)
            cos2 = cos_ref[row0 + ci * CHUNK:row0 + (ci + 1) * CHUNK, :]
            sin2 = sin_ref[row0 + ci * CHUNK:row0 + (ci + 1) * CHUNK, :]
            for h in range(RET_HEADS):
                col = lambda part: slice(POOL_WIDTH + part * RET_WIDTH + h * RET_HEAD_DIM,
                                         POOL_WIDTH + part * RET_WIDTH + (h + 1) * RET_HEAD_DIM)
                q = _rope(z_ref[rs, col(0)], cos2, sin2)
                k = _rope(z_ref[rs, col(1)], cos2, sin2) * scale
                v = _split(z_ref[rs, col(2)]) if precise else z_ref[rs, col(2)]
                s_old = s_scr[h]
                scores = _mm(q, k, precise, nt=True) * dmat_ref[h]
                qd = q * _head(qdec_ref[...], h)
                o = _mm(scores, v, precise) + _mm(qd, s_old, precise)
                kd = k * _head(kdec_ref[...], h)
                s_scr[h] = s_old * cdec_ref[h] + _mm(kd.T, v, precise)
                on = _group_norm(o) * _head(gn_ref[...], h)
                ret_scr[rs, h * RET_HEAD_DIM:(h + 1) * RET_HEAD_DIM] = _silu(z_ref[rs, col(3)]) * on
                next(pieces, None)
        mix_in = jnp.concatenate([pool_out, ret_scr[...]], axis=-1)
        next(pieces, None)
        y_ref[0, row0:row0 + half, :] = x + _mm(mix_in, (wout_hi_ref[...], wout_lo_ref[...]), precise)
        for _ in pieces:
            pass

    first, second = slice(0, half), slice(half, rows)
    is_precise = lambda step: step >= steps - precise_tail

    @pl.when((i == 0) & (c == 0))
    def _():
        for _ in project(load_x(x_ref, ya_ref, yb_ref, first), z0, is_precise(0)):
            pass

    def step(cur, nxt):
        x_first = load_x(x_ref, ya_ref, yb_ref, first)
        x_second = load_x(x_ref, ya_ref, yb_ref, second)
        mix_half(x_first, z0, 0, cur, project(x_second, z1, cur))
        x_next = load_x(xn_ref, yan_ref, ybn_ref, first)
        mix_half(x_second, z1, half, cur, project(x_next, z0, nxt))
        sfin_ref[0] = s_scr[...]

    variants = {}
    for s in range(steps):
        variants.setdefault((is_precise(s), is_precise((s + 1) % steps)), []).append(s)
    if len(variants) == 1:
        step(*next(iter(variants)))
    else:
        for flags, at in variants.items():
            pl.when(functools.reduce(jnp.logical_or, [c == s for s in at]))(
                functools.partial(step, *flags))


def _mixer_sample_kernel(*refs, seq_len, layer, has_prev):
    (x_ref, prev_ref, s0_ref, cos_ref, sin_ref, ng_ref, win_ref, wpool_ref, pscale_ref, gn_ref,
     wout_ref, dmat_ref, qdec_ref, kdec_ref, cdec_ref) = refs[:15]
    y_ref, tail_ref, sfin_ref, ubuf, ret_scr = refs[16 + 2 * has_prev:]
    if not has_prev:
        _zero_other_layers(tail_ref, layer)
        _zero_other_layers(sfin_ref, layer)
        tail_ref, sfin_ref = tail_ref.at[layer], sfin_ref.at[layer]
    ns = SAMPLE_SEQS
    rows = ns * seq_len
    x = x_ref[...].reshape(rows, D_MODEL)
    hn = _rmsnorm(x, ng_ref[...]).astype(BF16)
    z = _dot(hn, win_ref[...])
    ubuf[:, POOL_HIST - POOL_BUF:POOL_HIST, :] = prev_ref[...]
    ubuf[:, POOL_HIST:POOL_HIST + seq_len, :] = z[:, :POOL_WIDTH].reshape(ns, seq_len, POOL_WIDTH)
    pool_out = _pool_mix(ubuf, seq_len, 0, POOL_BUF, (wpool_ref,), pscale_ref[...])
    tail_ref[...] = ubuf[:, seq_len + POOL_HIST - POOL_BUF:seq_len + POOL_HIST, :]

    q_all, k_all, v_all, g_all = _qkvg(z)
    scale = RET_HEAD_DIM ** -0.5
    cos2 = cos_ref[...]
    sin2 = sin_ref[...]
    tok_seq = lax.broadcasted_iota(I32, (RET_HEAD_DIM, rows), 1) // seq_len
    for h in range(RET_HEADS):
        q = _rope(_head(q_all, h), cos2, sin2)
        k = _rope(_head(k_all, h), cos2, sin2) * scale
        vb = _head(v_all, h).astype(BF16)
        s_old = s0_ref[:, h]
        scores = _dot_nt(q.astype(BF16), k.astype(BF16)) * dmat_ref[h]
        qd = (q * _head(qdec_ref[...], h)).astype(BF16).reshape(ns, seq_len, RET_HEAD_DIM)
        o_state = jnp.einsum('bid,bde->bie', qd, s_old.astype(BF16), preferred_element_type=F32)
        o = _dot(scores.astype(BF16), vb) + o_state.reshape(rows, RET_HEAD_DIM)
        kdt = (k * _head(kdec_ref[...], h)).T
        lhs = jnp.concatenate(
            [jnp.where(tok_seq == b, kdt, 0.0).astype(BF16) for b in range(ns)], axis=0)
        upd = _dot(lhs, vb).reshape(ns, RET_HEAD_DIM, RET_HEAD_DIM)
        sfin_ref[:, h] = s_old * cdec_ref[h] + upd
        on = _group_norm(o) * _head(gn_ref[...], h)
        ret_scr[:, h * RET_HEAD_DIM:(h + 1) * RET_HEAD_DIM] = _silu(_head(g_all, h)) * on

    mix_in = jnp.concatenate([pool_out, ret_scr[...]], axis=-1).astype(BF16)
    y_ref[...] = (x + _dot(mix_in, wout_ref[...])).reshape(ns, seq_len, D_MODEL)


def _router_logits(hn, wr_cat_ref, wr_hi_ref, br_ref):
    h_hi, h_lo = _split(hn)
    part = _dot(h_hi, wr_cat_ref[...])
    return part[:, :LANES] + part[:, LANES:] + _dot(h_lo, wr_hi_ref[...]) + br_ref[...]


def _select(group_lg, expert_lg, gidx, eidx, axis):
    red = dict(axis=axis, keepdims=True)
    neg = jnp.float32(-jnp.inf)
    gmax = jnp.max(group_lg, **red)
    g_sel = jnp.min(jnp.where(group_lg == gmax, gidx, N_EXPERT_GROUPS), **red)
    p_sel = 1.0 / jnp.sum(jnp.exp(group_lg - gmax), **red)
    emask = (eidx >> 2) == g_sel
    v1 = jnp.max(jnp.where(emask, expert_lg, neg), **red)
    i1 = jnp.min(jnp.where(emask & (expert_lg == v1), eidx, N_EXPERTS), **red)
    emask2 = emask & (eidx != i1)
    v2 = jnp.max(jnp.where(emask2, expert_lg, neg), **red)
    i2 = jnp.min(jnp.where(emask2 & (expert_lg == v2), eidx, N_EXPERTS), **red)
    e2 = jnp.exp(v2 - v1)
    return g_sel, i1, i2, p_sel / (1.0 + e2), p_sel * e2 / (1.0 + e2)


def _moe_route_kernel(x_ref, ng_ref, wr_cat_ref, wr_hi_ref, br_ref, ha_ref, hb_ref, gate_ref,
                      idx_ref, count_ref, carry_scr):
    i = pl.program_id(0)
    rows = x_ref.shape[0]

    @pl.when(i == 0)
    def _():
        carry_scr[...] = jnp.zeros_like(carry_scr)

    hn = _rmsnorm(x_ref[...], ng_ref[...])
    ha_ref[...], hb_ref[...] = _pack_row(hn)
    lgt = _router_logits(hn, wr_cat_ref, wr_hi_ref, br_ref).T
    neg = jnp.float32(-jnp.inf)
    gidx = lax.broadcasted_iota(I32, (SUBLANES, rows), 0)
    eidx = lax.broadcasted_iota(I32, (N_EXPERTS, rows), 0)
    group_lg = jnp.where(gidx < N_EXPERT_GROUPS, lgt[0:SUBLANES], neg)
    expert_lg = lgt[EXPERT_LANE0:EXPERT_LANE0 + N_EXPERTS]
    g_sel, i1, i2, w1, w2 = _select(group_lg, expert_lg, gidx, eidx, 0)
    lo = jnp.minimum(i1, i2) - g_sel * EXPERTS_PER_GROUP
    hi = jnp.maximum(i1, i2) - g_sel * EXPERTS_PER_GROUP
    cls = g_sel * PAIRS_PER_GROUP + ((lo * (7 - lo)) >> 1) + (hi - lo - 1)
    w_lo = jnp.where(i1 < i2, w1, w2)
    w_hi = jnp.where(i1 < i2, w2, w1)

    crow = lax.broadcasted_iota(I32, (CLASS_ROWS, rows), 0)
    onehot = jnp.where(crow == cls, 1.0, 0.0)
    n_blk = rows // COUNT_BLOCK
    blocks = [onehot[:, j * COUNT_BLOCK:(j + 1) * COUNT_BLOCK] for j in range(n_blk)]
    r = lax.broadcasted_iota(I32, (COUNT_BLOCK, COUNT_BLOCK), 0)
    c = lax.broadcasted_iota(I32, (COUNT_BLOCK, COUNT_BLOCK), 1)
    upper = jnp.where(r < c, 1.0, 0.0).astype(BF16)
    within = _dot(jnp.concatenate(blocks, axis=0).astype(BF16), upper)
    carry = carry_scr[:, 0:1]
    ranks = []
    for j in range(n_blk):
        before = within[j * CLASS_ROWS:(j + 1) * CLASS_ROWS] + carry
        ranks.append(jnp.sum(blocks[j] * before, axis=0, keepdims=True))
        carry = carry + jnp.sum(blocks[j], axis=1, keepdims=True)
    rank = jnp.concatenate(ranks, axis=1)
    carry_scr[...] = jnp.broadcast_to(carry, carry_scr.shape)
    count_ref[...] = carry_scr[...]

    row8 = lax.broadcasted_iota(I32, (SUBLANES, rows), 0)
    idx_ref[...] = jnp.where(row8 == 0, cls, jnp.where(row8 == 1, rank.astype(I32), 0))
    rowl = lax.broadcasted_iota(I32, (LANES, rows), 0)
    gate_ref[...] = jnp.where(rowl == GATE_LO, w_lo, jnp.where(rowl == GATE_HI, w_hi, 0.0)).T


def _moe_pair_kernel(ea_ref, eb_ref, used_ref, ha_ref, hb_ref, gate_ref, wg_a_ref, wu_a_ref,
                     wd_a_ref, wg_b_ref, wu_b_ref, wd_b_ref, after_ref, ya_ref, yb_ref, wgu_scr,
                     wd_scr):
    del after_ref
    i = pl.program_id(0)
    prev = jnp.maximum(i - 1, 0)
    new_class = (i == 0) | (ea_ref[i] != ea_ref[prev]) | (eb_ref[i] != eb_ref[prev])

    @pl.when(new_class)
    def _():
        for s, (wg_ref, wu_ref, wd_ref) in enumerate(((wg_a_ref, wu_a_ref, wd_a_ref),
                                                      (wg_b_ref, wu_b_ref, wd_b_ref))):
            wgu_scr[s, :, :D_EXPERT] = wg_ref[0].astype(BF16)
            wgu_scr[s, :, D_EXPERT:] = wu_ref[0].astype(BF16)
            wd_scr[s] = wd_ref[0].astype(BF16)

    @pl.when(i < used_ref[0])
    def _():
        h = _unpack_row(ha_ref[...], hb_ref[...]).astype(BF16)
        gates = gate_ref[...]
        y = None
        for s, lane in enumerate((GATE_LO, GATE_HI)):
            gu = _dot(h, wgu_scr[s])
            act = _silu(gu[:, :D_EXPERT]) * gu[:, D_EXPERT:]
            part = _dot((act * gates[:, lane:lane + 1]).astype(BF16), wd_scr[s])
            y = part if y is None else y + part
        ya_ref[...], yb_ref[...] = _pack_row(y)


def _final_norm_kernel(x_ref, ya_ref, yb_ref, g_ref, o_ref):
    o_ref[...] = _rmsnorm(x_ref[...] + _unpack_row(ya_ref[...], yb_ref[...]), g_ref[...])


def _moe_dense_kernel(x_ref, ng_ref, wr_cat_ref, wr_hi_ref, br_ref, wg_ref, wu_ref, wd_ref, nf_ref,
                      y_ref, h_scr, gate_scr, acc_scr, *, final_norm):
    e = pl.program_id(1)

    @pl.when(e == 0)
    def _():
        hn = _rmsnorm(x_ref[...], ng_ref[...])
        h_scr[...] = hn.astype(BF16)
        lg = _router_logits(hn, wr_cat_ref, wr_hi_ref, br_ref)
        lane = lax.broadcasted_iota(I32, lg.shape, 1)
        neg = jnp.float32(-jnp.inf)
        group_lg = jnp.where(lane < N_EXPERT_GROUPS, lg, neg)
        is_expert = (lane >= EXPERT_LANE0) & (lane < EXPERT_LANE0 + N_EXPERTS)
        _, i1, i2, w1, w2 = _select(group_lg, jnp.where(is_expert, lg, neg), lane,
                                    jnp.where(is_expert, lane - EXPERT_LANE0, N_EXPERTS), 1)
        gate_scr[...] = (jnp.where(lane == i1 + EXPERT_LANE0, w1, 0.0)
                         + jnp.where(lane == i2 + EXPERT_LANE0, w2, 0.0))
        acc_scr[...] = jnp.zeros_like(acc_scr)

    lane = lax.broadcasted_iota(I32, gate_scr.shape, 1)
    gate = jnp.sum(jnp.where(lane == EXPERT_LANE0 + e, gate_scr[...], 0.0), axis=-1, keepdims=True)
    h = h_scr[...]
    act = _silu(_dot(h, wg_ref[0].astype(BF16))) * _dot(h, wu_ref[0].astype(BF16))
    acc_scr[...] += _dot((act * gate).astype(BF16), wd_ref[0].astype(BF16))

    @pl.when(e == N_EXPERTS - 1)
    def _():
        y = x_ref[...] + acc_scr[...]
        if final_norm:
            y = _rmsnorm(y, nf_ref[...])
        y_ref[...] = y


def _const_spec(shape):
    return pl.BlockSpec(shape, lambda *_: (0,) * len(shape))


def _resident_spec(shape):
    return pl.BlockSpec(shape, lambda *_: (0,) * len(shape), pipeline_mode=pl.Buffered(1))


_ANY_SPEC = pl.BlockSpec(memory_space=pl.ANY)


def _rope_tables(pos):
    half = RET_HEAD_DIM // 2
    inv = np.float32(ROPE_BASE) ** (-np.arange(half, dtype=np.float32) / np.float32(half))
    ang = pos.astype(np.float32)[:, None] * inv[None, :]
    cos, sin = np.cos(ang), np.sin(ang)
    return np.concatenate([cos, cos], -1), np.concatenate([-sin, sin], -1)


def _decay_consts(c, reps):
    f = np.float32
    log_g = np.log(f(1.0) - f(2.0) ** (f(-5.0) - np.arange(RET_HEADS, dtype=f)))
    i = np.arange(c, dtype=f)
    diff = i[:, None] - i[None, :]
    dmat = np.where(diff[None] >= 0, np.exp(np.maximum(diff, f(0))[None] * log_g[:, None, None]), f(0))
    qdec = np.exp((i + f(1))[None, :] * log_g[:, None])
    kdec = np.exp((f(c) - f(1) - i)[None, :] * log_g[:, None])
    cdec = np.exp(f(c) * log_g)
    dmat = np.einsum('ab,hij->haibj', np.eye(reps, dtype=f), dmat).reshape(RET_HEADS, reps * c, reps * c)
    lanes = lambda t: np.repeat(np.tile(t, (1, reps)).T, RET_HEAD_DIM, axis=1)
    cdec = np.broadcast_to(cdec[:, None, None], (RET_HEADS, 1, RET_HEAD_DIM))
    return tuple(np.ascontiguousarray(a, dtype=f) for a in (dmat, lanes(qdec), lanes(kdec), cdec))


def _split_weight(w):
    hi = lax.bitcast_convert_type(_bf16_round_bits(lax.bitcast_convert_type(w, jnp.uint32)), F32)
    return hi.astype(BF16), (w - hi).astype(BF16)


_W_IN_BLOCK = (None, D_MODEL, IN_WIDTH)
_W_POOL_BLOCK = (None, len(POOL_WINDOWS), POOL_GROUP_DIM, POOL_GROUP_DIM)
_W_OUT_BLOCK = (None, D_MODEL, D_MODEL)
_DECAY_SPECS = [
    _const_spec((RET_HEADS, CHUNK, CHUNK)), _const_spec((CHUNK, RET_WIDTH)),
    _const_spec((CHUNK, RET_WIDTH)), _const_spec((RET_HEADS, 1, RET_HEAD_DIM)),
]


def _layer_spec(block, layer, resident=False):
    index_map = lambda *_: (layer,) + (0,) * (len(block) - 1)
    if resident:
        return pl.BlockSpec(block, index_map, pipeline_mode=pl.Buffered(1))
    return pl.BlockSpec(block, index_map)


def _mixer_weight_specs(layer, split):
    n = 2 if split else 1
    row = lambda width: _layer_spec((None, 1, width), layer)
    return [row(D_MODEL), *[_layer_spec(_W_IN_BLOCK, layer, True)] * n,
            *[_layer_spec(_W_POOL_BLOCK, layer)] * n, row(POOL_WIDTH), row(RET_WIDTH),
            *[_layer_spec(_W_OUT_BLOCK, layer, True)] * n]


def _chain_out_spec(block, depth, layer, has_prev):
    zeros = (0,) * (len(block) - 1)
    if has_prev:
        return pl.BlockSpec((None,) + block, lambda i, *_: (layer, i) + zeros)
    return pl.BlockSpec((depth,) + block, lambda i, *_: (0, i) + zeros)


def _mixer_prompt(x, moe_y, weights, prev_out, layer, rows, precise_tail):
    b, l, _ = x.shape
    depth = weights[0].shape[0]
    assert l % rows == 0 and rows % CHUNK == 0
    cos2, sin2 = _rope_tables(np.arange(l))
    decay = _decay_consts(RET_CHUNK, 1)
    steps, half = l // rows, rows // 2
    tok = lambda i, c: (i, c, 0)

    def ahead(i, c):
        wraps = c + 1 == steps
        return (jnp.where(wraps, jnp.minimum(i + 1, b - 1), i), jnp.where(wraps, 0, 2 * (c + 1)), 0)

    if moe_y is None:
        ya = yb = jnp.zeros((1, rows, PACK_W), I32)
        y_spec = y_ahead = _const_spec((1, rows, PACK_W))
    else:
        ya, yb = (t.reshape(b, l, PACK_W) for t in moe_y)
        y_spec = pl.BlockSpec((1, rows, PACK_W), tok)
        y_ahead = pl.BlockSpec((1, half, PACK_W), ahead)
    has_prev = prev_out is not None
    n_in = 21
    y, tails, states = pl.pallas_call(
        functools.partial(_mixer_prompt_kernel, rows=rows, steps=steps, moe_in=moe_y is not None,
                          precise_tail=precise_tail, layer=layer, has_prev=has_prev),
        grid=(b, steps),
        in_specs=[pl.BlockSpec((1, rows, D_MODEL), tok), y_spec, y_spec,
                  pl.BlockSpec((1, half, D_MODEL), ahead), y_ahead, y_ahead,
                  pl.BlockSpec((rows, RET_HEAD_DIM), lambda i, c: (c, 0)),
                  pl.BlockSpec((rows, RET_HEAD_DIM), lambda i, c: (c, 0)),
                  *_mixer_weight_specs(layer, True), *_DECAY_SPECS,
                  *([_ANY_SPEC, _ANY_SPEC] if has_prev else [])],
        out_specs=[pl.BlockSpec((1, rows, D_MODEL), tok),
                   _chain_out_spec((1, POOL_BUF, POOL_WIDTH), depth, layer, has_prev),
                   _chain_out_spec((1, RET_HEADS, RET_HEAD_DIM, RET_HEAD_DIM), depth, layer, has_prev)],
        out_shape=[jax.ShapeDtypeStruct(x.shape, F32),
                   jax.ShapeDtypeStruct((depth, b, POOL_BUF, POOL_WIDTH), F32),
                   jax.ShapeDtypeStruct((depth, b, RET_HEADS, RET_HEAD_DIM, RET_HEAD_DIM), F32)],
        input_output_aliases={n_in: 1, n_in + 1: 2} if has_prev else {},
        scratch_shapes=[pltpu.VMEM((1, POOL_HIST + half, POOL_WIDTH), F32),
                        pltpu.VMEM((RET_HEADS, RET_HEAD_DIM, RET_HEAD_DIM), F32),
                        pltpu.VMEM((half, RET_WIDTH), F32),
                        pltpu.VMEM((half, IN_WIDTH), F32), pltpu.VMEM((half, IN_WIDTH), F32)],
        compiler_params=pltpu.CompilerParams(
            dimension_semantics=("arbitrary", "arbitrary"), vmem_limit_bytes=VMEM_LIMIT),
        name="mixer_prompt",
    )(x, ya, yb, x, ya, yb, cos2, sin2, *weights, *decay, *(prev_out if has_prev else ()))
    return y, (tails, states)


def _mixer_sample(x, pool_prev, s0, weights, prev_out, layer, after):
    b, l, _ = x.shape
    depth = s0.shape[0]
    ns = SAMPLE_SEQS
    assert ns * l == CHUNK and b % ns == 0
    cos2, sin2 = _rope_tables(PAST_LEN + np.arange(l))
    cos2, sin2 = np.tile(cos2, (ns, 1)), np.tile(sin2, (ns, 1))
    decay = _decay_consts(l, ns)
    seq3 = lambda i: (i, 0, 0)
    state_block = (ns, RET_HEADS, RET_HEAD_DIM, RET_HEAD_DIM)
    tail_block = (ns, POOL_BUF, POOL_WIDTH)
    has_prev = prev_out is not None
    n_in = 16
    y, tails, states = pl.pallas_call(
        functools.partial(_mixer_sample_kernel, seq_len=l, layer=layer, has_prev=has_prev),
        grid=(b // ns,),
        in_specs=[pl.BlockSpec((ns, l, D_MODEL), seq3),
                  pl.BlockSpec((None,) + tail_block, lambda i: (layer, i, 0, 0)),
                  pl.BlockSpec((None,) + state_block, lambda i: (layer, i, 0, 0, 0)),
                  _const_spec((CHUNK, RET_HEAD_DIM)), _const_spec((CHUNK, RET_HEAD_DIM)),
                  *_mixer_weight_specs(layer, False), *_DECAY_SPECS, _ANY_SPEC,
                  *([_ANY_SPEC, _ANY_SPEC] if has_prev else [])],
        out_specs=[pl.BlockSpec((ns, l, D_MODEL), seq3),
                   _chain_out_spec(tail_block, depth, layer, has_prev),
                   _chain_out_spec(state_block, depth, layer, has_prev)],
        out_shape=[jax.ShapeDtypeStruct(x.shape, F32),
                   jax.ShapeDtypeStruct((depth, b, POOL_BUF, POOL_WIDTH), F32),
                   jax.ShapeDtypeStruct(s0.shape, F32)],
        input_output_aliases={n_in: 1, n_in + 1: 2} if has_prev else {},
        scratch_shapes=[pltpu.VMEM((ns, POOL_HIST + l, POOL_WIDTH), F32),
                        pltpu.VMEM((CHUNK, RET_WIDTH), F32)],
        compiler_params=pltpu.CompilerParams(
            dimension_semantics=("arbitrary",), vmem_limit_bytes=VMEM_LIMIT),
        name="mixer_sample",
    )(x, pool_prev, s0, cos2, sin2, *weights, *decay, after, *(prev_out if has_prev else ()))
    return y, (tails, states)


def _router_weights(w_rg, b_rg, w_re, b_re):
    depth = w_rg.shape[0]
    gap = EXPERT_LANE0 - N_EXPERT_GROUPS
    rest = LANES - EXPERT_LANE0 - N_EXPERTS
    wr = jnp.concatenate([w_rg, jnp.zeros((depth, D_MODEL, gap), F32), w_re,
                          jnp.zeros((depth, D_MODEL, rest), F32)], axis=-1)
    br = jnp.concatenate([b_rg, jnp.zeros((depth, gap), F32), b_re, jnp.zeros((depth, rest), F32)],
                         axis=-1).reshape(depth, 1, LANES)
    wr_hi, wr_lo = _split_weight(wr)
    return jnp.concatenate([wr_hi, wr_lo], axis=-1), wr_hi, br


def _router_specs(layer):
    return [_layer_spec((None, D_MODEL, 2 * LANES), layer), _layer_spec((None, D_MODEL, LANES), layer),
            _layer_spec((None, 1, LANES), layer)]


def _sc_mesh():
    return plsc.VectorSubcoreMesh(core_axis_name="core", subcore_axis_name="subcore",
                                  num_cores=SC_CORES, num_subcores=SC_SUBCORES)


def _sc_params():
    params = pltpu.CompilerParams()
    if "needs_layout_passes" in pltpu.CompilerParams.__dataclass_fields__:
        params = dataclasses.replace(params, needs_layout_passes=False)
    return params


def _sc_gather(tables, idx):
    n = idx.shape[0]
    assert n % SC_WINDOW == 0
    nt = len(tables)

    def body(*refs):
        i_hbm = refs[nt]
        for t_hbm, o_hbm in zip(refs[:nt], refs[nt + 1:]):
            def gather_window(i_vmem, o_vmem, t_hbm=t_hbm):
                pltpu.sync_copy(t_hbm.at[i_vmem.at[0]], o_vmem)

            pltpu.emit_pipeline(
                gather_window, grid=(n // SC_WINDOW,),
                in_specs=[pl.BlockSpec((1, SC_WINDOW), lambda i: (0, i))],
                out_specs=[pl.BlockSpec((SC_WINDOW, t_hbm.shape[1]), lambda i: (i, 0))],
                core_axis_name=("core", "subcore"),
                dimension_semantics=(pltpu.PARALLEL,),
            )(i_hbm, o_hbm)

    out_type = tuple(jax.ShapeDtypeStruct((n, t.shape[1]), t.dtype) for t in tables)
    return pl.kernel(body, out_type=out_type, mesh=_sc_mesh(), name="sc_gather")(
        *tables, idx.reshape(1, n))


def _sc_slots(cls, rank, starts, n_slots):
    t = cls.shape[0]
    workers = SC_CORES * SC_SUBCORES
    slot_per, tok_per = n_slots // workers, t // workers
    assert n_slots % (workers * SC_LANES) == 0 and t % (workers * SC_LANES) == 0 and t & (t - 1) == 0
    assert n_slots % (SC_LANES * SC_UNROLL) == 0 and t % (SC_LANES * SC_UNROLL) == 0

    def body(cls_hbm, rank_hbm, starts_hbm, pos_hbm, slot_hbm, cls_v, rank_v, starts_v, pos_v, slot_v):
        wid = lax.axis_index("subcore") * SC_CORES + lax.axis_index("core")
        pltpu.sync_copy(cls_hbm, cls_v)
        pltpu.sync_copy(rank_hbm, rank_v)
        pltpu.sync_copy(starts_hbm, starts_v)

        lane = lax.iota(I32, SC_LANES)
        span = SC_LANES * SC_UNROLL

        @pl.loop(0, n_slots, step=span)
        def _(i):
            for u in range(SC_UNROLL):
                j = i + u * SC_LANES
                slot_v[pl.ds(j, SC_LANES)] = (lane + j) & (t - 1)

        @pl.loop(0, t, step=span)
        def _(i):
            for u in range(SC_UNROLL):
                j = i + u * SC_LANES
                at = pl.ds(j, SC_LANES)
                pos = plsc.load_gather(starts_v, [cls_v[at]]) + rank_v[at]
                pos_v[at] = pos
                plsc.store_scatter(slot_v, [pos], lane + j)

        tok_off = pl.multiple_of(wid * tok_per, SC_LANES)
        pltpu.sync_copy(pos_v.at[pl.ds(tok_off, tok_per)], pos_hbm.at[pl.ds(tok_off, tok_per)])
        slot_off = pl.multiple_of(wid * slot_per, SC_LANES)
        pltpu.sync_copy(slot_v.at[pl.ds(slot_off, slot_per)], slot_hbm.at[pl.ds(slot_off, slot_per)])

    return pl.kernel(
        body, mesh=_sc_mesh(), compiler_params=_sc_params(), name="sc_slots",
        out_type=(jax.ShapeDtypeStruct((t,), I32), jax.ShapeDtypeStruct((n_slots,), I32)),
        scratch_types=[pltpu.VMEM((t,), I32), pltpu.VMEM((t,), I32), pltpu.VMEM((CLASS_ROWS,), I32),
                       pltpu.VMEM((t,), I32), pltpu.VMEM((n_slots,), I32)],
    )(cls, rank, starts)


def _moe_route(x, norm_g, router, layer):
    t = x.shape[0]
    tok = lambda i: (i, 0)
    return pl.pallas_call(
        _moe_route_kernel,
        grid=(t // ROUTE_TILE,),
        in_specs=[pl.BlockSpec((ROUTE_TILE, D_MODEL), tok), _layer_spec((None, 1, D_MODEL), layer),
                  *_router_specs(layer)],
        out_specs=[pl.BlockSpec((ROUTE_TILE, PACK_W), tok), pl.BlockSpec((ROUTE_TILE, PACK_W), tok),
                   pl.BlockSpec((ROUTE_TILE, LANES), tok),
                   pl.BlockSpec((SUBLANES, ROUTE_TILE), lambda i: (0, i)),
                   _const_spec((CLASS_ROWS, LANES))],
        out_shape=[jax.ShapeDtypeStruct((t, PACK_W), I32), jax.ShapeDtypeStruct((t, PACK_W), I32),
                   jax.ShapeDtypeStruct((t, LANES), F32), jax.ShapeDtypeStruct((SUBLANES, t), I32),
                   jax.ShapeDtypeStruct((CLASS_ROWS, LANES), F32)],
        scratch_shapes=[pltpu.VMEM((CLASS_ROWS, LANES), F32)],
        compiler_params=pltpu.CompilerParams(
            dimension_semantics=("arbitrary",), vmem_limit_bytes=VMEM_LIMIT),
        name="moe_route",
    )(x, norm_g, *router)


def _moe_experts(routed, w_gate, w_up, w_down, layer, after):
    ha, hb, gates, idx, counts = routed
    t = ha.shape[0]
    n_slots = t + N_CLASSES * PAIR_TILE
    n_tiles = n_slots // PAIR_TILE

    cnt = counts[:, 0].astype(I32)
    padded = (cnt + PAIR_TILE - 1) // PAIR_TILE * PAIR_TILE
    ends = jnp.cumsum(padded)
    starts = ends - padded
    tile_cls = jnp.minimum(
        jnp.sum(ends[None, :N_CLASSES] <= (jnp.arange(n_tiles, dtype=I32) * PAIR_TILE)[:, None], axis=1),
        N_CLASSES - 1).astype(I32)
    first = (tile_cls // PAIRS_PER_GROUP) * EXPERTS_PER_GROUP
    tile_ea = first + jnp.asarray(PAIR_LO, I32)[tile_cls % PAIRS_PER_GROUP]
    tile_eb = first + jnp.asarray(PAIR_HI, I32)[tile_cls % PAIRS_PER_GROUP]
    used = (ends[N_CLASSES - 1:N_CLASSES] // PAIR_TILE).astype(I32)

    pos, slot_tok = _sc_slots(idx[0], idx[1], starts, n_slots)
    hsa, hsb, gate_s = _sc_gather((ha, hb, gates), slot_tok)

    row = lambda i, ea, eb, nu: (jnp.minimum(i, nu[0] - 1), 0)
    w_spec = lambda shape, which: pl.BlockSpec(
        (None, 1) + shape, lambda i, ea, eb, nu: (layer, (ea, eb)[which][i], 0, 0))
    gu_shape, d_shape = (D_MODEL, D_EXPERT), (D_EXPERT, D_MODEL)
    ysa, ysb = pl.pallas_call(
        _moe_pair_kernel,
        grid_spec=pltpu.PrefetchScalarGridSpec(
            num_scalar_prefetch=3, grid=(n_tiles,),
            in_specs=[pl.BlockSpec((PAIR_TILE, PACK_W), row), pl.BlockSpec((PAIR_TILE, PACK_W), row),
                      pl.BlockSpec((PAIR_TILE, LANES), row),
                      w_spec(gu_shape, 0), w_spec(gu_shape, 0), w_spec(d_shape, 0),
                      w_spec(gu_shape, 1), w_spec(gu_shape, 1), w_spec(d_shape, 1), _ANY_SPEC],
            out_specs=[pl.BlockSpec((PAIR_TILE, PACK_W), row), pl.BlockSpec((PAIR_TILE, PACK_W), row)],
            scratch_shapes=[pltpu.VMEM((2, D_MODEL, 2 * D_EXPERT), BF16),
                            pltpu.VMEM((2, D_EXPERT, D_MODEL), BF16)]),
        out_shape=[jax.ShapeDtypeStruct((n_slots, PACK_W), I32),
                   jax.ShapeDtypeStruct((n_slots, PACK_W), I32)],
        compiler_params=pltpu.CompilerParams(
            dimension_semantics=("arbitrary",), vmem_limit_bytes=VMEM_LIMIT),
        name="moe_pair",
    )(tile_ea, tile_eb, used, hsa, hsb, gate_s, w_gate, w_up, w_down, w_gate, w_up, w_down, after)
    return _sc_gather((ysa, ysb), pos)


def _final_norm(x, moe_y, g, rows):
    t = x.shape[0]
    tok = lambda i: (i, 0)
    return pl.pallas_call(
        _final_norm_kernel,
        grid=(t // rows,),
        in_specs=[pl.BlockSpec((rows, D_MODEL), tok), pl.BlockSpec((rows, PACK_W), tok),
                  pl.BlockSpec((rows, PACK_W), tok), _const_spec((1, D_MODEL))],
        out_specs=pl.BlockSpec((rows, D_MODEL), tok),
        out_shape=jax.ShapeDtypeStruct(x.shape, F32),
        compiler_params=pltpu.CompilerParams(
            dimension_semantics=("arbitrary",), vmem_limit_bytes=VMEM_LIMIT),
        name="final_norm",
    )(x, *moe_y, g.reshape(1, D_MODEL))


def _moe_dense(x, norm_g, router, w_gate, w_up, w_down, norm_final, layer, final_norm, rows):
    t = x.shape[0]
    assert t % rows == 0
    tok = lambda i, e: (i, 0)
    w_spec = lambda shape: pl.BlockSpec((None, 1) + shape, lambda i, e: (layer, e, 0, 0))
    return pl.pallas_call(
        functools.partial(_moe_dense_kernel, final_norm=final_norm),
        grid=(t // rows, N_EXPERTS),
        in_specs=[pl.BlockSpec((rows, D_MODEL), tok), _layer_spec((None, 1, D_MODEL), layer),
                  *_router_specs(layer),
                  w_spec((D_MODEL, D_EXPERT)), w_spec((D_MODEL, D_EXPERT)), w_spec((D_EXPERT, D_MODEL)),
                  _const_spec((1, D_MODEL))],
        out_specs=pl.BlockSpec((rows, D_MODEL), tok),
        out_shape=jax.ShapeDtypeStruct(x.shape, F32),
        scratch_shapes=[pltpu.VMEM((rows, D_MODEL), BF16),
                        pltpu.VMEM((rows, LANES), F32),
                        pltpu.VMEM((rows, D_MODEL), F32)],
        compiler_params=pltpu.CompilerParams(
            dimension_semantics=("arbitrary", "arbitrary"), vmem_limit_bytes=VMEM_LIMIT),
        name="moe_dense",
    )(x, norm_g, *router, w_gate, w_up, w_down, norm_final.reshape(1, D_MODEL))


def kernel(x_prompt, x_sample, cache_pool, state_ret, norm_mix, w_in, w_pool, pool_scale, ret_gn, w_out, norm_ffn, w_router_group, b_router_group, w_router_expert, b_router_expert, w_gate, w_up, w_down, norm_final):
    depth = norm_mix.shape[0]
    row = lambda a: a.reshape(depth, 1, a.shape[-1])
    mix_split = (row(norm_mix), *_split_weight(w_in), *_split_weight(w_pool), row(pool_scale),
                 row(ret_gn), *_split_weight(w_out))
    mix_hi = tuple(mix_split[i] for i in (0, 1, 3, 5, 6, 7))
    router = _router_weights(w_router_group, b_router_group, w_router_expert, b_router_expert)
    norm_ffn = row(norm_ffn)

    yp, ys = x_prompt, x_sample
    moe_p = None
    out_p = out_s = None
    for l in range(depth):
        yp, out_p = _mixer_prompt(yp, moe_p, mix_split, out_p, l, rows=512,
                                  precise_tail=PRECISE_TAIL_STEPS if l < depth - 1 else 0)
        routed = _moe_route(yp.reshape(-1, D_MODEL), norm_ffn, router, l)
        ys, out_s = _mixer_sample(ys, cache_pool, state_ret, mix_hi, out_s, l, after=routed[-1])
        ys = _moe_dense(ys.reshape(-1, D_MODEL), norm_ffn, router, w_gate, w_up, w_down, norm_final,
                        l, l == depth - 1, rows=1024).reshape(ys.shape)
        moe_p = _moe_experts(routed, w_gate, w_up, w_down, l, after=ys)
    yp = _final_norm(yp.reshape(-1, D_MODEL), moe_p, norm_final, rows=1024).reshape(yp.shape)
    return (yp, ys, *out_p, *out_s)
```

```python
import dataclasses
import functools

import jax
import jax.numpy as jnp
import numpy as np
from jax import lax
from jax.experimental import pallas as pl
from jax.experimental.pallas import tpu as pltpu
from jax.experimental.pallas import tpu_sc as plsc

F32 = jnp.float32
BF16 = jnp.bfloat16
I32 = jnp.int32

D_MODEL = 1024
POOL_WIDTH = 512
POOL_WINDOWS = (2, 4, 8, 16)
POOL_GROUP_DIM = 128
POOL_BUF = 15
POOL_HIST = 16
RET_WIDTH = 512
RET_HEADS = 4
RET_HEAD_DIM = 128
RET_CHUNK = 128
ROPE_BASE = 10000.0
IN_WIDTH = POOL_WIDTH + 4 * RET_WIDTH
N_EXPERT_GROUPS = 4
EXPERTS_PER_GROUP = 4
N_EXPERTS = 16
D_EXPERT = 256
RMS_EPS = 1e-6
GN_EPS = 1e-5
PAST_LEN = 16384

LANES = 128
SUBLANES = 8
EXPERT_LANE0 = 8
PAIRS_PER_GROUP = 6
N_CLASSES = N_EXPERT_GROUPS * PAIRS_PER_GROUP
CLASS_ROWS = 32
PAIR_LO = (0, 0, 0, 1, 1, 2)
PAIR_HI = (1, 2, 3, 2, 3, 3)
GATE_LO, GATE_HI = 0, 1
ROUTE_TILE = 1024
COUNT_BLOCK = 256
PAIR_TILE = 256
PACK_W = D_MODEL // 4
SC_CORES, SC_SUBCORES, SC_LANES = 2, 16, 16
SC_WINDOW = 128
SC_UNROLL = 8
PRECISE_TAIL_STEPS = 1
CHUNK = 128
SAMPLE_SEQS = 16
VMEM_LIMIT = 56 * 1024 * 1024


def _dot(a, b):
    return jnp.dot(a, b, preferred_element_type=F32)


def _dot_nt(a, b):
    return lax.dot_general(a, b, (((1,), (1,)), ((), ())), preferred_element_type=F32)


def _bf16_round_bits(u):
    return (u + jnp.uint32(0x7FFF) + ((u >> 16) & jnp.uint32(1))) & jnp.uint32(0xFFFF0000)


def _split(a):
    hi = pltpu.bitcast(_bf16_round_bits(pltpu.bitcast(a, jnp.uint32)), F32)
    return hi.astype(BF16), (a - hi).astype(BF16)


def _mm(a, b, precise, nt=False):
    dot = _dot_nt if nt else _dot
    if precise:
        b_hi, b_lo = b if isinstance(b, tuple) else _split(b)
        a_hi, a_lo = _split(a)
        return dot(a_hi, b_hi) + dot(a_lo, b_hi) + dot(a_hi, b_lo)
    return dot(a.astype(BF16), b[0] if isinstance(b, tuple) else b.astype(BF16))


def _rmsnorm(x, g):
    ms = jnp.mean(x * x, axis=-1, keepdims=True)
    return x * lax.rsqrt(ms + RMS_EPS) * g


def _pool_mix(ubuf, rows, t_first, n_prev, wpool_refs, pscale, precise=False, row0=0):
    ns = ubuf.shape[0]
    t = t_first + lax.broadcasted_iota(I32, (1, rows, POOL_GROUP_DIM), 1)
    base = POOL_HIST + row0
    outs = []
    for j, w in enumerate(POOL_WINDOWS):
        lanes = slice(j * POOL_GROUP_DIM, (j + 1) * POOL_GROUP_DIM)
        uj = ubuf[:, base:base + rows, lanes]
        acc = uj
        for i in range(1, w):
            acc = acc + ubuf[:, base - i:base - i + rows, lanes]
        cnt = jnp.minimum(w, n_prev + t + 1).astype(F32)
        d = (acc / cnt - uj).reshape(ns * rows, POOL_GROUP_DIM)
        outs.append(_mm(d, tuple(w[j] for w in wpool_refs), precise))
    return jnp.concatenate(outs, axis=-1) * pscale


def _rope(xh, cos2, sin2):
    return xh * cos2 + pltpu.roll(xh, RET_HEAD_DIM // 2, 1) * sin2


def _group_norm(o):
    mu = jnp.mean(o, axis=-1, keepdims=True)
    c = o - mu
    var = jnp.mean(c * c, axis=-1, keepdims=True)
    return c * lax.rsqrt(var + GN_EPS)


def _silu(x):
    return x * (1.0 / (1.0 + jnp.exp(-x)))


def _head(a, h):
    return a[:, h * RET_HEAD_DIM:(h + 1) * RET_HEAD_DIM]


def _qkvg(z):
    p, r = POOL_WIDTH, RET_WIDTH
    return z[:, p:p + r], z[:, p + r:p + 2 * r], z[:, p + 2 * r:p + 3 * r], z[:, p + 3 * r:p + 4 * r]


def _pack_bf16_pair(a, b):
    ua = pltpu.bitcast(a.astype(BF16).astype(F32), jnp.uint32)
    ub = pltpu.bitcast(b.astype(BF16).astype(F32), jnp.uint32)
    return pltpu.bitcast((ua >> 16) | (ub & jnp.uint32(0xFFFF0000)), I32)


def _unpack_bf16_pair(w):
    u = pltpu.bitcast(w, jnp.uint32)
    return pltpu.bitcast(u << 16, F32), pltpu.bitcast(u & jnp.uint32(0xFFFF0000), F32)


def _pack_row(y):
    q = PACK_W
    return _pack_bf16_pair(y[:, 0:q], y[:, q:2 * q]), _pack_bf16_pair(y[:, 2 * q:3 * q], y[:, 3 * q:])


def _unpack_row(wa, wb):
    return jnp.concatenate([*_unpack_bf16_pair(wa), *_unpack_bf16_pair(wb)], axis=-1)


def _zero_other_layers(ref, layer):
    for j in range(ref.shape[0]):
        if j != layer:
            ref[j] = jnp.zeros(ref.shape[1:], ref.dtype)


def _mixer_prompt_kernel(*refs, rows, steps, moe_in, precise_tail, layer, has_prev):
    (x_ref, ya_ref, yb_ref, cos_ref, sin_ref, ng_ref, win_hi_ref, win_lo_ref, wpool_hi_ref,
     wpool_lo_ref, pscale_ref, gn_ref, wout_hi_ref, wout_lo_ref, dmat_ref, qdec_ref, kdec_ref,
     cdec_ref) = refs[:18]
    y_ref, tail_ref, sfin_ref, ubuf, s_scr, ret_scr, z_scr = refs[18 + 2 * has_prev:]
    c = pl.program_id(1)
    if not has_prev:
        _zero_other_layers(tail_ref, layer)
        _zero_other_layers(sfin_ref, layer)
        tail_ref, sfin_ref = tail_ref.at[layer], sfin_ref.at[layer]

    @pl.when(c == 0)
    def _():
        ubuf[:, 0:POOL_HIST, :] = jnp.zeros((1, POOL_HIST, POOL_WIDTH), F32)
        s_scr[...] = jnp.zeros_like(s_scr)

    kv_cols = slice(POOL_WIDTH + RET_WIDTH, POOL_WIDTH + 3 * RET_WIDTH)

    def step(kv_precise, full_from):
        x = x_ref[0]
        if moe_in:
            x = x + _unpack_row(ya_ref[0], yb_ref[0])
        hn = _rmsnorm(x, ng_ref[...])
        hi, lo = _split(hn) if kv_precise else (hn.astype(BF16), None)
        z_scr[...] = _dot(hi, win_hi_ref[...])
        if kv_precise and full_from:
            z_scr[:full_from, kv_cols] += (_dot(lo[:full_from], win_hi_ref[:, kv_cols])
                                           + _dot(hi[:full_from], win_lo_ref[:, kv_cols]))
        if full_from < rows:
            z_scr[full_from:, :] += (_dot(lo[full_from:], win_hi_ref[...])
                                     + _dot(hi[full_from:], win_lo_ref[...]))

        ubuf[0, POOL_HIST:POOL_HIST + rows, :] = z_scr[:, :POOL_WIDTH]
        pool_w = (wpool_hi_ref, wpool_lo_ref)
        pool_parts = []
        if full_from:
            pool_parts.append(_pool_mix(ubuf, full_from, c * rows, 0, pool_w, pscale_ref[...]))
        if full_from < rows:
            pool_parts.append(_pool_mix(ubuf, rows - full_from, c * rows + full_from, 0, pool_w,
                                        pscale_ref[...], precise=True, row0=full_from))
        pool_out = jnp.concatenate(pool_parts, axis=0)
        tail_ref[...] = ubuf[:, rows + POOL_HIST - POOL_BUF:rows + POOL_HIST, :]
        ubuf[:, 0:POOL_HIST, :] = ubuf[:, rows:rows + POOL_HIST, :]

        scale = RET_HEAD_DIM ** -0.5
        for ci in range(rows // CHUNK):
            rs = slice(ci * CHUNK, (ci + 1) * CHUNK)
            full = ci * CHUNK >= full_from
            cos2 = cos_ref[rs, :]
            sin2 = sin_ref[rs, :]
            for h in range(RET_HEADS):
                col = lambda part: slice(POOL_WIDTH + part * RET_WIDTH + h * RET_HEAD_DIM,
                                         POOL_WIDTH + part * RET_WIDTH + (h + 1) * RET_HEAD_DIM)
                q = _rope(z_scr[rs, col(0)], cos2, sin2)
                k = _rope(z_scr[rs, col(1)], cos2, sin2) * scale
                v = z_scr[rs, col(2)]
                s_old = s_scr[h]
                scores = _mm(q, k, full, nt=True) * dmat_ref[h]
                qd = q * _head(qdec_ref[...], h)
                o = _mm(scores, v, full) + _mm(qd, s_old, full)
                kd = k * _head(kdec_ref[...], h)
                s_scr[h] = s_old * cdec_ref[h] + _mm(kd.T, v, kv_precise)
                on = _group_norm(o) * _head(gn_ref[...], h)
                ret_scr[rs, h * RET_HEAD_DIM:(h + 1) * RET_HEAD_DIM] = _silu(z_scr[rs, col(3)]) * on

        mix_in = jnp.concatenate([pool_out, ret_scr[...]], axis=-1)
        y_ref[0] = x + _dot(mix_in.astype(BF16), wout_hi_ref[...])
        if full_from < rows:
            m_hi, m_lo = _split(mix_in[full_from:])
            y_ref[0, full_from:, :] += _dot(m_lo, wout_hi_ref[...]) + _dot(m_hi, wout_lo_ref[...])
        sfin_ref[0] = s_scr[...]

    if precise_tail:
        pl.when(c < steps - precise_tail)(lambda: step(False, rows))
        if precise_tail > 1:
            pl.when((c >= steps - precise_tail) & (c < steps - 1))(lambda: step(True, rows))
        pl.when(c == steps - 1)(lambda: step(True, rows - CHUNK))
    else:
        step(False, rows)


def _mixer_sample_kernel(*refs, seq_len, layer, has_prev):
    (x_ref, prev_ref, s0_ref, cos_ref, sin_ref, ng_ref, win_ref, wpool_ref, pscale_ref, gn_ref,
     wout_ref, dmat_ref, qdec_ref, kdec_ref, cdec_ref) = refs[:15]
    y_ref, tail_ref, sfin_ref, ubuf, ret_scr = refs[16 + 2 * has_prev:]
    if not has_prev:
        _zero_other_layers(tail_ref, layer)
        _zero_other_layers(sfin_ref, layer)
        tail_ref, sfin_ref = tail_ref.at[layer], sfin_ref.at[layer]
    ns = SAMPLE_SEQS
    rows = ns * seq_len
    x = x_ref[...].reshape(rows, D_MODEL)
    hn = _rmsnorm(x, ng_ref[...]).astype(BF16)
    z = _dot(hn, win_ref[...])
    ubuf[:, POOL_HIST - POOL_BUF:POOL_HIST, :] = prev_ref[...]
    ubuf[:, POOL_HIST:POOL_HIST + seq_len, :] = z[:, :POOL_WIDTH].reshape(ns, seq_len, POOL_WIDTH)
    pool_out = _pool_mix(ubuf, seq_len, 0, POOL_BUF, (wpool_ref,), pscale_ref[...])
    tail_ref[...] = ubuf[:, seq_len + POOL_HIST - POOL_BUF:seq_len + POOL_HIST, :]

    q_all, k_all, v_all, g_all = _qkvg(z)
    scale = RET_HEAD_DIM ** -0.5
    cos2 = cos_ref[...]
    sin2 = sin_ref[...]
    tok_seq = lax.broadcasted_iota(I32, (RET_HEAD_DIM, rows), 1) // seq_len
    for h in range(RET_HEADS):
        q = _rope(_head(q_all, h), cos2, sin2)
        k = _rope(_head(k_all, h), cos2, sin2) * scale
        vb = _head(v_all, h).astype(BF16)
        s_old = s0_ref[:, h]
        scores = _dot_nt(q.astype(BF16), k.astype(BF16)) * dmat_ref[h]
        qd = (q * _head(qdec_ref[...], h)).astype(BF16).reshape(ns, seq_len, RET_HEAD_DIM)
        o_state = jnp.einsum('bid,bde->bie', qd, s_old.astype(BF16), preferred_element_type=F32)
        o = _dot(scores.astype(BF16), vb) + o_state.reshape(rows, RET_HEAD_DIM)
        kdt = (k * _head(kdec_ref[...], h)).T
        lhs = jnp.concatenate(
            [jnp.where(tok_seq == b, kdt, 0.0).astype(BF16) for b in range(ns)], axis=0)
        upd = _dot(lhs, vb).reshape(ns, RET_HEAD_DIM, RET_HEAD_DIM)
        sfin_ref[:, h] = s_old * cdec_ref[h] + upd
        on = _group_norm(o) * _head(gn_ref[...], h)
        ret_scr[:, h * RET_HEAD_DIM:(h + 1) * RET_HEAD_DIM] = _silu(_head(g_all, h)) * on

    mix_in = jnp.concatenate([pool_out, ret_scr[...]], axis=-1).astype(BF16)
    y_ref[...] = (x + _dot(mix_in, wout_ref[...])).reshape(ns, seq_len, D_MODEL)


def _router_logits(hn, wr_cat_ref, wr_hi_ref, br_ref):
    h_hi, h_lo = _split(hn)
    part = _dot(h_hi, wr_cat_ref[...])
    return part[:, :LANES] + part[:, LANES:] + _dot(h_lo, wr_hi_ref[...]) + br_ref[...]


def _select(group_lg, expert_lg, gidx, eidx, axis):
    red = dict(axis=axis, keepdims=True)
    neg = jnp.float32(-jnp.inf)
    gmax = jnp.max(group_lg, **red)
    g_sel = jnp.min(jnp.where(group_lg == gmax, gidx, N_EXPERT_GROUPS), **red)
    p_sel = 1.0 / jnp.sum(jnp.exp(group_lg - gmax), **red)
    emask = (eidx >> 2) == g_sel
    v1 = jnp.max(jnp.where(emask, expert_lg, neg), **red)
    i1 = jnp.min(jnp.where(emask & (expert_lg == v1), eidx, N_EXPERTS), **red)
    emask2 = emask & (eidx != i1)
    v2 = jnp.max(jnp.where(emask2, expert_lg, neg), **red)
    i2 = jnp.min(jnp.where(emask2 & (expert_lg == v2), eidx, N_EXPERTS), **red)
    e2 = jnp.exp(v2 - v1)
    return g_sel, i1, i2, p_sel / (1.0 + e2), p_sel * e2 / (1.0 + e2)


def _moe_route_kernel(x_ref, ng_ref, wr_cat_ref, wr_hi_ref, br_ref, ha_ref, hb_ref, gate_ref,
                      idx_ref, count_ref, carry_scr):
    i = pl.program_id(0)
    rows = x_ref.shape[0]

    @pl.when(i == 0)
    def _():
        carry_scr[...] = jnp.zeros_like(carry_scr)

    hn = _rmsnorm(x_ref[...], ng_ref[...])
    ha_ref[...], hb_ref[...] = _pack_row(hn)
    lgt = _router_logits(hn, wr_cat_ref, wr_hi_ref, br_ref).T
    neg = jnp.float32(-jnp.inf)
    gidx = lax.broadcasted_iota(I32, (SUBLANES, rows), 0)
    eidx = lax.broadcasted_iota(I32, (N_EXPERTS, rows), 0)
    group_lg = jnp.where(gidx < N_EXPERT_GROUPS, lgt[0:SUBLANES], neg)
    expert_lg = lgt[EXPERT_LANE0:EXPERT_LANE0 + N_EXPERTS]
    g_sel, i1, i2, w1, w2 = _select(group_lg, expert_lg, gidx, eidx, 0)
    lo = jnp.minimum(i1, i2) - g_sel * EXPERTS_PER_GROUP
    hi = jnp.maximum(i1, i2) - g_sel * EXPERTS_PER_GROUP
    cls = g_sel * PAIRS_PER_GROUP + ((lo * (7 - lo)) >> 1) + (hi - lo - 1)
    w_lo = jnp.where(i1 < i2, w1, w2)
    w_hi = jnp.where(i1 < i2, w2, w1)

    crow = lax.broadcasted_iota(I32, (CLASS_ROWS, rows), 0)
    onehot = jnp.where(crow == cls, 1.0, 0.0)
    n_blk = rows // COUNT_BLOCK
    blocks = [onehot[:, j * COUNT_BLOCK:(j + 1) * COUNT_BLOCK] for j in range(n_blk)]
    r = lax.broadcasted_iota(I32, (COUNT_BLOCK, COUNT_BLOCK), 0)
    c = lax.broadcasted_iota(I32, (COUNT_BLOCK, COUNT_BLOCK), 1)
    upper = jnp.where(r < c, 1.0, 0.0).astype(BF16)
    within = _dot(jnp.concatenate(blocks, axis=0).astype(BF16), upper)
    carry = carry_scr[:, 0:1]
    ranks = []
    for j in range(n_blk):
        before = within[j * CLASS_ROWS:(j + 1) * CLASS_ROWS] + carry
        ranks.append(jnp.sum(blocks[j] * before, axis=0, keepdims=True))
        carry = carry + jnp.sum(blocks[j], axis=1, keepdims=True)
    rank = jnp.concatenate(ranks, axis=1)
    carry_scr[...] = jnp.broadcast_to(carry, carry_scr.shape)
    count_ref[...] = carry_scr[...]

    row8 = lax.broadcasted_iota(I32, (SUBLANES, rows), 0)
    idx_ref[...] = jnp.where(row8 == 0, cls, jnp.where(row8 == 1, rank.astype(I32), 0))
    rowl = lax.broadcasted_iota(I32, (LANES, rows), 0)
    gate_ref[...] = jnp.where(rowl == GATE_LO, w_lo, jnp.where(rowl == GATE_HI, w_hi, 0.0)).T


def _moe_pair_kernel(ea_ref, eb_ref, used_ref, ha_ref, hb_ref, gate_ref, wg_a_ref, wu_a_ref,
                     wd_a_ref, wg_b_ref, wu_b_ref, wd_b_ref, after_ref, ya_ref, yb_ref, wgu_scr,
                     wd_scr):
    del after_ref
    i = pl.program_id(0)
    prev = jnp.maximum(i - 1, 0)
    new_class = (i == 0) | (ea_ref[i] != ea_ref[prev]) | (eb_ref[i] != eb_ref[prev])

    @pl.when(new_class)
    def _():
        for s, (wg_ref, wu_ref, wd_ref) in enumerate(((wg_a_ref, wu_a_ref, wd_a_ref),
                                                      (wg_b_ref, wu_b_ref, wd_b_ref))):
            wgu_scr[s, :, :D_EXPERT] = wg_ref[0].astype(BF16)
            wgu_scr[s, :, D_EXPERT:] = wu_ref[0].astype(BF16)
            wd_scr[s] = wd_ref[0].astype(BF16)

    @pl.when(i < used_ref[0])
    def _():
        h = _unpack_row(ha_ref[...], hb_ref[...]).astype(BF16)
        gates = gate_ref[...]
        y = None
        for s, lane in enumerate((GATE_LO, GATE_HI)):
            gu = _dot(h, wgu_scr[s])
            act = _silu(gu[:, :D_EXPERT]) * gu[:, D_EXPERT:]
            part = _dot((act * gates[:, lane:lane + 1]).astype(BF16), wd_scr[s])
            y = part if y is None else y + part
        ya_ref[...], yb_ref[...] = _pack_row(y)


def _final_norm_kernel(x_ref, ya_ref, yb_ref, g_ref, o_ref):
    o_ref[...] = _rmsnorm(x_ref[...] + _unpack_row(ya_ref[...], yb_ref[...]), g_ref[...])


def _moe_dense_kernel(x_ref, ng_ref, wr_cat_ref, wr_hi_ref, br_ref, wg_ref, wu_ref, wd_ref, nf_ref,
                      y_ref, h_scr, gate_scr, acc_scr, *, final_norm):
    e = pl.program_id(1)

    @pl.when(e == 0)
    def _():
        hn = _rmsnorm(x_ref[...], ng_ref[...])
        h_scr[...] = hn.astype(BF16)
        lg = _router_logits(hn, wr_cat_ref, wr_hi_ref, br_ref)
        lane = lax.broadcasted_iota(I32, lg.shape, 1)
        neg = jnp.float32(-jnp.inf)
        group_lg = jnp.where(lane < N_EXPERT_GROUPS, lg, neg)
        is_expert = (lane >= EXPERT_LANE0) & (lane < EXPERT_LANE0 + N_EXPERTS)
        _, i1, i2, w1, w2 = _select(group_lg, jnp.where(is_expert, lg, neg), lane,
                                    jnp.where(is_expert, lane - EXPERT_LANE0, N_EXPERTS), 1)
        gate_scr[...] = (jnp.where(lane == i1 + EXPERT_LANE0, w1, 0.0)
                         + jnp.where(lane == i2 + EXPERT_LANE0, w2, 0.0))
        acc_scr[...] = jnp.zeros_like(acc_scr)

    lane = lax.broadcasted_iota(I32, gate_scr.shape, 1)
    gate = jnp.sum(jnp.where(lane == EXPERT_LANE0 + e, gate_scr[...], 0.0), axis=-1, keepdims=True)
    h = h_scr[...]
    act = _silu(_dot(h, wg_ref[0].astype(BF16))) * _dot(h, wu_ref[0].astype(BF16))
    acc_scr[...] += _dot((act * gate).astype(BF16), wd_ref[0].astype(BF16))

    @pl.when(e == N_EXPERTS - 1)
    def _():
        y = x_ref[...] + acc_scr[...]
        if final_norm:
            y = _rmsnorm(y, nf_ref[...])
        y_ref[...] = y


def _const_spec(shape):
    return pl.BlockSpec(shape, lambda *_: (0,) * len(shape))


def _resident_spec(shape):
    return pl.BlockSpec(shape, lambda *_: (0,) * len(shape), pipeline_mode=pl.Buffered(1))


_ANY_SPEC = pl.BlockSpec(memory_space=pl.ANY)


def _rope_tables(pos):
    half = RET_HEAD_DIM // 2
    inv = np.float32(ROPE_BASE) ** (-np.arange(half, dtype=np.float32) / np.float32(half))
    ang = pos.astype(np.float32)[:, None] * inv[None, :]
    cos, sin = np.cos(ang), np.sin(ang)
    return np.concatenate([cos, cos], -1), np.concatenate([-sin, sin], -1)


def _decay_consts(c, reps):
    f = np.float32
    log_g = np.log(f(1.0) - f(2.0) ** (f(-5.0) - np.arange(RET_HEADS, dtype=f)))
    i = np.arange(c, dtype=f)
    diff = i[:, None] - i[None, :]
    dmat = np.where(diff[None] >= 0, np.exp(np.maximum(diff, f(0))[None] * log_g[:, None, None]), f(0))
    qdec = np.exp((i + f(1))[None, :] * log_g[:, None])
    kdec = np.exp((f(c) - f(1) - i)[None, :] * log_g[:, None])
    cdec = np.exp(f(c) * log_g)
    dmat = np.einsum('ab,hij->haibj', np.eye(reps, dtype=f), dmat).reshape(RET_HEADS, reps * c, reps * c)
    lanes = lambda t: np.repeat(np.tile(t, (1, reps)).T, RET_HEAD_DIM, axis=1)
    cdec = np.broadcast_to(cdec[:, None, None], (RET_HEADS, 1, RET_HEAD_DIM))
    return tuple(np.ascontiguousarray(a, dtype=f) for a in (dmat, lanes(qdec), lanes(kdec), cdec))


def _split_weight(w):
    hi = lax.bitcast_convert_type(_bf16_round_bits(lax.bitcast_convert_type(w, jnp.uint32)), F32)
    return hi.astype(BF16), (w - hi).astype(BF16)


_W_IN_BLOCK = (None, D_MODEL, IN_WIDTH)
_W_POOL_BLOCK = (None, len(POOL_WINDOWS), POOL_GROUP_DIM, POOL_GROUP_DIM)
_W_OUT_BLOCK = (None, D_MODEL, D_MODEL)
_DECAY_SPECS = [
    _const_spec((RET_HEADS, CHUNK, CHUNK)), _const_spec((CHUNK, RET_WIDTH)),
    _const_spec((CHUNK, RET_WIDTH)), _const_spec((RET_HEADS, 1, RET_HEAD_DIM)),
]


def _layer_spec(block, layer, resident=False):
    index_map = lambda *_: (layer,) + (0,) * (len(block) - 1)
    if resident:
        return pl.BlockSpec(block, index_map, pipeline_mode=pl.Buffered(1))
    return pl.BlockSpec(block, index_map)


def _mixer_weight_specs(layer, split):
    n = 2 if split else 1
    row = lambda width: _layer_spec((None, 1, width), layer)
    return [row(D_MODEL), *[_layer_spec(_W_IN_BLOCK, layer, True)] * n,
            *[_layer_spec(_W_POOL_BLOCK, layer)] * n, row(POOL_WIDTH), row(RET_WIDTH),
            *[_layer_spec(_W_OUT_BLOCK, layer, True)] * n]


def _chain_out_spec(block, depth, layer, has_prev):
    zeros = (0,) * (len(block) - 1)
    if has_prev:
        return pl.BlockSpec((None,) + block, lambda i, *_: (layer, i) + zeros)
    return pl.BlockSpec((depth,) + block, lambda i, *_: (0, i) + zeros)


def _mixer_prompt(x, moe_y, weights, prev_out, layer, rows, precise_tail):
    b, l, _ = x.shape
    depth = weights[0].shape[0]
    assert l % rows == 0 and rows % CHUNK == 0
    cos2, sin2 = _rope_tables(np.arange(l))
    decay = _decay_consts(RET_CHUNK, 1)
    steps = l // rows
    tok = lambda i, c: (i, c, 0)
    if moe_y is None:
        ya = yb = jnp.zeros((1, rows, PACK_W), I32)
        y_spec = _const_spec((1, rows, PACK_W))
    else:
        ya, yb = (t.reshape(b, l, PACK_W) for t in moe_y)
        y_spec = pl.BlockSpec((1, rows, PACK_W), tok)
    has_prev = prev_out is not None
    n_in = 18
    y, tails, states = pl.pallas_call(
        functools.partial(_mixer_prompt_kernel, rows=rows, steps=steps, moe_in=moe_y is not None,
                          precise_tail=precise_tail, layer=layer, has_prev=has_prev),
        grid=(b, steps),
        in_specs=[pl.BlockSpec((1, rows, D_MODEL), tok), y_spec, y_spec,
                  pl.BlockSpec((rows, RET_HEAD_DIM), lambda i, c: (c, 0)),
                  pl.BlockSpec((rows, RET_HEAD_DIM), lambda i, c: (c, 0)),
                  *_mixer_weight_specs(layer, True), *_DECAY_SPECS,
                  *([_ANY_SPEC, _ANY_SPEC] if has_prev else [])],
        out_specs=[pl.BlockSpec((1, rows, D_MODEL), tok),
                   _chain_out_spec((1, POOL_BUF, POOL_WIDTH), depth, layer, has_prev),
                   _chain_out_spec((1, RET_HEADS, RET_HEAD_DIM, RET_HEAD_DIM), depth, layer, has_prev)],
        out_shape=[jax.ShapeDtypeStruct(x.shape, F32),
                   jax.ShapeDtypeStruct((depth, b, POOL_BUF, POOL_WIDTH), F32),
                   jax.ShapeDtypeStruct((depth, b, RET_HEADS, RET_HEAD_DIM, RET_HEAD_DIM), F32)],
        input_output_aliases={n_in: 1, n_in + 1: 2} if has_prev else {},
        scratch_shapes=[pltpu.VMEM((1, POOL_HIST + rows, POOL_WIDTH), F32),
                        pltpu.VMEM((RET_HEADS, RET_HEAD_DIM, RET_HEAD_DIM), F32),
                        pltpu.VMEM((rows, RET_WIDTH), F32),
                        pltpu.VMEM((rows, IN_WIDTH), F32)],
        compiler_params=pltpu.CompilerParams(
            dimension_semantics=("arbitrary", "arbitrary"), vmem_limit_bytes=VMEM_LIMIT),
        name="mixer_prompt",
    )(x, ya, yb, cos2, sin2, *weights, *decay, *(prev_out if has_prev else ()))
    return y, (tails, states)


def _mixer_sample(x, pool_prev, s0, weights, prev_out, layer, after):
    b, l, _ = x.shape
    depth = s0.shape[0]
    ns = SAMPLE_SEQS
    assert ns * l == CHUNK and b % ns == 0
    cos2, sin2 = _rope_tables(PAST_LEN + np.arange(l))
    cos2, sin2 = np.tile(cos2, (ns, 1)), np.tile(sin2, (ns, 1))
    decay = _decay_consts(l, ns)
    seq3 = lambda i: (i, 0, 0)
    state_block = (ns, RET_HEADS, RET_HEAD_DIM, RET_HEAD_DIM)
    tail_block = (ns, POOL_BUF, POOL_WIDTH)
    has_prev = prev_out is not None
    n_in = 16
    y, tails, states = pl.pallas_call(
        functools.partial(_mixer_sample_kernel, seq_len=l, layer=layer, has_prev=has_prev),
        grid=(b // ns,),
        in_specs=[pl.BlockSpec((ns, l, D_MODEL), seq3),
                  pl.BlockSpec((None,) + tail_block, lambda i: (layer, i, 0, 0)),
                  pl.BlockSpec((None,) + state_block, lambda i: (layer, i, 0, 0, 0)),
                  _const_spec((CHUNK, RET_HEAD_DIM)), _const_spec((CHUNK, RET_HEAD_DIM)),
                  *_mixer_weight_specs(layer, False), *_DECAY_SPECS, _ANY_SPEC,
                  *([_ANY_SPEC, _ANY_SPEC] if has_prev else [])],
        out_specs=[pl.BlockSpec((ns, l, D_MODEL), seq3),
                   _chain_out_spec(tail_block, depth, layer, has_prev),
                   _chain_out_spec(state_block, depth, layer, has_prev)],
        out_shape=[jax.ShapeDtypeStruct(x.shape, F32),
                   jax.ShapeDtypeStruct((depth, b, POOL_BUF, POOL_WIDTH), F32),
                   jax.ShapeDtypeStruct(s0.shape, F32)],
        input_output_aliases={n_in: 1, n_in + 1: 2} if has_prev else {},
        scratch_shapes=[pltpu.VMEM((ns, POOL_HIST + l, POOL_WIDTH), F32),
                        pltpu.VMEM((CHUNK, RET_WIDTH), F32)],
        compiler_params=pltpu.CompilerParams(
            dimension_semantics=("arbitrary",), vmem_limit_bytes=VMEM_LIMIT),
        name="mixer_sample",
    )(x, pool_prev, s0, cos2, sin2, *weights, *decay, after, *(prev_out if has_prev else ()))
    return y, (tails, states)


def _router_weights(w_rg, b_rg, w_re, b_re):
    depth = w_rg.shape[0]
    gap = EXPERT_LANE0 - N_EXPERT_GROUPS
    rest = LANES - EXPERT_LANE0 - N_EXPERTS
    wr = jnp.concatenate([w_rg, jnp.zeros((depth, D_MODEL, gap), F32), w_re,
                          jnp.zeros((depth, D_MODEL, rest), F32)], axis=-1)
    br = jnp.concatenate([b_rg, jnp.zeros((depth, gap), F32), b_re, jnp.zeros((depth, rest), F32)],
                         axis=-1).reshape(depth, 1, LANES)
    wr_hi, wr_lo = _split_weight(wr)
    return jnp.concatenate([wr_hi, wr_lo], axis=-1), wr_hi, br


def _router_specs(layer):
    return [_layer_spec((None, D_MODEL, 2 * LANES), layer), _layer_spec((None, D_MODEL, LANES), layer),
            _layer_spec((None, 1, LANES), layer)]


def _sc_mesh():
    return plsc.VectorSubcoreMesh(core_axis_name="core", subcore_axis_name="subcore",
                                  num_cores=SC_CORES, num_subcores=SC_SUBCORES)


def _sc_params():
    params = pltpu.CompilerParams()
    if "needs_layout_passes" in pltpu.CompilerParams.__dataclass_fields__:
        params = dataclasses.replace(params, needs_layout_passes=False)
    return params


def _sc_gather(tables, idx):
    n = idx.shape[0]
    assert n % SC_WINDOW == 0
    nt = len(tables)

    def body(*refs):
        i_hbm = refs[nt]
        for t_hbm, o_hbm in zip(refs[:nt], refs[nt + 1:]):
            def gather_window(i_vmem, o_vmem, t_hbm=t_hbm):
                pltpu.sync_copy(t_hbm.at[i_vmem.at[0]], o_vmem)

            pltpu.emit_pipeline(
                gather_window, grid=(n // SC_WINDOW,),
                in_specs=[pl.BlockSpec((1, SC_WINDOW), lambda i: (0, i))],
                out_specs=[pl.BlockSpec((SC_WINDOW, t_hbm.shape[1]), lambda i: (i, 0))],
                core_axis_name=("core", "subcore"),
                dimension_semantics=(pltpu.PARALLEL,),
            )(i_hbm, o_hbm)

    out_type = tuple(jax.ShapeDtypeStruct((n, t.shape[1]), t.dtype) for t in tables)
    return pl.kernel(body, out_type=out_type, mesh=_sc_mesh(), name="sc_gather")(
        *tables, idx.reshape(1, n))


def _sc_slots(cls, rank, starts, n_slots):
    t = cls.shape[0]
    workers = SC_CORES * SC_SUBCORES
    slot_per, tok_per = n_slots // workers, t // workers
    assert n_slots % (workers * SC_LANES) == 0 and t % (workers * SC_LANES) == 0 and t & (t - 1) == 0
    assert n_slots % (SC_LANES * SC_UNROLL) == 0 and t % (SC_LANES * SC_UNROLL) == 0

    def body(cls_hbm, rank_hbm, starts_hbm, pos_hbm, slot_hbm, cls_v, rank_v, starts_v, pos_v, slot_v):
        wid = lax.axis_index("subcore") * SC_CORES + lax.axis_index("core")
        pltpu.sync_copy(cls_hbm, cls_v)
        pltpu.sync_copy(rank_hbm, rank_v)
        pltpu.sync_copy(starts_hbm, starts_v)

        lane = lax.iota(I32, SC_LANES)
        span = SC_LANES * SC_UNROLL

        @pl.loop(0, n_slots, step=span)
        def _(i):
            for u in range(SC_UNROLL):
                j = i + u * SC_LANES
                slot_v[pl.ds(j, SC_LANES)] = (lane + j) & (t - 1)

        @pl.loop(0, t, step=span)
        def _(i):
            for u in range(SC_UNROLL):
                j = i + u * SC_LANES
                at = pl.ds(j, SC_LANES)
                pos = plsc.load_gather(starts_v, [cls_v[at]]) + rank_v[at]
                pos_v[at] = pos
                plsc.store_scatter(slot_v, [pos], lane + j)

        tok_off = pl.multiple_of(wid * tok_per, SC_LANES)
        pltpu.sync_copy(pos_v.at[pl.ds(tok_off, tok_per)], pos_hbm.at[pl.ds(tok_off, tok_per)])
        slot_off = pl.multiple_of(wid * slot_per, SC_LANES)
        pltpu.sync_copy(slot_v.at[pl.ds(slot_off, slot_per)], slot_hbm.at[pl.ds(slot_off, slot_per)])

    return pl.kernel(
        body, mesh=_sc_mesh(), compiler_params=_sc_params(), name="sc_slots",
        out_type=(jax.ShapeDtypeStruct((t,), I32), jax.ShapeDtypeStruct((n_slots,), I32)),
        scratch_types=[pltpu.VMEM((t,), I32), pltpu.VMEM((t,), I32), pltpu.VMEM((CLASS_ROWS,), I32),
                       pltpu.VMEM((t,), I32), pltpu.VMEM((n_slots,), I32)],
    )(cls, rank, starts)


def _moe_route(x, norm_g, router, layer):
    t = x.shape[0]
    tok = lambda i: (i, 0)
    return pl.pallas_call(
        _moe_route_kernel,
        grid=(t // ROUTE_TILE,),
        in_specs=[pl.BlockSpec((ROUTE_TILE, D_MODEL), tok), _layer_spec((None, 1, D_MODEL), layer),
                  *_router_specs(layer)],
        out_specs=[pl.BlockSpec((ROUTE_TILE, PACK_W), tok), pl.BlockSpec((ROUTE_TILE, PACK_W), tok),
                   pl.BlockSpec((ROUTE_TILE, LANES), tok),
                   pl.BlockSpec((SUBLANES, ROUTE_TILE), lambda i: (0, i)),
                   _const_spec((CLASS_ROWS, LANES))],
        out_shape=[jax.ShapeDtypeStruct((t, PACK_W), I32), jax.ShapeDtypeStruct((t, PACK_W), I32),
                   jax.ShapeDtypeStruct((t, LANES), F32), jax.ShapeDtypeStruct((SUBLANES, t), I32),
                   jax.ShapeDtypeStruct((CLASS_ROWS, LANES), F32)],
        scratch_shapes=[pltpu.VMEM((CLASS_ROWS, LANES), F32)],
        compiler_params=pltpu.CompilerParams(
            dimension_semantics=("arbitrary",), vmem_limit_bytes=VMEM_LIMIT),
        name="moe_route",
    )(x, norm_g, *router)


def _moe_experts(routed, w_gate, w_up, w_down, layer, after):
    ha, hb, gates, idx, counts = routed
    t = ha.shape[0]
    n_slots = t + N_CLASSES * PAIR_TILE
    n_tiles = n_slots // PAIR_TILE

    cnt = counts[:, 0].astype(I32)
    padded = (cnt + PAIR_TILE - 1) // PAIR_TILE * PAIR_TILE
    ends = jnp.cumsum(padded)
    starts = ends - padded
    tile_cls = jnp.minimum(
        jnp.sum(ends[None, :N_CLASSES] <= (jnp.arange(n_tiles, dtype=I32) * PAIR_TILE)[:, None], axis=1),
        N_CLASSES - 1).astype(I32)
    first = (tile_cls // PAIRS_PER_GROUP) * EXPERTS_PER_GROUP
    tile_ea = first + jnp.asarray(PAIR_LO, I32)[tile_cls % PAIRS_PER_GROUP]
    tile_eb = first + jnp.asarray(PAIR_HI, I32)[tile_cls % PAIRS_PER_GROUP]
    used = (ends[N_CLASSES - 1:N_CLASSES] // PAIR_TILE).astype(I32)

    pos, slot_tok = _sc_slots(idx[0], idx[1], starts, n_slots)
    hsa, hsb, gate_s = _sc_gather((ha, hb, gates), slot_tok)

    row = lambda i, ea, eb, nu: (jnp.minimum(i, nu[0] - 1), 0)
    w_spec = lambda shape, which: pl.BlockSpec(
        (None, 1) + shape, lambda i, ea, eb, nu: (layer, (ea, eb)[which][i], 0, 0))
    gu_shape, d_shape = (D_MODEL, D_EXPERT), (D_EXPERT, D_MODEL)
    ysa, ysb = pl.pallas_call(
        _moe_pair_kernel,
        grid_spec=pltpu.PrefetchScalarGridSpec(
            num_scalar_prefetch=3, grid=(n_tiles,),
            in_specs=[pl.BlockSpec((PAIR_TILE, PACK_W), row), pl.BlockSpec((PAIR_TILE, PACK_W), row),
                      pl.BlockSpec((PAIR_TILE, LANES), row),
                      w_spec(gu_shape, 0), w_spec(gu_shape, 0), w_spec(d_shape, 0),
                      w_spec(gu_shape, 1), w_spec(gu_shape, 1), w_spec(d_shape, 1), _ANY_SPEC],
            out_specs=[pl.BlockSpec((PAIR_TILE, PACK_W), row), pl.BlockSpec((PAIR_TILE, PACK_W), row)],
            scratch_shapes=[pltpu.VMEM((2, D_MODEL, 2 * D_EXPERT), BF16),
                            pltpu.VMEM((2, D_EXPERT, D_MODEL), BF16)]),
        out_shape=[jax.ShapeDtypeStruct((n_slots, PACK_W), I32),
                   jax.ShapeDtypeStruct((n_slots, PACK_W), I32)],
        compiler_params=pltpu.CompilerParams(
            dimension_semantics=("arbitrary",), vmem_limit_bytes=VMEM_LIMIT),
        name="moe_pair",
    )(tile_ea, tile_eb, used, hsa, hsb, gate_s, w_gate, w_up, w_down, w_gate, w_up, w_down, after)
    return _sc_gather((ysa, ysb), pos)


def _final_norm(x, moe_y, g, rows):
    t = x.shape[0]
    tok = lambda i: (i, 0)
    return pl.pallas_call(
        _final_norm_kernel,
        grid=(t // rows,),
        in_specs=[pl.BlockSpec((rows, D_MODEL), tok), pl.BlockSpec((rows, PACK_W), tok),
                  pl.BlockSpec((rows, PACK_W), tok), _const_spec((1, D_MODEL))],
        out_specs=pl.BlockSpec((rows, D_MODEL), tok),
        out_shape=jax.ShapeDtypeStruct(x.shape, F32),
        compiler_params=pltpu.CompilerParams(
            dimension_semantics=("arbitrary",), vmem_limit_bytes=VMEM_LIMIT),
        name="final_norm",
    )(x, *moe_y, g.reshape(1, D_MODEL))


def _moe_dense(x, norm_g, router, w_gate, w_up, w_down, norm_final, layer, final_norm, rows):
    t = x.shape[0]
    assert t % rows == 0
    tok = lambda i, e: (i, 0)
    w_spec = lambda shape: pl.BlockSpec((None, 1) + shape, lambda i, e: (layer, e, 0, 0))
    return pl.pallas_call(
        functools.partial(_moe_dense_kernel, final_norm=final_norm),
        grid=(t // rows, N_EXPERTS),
        in_specs=[pl.BlockSpec((rows, D_MODEL), tok), _layer_spec((None, 1, D_MODEL), layer),
                  *_router_specs(layer),
                  w_spec((D_MODEL, D_EXPERT)), w_spec((D_MODEL, D_EXPERT)), w_spec((D_EXPERT, D_MODEL)),
                  _const_spec((1, D_MODEL))],
        out_specs=pl.BlockSpec((rows, D_MODEL), tok),
        out_shape=jax.ShapeDtypeStruct(x.shape, F32),
        scratch_shapes=[pltpu.VMEM((rows, D_MODEL), BF16),
                        pltpu.VMEM((rows, LANES), F32),
                        pltpu.VMEM((rows, D_MODEL), F32)],
        compiler_params=pltpu.CompilerParams(
            dimension_semantics=("arbitrary", "arbitrary"), vmem_limit_bytes=VMEM_LIMIT),
        name="moe_dense",
    )(x, norm_g, *router, w_gate, w_up, w_down, norm_final.reshape(1, D_MODEL))


def kernel(x_prompt, x_sample, cache_pool, state_ret, norm_mix, w_in, w_pool, pool_scale, ret_gn, w_out, norm_ffn, w_router_group, b_router_group, w_router_expert, b_router_expert, w_gate, w_up, w_down, norm_final):
    depth = norm_mix.shape[0]
    row = lambda a: a.reshape(depth, 1, a.shape[-1])
    mix_split = (row(norm_mix), *_split_weight(w_in), *_split_weight(w_pool), row(pool_scale),
                 row(ret_gn), *_split_weight(w_out))
    mix_hi = tuple(mix_split[i] for i in (0, 1, 3, 5, 6, 7))
    router = _router_weights(w_router_group, b_router_group, w_router_expert, b_router_expert)
    norm_ffn = row(norm_ffn)

    yp, ys = x_prompt, x_sample
    moe_p = None
    out_p = out_s = None
    for l in range(depth):
        yp, out_p = _mixer_prompt(yp, moe_p, mix_split, out_p, l, rows=512,
                                  precise_tail=PRECISE_TAIL_STEPS if l < depth - 1 else 0)
        routed = _moe_route(yp.reshape(-1, D_MODEL), norm_ffn, router, l)
        ys, out_s = _mixer_sample(ys, cache_pool, state_ret, mix_hi, out_s, l, after=routed[-1])
        ys = _moe_dense(ys.reshape(-1, D_MODEL), norm_ffn, router, w_gate, w_up, w_down, norm_final,
                        l, l == depth - 1, rows=1024).reshape(ys.shape)
        moe_p = _moe_experts(routed, w_gate, w_up, w_down, l, after=ys)
    yp = _final_norm(yp.reshape(-1, D_MODEL), moe_p, norm_final, rows=1024).reshape(yp.shape)
    return (yp, ys, *out_p, *out_s)
```

```python
import dataclasses
import functools

import jax
import jax.numpy as jnp
import numpy as np
from jax import lax
from jax.experimental import pallas as pl
from jax.experimental.pallas import tpu as pltpu
from jax.experimental.pallas import tpu_sc as plsc

F32 = jnp.float32
BF16 = jnp.bfloat16
I32 = jnp.int32

D_MODEL = 1024
POOL_WIDTH = 512
POOL_WINDOWS = (2, 4, 8, 16)
POOL_GROUP_DIM = 128
POOL_BUF = 15
POOL_HIST = 16
RET_WIDTH = 512
RET_HEADS = 4
RET_HEAD_DIM = 128
RET_CHUNK = 128
ROPE_BASE = 10000.0
IN_WIDTH = POOL_WIDTH + 4 * RET_WIDTH
N_EXPERT_GROUPS = 4
EXPERTS_PER_GROUP = 4
N_EXPERTS = 16
D_EXPERT = 256
RMS_EPS = 1e-6
GN_EPS = 1e-5
PAST_LEN = 16384

LANES = 128
SUBLANES = 8
EXPERT_LANE0 = 8
PAIRS_PER_GROUP = 6
N_CLASSES = N_EXPERT_GROUPS * PAIRS_PER_GROUP
CLASS_ROWS = 32
PAIR_LO = (0, 0, 0, 1, 1, 2)
PAIR_HI = (1, 2, 3, 2, 3, 3)
GATE_LO, GATE_HI = 0, 1
ROUTE_TILE = 1024
COUNT_BLOCK = 256
PAIR_TILE = 256
PACK_W = D_MODEL // 4
SC_CORES, SC_SUBCORES, SC_LANES = 2, 16, 16
SC_WINDOW = 128
SC_UNROLL = 8
PRECISE_TAIL_STEPS = 1
CHUNK = 128
SAMPLE_SEQS = 16
VMEM_LIMIT = 56 * 1024 * 1024


def _dot(a, b):
    return jnp.dot(a, b, preferred_element_type=F32)


def _dot_nt(a, b):
    return lax.dot_general(a, b, (((1,), (1,)), ((), ())), preferred_element_type=F32)


def _bf16_round_bits(u):
    return (u + jnp.uint32(0x7FFF) + ((u >> 16) & jnp.uint32(1))) & jnp.uint32(0xFFFF0000)


def _split(a):
    hi = pltpu.bitcast(_bf16_round_bits(pltpu.bitcast(a, jnp.uint32)), F32)
    return hi.astype(BF16), (a - hi).astype(BF16)


def _mm(a, b, precise, nt=False):
    dot = _dot_nt if nt else _dot
    if precise:
        b_hi, b_lo = b if isinstance(b, tuple) else _split(b)
        a_hi, a_lo = _split(a)
        return dot(a_hi, b_hi) + dot(a_lo, b_hi) + dot(a_hi, b_lo)
    return dot(a.astype(BF16), b[0] if isinstance(b, tuple) else b.astype(BF16))


def _rmsnorm(x, g):
    ms = jnp.mean(x * x, axis=-1, keepdims=True)
    return x * lax.rsqrt(ms + RMS_EPS) * g


def _pool_mix(ubuf, rows, t_first, n_prev, wpool_refs, pscale, precise=False, row0=0):
    ns = ubuf.shape[0]
    t = t_first + lax.broadcasted_iota(I32, (1, rows, POOL_GROUP_DIM), 1)
    base = POOL_HIST + row0
    outs = []
    for j, w in enumerate(POOL_WINDOWS):
        lanes = slice(j * POOL_GROUP_DIM, (j + 1) * POOL_GROUP_DIM)
        uj = ubuf[:, base:base + rows, lanes]
        acc = uj
        for i in range(1, w):
            acc = acc + ubuf[:, base - i:base - i + rows, lanes]
        cnt = jnp.minimum(w, n_prev + t + 1).astype(F32)
        d = (acc / cnt - uj).reshape(ns * rows, POOL_GROUP_DIM)
        outs.append(_mm(d, tuple(w[j] for w in wpool_refs), precise))
    return jnp.concatenate(outs, axis=-1) * pscale


def _rope(xh, cos2, sin2):
    return xh * cos2 + pltpu.roll(xh, RET_HEAD_DIM // 2, 1) * sin2


def _group_norm(o):
    mu = jnp.mean(o, axis=-1, keepdims=True)
    c = o - mu
    var = jnp.mean(c * c, axis=-1, keepdims=True)
    return c * lax.rsqrt(var + GN_EPS)


def _silu(x):
    return x * (1.0 / (1.0 + jnp.exp(-x)))


def _head(a, h):
    return a[:, h * RET_HEAD_DIM:(h + 1) * RET_HEAD_DIM]


def _qkvg(z):
    p, r = POOL_WIDTH, RET_WIDTH
    return z[:, p:p + r], z[:, p + r:p + 2 * r], z[:, p + 2 * r:p + 3 * r], z[:, p + 3 * r:p + 4 * r]


def _pack_bf16_pair(a, b):
    ua = pltpu.bitcast(a.astype(BF16).astype(F32), jnp.uint32)
    ub = pltpu.bitcast(b.astype(BF16).astype(F32), jnp.uint32)
    return pltpu.bitcast((ua >> 16) | (ub & jnp.uint32(0xFFFF0000)), I32)


def _unpack_bf16_pair(w):
    u = pltpu.bitcast(w, jnp.uint32)
    return pltpu.bitcast(u << 16, F32), pltpu.bitcast(u & jnp.uint32(0xFFFF0000), F32)


def _pack_row(y):
    q = PACK_W
    return _pack_bf16_pair(y[:, 0:q], y[:, q:2 * q]), _pack_bf16_pair(y[:, 2 * q:3 * q], y[:, 3 * q:])


def _unpack_row(wa, wb):
    return jnp.concatenate([*_unpack_bf16_pair(wa), *_unpack_bf16_pair(wb)], axis=-1)


def _zero_other_layers(ref, layer):
    for j in range(ref.shape[0]):
        if j != layer:
            ref[j] = jnp.zeros(ref.shape[1:], ref.dtype)


def _mixer_prompt_kernel(*refs, rows, steps, moe_in, precise_tail, layer, has_prev):
    (x_ref, ya_ref, yb_ref, cos_ref, sin_ref, ng_ref, win_hi_ref, win_lo_ref, wpool_hi_ref,
     wpool_lo_ref, pscale_ref, gn_ref, wout_hi_ref, wout_lo_ref, dmat_ref, qdec_ref, kdec_ref,
     cdec_ref) = refs[:18]
    y_ref, tail_ref, sfin_ref, ubuf, s_scr, ret_scr, z_scr = refs[19 + 2 * has_prev:]
    c = pl.program_id(1)
    if not has_prev:
        _zero_other_layers(tail_ref, layer)
        _zero_other_layers(sfin_ref, layer)
        tail_ref, sfin_ref = tail_ref.at[layer], sfin_ref.at[layer]

    @pl.when(c == 0)
    def _():
        ubuf[:, 0:POOL_HIST, :] = jnp.zeros((1, POOL_HIST, POOL_WIDTH), F32)
        s_scr[...] = jnp.zeros_like(s_scr)

    kv_cols = slice(POOL_WIDTH + RET_WIDTH, POOL_WIDTH + 3 * RET_WIDTH)

    def step(kv_precise, full_from):
        x = x_ref[0]
        if moe_in:
            x = x + _unpack_row(ya_ref[0], yb_ref[0])
        hn = _rmsnorm(x, ng_ref[...])
        hi, lo = _split(hn) if kv_precise else (hn.astype(BF16), None)
        z_scr[...] = _dot(hi, win_hi_ref[...])
        if kv_precise and full_from:
            z_scr[:full_from, kv_cols] += (_dot(lo[:full_from], win_hi_ref[:, kv_cols])
                                           + _dot(hi[:full_from], win_lo_ref[:, kv_cols]))
        if full_from < rows:
            z_scr[full_from:, :] += (_dot(lo[full_from:], win_hi_ref[...])
                                     + _dot(hi[full_from:], win_lo_ref[...]))

        ubuf[0, POOL_HIST:POOL_HIST + rows, :] = z_scr[:, :POOL_WIDTH]
        pool_w = (wpool_hi_ref, wpool_lo_ref)
        pool_parts = []
        if full_from:
            pool_parts.append(_pool_mix(ubuf, full_from, c * rows, 0, pool_w, pscale_ref[...]))
        if full_from < rows:
            pool_parts.append(_pool_mix(ubuf, rows - full_from, c * rows + full_from, 0, pool_w,
                                        pscale_ref[...], precise=True, row0=full_from))
        pool_out = jnp.concatenate(pool_parts, axis=0)
        tail_ref[...] = ubuf[:, rows + POOL_HIST - POOL_BUF:rows + POOL_HIST, :]
        ubuf[:, 0:POOL_HIST, :] = ubuf[:, rows:rows + POOL_HIST, :]

        scale = RET_HEAD_DIM ** -0.5
        for ci in range(rows // CHUNK):
            rs = slice(ci * CHUNK, (ci + 1) * CHUNK)
            full = ci * CHUNK >= full_from
            cos2 = cos_ref[rs, :]
            sin2 = sin_ref[rs, :]
            for h in range(RET_HEADS):
                col = lambda part: slice(POOL_WIDTH + part * RET_WIDTH + h * RET_HEAD_DIM,
                                         POOL_WIDTH + part * RET_WIDTH + (h + 1) * RET_HEAD_DIM)
                q = _rope(z_scr[rs, col(0)], cos2, sin2)
                k = _rope(z_scr[rs, col(1)], cos2, sin2) * scale
                v = z_scr[rs, col(2)]
                s_old = s_scr[h]
                scores = _mm(q, k, full, nt=True) * dmat_ref[h]
                qd = q * _head(qdec_ref[...], h)
                o = _mm(scores, v, full) + _mm(qd, s_old, full)
                kd = k * _head(kdec_ref[...], h)
                s_scr[h] = s_old * cdec_ref[h] + _mm(kd.T, v, kv_precise)
                on = _group_norm(o) * _head(gn_ref[...], h)
                ret_scr[rs, h * RET_HEAD_DIM:(h + 1) * RET_HEAD_DIM] = _silu(z_scr[rs, col(3)]) * on

        mix_in = jnp.concatenate([pool_out, ret_scr[...]], axis=-1)
        y_ref[0] = x + _dot(mix_in.astype(BF16), wout_hi_ref[...])
        if full_from < rows:
            m_hi, m_lo = _split(mix_in[full_from:])
            y_ref[0, full_from:, :] += _dot(m_lo, wout_hi_ref[...]) + _dot(m_hi, wout_lo_ref[...])
        sfin_ref[0] = s_scr[...]

    if precise_tail:
        pl.when(c < steps - precise_tail)(lambda: step(False, rows))
        if precise_tail > 1:
            pl.when((c >= steps - precise_tail) & (c < steps - 1))(lambda: step(True, rows))
        pl.when(c == steps - 1)(lambda: step(True, rows - CHUNK))
    else:
        step(False, rows)


def _mixer_sample_kernel(*refs, seq_len, layer, has_prev):
    (x_ref, prev_ref, s0_ref, cos_ref, sin_ref, ng_ref, win_ref, wpool_ref, pscale_ref, gn_ref,
     wout_ref, dmat_ref, qdec_ref, kdec_ref, cdec_ref) = refs[:15]
    y_ref, tail_ref, sfin_ref, ubuf, ret_scr = refs[16 + 2 * has_prev:]
    if not has_prev:
        _zero_other_layers(tail_ref, layer)
        _zero_other_layers(sfin_ref, layer)
        tail_ref, sfin_ref = tail_ref.at[layer], sfin_ref.at[layer]
    ns = SAMPLE_SEQS
    rows = ns * seq_len
    x = x_ref[...].reshape(rows, D_MODEL)
    hn = _rmsnorm(x, ng_ref[...]).astype(BF16)
    z = _dot(hn, win_ref[...])
    for j in range(POOL_BUF):
        ubuf[:, POOL_HIST - POOL_BUF + j, :] = prev_ref[j]
    ubuf[:, POOL_HIST:POOL_HIST + seq_len, :] = z[:, :POOL_WIDTH].reshape(ns, seq_len, POOL_WIDTH)
    pool_out = _pool_mix(ubuf, seq_len, 0, POOL_BUF, (wpool_ref,), pscale_ref[...])
    for j in range(POOL_BUF):
        tail_ref[j] = ubuf[:, seq_len + POOL_HIST - POOL_BUF + j, :]

    q_all, k_all, v_all, g_all = _qkvg(z)
    scale = RET_HEAD_DIM ** -0.5
    cos2 = cos_ref[...]
    sin2 = sin_ref[...]
    tok_seq = lax.broadcasted_iota(I32, (RET_HEAD_DIM, rows), 1) // seq_len
    for h in range(RET_HEADS):
        q = _rope(_head(q_all, h), cos2, sin2)
        k = _rope(_head(k_all, h), cos2, sin2) * scale
        vb = _head(v_all, h).astype(BF16)
        s_old = s0_ref[:, h]
        scores = _dot_nt(q.astype(BF16), k.astype(BF16)) * dmat_ref[h]
        qd = (q * _head(qdec_ref[...], h)).astype(BF16).reshape(ns, seq_len, RET_HEAD_DIM)
        o_state = jnp.einsum('bid,bde->bie', qd, s_old.astype(BF16), preferred_element_type=F32)
        o = _dot(scores.astype(BF16), vb) + o_state.reshape(rows, RET_HEAD_DIM)
        kdt = (k * _head(kdec_ref[...], h)).T
        lhs = jnp.concatenate(
            [jnp.where(tok_seq == b, kdt, 0.0).astype(BF16) for b in range(ns)], axis=0)
        upd = _dot(lhs, vb).reshape(ns, RET_HEAD_DIM, RET_HEAD_DIM)
        sfin_ref[:, h] = s_old * cdec_ref[h] + upd
        on = _group_norm(o) * _head(gn_ref[...], h)
        ret_scr[:, h * RET_HEAD_DIM:(h + 1) * RET_HEAD_DIM] = _silu(_head(g_all, h)) * on

    mix_in = jnp.concatenate([pool_out, ret_scr[...]], axis=-1).astype(BF16)
    y_ref[...] = (x + _dot(mix_in, wout_ref[...])).reshape(ns, seq_len, D_MODEL)


def _router_logits(hn, wr_cat_ref, wr_hi_ref, br_ref):
    h_hi, h_lo = _split(hn)
    part = _dot(h_hi, wr_cat_ref[...])
    return part[:, :LANES] + part[:, LANES:] + _dot(h_lo, wr_hi_ref[...]) + br_ref[...]


def _select(group_lg, expert_lg, gidx, eidx, axis):
    red = dict(axis=axis, keepdims=True)
    neg = jnp.float32(-jnp.inf)
    gmax = jnp.max(group_lg, **red)
    g_sel = jnp.min(jnp.where(group_lg == gmax, gidx, N_EXPERT_GROUPS), **red)
    p_sel = 1.0 / jnp.sum(jnp.exp(group_lg - gmax), **red)
    emask = (eidx >> 2) == g_sel
    v1 = jnp.max(jnp.where(emask, expert_lg, neg), **red)
    i1 = jnp.min(jnp.where(emask & (expert_lg == v1), eidx, N_EXPERTS), **red)
    emask2 = emask & (eidx != i1)
    v2 = jnp.max(jnp.where(emask2, expert_lg, neg), **red)
    i2 = jnp.min(jnp.where(emask2 & (expert_lg == v2), eidx, N_EXPERTS), **red)
    e2 = jnp.exp(v2 - v1)
    return g_sel, i1, i2, p_sel / (1.0 + e2), p_sel * e2 / (1.0 + e2)


def _moe_route_kernel(x_ref, ng_ref, wr_cat_ref, wr_hi_ref, br_ref, ha_ref, hb_ref, gate_ref,
                      idx_ref, count_ref, carry_scr):
    i = pl.program_id(0)
    rows = x_ref.shape[0]

    @pl.when(i == 0)
    def _():
        carry_scr[...] = jnp.zeros_like(carry_scr)

    hn = _rmsnorm(x_ref[...], ng_ref[...])
    ha_ref[...], hb_ref[...] = _pack_row(hn)
    lgt = _router_logits(hn, wr_cat_ref, wr_hi_ref, br_ref).T
    neg = jnp.float32(-jnp.inf)
    gidx = lax.broadcasted_iota(I32, (SUBLANES, rows), 0)
    eidx = lax.broadcasted_iota(I32, (N_EXPERTS, rows), 0)
    group_lg = jnp.where(gidx < N_EXPERT_GROUPS, lgt[0:SUBLANES], neg)
    expert_lg = lgt[EXPERT_LANE0:EXPERT_LANE0 + N_EXPERTS]
    g_sel, i1, i2, w1, w2 = _select(group_lg, expert_lg, gidx, eidx, 0)
    lo = jnp.minimum(i1, i2) - g_sel * EXPERTS_PER_GROUP
    hi = jnp.maximum(i1, i2) - g_sel * EXPERTS_PER_GROUP
    cls = g_sel * PAIRS_PER_GROUP + ((lo * (7 - lo)) >> 1) + (hi - lo - 1)
    w_lo = jnp.where(i1 < i2, w1, w2)
    w_hi = jnp.where(i1 < i2, w2, w1)

    crow = lax.broadcasted_iota(I32, (CLASS_ROWS, rows), 0)
    onehot = jnp.where(crow == cls, 1.0, 0.0)
    n_blk = rows // COUNT_BLOCK
    blocks = [onehot[:, j * COUNT_BLOCK:(j + 1) * COUNT_BLOCK] for j in range(n_blk)]
    r = lax.broadcasted_iota(I32, (COUNT_BLOCK, COUNT_BLOCK), 0)
    c = lax.broadcasted_iota(I32, (COUNT_BLOCK, COUNT_BLOCK), 1)
    upper = jnp.where(r < c, 1.0, 0.0).astype(BF16)
    within = _dot(jnp.concatenate(blocks, axis=0).astype(BF16), upper)
    carry = carry_scr[:, 0:1]
    ranks = []
    for j in range(n_blk):
        before = within[j * CLASS_ROWS:(j + 1) * CLASS_ROWS] + carry
        ranks.append(jnp.sum(blocks[j] * before, axis=0, keepdims=True))
        carry = carry + jnp.sum(blocks[j], axis=1, keepdims=True)
    rank = jnp.concatenate(ranks, axis=1)
    carry_scr[...] = jnp.broadcast_to(carry, carry_scr.shape)
    count_ref[...] = carry_scr[...]

    row8 = lax.broadcasted_iota(I32, (SUBLANES, rows), 0)
    idx_ref[...] = jnp.where(row8 == 0, cls, jnp.where(row8 == 1, rank.astype(I32), 0))
    rowl = lax.broadcasted_iota(I32, (LANES, rows), 0)
    gate_ref[...] = jnp.where(rowl == GATE_LO, w_lo, jnp.where(rowl == GATE_HI, w_hi, 0.0)).T


def _moe_pair_kernel(ea_ref, eb_ref, used_ref, ha_ref, hb_ref, gate_ref, wg_a_ref, wu_a_ref,
                     wd_a_ref, wg_b_ref, wu_b_ref, wd_b_ref, after_ref, ya_ref, yb_ref, wgu_scr,
                     wd_scr):
    del after_ref
    i = pl.program_id(0)
    prev = jnp.maximum(i - 1, 0)
    new_class = (i == 0) | (ea_ref[i] != ea_ref[prev]) | (eb_ref[i] != eb_ref[prev])

    @pl.when(new_class)
    def _():
        for s, (wg_ref, wu_ref, wd_ref) in enumerate(((wg_a_ref, wu_a_ref, wd_a_ref),
                                                      (wg_b_ref, wu_b_ref, wd_b_ref))):
            wgu_scr[s, :, :D_EXPERT] = wg_ref[0].astype(BF16)
            wgu_scr[s, :, D_EXPERT:] = wu_ref[0].astype(BF16)
            wd_scr[s] = wd_ref[0].astype(BF16)

    @pl.when(i < used_ref[0])
    def _():
        h = _unpack_row(ha_ref[...], hb_ref[...]).astype(BF16)
        gates = gate_ref[...]
        y = None
        for s, lane in enumerate((GATE_LO, GATE_HI)):
            gu = _dot(h, wgu_scr[s])
            act = _silu(gu[:, :D_EXPERT]) * gu[:, D_EXPERT:]
            part = _dot((act * gates[:, lane:lane + 1]).astype(BF16), wd_scr[s])
            y = part if y is None else y + part
        ya_ref[...], yb_ref[...] = _pack_row(y)


def _final_norm_kernel(x_ref, ya_ref, yb_ref, g_ref, o_ref):
    o_ref[...] = _rmsnorm(x_ref[...] + _unpack_row(ya_ref[...], yb_ref[...]), g_ref[...])


def _moe_dense_kernel(x_ref, ng_ref, wr_cat_ref, wr_hi_ref, br_ref, wg_ref, wu_ref, wd_ref, nf_ref,
                      after_ref, y_ref, h_scr, gate_scr, acc_scr, *, final_norm):
    del after_ref
    e = pl.program_id(1)

    @pl.when(e == 0)
    def _():
        hn = _rmsnorm(x_ref[...], ng_ref[...])
        h_scr[...] = hn.astype(BF16)
        lg = _router_logits(hn, wr_cat_ref, wr_hi_ref, br_ref)
        lane = lax.broadcasted_iota(I32, lg.shape, 1)
        neg = jnp.float32(-jnp.inf)
        group_lg = jnp.where(lane < N_EXPERT_GROUPS, lg, neg)
        is_expert = (lane >= EXPERT_LANE0) & (lane < EXPERT_LANE0 + N_EXPERTS)
        _, i1, i2, w1, w2 = _select(group_lg, jnp.where(is_expert, lg, neg), lane,
                                    jnp.where(is_expert, lane - EXPERT_LANE0, N_EXPERTS), 1)
        gate_scr[...] = (jnp.where(lane == i1 + EXPERT_LANE0, w1, 0.0)
                         + jnp.where(lane == i2 + EXPERT_LANE0, w2, 0.0))
        acc_scr[...] = jnp.zeros_like(acc_scr)

    lane = lax.broadcasted_iota(I32, gate_scr.shape, 1)
    gate = jnp.sum(jnp.where(lane == EXPERT_LANE0 + e, gate_scr[...], 0.0), axis=-1, keepdims=True)
    h = h_scr[...]
    act = _silu(_dot(h, wg_ref[0].astype(BF16))) * _dot(h, wu_ref[0].astype(BF16))
    acc_scr[...] += _dot((act * gate).astype(BF16), wd_ref[0].astype(BF16))

    @pl.when(e == N_EXPERTS - 1)
    def _():
        y = x_ref[...] + acc_scr[...]
        if final_norm:
            y = _rmsnorm(y, nf_ref[...])
        y_ref[...] = y


def _const_spec(shape):
    return pl.BlockSpec(shape, lambda *_: (0,) * len(shape))


def _resident_spec(shape):
    return pl.BlockSpec(shape, lambda *_: (0,) * len(shape), pipeline_mode=pl.Buffered(1))


_ANY_SPEC = pl.BlockSpec(memory_space=pl.ANY)


def _rope_tables(pos):
    half = RET_HEAD_DIM // 2
    inv = np.float32(ROPE_BASE) ** (-np.arange(half, dtype=np.float32) / np.float32(half))
    ang = pos.astype(np.float32)[:, None] * inv[None, :]
    cos, sin = np.cos(ang), np.sin(ang)
    return np.concatenate([cos, cos], -1), np.concatenate([-sin, sin], -1)


def _decay_consts(c, reps):
    f = np.float32
    log_g = np.log(f(1.0) - f(2.0) ** (f(-5.0) - np.arange(RET_HEADS, dtype=f)))
    i = np.arange(c, dtype=f)
    diff = i[:, None] - i[None, :]
    dmat = np.where(diff[None] >= 0, np.exp(np.maximum(diff, f(0))[None] * log_g[:, None, None]), f(0))
    qdec = np.exp((i + f(1))[None, :] * log_g[:, None])
    kdec = np.exp((f(c) - f(1) - i)[None, :] * log_g[:, None])
    cdec = np.exp(f(c) * log_g)
    dmat = np.einsum('ab,hij->haibj', np.eye(reps, dtype=f), dmat).reshape(RET_HEADS, reps * c, reps * c)
    lanes = lambda t: np.repeat(np.tile(t, (1, reps)).T, RET_HEAD_DIM, axis=1)
    cdec = np.broadcast_to(cdec[:, None, None], (RET_HEADS, 1, RET_HEAD_DIM))
    return tuple(np.ascontiguousarray(a, dtype=f) for a in (dmat, lanes(qdec), lanes(kdec), cdec))


def _split_weight(w):
    hi = lax.bitcast_convert_type(_bf16_round_bits(lax.bitcast_convert_type(w, jnp.uint32)), F32)
    return hi.astype(BF16), (w - hi).astype(BF16)


_W_IN_BLOCK = (None, D_MODEL, IN_WIDTH)
_W_POOL_BLOCK = (None, len(POOL_WINDOWS), POOL_GROUP_DIM, POOL_GROUP_DIM)
_W_OUT_BLOCK = (None, D_MODEL, D_MODEL)
_DECAY_SPECS = [
    _const_spec((RET_HEADS, CHUNK, CHUNK)), _const_spec((CHUNK, RET_WIDTH)),
    _const_spec((CHUNK, RET_WIDTH)), _const_spec((RET_HEADS, 1, RET_HEAD_DIM)),
]


def _layer_spec(block, layer, resident=False):
    index_map = lambda *_: (layer,) + (0,) * (len(block) - 1)
    if resident:
        return pl.BlockSpec(block, index_map, pipeline_mode=pl.Buffered(1))
    return pl.BlockSpec(block, index_map)


def _mixer_weight_specs(layer, split):
    n = 2 if split else 1
    row = lambda width: _layer_spec((None, 1, width), layer)
    return [row(D_MODEL), *[_layer_spec(_W_IN_BLOCK, layer, True)] * n,
            *[_layer_spec(_W_POOL_BLOCK, layer)] * n, row(POOL_WIDTH), row(RET_WIDTH),
            *[_layer_spec(_W_OUT_BLOCK, layer, True)] * n]


def _chain_out_spec(block, depth, layer, has_prev):
    zeros = (0,) * (len(block) - 1)
    if has_prev:
        return pl.BlockSpec((None,) + block, lambda i, *_: (layer, i) + zeros)
    return pl.BlockSpec((depth,) + block, lambda i, *_: (0, i) + zeros)


def _mixer_prompt(x, moe_y, weights, prev_out, layer, rows, precise_tail, after):
    b, l, _ = x.shape
    depth = weights[0].shape[0]
    assert l % rows == 0 and rows % CHUNK == 0
    cos2, sin2 = _rope_tables(np.arange(l))
    decay = _decay_consts(RET_CHUNK, 1)
    steps = l // rows
    tok = lambda i, c: (i, c, 0)
    if moe_y is None:
        ya = yb = jnp.zeros((1, rows, PACK_W), I32)
        y_spec = _const_spec((1, rows, PACK_W))
    else:
        ya, yb = (t.reshape(b, l, PACK_W) for t in moe_y)
        y_spec = pl.BlockSpec((1, rows, PACK_W), tok)
    has_prev = prev_out is not None
    n_in = 19
    y, tails, states = pl.pallas_call(
        functools.partial(_mixer_prompt_kernel, rows=rows, steps=steps, moe_in=moe_y is not None,
                          precise_tail=precise_tail, layer=layer, has_prev=has_prev),
        grid=(b, steps),
        in_specs=[pl.BlockSpec((1, rows, D_MODEL), tok), y_spec, y_spec,
                  pl.BlockSpec((rows, RET_HEAD_DIM), lambda i, c: (c, 0)),
                  pl.BlockSpec((rows, RET_HEAD_DIM), lambda i, c: (c, 0)),
                  *_mixer_weight_specs(layer, True), *_DECAY_SPECS, _ANY_SPEC,
                  *([_ANY_SPEC, _ANY_SPEC] if has_prev else [])],
        out_specs=[pl.BlockSpec((1, rows, D_MODEL), tok),
                   _chain_out_spec((1, POOL_BUF, POOL_WIDTH), depth, layer, has_prev),
                   _chain_out_spec((1, RET_HEADS, RET_HEAD_DIM, RET_HEAD_DIM), depth, layer, has_prev)],
        out_shape=[jax.ShapeDtypeStruct(x.shape, F32),
                   jax.ShapeDtypeStruct((depth, b, POOL_BUF, POOL_WIDTH), F32),
                   jax.ShapeDtypeStruct((depth, b, RET_HEADS, RET_HEAD_DIM, RET_HEAD_DIM), F32)],
        input_output_aliases={n_in: 1, n_in + 1: 2} if has_prev else {},
        scratch_shapes=[pltpu.VMEM((1, POOL_HIST + rows, POOL_WIDTH), F32),
                        pltpu.VMEM((RET_HEADS, RET_HEAD_DIM, RET_HEAD_DIM), F32),
                        pltpu.VMEM((rows, RET_WIDTH), F32),
                        pltpu.VMEM((rows, IN_WIDTH), F32)],
        compiler_params=pltpu.CompilerParams(
            dimension_semantics=("arbitrary", "arbitrary"), vmem_limit_bytes=VMEM_LIMIT),
        name="mixer_prompt",
    )(x, ya, yb, cos2, sin2, *weights, *decay, after, *(prev_out if has_prev else ()))
    return y, (tails, states)


def _mixer_sample(x, pool_prev, s0, weights, prev_out, layer, after):
    b, l, _ = x.shape
    depth = s0.shape[0]
    ns = SAMPLE_SEQS
    assert ns * l == CHUNK and b % ns == 0
    cos2, sin2 = _rope_tables(PAST_LEN + np.arange(l))
    cos2, sin2 = np.tile(cos2, (ns, 1)), np.tile(sin2, (ns, 1))
    decay = _decay_consts(l, ns)
    seq3 = lambda i: (i, 0, 0)
    state_block = (ns, RET_HEADS, RET_HEAD_DIM, RET_HEAD_DIM)
    tail_block = (POOL_BUF, ns, POOL_WIDTH)
    if prev_out is None:
        tail_out = pl.BlockSpec((depth,) + tail_block, lambda i: (0, 0, i, 0))
    else:
        tail_out = pl.BlockSpec((None,) + tail_block, lambda i: (layer, 0, i, 0))
    has_prev = prev_out is not None
    n_in = 16
    y, tails, states = pl.pallas_call(
        functools.partial(_mixer_sample_kernel, seq_len=l, layer=layer, has_prev=has_prev),
        grid=(b // ns,),
        in_specs=[pl.BlockSpec((ns, l, D_MODEL), seq3),
                  pl.BlockSpec((None,) + tail_block, lambda i: (layer, 0, i, 0)),
                  pl.BlockSpec((None,) + state_block, lambda i: (layer, i, 0, 0, 0)),
                  _const_spec((CHUNK, RET_HEAD_DIM)), _const_spec((CHUNK, RET_HEAD_DIM)),
                  *_mixer_weight_specs(layer, False), *_DECAY_SPECS, _ANY_SPEC,
                  *([_ANY_SPEC, _ANY_SPEC] if has_prev else [])],
        out_specs=[pl.BlockSpec((ns, l, D_MODEL), seq3), tail_out,
                   _chain_out_spec(state_block, depth, layer, has_prev)],
        out_shape=[jax.ShapeDtypeStruct(x.shape, F32),
                   jax.ShapeDtypeStruct((depth, POOL_BUF, b, POOL_WIDTH), F32),
                   jax.ShapeDtypeStruct(s0.shape, F32)],
        input_output_aliases={n_in: 1, n_in + 1: 2} if has_prev else {},
        scratch_shapes=[pltpu.VMEM((ns, POOL_HIST + l, POOL_WIDTH), F32),
                        pltpu.VMEM((CHUNK, RET_WIDTH), F32)],
        compiler_params=pltpu.CompilerParams(
            dimension_semantics=("arbitrary",), vmem_limit_bytes=VMEM_LIMIT),
        name="mixer_sample",
    )(x, pool_prev, s0, cos2, sin2, *weights, *decay, after, *(prev_out if has_prev else ()))
    return y, (tails, states)


def _router_weights(w_rg, b_rg, w_re, b_re):
    depth = w_rg.shape[0]
    gap = EXPERT_LANE0 - N_EXPERT_GROUPS
    rest = LANES - EXPERT_LANE0 - N_EXPERTS
    wr = jnp.concatenate([w_rg, jnp.zeros((depth, D_MODEL, gap), F32), w_re,
                          jnp.zeros((depth, D_MODEL, rest), F32)], axis=-1)
    br = jnp.concatenate([b_rg, jnp.zeros((depth, gap), F32), b_re, jnp.zeros((depth, rest), F32)],
                         axis=-1).reshape(depth, 1, LANES)
    wr_hi, wr_lo = _split_weight(wr)
    return jnp.concatenate([wr_hi, wr_lo], axis=-1), wr_hi, br


def _router_specs(layer):
    return [_layer_spec((None, D_MODEL, 2 * LANES), layer), _layer_spec((None, D_MODEL, LANES), layer),
            _layer_spec((None, 1, LANES), layer)]


def _sc_mesh():
    return plsc.VectorSubcoreMesh(core_axis_name="core", subcore_axis_name="subcore",
                                  num_cores=SC_CORES, num_subcores=SC_SUBCORES)


def _sc_params():
    params = pltpu.CompilerParams()
    if "needs_layout_passes" in pltpu.CompilerParams.__dataclass_fields__:
        params = dataclasses.replace(params, needs_layout_passes=False)
    return params


def _sc_gather(tables, idx):
    n = idx.shape[0]
    assert n % SC_WINDOW == 0
    nt = len(tables)

    def body(*refs):
        i_hbm = refs[nt]
        for t_hbm, o_hbm in zip(refs[:nt], refs[nt + 1:]):
            def gather_window(i_vmem, o_vmem, t_hbm=t_hbm):
                pltpu.sync_copy(t_hbm.at[i_vmem.at[0]], o_vmem)

            pltpu.emit_pipeline(
                gather_window, grid=(n // SC_WINDOW,),
                in_specs=[pl.BlockSpec((1, SC_WINDOW), lambda i: (0, i))],
                out_specs=[pl.BlockSpec((SC_WINDOW, t_hbm.shape[1]), lambda i: (i, 0))],
                core_axis_name=("core", "subcore"),
                dimension_semantics=(pltpu.PARALLEL,),
            )(i_hbm, o_hbm)

    out_type = tuple(jax.ShapeDtypeStruct((n, t.shape[1]), t.dtype) for t in tables)
    return pl.kernel(body, out_type=out_type, mesh=_sc_mesh(), name="sc_gather")(
        *tables, idx.reshape(1, n))


def _sc_slots(cls, rank, starts, n_slots):
    t = cls.shape[0]
    workers = SC_CORES * SC_SUBCORES
    slot_per, tok_per = n_slots // workers, t // workers
    assert n_slots % (workers * SC_LANES) == 0 and t % (workers * SC_LANES) == 0 and t & (t - 1) == 0
    assert n_slots % (SC_LANES * SC_UNROLL) == 0 and t % (SC_LANES * SC_UNROLL) == 0

    def body(cls_hbm, rank_hbm, starts_hbm, pos_hbm, slot_hbm, cls_v, rank_v, starts_v, pos_v, slot_v):
        wid = lax.axis_index("subcore") * SC_CORES + lax.axis_index("core")
        pltpu.sync_copy(cls_hbm, cls_v)
        pltpu.sync_copy(rank_hbm, rank_v)
        pltpu.sync_copy(starts_hbm, starts_v)

        lane = lax.iota(I32, SC_LANES)
        span = SC_LANES * SC_UNROLL

        @pl.loop(0, n_slots, step=span)
        def _(i):
            for u in range(SC_UNROLL):
                j = i + u * SC_LANES
                slot_v[pl.ds(j, SC_LANES)] = (lane + j) & (t - 1)

        @pl.loop(0, t, step=span)
        def _(i):
            for u in range(SC_UNROLL):
                j = i + u * SC_LANES
                at = pl.ds(j, SC_LANES)
                pos = plsc.load_gather(starts_v, [cls_v[at]]) + rank_v[at]
                pos_v[at] = pos
                plsc.store_scatter(slot_v, [pos], lane + j)

        tok_off = pl.multiple_of(wid * tok_per, SC_LANES)
        pltpu.sync_copy(pos_v.at[pl.ds(tok_off, tok_per)], pos_hbm.at[pl.ds(tok_off, tok_per)])
        slot_off = pl.multiple_of(wid * slot_per, SC_LANES)
        pltpu.sync_copy(slot_v.at[pl.ds(slot_off, slot_per)], slot_hbm.at[pl.ds(slot_off, slot_per)])

    return pl.kernel(
        body, mesh=_sc_mesh(), compiler_params=_sc_params(), name="sc_slots",
        out_type=(jax.ShapeDtypeStruct((t,), I32), jax.ShapeDtypeStruct((n_slots,), I32)),
        scratch_types=[pltpu.VMEM((t,), I32), pltpu.VMEM((t,), I32), pltpu.VMEM((CLASS_ROWS,), I32),
                       pltpu.VMEM((t,), I32), pltpu.VMEM((n_slots,), I32)],
    )(cls, rank, starts)


def _moe_route(x, norm_g, router, layer):
    t = x.shape[0]
    tok = lambda i: (i, 0)
    return pl.pallas_call(
        _moe_route_kernel,
        grid=(t // ROUTE_TILE,),
        in_specs=[pl.BlockSpec((ROUTE_TILE, D_MODEL), tok), _layer_spec((None, 1, D_MODEL), layer),
                  *_router_specs(layer)],
        out_specs=[pl.BlockSpec((ROUTE_TILE, PACK_W), tok), pl.BlockSpec((ROUTE_TILE, PACK_W), tok),
                   pl.BlockSpec((ROUTE_TILE, LANES), tok),
                   pl.BlockSpec((SUBLANES, ROUTE_TILE), lambda i: (0, i)),
                   _const_spec((CLASS_ROWS, LANES))],
        out_shape=[jax.ShapeDtypeStruct((t, PACK_W), I32), jax.ShapeDtypeStruct((t, PACK_W), I32),
                   jax.ShapeDtypeStruct((t, LANES), F32), jax.ShapeDtypeStruct((SUBLANES, t), I32),
                   jax.ShapeDtypeStruct((CLASS_ROWS, LANES), F32)],
        scratch_shapes=[pltpu.VMEM((CLASS_ROWS, LANES), F32)],
        compiler_params=pltpu.CompilerParams(
            dimension_semantics=("arbitrary",), vmem_limit_bytes=VMEM_LIMIT),
        name="moe_route",
    )(x, norm_g, *router)


def _moe_experts(routed, w_gate, w_up, w_down, layer, after):
    ha, hb, gates, idx, counts = routed
    t = ha.shape[0]
    n_slots = t + N_CLASSES * PAIR_TILE
    n_tiles = n_slots // PAIR_TILE

    cnt = counts[:, 0].astype(I32)
    padded = (cnt + PAIR_TILE - 1) // PAIR_TILE * PAIR_TILE
    ends = jnp.cumsum(padded)
    starts = ends - padded
    tile_cls = jnp.minimum(
        jnp.sum(ends[None, :N_CLASSES] <= (jnp.arange(n_tiles, dtype=I32) * PAIR_TILE)[:, None], axis=1),
        N_CLASSES - 1).astype(I32)
    first = (tile_cls // PAIRS_PER_GROUP) * EXPERTS_PER_GROUP
    tile_ea = first + jnp.asarray(PAIR_LO, I32)[tile_cls % PAIRS_PER_GROUP]
    tile_eb = first + jnp.asarray(PAIR_HI, I32)[tile_cls % PAIRS_PER_GROUP]
    used = (ends[N_CLASSES - 1:N_CLASSES] // PAIR_TILE).astype(I32)

    pos, slot_tok = _sc_slots(idx[0], idx[1], starts, n_slots)
    hsa, hsb, gate_s = _sc_gather((ha, hb, gates), slot_tok)

    row = lambda i, ea, eb, nu: (jnp.minimum(i, nu[0] - 1), 0)
    w_spec = lambda shape, which: pl.BlockSpec(
        (None, 1) + shape, lambda i, ea, eb, nu: (layer, (ea, eb)[which][i], 0, 0))
    gu_shape, d_shape = (D_MODEL, D_EXPERT), (D_EXPERT, D_MODEL)
    ysa, ysb = pl.pallas_call(
        _moe_pair_kernel,
        grid_spec=pltpu.PrefetchScalarGridSpec(
            num_scalar_prefetch=3, grid=(n_tiles,),
            in_specs=[pl.BlockSpec((PAIR_TILE, PACK_W), row), pl.BlockSpec((PAIR_TILE, PACK_W), row),
                      pl.BlockSpec((PAIR_TILE, LANES), row),
                      w_spec(gu_shape, 0), w_spec(gu_shape, 0), w_spec(d_shape, 0),
                      w_spec(gu_shape, 1), w_spec(gu_shape, 1), w_spec(d_shape, 1), _ANY_SPEC],
            out_specs=[pl.BlockSpec((PAIR_TILE, PACK_W), row), pl.BlockSpec((PAIR_TILE, PACK_W), row)],
            scratch_shapes=[pltpu.VMEM((2, D_MODEL, 2 * D_EXPERT), BF16),
                            pltpu.VMEM((2, D_EXPERT, D_MODEL), BF16)]),
        out_shape=[jax.ShapeDtypeStruct((n_slots, PACK_W), I32),
                   jax.ShapeDtypeStruct((n_slots, PACK_W), I32)],
        compiler_params=pltpu.CompilerParams(
            dimension_semantics=("arbitrary",), vmem_limit_bytes=VMEM_LIMIT),
        name="moe_pair",
    )(tile_ea, tile_eb, used, hsa, hsb, gate_s, w_gate, w_up, w_down, w_gate, w_up, w_down, after)
    return (ysa, ysb), pos


def _final_norm(x, moe_y, g, rows):
    t = x.shape[0]
    tok = lambda i: (i, 0)
    return pl.pallas_call(
        _final_norm_kernel,
        grid=(t // rows,),
        in_specs=[pl.BlockSpec((rows, D_MODEL), tok), pl.BlockSpec((rows, PACK_W), tok),
                  pl.BlockSpec((rows, PACK_W), tok), _const_spec((1, D_MODEL))],
        out_specs=pl.BlockSpec((rows, D_MODEL), tok),
        out_shape=jax.ShapeDtypeStruct(x.shape, F32),
        compiler_params=pltpu.CompilerParams(
            dimension_semantics=("arbitrary",), vmem_limit_bytes=VMEM_LIMIT),
        name="final_norm",
    )(x, *moe_y, g.reshape(1, D_MODEL))


def _moe_dense(x, norm_g, router, w_gate, w_up, w_down, norm_final, layer, final_norm, rows, after):
    t = x.shape[0]
    assert t % rows == 0
    tok = lambda i, e: (i, 0)
    w_spec = lambda shape: pl.BlockSpec((None, 1) + shape, lambda i, e: (layer, e, 0, 0))
    return pl.pallas_call(
        functools.partial(_moe_dense_kernel, final_norm=final_norm),
        grid=(t // rows, N_EXPERTS),
        in_specs=[pl.BlockSpec((rows, D_MODEL), tok), _layer_spec((None, 1, D_MODEL), layer),
                  *_router_specs(layer),
                  w_spec((D_MODEL, D_EXPERT)), w_spec((D_MODEL, D_EXPERT)), w_spec((D_EXPERT, D_MODEL)),
                  _const_spec((1, D_MODEL)), _ANY_SPEC],
        out_specs=pl.BlockSpec((rows, D_MODEL), tok),
        out_shape=jax.ShapeDtypeStruct(x.shape, F32),
        scratch_shapes=[pltpu.VMEM((rows, D_MODEL), BF16),
                        pltpu.VMEM((rows, LANES), F32),
                        pltpu.VMEM((rows, D_MODEL), F32)],
        compiler_params=pltpu.CompilerParams(
            dimension_semantics=("arbitrary", "arbitrary"), vmem_limit_bytes=VMEM_LIMIT),
        name="moe_dense",
    )(x, norm_g, *router, w_gate, w_up, w_down, norm_final.reshape(1, D_MODEL), after)


def kernel(x_prompt, x_sample, cache_pool, state_ret, norm_mix, w_in, w_pool, pool_scale, ret_gn, w_out, norm_ffn, w_router_group, b_router_group, w_router_expert, b_router_expert, w_gate, w_up, w_down, norm_final):
    depth = norm_mix.shape[0]
    row = lambda a: a.reshape(depth, 1, a.shape[-1])
    mix_split = (row(norm_mix), *_split_weight(w_in), *_split_weight(w_pool), row(pool_scale),
                 row(ret_gn), *_split_weight(w_out))
    mix_hi = tuple(mix_split[i] for i in (0, 1, 3, 5, 6, 7))
    router = _router_weights(w_router_group, b_router_group, w_router_expert, b_router_expert)
    norm_ffn = row(norm_ffn)
    pool_prev = jnp.swapaxes(cache_pool, 1, 2)

    yp, ys = x_prompt, x_sample
    moe_p = None
    out_p = out_s = None
    for l in range(depth):
        yp, out_p = _mixer_prompt(yp, moe_p, mix_split, out_p, l, rows=512,
                                  precise_tail=PRECISE_TAIL_STEPS if l < depth - 1 else 0, after=ys)
        routed = _moe_route(yp.reshape(-1, D_MODEL), norm_ffn, router, l)
        ys, out_s = _mixer_sample(ys, pool_prev, state_ret, mix_hi, out_s, l, after=routed[-1])
        sorted_y, pos = _moe_experts(routed, w_gate, w_up, w_down, l, after=ys)
        ys = _moe_dense(ys.reshape(-1, D_MODEL), norm_ffn, router, w_gate, w_up, w_down, norm_final,
                        l, l == depth - 1, rows=1024, after=sorted_y[0]).reshape(ys.shape)
        moe_p = _sc_gather(sorted_y, pos)
    yp = _final_norm(yp.reshape(-1, D_MODEL), moe_p, norm_final, rows=1024).reshape(yp.shape)
    return (yp, ys, *out_p, jnp.swapaxes(out_s[0], 1, 2), out_s[1])
```

```python
import dataclasses
import functools

import jax
import jax.numpy as jnp
import numpy as np
from jax import lax
from jax.experimental import pallas as pl
from jax.experimental.pallas import tpu as pltpu
from jax.experimental.pallas import tpu_sc as plsc

F32 = jnp.float32
BF16 = jnp.bfloat16
I32 = jnp.int32

D_MODEL = 1024
POOL_WIDTH = 512
POOL_WINDOWS = (2, 4, 8, 16)
POOL_GROUP_DIM = 128
POOL_BUF = 15
POOL_HIST = 16
RET_WIDTH = 512
RET_HEADS = 4
RET_HEAD_DIM = 128
RET_CHUNK = 128
ROPE_BASE = 10000.0
IN_WIDTH = POOL_WIDTH + 4 * RET_WIDTH
N_EXPERT_GROUPS = 4
EXPERTS_PER_GROUP = 4
N_EXPERTS = 16
D_EXPERT = 256
RMS_EPS = 1e-6
GN_EPS = 1e-5
PAST_LEN = 16384

LANES = 128
SUBLANES = 8
EXPERT_LANE0 = 8
PAIRS_PER_GROUP = 6
N_CLASSES = N_EXPERT_GROUPS * PAIRS_PER_GROUP
CLASS_ROWS = 32
PAIR_LO = (0, 0, 0, 1, 1, 2)
PAIR_HI = (1, 2, 3, 2, 3, 3)
GATE_LO, GATE_HI = 0, 1
COUNT_BLOCK = 256
PAIR_TILE = 256
PACK_W = D_MODEL // 4
SC_CORES, SC_SUBCORES, SC_LANES = 2, 16, 16
SC_WINDOW = 128
SC_UNROLL = 8
PRECISE_TAIL_STEPS = 1
CHUNK = 128
SAMPLE_SEQS = 16
VMEM_LIMIT = 56 * 1024 * 1024


def _dot(a, b):
    return jnp.dot(a, b, preferred_element_type=F32)


def _dot_nt(a, b):
    return lax.dot_general(a, b, (((1,), (1,)), ((), ())), preferred_element_type=F32)


def _bf16_round_bits(u):
    return (u + jnp.uint32(0x7FFF) + ((u >> 16) & jnp.uint32(1))) & jnp.uint32(0xFFFF0000)


def _split(a):
    hi = pltpu.bitcast(_bf16_round_bits(pltpu.bitcast(a, jnp.uint32)), F32)
    return hi.astype(BF16), (a - hi).astype(BF16)


def _mm(a, b, precise, nt=False):
    dot = _dot_nt if nt else _dot
    if precise:
        b_hi, b_lo = b if isinstance(b, tuple) else _split(b)
        a_hi, a_lo = _split(a)
        return dot(a_hi, b_hi) + dot(a_lo, b_hi) + dot(a_hi, b_lo)
    return dot(a.astype(BF16), b[0] if isinstance(b, tuple) else b.astype(BF16))


def _rmsnorm(x, g):
    ms = jnp.mean(x * x, axis=-1, keepdims=True)
    return x * lax.rsqrt(ms + RMS_EPS) * g


def _pool_mix(ubuf, rows, t_first, n_prev, wpool_refs, pscale, precise=False, row0=0):
    ns = ubuf.shape[0]
    t = t_first + lax.broadcasted_iota(I32, (1, rows, POOL_GROUP_DIM), 1)
    base = POOL_HIST + row0
    outs = []
    for j, w in enumerate(POOL_WINDOWS):
        lanes = slice(j * POOL_GROUP_DIM, (j + 1) * POOL_GROUP_DIM)
        uj = ubuf[:, base:base + rows, lanes]
        acc = uj
        for i in range(1, w):
            acc = acc + ubuf[:, base - i:base - i + rows, lanes]
        cnt = jnp.minimum(w, n_prev + t + 1).astype(F32)
        d = (acc / cnt - uj).reshape(ns * rows, POOL_GROUP_DIM)
        outs.append(_mm(d, tuple(w[j] for w in wpool_refs), precise))
    return jnp.concatenate(outs, axis=-1) * pscale


def _rope(xh, cos2, sin2):
    return xh * cos2 + pltpu.roll(xh, RET_HEAD_DIM // 2, 1) * sin2


def _group_norm(o):
    mu = jnp.mean(o, axis=-1, keepdims=True)
    c = o - mu
    var = jnp.mean(c * c, axis=-1, keepdims=True)
    return c * lax.rsqrt(var + GN_EPS)


def _silu(x):
    return x * (1.0 / (1.0 + jnp.exp(-x)))


def _head(a, h):
    return a[:, h * RET_HEAD_DIM:(h + 1) * RET_HEAD_DIM]


def _qkvg(z):
    p, r = POOL_WIDTH, RET_WIDTH
    return z[:, p:p + r], z[:, p + r:p + 2 * r], z[:, p + 2 * r:p + 3 * r], z[:, p + 3 * r:p + 4 * r]


def _pack_bf16_pair(a, b):
    ua = pltpu.bitcast(a.astype(BF16).astype(F32), jnp.uint32)
    ub = pltpu.bitcast(b.astype(BF16).astype(F32), jnp.uint32)
    return pltpu.bitcast((ua >> 16) | (ub & jnp.uint32(0xFFFF0000)), I32)


def _unpack_bf16_pair(w):
    u = pltpu.bitcast(w, jnp.uint32)
    return pltpu.bitcast(u << 16, F32), pltpu.bitcast(u & jnp.uint32(0xFFFF0000), F32)


def _pack_row(y):
    q = PACK_W
    return _pack_bf16_pair(y[:, 0:q], y[:, q:2 * q]), _pack_bf16_pair(y[:, 2 * q:3 * q], y[:, 3 * q:])


def _unpack_row(wa, wb):
    return jnp.concatenate([*_unpack_bf16_pair(wa), *_unpack_bf16_pair(wb)], axis=-1)


def _zero_other_layers(ref, layer):
    for j in range(ref.shape[0]):
        if j != layer:
            ref[j] = jnp.zeros(ref.shape[1:], ref.dtype)


def _mixer_prompt_kernel(*refs, rows, steps, moe_in, precise_tail, layer, has_prev):
    (x_ref, ya_ref, yb_ref, cos_ref, sin_ref, ng_ref, win_hi_ref, win_lo_ref, wpool_hi_ref,
     wpool_lo_ref, pscale_ref, gn_ref, wout_hi_ref, wout_lo_ref, dmat_ref, qdec_ref, kdec_ref,
     cdec_ref) = refs[:18]
    route_in = refs[18:22]
    y_ref, tail_ref, sfin_ref, *route_out, ubuf, s_scr, ret_scr, z_scr, count_scr = (
        refs[23 + 2 * has_prev:])
    c = pl.program_id(1)

    @pl.when((pl.program_id(0) == 0) & (c == 0))
    def _():
        count_scr[...] = jnp.zeros_like(count_scr)
    if not has_prev:
        _zero_other_layers(tail_ref, layer)
        _zero_other_layers(sfin_ref, layer)
        tail_ref, sfin_ref = tail_ref.at[layer], sfin_ref.at[layer]

    @pl.when(c == 0)
    def _():
        ubuf[:, 0:POOL_HIST, :] = jnp.zeros((1, POOL_HIST, POOL_WIDTH), F32)
        s_scr[...] = jnp.zeros_like(s_scr)

    kv_cols = slice(POOL_WIDTH + RET_WIDTH, POOL_WIDTH + 3 * RET_WIDTH)

    def step(kv_precise, full_from):
        x = x_ref[0]
        if moe_in:
            x = x + _unpack_row(ya_ref[0], yb_ref[0])
        hn = _rmsnorm(x, ng_ref[...])
        hi, lo = _split(hn) if kv_precise else (hn.astype(BF16), None)
        z_scr[...] = _dot(hi, win_hi_ref[...])
        if kv_precise and full_from:
            z_scr[:full_from, kv_cols] += (_dot(lo[:full_from], win_hi_ref[:, kv_cols])
                                           + _dot(hi[:full_from], win_lo_ref[:, kv_cols]))
        if full_from < rows:
            z_scr[full_from:, :] += (_dot(lo[full_from:], win_hi_ref[...])
                                     + _dot(hi[full_from:], win_lo_ref[...]))

        ubuf[0, POOL_HIST:POOL_HIST + rows, :] = z_scr[:, :POOL_WIDTH]
        pool_w = (wpool_hi_ref, wpool_lo_ref)
        pool_parts = []
        if full_from:
            pool_parts.append(_pool_mix(ubuf, full_from, c * rows, 0, pool_w, pscale_ref[...]))
        if full_from < rows:
            pool_parts.append(_pool_mix(ubuf, rows - full_from, c * rows + full_from, 0, pool_w,
                                        pscale_ref[...], precise=True, row0=full_from))
        pool_out = jnp.concatenate(pool_parts, axis=0)
        tail_ref[...] = ubuf[:, rows + POOL_HIST - POOL_BUF:rows + POOL_HIST, :]
        ubuf[:, 0:POOL_HIST, :] = ubuf[:, rows:rows + POOL_HIST, :]

        scale = RET_HEAD_DIM ** -0.5
        for ci in range(rows // CHUNK):
            rs = slice(ci * CHUNK, (ci + 1) * CHUNK)
            full = ci * CHUNK >= full_from
            cos2 = cos_ref[rs, :]
            sin2 = sin_ref[rs, :]
            for h in range(RET_HEADS):
                col = lambda part: slice(POOL_WIDTH + part * RET_WIDTH + h * RET_HEAD_DIM,
                                         POOL_WIDTH + part * RET_WIDTH + (h + 1) * RET_HEAD_DIM)
                q = _rope(z_scr[rs, col(0)], cos2, sin2)
                k = _rope(z_scr[rs, col(1)], cos2, sin2) * scale
                v = z_scr[rs, col(2)]
                s_old = s_scr[h]
                scores = _mm(q, k, full, nt=True) * dmat_ref[h]
                qd = q * _head(qdec_ref[...], h)
                o = _mm(scores, v, full) + _mm(qd, s_old, full)
                kd = k * _head(kdec_ref[...], h)
                s_scr[h] = s_old * cdec_ref[h] + _mm(kd.T, v, kv_precise)
                on = _group_norm(o) * _head(gn_ref[...], h)
                ret_scr[rs, h * RET_HEAD_DIM:(h + 1) * RET_HEAD_DIM] = _silu(z_scr[rs, col(3)]) * on

        mix_in = jnp.concatenate([pool_out, ret_scr[...]], axis=-1)
        y_ref[0] = x + _dot(mix_in.astype(BF16), wout_hi_ref[...])
        if full_from < rows:
            m_hi, m_lo = _split(mix_in[full_from:])
            y_ref[0, full_from:, :] += _dot(m_lo, wout_hi_ref[...]) + _dot(m_hi, wout_lo_ref[...])
        sfin_ref[0] = s_scr[...]
        _route_tile(y_ref[0], *route_in, *route_out, count_scr)

    if precise_tail:
        pl.when(c < steps - precise_tail)(lambda: step(False, rows))
        if precise_tail > 1:
            pl.when((c >= steps - precise_tail) & (c < steps - 1))(lambda: step(True, rows))
        pl.when(c == steps - 1)(lambda: step(True, rows - CHUNK))
    else:
        step(False, rows)


def _mixer_sample_kernel(*refs, seq_len, layer, has_prev):
    (x_ref, prev_ref, s0_ref, cos_ref, sin_ref, ng_ref, win_ref, wpool_ref, pscale_ref, gn_ref,
     wout_ref, dmat_ref, qdec_ref, kdec_ref, cdec_ref) = refs[:15]
    y_ref, tail_ref, sfin_ref, ubuf, ret_scr = refs[16 + 2 * has_prev:]
    if not has_prev:
        _zero_other_layers(tail_ref, layer)
        _zero_other_layers(sfin_ref, layer)
        tail_ref, sfin_ref = tail_ref.at[layer], sfin_ref.at[layer]
    ns = SAMPLE_SEQS
    rows = ns * seq_len
    x = x_ref[...].reshape(rows, D_MODEL)
    hn = _rmsnorm(x, ng_ref[...]).astype(BF16)
    z = _dot(hn, win_ref[...])
    for j in range(POOL_BUF):
        ubuf[:, POOL_HIST - POOL_BUF + j, :] = prev_ref[j]
    ubuf[:, POOL_HIST:POOL_HIST + seq_len, :] = z[:, :POOL_WIDTH].reshape(ns, seq_len, POOL_WIDTH)
    pool_out = _pool_mix(ubuf, seq_len, 0, POOL_BUF, (wpool_ref,), pscale_ref[...])
    for j in range(POOL_BUF):
        tail_ref[j] = ubuf[:, seq_len + POOL_HIST - POOL_BUF + j, :]

    q_all, k_all, v_all, g_all = _qkvg(z)
    scale = RET_HEAD_DIM ** -0.5
    cos2 = cos_ref[...]
    sin2 = sin_ref[...]
    tok_seq = lax.broadcasted_iota(I32, (RET_HEAD_DIM, rows), 1) // seq_len
    for h in range(RET_HEADS):
        q = _rope(_head(q_all, h), cos2, sin2)
        k = _rope(_head(k_all, h), cos2, sin2) * scale
        vb = _head(v_all, h).astype(BF16)
        s_old = s0_ref[:, h]
        scores = _dot_nt(q.astype(BF16), k.astype(BF16)) * dmat_ref[h]
        qd = (q * _head(qdec_ref[...], h)).astype(BF16).reshape(ns, seq_len, RET_HEAD_DIM)
        o_state = jnp.einsum('bid,bde->bie', qd, s_old.astype(BF16), preferred_element_type=F32)
        o = _dot(scores.astype(BF16), vb) + o_state.reshape(rows, RET_HEAD_DIM)
        kdt = (k * _head(kdec_ref[...], h)).T
        lhs = jnp.concatenate(
            [jnp.where(tok_seq == b, kdt, 0.0).astype(BF16) for b in range(ns)], axis=0)
        upd = _dot(lhs, vb).reshape(ns, RET_HEAD_DIM, RET_HEAD_DIM)
        sfin_ref[:, h] = s_old * cdec_ref[h] + upd
        on = _group_norm(o) * _head(gn_ref[...], h)
        ret_scr[:, h * RET_HEAD_DIM:(h + 1) * RET_HEAD_DIM] = _silu(_head(g_all, h)) * on

    mix_in = jnp.concatenate([pool_out, ret_scr[...]], axis=-1).astype(BF16)
    y_ref[...] = (x + _dot(mix_in, wout_ref[...])).reshape(ns, seq_len, D_MODEL)


def _router_logits(hn, wr_cat_ref, wr_hi_ref, br_ref):
    h_hi, h_lo = _split(hn)
    part = _dot(h_hi, wr_cat_ref[...])
    return part[:, :LANES] + part[:, LANES:] + _dot(h_lo, wr_hi_ref[...]) + br_ref[...]


def _select(group_lg, expert_lg, gidx, eidx, axis):
    red = dict(axis=axis, keepdims=True)
    neg = jnp.float32(-jnp.inf)
    gmax = jnp.max(group_lg, **red)
    g_sel = jnp.min(jnp.where(group_lg == gmax, gidx, N_EXPERT_GROUPS), **red)
    p_sel = 1.0 / jnp.sum(jnp.exp(group_lg - gmax), **red)
    emask = (eidx >> 2) == g_sel
    v1 = jnp.max(jnp.where(emask, expert_lg, neg), **red)
    i1 = jnp.min(jnp.where(emask & (expert_lg == v1), eidx, N_EXPERTS), **red)
    emask2 = emask & (eidx != i1)
    v2 = jnp.max(jnp.where(emask2, expert_lg, neg), **red)
    i2 = jnp.min(jnp.where(emask2 & (expert_lg == v2), eidx, N_EXPERTS), **red)
    e2 = jnp.exp(v2 - v1)
    return g_sel, i1, i2, p_sel / (1.0 + e2), p_sel * e2 / (1.0 + e2)


def _route_tile(x, ng_ref, wr_cat_ref, wr_hi_ref, br_ref, ha_ref, hb_ref, gate_ref, idx_ref,
                count_ref, carry_scr):
    rows = x.shape[0]
    hn = _rmsnorm(x, ng_ref[...])
    ha_ref[...], hb_ref[...] = _pack_row(hn)
    lgt = _router_logits(hn, wr_cat_ref, wr_hi_ref, br_ref).T
    neg = jnp.float32(-jnp.inf)
    gidx = lax.broadcasted_iota(I32, (SUBLANES, rows), 0)
    eidx = lax.broadcasted_iota(I32, (N_EXPERTS, rows), 0)
    group_lg = jnp.where(gidx < N_EXPERT_GROUPS, lgt[0:SUBLANES], neg)
    expert_lg = lgt[EXPERT_LANE0:EXPERT_LANE0 + N_EXPERTS]
    g_sel, i1, i2, w1, w2 = _select(group_lg, expert_lg, gidx, eidx, 0)
    lo = jnp.minimum(i1, i2) - g_sel * EXPERTS_PER_GROUP
    hi = jnp.maximum(i1, i2) - g_sel * EXPERTS_PER_GROUP
    cls = g_sel * PAIRS_PER_GROUP + ((lo * (7 - lo)) >> 1) + (hi - lo - 1)
    w_lo = jnp.where(i1 < i2, w1, w2)
    w_hi = jnp.where(i1 < i2, w2, w1)

    crow = lax.broadcasted_iota(I32, (CLASS_ROWS, rows), 0)
    onehot = jnp.where(crow == cls, 1.0, 0.0)
    n_blk = rows // COUNT_BLOCK
    blocks = [onehot[:, j * COUNT_BLOCK:(j + 1) * COUNT_BLOCK] for j in range(n_blk)]
    r = lax.broadcasted_iota(I32, (COUNT_BLOCK, COUNT_BLOCK), 0)
    c = lax.broadcasted_iota(I32, (COUNT_BLOCK, COUNT_BLOCK), 1)
    upper = jnp.where(r < c, 1.0, 0.0).astype(BF16)
    within = _dot(jnp.concatenate(blocks, axis=0).astype(BF16), upper)
    carry = carry_scr[:, 0:1]
    ranks = []
    for j in range(n_blk):
        before = within[j * CLASS_ROWS:(j + 1) * CLASS_ROWS] + carry
        ranks.append(jnp.sum(blocks[j] * before, axis=0, keepdims=True))
        carry = carry + jnp.sum(blocks[j], axis=1, keepdims=True)
    rank = jnp.concatenate(ranks, axis=1)
    carry_scr[...] = jnp.broadcast_to(carry, carry_scr.shape)
    count_ref[...] = carry_scr[...]

    row8 = lax.broadcasted_iota(I32, (SUBLANES, rows), 0)
    idx_ref[...] = jnp.where(row8 == 0, cls, jnp.where(row8 == 1, rank.astype(I32), 0))
    rowl = lax.broadcasted_iota(I32, (LANES, rows), 0)
    gate_ref[...] = jnp.where(rowl == GATE_LO, w_lo, jnp.where(rowl == GATE_HI, w_hi, 0.0)).T


def _moe_pair_kernel(ea_ref, eb_ref, used_ref, ha_ref, hb_ref, gate_ref, wg_a_ref, wu_a_ref,
                     wd_a_ref, wg_b_ref, wu_b_ref, wd_b_ref, after_ref, ya_ref, yb_ref, wgu_scr,
                     wd_scr):
    del after_ref
    i = pl.program_id(0)
    prev = jnp.maximum(i - 1, 0)
    new_class = (i == 0) | (ea_ref[i] != ea_ref[prev]) | (eb_ref[i] != eb_ref[prev])

    @pl.when(new_class)
    def _():
        for s, (wg_ref, wu_ref, wd_ref) in enumerate(((wg_a_ref, wu_a_ref, wd_a_ref),
                                                      (wg_b_ref, wu_b_ref, wd_b_ref))):
            wgu_scr[s, :, :D_EXPERT] = wg_ref[0].astype(BF16)
            wgu_scr[s, :, D_EXPERT:] = wu_ref[0].astype(BF16)
            wd_scr[s] = wd_ref[0].astype(BF16)

    @pl.when(i < used_ref[0])
    def _():
        h = _unpack_row(ha_ref[...], hb_ref[...]).astype(BF16)
        gates = gate_ref[...]
        y = None
        for s, lane in enumerate((GATE_LO, GATE_HI)):
            gu = _dot(h, wgu_scr[s])
            act = _silu(gu[:, :D_EXPERT]) * gu[:, D_EXPERT:]
            part = _dot((act * gates[:, lane:lane + 1]).astype(BF16), wd_scr[s])
            y = part if y is None else y + part
        ya_ref[...], yb_ref[...] = _pack_row(y)


def _final_norm_kernel(x_ref, ya_ref, yb_ref, g_ref, o_ref):
    o_ref[...] = _rmsnorm(x_ref[...] + _unpack_row(ya_ref[...], yb_ref[...]), g_ref[...])


def _moe_dense_kernel(x_ref, ng_ref, wr_cat_ref, wr_hi_ref, br_ref, wg_ref, wu_ref, wd_ref, nf_ref,
                      after_ref, y_ref, h_scr, gate_scr, acc_scr, *, final_norm):
    del after_ref
    e = pl.program_id(1)

    @pl.when(e == 0)
    def _():
        hn = _rmsnorm(x_ref[...], ng_ref[...])
        h_scr[...] = hn.astype(BF16)
        lg = _router_logits(hn, wr_cat_ref, wr_hi_ref, br_ref)
        lane = lax.broadcasted_iota(I32, lg.shape, 1)
        neg = jnp.float32(-jnp.inf)
        group_lg = jnp.where(lane < N_EXPERT_GROUPS, lg, neg)
        is_expert = (lane >= EXPERT_LANE0) & (lane < EXPERT_LANE0 + N_EXPERTS)
        _, i1, i2, w1, w2 = _select(group_lg, jnp.where(is_expert, lg, neg), lane,
                                    jnp.where(is_expert, lane - EXPERT_LANE0, N_EXPERTS), 1)
        gate_scr[...] = (jnp.where(lane == i1 + EXPERT_LANE0, w1, 0.0)
                         + jnp.where(lane == i2 + EXPERT_LANE0, w2, 0.0))
        acc_scr[...] = jnp.zeros_like(acc_scr)

    lane = lax.broadcasted_iota(I32, gate_scr.shape, 1)
    gate = jnp.sum(jnp.where(lane == EXPERT_LANE0 + e, gate_scr[...], 0.0), axis=-1, keepdims=True)
    h = h_scr[...]
    act = _silu(_dot(h, wg_ref[0].astype(BF16))) * _dot(h, wu_ref[0].astype(BF16))
    acc_scr[...] += _dot((act * gate).astype(BF16), wd_ref[0].astype(BF16))

    @pl.when(e == N_EXPERTS - 1)
    def _():
        y = x_ref[...] + acc_scr[...]
        if final_norm:
            y = _rmsnorm(y, nf_ref[...])
        y_ref[...] = y


def _const_spec(shape):
    return pl.BlockSpec(shape, lambda *_: (0,) * len(shape))


def _resident_spec(shape):
    return pl.BlockSpec(shape, lambda *_: (0,) * len(shape), pipeline_mode=pl.Buffered(1))


_ANY_SPEC = pl.BlockSpec(memory_space=pl.ANY)


def _rope_tables(pos):
    half = RET_HEAD_DIM // 2
    inv = np.float32(ROPE_BASE) ** (-np.arange(half, dtype=np.float32) / np.float32(half))
    ang = pos.astype(np.float32)[:, None] * inv[None, :]
    cos, sin = np.cos(ang), np.sin(ang)
    return np.concatenate([cos, cos], -1), np.concatenate([-sin, sin], -1)


def _decay_consts(c, reps):
    f = np.float32
    log_g = np.log(f(1.0) - f(2.0) ** (f(-5.0) - np.arange(RET_HEADS, dtype=f)))
    i = np.arange(c, dtype=f)
    diff = i[:, None] - i[None, :]
    dmat = np.where(diff[None] >= 0, np.exp(np.maximum(diff, f(0))[None] * log_g[:, None, None]), f(0))
    qdec = np.exp((i + f(1))[None, :] * log_g[:, None])
    kdec = np.exp((f(c) - f(1) - i)[None, :] * log_g[:, None])
    cdec = np.exp(f(c) * log_g)
    dmat = np.einsum('ab,hij->haibj', np.eye(reps, dtype=f), dmat).reshape(RET_HEADS, reps * c, reps * c)
    lanes = lambda t: np.repeat(np.tile(t, (1, reps)).T, RET_HEAD_DIM, axis=1)
    cdec = np.broadcast_to(cdec[:, None, None], (RET_HEADS, 1, RET_HEAD_DIM))
    return tuple(np.ascontiguousarray(a, dtype=f) for a in (dmat, lanes(qdec), lanes(kdec), cdec))


def _split_weight(w):
    hi = lax.bitcast_convert_type(_bf16_round_bits(lax.bitcast_convert_type(w, jnp.uint32)), F32)
    return hi.astype(BF16), (w - hi).astype(BF16)


_W_IN_BLOCK = (None, D_MODEL, IN_WIDTH)
_W_POOL_BLOCK = (None, len(POOL_WINDOWS), POOL_GROUP_DIM, POOL_GROUP_DIM)
_W_OUT_BLOCK = (None, D_MODEL, D_MODEL)
_DECAY_SPECS = [
    _const_spec((RET_HEADS, CHUNK, CHUNK)), _const_spec((CHUNK, RET_WIDTH)),
    _const_spec((CHUNK, RET_WIDTH)), _const_spec((RET_HEADS, 1, RET_HEAD_DIM)),
]


def _layer_spec(block, layer, resident=False):
    index_map = lambda *_: (layer,) + (0,) * (len(block) - 1)
    if resident:
        return pl.BlockSpec(block, index_map, pipeline_mode=pl.Buffered(1))
    return pl.BlockSpec(block, index_map)


def _mixer_weight_specs(layer, split):
    n = 2 if split else 1
    row = lambda width: _layer_spec((None, 1, width), layer)
    return [row(D_MODEL), *[_layer_spec(_W_IN_BLOCK, layer, True)] * n,
            *[_layer_spec(_W_POOL_BLOCK, layer)] * n, row(POOL_WIDTH), row(RET_WIDTH),
            *[_layer_spec(_W_OUT_BLOCK, layer, True)] * n]


def _chain_out_spec(block, depth, layer, has_prev):
    zeros = (0,) * (len(block) - 1)
    if has_prev:
        return pl.BlockSpec((None,) + block, lambda i, *_: (layer, i) + zeros)
    return pl.BlockSpec((depth,) + block, lambda i, *_: (0, i) + zeros)


def _mixer_prompt(x, moe_y, weights, route_w, prev_out, layer, rows, precise_tail, after):
    b, l, _ = x.shape
    t = b * l
    depth = weights[0].shape[0]
    assert l % rows == 0 and rows % CHUNK == 0
    cos2, sin2 = _rope_tables(np.arange(l))
    decay = _decay_consts(RET_CHUNK, 1)
    steps = l // rows
    tok = lambda i, c: (i, c, 0)
    if moe_y is None:
        ya = yb = jnp.zeros((1, rows, PACK_W), I32)
        y_spec = _const_spec((1, rows, PACK_W))
    else:
        ya, yb = (t.reshape(b, l, PACK_W) for t in moe_y)
        y_spec = pl.BlockSpec((1, rows, PACK_W), tok)
    has_prev = prev_out is not None
    n_in = 23
    flat = lambda i, c: (i * steps + c, 0)
    y, tails, states, *routed = pl.pallas_call(
        functools.partial(_mixer_prompt_kernel, rows=rows, steps=steps, moe_in=moe_y is not None,
                          precise_tail=precise_tail, layer=layer, has_prev=has_prev),
        grid=(b, steps),
        in_specs=[pl.BlockSpec((1, rows, D_MODEL), tok), y_spec, y_spec,
                  pl.BlockSpec((rows, RET_HEAD_DIM), lambda i, c: (c, 0)),
                  pl.BlockSpec((rows, RET_HEAD_DIM), lambda i, c: (c, 0)),
                  *_mixer_weight_specs(layer, True), *_DECAY_SPECS,
                  _layer_spec((None, 1, D_MODEL), layer), *_router_specs(layer), _ANY_SPEC,
                  *([_ANY_SPEC, _ANY_SPEC] if has_prev else [])],
        out_specs=[pl.BlockSpec((1, rows, D_MODEL), tok),
                   _chain_out_spec((1, POOL_BUF, POOL_WIDTH), depth, layer, has_prev),
                   _chain_out_spec((1, RET_HEADS, RET_HEAD_DIM, RET_HEAD_DIM), depth, layer, has_prev),
                   pl.BlockSpec((rows, PACK_W), flat), pl.BlockSpec((rows, PACK_W), flat),
                   pl.BlockSpec((rows, LANES), flat),
                   pl.BlockSpec((SUBLANES, rows), lambda i, c: (0, i * steps + c)),
                   _const_spec((CLASS_ROWS, LANES))],
        out_shape=[jax.ShapeDtypeStruct(x.shape, F32),
                   jax.ShapeDtypeStruct((depth, b, POOL_BUF, POOL_WIDTH), F32),
                   jax.ShapeDtypeStruct((depth, b, RET_HEADS, RET_HEAD_DIM, RET_HEAD_DIM), F32),
                   jax.ShapeDtypeStruct((t, PACK_W), I32), jax.ShapeDtypeStruct((t, PACK_W), I32),
                   jax.ShapeDtypeStruct((t, LANES), F32), jax.ShapeDtypeStruct((SUBLANES, t), I32),
                   jax.ShapeDtypeStruct((CLASS_ROWS, LANES), F32)],
        input_output_aliases={n_in: 1, n_in + 1: 2} if has_prev else {},
        scratch_shapes=[pltpu.VMEM((1, POOL_HIST + rows, POOL_WIDTH), F32),
                        pltpu.VMEM((RET_HEADS, RET_HEAD_DIM, RET_HEAD_DIM), F32),
                        pltpu.VMEM((rows, RET_WIDTH), F32),
                        pltpu.VMEM((rows, IN_WIDTH), F32),
                        pltpu.VMEM((CLASS_ROWS, LANES), F32)],
        compiler_params=pltpu.CompilerParams(
            dimension_semantics=("arbitrary", "arbitrary"), vmem_limit_bytes=VMEM_LIMIT),
        name="mixer_prompt",
    )(x, ya, yb, cos2, sin2, *weights, *decay, *route_w, after, *(prev_out if has_prev else ()))
    return y, (tails, states), tuple(routed)


def _mixer_sample(x, pool_prev, s0, weights, prev_out, layer, after):
    b, l, _ = x.shape
    depth = s0.shape[0]
    ns = SAMPLE_SEQS
    assert ns * l == CHUNK and b % ns == 0
    cos2, sin2 = _rope_tables(PAST_LEN + np.arange(l))
    cos2, sin2 = np.tile(cos2, (ns, 1)), np.tile(sin2, (ns, 1))
    decay = _decay_consts(l, ns)
    seq3 = lambda i: (i, 0, 0)
    state_block = (ns, RET_HEADS, RET_HEAD_DIM, RET_HEAD_DIM)
    tail_block = (POOL_BUF, ns, POOL_WIDTH)
    if prev_out is None:
        tail_out = pl.BlockSpec((depth,) + tail_block, lambda i: (0, 0, i, 0))
    else:
        tail_out = pl.BlockSpec((None,) + tail_block, lambda i: (layer, 0, i, 0))
    has_prev = prev_out is not None
    n_in = 16
    y, tails, states = pl.pallas_call(
        functools.partial(_mixer_sample_kernel, seq_len=l, layer=layer, has_prev=has_prev),
        grid=(b // ns,),
        in_specs=[pl.BlockSpec((ns, l, D_MODEL), seq3),
                  pl.BlockSpec((None,) + tail_block, lambda i: (layer, 0, i, 0)),
                  pl.BlockSpec((None,) + state_block, lambda i: (layer, i, 0, 0, 0)),
                  _const_spec((CHUNK, RET_HEAD_DIM)), _const_spec((CHUNK, RET_HEAD_DIM)),
                  *_mixer_weight_specs(layer, False), *_DECAY_SPECS, _ANY_SPEC,
                  *([_ANY_SPEC, _ANY_SPEC] if has_prev else [])],
        out_specs=[pl.BlockSpec((ns, l, D_MODEL), seq3), tail_out,
                   _chain_out_spec(state_block, depth, layer, has_prev)],
        out_shape=[jax.ShapeDtypeStruct(x.shape, F32),
                   jax.ShapeDtypeStruct((depth, POOL_BUF, b, POOL_WIDTH), F32),
                   jax.ShapeDtypeStruct(s0.shape, F32)],
        input_output_aliases={n_in: 1, n_in + 1: 2} if has_prev else {},
        scratch_shapes=[pltpu.VMEM((ns, POOL_HIST + l, POOL_WIDTH), F32),
                        pltpu.VMEM((CHUNK, RET_WIDTH), F32)],
        compiler_params=pltpu.CompilerParams(
            dimension_semantics=("arbitrary",), vmem_limit_bytes=VMEM_LIMIT),
        name="mixer_sample",
    )(x, pool_prev, s0, cos2, sin2, *weights, *decay, after, *(prev_out if has_prev else ()))
    return y, (tails, states)


def _router_weights(w_rg, b_rg, w_re, b_re):
    depth = w_rg.shape[0]
    gap = EXPERT_LANE0 - N_EXPERT_GROUPS
    rest = LANES - EXPERT_LANE0 - N_EXPERTS
    wr = jnp.concatenate([w_rg, jnp.zeros((depth, D_MODEL, gap), F32), w_re,
                          jnp.zeros((depth, D_MODEL, rest), F32)], axis=-1)
    br = jnp.concatenate([b_rg, jnp.zeros((depth, gap), F32), b_re, jnp.zeros((depth, rest), F32)],
                         axis=-1).reshape(depth, 1, LANES)
    wr_hi, wr_lo = _split_weight(wr)
    return jnp.concatenate([wr_hi, wr_lo], axis=-1), wr_hi, br


def _router_specs(layer):
    return [_layer_spec((None, D_MODEL, 2 * LANES), layer), _layer_spec((None, D_MODEL, LANES), layer),
            _layer_spec((None, 1, LANES), layer)]


def _sc_mesh():
    return plsc.VectorSubcoreMesh(core_axis_name="core", subcore_axis_name="subcore",
                                  num_cores=SC_CORES, num_subcores=SC_SUBCORES)


def _sc_params():
    params = pltpu.CompilerParams()
    if "needs_layout_passes" in pltpu.CompilerParams.__dataclass_fields__:
        params = dataclasses.replace(params, needs_layout_passes=False)
    return params


def _sc_gather(tables, idx):
    n = idx.shape[0]
    assert n % SC_WINDOW == 0
    nt = len(tables)

    def body(*refs):
        i_hbm = refs[nt]
        for t_hbm, o_hbm in zip(refs[:nt], refs[nt + 1:]):
            def gather_window(i_vmem, o_vmem, t_hbm=t_hbm):
                pltpu.sync_copy(t_hbm.at[i_vmem.at[0]], o_vmem)

            pltpu.emit_pipeline(
                gather_window, grid=(n // SC_WINDOW,),
                in_specs=[pl.BlockSpec((1, SC_WINDOW), lambda i: (0, i))],
                out_specs=[pl.BlockSpec((SC_WINDOW, t_hbm.shape[1]), lambda i: (i, 0))],
                core_axis_name=("core", "subcore"),
                dimension_semantics=(pltpu.PARALLEL,),
            )(i_hbm, o_hbm)

    out_type = tuple(jax.ShapeDtypeStruct((n, t.shape[1]), t.dtype) for t in tables)
    return pl.kernel(body, out_type=out_type, mesh=_sc_mesh(), name="sc_gather")(
        *tables, idx.reshape(1, n))


def _sc_slots(cls, rank, starts, n_slots):
    t = cls.shape[0]
    workers = SC_CORES * SC_SUBCORES
    slot_per, tok_per = n_slots // workers, t // workers
    assert n_slots % (workers * SC_LANES) == 0 and t % (workers * SC_LANES) == 0 and t & (t - 1) == 0
    assert n_slots % (SC_LANES * SC_UNROLL) == 0 and t % (SC_LANES * SC_UNROLL) == 0

    def body(cls_hbm, rank_hbm, starts_hbm, pos_hbm, slot_hbm, cls_v, rank_v, starts_v, pos_v, slot_v):
        wid = lax.axis_index("subcore") * SC_CORES + lax.axis_index("core")
        pltpu.sync_copy(cls_hbm, cls_v)
        pltpu.sync_copy(rank_hbm, rank_v)
        pltpu.sync_copy(starts_hbm, starts_v)

        lane = lax.iota(I32, SC_LANES)
        span = SC_LANES * SC_UNROLL

        @pl.loop(0, n_slots, step=span)
        def _(i):
            for u in range(SC_UNROLL):
                j = i + u * SC_LANES
                slot_v[pl.ds(j, SC_LANES)] = (lane + j) & (t - 1)

        @pl.loop(0, t, step=span)
        def _(i):
            for u in range(SC_UNROLL):
                j = i + u * SC_LANES
                at = pl.ds(j, SC_LANES)
                pos = plsc.load_gather(starts_v, [cls_v[at]]) + rank_v[at]
                pos_v[at] = pos
                plsc.store_scatter(slot_v, [pos], lane + j)

        tok_off = pl.multiple_of(wid * tok_per, SC_LANES)
        pltpu.sync_copy(pos_v.at[pl.ds(tok_off, tok_per)], pos_hbm.at[pl.ds(tok_off, tok_per)])
        slot_off = pl.multiple_of(wid * slot_per, SC_LANES)
        pltpu.sync_copy(slot_v.at[pl.ds(slot_off, slot_per)], slot_hbm.at[pl.ds(slot_off, slot_per)])

    return pl.kernel(
        body, mesh=_sc_mesh(), compiler_params=_sc_params(), name="sc_slots",
        out_type=(jax.ShapeDtypeStruct((t,), I32), jax.ShapeDtypeStruct((n_slots,), I32)),
        scratch_types=[pltpu.VMEM((t,), I32), pltpu.VMEM((t,), I32), pltpu.VMEM((CLASS_ROWS,), I32),
                       pltpu.VMEM((t,), I32), pltpu.VMEM((n_slots,), I32)],
    )(cls, rank, starts)


def _moe_experts(routed, w_gate, w_up, w_down, layer, after):
    ha, hb, gates, idx, counts = routed
    t = ha.shape[0]
    n_slots = t + N_CLASSES * PAIR_TILE
    n_tiles = n_slots // PAIR_TILE

    cnt = counts[:, 0].astype(I32)
    padded = (cnt + PAIR_TILE - 1) // PAIR_TILE * PAIR_TILE
    ends = jnp.cumsum(padded)
    starts = ends - padded
    tile_cls = jnp.minimum(
        jnp.sum(ends[None, :N_CLASSES] <= (jnp.arange(n_tiles, dtype=I32) * PAIR_TILE)[:, None], axis=1),
        N_CLASSES - 1).astype(I32)
    first = (tile_cls // PAIRS_PER_GROUP) * EXPERTS_PER_GROUP
    tile_ea = first + jnp.asarray(PAIR_LO, I32)[tile_cls % PAIRS_PER_GROUP]
    tile_eb = first + jnp.asarray(PAIR_HI, I32)[tile_cls % PAIRS_PER_GROUP]
    used = (ends[N_CLASSES - 1:N_CLASSES] // PAIR_TILE).astype(I32)

    pos, slot_tok = _sc_slots(idx[0], idx[1], starts, n_slots)
    hsa, hsb, gate_s = _sc_gather((ha, hb, gates), slot_tok)

    row = lambda i, ea, eb, nu: (jnp.minimum(i, nu[0] - 1), 0)
    w_spec = lambda shape, which: pl.BlockSpec(
        (None, 1) + shape, lambda i, ea, eb, nu: (layer, (ea, eb)[which][i], 0, 0))
    gu_shape, d_shape = (D_MODEL, D_EXPERT), (D_EXPERT, D_MODEL)
    ysa, ysb = pl.pallas_call(
        _moe_pair_kernel,
        grid_spec=pltpu.PrefetchScalarGridSpec(
            num_scalar_prefetch=3, grid=(n_tiles,),
            in_specs=[pl.BlockSpec((PAIR_TILE, PACK_W), row), pl.BlockSpec((PAIR_TILE, PACK_W), row),
                      pl.BlockSpec((PAIR_TILE, LANES), row),
                      w_spec(gu_shape, 0), w_spec(gu_shape, 0), w_spec(d_shape, 0),
                      w_spec(gu_shape, 1), w_spec(gu_shape, 1), w_spec(d_shape, 1), _ANY_SPEC],
            out_specs=[pl.BlockSpec((PAIR_TILE, PACK_W), row), pl.BlockSpec((PAIR_TILE, PACK_W), row)],
            scratch_shapes=[pltpu.VMEM((2, D_MODEL, 2 * D_EXPERT), BF16),
                            pltpu.VMEM((2, D_EXPERT, D_MODEL), BF16)]),
        out_shape=[jax.ShapeDtypeStruct((n_slots, PACK_W), I32),
                   jax.ShapeDtypeStruct((n_slots, PACK_W), I32)],
        compiler_params=pltpu.CompilerParams(
            dimension_semantics=("arbitrary",), vmem_limit_bytes=VMEM_LIMIT),
        name="moe_pair",
    )(tile_ea, tile_eb, used, hsa, hsb, gate_s, w_gate, w_up, w_down, w_gate, w_up, w_down, after)
    return (ysa, ysb), pos


def _final_norm(x, moe_y, g, rows):
    t = x.shape[0]
    tok = lambda i: (i, 0)
    return pl.pallas_call(
        _final_norm_kernel,
        grid=(t // rows,),
        in_specs=[pl.BlockSpec((rows, D_MODEL), tok), pl.BlockSpec((rows, PACK_W), tok),
                  pl.BlockSpec((rows, PACK_W), tok), _const_spec((1, D_MODEL))],
        out_specs=pl.BlockSpec((rows, D_MODEL), tok),
        out_shape=jax.ShapeDtypeStruct(x.shape, F32),
        compiler_params=pltpu.CompilerParams(
            dimension_semantics=("arbitrary",), vmem_limit_bytes=VMEM_LIMIT),
        name="final_norm",
    )(x, *moe_y, g.reshape(1, D_MODEL))


def _moe_dense(x, norm_g, router, w_gate, w_up, w_down, norm_final, layer, final_norm, rows, after):
    t = x.shape[0]
    assert t % rows == 0
    tok = lambda i, e: (i, 0)
    w_spec = lambda shape: pl.BlockSpec((None, 1) + shape, lambda i, e: (layer, e, 0, 0))
    return pl.pallas_call(
        functools.partial(_moe_dense_kernel, final_norm=final_norm),
        grid=(t // rows, N_EXPERTS),
        in_specs=[pl.BlockSpec((rows, D_MODEL), tok), _layer_spec((None, 1, D_MODEL), layer),
                  *_router_specs(layer),
                  w_spec((D_MODEL, D_EXPERT)), w_spec((D_MODEL, D_EXPERT)), w_spec((D_EXPERT, D_MODEL)),
                  _const_spec((1, D_MODEL)), _ANY_SPEC],
        out_specs=pl.BlockSpec((rows, D_MODEL), tok),
        out_shape=jax.ShapeDtypeStruct(x.shape, F32),
        scratch_shapes=[pltpu.VMEM((rows, D_MODEL), BF16),
                        pltpu.VMEM((rows, LANES), F32),
                        pltpu.VMEM((rows, D_MODEL), F32)],
        compiler_params=pltpu.CompilerParams(
            dimension_semantics=("arbitrary", "arbitrary"), vmem_limit_bytes=VMEM_LIMIT),
        name="moe_dense",
    )(x, norm_g, *router, w_gate, w_up, w_down, norm_final.reshape(1, D_MODEL), after)


def kernel(x_prompt, x_sample, cache_pool, state_ret, norm_mix, w_in, w_pool, pool_scale, ret_gn, w_out, norm_ffn, w_router_group, b_router_group, w_router_expert, b_router_expert, w_gate, w_up, w_down, norm_final):
    depth = norm_mix.shape[0]
    row = lambda a: a.reshape(depth, 1, a.shape[-1])
    mix_split = (row(norm_mix), *_split_weight(w_in), *_split_weight(w_pool), row(pool_scale),
                 row(ret_gn), *_split_weight(w_out))
    mix_hi = tuple(mix_split[i] for i in (0, 1, 3, 5, 6, 7))
    router = _router_weights(w_router_group, b_router_group, w_router_expert, b_router_expert)
    norm_ffn = row(norm_ffn)
    pool_prev = jnp.swapaxes(cache_pool, 1, 2)

    yp, ys = x_prompt, x_sample
    moe_p = None
    out_p = out_s = None
    for l in range(depth):
        yp, out_p, routed = _mixer_prompt(
            yp, moe_p, mix_split, (norm_ffn, *router), out_p, l, rows=512,
            precise_tail=PRECISE_TAIL_STEPS if l < depth - 1 else 0, after=ys)
        ys, out_s = _mixer_sample(ys, pool_prev, state_ret, mix_hi, out_s, l, after=routed[-1])
        sorted_y, pos = _moe_experts(routed, w_gate, w_up, w_down, l, after=ys)
        ys = _moe_dense(ys.reshape(-1, D_MODEL), norm_ffn, router, w_gate, w_up, w_down, norm_final,
                        l, l == depth - 1, rows=1024, after=sorted_y[0]).reshape(ys.shape)
        moe_p = _sc_gather(sorted_y, pos)
    yp = _final_norm(yp.reshape(-1, D_MODEL), moe_p, norm_final, rows=1024).reshape(yp.shape)
    return (yp, ys, *out_p, jnp.swapaxes(out_s[0], 1, 2), out_s[1])
```

```python
import dataclasses
import functools

import jax
import jax.numpy as jnp
import numpy as np
from jax import lax
from jax.experimental import pallas as pl
from jax.experimental.pallas import tpu as pltpu
from jax.experimental.pallas import tpu_sc as plsc

F32 = jnp.float32
BF16 = jnp.bfloat16
I32 = jnp.int32

D_MODEL = 1024
POOL_WIDTH = 512
POOL_WINDOWS = (2, 4, 8, 16)
POOL_GROUP_DIM = 128
POOL_BUF = 15
POOL_HIST = 16
RET_WIDTH = 512
RET_HEADS = 4
RET_HEAD_DIM = 128
RET_CHUNK = 128
ROPE_BASE = 10000.0
IN_WIDTH = POOL_WIDTH + 4 * RET_WIDTH
N_EXPERT_GROUPS = 4
EXPERTS_PER_GROUP = 4
N_EXPERTS = 16
D_EXPERT = 256
RMS_EPS = 1e-6
GN_EPS = 1e-5
PAST_LEN = 16384

LANES = 128
SUBLANES = 8
EXPERT_LANE0 = 8
PAIRS_PER_GROUP = 6
N_CLASSES = N_EXPERT_GROUPS * PAIRS_PER_GROUP
CLASS_ROWS = 32
PAIR_LO = (0, 0, 0, 1, 1, 2)
PAIR_HI = (1, 2, 3, 2, 3, 3)
GATE_LO, GATE_HI = 0, 1
COUNT_BLOCK = 256
PAIR_TILE = 256
PACK_W = D_MODEL // 4
SC_CORES, SC_SUBCORES, SC_LANES = 2, 16, 16
SC_WINDOW = 128
SC_UNROLL = 8
PRECISE_TAIL_STEPS = 1
CHUNK = 128
SAMPLE_SEQS = 16
VMEM_LIMIT = 56 * 1024 * 1024


def _dot(a, b):
    return jnp.dot(a, b, preferred_element_type=F32)


def _dot_nt(a, b):
    return lax.dot_general(a, b, (((1,), (1,)), ((), ())), preferred_element_type=F32)


def _bf16_round_bits(u):
    return (u + jnp.uint32(0x7FFF) + ((u >> 16) & jnp.uint32(1))) & jnp.uint32(0xFFFF0000)


def _split(a):
    hi = pltpu.bitcast(_bf16_round_bits(pltpu.bitcast(a, jnp.uint32)), F32)
    return hi.astype(BF16), (a - hi).astype(BF16)


def _mm(a, b, precise, nt=False):
    dot = _dot_nt if nt else _dot
    if precise:
        b_hi, b_lo = b if isinstance(b, tuple) else _split(b)
        a_hi, a_lo = _split(a)
        return dot(a_hi, b_hi) + dot(a_lo, b_hi) + dot(a_hi, b_lo)
    return dot(a.astype(BF16), b[0] if isinstance(b, tuple) else b.astype(BF16))


def _rmsnorm(x, g):
    ms = jnp.mean(x * x, axis=-1, keepdims=True)
    return x * lax.rsqrt(ms + RMS_EPS) * g


def _pool_mix(ubuf, rows, t_first, n_prev, wpool_refs, pscale, precise=False, row0=0):
    ns = ubuf.shape[0]
    t = t_first + lax.broadcasted_iota(I32, (1, rows, POOL_GROUP_DIM), 1)
    base = POOL_HIST + row0
    outs = []
    for j, w in enumerate(POOL_WINDOWS):
        lanes = slice(j * POOL_GROUP_DIM, (j + 1) * POOL_GROUP_DIM)
        uj = ubuf[:, base:base + rows, lanes]
        acc = uj
        for i in range(1, w):
            acc = acc + ubuf[:, base - i:base - i + rows, lanes]
        cnt = jnp.minimum(w, n_prev + t + 1).astype(F32)
        d = (acc / cnt - uj).reshape(ns * rows, POOL_GROUP_DIM)
        outs.append(_mm(d, tuple(w[j] for w in wpool_refs), precise))
    return jnp.concatenate(outs, axis=-1) * pscale


def _rope(xh, cos2, sin2):
    return xh * cos2 + pltpu.roll(xh, RET_HEAD_DIM // 2, 1) * sin2


def _group_norm(o):
    mu = jnp.mean(o, axis=-1, keepdims=True)
    c = o - mu
    var = jnp.mean(c * c, axis=-1, keepdims=True)
    return c * lax.rsqrt(var + GN_EPS)


def _silu(x):
    return x * (1.0 / (1.0 + jnp.exp(-x)))


def _head(a, h):
    return a[:, h * RET_HEAD_DIM:(h + 1) * RET_HEAD_DIM]


def _qkvg(z):
    p, r = POOL_WIDTH, RET_WIDTH
    return z[:, p:p + r], z[:, p + r:p + 2 * r], z[:, p + 2 * r:p + 3 * r], z[:, p + 3 * r:p + 4 * r]


def _pack_bf16_pair(a, b):
    ua = pltpu.bitcast(a.astype(BF16).astype(F32), jnp.uint32)
    ub = pltpu.bitcast(b.astype(BF16).astype(F32), jnp.uint32)
    return pltpu.bitcast((ua >> 16) | (ub & jnp.uint32(0xFFFF0000)), I32)


def _unpack_bf16_pair(w):
    u = pltpu.bitcast(w, jnp.uint32)
    return pltpu.bitcast(u << 16, F32), pltpu.bitcast(u & jnp.uint32(0xFFFF0000), F32)


def _pack_row(y):
    q = PACK_W
    return _pack_bf16_pair(y[:, 0:q], y[:, q:2 * q]), _pack_bf16_pair(y[:, 2 * q:3 * q], y[:, 3 * q:])


def _unpack_row(wa, wb):
    return jnp.concatenate([*_unpack_bf16_pair(wa), *_unpack_bf16_pair(wb)], axis=-1)


def _zero_other_layers(ref, layer):
    for j in range(ref.shape[0]):
        if j != layer:
            ref[j] = jnp.zeros(ref.shape[1:], ref.dtype)


def _mixer_prompt_kernel(*refs, rows, steps, moe_in, precise_tail, layer, has_prev):
    (x_ref, ya_ref, yb_ref, cos_ref, sin_ref, ng_ref, win_hi_ref, win_lo_ref, wpool_hi_ref,
     wpool_lo_ref, pscale_ref, gn_ref, wout_hi_ref, wout_lo_ref, dmat_ref, qdec_ref, kdec_ref,
     cdec_ref) = refs[:18]
    route_in = refs[18:22]
    y_ref, tail_ref, sfin_ref, *route_out, ubuf, s_scr, ret_scr, z_scr, count_scr = (
        refs[23 + 2 * has_prev:])
    c = pl.program_id(1)

    @pl.when((pl.program_id(0) == 0) & (c == 0))
    def _():
        count_scr[...] = jnp.zeros_like(count_scr)
    if not has_prev:
        _zero_other_layers(tail_ref, layer)
        _zero_other_layers(sfin_ref, layer)
        tail_ref, sfin_ref = tail_ref.at[layer], sfin_ref.at[layer]

    @pl.when(c == 0)
    def _():
        ubuf[:, 0:POOL_HIST, :] = jnp.zeros((1, POOL_HIST, POOL_WIDTH), F32)
        s_scr[...] = jnp.zeros_like(s_scr)

    kv_cols = slice(POOL_WIDTH + RET_WIDTH, POOL_WIDTH + 3 * RET_WIDTH)

    def step(kv_precise, full_from):
        x = x_ref[0]
        if moe_in:
            x = x + _unpack_row(ya_ref[0], yb_ref[0])
        hn = _rmsnorm(x, ng_ref[...])
        hi, lo = _split(hn) if kv_precise else (hn.astype(BF16), None)
        z_scr[...] = _dot(hi, win_hi_ref[...])
        if kv_precise and full_from:
            z_scr[:full_from, kv_cols] += (_dot(lo[:full_from], win_hi_ref[:, kv_cols])
                                           + _dot(hi[:full_from], win_lo_ref[:, kv_cols]))
        if full_from < rows:
            z_scr[full_from:, :] += (_dot(lo[full_from:], win_hi_ref[...])
                                     + _dot(hi[full_from:], win_lo_ref[...]))

        ubuf[0, POOL_HIST:POOL_HIST + rows, :] = z_scr[:, :POOL_WIDTH]
        pool_w = (wpool_hi_ref, wpool_lo_ref)
        pool_parts = []
        if full_from:
            pool_parts.append(_pool_mix(ubuf, full_from, c * rows, 0, pool_w, pscale_ref[...]))
        if full_from < rows:
            pool_parts.append(_pool_mix(ubuf, rows - full_from, c * rows + full_from, 0, pool_w,
                                        pscale_ref[...], precise=True, row0=full_from))
        pool_out = jnp.concatenate(pool_parts, axis=0)
        tail_ref[...] = ubuf[:, rows + POOL_HIST - POOL_BUF:rows + POOL_HIST, :]
        ubuf[:, 0:POOL_HIST, :] = ubuf[:, rows:rows + POOL_HIST, :]

        scale = RET_HEAD_DIM ** -0.5
        for ci in range(rows // CHUNK):
            rs = slice(ci * CHUNK, (ci + 1) * CHUNK)
            full = ci * CHUNK >= full_from
            cos2 = cos_ref[rs, :]
            sin2 = sin_ref[rs, :]
            for h in range(RET_HEADS):
                col = lambda part: slice(POOL_WIDTH + part * RET_WIDTH + h * RET_HEAD_DIM,
                                         POOL_WIDTH + part * RET_WIDTH + (h + 1) * RET_HEAD_DIM)
                q = _rope(z_scr[rs, col(0)], cos2, sin2)
                k = _rope(z_scr[rs, col(1)], cos2, sin2) * scale
                v = z_scr[rs, col(2)]
                s_old = s_scr[h]
                scores = _mm(q, k, full, nt=True) * dmat_ref[h]
                qd = q * _head(qdec_ref[...], h)
                o = _mm(scores, v, full) + _mm(qd, s_old, full)
                kd = k * _head(kdec_ref[...], h)
                s_scr[h] = s_old * cdec_ref[h] + _mm(kd.T, v, kv_precise)
                on = _group_norm(o) * _head(gn_ref[...], h)
                ret_scr[rs, h * RET_HEAD_DIM:(h + 1) * RET_HEAD_DIM] = _silu(z_scr[rs, col(3)]) * on

        mix_in = jnp.concatenate([pool_out, ret_scr[...]], axis=-1)
        y_ref[0] = x + _dot(mix_in.astype(BF16), wout_hi_ref[...])
        if full_from < rows:
            m_hi, m_lo = _split(mix_in[full_from:])
            y_ref[0, full_from:, :] += _dot(m_lo, wout_hi_ref[...]) + _dot(m_hi, wout_lo_ref[...])
        sfin_ref[0] = s_scr[...]
        _route_tile(y_ref[0], *route_in, *route_out, count_scr)

    if precise_tail:
        pl.when(c < steps - precise_tail)(lambda: step(False, rows))
        if precise_tail > 1:
            pl.when((c >= steps - precise_tail) & (c < steps - 1))(lambda: step(True, rows))
        pl.when(c == steps - 1)(lambda: step(True, rows - CHUNK))
    else:
        step(False, rows)


def _mixer_sample_kernel(*refs, seq_len, layer, has_prev):
    (x_ref, prev_ref, s0_ref, cos_ref, sin_ref, ng_ref, win_ref, wpool_ref, pscale_ref, gn_ref,
     wout_ref, dmat_ref, qdec_ref, kdec_ref, cdec_ref) = refs[:15]
    y_ref, tail_ref, sfin_ref, ubuf, ret_scr = refs[16 + 2 * has_prev:]
    if not has_prev:
        _zero_other_layers(tail_ref, layer)
        _zero_other_layers(sfin_ref, layer)
        tail_ref, sfin_ref = tail_ref.at[layer], sfin_ref.at[layer]
    ns = SAMPLE_SEQS
    rows = ns * seq_len
    x = x_ref[...].reshape(rows, D_MODEL)
    hn = _rmsnorm(x, ng_ref[...]).astype(BF16)
    z = _dot(hn, win_ref[...])
    for j in range(POOL_BUF):
        ubuf[:, POOL_HIST - POOL_BUF + j, :] = prev_ref[j]
    ubuf[:, POOL_HIST:POOL_HIST + seq_len, :] = z[:, :POOL_WIDTH].reshape(ns, seq_len, POOL_WIDTH)
    pool_out = _pool_mix(ubuf, seq_len, 0, POOL_BUF, (wpool_ref,), pscale_ref[...])
    for j in range(POOL_BUF):
        tail_ref[j] = ubuf[:, seq_len + POOL_HIST - POOL_BUF + j, :]

    q_all, k_all, v_all, g_all = _qkvg(z)
    scale = RET_HEAD_DIM ** -0.5
    cos2 = cos_ref[...]
    sin2 = sin_ref[...]
    tok_seq = lax.broadcasted_iota(I32, (RET_HEAD_DIM, rows), 1) // seq_len
    for h in range(RET_HEADS):
        q = _rope(_head(q_all, h), cos2, sin2)
        k = _rope(_head(k_all, h), cos2, sin2) * scale
        vb = _head(v_all, h).astype(BF16)
        s_old = s0_ref[:, h]
        scores = _dot_nt(q.astype(BF16), k.astype(BF16)) * dmat_ref[h]
        qd = (q * _head(qdec_ref[...], h)).astype(BF16).reshape(ns, seq_len, RET_HEAD_DIM)
        o_state = jnp.einsum('bid,bde->bie', qd, s_old.astype(BF16), preferred_element_type=F32)
        o = _dot(scores.astype(BF16), vb) + o_state.reshape(rows, RET_HEAD_DIM)
        kdt = (k * _head(kdec_ref[...], h)).T
        lhs = jnp.concatenate(
            [jnp.where(tok_seq == b, kdt, 0.0).astype(BF16) for b in range(ns)], axis=0)
        upd = _dot(lhs, vb).reshape(ns, RET_HEAD_DIM, RET_HEAD_DIM)
        sfin_ref[:, h] = s_old * cdec_ref[h] + upd
        on = _group_norm(o) * _head(gn_ref[...], h)
        ret_scr[:, h * RET_HEAD_DIM:(h + 1) * RET_HEAD_DIM] = _silu(_head(g_all, h)) * on

    mix_in = jnp.concatenate([pool_out, ret_scr[...]], axis=-1).astype(BF16)
    y_ref[...] = (x + _dot(mix_in, wout_ref[...])).reshape(ns, seq_len, D_MODEL)


def _router_logits(hn, wr_cat_ref, wr_hi_ref, br_ref):
    h_hi, h_lo = _split(hn)
    part = _dot(h_hi, wr_cat_ref[...])
    return part[:, :LANES] + part[:, LANES:] + _dot(h_lo, wr_hi_ref[...]) + br_ref[...]


def _select(group_lg, expert_lg, gidx, eidx, axis):
    red = dict(axis=axis, keepdims=True)
    neg = jnp.float32(-jnp.inf)
    gmax = jnp.max(group_lg, **red)
    g_sel = jnp.min(jnp.where(group_lg == gmax, gidx, N_EXPERT_GROUPS), **red)
    p_sel = 1.0 / jnp.sum(jnp.exp(group_lg - gmax), **red)
    emask = (eidx >> 2) == g_sel
    v1 = jnp.max(jnp.where(emask, expert_lg, neg), **red)
    i1 = jnp.min(jnp.where(emask & (expert_lg == v1), eidx, N_EXPERTS), **red)
    emask2 = emask & (eidx != i1)
    v2 = jnp.max(jnp.where(emask2, expert_lg, neg), **red)
    i2 = jnp.min(jnp.where(emask2 & (expert_lg == v2), eidx, N_EXPERTS), **red)
    e2 = jnp.exp(v2 - v1)
    return g_sel, i1, i2, p_sel / (1.0 + e2), p_sel * e2 / (1.0 + e2)


def _route_tile(x, ng_ref, wr_cat_ref, wr_hi_ref, br_ref, ha_ref, hb_ref, gate_ref, idx_ref,
                count_ref, carry_scr):
    rows = x.shape[0]
    hn = _rmsnorm(x, ng_ref[...])
    ha_ref[...], hb_ref[...] = _pack_row(hn)
    lgt = _router_logits(hn, wr_cat_ref, wr_hi_ref, br_ref).T
    neg = jnp.float32(-jnp.inf)
    gidx = lax.broadcasted_iota(I32, (SUBLANES, rows), 0)
    eidx = lax.broadcasted_iota(I32, (N_EXPERTS, rows), 0)
    group_lg = jnp.where(gidx < N_EXPERT_GROUPS, lgt[0:SUBLANES], neg)
    expert_lg = lgt[EXPERT_LANE0:EXPERT_LANE0 + N_EXPERTS]
    g_sel, i1, i2, w1, w2 = _select(group_lg, expert_lg, gidx, eidx, 0)
    lo = jnp.minimum(i1, i2) - g_sel * EXPERTS_PER_GROUP
    hi = jnp.maximum(i1, i2) - g_sel * EXPERTS_PER_GROUP
    cls = g_sel * PAIRS_PER_GROUP + ((lo * (7 - lo)) >> 1) + (hi - lo - 1)
    w_lo = jnp.where(i1 < i2, w1, w2)
    w_hi = jnp.where(i1 < i2, w2, w1)

    crow = lax.broadcasted_iota(I32, (CLASS_ROWS, rows), 0)
    onehot = jnp.where(crow == cls, 1.0, 0.0)
    n_blk = rows // COUNT_BLOCK
    blocks = [onehot[:, j * COUNT_BLOCK:(j + 1) * COUNT_BLOCK] for j in range(n_blk)]
    r = lax.broadcasted_iota(I32, (COUNT_BLOCK, COUNT_BLOCK), 0)
    c = lax.broadcasted_iota(I32, (COUNT_BLOCK, COUNT_BLOCK), 1)
    upper = jnp.where(r < c, 1.0, 0.0).astype(BF16)
    within = _dot(jnp.concatenate(blocks, axis=0).astype(BF16), upper)
    carry = carry_scr[:, 0:1]
    ranks = []
    for j in range(n_blk):
        before = within[j * CLASS_ROWS:(j + 1) * CLASS_ROWS] + carry
        ranks.append(jnp.sum(blocks[j] * before, axis=0, keepdims=True))
        carry = carry + jnp.sum(blocks[j], axis=1, keepdims=True)
    rank = jnp.concatenate(ranks, axis=1)
    carry_scr[...] = jnp.broadcast_to(carry, carry_scr.shape)
    count_ref[...] = carry_scr[...]

    row8 = lax.broadcasted_iota(I32, (SUBLANES, rows), 0)
    idx_ref[...] = jnp.where(row8 == 0, cls, jnp.where(row8 == 1, rank.astype(I32), 0))
    rowl = lax.broadcasted_iota(I32, (LANES, rows), 0)
    gate_ref[...] = jnp.where(rowl == GATE_LO, w_lo, jnp.where(rowl == GATE_HI, w_hi, 0.0)).T


def _expert_cast_kernel(wg_ref, wu_ref, wd_ref, after_ref, wgu_ref, wdb_ref):
    del after_ref
    wgu_ref[0, :, :D_EXPERT] = wg_ref[0].astype(BF16)
    wgu_ref[0, :, D_EXPERT:] = wu_ref[0].astype(BF16)
    wdb_ref[0] = wd_ref[0].astype(BF16)


def _moe_pair_kernel(ea_ref, eb_ref, used_ref, ha_ref, hb_ref, gate_ref, wgu_a_ref, wd_a_ref,
                     wgu_b_ref, wd_b_ref, after_ref, ya_ref, yb_ref):
    del ea_ref, eb_ref, after_ref

    @pl.when(pl.program_id(0) < used_ref[0])
    def _():
        h = _unpack_row(ha_ref[...], hb_ref[...]).astype(BF16)
        gates = gate_ref[...]
        y = None
        for wgu_ref, wd_ref, lane in ((wgu_a_ref, wd_a_ref, GATE_LO), (wgu_b_ref, wd_b_ref, GATE_HI)):
            gu = _dot(h, wgu_ref[0])
            act = _silu(gu[:, :D_EXPERT]) * gu[:, D_EXPERT:]
            part = _dot((act * gates[:, lane:lane + 1]).astype(BF16), wd_ref[0])
            y = part if y is None else y + part
        ya_ref[...], yb_ref[...] = _pack_row(y)


def _final_norm_kernel(x_ref, ya_ref, yb_ref, g_ref, o_ref):
    o_ref[...] = _rmsnorm(x_ref[...] + _unpack_row(ya_ref[...], yb_ref[...]), g_ref[...])


def _moe_dense_kernel(x_ref, ng_ref, wr_cat_ref, wr_hi_ref, br_ref, wgu_ref, wd_ref, nf_ref,
                      after_ref, y_ref, h_scr, gate_scr, acc_scr, *, final_norm):
    del after_ref
    e = pl.program_id(1)

    @pl.when(e == 0)
    def _():
        hn = _rmsnorm(x_ref[...], ng_ref[...])
        h_scr[...] = hn.astype(BF16)
        lg = _router_logits(hn, wr_cat_ref, wr_hi_ref, br_ref)
        lane = lax.broadcasted_iota(I32, lg.shape, 1)
        neg = jnp.float32(-jnp.inf)
        group_lg = jnp.where(lane < N_EXPERT_GROUPS, lg, neg)
        is_expert = (lane >= EXPERT_LANE0) & (lane < EXPERT_LANE0 + N_EXPERTS)
        _, i1, i2, w1, w2 = _select(group_lg, jnp.where(is_expert, lg, neg), lane,
                                    jnp.where(is_expert, lane - EXPERT_LANE0, N_EXPERTS), 1)
        gate_scr[...] = (jnp.where(lane == i1 + EXPERT_LANE0, w1, 0.0)
                         + jnp.where(lane == i2 + EXPERT_LANE0, w2, 0.0))
        acc_scr[...] = jnp.zeros_like(acc_scr)

    lane = lax.broadcasted_iota(I32, gate_scr.shape, 1)
    gate = jnp.sum(jnp.where(lane == EXPERT_LANE0 + e, gate_scr[...], 0.0), axis=-1, keepdims=True)
    gu = _dot(h_scr[...], wgu_ref[0])
    act = _silu(gu[:, :D_EXPERT]) * gu[:, D_EXPERT:]
    acc_scr[...] += _dot((act * gate).astype(BF16), wd_ref[0])

    @pl.when(e == N_EXPERTS - 1)
    def _():
        y = x_ref[...] + acc_scr[...]
        if final_norm:
            y = _rmsnorm(y, nf_ref[...])
        y_ref[...] = y


def _const_spec(shape):
    return pl.BlockSpec(shape, lambda *_: (0,) * len(shape))


def _resident_spec(shape):
    return pl.BlockSpec(shape, lambda *_: (0,) * len(shape), pipeline_mode=pl.Buffered(1))


_ANY_SPEC = pl.BlockSpec(memory_space=pl.ANY)


def _rope_tables(pos):
    half = RET_HEAD_DIM // 2
    inv = np.float32(ROPE_BASE) ** (-np.arange(half, dtype=np.float32) / np.float32(half))
    ang = pos.astype(np.float32)[:, None] * inv[None, :]
    cos, sin = np.cos(ang), np.sin(ang)
    return np.concatenate([cos, cos], -1), np.concatenate([-sin, sin], -1)


def _decay_consts(c, reps):
    f = np.float32
    log_g = np.log(f(1.0) - f(2.0) ** (f(-5.0) - np.arange(RET_HEADS, dtype=f)))
    i = np.arange(c, dtype=f)
    diff = i[:, None] - i[None, :]
    dmat = np.where(diff[None] >= 0, np.exp(np.maximum(diff, f(0))[None] * log_g[:, None, None]), f(0))
    qdec = np.exp((i + f(1))[None, :] * log_g[:, None])
    kdec = np.exp((f(c) - f(1) - i)[None, :] * log_g[:, None])
    cdec = np.exp(f(c) * log_g)
    dmat = np.einsum('ab,hij->haibj', np.eye(reps, dtype=f), dmat).reshape(RET_HEADS, reps * c, reps * c)
    lanes = lambda t: np.repeat(np.tile(t, (1, reps)).T, RET_HEAD_DIM, axis=1)
    cdec = np.broadcast_to(cdec[:, None, None], (RET_HEADS, 1, RET_HEAD_DIM))
    return tuple(np.ascontiguousarray(a, dtype=f) for a in (dmat, lanes(qdec), lanes(kdec), cdec))


def _split_weight(w):
    hi = lax.bitcast_convert_type(_bf16_round_bits(lax.bitcast_convert_type(w, jnp.uint32)), F32)
    return hi.astype(BF16), (w - hi).astype(BF16)


_W_IN_BLOCK = (None, D_MODEL, IN_WIDTH)
_W_POOL_BLOCK = (None, len(POOL_WINDOWS), POOL_GROUP_DIM, POOL_GROUP_DIM)
_W_OUT_BLOCK = (None, D_MODEL, D_MODEL)
_DECAY_SPECS = [
    _const_spec((RET_HEADS, CHUNK, CHUNK)), _const_spec((CHUNK, RET_WIDTH)),
    _const_spec((CHUNK, RET_WIDTH)), _const_spec((RET_HEADS, 1, RET_HEAD_DIM)),
]


def _layer_spec(block, layer, resident=False):
    index_map = lambda *_: (layer,) + (0,) * (len(block) - 1)
    if resident:
        return pl.BlockSpec(block, index_map, pipeline_mode=pl.Buffered(1))
    return pl.BlockSpec(block, index_map)


def _mixer_weight_specs(layer, split):
    n = 2 if split else 1
    row = lambda width: _layer_spec((None, 1, width), layer)
    return [row(D_MODEL), *[_layer_spec(_W_IN_BLOCK, layer, True)] * n,
            *[_layer_spec(_W_POOL_BLOCK, layer)] * n, row(POOL_WIDTH), row(RET_WIDTH),
            *[_layer_spec(_W_OUT_BLOCK, layer, True)] * n]


def _chain_out_spec(block, depth, layer, has_prev):
    zeros = (0,) * (len(block) - 1)
    if has_prev:
        return pl.BlockSpec((None,) + block, lambda i, *_: (layer, i) + zeros)
    return pl.BlockSpec((depth,) + block, lambda i, *_: (0, i) + zeros)


def _mixer_prompt(x, moe_y, weights, route_w, prev_out, layer, rows, precise_tail, after):
    b, l, _ = x.shape
    t = b * l
    depth = weights[0].shape[0]
    assert l % rows == 0 and rows % CHUNK == 0
    cos2, sin2 = _rope_tables(np.arange(l))
    decay = _decay_consts(RET_CHUNK, 1)
    steps = l // rows
    tok = lambda i, c: (i, c, 0)
    if moe_y is None:
        ya = yb = jnp.zeros((1, rows, PACK_W), I32)
        y_spec = _const_spec((1, rows, PACK_W))
    else:
        ya, yb = (t.reshape(b, l, PACK_W) for t in moe_y)
        y_spec = pl.BlockSpec((1, rows, PACK_W), tok)
    has_prev = prev_out is not None
    n_in = 23
    flat = lambda i, c: (i * steps + c, 0)
    y, tails, states, *routed = pl.pallas_call(
        functools.partial(_mixer_prompt_kernel, rows=rows, steps=steps, moe_in=moe_y is not None,
                          precise_tail=precise_tail, layer=layer, has_prev=has_prev),
        grid=(b, steps),
        in_specs=[pl.BlockSpec((1, rows, D_MODEL), tok), y_spec, y_spec,
                  pl.BlockSpec((rows, RET_HEAD_DIM), lambda i, c: (c, 0)),
                  pl.BlockSpec((rows, RET_HEAD_DIM), lambda i, c: (c, 0)),
                  *_mixer_weight_specs(layer, True), *_DECAY_SPECS,
                  _layer_spec((None, 1, D_MODEL), layer), *_router_specs(layer), _ANY_SPEC,
                  *([_ANY_SPEC, _ANY_SPEC] if has_prev else [])],
        out_specs=[pl.BlockSpec((1, rows, D_MODEL), tok),
                   _chain_out_spec((1, POOL_BUF, POOL_WIDTH), depth, layer, has_prev),
                   _chain_out_spec((1, RET_HEADS, RET_HEAD_DIM, RET_HEAD_DIM), depth, layer, has_prev),
                   pl.BlockSpec((rows, PACK_W), flat), pl.BlockSpec((rows, PACK_W), flat),
                   pl.BlockSpec((rows, LANES), flat),
                   pl.BlockSpec((SUBLANES, rows), lambda i, c: (0, i * steps + c)),
                   _const_spec((CLASS_ROWS, LANES))],
        out_shape=[jax.ShapeDtypeStruct(x.shape, F32),
                   jax.ShapeDtypeStruct((depth, b, POOL_BUF, POOL_WIDTH), F32),
                   jax.ShapeDtypeStruct((depth, b, RET_HEADS, RET_HEAD_DIM, RET_HEAD_DIM), F32),
                   jax.ShapeDtypeStruct((t, PACK_W), I32), jax.ShapeDtypeStruct((t, PACK_W), I32),
                   jax.ShapeDtypeStruct((t, LANES), F32), jax.ShapeDtypeStruct((SUBLANES, t), I32),
                   jax.ShapeDtypeStruct((CLASS_ROWS, LANES), F32)],
        input_output_aliases={n_in: 1, n_in + 1: 2} if has_prev else {},
        scratch_shapes=[pltpu.VMEM((1, POOL_HIST + rows, POOL_WIDTH), F32),
                        pltpu.VMEM((RET_HEADS, RET_HEAD_DIM, RET_HEAD_DIM), F32),
                        pltpu.VMEM((rows, RET_WIDTH), F32),
                        pltpu.VMEM((rows, IN_WIDTH), F32),
                        pltpu.VMEM((CLASS_ROWS, LANES), F32)],
        compiler_params=pltpu.CompilerParams(
            dimension_semantics=("arbitrary", "arbitrary"), vmem_limit_bytes=VMEM_LIMIT),
        name="mixer_prompt",
    )(x, ya, yb, cos2, sin2, *weights, *decay, *route_w, after, *(prev_out if has_prev else ()))
    return y, (tails, states), tuple(routed)


def _mixer_sample(x, pool_prev, s0, weights, prev_out, layer, after):
    b, l, _ = x.shape
    depth = s0.shape[0]
    ns = SAMPLE_SEQS
    assert ns * l == CHUNK and b % ns == 0
    cos2, sin2 = _rope_tables(PAST_LEN + np.arange(l))
    cos2, sin2 = np.tile(cos2, (ns, 1)), np.tile(sin2, (ns, 1))
    decay = _decay_consts(l, ns)
    seq3 = lambda i: (i, 0, 0)
    state_block = (ns, RET_HEADS, RET_HEAD_DIM, RET_HEAD_DIM)
    tail_block = (POOL_BUF, ns, POOL_WIDTH)
    if prev_out is None:
        tail_out = pl.BlockSpec((depth,) + tail_block, lambda i: (0, 0, i, 0))
    else:
        tail_out = pl.BlockSpec((None,) + tail_block, lambda i: (layer, 0, i, 0))
    has_prev = prev_out is not None
    n_in = 16
    y, tails, states = pl.pallas_call(
        functools.partial(_mixer_sample_kernel, seq_len=l, layer=layer, has_prev=has_prev),
        grid=(b // ns,),
        in_specs=[pl.BlockSpec((ns, l, D_MODEL), seq3),
                  pl.BlockSpec((None,) + tail_block, lambda i: (layer, 0, i, 0)),
                  pl.BlockSpec((None,) + state_block, lambda i: (layer, i, 0, 0, 0)),
                  _const_spec((CHUNK, RET_HEAD_DIM)), _const_spec((CHUNK, RET_HEAD_DIM)),
                  *_mixer_weight_specs(layer, False), *_DECAY_SPECS, _ANY_SPEC,
                  *([_ANY_SPEC, _ANY_SPEC] if has_prev else [])],
        out_specs=[pl.BlockSpec((ns, l, D_MODEL), seq3), tail_out,
                   _chain_out_spec(state_block, depth, layer, has_prev)],
        out_shape=[jax.ShapeDtypeStruct(x.shape, F32),
                   jax.ShapeDtypeStruct((depth, POOL_BUF, b, POOL_WIDTH), F32),
                   jax.ShapeDtypeStruct(s0.shape, F32)],
        input_output_aliases={n_in: 1, n_in + 1: 2} if has_prev else {},
        scratch_shapes=[pltpu.VMEM((ns, POOL_HIST + l, POOL_WIDTH), F32),
                        pltpu.VMEM((CHUNK, RET_WIDTH), F32)],
        compiler_params=pltpu.CompilerParams(
            dimension_semantics=("arbitrary",), vmem_limit_bytes=VMEM_LIMIT),
        name="mixer_sample",
    )(x, pool_prev, s0, cos2, sin2, *weights, *decay, after, *(prev_out if has_prev else ()))
    return y, (tails, states)


def _router_weights(w_rg, b_rg, w_re, b_re):
    depth = w_rg.shape[0]
    gap = EXPERT_LANE0 - N_EXPERT_GROUPS
    rest = LANES - EXPERT_LANE0 - N_EXPERTS
    wr = jnp.concatenate([w_rg, jnp.zeros((depth, D_MODEL, gap), F32), w_re,
                          jnp.zeros((depth, D_MODEL, rest), F32)], axis=-1)
    br = jnp.concatenate([b_rg, jnp.zeros((depth, gap), F32), b_re, jnp.zeros((depth, rest), F32)],
                         axis=-1).reshape(depth, 1, LANES)
    wr_hi, wr_lo = _split_weight(wr)
    return jnp.concatenate([wr_hi, wr_lo], axis=-1), wr_hi, br


def _router_specs(layer):
    return [_layer_spec((None, D_MODEL, 2 * LANES), layer), _layer_spec((None, D_MODEL, LANES), layer),
            _layer_spec((None, 1, LANES), layer)]


def _sc_mesh():
    return plsc.VectorSubcoreMesh(core_axis_name="core", subcore_axis_name="subcore",
                                  num_cores=SC_CORES, num_subcores=SC_SUBCORES)


def _sc_params():
    params = pltpu.CompilerParams()
    if "needs_layout_passes" in pltpu.CompilerParams.__dataclass_fields__:
        params = dataclasses.replace(params, needs_layout_passes=False)
    return params


def _sc_gather(tables, idx, after):
    n = idx.shape[0]
    assert n % SC_WINDOW == 0
    nt = len(tables)

    def body(*refs):
        i_hbm = refs[nt]
        for t_hbm, o_hbm in zip(refs[:nt], refs[nt + 2:]):
            def gather_window(i_vmem, o_vmem, t_hbm=t_hbm):
                pltpu.sync_copy(t_hbm.at[i_vmem.at[0]], o_vmem)

            pltpu.emit_pipeline(
                gather_window, grid=(n // SC_WINDOW,),
                in_specs=[pl.BlockSpec((1, SC_WINDOW), lambda i: (0, i))],
                out_specs=[pl.BlockSpec((SC_WINDOW, t_hbm.shape[1]), lambda i: (i, 0))],
                core_axis_name=("core", "subcore"),
                dimension_semantics=(pltpu.PARALLEL,),
            )(i_hbm, o_hbm)

    out_type = tuple(jax.ShapeDtypeStruct((n, t.shape[1]), t.dtype) for t in tables)
    return pl.kernel(body, out_type=out_type, mesh=_sc_mesh(), name="sc_gather")(
        *tables, idx.reshape(1, n), after)


def _sc_slots(cls, rank, starts, n_slots):
    t = cls.shape[0]
    workers = SC_CORES * SC_SUBCORES
    slot_per, tok_per = n_slots // workers, t // workers
    assert n_slots % (workers * SC_LANES) == 0 and t % (workers * SC_LANES) == 0 and t & (t - 1) == 0
    assert n_slots % (SC_LANES * SC_UNROLL) == 0 and t % (SC_LANES * SC_UNROLL) == 0

    def body(cls_hbm, rank_hbm, starts_hbm, pos_hbm, slot_hbm, cls_v, rank_v, starts_v, pos_v, slot_v):
        wid = lax.axis_index("subcore") * SC_CORES + lax.axis_index("core")
        pltpu.sync_copy(cls_hbm, cls_v)
        pltpu.sync_copy(rank_hbm, rank_v)
        pltpu.sync_copy(starts_hbm, starts_v)

        lane = lax.iota(I32, SC_LANES)
        span = SC_LANES * SC_UNROLL

        @pl.loop(0, n_slots, step=span)
        def _(i):
            for u in range(SC_UNROLL):
                j = i + u * SC_LANES
                slot_v[pl.ds(j, SC_LANES)] = (lane + j) & (t - 1)

        @pl.loop(0, t, step=span)
        def _(i):
            for u in range(SC_UNROLL):
                j = i + u * SC_LANES
                at = pl.ds(j, SC_LANES)
                pos = plsc.load_gather(starts_v, [cls_v[at]]) + rank_v[at]
                pos_v[at] = pos
                plsc.store_scatter(slot_v, [pos], lane + j)

        tok_off = pl.multiple_of(wid * tok_per, SC_LANES)
        pltpu.sync_copy(pos_v.at[pl.ds(tok_off, tok_per)], pos_hbm.at[pl.ds(tok_off, tok_per)])
        slot_off = pl.multiple_of(wid * slot_per, SC_LANES)
        pltpu.sync_copy(slot_v.at[pl.ds(slot_off, slot_per)], slot_hbm.at[pl.ds(slot_off, slot_per)])

    return pl.kernel(
        body, mesh=_sc_mesh(), compiler_params=_sc_params(), name="sc_slots",
        out_type=(jax.ShapeDtypeStruct((t,), I32), jax.ShapeDtypeStruct((n_slots,), I32)),
        scratch_types=[pltpu.VMEM((t,), I32), pltpu.VMEM((t,), I32), pltpu.VMEM((CLASS_ROWS,), I32),
                       pltpu.VMEM((t,), I32), pltpu.VMEM((n_slots,), I32)],
    )(cls, rank, starts)


def _moe_experts(routed, wgu, wd, after):
    ha, hb, gates, idx, counts = routed
    t = ha.shape[0]
    n_slots = t + N_CLASSES * PAIR_TILE
    n_tiles = n_slots // PAIR_TILE

    cnt = counts[:, 0].astype(I32)
    padded = (cnt + PAIR_TILE - 1) // PAIR_TILE * PAIR_TILE
    ends = jnp.cumsum(padded)
    starts = ends - padded
    tile_cls = jnp.minimum(
        jnp.sum(ends[None, :N_CLASSES] <= (jnp.arange(n_tiles, dtype=I32) * PAIR_TILE)[:, None], axis=1),
        N_CLASSES - 1).astype(I32)
    first = (tile_cls // PAIRS_PER_GROUP) * EXPERTS_PER_GROUP
    tile_ea = first + jnp.asarray(PAIR_LO, I32)[tile_cls % PAIRS_PER_GROUP]
    tile_eb = first + jnp.asarray(PAIR_HI, I32)[tile_cls % PAIRS_PER_GROUP]
    used = (ends[N_CLASSES - 1:N_CLASSES] // PAIR_TILE).astype(I32)

    pos, slot_tok = _sc_slots(idx[0], idx[1], starts, n_slots)
    hsa, hsb, gate_s = _sc_gather((ha, hb, gates), slot_tok, after=wgu)

    row = lambda i, ea, eb, nu: (jnp.minimum(i, nu[0] - 1), 0)
    w_spec = lambda shape, which: pl.BlockSpec(
        (1,) + shape, lambda i, ea, eb, nu: ((ea, eb)[which][i], 0, 0))
    gu_shape, d_shape = (D_MODEL, 2 * D_EXPERT), (D_EXPERT, D_MODEL)
    ysa, ysb = pl.pallas_call(
        _moe_pair_kernel,
        grid_spec=pltpu.PrefetchScalarGridSpec(
            num_scalar_prefetch=3, grid=(n_tiles,),
            in_specs=[pl.BlockSpec((PAIR_TILE, PACK_W), row), pl.BlockSpec((PAIR_TILE, PACK_W), row),
                      pl.BlockSpec((PAIR_TILE, LANES), row),
                      w_spec(gu_shape, 0), w_spec(d_shape, 0), w_spec(gu_shape, 1), w_spec(d_shape, 1),
                      _ANY_SPEC],
            out_specs=[pl.BlockSpec((PAIR_TILE, PACK_W), row), pl.BlockSpec((PAIR_TILE, PACK_W), row)]),
        out_shape=[jax.ShapeDtypeStruct((n_slots, PACK_W), I32),
                   jax.ShapeDtypeStruct((n_slots, PACK_W), I32)],
        compiler_params=pltpu.CompilerParams(
            dimension_semantics=("arbitrary",), vmem_limit_bytes=VMEM_LIMIT),
        name="moe_pair",
    )(tile_ea, tile_eb, used, hsa, hsb, gate_s, wgu, wd, wgu, wd, after)
    return (ysa, ysb), pos


def _expert_cast(w_gate, w_up, w_down, layer, after):
    w_spec = lambda shape: pl.BlockSpec((None, 1) + shape, lambda e: (layer, e, 0, 0))
    out = lambda shape: pl.BlockSpec((1,) + shape, lambda e: (e, 0, 0))
    return pl.pallas_call(
        _expert_cast_kernel,
        grid=(N_EXPERTS,),
        in_specs=[w_spec((D_MODEL, D_EXPERT)), w_spec((D_MODEL, D_EXPERT)), w_spec((D_EXPERT, D_MODEL)),
                  _ANY_SPEC],
        out_specs=[out((D_MODEL, 2 * D_EXPERT)), out((D_EXPERT, D_MODEL))],
        out_shape=[jax.ShapeDtypeStruct((N_EXPERTS, D_MODEL, 2 * D_EXPERT), BF16),
                   jax.ShapeDtypeStruct((N_EXPERTS, D_EXPERT, D_MODEL), BF16)],
        compiler_params=pltpu.CompilerParams(
            dimension_semantics=("arbitrary",), vmem_limit_bytes=VMEM_LIMIT),
        name="expert_cast",
    )(w_gate, w_up, w_down, after)


def _final_norm(x, moe_y, g, rows):
    t = x.shape[0]
    tok = lambda i: (i, 0)
    return pl.pallas_call(
        _final_norm_kernel,
        grid=(t // rows,),
        in_specs=[pl.BlockSpec((rows, D_MODEL), tok), pl.BlockSpec((rows, PACK_W), tok),
                  pl.BlockSpec((rows, PACK_W), tok), _const_spec((1, D_MODEL))],
        out_specs=pl.BlockSpec((rows, D_MODEL), tok),
        out_shape=jax.ShapeDtypeStruct(x.shape, F32),
        compiler_params=pltpu.CompilerParams(
            dimension_semantics=("arbitrary",), vmem_limit_bytes=VMEM_LIMIT),
        name="final_norm",
    )(x, *moe_y, g.reshape(1, D_MODEL))


def _moe_dense(x, norm_g, router, wgu, wd, norm_final, layer, final_norm, rows, after):
    t = x.shape[0]
    assert t % rows == 0
    tok = lambda i, e: (i, 0)
    w_spec = lambda shape: pl.BlockSpec((1,) + shape, lambda i, e: (e, 0, 0))
    return pl.pallas_call(
        functools.partial(_moe_dense_kernel, final_norm=final_norm),
        grid=(t // rows, N_EXPERTS),
        in_specs=[pl.BlockSpec((rows, D_MODEL), tok), _layer_spec((None, 1, D_MODEL), layer),
                  *_router_specs(layer),
                  w_spec((D_MODEL, 2 * D_EXPERT)), w_spec((D_EXPERT, D_MODEL)),
                  _const_spec((1, D_MODEL)), _ANY_SPEC],
        out_specs=pl.BlockSpec((rows, D_MODEL), tok),
        out_shape=jax.ShapeDtypeStruct(x.shape, F32),
        scratch_shapes=[pltpu.VMEM((rows, D_MODEL), BF16),
                        pltpu.VMEM((rows, LANES), F32),
                        pltpu.VMEM((rows, D_MODEL), F32)],
        compiler_params=pltpu.CompilerParams(
            dimension_semantics=("arbitrary", "arbitrary"), vmem_limit_bytes=VMEM_LIMIT),
        name="moe_dense",
    )(x, norm_g, *router, wgu, wd, norm_final.reshape(1, D_MODEL), after)


def kernel(x_prompt, x_sample, cache_pool, state_ret, norm_mix, w_in, w_pool, pool_scale, ret_gn, w_out, norm_ffn, w_router_group, b_router_group, w_router_expert, b_router_expert, w_gate, w_up, w_down, norm_final):
    depth = norm_mix.shape[0]
    row = lambda a: a.reshape(depth, 1, a.shape[-1])
    mix_split = (row(norm_mix), *_split_weight(w_in), *_split_weight(w_pool), row(pool_scale),
                 row(ret_gn), *_split_weight(w_out))
    mix_hi = tuple(mix_split[i] for i in (0, 1, 3, 5, 6, 7))
    router = _router_weights(w_router_group, b_router_group, w_router_expert, b_router_expert)
    norm_ffn = row(norm_ffn)
    pool_prev = jnp.swapaxes(cache_pool, 1, 2)

    yp, ys = x_prompt, x_sample
    moe_p = None
    out_p = out_s = None
    for l in range(depth):
        yp, out_p, routed = _mixer_prompt(
            yp, moe_p, mix_split, (norm_ffn, *router), out_p, l, rows=512,
            precise_tail=PRECISE_TAIL_STEPS if l < depth - 1 else 0, after=ys)
        wgu, wd = _expert_cast(w_gate, w_up, w_down, l, after=routed[-1])
        ys, out_s = _mixer_sample(ys, pool_prev, state_ret, mix_hi, out_s, l, after=wgu)
        sorted_y, pos = _moe_experts(routed, wgu, wd, after=ys)
        ys = _moe_dense(ys.reshape(-1, D_MODEL), norm_ffn, router, wgu, wd, norm_final,
                        l, l == depth - 1, rows=1024, after=sorted_y[0]).reshape(ys.shape)
        moe_p = _sc_gather(sorted_y, pos, after=routed[-1])
    yp = _final_norm(yp.reshape(-1, D_MODEL), moe_p, norm_final, rows=1024).reshape(yp.shape)
    return (yp, ys, *out_p, jnp.swapaxes(out_s[0], 1, 2), out_s[1])
```

```python
import dataclasses
import functools

import jax
import jax.numpy as jnp
import numpy as np
from jax import lax
from jax.experimental import pallas as pl
from jax.experimental.pallas import tpu as pltpu
from jax.experimental.pallas import tpu_sc as plsc

F32 = jnp.float32
BF16 = jnp.bfloat16
I32 = jnp.int32

D_MODEL = 1024
POOL_WIDTH = 512
POOL_WINDOWS = (2, 4, 8, 16)
POOL_GROUP_DIM = 128
POOL_BUF = 15
POOL_HIST = 16
RET_WIDTH = 512
RET_HEADS = 4
RET_HEAD_DIM = 128
RET_CHUNK = 128
ROPE_BASE = 10000.0
IN_WIDTH = POOL_WIDTH + 4 * RET_WIDTH
N_EXPERT_GROUPS = 4
EXPERTS_PER_GROUP = 4
N_EXPERTS = 16
D_EXPERT = 256
RMS_EPS = 1e-6
GN_EPS = 1e-5
PAST_LEN = 16384

LANES = 128
SUBLANES = 8
EXPERT_LANE0 = 8
PAIRS_PER_GROUP = 6
N_CLASSES = N_EXPERT_GROUPS * PAIRS_PER_GROUP
CLASS_ROWS = 32
PAIR_LO = (0, 0, 0, 1, 1, 2)
PAIR_HI = (1, 2, 3, 2, 3, 3)
GATE_LO, GATE_HI = 0, 1
COUNT_BLOCK = 256
PAIR_TILE = 256
PACK_W = D_MODEL // 4
SC_CORES, SC_SUBCORES, SC_LANES = 2, 16, 16
SC_WINDOW = 128
SC_UNROLL = 8
PRECISE_TAIL_STEPS = 1
CHUNK = 128
SAMPLE_SEQS = 16
VMEM_LIMIT = 56 * 1024 * 1024


def _dot(a, b):
    return jnp.dot(a, b, preferred_element_type=F32)


def _dot_nt(a, b):
    return lax.dot_general(a, b, (((1,), (1,)), ((), ())), preferred_element_type=F32)


def _bf16_round_bits(u):
    return (u + jnp.uint32(0x7FFF) + ((u >> 16) & jnp.uint32(1))) & jnp.uint32(0xFFFF0000)


def _split(a):
    hi = pltpu.bitcast(_bf16_round_bits(pltpu.bitcast(a, jnp.uint32)), F32)
    return hi.astype(BF16), (a - hi).astype(BF16)


def _mm(a, b, precise, nt=False):
    dot = _dot_nt if nt else _dot
    if precise:
        b_hi, b_lo = b if isinstance(b, tuple) else _split(b)
        a_hi, a_lo = _split(a)
        return dot(a_hi, b_hi) + dot(a_lo, b_hi) + dot(a_hi, b_lo)
    return dot(a.astype(BF16), b[0] if isinstance(b, tuple) else b.astype(BF16))


def _rmsnorm(x, g):
    ms = jnp.mean(x * x, axis=-1, keepdims=True)
    return x * lax.rsqrt(ms + RMS_EPS) * g


def _pool_mix(ubuf, rows, t_first, n_prev, wpool_refs, pscale, precise=False, row0=0):
    ns = ubuf.shape[0]
    t = t_first + lax.broadcasted_iota(I32, (1, rows, POOL_GROUP_DIM), 1)
    base = POOL_HIST + row0
    outs = []
    for j, w in enumerate(POOL_WINDOWS):
        lanes = slice(j * POOL_GROUP_DIM, (j + 1) * POOL_GROUP_DIM)
        uj = ubuf[:, base:base + rows, lanes]
        acc = uj
        for i in range(1, w):
            acc = acc + ubuf[:, base - i:base - i + rows, lanes]
        cnt = jnp.minimum(w, n_prev + t + 1).astype(F32)
        d = (acc / cnt - uj).reshape(ns * rows, POOL_GROUP_DIM)
        outs.append(_mm(d, tuple(w[j] for w in wpool_refs), precise))
    return jnp.concatenate(outs, axis=-1) * pscale


def _rope(xh, cos2, sin2):
    return xh * cos2 + pltpu.roll(xh, RET_HEAD_DIM // 2, 1) * sin2


def _group_norm(o):
    mu = jnp.mean(o, axis=-1, keepdims=True)
    c = o - mu
    var = jnp.mean(c * c, axis=-1, keepdims=True)
    return c * lax.rsqrt(var + GN_EPS)


def _silu(x):
    return x * (1.0 / (1.0 + jnp.exp(-x)))


def _head(a, h):
    return a[:, h * RET_HEAD_DIM:(h + 1) * RET_HEAD_DIM]


def _qkvg(z):
    p, r = POOL_WIDTH, RET_WIDTH
    return z[:, p:p + r], z[:, p + r:p + 2 * r], z[:, p + 2 * r:p + 3 * r], z[:, p + 3 * r:p + 4 * r]


def _pack_bf16_pair(a, b):
    ua = pltpu.bitcast(a.astype(BF16).astype(F32), jnp.uint32)
    ub = pltpu.bitcast(b.astype(BF16).astype(F32), jnp.uint32)
    return pltpu.bitcast((ua >> 16) | (ub & jnp.uint32(0xFFFF0000)), I32)


def _unpack_bf16_pair(w):
    u = pltpu.bitcast(w, jnp.uint32)
    return pltpu.bitcast(u << 16, F32), pltpu.bitcast(u & jnp.uint32(0xFFFF0000), F32)


def _pack_row(y):
    q = PACK_W
    return _pack_bf16_pair(y[:, 0:q], y[:, q:2 * q]), _pack_bf16_pair(y[:, 2 * q:3 * q], y[:, 3 * q:])


def _unpack_row(wa, wb):
    return jnp.concatenate([*_unpack_bf16_pair(wa), *_unpack_bf16_pair(wb)], axis=-1)


def _zero_other_layers(ref, layer):
    for j in range(ref.shape[0]):
        if j != layer:
            ref[j] = jnp.zeros(ref.shape[1:], ref.dtype)


def _mixer_prompt_kernel(*refs, rows, steps, moe_in, precise_tail, layer, has_prev):
    (x_ref, ya_ref, yb_ref, cos_ref, sin_ref, ng_ref, win_hi_ref, win_lo_ref, wpool_hi_ref,
     wpool_lo_ref, pscale_ref, gn_ref, wout_hi_ref, wout_lo_ref, dmat_ref, qdec_ref, kdec_ref,
     cdec_ref) = refs[:18]
    route_in = refs[18:22]
    y_ref, tail_ref, sfin_ref, *route_out, ubuf, s_scr, ret_scr, z_scr, count_scr = (
        refs[23 + 2 * has_prev:])
    c = pl.program_id(1)

    @pl.when((pl.program_id(0) == 0) & (c == 0))
    def _():
        count_scr[...] = jnp.zeros_like(count_scr)
    if not has_prev:
        _zero_other_layers(tail_ref, layer)
        _zero_other_layers(sfin_ref, layer)
        tail_ref, sfin_ref = tail_ref.at[layer], sfin_ref.at[layer]

    @pl.when(c == 0)
    def _():
        ubuf[:, 0:POOL_HIST, :] = jnp.zeros((1, POOL_HIST, POOL_WIDTH), F32)
        s_scr[...] = jnp.zeros_like(s_scr)

    kv_cols = slice(POOL_WIDTH + RET_WIDTH, POOL_WIDTH + 3 * RET_WIDTH)

    def step(kv_precise, full_from):
        x = x_ref[0]
        if moe_in:
            x = x + _unpack_row(ya_ref[0], yb_ref[0])
        hn = _rmsnorm(x, ng_ref[...])
        hi, lo = _split(hn) if kv_precise else (hn.astype(BF16), None)
        z_scr[...] = _dot(hi, win_hi_ref[...])
        if kv_precise and full_from:
            z_scr[:full_from, kv_cols] += (_dot(lo[:full_from], win_hi_ref[:, kv_cols])
                                           + _dot(hi[:full_from], win_lo_ref[:, kv_cols]))
        if full_from < rows:
            z_scr[full_from:, :] += (_dot(lo[full_from:], win_hi_ref[...])
                                     + _dot(hi[full_from:], win_lo_ref[...]))

        ubuf[0, POOL_HIST:POOL_HIST + rows, :] = z_scr[:, :POOL_WIDTH]
        pool_w = (wpool_hi_ref, wpool_lo_ref)
        pool_parts = []
        if full_from:
            pool_parts.append(_pool_mix(ubuf, full_from, c * rows, 0, pool_w, pscale_ref[...]))
        if full_from < rows:
            pool_parts.append(_pool_mix(ubuf, rows - full_from, c * rows + full_from, 0, pool_w,
                                        pscale_ref[...], precise=True, row0=full_from))
        pool_out = jnp.concatenate(pool_parts, axis=0)
        tail_ref[...] = ubuf[:, rows + POOL_HIST - POOL_BUF:rows + POOL_HIST, :]
        ubuf[:, 0:POOL_HIST, :] = ubuf[:, rows:rows + POOL_HIST, :]

        scale = RET_HEAD_DIM ** -0.5
        for ci in range(rows // CHUNK):
            rs = slice(ci * CHUNK, (ci + 1) * CHUNK)
            full = ci * CHUNK >= full_from
            cos2 = cos_ref[rs, :]
            sin2 = sin_ref[rs, :]
            for h in range(RET_HEADS):
                col = lambda part: slice(POOL_WIDTH + part * RET_WIDTH + h * RET_HEAD_DIM,
                                         POOL_WIDTH + part * RET_WIDTH + (h + 1) * RET_HEAD_DIM)
                q = _rope(z_scr[rs, col(0)], cos2, sin2)
                k = _rope(z_scr[rs, col(1)], cos2, sin2) * scale
                v = z_scr[rs, col(2)]
                s_old = s_scr[h]
                scores = _mm(q, k, full, nt=True) * dmat_ref[h]
                qd = q * _head(qdec_ref[...], h)
                o = _mm(scores, v, full) + _mm(qd, s_old, full)
                kd = k * _head(kdec_ref[...], h)
                s_scr[h] = s_old * cdec_ref[h] + _mm(kd.T, v, kv_precise)
                on = _group_norm(o) * _head(gn_ref[...], h)
                ret_scr[rs, h * RET_HEAD_DIM:(h + 1) * RET_HEAD_DIM] = _silu(z_scr[rs, col(3)]) * on

        mix_in = jnp.concatenate([pool_out, ret_scr[...]], axis=-1)
        y_ref[0] = x + _dot(mix_in.astype(BF16), wout_hi_ref[...])
        if full_from < rows:
            m_hi, m_lo = _split(mix_in[full_from:])
            y_ref[0, full_from:, :] += _dot(m_lo, wout_hi_ref[...]) + _dot(m_hi, wout_lo_ref[...])
        sfin_ref[0] = s_scr[...]
        _route_tile(y_ref[0], full_from, *route_in, *route_out, count_scr)

    if precise_tail:
        pl.when(c < steps - precise_tail)(lambda: step(False, rows))
        if precise_tail > 1:
            pl.when((c >= steps - precise_tail) & (c < steps - 1))(lambda: step(True, rows))
        pl.when(c == steps - 1)(lambda: step(True, rows - CHUNK))
    else:
        step(False, rows)


def _mixer_sample_kernel(*refs, seq_len, layer, has_prev):
    (x_ref, prev_ref, s0_ref, cos_ref, sin_ref, ng_ref, win_ref, wpool_ref, pscale_ref, gn_ref,
     wout_ref, dmat_ref, qdec_ref, kdec_ref, cdec_ref) = refs[:15]
    y_ref, tail_ref, sfin_ref, ubuf, ret_scr = refs[16 + 2 * has_prev:]
    if not has_prev:
        _zero_other_layers(tail_ref, layer)
        _zero_other_layers(sfin_ref, layer)
        tail_ref, sfin_ref = tail_ref.at[layer], sfin_ref.at[layer]
    ns = SAMPLE_SEQS
    rows = ns * seq_len
    x = x_ref[...].reshape(rows, D_MODEL)
    hn = _rmsnorm(x, ng_ref[...]).astype(BF16)
    z = _dot(hn, win_ref[...])
    for j in range(POOL_BUF):
        ubuf[:, POOL_HIST - POOL_BUF + j, :] = prev_ref[j]
    ubuf[:, POOL_HIST:POOL_HIST + seq_len, :] = z[:, :POOL_WIDTH].reshape(ns, seq_len, POOL_WIDTH)
    pool_out = _pool_mix(ubuf, seq_len, 0, POOL_BUF, (wpool_ref,), pscale_ref[...])
    for j in range(POOL_BUF):
        tail_ref[j] = ubuf[:, seq_len + POOL_HIST - POOL_BUF + j, :]

    q_all, k_all, v_all, g_all = _qkvg(z)
    scale = RET_HEAD_DIM ** -0.5
    cos2 = cos_ref[...]
    sin2 = sin_ref[...]
    tok_seq = lax.broadcasted_iota(I32, (RET_HEAD_DIM, rows), 1) // seq_len
    for h in range(RET_HEADS):
        q = _rope(_head(q_all, h), cos2, sin2)
        k = _rope(_head(k_all, h), cos2, sin2) * scale
        vb = _head(v_all, h).astype(BF16)
        s_old = s0_ref[:, h]
        scores = _dot_nt(q.astype(BF16), k.astype(BF16)) * dmat_ref[h]
        qd = (q * _head(qdec_ref[...], h)).astype(BF16).reshape(ns, seq_len, RET_HEAD_DIM)
        o_state = jnp.einsum('bid,bde->bie', qd, s_old.astype(BF16), preferred_element_type=F32)
        o = _dot(scores.astype(BF16), vb) + o_state.reshape(rows, RET_HEAD_DIM)
        kdt = (k * _head(kdec_ref[...], h)).T
        lhs = jnp.concatenate(
            [jnp.where(tok_seq == b, kdt, 0.0).astype(BF16) for b in range(ns)], axis=0)
        upd = _dot(lhs, vb).reshape(ns, RET_HEAD_DIM, RET_HEAD_DIM)
        sfin_ref[:, h] = s_old * cdec_ref[h] + upd
        on = _group_norm(o) * _head(gn_ref[...], h)
        ret_scr[:, h * RET_HEAD_DIM:(h + 1) * RET_HEAD_DIM] = _silu(_head(g_all, h)) * on

    mix_in = jnp.concatenate([pool_out, ret_scr[...]], axis=-1).astype(BF16)
    y_ref[...] = (x + _dot(mix_in, wout_ref[...])).reshape(ns, seq_len, D_MODEL)


def _router_logits(hn, wr_cat_ref, wr_hi_ref, br_ref, precise_from=0):
    parts = []
    if precise_from:
        parts.append(_dot(hn[:precise_from].astype(BF16), wr_hi_ref[...]))
    if precise_from < hn.shape[0]:
        h_hi, h_lo = _split(hn[precise_from:])
        part = _dot(h_hi, wr_cat_ref[...])
        parts.append(part[:, :LANES] + part[:, LANES:] + _dot(h_lo, wr_hi_ref[...]))
    return jnp.concatenate(parts, axis=0) + br_ref[...]


def _select(group_lg, expert_lg, gidx, eidx, axis):
    red = dict(axis=axis, keepdims=True)
    neg = jnp.float32(-jnp.inf)
    gmax = jnp.max(group_lg, **red)
    g_sel = jnp.min(jnp.where(group_lg == gmax, gidx, N_EXPERT_GROUPS), **red)
    p_sel = 1.0 / jnp.sum(jnp.exp(group_lg - gmax), **red)
    emask = (eidx >> 2) == g_sel
    v1 = jnp.max(jnp.where(emask, expert_lg, neg), **red)
    i1 = jnp.min(jnp.where(emask & (expert_lg == v1), eidx, N_EXPERTS), **red)
    emask2 = emask & (eidx != i1)
    v2 = jnp.max(jnp.where(emask2, expert_lg, neg), **red)
    i2 = jnp.min(jnp.where(emask2 & (expert_lg == v2), eidx, N_EXPERTS), **red)
    e2 = jnp.exp(v2 - v1)
    return g_sel, i1, i2, p_sel / (1.0 + e2), p_sel * e2 / (1.0 + e2)


def _route_tile(x, precise_from, ng_ref, wr_cat_ref, wr_hi_ref, br_ref, ha_ref, hb_ref, gate_ref,
                idx_ref, count_ref, carry_scr):
    rows = x.shape[0]
    hn = _rmsnorm(x, ng_ref[...])
    ha_ref[...], hb_ref[...] = _pack_row(hn)
    lgt = _router_logits(hn, wr_cat_ref, wr_hi_ref, br_ref, precise_from).T
    neg = jnp.float32(-jnp.inf)
    gidx = lax.broadcasted_iota(I32, (SUBLANES, rows), 0)
    eidx = lax.broadcasted_iota(I32, (N_EXPERTS, rows), 0)
    group_lg = jnp.where(gidx < N_EXPERT_GROUPS, lgt[0:SUBLANES], neg)
    expert_lg = lgt[EXPERT_LANE0:EXPERT_LANE0 + N_EXPERTS]
    g_sel, i1, i2, w1, w2 = _select(group_lg, expert_lg, gidx, eidx, 0)
    lo = jnp.minimum(i1, i2) - g_sel * EXPERTS_PER_GROUP
    hi = jnp.maximum(i1, i2) - g_sel * EXPERTS_PER_GROUP
    cls = g_sel * PAIRS_PER_GROUP + ((lo * (7 - lo)) >> 1) + (hi - lo - 1)
    w_lo = jnp.where(i1 < i2, w1, w2)
    w_hi = jnp.where(i1 < i2, w2, w1)

    crow = lax.broadcasted_iota(I32, (CLASS_ROWS, rows), 0)
    onehot = jnp.where(crow == cls, 1.0, 0.0)
    n_blk = rows // COUNT_BLOCK
    blocks = [onehot[:, j * COUNT_BLOCK:(j + 1) * COUNT_BLOCK] for j in range(n_blk)]
    r = lax.broadcasted_iota(I32, (COUNT_BLOCK, COUNT_BLOCK), 0)
    c = lax.broadcasted_iota(I32, (COUNT_BLOCK, COUNT_BLOCK), 1)
    upper = jnp.where(r < c, 1.0, 0.0).astype(BF16)
    within = _dot(jnp.concatenate(blocks, axis=0).astype(BF16), upper)
    carry = carry_scr[:, 0:1]
    ranks = []
    for j in range(n_blk):
        before = within[j * CLASS_ROWS:(j + 1) * CLASS_ROWS] + carry
        ranks.append(jnp.sum(blocks[j] * before, axis=0, keepdims=True))
        carry = carry + jnp.sum(blocks[j], axis=1, keepdims=True)
    rank = jnp.concatenate(ranks, axis=1)
    carry_scr[...] = jnp.broadcast_to(carry, carry_scr.shape)
    count_ref[...] = carry_scr[...]

    row8 = lax.broadcasted_iota(I32, (SUBLANES, rows), 0)
    idx_ref[...] = jnp.where(row8 == 0, cls, jnp.where(row8 == 1, rank.astype(I32), 0))
    rowl = lax.broadcasted_iota(I32, (LANES, rows), 0)
    gate_ref[...] = jnp.where(rowl == GATE_LO, w_lo, jnp.where(rowl == GATE_HI, w_hi, 0.0)).T


def _expert_cast_kernel(wg_ref, wu_ref, wd_ref, after_ref, wgu_ref, wdb_ref):
    del after_ref
    wgu_ref[0, :, :D_EXPERT] = wg_ref[0].astype(BF16)
    wgu_ref[0, :, D_EXPERT:] = wu_ref[0].astype(BF16)
    wdb_ref[0] = wd_ref[0].astype(BF16)


def _moe_pair_kernel(ea_ref, eb_ref, used_ref, ha_ref, hb_ref, gate_ref, wgu_a_ref, wd_a_ref,
                     wgu_b_ref, wd_b_ref, after_ref, ya_ref, yb_ref):
    del ea_ref, eb_ref, after_ref

    @pl.when(pl.program_id(0) < used_ref[0])
    def _():
        h = _unpack_row(ha_ref[...], hb_ref[...]).astype(BF16)
        gates = gate_ref[...]
        y = None
        for wgu_ref, wd_ref, lane in ((wgu_a_ref, wd_a_ref, GATE_LO), (wgu_b_ref, wd_b_ref, GATE_HI)):
            gu = _dot(h, wgu_ref[0])
            act = _silu(gu[:, :D_EXPERT]) * gu[:, D_EXPERT:]
            part = _dot((act * gates[:, lane:lane + 1]).astype(BF16), wd_ref[0])
            y = part if y is None else y + part
        ya_ref[...], yb_ref[...] = _pack_row(y)


def _final_norm_kernel(x_ref, ya_ref, yb_ref, g_ref, o_ref):
    o_ref[...] = _rmsnorm(x_ref[...] + _unpack_row(ya_ref[...], yb_ref[...]), g_ref[...])


def _moe_dense_kernel(x_ref, ng_ref, wr_cat_ref, wr_hi_ref, br_ref, wgu_ref, wd_ref, nf_ref,
                      after_ref, y_ref, h_scr, gate_scr, acc_scr, *, final_norm):
    del after_ref
    e = pl.program_id(1)

    @pl.when(e == 0)
    def _():
        hn = _rmsnorm(x_ref[...], ng_ref[...])
        h_scr[...] = hn.astype(BF16)
        lg = _router_logits(hn, wr_cat_ref, wr_hi_ref, br_ref)
        lane = lax.broadcasted_iota(I32, lg.shape, 1)
        neg = jnp.float32(-jnp.inf)
        group_lg = jnp.where(lane < N_EXPERT_GROUPS, lg, neg)
        is_expert = (lane >= EXPERT_LANE0) & (lane < EXPERT_LANE0 + N_EXPERTS)
        _, i1, i2, w1, w2 = _select(group_lg, jnp.where(is_expert, lg, neg), lane,
                                    jnp.where(is_expert, lane - EXPERT_LANE0, N_EXPERTS), 1)
        gate_scr[...] = (jnp.where(lane == i1 + EXPERT_LANE0, w1, 0.0)
                         + jnp.where(lane == i2 + EXPERT_LANE0, w2, 0.0))
        acc_scr[...] = jnp.zeros_like(acc_scr)

    lane = lax.broadcasted_iota(I32, gate_scr.shape, 1)
    gate = jnp.sum(jnp.where(lane == EXPERT_LANE0 + e, gate_scr[...], 0.0), axis=-1, keepdims=True)
    gu = _dot(h_scr[...], wgu_ref[0])
    act = _silu(gu[:, :D_EXPERT]) * gu[:, D_EXPERT:]
    acc_scr[...] += _dot((act * gate).astype(BF16), wd_ref[0])

    @pl.when(e == N_EXPERTS - 1)
    def _():
        y = x_ref[...] + acc_scr[...]
        if final_norm:
            y = _rmsnorm(y, nf_ref[...])
        y_ref[...] = y


def _const_spec(shape):
    return pl.BlockSpec(shape, lambda *_: (0,) * len(shape))


def _resident_spec(shape):
    return pl.BlockSpec(shape, lambda *_: (0,) * len(shape), pipeline_mode=pl.Buffered(1))


_ANY_SPEC = pl.BlockSpec(memory_space=pl.ANY)


def _rope_tables(pos):
    half = RET_HEAD_DIM // 2
    inv = np.float32(ROPE_BASE) ** (-np.arange(half, dtype=np.float32) / np.float32(half))
    ang = pos.astype(np.float32)[:, None] * inv[None, :]
    cos, sin = np.cos(ang), np.sin(ang)
    return np.concatenate([cos, cos], -1), np.concatenate([-sin, sin], -1)


def _decay_consts(c, reps):
    f = np.float32
    log_g = np.log(f(1.0) - f(2.0) ** (f(-5.0) - np.arange(RET_HEADS, dtype=f)))
    i = np.arange(c, dtype=f)
    diff = i[:, None] - i[None, :]
    dmat = np.where(diff[None] >= 0, np.exp(np.maximum(diff, f(0))[None] * log_g[:, None, None]), f(0))
    qdec = np.exp((i + f(1))[None, :] * log_g[:, None])
    kdec = np.exp((f(c) - f(1) - i)[None, :] * log_g[:, None])
    cdec = np.exp(f(c) * log_g)
    dmat = np.einsum('ab,hij->haibj', np.eye(reps, dtype=f), dmat).reshape(RET_HEADS, reps * c, reps * c)
    lanes = lambda t: np.repeat(np.tile(t, (1, reps)).T, RET_HEAD_DIM, axis=1)
    cdec = np.broadcast_to(cdec[:, None, None], (RET_HEADS, 1, RET_HEAD_DIM))
    return tuple(np.ascontiguousarray(a, dtype=f) for a in (dmat, lanes(qdec), lanes(kdec), cdec))


def _split_weight(w):
    hi = lax.bitcast_convert_type(_bf16_round_bits(lax.bitcast_convert_type(w, jnp.uint32)), F32)
    return hi.astype(BF16), (w - hi).astype(BF16)


_W_IN_BLOCK = (None, D_MODEL, IN_WIDTH)
_W_POOL_BLOCK = (None, len(POOL_WINDOWS), POOL_GROUP_DIM, POOL_GROUP_DIM)
_W_OUT_BLOCK = (None, D_MODEL, D_MODEL)
_DECAY_SPECS = [
    _const_spec((RET_HEADS, CHUNK, CHUNK)), _const_spec((CHUNK, RET_WIDTH)),
    _const_spec((CHUNK, RET_WIDTH)), _const_spec((RET_HEADS, 1, RET_HEAD_DIM)),
]


def _layer_spec(block, layer, resident=False):
    index_map = lambda *_: (layer,) + (0,) * (len(block) - 1)
    if resident:
        return pl.BlockSpec(block, index_map, pipeline_mode=pl.Buffered(1))
    return pl.BlockSpec(block, index_map)


def _mixer_weight_specs(layer, split):
    n = 2 if split else 1
    row = lambda width: _layer_spec((None, 1, width), layer)
    return [row(D_MODEL), *[_layer_spec(_W_IN_BLOCK, layer, True)] * n,
            *[_layer_spec(_W_POOL_BLOCK, layer)] * n, row(POOL_WIDTH), row(RET_WIDTH),
            *[_layer_spec(_W_OUT_BLOCK, layer, True)] * n]


def _chain_out_spec(block, depth, layer, has_prev):
    zeros = (0,) * (len(block) - 1)
    if has_prev:
        return pl.BlockSpec((None,) + block, lambda i, *_: (layer, i) + zeros)
    return pl.BlockSpec((depth,) + block, lambda i, *_: (0, i) + zeros)


def _mixer_prompt(x, moe_y, weights, route_w, prev_out, layer, rows, precise_tail, after):
    b, l, _ = x.shape
    t = b * l
    depth = weights[0].shape[0]
    assert l % rows == 0 and rows % CHUNK == 0
    cos2, sin2 = _rope_tables(np.arange(l))
    decay = _decay_consts(RET_CHUNK, 1)
    steps = l // rows
    tok = lambda i, c: (i, c, 0)
    if moe_y is None:
        ya = yb = jnp.zeros((1, rows, PACK_W), I32)
        y_spec = _const_spec((1, rows, PACK_W))
    else:
        ya, yb = (t.reshape(b, l, PACK_W) for t in moe_y)
        y_spec = pl.BlockSpec((1, rows, PACK_W), tok)
    has_prev = prev_out is not None
    n_in = 23
    flat = lambda i, c: (i * steps + c, 0)
    y, tails, states, *routed = pl.pallas_call(
        functools.partial(_mixer_prompt_kernel, rows=rows, steps=steps, moe_in=moe_y is not None,
                          precise_tail=precise_tail, layer=layer, has_prev=has_prev),
        grid=(b, steps),
        in_specs=[pl.BlockSpec((1, rows, D_MODEL), tok), y_spec, y_spec,
                  pl.BlockSpec((rows, RET_HEAD_DIM), lambda i, c: (c, 0)),
                  pl.BlockSpec((rows, RET_HEAD_DIM), lambda i, c: (c, 0)),
                  *_mixer_weight_specs(layer, True), *_DECAY_SPECS,
                  _layer_spec((None, 1, D_MODEL), layer), *_router_specs(layer), _ANY_SPEC,
                  *([_ANY_SPEC, _ANY_SPEC] if has_prev else [])],
        out_specs=[pl.BlockSpec((1, rows, D_MODEL), tok),
                   _chain_out_spec((1, POOL_BUF, POOL_WIDTH), depth, layer, has_prev),
                   _chain_out_spec((1, RET_HEADS, RET_HEAD_DIM, RET_HEAD_DIM), depth, layer, has_prev),
                   pl.BlockSpec((rows, PACK_W), flat), pl.BlockSpec((rows, PACK_W), flat),
                   pl.BlockSpec((rows, LANES), flat),
                   pl.BlockSpec((SUBLANES, rows), lambda i, c: (0, i * steps + c)),
                   _const_spec((CLASS_ROWS, LANES))],
        out_shape=[jax.ShapeDtypeStruct(x.shape, F32),
                   jax.ShapeDtypeStruct((depth, b, POOL_BUF, POOL_WIDTH), F32),
                   jax.ShapeDtypeStruct((depth, b, RET_HEADS, RET_HEAD_DIM, RET_HEAD_DIM), F32),
                   jax.ShapeDtypeStruct((t, PACK_W), I32), jax.ShapeDtypeStruct((t, PACK_W), I32),
                   jax.ShapeDtypeStruct((t, LANES), F32), jax.ShapeDtypeStruct((SUBLANES, t), I32),
                   jax.ShapeDtypeStruct((CLASS_ROWS, LANES), F32)],
        input_output_aliases={n_in: 1, n_in + 1: 2} if has_prev else {},
        scratch_shapes=[pltpu.VMEM((1, POOL_HIST + rows, POOL_WIDTH), F32),
                        pltpu.VMEM((RET_HEADS, RET_HEAD_DIM, RET_HEAD_DIM), F32),
                        pltpu.VMEM((rows, RET_WIDTH), F32),
                        pltpu.VMEM((rows, IN_WIDTH), F32),
                        pltpu.VMEM((CLASS_ROWS, LANES), F32)],
        compiler_params=pltpu.CompilerParams(
            dimension_semantics=("arbitrary", "arbitrary"), vmem_limit_bytes=VMEM_LIMIT),
        name="mixer_prompt",
    )(x, ya, yb, cos2, sin2, *weights, *decay, *route_w, after, *(prev_out if has_prev else ()))
    return y, (tails, states), tuple(routed)


def _mixer_sample(x, pool_prev, s0, weights, prev_out, layer, after):
    b, l, _ = x.shape
    depth = s0.shape[0]
    ns = SAMPLE_SEQS
    assert ns * l == CHUNK and b % ns == 0
    cos2, sin2 = _rope_tables(PAST_LEN + np.arange(l))
    cos2, sin2 = np.tile(cos2, (ns, 1)), np.tile(sin2, (ns, 1))
    decay = _decay_consts(l, ns)
    seq3 = lambda i: (i, 0, 0)
    state_block = (ns, RET_HEADS, RET_HEAD_DIM, RET_HEAD_DIM)
    tail_block = (POOL_BUF, ns, POOL_WIDTH)
    if prev_out is None:
        tail_out = pl.BlockSpec((depth,) + tail_block, lambda i: (0, 0, i, 0))
    else:
        tail_out = pl.BlockSpec((None,) + tail_block, lambda i: (layer, 0, i, 0))
    has_prev = prev_out is not None
    n_in = 16
    y, tails, states = pl.pallas_call(
        functools.partial(_mixer_sample_kernel, seq_len=l, layer=layer, has_prev=has_prev),
        grid=(b // ns,),
        in_specs=[pl.BlockSpec((ns, l, D_MODEL), seq3),
                  pl.BlockSpec((None,) + tail_block, lambda i: (layer, 0, i, 0)),
                  pl.BlockSpec((None,) + state_block, lambda i: (layer, i, 0, 0, 0)),
                  _const_spec((CHUNK, RET_HEAD_DIM)), _const_spec((CHUNK, RET_HEAD_DIM)),
                  *_mixer_weight_specs(layer, False), *_DECAY_SPECS, _ANY_SPEC,
                  *([_ANY_SPEC, _ANY_SPEC] if has_prev else [])],
        out_specs=[pl.BlockSpec((ns, l, D_MODEL), seq3), tail_out,
                   _chain_out_spec(state_block, depth, layer, has_prev)],
        out_shape=[jax.ShapeDtypeStruct(x.shape, F32),
                   jax.ShapeDtypeStruct((depth, POOL_BUF, b, POOL_WIDTH), F32),
                   jax.ShapeDtypeStruct(s0.shape, F32)],
        input_output_aliases={n_in: 1, n_in + 1: 2} if has_prev else {},
        scratch_shapes=[pltpu.VMEM((ns, POOL_HIST + l, POOL_WIDTH), F32),
                        pltpu.VMEM((CHUNK, RET_WIDTH), F32)],
        compiler_params=pltpu.CompilerParams(
            dimension_semantics=("arbitrary",), vmem_limit_bytes=VMEM_LIMIT),
        name="mixer_sample",
    )(x, pool_prev, s0, cos2, sin2, *weights, *decay, after, *(prev_out if has_prev else ()))
    return y, (tails, states)


def _router_weights(w_rg, b_rg, w_re, b_re):
    depth = w_rg.shape[0]
    gap = EXPERT_LANE0 - N_EXPERT_GROUPS
    rest = LANES - EXPERT_LANE0 - N_EXPERTS
    wr = jnp.concatenate([w_rg, jnp.zeros((depth, D_MODEL, gap), F32), w_re,
                          jnp.zeros((depth, D_MODEL, rest), F32)], axis=-1)
    br = jnp.concatenate([b_rg, jnp.zeros((depth, gap), F32), b_re, jnp.zeros((depth, rest), F32)],
                         axis=-1).reshape(depth, 1, LANES)
    wr_hi, wr_lo = _split_weight(wr)
    return jnp.concatenate([wr_hi, wr_lo], axis=-1), wr_hi, br


def _router_specs(layer):
    return [_layer_spec((None, D_MODEL, 2 * LANES), layer), _layer_spec((None, D_MODEL, LANES), layer),
            _layer_spec((None, 1, LANES), layer)]


def _sc_mesh():
    return plsc.VectorSubcoreMesh(core_axis_name="core", subcore_axis_name="subcore",
                                  num_cores=SC_CORES, num_subcores=SC_SUBCORES)


def _sc_params():
    params = pltpu.CompilerParams()
    if "needs_layout_passes" in pltpu.CompilerParams.__dataclass_fields__:
        params = dataclasses.replace(params, needs_layout_passes=False)
    return params


def _sc_gather(tables, idx, after):
    n = idx.shape[0]
    assert n % SC_WINDOW == 0
    nt = len(tables)

    def body(*refs):
        i_hbm = refs[nt]
        for t_hbm, o_hbm in zip(refs[:nt], refs[nt + 2:]):
            def gather_window(i_vmem, o_vmem, t_hbm=t_hbm):
                pltpu.sync_copy(t_hbm.at[i_vmem.at[0]], o_vmem)

            pltpu.emit_pipeline(
                gather_window, grid=(n // SC_WINDOW,),
                in_specs=[pl.BlockSpec((1, SC_WINDOW), lambda i: (0, i))],
                out_specs=[pl.BlockSpec((SC_WINDOW, t_hbm.shape[1]), lambda i: (i, 0))],
                core_axis_name=("core", "subcore"),
                dimension_semantics=(pltpu.PARALLEL,),
            )(i_hbm, o_hbm)

    out_type = tuple(jax.ShapeDtypeStruct((n, t.shape[1]), t.dtype) for t in tables)
    return pl.kernel(body, out_type=out_type, mesh=_sc_mesh(), name="sc_gather")(
        *tables, idx.reshape(1, n), after)


def _sc_slots(cls, rank, starts, n_slots):
    t = cls.shape[0]
    workers = SC_CORES * SC_SUBCORES
    slot_per, tok_per = n_slots // workers, t // workers
    assert n_slots % (workers * SC_LANES) == 0 and t % (workers * SC_LANES) == 0 and t & (t - 1) == 0
    assert n_slots % (SC_LANES * SC_UNROLL) == 0 and t % (SC_LANES * SC_UNROLL) == 0

    def body(cls_hbm, rank_hbm, starts_hbm, pos_hbm, slot_hbm, cls_v, rank_v, starts_v, pos_v, slot_v):
        wid = lax.axis_index("subcore") * SC_CORES + lax.axis_index("core")
        pltpu.sync_copy(cls_hbm, cls_v)
        pltpu.sync_copy(rank_hbm, rank_v)
        pltpu.sync_copy(starts_hbm, starts_v)

        lane = lax.iota(I32, SC_LANES)
        span = SC_LANES * SC_UNROLL

        @pl.loop(0, n_slots, step=span)
        def _(i):
            for u in range(SC_UNROLL):
                j = i + u * SC_LANES
                slot_v[pl.ds(j, SC_LANES)] = (lane + j) & (t - 1)

        @pl.loop(0, t, step=span)
        def _(i):
            for u in range(SC_UNROLL):
                j = i + u * SC_LANES
                at = pl.ds(j, SC_LANES)
                pos = plsc.load_gather(starts_v, [cls_v[at]]) + rank_v[at]
                pos_v[at] = pos
                plsc.store_scatter(slot_v, [pos], lane + j)

        tok_off = pl.multiple_of(wid * tok_per, SC_LANES)
        pltpu.sync_copy(pos_v.at[pl.ds(tok_off, tok_per)], pos_hbm.at[pl.ds(tok_off, tok_per)])
        slot_off = pl.multiple_of(wid * slot_per, SC_LANES)
        pltpu.sync_copy(slot_v.at[pl.ds(slot_off, slot_per)], slot_hbm.at[pl.ds(slot_off, slot_per)])

    return pl.kernel(
        body, mesh=_sc_mesh(), compiler_params=_sc_params(), name="sc_slots",
        out_type=(jax.ShapeDtypeStruct((t,), I32), jax.ShapeDtypeStruct((n_slots,), I32)),
        scratch_types=[pltpu.VMEM((t,), I32), pltpu.VMEM((t,), I32), pltpu.VMEM((CLASS_ROWS,), I32),
                       pltpu.VMEM((t,), I32), pltpu.VMEM((n_slots,), I32)],
    )(cls, rank, starts)


def _moe_experts(routed, wgu, wd, after):
    ha, hb, gates, idx, counts = routed
    t = ha.shape[0]
    n_slots = t + N_CLASSES * PAIR_TILE
    n_tiles = n_slots // PAIR_TILE

    cnt = counts[:, 0].astype(I32)
    padded = (cnt + PAIR_TILE - 1) // PAIR_TILE * PAIR_TILE
    ends = jnp.cumsum(padded)
    starts = ends - padded
    tile_cls = jnp.minimum(
        jnp.sum(ends[None, :N_CLASSES] <= (jnp.arange(n_tiles, dtype=I32) * PAIR_TILE)[:, None], axis=1),
        N_CLASSES - 1).astype(I32)
    first = (tile_cls // PAIRS_PER_GROUP) * EXPERTS_PER_GROUP
    tile_ea = first + jnp.asarray(PAIR_LO, I32)[tile_cls % PAIRS_PER_GROUP]
    tile_eb = first + jnp.asarray(PAIR_HI, I32)[tile_cls % PAIRS_PER_GROUP]
    used = (ends[N_CLASSES - 1:N_CLASSES] // PAIR_TILE).astype(I32)

    pos, slot_tok = _sc_slots(idx[0], idx[1], starts, n_slots)
    hsa, hsb, gate_s = _sc_gather((ha, hb, gates), slot_tok, after=wgu)

    row = lambda i, ea, eb, nu: (jnp.minimum(i, nu[0] - 1), 0)
    w_spec = lambda shape, which: pl.BlockSpec(
        (1,) + shape, lambda i, ea, eb, nu: ((ea, eb)[which][i], 0, 0))
    gu_shape, d_shape = (D_MODEL, 2 * D_EXPERT), (D_EXPERT, D_MODEL)
    ysa, ysb = pl.pallas_call(
        _moe_pair_kernel,
        grid_spec=pltpu.PrefetchScalarGridSpec(
            num_scalar_prefetch=3, grid=(n_tiles,),
            in_specs=[pl.BlockSpec((PAIR_TILE, PACK_W), row), pl.BlockSpec((PAIR_TILE, PACK_W), row),
                      pl.BlockSpec((PAIR_TILE, LANES), row),
                      w_spec(gu_shape, 0), w_spec(d_shape, 0), w_spec(gu_shape, 1), w_spec(d_shape, 1),
                      _ANY_SPEC],
            out_specs=[pl.BlockSpec((PAIR_TILE, PACK_W), row), pl.BlockSpec((PAIR_TILE, PACK_W), row)]),
        out_shape=[jax.ShapeDtypeStruct((n_slots, PACK_W), I32),
                   jax.ShapeDtypeStruct((n_slots, PACK_W), I32)],
        compiler_params=pltpu.CompilerParams(
            dimension_semantics=("arbitrary",), vmem_limit_bytes=VMEM_LIMIT),
        name="moe_pair",
    )(tile_ea, tile_eb, used, hsa, hsb, gate_s, wgu, wd, wgu, wd, after)
    return (ysa, ysb), pos


def _expert_cast(w_gate, w_up, w_down, layer, after):
    w_spec = lambda shape: pl.BlockSpec((None, 1) + shape, lambda e: (layer, e, 0, 0))
    out = lambda shape: pl.BlockSpec((1,) + shape, lambda e: (e, 0, 0))
    return pl.pallas_call(
        _expert_cast_kernel,
        grid=(N_EXPERTS,),
        in_specs=[w_spec((D_MODEL, D_EXPERT)), w_spec((D_MODEL, D_EXPERT)), w_spec((D_EXPERT, D_MODEL)),
                  _ANY_SPEC],
        out_specs=[out((D_MODEL, 2 * D_EXPERT)), out((D_EXPERT, D_MODEL))],
        out_shape=[jax.ShapeDtypeStruct((N_EXPERTS, D_MODEL, 2 * D_EXPERT), BF16),
                   jax.ShapeDtypeStruct((N_EXPERTS, D_EXPERT, D_MODEL), BF16)],
        compiler_params=pltpu.CompilerParams(
            dimension_semantics=("arbitrary",), vmem_limit_bytes=VMEM_LIMIT),
        name="expert_cast",
    )(w_gate, w_up, w_down, after)


def _final_norm(x, moe_y, g, rows):
    t = x.shape[0]
    tok = lambda i: (i, 0)
    return pl.pallas_call(
        _final_norm_kernel,
        grid=(t // rows,),
        in_specs=[pl.BlockSpec((rows, D_MODEL), tok), pl.BlockSpec((rows, PACK_W), tok),
                  pl.BlockSpec((rows, PACK_W), tok), _const_spec((1, D_MODEL))],
        out_specs=pl.BlockSpec((rows, D_MODEL), tok),
        out_shape=jax.ShapeDtypeStruct(x.shape, F32),
        compiler_params=pltpu.CompilerParams(
            dimension_semantics=("arbitrary",), vmem_limit_bytes=VMEM_LIMIT),
        name="final_norm",
    )(x, *moe_y, g.reshape(1, D_MODEL))


def _moe_dense(x, norm_g, router, wgu, wd, norm_final, layer, final_norm, rows, after):
    t = x.shape[0]
    assert t % rows == 0
    tok = lambda i, e: (i, 0)
    w_spec = lambda shape: pl.BlockSpec((1,) + shape, lambda i, e: (e, 0, 0))
    return pl.pallas_call(
        functools.partial(_moe_dense_kernel, final_norm=final_norm),
        grid=(t // rows, N_EXPERTS),
        in_specs=[pl.BlockSpec((rows, D_MODEL), tok), _layer_spec((None, 1, D_MODEL), layer),
                  *_router_specs(layer),
                  w_spec((D_MODEL, 2 * D_EXPERT)), w_spec((D_EXPERT, D_MODEL)),
                  _const_spec((1, D_MODEL)), _ANY_SPEC],
        out_specs=pl.BlockSpec((rows, D_MODEL), tok),
        out_shape=jax.ShapeDtypeStruct(x.shape, F32),
        scratch_shapes=[pltpu.VMEM((rows, D_MODEL), BF16),
                        pltpu.VMEM((rows, LANES), F32),
                        pltpu.VMEM((rows, D_MODEL), F32)],
        compiler_params=pltpu.CompilerParams(
            dimension_semantics=("arbitrary", "arbitrary"), vmem_limit_bytes=VMEM_LIMIT),
        name="moe_dense",
    )(x, norm_g, *router, wgu, wd, norm_final.reshape(1, D_MODEL), after)


def kernel(x_prompt, x_sample, cache_pool, state_ret, norm_mix, w_in, w_pool, pool_scale, ret_gn, w_out, norm_ffn, w_router_group, b_router_group, w_router_expert, b_router_expert, w_gate, w_up, w_down, norm_final):
    depth = norm_mix.shape[0]
    row = lambda a: a.reshape(depth, 1, a.shape[-1])
    mix_split = (row(norm_mix), *_split_weight(w_in), *_split_weight(w_pool), row(pool_scale),
                 row(ret_gn), *_split_weight(w_out))
    mix_hi = tuple(mix_split[i] for i in (0, 1, 3, 5, 6, 7))
    router = _router_weights(w_router_group, b_router_group, w_router_expert, b_router_expert)
    norm_ffn = row(norm_ffn)
    pool_prev = jnp.swapaxes(cache_pool, 1, 2)

    yp, ys = x_prompt, x_sample
    moe_p = None
    out_p = out_s = None
    for l in range(depth):
        yp, out_p, routed = _mixer_prompt(
            yp, moe_p, mix_split, (norm_ffn, *router), out_p, l, rows=512,
            precise_tail=PRECISE_TAIL_STEPS if l < depth - 1 else 0, after=ys)
        wgu, wd = _expert_cast(w_gate, w_up, w_down, l, after=routed[-1])
        ys, out_s = _mixer_sample(ys, pool_prev, state_ret, mix_hi, out_s, l, after=wgu)
        sorted_y, pos = _moe_experts(routed, wgu, wd, after=ys)
        ys = _moe_dense(ys.reshape(-1, D_MODEL), norm_ffn, router, wgu, wd, norm_final,
                        l, l == depth - 1, rows=1024, after=sorted_y[0]).reshape(ys.shape)
        moe_p = _sc_gather(sorted_y, pos, after=routed[-1])
    yp = _final_norm(yp.reshape(-1, D_MODEL), moe_p, norm_final, rows=1024).reshape(yp.shape)
    return (yp, ys, *out_p, jnp.swapaxes(out_s[0], 1, 2), out_s[1])
```

```python
import dataclasses
import functools

import jax
import jax.numpy as jnp
import numpy as np
from jax import lax
from jax.experimental import pallas as pl
from jax.experimental.pallas import tpu as pltpu
from jax.experimental.pallas import tpu_sc as plsc

F32 = jnp.float32
BF16 = jnp.bfloat16
I32 = jnp.int32

D_MODEL = 1024
POOL_WIDTH = 512
POOL_WINDOWS = (2, 4, 8, 16)
POOL_GROUP_DIM = 128
POOL_BUF = 15
POOL_HIST = 16
RET_WIDTH = 512
RET_HEADS = 4
RET_HEAD_DIM = 128
RET_CHUNK = 128
ROPE_BASE = 10000.0
IN_WIDTH = POOL_WIDTH + 4 * RET_WIDTH
N_EXPERT_GROUPS = 4
EXPERTS_PER_GROUP = 4
N_EXPERTS = 16
D_EXPERT = 256
RMS_EPS = 1e-6
GN_EPS = 1e-5
PAST_LEN = 16384

LANES = 128
SUBLANES = 8
EXPERT_LANE0 = 8
PAIRS_PER_GROUP = 6
N_CLASSES = N_EXPERT_GROUPS * PAIRS_PER_GROUP
CLASS_ROWS = 32
PAIR_LO = (0, 0, 0, 1, 1, 2)
PAIR_HI = (1, 2, 3, 2, 3, 3)
GATE_LO, GATE_HI = 0, 1
COUNT_BLOCK = 256
PAIR_TILE = 256
CAST_EXPERTS = 2
PACK_W = D_MODEL // 4
SC_CORES, SC_SUBCORES, SC_LANES = 2, 16, 16
SC_WINDOW = 128
SC_UNROLL = 8
PRECISE_TAIL_STEPS = 1
CHUNK = 128
SAMPLE_SEQS = 16
VMEM_LIMIT = 56 * 1024 * 1024


def _dot(a, b):
    return jnp.dot(a, b, preferred_element_type=F32)


def _dot_nt(a, b):
    return lax.dot_general(a, b, (((1,), (1,)), ((), ())), preferred_element_type=F32)


def _bf16_round_bits(u):
    return (u + jnp.uint32(0x7FFF) + ((u >> 16) & jnp.uint32(1))) & jnp.uint32(0xFFFF0000)


def _split(a):
    hi = pltpu.bitcast(_bf16_round_bits(pltpu.bitcast(a, jnp.uint32)), F32)
    return hi.astype(BF16), (a - hi).astype(BF16)


def _mm(a, b, precise, nt=False):
    dot = _dot_nt if nt else _dot
    if precise:
        b_hi, b_lo = b if isinstance(b, tuple) else _split(b)
        a_hi, a_lo = _split(a)
        return dot(a_hi, b_hi) + dot(a_lo, b_hi) + dot(a_hi, b_lo)
    return dot(a.astype(BF16), b[0] if isinstance(b, tuple) else b.astype(BF16))


def _rmsnorm(x, g):
    ms = jnp.mean(x * x, axis=-1, keepdims=True)
    return x * lax.rsqrt(ms + RMS_EPS) * g


def _pool_mix(ubuf, rows, t_first, n_prev, wpool_refs, pscale, precise=False, row0=0):
    ns = ubuf.shape[0]
    t = t_first + lax.broadcasted_iota(I32, (1, rows, POOL_GROUP_DIM), 1)
    base = POOL_HIST + row0
    outs = []
    for j, w in enumerate(POOL_WINDOWS):
        lanes = slice(j * POOL_GROUP_DIM, (j + 1) * POOL_GROUP_DIM)
        uj = ubuf[:, base:base + rows, lanes]
        acc = uj
        for i in range(1, w):
            acc = acc + ubuf[:, base - i:base - i + rows, lanes]
        cnt = jnp.minimum(w, n_prev + t + 1).astype(F32)
        d = (acc / cnt - uj).reshape(ns * rows, POOL_GROUP_DIM)
        outs.append(_mm(d, tuple(w[j] for w in wpool_refs), precise))
    return jnp.concatenate(outs, axis=-1) * pscale


def _rope(xh, cos2, sin2):
    return xh * cos2 + pltpu.roll(xh, RET_HEAD_DIM // 2, 1) * sin2


def _group_norm(o):
    mu = jnp.mean(o, axis=-1, keepdims=True)
    c = o - mu
    var = jnp.mean(c * c, axis=-1, keepdims=True)
    return c * lax.rsqrt(var + GN_EPS)


def _silu(x):
    return x * (1.0 / (1.0 + jnp.exp(-x)))


def _head(a, h):
    return a[:, h * RET_HEAD_DIM:(h + 1) * RET_HEAD_DIM]


def _qkvg(z):
    p, r = POOL_WIDTH, RET_WIDTH
    return z[:, p:p + r], z[:, p + r:p + 2 * r], z[:, p + 2 * r:p + 3 * r], z[:, p + 3 * r:p + 4 * r]


def _pack_bf16_pair(a, b):
    ua = pltpu.bitcast(a.astype(BF16).astype(F32), jnp.uint32)
    ub = pltpu.bitcast(b.astype(BF16).astype(F32), jnp.uint32)
    return pltpu.bitcast((ua >> 16) | (ub & jnp.uint32(0xFFFF0000)), I32)


def _unpack_bf16_pair(w):
    u = pltpu.bitcast(w, jnp.uint32)
    return pltpu.bitcast(u << 16, F32), pltpu.bitcast(u & jnp.uint32(0xFFFF0000), F32)


def _pack_row(y):
    q = PACK_W
    return _pack_bf16_pair(y[:, 0:q], y[:, q:2 * q]), _pack_bf16_pair(y[:, 2 * q:3 * q], y[:, 3 * q:])


def _unpack_row(wa, wb):
    return jnp.concatenate([*_unpack_bf16_pair(wa), *_unpack_bf16_pair(wb)], axis=-1)


def _zero_other_layers(ref, layer):
    for j in range(ref.shape[0]):
        if j != layer:
            ref[j] = jnp.zeros(ref.shape[1:], ref.dtype)


def _mixer_prompt_kernel(*refs, rows, steps, moe_in, precise_tail, layer, has_prev):
    (x_ref, ya_ref, yb_ref, cos_ref, sin_ref, ng_ref, win_hi_ref, win_lo_ref, wpool_hi_ref,
     wpool_lo_ref, pscale_ref, gn_ref, wout_hi_ref, wout_lo_ref, dmat_ref, qdec_ref, kdec_ref,
     cdec_ref) = refs[:18]
    route_in = refs[18:22]
    y_ref, tail_ref, sfin_ref, *route_out, ubuf, s_scr, ret_scr, z_scr, count_scr = (
        refs[23 + 2 * has_prev:])
    c = pl.program_id(1)

    @pl.when((pl.program_id(0) == 0) & (c == 0))
    def _():
        count_scr[...] = jnp.zeros_like(count_scr)
    if not has_prev:
        _zero_other_layers(tail_ref, layer)
        _zero_other_layers(sfin_ref, layer)
        tail_ref, sfin_ref = tail_ref.at[layer], sfin_ref.at[layer]

    @pl.when(c == 0)
    def _():
        ubuf[:, 0:POOL_HIST, :] = jnp.zeros((1, POOL_HIST, POOL_WIDTH), F32)
        s_scr[...] = jnp.zeros_like(s_scr)

    kv_cols = slice(POOL_WIDTH + RET_WIDTH, POOL_WIDTH + 3 * RET_WIDTH)

    def step(kv_precise, full_from):
        x = x_ref[0]
        if moe_in:
            x = x + _unpack_row(ya_ref[0], yb_ref[0])
        hn = _rmsnorm(x, ng_ref[...])
        hi, lo = _split(hn) if kv_precise else (hn.astype(BF16), None)
        z_scr[...] = _dot(hi, win_hi_ref[...])
        if kv_precise and full_from:
            z_scr[:full_from, kv_cols] += (_dot(lo[:full_from], win_hi_ref[:, kv_cols])
                                           + _dot(hi[:full_from], win_lo_ref[:, kv_cols]))
        if full_from < rows:
            z_scr[full_from:, :] += (_dot(lo[full_from:], win_hi_ref[...])
                                     + _dot(hi[full_from:], win_lo_ref[...]))

        ubuf[0, POOL_HIST:POOL_HIST + rows, :] = z_scr[:, :POOL_WIDTH]
        pool_w = (wpool_hi_ref, wpool_lo_ref)
        pool_parts = []
        if full_from:
            pool_parts.append(_pool_mix(ubuf, full_from, c * rows, 0, pool_w, pscale_ref[...]))
        if full_from < rows:
            pool_parts.append(_pool_mix(ubuf, rows - full_from, c * rows + full_from, 0, pool_w,
                                        pscale_ref[...], precise=True, row0=full_from))
        pool_out = jnp.concatenate(pool_parts, axis=0)
        tail_ref[...] = ubuf[:, rows + POOL_HIST - POOL_BUF:rows + POOL_HIST, :]
        ubuf[:, 0:POOL_HIST, :] = ubuf[:, rows:rows + POOL_HIST, :]

        scale = RET_HEAD_DIM ** -0.5
        for ci in range(rows // CHUNK):
            rs = slice(ci * CHUNK, (ci + 1) * CHUNK)
            full = ci * CHUNK >= full_from
            cos2 = cos_ref[rs, :]
            sin2 = sin_ref[rs, :]
            for h in range(RET_HEADS):
                col = lambda part: slice(POOL_WIDTH + part * RET_WIDTH + h * RET_HEAD_DIM,
                                         POOL_WIDTH + part * RET_WIDTH + (h + 1) * RET_HEAD_DIM)
                q = _rope(z_scr[rs, col(0)], cos2, sin2)
                k = _rope(z_scr[rs, col(1)], cos2, sin2) * scale
                v = z_scr[rs, col(2)]
                s_old = s_scr[h]
                scores = _mm(q, k, full, nt=True) * dmat_ref[h]
                qd = q * _head(qdec_ref[...], h)
                o = _mm(scores, v, full) + _mm(qd, s_old, full)
                kd = k * _head(kdec_ref[...], h)
                s_scr[h] = s_old * cdec_ref[h] + _mm(kd.T, v, kv_precise)
                on = _group_norm(o) * _head(gn_ref[...], h)
                ret_scr[rs, h * RET_HEAD_DIM:(h + 1) * RET_HEAD_DIM] = _silu(z_scr[rs, col(3)]) * on

        mix_in = jnp.concatenate([pool_out, ret_scr[...]], axis=-1)
        y_ref[0] = x + _dot(mix_in.astype(BF16), wout_hi_ref[...])
        if full_from < rows:
            m_hi, m_lo = _split(mix_in[full_from:])
            y_ref[0, full_from:, :] += _dot(m_lo, wout_hi_ref[...]) + _dot(m_hi, wout_lo_ref[...])
        sfin_ref[0] = s_scr[...]
        _route_tile(y_ref[0], full_from, *route_in, *route_out, count_scr)

    if precise_tail:
        pl.when(c < steps - precise_tail)(lambda: step(False, rows))
        if precise_tail > 1:
            pl.when((c >= steps - precise_tail) & (c < steps - 1))(lambda: step(True, rows))
        pl.when(c == steps - 1)(lambda: step(True, rows - CHUNK))
    else:
        step(False, rows)


def _mixer_sample_kernel(*refs, seq_len, layer, has_prev):
    (x_ref, prev_ref, s0_ref, cos_ref, sin_ref, ng_ref, win_ref, wpool_ref, pscale_ref, gn_ref,
     wout_ref, dmat_ref, qdec_ref, kdec_ref, cdec_ref) = refs[:15]
    y_ref, tail_ref, sfin_ref, ubuf, ret_scr = refs[16 + 2 * has_prev:]
    if not has_prev:
        _zero_other_layers(tail_ref, layer)
        _zero_other_layers(sfin_ref, layer)
        tail_ref, sfin_ref = tail_ref.at[layer], sfin_ref.at[layer]
    ns = SAMPLE_SEQS
    rows = ns * seq_len
    x = x_ref[...].reshape(rows, D_MODEL)
    hn = _rmsnorm(x, ng_ref[...]).astype(BF16)
    z = _dot(hn, win_ref[...])
    for j in range(POOL_BUF):
        ubuf[:, POOL_HIST - POOL_BUF + j, :] = prev_ref[j]
    ubuf[:, POOL_HIST:POOL_HIST + seq_len, :] = z[:, :POOL_WIDTH].reshape(ns, seq_len, POOL_WIDTH)
    pool_out = _pool_mix(ubuf, seq_len, 0, POOL_BUF, (wpool_ref,), pscale_ref[...])
    for j in range(POOL_BUF):
        tail_ref[j] = ubuf[:, seq_len + POOL_HIST - POOL_BUF + j, :]

    q_all, k_all, v_all, g_all = _qkvg(z)
    scale = RET_HEAD_DIM ** -0.5
    cos2 = cos_ref[...]
    sin2 = sin_ref[...]
    tok_seq = lax.broadcasted_iota(I32, (RET_HEAD_DIM, rows), 1) // seq_len
    for h in range(RET_HEADS):
        q = _rope(_head(q_all, h), cos2, sin2)
        k = _rope(_head(k_all, h), cos2, sin2) * scale
        vb = _head(v_all, h).astype(BF16)
        s_old = s0_ref[:, h]
        scores = _dot_nt(q.astype(BF16), k.astype(BF16)) * dmat_ref[h]
        qd = (q * _head(qdec_ref[...], h)).astype(BF16).reshape(ns, seq_len, RET_HEAD_DIM)
        o_state = jnp.einsum('bid,bde->bie', qd, s_old.astype(BF16), preferred_element_type=F32)
        o = _dot(scores.astype(BF16), vb) + o_state.reshape(rows, RET_HEAD_DIM)
        kdt = (k * _head(kdec_ref[...], h)).T
        lhs = jnp.concatenate(
            [jnp.where(tok_seq == b, kdt, 0.0).astype(BF16) for b in range(ns)], axis=0)
        upd = _dot(lhs, vb).reshape(ns, RET_HEAD_DIM, RET_HEAD_DIM)
        sfin_ref[:, h] = s_old * cdec_ref[h] + upd
        on = _group_norm(o) * _head(gn_ref[...], h)
        ret_scr[:, h * RET_HEAD_DIM:(h + 1) * RET_HEAD_DIM] = _silu(_head(g_all, h)) * on

    mix_in = jnp.concatenate([pool_out, ret_scr[...]], axis=-1).astype(BF16)
    y_ref[...] = (x + _dot(mix_in, wout_ref[...])).reshape(ns, seq_len, D_MODEL)


def _router_logits(hn, wr_cat_ref, wr_hi_ref, br_ref, precise_from=0):
    parts = []
    if precise_from:
        parts.append(_dot(hn[:precise_from].astype(BF16), wr_hi_ref[...]))
    if precise_from < hn.shape[0]:
        h_hi, h_lo = _split(hn[precise_from:])
        part = _dot(h_hi, wr_cat_ref[...])
        parts.append(part[:, :LANES] + part[:, LANES:] + _dot(h_lo, wr_hi_ref[...]))
    return jnp.concatenate(parts, axis=0) + br_ref[...]


def _select(group_lg, expert_lg, gidx, eidx, axis):
    red = dict(axis=axis, keepdims=True)
    neg = jnp.float32(-jnp.inf)
    gmax = jnp.max(group_lg, **red)
    g_sel = jnp.min(jnp.where(group_lg == gmax, gidx, N_EXPERT_GROUPS), **red)
    p_sel = 1.0 / jnp.sum(jnp.exp(group_lg - gmax), **red)
    emask = (eidx >> 2) == g_sel
    v1 = jnp.max(jnp.where(emask, expert_lg, neg), **red)
    i1 = jnp.min(jnp.where(emask & (expert_lg == v1), eidx, N_EXPERTS), **red)
    emask2 = emask & (eidx != i1)
    v2 = jnp.max(jnp.where(emask2, expert_lg, neg), **red)
    i2 = jnp.min(jnp.where(emask2 & (expert_lg == v2), eidx, N_EXPERTS), **red)
    e2 = jnp.exp(v2 - v1)
    return g_sel, i1, i2, p_sel / (1.0 + e2), p_sel * e2 / (1.0 + e2)


def _route_tile(x, precise_from, ng_ref, wr_cat_ref, wr_hi_ref, br_ref, ha_ref, hb_ref, gate_ref,
                idx_ref, count_ref, carry_scr):
    rows = x.shape[0]
    hn = _rmsnorm(x, ng_ref[...])
    ha_ref[...], hb_ref[...] = _pack_row(hn)
    lgt = _router_logits(hn, wr_cat_ref, wr_hi_ref, br_ref, precise_from).T
    neg = jnp.float32(-jnp.inf)
    gidx = lax.broadcasted_iota(I32, (SUBLANES, rows), 0)
    eidx = lax.broadcasted_iota(I32, (N_EXPERTS, rows), 0)
    group_lg = jnp.where(gidx < N_EXPERT_GROUPS, lgt[0:SUBLANES], neg)
    expert_lg = lgt[EXPERT_LANE0:EXPERT_LANE0 + N_EXPERTS]
    g_sel, i1, i2, w1, w2 = _select(group_lg, expert_lg, gidx, eidx, 0)
    lo = jnp.minimum(i1, i2) - g_sel * EXPERTS_PER_GROUP
    hi = jnp.maximum(i1, i2) - g_sel * EXPERTS_PER_GROUP
    cls = g_sel * PAIRS_PER_GROUP + ((lo * (7 - lo)) >> 1) + (hi - lo - 1)
    w_lo = jnp.where(i1 < i2, w1, w2)
    w_hi = jnp.where(i1 < i2, w2, w1)

    crow = lax.broadcasted_iota(I32, (CLASS_ROWS, rows), 0)
    onehot = jnp.where(crow == cls, 1.0, 0.0)
    n_blk = rows // COUNT_BLOCK
    blocks = [onehot[:, j * COUNT_BLOCK:(j + 1) * COUNT_BLOCK] for j in range(n_blk)]
    r = lax.broadcasted_iota(I32, (COUNT_BLOCK, COUNT_BLOCK), 0)
    c = lax.broadcasted_iota(I32, (COUNT_BLOCK, COUNT_BLOCK), 1)
    upper = jnp.where(r < c, 1.0, 0.0).astype(BF16)
    within = _dot(jnp.concatenate(blocks, axis=0).astype(BF16), upper)
    carry = carry_scr[:, 0:1]
    ranks = []
    for j in range(n_blk):
        before = within[j * CLASS_ROWS:(j + 1) * CLASS_ROWS] + carry
        ranks.append(jnp.sum(blocks[j] * before, axis=0, keepdims=True))
        carry = carry + jnp.sum(blocks[j], axis=1, keepdims=True)
    rank = jnp.concatenate(ranks, axis=1)
    carry_scr[...] = jnp.broadcast_to(carry, carry_scr.shape)
    count_ref[...] = carry_scr[...]

    row8 = lax.broadcasted_iota(I32, (SUBLANES, rows), 0)
    idx_ref[...] = jnp.where(row8 == 0, cls, jnp.where(row8 == 1, rank.astype(I32), 0))
    rowl = lax.broadcasted_iota(I32, (LANES, rows), 0)
    gate_ref[...] = jnp.where(rowl == GATE_LO, w_lo, jnp.where(rowl == GATE_HI, w_hi, 0.0)).T


def _expert_cast_kernel(wg_ref, wu_ref, wd_ref, after_ref, wgu_ref, wdb_ref):
    del after_ref
    wgu_ref[:, :, :D_EXPERT] = wg_ref[...].astype(BF16)
    wgu_ref[:, :, D_EXPERT:] = wu_ref[...].astype(BF16)
    wdb_ref[...] = wd_ref[...].astype(BF16)


def _moe_pair_kernel(ea_ref, eb_ref, used_ref, ha_ref, hb_ref, gate_ref, wgu_a_ref, wd_a_ref,
                     wgu_b_ref, wd_b_ref, after_ref, ya_ref, yb_ref):
    del ea_ref, eb_ref, after_ref

    @pl.when(pl.program_id(0) < used_ref[0])
    def _():
        h = _unpack_row(ha_ref[...], hb_ref[...]).astype(BF16)
        gates = gate_ref[...]
        y = None
        for wgu_ref, wd_ref, lane in ((wgu_a_ref, wd_a_ref, GATE_LO), (wgu_b_ref, wd_b_ref, GATE_HI)):
            gu = _dot(h, wgu_ref[0])
            act = _silu(gu[:, :D_EXPERT]) * gu[:, D_EXPERT:]
            part = _dot((act * gates[:, lane:lane + 1]).astype(BF16), wd_ref[0])
            y = part if y is None else y + part
        ya_ref[...], yb_ref[...] = _pack_row(y)


def _final_norm_kernel(x_ref, ya_ref, yb_ref, g_ref, o_ref):
    o_ref[...] = _rmsnorm(x_ref[...] + _unpack_row(ya_ref[...], yb_ref[...]), g_ref[...])


def _moe_dense_kernel(x_ref, ng_ref, wr_cat_ref, wr_hi_ref, br_ref, wgu_ref, wd_ref, nf_ref,
                      after_ref, y_ref, h_scr, gate_scr, acc_scr, *, final_norm):
    del after_ref
    e = pl.program_id(1)

    @pl.when(e == 0)
    def _():
        hn = _rmsnorm(x_ref[...], ng_ref[...])
        h_scr[...] = hn.astype(BF16)
        lg = _router_logits(hn, wr_cat_ref, wr_hi_ref, br_ref)
        lane = lax.broadcasted_iota(I32, lg.shape, 1)
        neg = jnp.float32(-jnp.inf)
        group_lg = jnp.where(lane < N_EXPERT_GROUPS, lg, neg)
        is_expert = (lane >= EXPERT_LANE0) & (lane < EXPERT_LANE0 + N_EXPERTS)
        _, i1, i2, w1, w2 = _select(group_lg, jnp.where(is_expert, lg, neg), lane,
                                    jnp.where(is_expert, lane - EXPERT_LANE0, N_EXPERTS), 1)
        gate_scr[...] = (jnp.where(lane == i1 + EXPERT_LANE0, w1, 0.0)
                         + jnp.where(lane == i2 + EXPERT_LANE0, w2, 0.0))
        acc_scr[...] = jnp.zeros_like(acc_scr)

    lane = lax.broadcasted_iota(I32, gate_scr.shape, 1)
    gate = jnp.sum(jnp.where(lane == EXPERT_LANE0 + e, gate_scr[...], 0.0), axis=-1, keepdims=True)
    gu = _dot(h_scr[...], wgu_ref[0])
    act = _silu(gu[:, :D_EXPERT]) * gu[:, D_EXPERT:]
    acc_scr[...] += _dot((act * gate).astype(BF16), wd_ref[0])

    @pl.when(e == N_EXPERTS - 1)
    def _():
        y = x_ref[...] + acc_scr[...]
        if final_norm:
            y = _rmsnorm(y, nf_ref[...])
        y_ref[...] = y


def _const_spec(shape):
    return pl.BlockSpec(shape, lambda *_: (0,) * len(shape))


def _resident_spec(shape):
    return pl.BlockSpec(shape, lambda *_: (0,) * len(shape), pipeline_mode=pl.Buffered(1))


_ANY_SPEC = pl.BlockSpec(memory_space=pl.ANY)


def _rope_tables(pos):
    half = RET_HEAD_DIM // 2
    inv = np.float32(ROPE_BASE) ** (-np.arange(half, dtype=np.float32) / np.float32(half))
    ang = pos.astype(np.float32)[:, None] * inv[None, :]
    cos, sin = np.cos(ang), np.sin(ang)
    return np.concatenate([cos, cos], -1), np.concatenate([-sin, sin], -1)


def _decay_consts(c, reps):
    f = np.float32
    log_g = np.log(f(1.0) - f(2.0) ** (f(-5.0) - np.arange(RET_HEADS, dtype=f)))
    i = np.arange(c, dtype=f)
    diff = i[:, None] - i[None, :]
    dmat = np.where(diff[None] >= 0, np.exp(np.maximum(diff, f(0))[None] * log_g[:, None, None]), f(0))
    qdec = np.exp((i + f(1))[None, :] * log_g[:, None])
    kdec = np.exp((f(c) - f(1) - i)[None, :] * log_g[:, None])
    cdec = np.exp(f(c) * log_g)
    dmat = np.einsum('ab,hij->haibj', np.eye(reps, dtype=f), dmat).reshape(RET_HEADS, reps * c, reps * c)
    lanes = lambda t: np.repeat(np.tile(t, (1, reps)).T, RET_HEAD_DIM, axis=1)
    cdec = np.broadcast_to(cdec[:, None, None], (RET_HEADS, 1, RET_HEAD_DIM))
    return tuple(np.ascontiguousarray(a, dtype=f) for a in (dmat, lanes(qdec), lanes(kdec), cdec))


def _split_weight(w):
    hi = lax.bitcast_convert_type(_bf16_round_bits(lax.bitcast_convert_type(w, jnp.uint32)), F32)
    return hi.astype(BF16), (w - hi).astype(BF16)


_W_IN_BLOCK = (None, D_MODEL, IN_WIDTH)
_W_POOL_BLOCK = (None, len(POOL_WINDOWS), POOL_GROUP_DIM, POOL_GROUP_DIM)
_W_OUT_BLOCK = (None, D_MODEL, D_MODEL)
_DECAY_SPECS = [
    _const_spec((RET_HEADS, CHUNK, CHUNK)), _const_spec((CHUNK, RET_WIDTH)),
    _const_spec((CHUNK, RET_WIDTH)), _const_spec((RET_HEADS, 1, RET_HEAD_DIM)),
]


def _layer_spec(block, layer, resident=False):
    index_map = lambda *_: (layer,) + (0,) * (len(block) - 1)
    if resident:
        return pl.BlockSpec(block, index_map, pipeline_mode=pl.Buffered(1))
    return pl.BlockSpec(block, index_map)


def _mixer_weight_specs(layer, split):
    n = 2 if split else 1
    row = lambda width: _layer_spec((None, 1, width), layer)
    return [row(D_MODEL), *[_layer_spec(_W_IN_BLOCK, layer, True)] * n,
            *[_layer_spec(_W_POOL_BLOCK, layer)] * n, row(POOL_WIDTH), row(RET_WIDTH),
            *[_layer_spec(_W_OUT_BLOCK, layer, True)] * n]


def _chain_out_spec(block, depth, layer, has_prev):
    zeros = (0,) * (len(block) - 1)
    if has_prev:
        return pl.BlockSpec((None,) + block, lambda i, *_: (layer, i) + zeros)
    return pl.BlockSpec((depth,) + block, lambda i, *_: (0, i) + zeros)


def _mixer_prompt(x, moe_y, weights, route_w, prev_out, layer, rows, precise_tail, after):
    b, l, _ = x.shape
    t = b * l
    depth = weights[0].shape[0]
    assert l % rows == 0 and rows % CHUNK == 0
    cos2, sin2 = _rope_tables(np.arange(l))
    decay = _decay_consts(RET_CHUNK, 1)
    steps = l // rows
    tok = lambda i, c: (i, c, 0)
    if moe_y is None:
        ya = yb = jnp.zeros((1, rows, PACK_W), I32)
        y_spec = _const_spec((1, rows, PACK_W))
    else:
        ya, yb = (t.reshape(b, l, PACK_W) for t in moe_y)
        y_spec = pl.BlockSpec((1, rows, PACK_W), tok)
    has_prev = prev_out is not None
    n_in = 23
    flat = lambda i, c: (i * steps + c, 0)
    y, tails, states, *routed = pl.pallas_call(
        functools.partial(_mixer_prompt_kernel, rows=rows, steps=steps, moe_in=moe_y is not None,
                          precise_tail=precise_tail, layer=layer, has_prev=has_prev),
        grid=(b, steps),
        in_specs=[pl.BlockSpec((1, rows, D_MODEL), tok), y_spec, y_spec,
                  pl.BlockSpec((rows, RET_HEAD_DIM), lambda i, c: (c, 0)),
                  pl.BlockSpec((rows, RET_HEAD_DIM), lambda i, c: (c, 0)),
                  *_mixer_weight_specs(layer, True), *_DECAY_SPECS,
                  _layer_spec((None, 1, D_MODEL), layer), *_router_specs(layer), _ANY_SPEC,
                  *([_ANY_SPEC, _ANY_SPEC] if has_prev else [])],
        out_specs=[pl.BlockSpec((1, rows, D_MODEL), tok),
                   _chain_out_spec((1, POOL_BUF, POOL_WIDTH), depth, layer, has_prev),
                   _chain_out_spec((1, RET_HEADS, RET_HEAD_DIM, RET_HEAD_DIM), depth, layer, has_prev),
                   pl.BlockSpec((rows, PACK_W), flat), pl.BlockSpec((rows, PACK_W), flat),
                   pl.BlockSpec((rows, LANES), flat),
                   pl.BlockSpec((SUBLANES, rows), lambda i, c: (0, i * steps + c)),
                   _const_spec((CLASS_ROWS, LANES))],
        out_shape=[jax.ShapeDtypeStruct(x.shape, F32),
                   jax.ShapeDtypeStruct((depth, b, POOL_BUF, POOL_WIDTH), F32),
                   jax.ShapeDtypeStruct((depth, b, RET_HEADS, RET_HEAD_DIM, RET_HEAD_DIM), F32),
                   jax.ShapeDtypeStruct((t, PACK_W), I32), jax.ShapeDtypeStruct((t, PACK_W), I32),
                   jax.ShapeDtypeStruct((t, LANES), F32), jax.ShapeDtypeStruct((SUBLANES, t), I32),
                   jax.ShapeDtypeStruct((CLASS_ROWS, LANES), F32)],
        input_output_aliases={n_in: 1, n_in + 1: 2} if has_prev else {},
        scratch_shapes=[pltpu.VMEM((1, POOL_HIST + rows, POOL_WIDTH), F32),
                        pltpu.VMEM((RET_HEADS, RET_HEAD_DIM, RET_HEAD_DIM), F32),
                        pltpu.VMEM((rows, RET_WIDTH), F32),
                        pltpu.VMEM((rows, IN_WIDTH), F32),
                        pltpu.VMEM((CLASS_ROWS, LANES), F32)],
        compiler_params=pltpu.CompilerParams(
            dimension_semantics=("arbitrary", "arbitrary"), vmem_limit_bytes=VMEM_LIMIT),
        name="mixer_prompt",
    )(x, ya, yb, cos2, sin2, *weights, *decay, *route_w, after, *(prev_out if has_prev else ()))
    return y, (tails, states), tuple(routed)


def _mixer_sample(x, pool_prev, s0, weights, prev_out, layer, after):
    b, l, _ = x.shape
    depth = s0.shape[0]
    ns = SAMPLE_SEQS
    assert ns * l == CHUNK and b % ns == 0
    cos2, sin2 = _rope_tables(PAST_LEN + np.arange(l))
    cos2, sin2 = np.tile(cos2, (ns, 1)), np.tile(sin2, (ns, 1))
    decay = _decay_consts(l, ns)
    seq3 = lambda i: (i, 0, 0)
    state_block = (ns, RET_HEADS, RET_HEAD_DIM, RET_HEAD_DIM)
    tail_block = (POOL_BUF, ns, POOL_WIDTH)
    if prev_out is None:
        tail_out = pl.BlockSpec((depth,) + tail_block, lambda i: (0, 0, i, 0))
    else:
        tail_out = pl.BlockSpec((None,) + tail_block, lambda i: (layer, 0, i, 0))
    has_prev = prev_out is not None
    n_in = 16
    y, tails, states = pl.pallas_call(
        functools.partial(_mixer_sample_kernel, seq_len=l, layer=layer, has_prev=has_prev),
        grid=(b // ns,),
        in_specs=[pl.BlockSpec((ns, l, D_MODEL), seq3),
                  pl.BlockSpec((None,) + tail_block, lambda i: (layer, 0, i, 0)),
                  pl.BlockSpec((None,) + state_block, lambda i: (layer, i, 0, 0, 0)),
                  _const_spec((CHUNK, RET_HEAD_DIM)), _const_spec((CHUNK, RET_HEAD_DIM)),
                  *_mixer_weight_specs(layer, False), *_DECAY_SPECS, _ANY_SPEC,
                  *([_ANY_SPEC, _ANY_SPEC] if has_prev else [])],
        out_specs=[pl.BlockSpec((ns, l, D_MODEL), seq3), tail_out,
                   _chain_out_spec(state_block, depth, layer, has_prev)],
        out_shape=[jax.ShapeDtypeStruct(x.shape, F32),
                   jax.ShapeDtypeStruct((depth, POOL_BUF, b, POOL_WIDTH), F32),
                   jax.ShapeDtypeStruct(s0.shape, F32)],
        input_output_aliases={n_in: 1, n_in + 1: 2} if has_prev else {},
        scratch_shapes=[pltpu.VMEM((ns, POOL_HIST + l, POOL_WIDTH), F32),
                        pltpu.VMEM((CHUNK, RET_WIDTH), F32)],
        compiler_params=pltpu.CompilerParams(
            dimension_semantics=("arbitrary",), vmem_limit_bytes=VMEM_LIMIT),
        name="mixer_sample",
    )(x, pool_prev, s0, cos2, sin2, *weights, *decay, after, *(prev_out if has_prev else ()))
    return y, (tails, states)


def _router_weights(w_rg, b_rg, w_re, b_re):
    depth = w_rg.shape[0]
    gap = EXPERT_LANE0 - N_EXPERT_GROUPS
    rest = LANES - EXPERT_LANE0 - N_EXPERTS
    wr = jnp.concatenate([w_rg, jnp.zeros((depth, D_MODEL, gap), F32), w_re,
                          jnp.zeros((depth, D_MODEL, rest), F32)], axis=-1)
    br = jnp.concatenate([b_rg, jnp.zeros((depth, gap), F32), b_re, jnp.zeros((depth, rest), F32)],
                         axis=-1).reshape(depth, 1, LANES)
    wr_hi, wr_lo = _split_weight(wr)
    return jnp.concatenate([wr_hi, wr_lo], axis=-1), wr_hi, br


def _router_specs(layer):
    return [_layer_spec((None, D_MODEL, 2 * LANES), layer), _layer_spec((None, D_MODEL, LANES), layer),
            _layer_spec((None, 1, LANES), layer)]


def _sc_mesh():
    return plsc.VectorSubcoreMesh(core_axis_name="core", subcore_axis_name="subcore",
                                  num_cores=SC_CORES, num_subcores=SC_SUBCORES)


def _sc_params():
    params = pltpu.CompilerParams()
    if "needs_layout_passes" in pltpu.CompilerParams.__dataclass_fields__:
        params = dataclasses.replace(params, needs_layout_passes=False)
    return params


def _sc_gather(tables, idx, after):
    n = idx.shape[0]
    assert n % SC_WINDOW == 0
    nt = len(tables)

    def body(*refs):
        i_hbm = refs[nt]
        for t_hbm, o_hbm in zip(refs[:nt], refs[nt + 2:]):
            def gather_window(i_vmem, o_vmem, t_hbm=t_hbm):
                pltpu.sync_copy(t_hbm.at[i_vmem.at[0]], o_vmem)

            pltpu.emit_pipeline(
                gather_window, grid=(n // SC_WINDOW,),
                in_specs=[pl.BlockSpec((1, SC_WINDOW), lambda i: (0, i))],
                out_specs=[pl.BlockSpec((SC_WINDOW, t_hbm.shape[1]), lambda i: (i, 0))],
                core_axis_name=("core", "subcore"),
                dimension_semantics=(pltpu.PARALLEL,),
            )(i_hbm, o_hbm)

    out_type = tuple(jax.ShapeDtypeStruct((n, t.shape[1]), t.dtype) for t in tables)
    return pl.kernel(body, out_type=out_type, mesh=_sc_mesh(), name="sc_gather")(
        *tables, idx.reshape(1, n), after)


def _sc_slots(cls, rank, starts, n_slots):
    t = cls.shape[0]
    workers = SC_CORES * SC_SUBCORES
    slot_per, tok_per = n_slots // workers, t // workers
    assert n_slots % (workers * SC_LANES) == 0 and t % (workers * SC_LANES) == 0 and t & (t - 1) == 0
    assert n_slots % (SC_LANES * SC_UNROLL) == 0 and t % (SC_LANES * SC_UNROLL) == 0

    def body(cls_hbm, rank_hbm, starts_hbm, pos_hbm, slot_hbm, cls_v, rank_v, starts_v, pos_v, slot_v):
        wid = lax.axis_index("subcore") * SC_CORES + lax.axis_index("core")
        pltpu.sync_copy(cls_hbm, cls_v)
        pltpu.sync_copy(rank_hbm, rank_v)
        pltpu.sync_copy(starts_hbm, starts_v)

        lane = lax.iota(I32, SC_LANES)
        span = SC_LANES * SC_UNROLL

        @pl.loop(0, n_slots, step=span)
        def _(i):
            for u in range(SC_UNROLL):
                j = i + u * SC_LANES
                slot_v[pl.ds(j, SC_LANES)] = (lane + j) & (t - 1)

        @pl.loop(0, t, step=span)
        def _(i):
            for u in range(SC_UNROLL):
                j = i + u * SC_LANES
                at = pl.ds(j, SC_LANES)
                pos = plsc.load_gather(starts_v, [cls_v[at]]) + rank_v[at]
                pos_v[at] = pos
                plsc.store_scatter(slot_v, [pos], lane + j)

        tok_off = pl.multiple_of(wid * tok_per, SC_LANES)
        pltpu.sync_copy(pos_v.at[pl.ds(tok_off, tok_per)], pos_hbm.at[pl.ds(tok_off, tok_per)])
        slot_off = pl.multiple_of(wid * slot_per, SC_LANES)
        pltpu.sync_copy(slot_v.at[pl.ds(slot_off, slot_per)], slot_hbm.at[pl.ds(slot_off, slot_per)])

    return pl.kernel(
        body, mesh=_sc_mesh(), compiler_params=_sc_params(), name="sc_slots",
        out_type=(jax.ShapeDtypeStruct((t,), I32), jax.ShapeDtypeStruct((n_slots,), I32)),
        scratch_types=[pltpu.VMEM((t,), I32), pltpu.VMEM((t,), I32), pltpu.VMEM((CLASS_ROWS,), I32),
                       pltpu.VMEM((t,), I32), pltpu.VMEM((n_slots,), I32)],
    )(cls, rank, starts)


def _moe_experts(routed, wgu, wd, after):
    ha, hb, gates, idx, counts = routed
    t = ha.shape[0]
    n_slots = t + N_CLASSES * PAIR_TILE
    n_tiles = n_slots // PAIR_TILE

    cnt = counts[:, 0].astype(I32)
    padded = (cnt + PAIR_TILE - 1) // PAIR_TILE * PAIR_TILE
    ends = jnp.cumsum(padded)
    starts = ends - padded
    tile_cls = jnp.minimum(
        jnp.sum(ends[None, :N_CLASSES] <= (jnp.arange(n_tiles, dtype=I32) * PAIR_TILE)[:, None], axis=1),
        N_CLASSES - 1).astype(I32)
    first = (tile_cls // PAIRS_PER_GROUP) * EXPERTS_PER_GROUP
    tile_ea = first + jnp.asarray(PAIR_LO, I32)[tile_cls % PAIRS_PER_GROUP]
    tile_eb = first + jnp.asarray(PAIR_HI, I32)[tile_cls % PAIRS_PER_GROUP]
    used = (ends[N_CLASSES - 1:N_CLASSES] // PAIR_TILE).astype(I32)

    pos, slot_tok = _sc_slots(idx[0], idx[1], starts, n_slots)
    hsa, hsb, gate_s = _sc_gather((ha, hb, gates), slot_tok, after=wgu)

    row = lambda i, ea, eb, nu: (jnp.minimum(i, nu[0] - 1), 0)
    w_spec = lambda shape, which: pl.BlockSpec(
        (1,) + shape, lambda i, ea, eb, nu: ((ea, eb)[which][i], 0, 0))
    gu_shape, d_shape = (D_MODEL, 2 * D_EXPERT), (D_EXPERT, D_MODEL)
    ysa, ysb = pl.pallas_call(
        _moe_pair_kernel,
        grid_spec=pltpu.PrefetchScalarGridSpec(
            num_scalar_prefetch=3, grid=(n_tiles,),
            in_specs=[pl.BlockSpec((PAIR_TILE, PACK_W), row), pl.BlockSpec((PAIR_TILE, PACK_W), row),
                      pl.BlockSpec((PAIR_TILE, LANES), row),
                      w_spec(gu_shape, 0), w_spec(d_shape, 0), w_spec(gu_shape, 1), w_spec(d_shape, 1),
                      _ANY_SPEC],
            out_specs=[pl.BlockSpec((PAIR_TILE, PACK_W), row), pl.BlockSpec((PAIR_TILE, PACK_W), row)]),
        out_shape=[jax.ShapeDtypeStruct((n_slots, PACK_W), I32),
                   jax.ShapeDtypeStruct((n_slots, PACK_W), I32)],
        compiler_params=pltpu.CompilerParams(
            dimension_semantics=("arbitrary",), vmem_limit_bytes=VMEM_LIMIT),
        name="moe_pair",
    )(tile_ea, tile_eb, used, hsa, hsb, gate_s, wgu, wd, wgu, wd, after)
    return (ysa, ysb), pos


def _expert_cast(w_gate, w_up, w_down, layer, after):
    per = CAST_EXPERTS
    w_spec = lambda shape: pl.BlockSpec((None, per) + shape, lambda e: (layer, e, 0, 0))
    out = lambda shape: pl.BlockSpec((per,) + shape, lambda e: (e, 0, 0))
    return pl.pallas_call(
        _expert_cast_kernel,
        grid=(N_EXPERTS // per,),
        in_specs=[w_spec((D_MODEL, D_EXPERT)), w_spec((D_MODEL, D_EXPERT)), w_spec((D_EXPERT, D_MODEL)),
                  _ANY_SPEC],
        out_specs=[out((D_MODEL, 2 * D_EXPERT)), out((D_EXPERT, D_MODEL))],
        out_shape=[jax.ShapeDtypeStruct((N_EXPERTS, D_MODEL, 2 * D_EXPERT), BF16),
                   jax.ShapeDtypeStruct((N_EXPERTS, D_EXPERT, D_MODEL), BF16)],
        compiler_params=pltpu.CompilerParams(
            dimension_semantics=("arbitrary",), vmem_limit_bytes=VMEM_LIMIT),
        name="expert_cast",
    )(w_gate, w_up, w_down, after)


def _final_norm(x, moe_y, g, rows):
    t = x.shape[0]
    rows = min(rows, t)
    assert t % rows == 0
    tok = lambda i: (i, 0)
    return pl.pallas_call(
        _final_norm_kernel,
        grid=(t // rows,),
        in_specs=[pl.BlockSpec((rows, D_MODEL), tok), pl.BlockSpec((rows, PACK_W), tok),
                  pl.BlockSpec((rows, PACK_W), tok), _const_spec((1, D_MODEL))],
        out_specs=pl.BlockSpec((rows, D_MODEL), tok),
        out_shape=jax.ShapeDtypeStruct(x.shape, F32),
        compiler_params=pltpu.CompilerParams(
            dimension_semantics=("arbitrary",), vmem_limit_bytes=VMEM_LIMIT),
        name="final_norm",
    )(x, *moe_y, g.reshape(1, D_MODEL))


def _moe_dense(x, norm_g, router, wgu, wd, norm_final, layer, final_norm, rows, after):
    t = x.shape[0]
    assert t % rows == 0
    tok = lambda i, e: (i, 0)
    w_spec = lambda shape: pl.BlockSpec((1,) + shape, lambda i, e: (e, 0, 0))
    return pl.pallas_call(
        functools.partial(_moe_dense_kernel, final_norm=final_norm),
        grid=(t // rows, N_EXPERTS),
        in_specs=[pl.BlockSpec((rows, D_MODEL), tok), _layer_spec((None, 1, D_MODEL), layer),
                  *_router_specs(layer),
                  w_spec((D_MODEL, 2 * D_EXPERT)), w_spec((D_EXPERT, D_MODEL)),
                  _const_spec((1, D_MODEL)), _ANY_SPEC],
        out_specs=pl.BlockSpec((rows, D_MODEL), tok),
        out_shape=jax.ShapeDtypeStruct(x.shape, F32),
        scratch_shapes=[pltpu.VMEM((rows, D_MODEL), BF16),
                        pltpu.VMEM((rows, LANES), F32),
                        pltpu.VMEM((rows, D_MODEL), F32)],
        compiler_params=pltpu.CompilerParams(
            dimension_semantics=("arbitrary", "arbitrary"), vmem_limit_bytes=VMEM_LIMIT),
        name="moe_dense",
    )(x, norm_g, *router, wgu, wd, norm_final.reshape(1, D_MODEL), after)


def kernel(x_prompt, x_sample, cache_pool, state_ret, norm_mix, w_in, w_pool, pool_scale, ret_gn, w_out, norm_ffn, w_router_group, b_router_group, w_router_expert, b_router_expert, w_gate, w_up, w_down, norm_final):
    depth = norm_mix.shape[0]
    row = lambda a: a.reshape(depth, 1, a.shape[-1])
    mix_split = (row(norm_mix), *_split_weight(w_in), *_split_weight(w_pool), row(pool_scale),
                 row(ret_gn), *_split_weight(w_out))
    mix_hi = tuple(mix_split[i] for i in (0, 1, 3, 5, 6, 7))
    router = _router_weights(w_router_group, b_router_group, w_router_expert, b_router_expert)
    norm_ffn = row(norm_ffn)
    pool_prev = jnp.swapaxes(cache_pool, 1, 2)

    yp, ys = x_prompt, x_sample
    moe_p = None
    out_p = out_s = None
    for l in range(depth):
        yp, out_p, routed = _mixer_prompt(
            yp, moe_p, mix_split, (norm_ffn, *router), out_p, l, rows=512,
            precise_tail=PRECISE_TAIL_STEPS if l < depth - 1 else 0, after=ys)
        wgu, wd = _expert_cast(w_gate, w_up, w_down, l, after=routed[-1])
        ys, out_s = _mixer_sample(ys, pool_prev, state_ret, mix_hi, out_s, l, after=wgu)
        sorted_y, pos = _moe_experts(routed, wgu, wd, after=ys)
        ys = _moe_dense(ys.reshape(-1, D_MODEL), norm_ffn, router, wgu, wd, norm_final,
                        l, l == depth - 1, rows=1024, after=sorted_y[0]).reshape(ys.shape)
        moe_p = _sc_gather(sorted_y, pos, after=routed[-1])
    yp = _final_norm(yp.reshape(-1, D_MODEL), moe_p, norm_final, rows=2048).reshape(yp.shape)
    return (yp, ys, *out_p, jnp.swapaxes(out_s[0], 1, 2), out_s[1])
```

```python
import dataclasses
import functools

import jax
import jax.numpy as jnp
import numpy as np
from jax import lax
from jax.experimental import pallas as pl
from jax.experimental.pallas import tpu as pltpu
from jax.experimental.pallas import tpu_sc as plsc

F32 = jnp.float32
BF16 = jnp.bfloat16
I32 = jnp.int32

D_MODEL = 1024
POOL_WIDTH = 512
POOL_WINDOWS = (2, 4, 8, 16)
POOL_GROUP_DIM = 128
POOL_BUF = 15
POOL_HIST = 16
RET_WIDTH = 512
RET_HEADS = 4
RET_HEAD_DIM = 128
RET_CHUNK = 128
ROPE_BASE = 10000.0
IN_WIDTH = POOL_WIDTH + 4 * RET_WIDTH
N_EXPERT_GROUPS = 4
EXPERTS_PER_GROUP = 4
N_EXPERTS = 16
D_EXPERT = 256
RMS_EPS = 1e-6
GN_EPS = 1e-5
PAST_LEN = 16384

LANES = 128
SUBLANES = 8
EXPERT_LANE0 = 8
PAIRS_PER_GROUP = 6
N_CLASSES = N_EXPERT_GROUPS * PAIRS_PER_GROUP
CLASS_ROWS = 32
PAIR_LO = (0, 0, 0, 1, 1, 2)
PAIR_HI = (1, 2, 3, 2, 3, 3)
GATE_LO, GATE_HI = 0, 1
COUNT_BLOCK = 256
PAIR_TILE = 256
CAST_EXPERTS = 2
PACK_W = D_MODEL // 4
SC_CORES, SC_SUBCORES, SC_LANES = 2, 16, 16
SC_WINDOW = 128
SC_UNROLL = 8
PRECISE_TAIL_STEPS = 1
CHUNK = 128
SAMPLE_SEQS = 16
VMEM_LIMIT = 56 * 1024 * 1024


def _dot(a, b):
    return jnp.dot(a, b, preferred_element_type=F32)


def _dot_nt(a, b):
    return lax.dot_general(a, b, (((1,), (1,)), ((), ())), preferred_element_type=F32)


def _bf16_round_bits(u):
    return (u + jnp.uint32(0x7FFF) + ((u >> 16) & jnp.uint32(1))) & jnp.uint32(0xFFFF0000)


def _split(a):
    hi = pltpu.bitcast(_bf16_round_bits(pltpu.bitcast(a, jnp.uint32)), F32)
    return hi.astype(BF16), (a - hi).astype(BF16)


def _mm(a, b, precise, nt=False):
    dot = _dot_nt if nt else _dot
    if precise:
        b_hi, b_lo = b if isinstance(b, tuple) else _split(b)
        a_hi, a_lo = _split(a)
        return dot(a_hi, b_hi) + dot(a_lo, b_hi) + dot(a_hi, b_lo)
    return dot(a.astype(BF16), b[0] if isinstance(b, tuple) else b.astype(BF16))


def _rmsnorm(x, g):
    ms = jnp.mean(x * x, axis=-1, keepdims=True)
    return x * lax.rsqrt(ms + RMS_EPS) * g


def _pool_mix(ubuf, rows, t_first, n_prev, wpool_refs, pscale, precise=False, row0=0):
    ns = ubuf.shape[0]
    t = t_first + lax.broadcasted_iota(I32, (1, rows, POOL_GROUP_DIM), 1)
    base = POOL_HIST + row0
    outs = []
    for j, w in enumerate(POOL_WINDOWS):
        lanes = slice(j * POOL_GROUP_DIM, (j + 1) * POOL_GROUP_DIM)
        uj = ubuf[:, base:base + rows, lanes]
        acc = uj
        for i in range(1, w):
            acc = acc + ubuf[:, base - i:base - i + rows, lanes]
        cnt = jnp.minimum(w, n_prev + t + 1).astype(F32)
        d = (acc / cnt - uj).reshape(ns * rows, POOL_GROUP_DIM)
        outs.append(_mm(d, tuple(w[j] for w in wpool_refs), precise))
    return jnp.concatenate(outs, axis=-1) * pscale


def _rope(xh, cos2, sin2):
    return xh * cos2 + pltpu.roll(xh, RET_HEAD_DIM // 2, 1) * sin2


def _group_norm(o):
    mu = jnp.mean(o, axis=-1, keepdims=True)
    c = o - mu
    var = jnp.mean(c * c, axis=-1, keepdims=True)
    return c * lax.rsqrt(var + GN_EPS)


def _silu(x):
    return x * (1.0 / (1.0 + jnp.exp(-x)))


def _head(a, h):
    return a[:, h * RET_HEAD_DIM:(h + 1) * RET_HEAD_DIM]


def _qkvg(z):
    p, r = POOL_WIDTH, RET_WIDTH
    return z[:, p:p + r], z[:, p + r:p + 2 * r], z[:, p + 2 * r:p + 3 * r], z[:, p + 3 * r:p + 4 * r]


def _pack_bf16_pair(a, b):
    ua = pltpu.bitcast(a.astype(BF16).astype(F32), jnp.uint32)
    ub = pltpu.bitcast(b.astype(BF16).astype(F32), jnp.uint32)
    return pltpu.bitcast((ua >> 16) | (ub & jnp.uint32(0xFFFF0000)), I32)


def _unpack_bf16_pair(w):
    u = pltpu.bitcast(w, jnp.uint32)
    return pltpu.bitcast(u << 16, F32), pltpu.bitcast(u & jnp.uint32(0xFFFF0000), F32)


def _pack_row(y):
    q = PACK_W
    return _pack_bf16_pair(y[:, 0:q], y[:, q:2 * q]), _pack_bf16_pair(y[:, 2 * q:3 * q], y[:, 3 * q:])


def _unpack_row(wa, wb):
    return jnp.concatenate([*_unpack_bf16_pair(wa), *_unpack_bf16_pair(wb)], axis=-1)


def _zero_other_layers(ref, layer):
    for j in range(ref.shape[0]):
        if j != layer:
            ref[j] = jnp.zeros(ref.shape[1:], ref.dtype)


def _mixer_prompt_kernel(*refs, rows, steps, moe_in, precise_tail, layer, has_prev):
    (x_ref, ya_ref, yb_ref, cos_ref, sin_ref, ng_ref, win_hi_ref, win_lo_ref, wpool_hi_ref,
     wpool_lo_ref, pscale_ref, gn_ref, wout_hi_ref, wout_lo_ref, dmat_ref, qdec_ref, kdec_ref,
     cdec_ref) = refs[:18]
    route_in = refs[18:22]
    y_ref, tail_ref, sfin_ref, *route_out, ubuf, s_scr, ret_scr, z_scr, count_scr = (
        refs[23 + 2 * has_prev:])
    c = pl.program_id(1)

    @pl.when((pl.program_id(0) == 0) & (c == 0))
    def _():
        count_scr[...] = jnp.zeros_like(count_scr)
    if not has_prev:
        _zero_other_layers(tail_ref, layer)
        _zero_other_layers(sfin_ref, layer)
        tail_ref, sfin_ref = tail_ref.at[layer], sfin_ref.at[layer]

    @pl.when(c == 0)
    def _():
        ubuf[:, 0:POOL_HIST, :] = jnp.zeros((1, POOL_HIST, POOL_WIDTH), F32)
        s_scr[...] = jnp.zeros_like(s_scr)

    kv_cols = slice(POOL_WIDTH + RET_WIDTH, POOL_WIDTH + 3 * RET_WIDTH)

    def step(kv_precise, full_from):
        x = x_ref[0]
        if moe_in:
            x = x + _unpack_row(ya_ref[0], yb_ref[0])
        hn = _rmsnorm(x, ng_ref[...])
        hi, lo = _split(hn) if kv_precise else (hn.astype(BF16), None)
        z_scr[...] = _dot(hi, win_hi_ref[...])
        if kv_precise and full_from:
            z_scr[:full_from, kv_cols] += (_dot(lo[:full_from], win_hi_ref[:, kv_cols])
                                           + _dot(hi[:full_from], win_lo_ref[:, kv_cols]))
        if full_from < rows:
            z_scr[full_from:, :] += (_dot(lo[full_from:], win_hi_ref[...])
                                     + _dot(hi[full_from:], win_lo_ref[...]))

        ubuf[0, POOL_HIST:POOL_HIST + rows, :] = z_scr[:, :POOL_WIDTH]
        pool_w = (wpool_hi_ref, wpool_lo_ref)
        pool_parts = []
        if full_from:
            pool_parts.append(_pool_mix(ubuf, full_from, c * rows, 0, pool_w, pscale_ref[...]))
        if full_from < rows:
            pool_parts.append(_pool_mix(ubuf, rows - full_from, c * rows + full_from, 0, pool_w,
                                        pscale_ref[...], precise=True, row0=full_from))
        pool_out = jnp.concatenate(pool_parts, axis=0)
        tail_ref[...] = ubuf[:, rows + POOL_HIST - POOL_BUF:rows + POOL_HIST, :]
        ubuf[:, 0:POOL_HIST, :] = ubuf[:, rows:rows + POOL_HIST, :]

        scale = RET_HEAD_DIM ** -0.5
        for ci in range(rows // CHUNK):
            rs = slice(ci * CHUNK, (ci + 1) * CHUNK)
            full = ci * CHUNK >= full_from
            cos2 = cos_ref[rs, :]
            sin2 = sin_ref[rs, :]
            for h in range(RET_HEADS):
                col = lambda part: slice(POOL_WIDTH + part * RET_WIDTH + h * RET_HEAD_DIM,
                                         POOL_WIDTH + part * RET_WIDTH + (h + 1) * RET_HEAD_DIM)
                q = _rope(z_scr[rs, col(0)], cos2, sin2)
                k = _rope(z_scr[rs, col(1)], cos2, sin2) * scale
                v = z_scr[rs, col(2)]
                s_old = s_scr[h]
                scores = _mm(q, k, full, nt=True) * dmat_ref[h]
                qd = q * _head(qdec_ref[...], h)
                o = _mm(scores, v, full) + _mm(qd, s_old, full)
                kd = k * _head(kdec_ref[...], h)
                s_scr[h] = s_old * cdec_ref[h] + _mm(kd.T, v, kv_precise)
                on = _group_norm(o) * _head(gn_ref[...], h)
                ret_scr[rs, h * RET_HEAD_DIM:(h + 1) * RET_HEAD_DIM] = _silu(z_scr[rs, col(3)]) * on

        mix_in = jnp.concatenate([pool_out, ret_scr[...]], axis=-1)
        y_ref[0] = x + _dot(mix_in.astype(BF16), wout_hi_ref[...])
        if full_from < rows:
            m_hi, m_lo = _split(mix_in[full_from:])
            y_ref[0, full_from:, :] += _dot(m_lo, wout_hi_ref[...]) + _dot(m_hi, wout_lo_ref[...])
        sfin_ref[0] = s_scr[...]
        _route_tile(y_ref[0], full_from, *route_in, *route_out, count_scr)

    if precise_tail:
        pl.when(c < steps - precise_tail)(lambda: step(False, rows))
        if precise_tail > 1:
            pl.when((c >= steps - precise_tail) & (c < steps - 1))(lambda: step(True, rows))
        pl.when(c == steps - 1)(lambda: step(True, rows - CHUNK))
    else:
        step(False, rows)


def _mixer_sample_kernel(*refs, seq_len, layer, has_prev):
    (x_ref, prev_ref, s0_ref, cos_ref, sin_ref, ng_ref, win_ref, wpool_ref, pscale_ref, gn_ref,
     wout_ref, dmat_ref, qdec_ref, kdec_ref, cdec_ref) = refs[:15]
    y_ref, tail_ref, sfin_ref, ubuf, ret_scr = refs[16 + 2 * has_prev:]
    if not has_prev:
        _zero_other_layers(tail_ref, layer)
        _zero_other_layers(sfin_ref, layer)
        tail_ref, sfin_ref = tail_ref.at[layer], sfin_ref.at[layer]
    ns = SAMPLE_SEQS
    rows = ns * seq_len
    x = x_ref[...].reshape(rows, D_MODEL)
    hn = _rmsnorm(x, ng_ref[...]).astype(BF16)
    z = _dot(hn, win_ref[...])
    for j in range(POOL_BUF):
        ubuf[:, POOL_HIST - POOL_BUF + j, :] = prev_ref[j]
    ubuf[:, POOL_HIST:POOL_HIST + seq_len, :] = z[:, :POOL_WIDTH].reshape(ns, seq_len, POOL_WIDTH)
    pool_out = _pool_mix(ubuf, seq_len, 0, POOL_BUF, (wpool_ref,), pscale_ref[...])
    for j in range(POOL_BUF):
        tail_ref[j] = ubuf[:, seq_len + POOL_HIST - POOL_BUF + j, :]

    q_all, k_all, v_all, g_all = _qkvg(z)
    scale = RET_HEAD_DIM ** -0.5
    cos2 = cos_ref[...]
    sin2 = sin_ref[...]
    tok_seq = lax.broadcasted_iota(I32, (RET_HEAD_DIM, rows), 1) // seq_len
    for h in range(RET_HEADS):
        q = _rope(_head(q_all, h), cos2, sin2)
        k = _rope(_head(k_all, h), cos2, sin2) * scale
        vb = _head(v_all, h).astype(BF16)
        s_old = s0_ref[:, h]
        scores = _dot_nt(q.astype(BF16), k.astype(BF16)) * dmat_ref[h]
        qd = (q * _head(qdec_ref[...], h)).astype(BF16).reshape(ns, seq_len, RET_HEAD_DIM)
        o_state = jnp.einsum('bid,bde->bie', qd, s_old.astype(BF16), preferred_element_type=F32)
        o = _dot(scores.astype(BF16), vb) + o_state.reshape(rows, RET_HEAD_DIM)
        kdt = (k * _head(kdec_ref[...], h)).T
        lhs = jnp.concatenate(
            [jnp.where(tok_seq == b, kdt, 0.0).astype(BF16) for b in range(ns)], axis=0)
        upd = _dot(lhs, vb).reshape(ns, RET_HEAD_DIM, RET_HEAD_DIM)
        sfin_ref[:, h] = s_old * cdec_ref[h] + upd
        on = _group_norm(o) * _head(gn_ref[...], h)
        ret_scr[:, h * RET_HEAD_DIM:(h + 1) * RET_HEAD_DIM] = _silu(_head(g_all, h)) * on

    mix_in = jnp.concatenate([pool_out, ret_scr[...]], axis=-1).astype(BF16)
    y_ref[...] = (x + _dot(mix_in, wout_ref[...])).reshape(ns, seq_len, D_MODEL)


def _router_logits(hn, wr_cat_ref, wr_hi_ref, br_ref, precise_from=0):
    parts = []
    if precise_from:
        parts.append(_dot(hn[:precise_from].astype(BF16), wr_hi_ref[...]))
    if precise_from < hn.shape[0]:
        h_hi, h_lo = _split(hn[precise_from:])
        part = _dot(h_hi, wr_cat_ref[...])
        parts.append(part[:, :LANES] + part[:, LANES:] + _dot(h_lo, wr_hi_ref[...]))
    return jnp.concatenate(parts, axis=0) + br_ref[...]


def _select(group_lg, expert_lg, gidx, eidx, axis):
    red = dict(axis=axis, keepdims=True)
    neg = jnp.float32(-jnp.inf)
    gmax = jnp.max(group_lg, **red)
    g_sel = jnp.min(jnp.where(group_lg == gmax, gidx, N_EXPERT_GROUPS), **red)
    p_sel = 1.0 / jnp.sum(jnp.exp(group_lg - gmax), **red)
    emask = (eidx >> 2) == g_sel
    v1 = jnp.max(jnp.where(emask, expert_lg, neg), **red)
    i1 = jnp.min(jnp.where(emask & (expert_lg == v1), eidx, N_EXPERTS), **red)
    emask2 = emask & (eidx != i1)
    v2 = jnp.max(jnp.where(emask2, expert_lg, neg), **red)
    i2 = jnp.min(jnp.where(emask2 & (expert_lg == v2), eidx, N_EXPERTS), **red)
    e2 = jnp.exp(v2 - v1)
    return g_sel, i1, i2, p_sel / (1.0 + e2), p_sel * e2 / (1.0 + e2)


def _route_tile(x, precise_from, ng_ref, wr_cat_ref, wr_hi_ref, br_ref, ha_ref, hb_ref, gate_ref,
                idx_ref, count_ref, carry_scr):
    rows = x.shape[0]
    hn = _rmsnorm(x, ng_ref[...])
    ha_ref[...], hb_ref[...] = _pack_row(hn)
    lgt = _router_logits(hn, wr_cat_ref, wr_hi_ref, br_ref, precise_from).T
    neg = jnp.float32(-jnp.inf)
    gidx = lax.broadcasted_iota(I32, (SUBLANES, rows), 0)
    eidx = lax.broadcasted_iota(I32, (N_EXPERTS, rows), 0)
    group_lg = jnp.where(gidx < N_EXPERT_GROUPS, lgt[0:SUBLANES], neg)
    expert_lg = lgt[EXPERT_LANE0:EXPERT_LANE0 + N_EXPERTS]
    g_sel, i1, i2, w1, w2 = _select(group_lg, expert_lg, gidx, eidx, 0)
    lo = jnp.minimum(i1, i2) - g_sel * EXPERTS_PER_GROUP
    hi = jnp.maximum(i1, i2) - g_sel * EXPERTS_PER_GROUP
    cls = g_sel * PAIRS_PER_GROUP + ((lo * (7 - lo)) >> 1) + (hi - lo - 1)
    w_lo = jnp.where(i1 < i2, w1, w2)
    w_hi = jnp.where(i1 < i2, w2, w1)

    crow = lax.broadcasted_iota(I32, (CLASS_ROWS, rows), 0)
    onehot = jnp.where(crow == cls, 1.0, 0.0)
    n_blk = rows // COUNT_BLOCK
    blocks = [onehot[:, j * COUNT_BLOCK:(j + 1) * COUNT_BLOCK] for j in range(n_blk)]
    r = lax.broadcasted_iota(I32, (COUNT_BLOCK, COUNT_BLOCK), 0)
    c = lax.broadcasted_iota(I32, (COUNT_BLOCK, COUNT_BLOCK), 1)
    upper = jnp.where(r < c, 1.0, 0.0).astype(BF16)
    within = _dot(jnp.concatenate(blocks, axis=0).astype(BF16), upper)
    carry = carry_scr[:, 0:1]
    ranks = []
    for j in range(n_blk):
        before = within[j * CLASS_ROWS:(j + 1) * CLASS_ROWS] + carry
        ranks.append(jnp.sum(blocks[j] * before, axis=0, keepdims=True))
        carry = carry + jnp.sum(blocks[j], axis=1, keepdims=True)
    rank = jnp.concatenate(ranks, axis=1)
    carry_scr[...] = jnp.broadcast_to(carry, carry_scr.shape)
    count_ref[...] = carry_scr[...]

    row8 = lax.broadcasted_iota(I32, (SUBLANES, rows), 0)
    idx_ref[...] = jnp.where(row8 == 0, cls, jnp.where(row8 == 1, rank.astype(I32), 0))
    rowl = lax.broadcasted_iota(I32, (LANES, rows), 0)
    gate_ref[...] = jnp.where(rowl == GATE_LO, w_lo, jnp.where(rowl == GATE_HI, w_hi, 0.0)).T


def _expert_cast_kernel(wg_ref, wu_ref, wd_ref, after_ref, wgu_ref, wdb_ref):
    del after_ref
    wgu_ref[:, :, :D_EXPERT] = wg_ref[...].astype(BF16)
    wgu_ref[:, :, D_EXPERT:] = wu_ref[...].astype(BF16)
    wdb_ref[...] = wd_ref[...].astype(BF16)


def _moe_pair_kernel(ea_ref, eb_ref, used_ref, ha_ref, hb_ref, gate_ref, wgu_a_ref, wd_a_ref,
                     wgu_b_ref, wd_b_ref, after_ref, ya_ref, yb_ref):
    del ea_ref, eb_ref, after_ref

    @pl.when(pl.program_id(0) < used_ref[0])
    def _():
        h = _unpack_row(ha_ref[...], hb_ref[...]).astype(BF16)
        gates = gate_ref[...]
        y = None
        for wgu_ref, wd_ref, lane in ((wgu_a_ref, wd_a_ref, GATE_LO), (wgu_b_ref, wd_b_ref, GATE_HI)):
            gu = _dot(h, wgu_ref[0])
            act = _silu(gu[:, :D_EXPERT]) * gu[:, D_EXPERT:]
            part = _dot((act * gates[:, lane:lane + 1]).astype(BF16), wd_ref[0])
            y = part if y is None else y + part
        ya_ref[...], yb_ref[...] = _pack_row(y)


def _final_norm_kernel(x_ref, ya_ref, yb_ref, g_ref, o_ref):
    o_ref[...] = _rmsnorm(x_ref[...] + _unpack_row(ya_ref[...], yb_ref[...]), g_ref[...])


def _moe_dense_kernel(x_ref, ng_ref, wr_cat_ref, wr_hi_ref, br_ref, wgu_ref, wd_ref, nf_ref,
                      after_ref, y_ref, h_scr, gate_scr, acc_scr, *, final_norm):
    del after_ref
    e = pl.program_id(1)

    @pl.when(e == 0)
    def _():
        hn = _rmsnorm(x_ref[...], ng_ref[...])
        h_scr[...] = hn.astype(BF16)
        lg = _router_logits(hn, wr_cat_ref, wr_hi_ref, br_ref)
        lane = lax.broadcasted_iota(I32, lg.shape, 1)
        neg = jnp.float32(-jnp.inf)
        group_lg = jnp.where(lane < N_EXPERT_GROUPS, lg, neg)
        is_expert = (lane >= EXPERT_LANE0) & (lane < EXPERT_LANE0 + N_EXPERTS)
        _, i1, i2, w1, w2 = _select(group_lg, jnp.where(is_expert, lg, neg), lane,
                                    jnp.where(is_expert, lane - EXPERT_LANE0, N_EXPERTS), 1)
        gate_scr[...] = (jnp.where(lane == i1 + EXPERT_LANE0, w1, 0.0)
                         + jnp.where(lane == i2 + EXPERT_LANE0, w2, 0.0))
        acc_scr[...] = jnp.zeros_like(acc_scr)

    lane = lax.broadcasted_iota(I32, gate_scr.shape, 1)
    gate = jnp.sum(jnp.where(lane == EXPERT_LANE0 + e, gate_scr[...], 0.0), axis=-1, keepdims=True)
    gu = _dot(h_scr[...], wgu_ref[0])
    act = _silu(gu[:, :D_EXPERT]) * gu[:, D_EXPERT:]
    acc_scr[...] += _dot((act * gate).astype(BF16), wd_ref[0])

    @pl.when(e == N_EXPERTS - 1)
    def _():
        y = x_ref[...] + acc_scr[...]
        if final_norm:
            y = _rmsnorm(y, nf_ref[...])
        y_ref[...] = y


def _const_spec(shape):
    return pl.BlockSpec(shape, lambda *_: (0,) * len(shape))


def _resident_spec(shape):
    return pl.BlockSpec(shape, lambda *_: (0,) * len(shape), pipeline_mode=pl.Buffered(1))


_ANY_SPEC = pl.BlockSpec(memory_space=pl.ANY)


def _rope_tables(pos):
    half = RET_HEAD_DIM // 2
    inv = np.float32(ROPE_BASE) ** (-np.arange(half, dtype=np.float32) / np.float32(half))
    ang = pos.astype(np.float32)[:, None] * inv[None, :]
    cos, sin = np.cos(ang), np.sin(ang)
    return np.concatenate([cos, cos], -1), np.concatenate([-sin, sin], -1)


def _decay_consts(c, reps):
    f = np.float32
    log_g = np.log(f(1.0) - f(2.0) ** (f(-5.0) - np.arange(RET_HEADS, dtype=f)))
    i = np.arange(c, dtype=f)
    diff = i[:, None] - i[None, :]
    dmat = np.where(diff[None] >= 0, np.exp(np.maximum(diff, f(0))[None] * log_g[:, None, None]), f(0))
    qdec = np.exp((i + f(1))[None, :] * log_g[:, None])
    kdec = np.exp((f(c) - f(1) - i)[None, :] * log_g[:, None])
    cdec = np.exp(f(c) * log_g)
    dmat = np.einsum('ab,hij->haibj', np.eye(reps, dtype=f), dmat).reshape(RET_HEADS, reps * c, reps * c)
    lanes = lambda t: np.repeat(np.tile(t, (1, reps)).T, RET_HEAD_DIM, axis=1)
    cdec = np.broadcast_to(cdec[:, None, None], (RET_HEADS, 1, RET_HEAD_DIM))
    return tuple(np.ascontiguousarray(a, dtype=f) for a in (dmat, lanes(qdec), lanes(kdec), cdec))


def _split_weight(w):
    hi = lax.bitcast_convert_type(_bf16_round_bits(lax.bitcast_convert_type(w, jnp.uint32)), F32)
    return hi.astype(BF16), (w - hi).astype(BF16)


_W_IN_BLOCK = (None, D_MODEL, IN_WIDTH)
_W_POOL_BLOCK = (None, len(POOL_WINDOWS), POOL_GROUP_DIM, POOL_GROUP_DIM)
_W_OUT_BLOCK = (None, D_MODEL, D_MODEL)
_DECAY_SPECS = [
    _const_spec((RET_HEADS, CHUNK, CHUNK)), _const_spec((CHUNK, RET_WIDTH)),
    _const_spec((CHUNK, RET_WIDTH)), _const_spec((RET_HEADS, 1, RET_HEAD_DIM)),
]


def _layer_spec(block, layer, resident=False):
    index_map = lambda *_: (layer,) + (0,) * (len(block) - 1)
    if resident:
        return pl.BlockSpec(block, index_map, pipeline_mode=pl.Buffered(1))
    return pl.BlockSpec(block, index_map)


def _mixer_weight_specs(layer, split):
    n = 2 if split else 1
    row = lambda width: _layer_spec((None, 1, width), layer)
    return [row(D_MODEL), *[_layer_spec(_W_IN_BLOCK, layer, True)] * n,
            *[_layer_spec(_W_POOL_BLOCK, layer)] * n, row(POOL_WIDTH), row(RET_WIDTH),
            *[_layer_spec(_W_OUT_BLOCK, layer, True)] * n]


def _chain_out_spec(block, depth, layer, has_prev):
    zeros = (0,) * (len(block) - 1)
    if has_prev:
        return pl.BlockSpec((None,) + block, lambda i, *_: (layer, i) + zeros)
    return pl.BlockSpec((depth,) + block, lambda i, *_: (0, i) + zeros)


def _mixer_prompt(x, moe_y, weights, route_w, prev_out, layer, rows, precise_tail, after):
    b, l, _ = x.shape
    t = b * l
    depth = weights[0].shape[0]
    assert l % rows == 0 and rows % CHUNK == 0
    cos2, sin2 = _rope_tables(np.arange(l))
    decay = _decay_consts(RET_CHUNK, 1)
    steps = l // rows
    tok = lambda i, c: (i, c, 0)
    if moe_y is None:
        ya = yb = jnp.zeros((1, rows, PACK_W), I32)
        y_spec = _const_spec((1, rows, PACK_W))
    else:
        ya, yb = (t.reshape(b, l, PACK_W) for t in moe_y)
        y_spec = pl.BlockSpec((1, rows, PACK_W), tok)
    has_prev = prev_out is not None
    n_in = 23
    flat = lambda i, c: (i * steps + c, 0)
    y, tails, states, *routed = pl.pallas_call(
        functools.partial(_mixer_prompt_kernel, rows=rows, steps=steps, moe_in=moe_y is not None,
                          precise_tail=precise_tail, layer=layer, has_prev=has_prev),
        grid=(b, steps),
        in_specs=[pl.BlockSpec((1, rows, D_MODEL), tok), y_spec, y_spec,
                  pl.BlockSpec((rows, RET_HEAD_DIM), lambda i, c: (c, 0)),
                  pl.BlockSpec((rows, RET_HEAD_DIM), lambda i, c: (c, 0)),
                  *_mixer_weight_specs(layer, True), *_DECAY_SPECS,
                  _layer_spec((None, 1, D_MODEL), layer), *_router_specs(layer), _ANY_SPEC,
                  *([_ANY_SPEC, _ANY_SPEC] if has_prev else [])],
        out_specs=[pl.BlockSpec((1, rows, D_MODEL), tok),
                   _chain_out_spec((1, POOL_BUF, POOL_WIDTH), depth, layer, has_prev),
                   _chain_out_spec((1, RET_HEADS, RET_HEAD_DIM, RET_HEAD_DIM), depth, layer, has_prev),
                   pl.BlockSpec((rows, PACK_W), flat), pl.BlockSpec((rows, PACK_W), flat),
                   pl.BlockSpec((rows, LANES), flat),
                   pl.BlockSpec((SUBLANES, rows), lambda i, c: (0, i * steps + c)),
                   _const_spec((CLASS_ROWS, LANES))],
        out_shape=[jax.ShapeDtypeStruct(x.shape, F32),
                   jax.ShapeDtypeStruct((depth, b, POOL_BUF, POOL_WIDTH), F32),
                   jax.ShapeDtypeStruct((depth, b, RET_HEADS, RET_HEAD_DIM, RET_HEAD_DIM), F32),
                   jax.ShapeDtypeStruct((t, PACK_W), I32), jax.ShapeDtypeStruct((t, PACK_W), I32),
                   jax.ShapeDtypeStruct((t, LANES), F32), jax.ShapeDtypeStruct((SUBLANES, t), I32),
                   jax.ShapeDtypeStruct((CLASS_ROWS, LANES), F32)],
        input_output_aliases={n_in: 1, n_in + 1: 2} if has_prev else {},
        scratch_shapes=[pltpu.VMEM((1, POOL_HIST + rows, POOL_WIDTH), F32),
                        pltpu.VMEM((RET_HEADS, RET_HEAD_DIM, RET_HEAD_DIM), F32),
                        pltpu.VMEM((rows, RET_WIDTH), F32),
                        pltpu.VMEM((rows, IN_WIDTH), F32),
                        pltpu.VMEM((CLASS_ROWS, LANES), F32)],
        compiler_params=pltpu.CompilerParams(
            dimension_semantics=("arbitrary", "arbitrary"), vmem_limit_bytes=VMEM_LIMIT),
        name="mixer_prompt",
    )(x, ya, yb, cos2, sin2, *weights, *decay, *route_w, after, *(prev_out if has_prev else ()))
    return y, (tails, states), tuple(routed)


def _mixer_sample(x, pool_prev, s0, weights, prev_out, layer, after):
    b, l, _ = x.shape
    depth = s0.shape[0]
    ns = SAMPLE_SEQS
    assert ns * l == CHUNK and b % ns == 0
    cos2, sin2 = _rope_tables(PAST_LEN + np.arange(l))
    cos2, sin2 = np.tile(cos2, (ns, 1)), np.tile(sin2, (ns, 1))
    decay = _decay_consts(l, ns)
    seq3 = lambda i: (i, 0, 0)
    state_block = (ns, RET_HEADS, RET_HEAD_DIM, RET_HEAD_DIM)
    tail_block = (POOL_BUF, ns, POOL_WIDTH)
    if prev_out is None:
        tail_out = pl.BlockSpec((depth,) + tail_block, lambda i: (0, 0, i, 0))
    else:
        tail_out = pl.BlockSpec((None,) + tail_block, lambda i: (layer, 0, i, 0))
    has_prev = prev_out is not None
    n_in = 16
    y, tails, states = pl.pallas_call(
        functools.partial(_mixer_sample_kernel, seq_len=l, layer=layer, has_prev=has_prev),
        grid=(b // ns,),
        in_specs=[pl.BlockSpec((ns, l, D_MODEL), seq3),
                  pl.BlockSpec((None,) + tail_block, lambda i: (layer, 0, i, 0)),
                  pl.BlockSpec((None,) + state_block, lambda i: (layer, i, 0, 0, 0)),
                  _const_spec((CHUNK, RET_HEAD_DIM)), _const_spec((CHUNK, RET_HEAD_DIM)),
                  *_mixer_weight_specs(layer, False), *_DECAY_SPECS, _ANY_SPEC,
                  *([_ANY_SPEC, _ANY_SPEC] if has_prev else [])],
        out_specs=[pl.BlockSpec((ns, l, D_MODEL), seq3), tail_out,
                   _chain_out_spec(state_block, depth, layer, has_prev)],
        out_shape=[jax.ShapeDtypeStruct(x.shape, F32),
                   jax.ShapeDtypeStruct((depth, POOL_BUF, b, POOL_WIDTH), F32),
                   jax.ShapeDtypeStruct(s0.shape, F32)],
        input_output_aliases={n_in: 1, n_in + 1: 2} if has_prev else {},
        scratch_shapes=[pltpu.VMEM((ns, POOL_HIST + l, POOL_WIDTH), F32),
                        pltpu.VMEM((CHUNK, RET_WIDTH), F32)],
        compiler_params=pltpu.CompilerParams(
            dimension_semantics=("arbitrary",), vmem_limit_bytes=VMEM_LIMIT),
        name="mixer_sample",
    )(x, pool_prev, s0, cos2, sin2, *weights, *decay, after, *(prev_out if has_prev else ()))
    return y, (tails, states)


def _router_weights(w_rg, b_rg, w_re, b_re):
    depth = w_rg.shape[0]
    gap = EXPERT_LANE0 - N_EXPERT_GROUPS
    rest = LANES - EXPERT_LANE0 - N_EXPERTS
    wr = jnp.concatenate([w_rg, jnp.zeros((depth, D_MODEL, gap), F32), w_re,
                          jnp.zeros((depth, D_MODEL, rest), F32)], axis=-1)
    br = jnp.concatenate([b_rg, jnp.zeros((depth, gap), F32), b_re, jnp.zeros((depth, rest), F32)],
                         axis=-1).reshape(depth, 1, LANES)
    wr_hi, wr_lo = _split_weight(wr)
    return jnp.concatenate([wr_hi, wr_lo], axis=-1), wr_hi, br


def _router_specs(layer):
    return [_layer_spec((None, D_MODEL, 2 * LANES), layer), _layer_spec((None, D_MODEL, LANES), layer),
            _layer_spec((None, 1, LANES), layer)]


def _sc_mesh():
    return plsc.VectorSubcoreMesh(core_axis_name="core", subcore_axis_name="subcore",
                                  num_cores=SC_CORES, num_subcores=SC_SUBCORES)


def _sc_params():
    params = pltpu.CompilerParams()
    if "needs_layout_passes" in pltpu.CompilerParams.__dataclass_fields__:
        params = dataclasses.replace(params, needs_layout_passes=False)
    return params


def _sc_gather(tables, idx, after):
    n = idx.shape[0]
    assert n % SC_WINDOW == 0
    nt = len(tables)

    def body(*refs):
        i_hbm = refs[nt]
        for t_hbm, o_hbm in zip(refs[:nt], refs[nt + 2:]):
            def gather_window(i_vmem, o_vmem, t_hbm=t_hbm):
                pltpu.sync_copy(t_hbm.at[i_vmem.at[0]], o_vmem)

            pltpu.emit_pipeline(
                gather_window, grid=(n // SC_WINDOW,),
                in_specs=[pl.BlockSpec((1, SC_WINDOW), lambda i: (0, i))],
                out_specs=[pl.BlockSpec((SC_WINDOW, t_hbm.shape[1]), lambda i: (i, 0))],
                core_axis_name=("core", "subcore"),
                dimension_semantics=(pltpu.PARALLEL,),
            )(i_hbm, o_hbm)

    out_type = tuple(jax.ShapeDtypeStruct((n, t.shape[1]), t.dtype) for t in tables)
    return pl.kernel(body, out_type=out_type, mesh=_sc_mesh(), name="sc_gather")(
        *tables, idx.reshape(1, n), after)


def _sc_slots(cls, rank, starts, n_slots):
    t = cls.shape[0]
    workers = SC_CORES * SC_SUBCORES
    slot_per, tok_per = n_slots // workers, t // workers
    assert n_slots % (workers * SC_LANES) == 0 and t % (workers * SC_LANES) == 0 and t & (t - 1) == 0
    assert n_slots % (SC_LANES * SC_UNROLL) == 0 and t % (SC_LANES * SC_UNROLL) == 0

    def body(cls_hbm, rank_hbm, starts_hbm, pos_hbm, slot_hbm, cls_v, rank_v, starts_v, pos_v, slot_v):
        wid = lax.axis_index("subcore") * SC_CORES + lax.axis_index("core")
        pltpu.sync_copy(cls_hbm, cls_v)
        pltpu.sync_copy(rank_hbm, rank_v)
        pltpu.sync_copy(starts_hbm, starts_v)

        lane = lax.iota(I32, SC_LANES)
        span = SC_LANES * SC_UNROLL

        @pl.loop(0, n_slots, step=span)
        def _(i):
            for u in range(SC_UNROLL):
                j = i + u * SC_LANES
                slot_v[pl.ds(j, SC_LANES)] = (lane + j) & (t - 1)

        @pl.loop(0, t, step=span)
        def _(i):
            for u in range(SC_UNROLL):
                j = i + u * SC_LANES
                at = pl.ds(j, SC_LANES)
                pos = plsc.load_gather(starts_v, [cls_v[at]]) + rank_v[at]
                pos_v[at] = pos
                plsc.store_scatter(slot_v, [pos], lane + j)

        tok_off = pl.multiple_of(wid * tok_per, SC_LANES)
        pltpu.sync_copy(pos_v.at[pl.ds(tok_off, tok_per)], pos_hbm.at[pl.ds(tok_off, tok_per)])
        slot_off = pl.multiple_of(wid * slot_per, SC_LANES)
        pltpu.sync_copy(slot_v.at[pl.ds(slot_off, slot_per)], slot_hbm.at[pl.ds(slot_off, slot_per)])

    return pl.kernel(
        body, mesh=_sc_mesh(), compiler_params=_sc_params(), name="sc_slots",
        out_type=(jax.ShapeDtypeStruct((t,), I32), jax.ShapeDtypeStruct((n_slots,), I32)),
        scratch_types=[pltpu.VMEM((t,), I32), pltpu.VMEM((t,), I32), pltpu.VMEM((CLASS_ROWS,), I32),
                       pltpu.VMEM((t,), I32), pltpu.VMEM((n_slots,), I32)],
    )(cls, rank, starts)


def _moe_experts(routed, wgu, wd, after):
    ha, hb, gates, idx, counts = routed
    t = ha.shape[0]
    n_slots = t + N_CLASSES * PAIR_TILE
    n_tiles = n_slots // PAIR_TILE

    cnt = counts[:, 0].astype(I32)
    padded = (cnt + PAIR_TILE - 1) // PAIR_TILE * PAIR_TILE
    ends = jnp.cumsum(padded)
    starts = ends - padded
    tile_cls = jnp.minimum(
        jnp.sum(ends[None, :N_CLASSES] <= (jnp.arange(n_tiles, dtype=I32) * PAIR_TILE)[:, None], axis=1),
        N_CLASSES - 1).astype(I32)
    first = (tile_cls // PAIRS_PER_GROUP) * EXPERTS_PER_GROUP
    tile_ea = first + jnp.asarray(PAIR_LO, I32)[tile_cls % PAIRS_PER_GROUP]
    tile_eb = first + jnp.asarray(PAIR_HI, I32)[tile_cls % PAIRS_PER_GROUP]
    used = (ends[N_CLASSES - 1:N_CLASSES] // PAIR_TILE).astype(I32)

    pos, slot_tok = _sc_slots(idx[0], idx[1], starts, n_slots)
    hsa, hsb, gate_s = _sc_gather((ha, hb, gates), slot_tok, after=after)

    row = lambda i, ea, eb, nu: (jnp.minimum(i, nu[0] - 1), 0)
    w_spec = lambda shape, which: pl.BlockSpec(
        (1,) + shape, lambda i, ea, eb, nu: ((ea, eb)[which][i], 0, 0))
    gu_shape, d_shape = (D_MODEL, 2 * D_EXPERT), (D_EXPERT, D_MODEL)
    ysa, ysb = pl.pallas_call(
        _moe_pair_kernel,
        grid_spec=pltpu.PrefetchScalarGridSpec(
            num_scalar_prefetch=3, grid=(n_tiles,),
            in_specs=[pl.BlockSpec((PAIR_TILE, PACK_W), row), pl.BlockSpec((PAIR_TILE, PACK_W), row),
                      pl.BlockSpec((PAIR_TILE, LANES), row),
                      w_spec(gu_shape, 0), w_spec(d_shape, 0), w_spec(gu_shape, 1), w_spec(d_shape, 1),
                      _ANY_SPEC],
            out_specs=[pl.BlockSpec((PAIR_TILE, PACK_W), row), pl.BlockSpec((PAIR_TILE, PACK_W), row)]),
        out_shape=[jax.ShapeDtypeStruct((n_slots, PACK_W), I32),
                   jax.ShapeDtypeStruct((n_slots, PACK_W), I32)],
        compiler_params=pltpu.CompilerParams(
            dimension_semantics=("arbitrary",), vmem_limit_bytes=VMEM_LIMIT),
        name="moe_pair",
    )(tile_ea, tile_eb, used, hsa, hsb, gate_s, wgu, wd, wgu, wd, after)
    return (ysa, ysb), pos


def _expert_cast(w_gate, w_up, w_down, layer, after):
    per = CAST_EXPERTS
    w_spec = lambda shape: pl.BlockSpec((None, per) + shape, lambda e: (layer, e, 0, 0))
    out = lambda shape: pl.BlockSpec((per,) + shape, lambda e: (e, 0, 0))
    return pl.pallas_call(
        _expert_cast_kernel,
        grid=(N_EXPERTS // per,),
        in_specs=[w_spec((D_MODEL, D_EXPERT)), w_spec((D_MODEL, D_EXPERT)), w_spec((D_EXPERT, D_MODEL)),
                  _ANY_SPEC],
        out_specs=[out((D_MODEL, 2 * D_EXPERT)), out((D_EXPERT, D_MODEL))],
        out_shape=[jax.ShapeDtypeStruct((N_EXPERTS, D_MODEL, 2 * D_EXPERT), BF16),
                   jax.ShapeDtypeStruct((N_EXPERTS, D_EXPERT, D_MODEL), BF16)],
        compiler_params=pltpu.CompilerParams(
            dimension_semantics=("arbitrary",), vmem_limit_bytes=VMEM_LIMIT),
        name="expert_cast",
    )(w_gate, w_up, w_down, after)


def _final_norm(x, moe_y, g, rows):
    t = x.shape[0]
    rows = min(rows, t)
    assert t % rows == 0
    tok = lambda i: (i, 0)
    return pl.pallas_call(
        _final_norm_kernel,
        grid=(t // rows,),
        in_specs=[pl.BlockSpec((rows, D_MODEL), tok), pl.BlockSpec((rows, PACK_W), tok),
                  pl.BlockSpec((rows, PACK_W), tok), _const_spec((1, D_MODEL))],
        out_specs=pl.BlockSpec((rows, D_MODEL), tok),
        out_shape=jax.ShapeDtypeStruct(x.shape, F32),
        compiler_params=pltpu.CompilerParams(
            dimension_semantics=("arbitrary",), vmem_limit_bytes=VMEM_LIMIT),
        name="final_norm",
    )(x, *moe_y, g.reshape(1, D_MODEL))


def _moe_dense(x, norm_g, router, wgu, wd, norm_final, layer, final_norm, rows, after):
    t = x.shape[0]
    assert t % rows == 0
    tok = lambda i, e: (i, 0)
    w_spec = lambda shape: pl.BlockSpec((1,) + shape, lambda i, e: (e, 0, 0))
    return pl.pallas_call(
        functools.partial(_moe_dense_kernel, final_norm=final_norm),
        grid=(t // rows, N_EXPERTS),
        in_specs=[pl.BlockSpec((rows, D_MODEL), tok), _layer_spec((None, 1, D_MODEL), layer),
                  *_router_specs(layer),
                  w_spec((D_MODEL, 2 * D_EXPERT)), w_spec((D_EXPERT, D_MODEL)),
                  _const_spec((1, D_MODEL)), _ANY_SPEC],
        out_specs=pl.BlockSpec((rows, D_MODEL), tok),
        out_shape=jax.ShapeDtypeStruct(x.shape, F32),
        scratch_shapes=[pltpu.VMEM((rows, D_MODEL), BF16),
                        pltpu.VMEM((rows, LANES), F32),
                        pltpu.VMEM((rows, D_MODEL), F32)],
        compiler_params=pltpu.CompilerParams(
            dimension_semantics=("arbitrary", "arbitrary"), vmem_limit_bytes=VMEM_LIMIT),
        name="moe_dense",
    )(x, norm_g, *router, wgu, wd, norm_final.reshape(1, D_MODEL), after)


def kernel(x_prompt, x_sample, cache_pool, state_ret, norm_mix, w_in, w_pool, pool_scale, ret_gn, w_out, norm_ffn, w_router_group, b_router_group, w_router_expert, b_router_expert, w_gate, w_up, w_down, norm_final):
    depth = norm_mix.shape[0]
    row = lambda a: a.reshape(depth, 1, a.shape[-1])
    mix_split = (row(norm_mix), *_split_weight(w_in), *_split_weight(w_pool), row(pool_scale),
                 row(ret_gn), *_split_weight(w_out))
    mix_hi = tuple(mix_split[i] for i in (0, 1, 3, 5, 6, 7))
    router = _router_weights(w_router_group, b_router_group, w_router_expert, b_router_expert)
    norm_ffn = row(norm_ffn)
    pool_prev = jnp.swapaxes(cache_pool, 1, 2)

    yp, ys = x_prompt, x_sample
    moe_p = None
    out_p = out_s = None
    for l in range(depth):
        yp, out_p, routed = _mixer_prompt(
            yp, moe_p, mix_split, (norm_ffn, *router), out_p, l, rows=512,
            precise_tail=PRECISE_TAIL_STEPS if l < depth - 1 else 0, after=ys)
        ys, out_s = _mixer_sample(ys, pool_prev, state_ret, mix_hi, out_s, l, after=routed[-1])
        wgu, wd = _expert_cast(w_gate, w_up, w_down, l, after=ys)
        sorted_y, pos = _moe_experts(routed, wgu, wd, after=ys)
        ys = _moe_dense(ys.reshape(-1, D_MODEL), norm_ffn, router, wgu, wd, norm_final,
                        l, l == depth - 1, rows=1024, after=sorted_y[0]).reshape(ys.shape)
        moe_p = _sc_gather(sorted_y, pos, after=routed[-1])
    yp = _final_norm(yp.reshape(-1, D_MODEL), moe_p, norm_final, rows=2048).reshape(yp.shape)
    return (yp, ys, *out_p, jnp.swapaxes(out_s[0], 1, 2), out_s[1])
```

```python
import dataclasses
import functools
import itertools

import jax
import jax.numpy as jnp
import numpy as np
from jax import lax
from jax.experimental import pallas as pl
from jax.experimental.pallas import tpu as pltpu
from jax.experimental.pallas import tpu_sc as plsc

F32 = jnp.float32
BF16 = jnp.bfloat16
I32 = jnp.int32

D_MODEL = 1024
POOL_WIDTH = 512
POOL_WINDOWS = (2, 4, 8, 16)
POOL_GROUP_DIM = 128
POOL_BUF = 15
POOL_HIST = 16
RET_WIDTH = 512
RET_HEADS = 4
RET_HEAD_DIM = 128
RET_CHUNK = 128
ROPE_BASE = 10000.0
IN_WIDTH = POOL_WIDTH + 4 * RET_WIDTH
N_EXPERT_GROUPS = 4
EXPERTS_PER_GROUP = 4
N_EXPERTS = N_EXPERT_GROUPS * EXPERTS_PER_GROUP
D_EXPERT = 256
RMS_EPS = 1e-6
GN_EPS = 1e-5
PAST_LEN = 16384

LANES = 128
SUBLANES = 8
EXPERT_LANE0 = 8
GROUP_SHIFT = EXPERTS_PER_GROUP.bit_length() - 1
PAIRS = tuple(itertools.combinations(range(EXPERTS_PER_GROUP), 2))
PAIRS_PER_GROUP = len(PAIRS)
PAIR_LO, PAIR_HI = zip(*PAIRS)
N_CLASSES = N_EXPERT_GROUPS * PAIRS_PER_GROUP
CLASS_ROWS = 32
GATE_LO, GATE_HI = 0, 1
COUNT_BLOCK = 256
PAIR_TILE = 256
CAST_EXPERTS = 2
PACK_W = D_MODEL // 4
SC_CORES, SC_SUBCORES, SC_LANES = 2, 16, 16
SC_WINDOW = 128
SC_UNROLL = 8
PRECISE_TAIL_STEPS = 1
CHUNK = 128
SAMPLE_SEQS = 16
VMEM_LIMIT = 56 * 1024 * 1024


def _dot(a, b):
    return jnp.dot(a, b, preferred_element_type=F32)


def _dot_nt(a, b):
    return lax.dot_general(a, b, (((1,), (1,)), ((), ())), preferred_element_type=F32)


def _bf16_round_bits(u):
    return (u + jnp.uint32(0x7FFF) + ((u >> 16) & jnp.uint32(1))) & jnp.uint32(0xFFFF0000)


def _split(a):
    hi = pltpu.bitcast(_bf16_round_bits(pltpu.bitcast(a, jnp.uint32)), F32)
    return hi.astype(BF16), (a - hi).astype(BF16)


def _mm(a, b, precise, nt=False):
    dot = _dot_nt if nt else _dot
    if precise:
        b_hi, b_lo = b if isinstance(b, tuple) else _split(b)
        a_hi, a_lo = _split(a)
        return dot(a_hi, b_hi) + dot(a_lo, b_hi) + dot(a_hi, b_lo)
    return dot(a.astype(BF16), b[0] if isinstance(b, tuple) else b.astype(BF16))


def _rmsnorm(x, g):
    ms = jnp.mean(x * x, axis=-1, keepdims=True)
    return x * lax.rsqrt(ms + RMS_EPS) * g


def _pool_mix(ubuf, rows, t_first, n_prev, wpool_refs, pscale, precise=False, row0=0):
    ns = ubuf.shape[0]
    t = t_first + lax.broadcasted_iota(I32, (1, rows, POOL_GROUP_DIM), 1)
    base = POOL_HIST + row0
    outs = []
    for j, w in enumerate(POOL_WINDOWS):
        lanes = slice(j * POOL_GROUP_DIM, (j + 1) * POOL_GROUP_DIM)
        uj = ubuf[:, base:base + rows, lanes]
        acc = uj
        for i in range(1, w):
            acc = acc + ubuf[:, base - i:base - i + rows, lanes]
        cnt = jnp.minimum(w, n_prev + t + 1).astype(F32)
        d = (acc / cnt - uj).reshape(ns * rows, POOL_GROUP_DIM)
        outs.append(_mm(d, tuple(w[j] for w in wpool_refs), precise))
    return jnp.concatenate(outs, axis=-1) * pscale


def _rope(xh, cos2, sin2):
    return xh * cos2 + pltpu.roll(xh, RET_HEAD_DIM // 2, 1) * sin2


def _group_norm(o):
    mu = jnp.mean(o, axis=-1, keepdims=True)
    c = o - mu
    var = jnp.mean(c * c, axis=-1, keepdims=True)
    return c * lax.rsqrt(var + GN_EPS)


def _silu(x):
    return x * (1.0 / (1.0 + jnp.exp(-x)))


def _head(a, h):
    return a[:, h * RET_HEAD_DIM:(h + 1) * RET_HEAD_DIM]


def _qkvg(z):
    p, r = POOL_WIDTH, RET_WIDTH
    return z[:, p:p + r], z[:, p + r:p + 2 * r], z[:, p + 2 * r:p + 3 * r], z[:, p + 3 * r:p + 4 * r]


def _pack_bf16_pair(a, b):
    ua = pltpu.bitcast(a.astype(BF16).astype(F32), jnp.uint32)
    ub = pltpu.bitcast(b.astype(BF16).astype(F32), jnp.uint32)
    return pltpu.bitcast((ua >> 16) | (ub & jnp.uint32(0xFFFF0000)), I32)


def _unpack_bf16_pair(w):
    u = pltpu.bitcast(w, jnp.uint32)
    return pltpu.bitcast(u << 16, F32), pltpu.bitcast(u & jnp.uint32(0xFFFF0000), F32)


def _pack_row(y):
    q = PACK_W
    return _pack_bf16_pair(y[:, 0:q], y[:, q:2 * q]), _pack_bf16_pair(y[:, 2 * q:3 * q], y[:, 3 * q:])


def _unpack_row(wa, wb):
    return jnp.concatenate([*_unpack_bf16_pair(wa), *_unpack_bf16_pair(wb)], axis=-1)


def _zero_other_layers(ref, layer):
    for j in range(ref.shape[0]):
        if j != layer:
            ref[j] = jnp.zeros(ref.shape[1:], ref.dtype)


def _mixer_prompt_kernel(*refs, rows, steps, moe_in, precise_tail, layer, has_prev):
    (x_ref, ya_ref, yb_ref, cos_ref, sin_ref, ng_ref, win_hi_ref, win_lo_ref, wpool_hi_ref,
     wpool_lo_ref, pscale_ref, gn_ref, wout_hi_ref, wout_lo_ref, dmat_ref, qdec_ref, kdec_ref,
     cdec_ref) = refs[:18]
    route_in = refs[18:22]
    y_ref, tail_ref, sfin_ref, *route_out, ubuf, s_scr, ret_scr, z_scr, count_scr = (
        refs[23 + 2 * has_prev:])
    c = pl.program_id(1)

    @pl.when((pl.program_id(0) == 0) & (c == 0))
    def _():
        count_scr[...] = jnp.zeros_like(count_scr)
    if not has_prev:
        _zero_other_layers(tail_ref, layer)
        _zero_other_layers(sfin_ref, layer)
        tail_ref, sfin_ref = tail_ref.at[layer], sfin_ref.at[layer]

    @pl.when(c == 0)
    def _():
        ubuf[:, 0:POOL_HIST, :] = jnp.zeros((1, POOL_HIST, POOL_WIDTH), F32)
        s_scr[...] = jnp.zeros_like(s_scr)

    kv_cols = slice(POOL_WIDTH + RET_WIDTH, POOL_WIDTH + 3 * RET_WIDTH)

    def step(kv_precise, full_from):
        x = x_ref[0]
        if moe_in:
            x = x + _unpack_row(ya_ref[0], yb_ref[0])
        hn = _rmsnorm(x, ng_ref[...])
        hi, lo = _split(hn) if kv_precise else (hn.astype(BF16), None)
        z_scr[...] = _dot(hi, win_hi_ref[...])
        if kv_precise and full_from:
            z_scr[:full_from, kv_cols] += (_dot(lo[:full_from], win_hi_ref[:, kv_cols])
                                           + _dot(hi[:full_from], win_lo_ref[:, kv_cols]))
        if full_from < rows:
            z_scr[full_from:, :] += (_dot(lo[full_from:], win_hi_ref[...])
                                     + _dot(hi[full_from:], win_lo_ref[...]))

        ubuf[0, POOL_HIST:POOL_HIST + rows, :] = z_scr[:, :POOL_WIDTH]
        pool_w = (wpool_hi_ref, wpool_lo_ref)
        pool_parts = []
        if full_from:
            pool_parts.append(_pool_mix(ubuf, full_from, c * rows, 0, pool_w, pscale_ref[...]))
        if full_from < rows:
            pool_parts.append(_pool_mix(ubuf, rows - full_from, c * rows + full_from, 0, pool_w,
                                        pscale_ref[...], precise=True, row0=full_from))
        pool_out = jnp.concatenate(pool_parts, axis=0)
        tail_ref[...] = ubuf[:, rows + POOL_HIST - POOL_BUF:rows + POOL_HIST, :]
        ubuf[:, 0:POOL_HIST, :] = ubuf[:, rows:rows + POOL_HIST, :]

        scale = RET_HEAD_DIM ** -0.5
        for ci in range(rows // CHUNK):
            rs = slice(ci * CHUNK, (ci + 1) * CHUNK)
            full = ci * CHUNK >= full_from
            cos2 = cos_ref[rs, :]
            sin2 = sin_ref[rs, :]
            for h in range(RET_HEADS):
                col = lambda part: slice(POOL_WIDTH + part * RET_WIDTH + h * RET_HEAD_DIM,
                                         POOL_WIDTH + part * RET_WIDTH + (h + 1) * RET_HEAD_DIM)
                q = _rope(z_scr[rs, col(0)], cos2, sin2)
                k = _rope(z_scr[rs, col(1)], cos2, sin2) * scale
                v = z_scr[rs, col(2)]
                s_old = s_scr[h]
                scores = _mm(q, k, full, nt=True) * dmat_ref[h]
                qd = q * _head(qdec_ref[...], h)
                o = _mm(scores, v, full) + _mm(qd, s_old, full)
                kd = k * _head(kdec_ref[...], h)
                s_scr[h] = s_old * cdec_ref[h] + _mm(kd.T, v, kv_precise)
                on = _group_norm(o) * _head(gn_ref[...], h)
                ret_scr[rs, h * RET_HEAD_DIM:(h + 1) * RET_HEAD_DIM] = _silu(z_scr[rs, col(3)]) * on

        mix_in = jnp.concatenate([pool_out, ret_scr[...]], axis=-1)
        y_ref[0] = x + _dot(mix_in.astype(BF16), wout_hi_ref[...])
        if full_from < rows:
            m_hi, m_lo = _split(mix_in[full_from:])
            y_ref[0, full_from:, :] += _dot(m_lo, wout_hi_ref[...]) + _dot(m_hi, wout_lo_ref[...])
        sfin_ref[0] = s_scr[...]
        _route_tile(y_ref[0], full_from, *route_in, *route_out, count_scr)

    if precise_tail:
        pl.when(c < steps - precise_tail)(lambda: step(False, rows))
        if precise_tail > 1:
            pl.when((c >= steps - precise_tail) & (c < steps - 1))(lambda: step(True, rows))
        pl.when(c == steps - 1)(lambda: step(True, rows - CHUNK))
    else:
        step(False, rows)


def _mixer_sample_kernel(*refs, seq_len, layer, has_prev):
    (x_ref, prev_ref, s0_ref, cos_ref, sin_ref, ng_ref, win_ref, wpool_ref, pscale_ref, gn_ref,
     wout_ref, dmat_ref, qdec_ref, kdec_ref, cdec_ref) = refs[:15]
    y_ref, tail_ref, sfin_ref, ubuf, ret_scr = refs[16 + 2 * has_prev:]
    if not has_prev:
        _zero_other_layers(tail_ref, layer)
        _zero_other_layers(sfin_ref, layer)
        tail_ref, sfin_ref = tail_ref.at[layer], sfin_ref.at[layer]
    ns = SAMPLE_SEQS
    rows = ns * seq_len
    x = x_ref[...].reshape(rows, D_MODEL)
    hn = _rmsnorm(x, ng_ref[...]).astype(BF16)
    z = _dot(hn, win_ref[...])
    for j in range(POOL_BUF):
        ubuf[:, POOL_HIST - POOL_BUF + j, :] = prev_ref[j]
    ubuf[:, POOL_HIST:POOL_HIST + seq_len, :] = z[:, :POOL_WIDTH].reshape(ns, seq_len, POOL_WIDTH)
    pool_out = _pool_mix(ubuf, seq_len, 0, POOL_BUF, (wpool_ref,), pscale_ref[...])
    for j in range(POOL_BUF):
        tail_ref[j] = ubuf[:, seq_len + POOL_HIST - POOL_BUF + j, :]

    q_all, k_all, v_all, g_all = _qkvg(z)
    scale = RET_HEAD_DIM ** -0.5
    cos2 = cos_ref[...]
    sin2 = sin_ref[...]
    tok_seq = lax.broadcasted_iota(I32, (RET_HEAD_DIM, rows), 1) // seq_len
    for h in range(RET_HEADS):
        q = _rope(_head(q_all, h), cos2, sin2)
        k = _rope(_head(k_all, h), cos2, sin2) * scale
        vb = _head(v_all, h).astype(BF16)
        s_old = s0_ref[:, h]
        scores = _dot_nt(q.astype(BF16), k.astype(BF16)) * dmat_ref[h]
        qd = (q * _head(qdec_ref[...], h)).astype(BF16).reshape(ns, seq_len, RET_HEAD_DIM)
        o_state = jnp.einsum('bid,bde->bie', qd, s_old.astype(BF16), preferred_element_type=F32)
        o = _dot(scores.astype(BF16), vb) + o_state.reshape(rows, RET_HEAD_DIM)
        kdt = (k * _head(kdec_ref[...], h)).T
        lhs = jnp.concatenate(
            [jnp.where(tok_seq == b, kdt, 0.0).astype(BF16) for b in range(ns)], axis=0)
        upd = _dot(lhs, vb).reshape(ns, RET_HEAD_DIM, RET_HEAD_DIM)
        sfin_ref[:, h] = s_old * cdec_ref[h] + upd
        on = _group_norm(o) * _head(gn_ref[...], h)
        ret_scr[:, h * RET_HEAD_DIM:(h + 1) * RET_HEAD_DIM] = _silu(_head(g_all, h)) * on

    mix_in = jnp.concatenate([pool_out, ret_scr[...]], axis=-1).astype(BF16)
    y_ref[...] = (x + _dot(mix_in, wout_ref[...])).reshape(ns, seq_len, D_MODEL)


def _router_logits(hn, wr_cat_ref, wr_hi_ref, br_ref, precise_from=0):
    parts = []
    if precise_from:
        parts.append(_dot(hn[:precise_from].astype(BF16), wr_hi_ref[...]))
    if precise_from < hn.shape[0]:
        h_hi, h_lo = _split(hn[precise_from:])
        part = _dot(h_hi, wr_cat_ref[...])
        parts.append(part[:, :LANES] + part[:, LANES:] + _dot(h_lo, wr_hi_ref[...]))
    return jnp.concatenate(parts, axis=0) + br_ref[...]


def _select(group_lg, expert_lg, gidx, eidx, axis):
    red = dict(axis=axis, keepdims=True)
    neg = jnp.float32(-jnp.inf)
    gmax = jnp.max(group_lg, **red)
    g_sel = jnp.min(jnp.where(group_lg == gmax, gidx, N_EXPERT_GROUPS), **red)
    p_sel = 1.0 / jnp.sum(jnp.exp(group_lg - gmax), **red)
    emask = (eidx >> GROUP_SHIFT) == g_sel
    v1 = jnp.max(jnp.where(emask, expert_lg, neg), **red)
    i1 = jnp.min(jnp.where(emask & (expert_lg == v1), eidx, N_EXPERTS), **red)
    emask2 = emask & (eidx != i1)
    v2 = jnp.max(jnp.where(emask2, expert_lg, neg), **red)
    i2 = jnp.min(jnp.where(emask2 & (expert_lg == v2), eidx, N_EXPERTS), **red)
    e2 = jnp.exp(v2 - v1)
    return g_sel, i1, i2, p_sel / (1.0 + e2), p_sel * e2 / (1.0 + e2)


def _route_tile(x, precise_from, ng_ref, wr_cat_ref, wr_hi_ref, br_ref, ha_ref, hb_ref, gate_ref,
                idx_ref, count_ref, carry_scr):
    rows = x.shape[0]
    hn = _rmsnorm(x, ng_ref[...])
    ha_ref[...], hb_ref[...] = _pack_row(hn)
    lgt = _router_logits(hn, wr_cat_ref, wr_hi_ref, br_ref, precise_from).T
    neg = jnp.float32(-jnp.inf)
    gidx = lax.broadcasted_iota(I32, (SUBLANES, rows), 0)
    eidx = lax.broadcasted_iota(I32, (N_EXPERTS, rows), 0)
    group_lg = jnp.where(gidx < N_EXPERT_GROUPS, lgt[0:SUBLANES], neg)
    expert_lg = lgt[EXPERT_LANE0:EXPERT_LANE0 + N_EXPERTS]
    g_sel, i1, i2, w1, w2 = _select(group_lg, expert_lg, gidx, eidx, 0)
    lo = jnp.minimum(i1, i2) - g_sel * EXPERTS_PER_GROUP
    hi = jnp.maximum(i1, i2) - g_sel * EXPERTS_PER_GROUP
    pair = ((lo * (2 * EXPERTS_PER_GROUP - 1 - lo)) >> 1) + (hi - lo - 1)
    cls = g_sel * PAIRS_PER_GROUP + pair
    w_lo = jnp.where(i1 < i2, w1, w2)
    w_hi = jnp.where(i1 < i2, w2, w1)

    crow = lax.broadcasted_iota(I32, (CLASS_ROWS, rows), 0)
    onehot = jnp.where(crow == cls, 1.0, 0.0)
    n_blk = rows // COUNT_BLOCK
    blocks = [onehot[:, j * COUNT_BLOCK:(j + 1) * COUNT_BLOCK] for j in range(n_blk)]
    r = lax.broadcasted_iota(I32, (COUNT_BLOCK, COUNT_BLOCK), 0)
    c = lax.broadcasted_iota(I32, (COUNT_BLOCK, COUNT_BLOCK), 1)
    upper = jnp.where(r < c, 1.0, 0.0).astype(BF16)
    within = _dot(jnp.concatenate(blocks, axis=0).astype(BF16), upper)
    carry = carry_scr[:, 0:1]
    ranks = []
    for j in range(n_blk):
        before = within[j * CLASS_ROWS:(j + 1) * CLASS_ROWS] + carry
        ranks.append(jnp.sum(blocks[j] * before, axis=0, keepdims=True))
        carry = carry + jnp.sum(blocks[j], axis=1, keepdims=True)
    rank = jnp.concatenate(ranks, axis=1)
    carry_scr[...] = jnp.broadcast_to(carry, carry_scr.shape)
    count_ref[...] = carry_scr[...]

    row8 = lax.broadcasted_iota(I32, (SUBLANES, rows), 0)
    idx_ref[...] = jnp.where(row8 == 0, cls, jnp.where(row8 == 1, rank.astype(I32), 0))
    rowl = lax.broadcasted_iota(I32, (LANES, rows), 0)
    gate_ref[...] = jnp.where(rowl == GATE_LO, w_lo, jnp.where(rowl == GATE_HI, w_hi, 0.0)).T


def _expert_cast_kernel(wg_ref, wu_ref, wd_ref, after_ref, wgu_ref, wdb_ref):
    del after_ref
    wgu_ref[:, :, :D_EXPERT] = wg_ref[...].astype(BF16)
    wgu_ref[:, :, D_EXPERT:] = wu_ref[...].astype(BF16)
    wdb_ref[...] = wd_ref[...].astype(BF16)


def _moe_pair_kernel(ea_ref, eb_ref, used_ref, ha_ref, hb_ref, gate_ref, wgu_a_ref, wd_a_ref,
                     wgu_b_ref, wd_b_ref, after_ref, ya_ref, yb_ref):
    del ea_ref, eb_ref, after_ref

    @pl.when(pl.program_id(0) < used_ref[0])
    def _():
        h = _unpack_row(ha_ref[...], hb_ref[...]).astype(BF16)
        gates = gate_ref[...]
        y = None
        for wgu_ref, wd_ref, lane in ((wgu_a_ref, wd_a_ref, GATE_LO), (wgu_b_ref, wd_b_ref, GATE_HI)):
            gu = _dot(h, wgu_ref[0])
            act = _silu(gu[:, :D_EXPERT]) * gu[:, D_EXPERT:]
            part = _dot((act * gates[:, lane:lane + 1]).astype(BF16), wd_ref[0])
            y = part if y is None else y + part
        ya_ref[...], yb_ref[...] = _pack_row(y)


def _final_norm_kernel(x_ref, ya_ref, yb_ref, g_ref, o_ref):
    o_ref[...] = _rmsnorm(x_ref[...] + _unpack_row(ya_ref[...], yb_ref[...]), g_ref[...])


def _moe_dense_kernel(x_ref, ng_ref, wr_cat_ref, wr_hi_ref, br_ref, wgu_ref, wd_ref, nf_ref,
                      after_ref, y_ref, h_scr, gate_scr, acc_scr, *, final_norm):
    del after_ref
    e = pl.program_id(1)

    @pl.when(e == 0)
    def _():
        hn = _rmsnorm(x_ref[...], ng_ref[...])
        h_scr[...] = hn.astype(BF16)
        lg = _router_logits(hn, wr_cat_ref, wr_hi_ref, br_ref)
        lane = lax.broadcasted_iota(I32, lg.shape, 1)
        neg = jnp.float32(-jnp.inf)
        group_lg = jnp.where(lane < N_EXPERT_GROUPS, lg, neg)
        is_expert = (lane >= EXPERT_LANE0) & (lane < EXPERT_LANE0 + N_EXPERTS)
        _, i1, i2, w1, w2 = _select(group_lg, jnp.where(is_expert, lg, neg), lane,
                                    jnp.where(is_expert, lane - EXPERT_LANE0, N_EXPERTS), 1)
        gate_scr[...] = (jnp.where(lane == i1 + EXPERT_LANE0, w1, 0.0)
                         + jnp.where(lane == i2 + EXPERT_LANE0, w2, 0.0))
        acc_scr[...] = jnp.zeros_like(acc_scr)

    lane = lax.broadcasted_iota(I32, gate_scr.shape, 1)
    gate = jnp.sum(jnp.where(lane == EXPERT_LANE0 + e, gate_scr[...], 0.0), axis=-1, keepdims=True)
    gu = _dot(h_scr[...], wgu_ref[0])
    act = _silu(gu[:, :D_EXPERT]) * gu[:, D_EXPERT:]
    acc_scr[...] += _dot((act * gate).astype(BF16), wd_ref[0])

    @pl.when(e == N_EXPERTS - 1)
    def _():
        y = x_ref[...] + acc_scr[...]
        if final_norm:
            y = _rmsnorm(y, nf_ref[...])
        y_ref[...] = y


def _const_spec(shape):
    return pl.BlockSpec(shape, lambda *_: (0,) * len(shape))


def _resident_spec(shape):
    return pl.BlockSpec(shape, lambda *_: (0,) * len(shape), pipeline_mode=pl.Buffered(1))


_ANY_SPEC = pl.BlockSpec(memory_space=pl.ANY)


def _rope_tables(pos):
    half = RET_HEAD_DIM // 2
    inv = np.float32(ROPE_BASE) ** (-np.arange(half, dtype=np.float32) / np.float32(half))
    ang = pos.astype(np.float32)[:, None] * inv[None, :]
    cos, sin = np.cos(ang), np.sin(ang)
    return np.concatenate([cos, cos], -1), np.concatenate([-sin, sin], -1)


def _decay_consts(c, reps):
    f = np.float32
    log_g = np.log(f(1.0) - f(2.0) ** (f(-5.0) - np.arange(RET_HEADS, dtype=f)))
    i = np.arange(c, dtype=f)
    diff = i[:, None] - i[None, :]
    dmat = np.where(diff[None] >= 0, np.exp(np.maximum(diff, f(0))[None] * log_g[:, None, None]), f(0))
    qdec = np.exp((i + f(1))[None, :] * log_g[:, None])
    kdec = np.exp((f(c) - f(1) - i)[None, :] * log_g[:, None])
    cdec = np.exp(f(c) * log_g)
    dmat = np.einsum('ab,hij->haibj', np.eye(reps, dtype=f), dmat).reshape(RET_HEADS, reps * c, reps * c)
    lanes = lambda t: np.repeat(np.tile(t, (1, reps)).T, RET_HEAD_DIM, axis=1)
    cdec = np.broadcast_to(cdec[:, None, None], (RET_HEADS, 1, RET_HEAD_DIM))
    return tuple(np.ascontiguousarray(a, dtype=f) for a in (dmat, lanes(qdec), lanes(kdec), cdec))


def _split_weight(w):
    hi = lax.bitcast_convert_type(_bf16_round_bits(lax.bitcast_convert_type(w, jnp.uint32)), F32)
    return hi.astype(BF16), (w - hi).astype(BF16)


_W_IN_BLOCK = (None, D_MODEL, IN_WIDTH)
_W_POOL_BLOCK = (None, len(POOL_WINDOWS), POOL_GROUP_DIM, POOL_GROUP_DIM)
_W_OUT_BLOCK = (None, D_MODEL, D_MODEL)
_DECAY_SPECS = [
    _const_spec((RET_HEADS, CHUNK, CHUNK)), _const_spec((CHUNK, RET_WIDTH)),
    _const_spec((CHUNK, RET_WIDTH)), _const_spec((RET_HEADS, 1, RET_HEAD_DIM)),
]


def _layer_spec(block, layer, resident=False):
    index_map = lambda *_: (layer,) + (0,) * (len(block) - 1)
    if resident:
        return pl.BlockSpec(block, index_map, pipeline_mode=pl.Buffered(1))
    return pl.BlockSpec(block, index_map)


def _mixer_weight_specs(layer, split):
    n = 2 if split else 1
    row = lambda width: _layer_spec((None, 1, width), layer)
    return [row(D_MODEL), *[_layer_spec(_W_IN_BLOCK, layer, True)] * n,
            *[_layer_spec(_W_POOL_BLOCK, layer)] * n, row(POOL_WIDTH), row(RET_WIDTH),
            *[_layer_spec(_W_OUT_BLOCK, layer, True)] * n]


def _chain_out_spec(block, depth, layer, has_prev):
    zeros = (0,) * (len(block) - 1)
    if has_prev:
        return pl.BlockSpec((None,) + block, lambda i, *_: (layer, i) + zeros)
    return pl.BlockSpec((depth,) + block, lambda i, *_: (0, i) + zeros)


def _mixer_prompt(x, moe_y, weights, route_w, prev_out, layer, rows, precise_tail, after):
    b, l, _ = x.shape
    t = b * l
    depth = weights[0].shape[0]
    assert l % rows == 0 and rows % CHUNK == 0
    cos2, sin2 = _rope_tables(np.arange(l))
    decay = _decay_consts(RET_CHUNK, 1)
    steps = l // rows
    tok = lambda i, c: (i, c, 0)
    if moe_y is None:
        ya = yb = jnp.zeros((1, rows, PACK_W), I32)
        y_spec = _const_spec((1, rows, PACK_W))
    else:
        ya, yb = (t.reshape(b, l, PACK_W) for t in moe_y)
        y_spec = pl.BlockSpec((1, rows, PACK_W), tok)
    has_prev = prev_out is not None
    n_in = 23
    flat = lambda i, c: (i * steps + c, 0)
    y, tails, states, *routed = pl.pallas_call(
        functools.partial(_mixer_prompt_kernel, rows=rows, steps=steps, moe_in=moe_y is not None,
                          precise_tail=precise_tail, layer=layer, has_prev=has_prev),
        grid=(b, steps),
        in_specs=[pl.BlockSpec((1, rows, D_MODEL), tok), y_spec, y_spec,
                  pl.BlockSpec((rows, RET_HEAD_DIM), lambda i, c: (c, 0)),
                  pl.BlockSpec((rows, RET_HEAD_DIM), lambda i, c: (c, 0)),
                  *_mixer_weight_specs(layer, True), *_DECAY_SPECS,
                  _layer_spec((None, 1, D_MODEL), layer), *_router_specs(layer), _ANY_SPEC,
                  *([_ANY_SPEC, _ANY_SPEC] if has_prev else [])],
        out_specs=[pl.BlockSpec((1, rows, D_MODEL), tok),
                   _chain_out_spec((1, POOL_BUF, POOL_WIDTH), depth, layer, has_prev),
                   _chain_out_spec((1, RET_HEADS, RET_HEAD_DIM, RET_HEAD_DIM), depth, layer, has_prev),
                   pl.BlockSpec((rows, PACK_W), flat), pl.BlockSpec((rows, PACK_W), flat),
                   pl.BlockSpec((rows, LANES), flat),
                   pl.BlockSpec((SUBLANES, rows), lambda i, c: (0, i * steps + c)),
                   _const_spec((CLASS_ROWS, LANES))],
        out_shape=[jax.ShapeDtypeStruct(x.shape, F32),
                   jax.ShapeDtypeStruct((depth, b, POOL_BUF, POOL_WIDTH), F32),
                   jax.ShapeDtypeStruct((depth, b, RET_HEADS, RET_HEAD_DIM, RET_HEAD_DIM), F32),
                   jax.ShapeDtypeStruct((t, PACK_W), I32), jax.ShapeDtypeStruct((t, PACK_W), I32),
                   jax.ShapeDtypeStruct((t, LANES), F32), jax.ShapeDtypeStruct((SUBLANES, t), I32),
                   jax.ShapeDtypeStruct((CLASS_ROWS, LANES), F32)],
        input_output_aliases={n_in: 1, n_in + 1: 2} if has_prev else {},
        scratch_shapes=[pltpu.VMEM((1, POOL_HIST + rows, POOL_WIDTH), F32),
                        pltpu.VMEM((RET_HEADS, RET_HEAD_DIM, RET_HEAD_DIM), F32),
                        pltpu.VMEM((rows, RET_WIDTH), F32),
                        pltpu.VMEM((rows, IN_WIDTH), F32),
                        pltpu.VMEM((CLASS_ROWS, LANES), F32)],
        compiler_params=pltpu.CompilerParams(
            dimension_semantics=("arbitrary", "arbitrary"), vmem_limit_bytes=VMEM_LIMIT),
        name="mixer_prompt",
    )(x, ya, yb, cos2, sin2, *weights, *decay, *route_w, after, *(prev_out if has_prev else ()))
    return y, (tails, states), tuple(routed)


def _mixer_sample(x, pool_prev, s0, weights, prev_out, layer, after):
    b, l, _ = x.shape
    depth = s0.shape[0]
    ns = SAMPLE_SEQS
    assert ns * l == CHUNK and b % ns == 0
    cos2, sin2 = _rope_tables(PAST_LEN + np.arange(l))
    cos2, sin2 = np.tile(cos2, (ns, 1)), np.tile(sin2, (ns, 1))
    decay = _decay_consts(l, ns)
    seq3 = lambda i: (i, 0, 0)
    state_block = (ns, RET_HEADS, RET_HEAD_DIM, RET_HEAD_DIM)
    tail_block = (POOL_BUF, ns, POOL_WIDTH)
    if prev_out is None:
        tail_out = pl.BlockSpec((depth,) + tail_block, lambda i: (0, 0, i, 0))
    else:
        tail_out = pl.BlockSpec((None,) + tail_block, lambda i: (layer, 0, i, 0))
    has_prev = prev_out is not None
    n_in = 16
    y, tails, states = pl.pallas_call(
        functools.partial(_mixer_sample_kernel, seq_len=l, layer=layer, has_prev=has_prev),
        grid=(b // ns,),
        in_specs=[pl.BlockSpec((ns, l, D_MODEL), seq3),
                  pl.BlockSpec((None,) + tail_block, lambda i: (layer, 0, i, 0)),
                  pl.BlockSpec((None,) + state_block, lambda i: (layer, i, 0, 0, 0)),
                  _const_spec((CHUNK, RET_HEAD_DIM)), _const_spec((CHUNK, RET_HEAD_DIM)),
                  *_mixer_weight_specs(layer, False), *_DECAY_SPECS, _ANY_SPEC,
                  *([_ANY_SPEC, _ANY_SPEC] if has_prev else [])],
        out_specs=[pl.BlockSpec((ns, l, D_MODEL), seq3), tail_out,
                   _chain_out_spec(state_block, depth, layer, has_prev)],
        out_shape=[jax.ShapeDtypeStruct(x.shape, F32),
                   jax.ShapeDtypeStruct((depth, POOL_BUF, b, POOL_WIDTH), F32),
                   jax.ShapeDtypeStruct(s0.shape, F32)],
        input_output_aliases={n_in: 1, n_in + 1: 2} if has_prev else {},
        scratch_shapes=[pltpu.VMEM((ns, POOL_HIST + l, POOL_WIDTH), F32),
                        pltpu.VMEM((CHUNK, RET_WIDTH), F32)],
        compiler_params=pltpu.CompilerParams(
            dimension_semantics=("arbitrary",), vmem_limit_bytes=VMEM_LIMIT),
        name="mixer_sample",
    )(x, pool_prev, s0, cos2, sin2, *weights, *decay, after, *(prev_out if has_prev else ()))
    return y, (tails, states)


def _router_weights(w_rg, b_rg, w_re, b_re):
    depth = w_rg.shape[0]
    gap = EXPERT_LANE0 - N_EXPERT_GROUPS
    rest = LANES - EXPERT_LANE0 - N_EXPERTS
    wr = jnp.concatenate([w_rg, jnp.zeros((depth, D_MODEL, gap), F32), w_re,
                          jnp.zeros((depth, D_MODEL, rest), F32)], axis=-1)
    br = jnp.concatenate([b_rg, jnp.zeros((depth, gap), F32), b_re, jnp.zeros((depth, rest), F32)],
                         axis=-1).reshape(depth, 1, LANES)
    wr_hi, wr_lo = _split_weight(wr)
    return jnp.concatenate([wr_hi, wr_lo], axis=-1), wr_hi, br


def _router_specs(layer):
    return [_layer_spec((None, D_MODEL, 2 * LANES), layer), _layer_spec((None, D_MODEL, LANES), layer),
            _layer_spec((None, 1, LANES), layer)]


def _sc_mesh():
    return plsc.VectorSubcoreMesh(core_axis_name="core", subcore_axis_name="subcore",
                                  num_cores=SC_CORES, num_subcores=SC_SUBCORES)


def _sc_params():
    params = pltpu.CompilerParams()
    if "needs_layout_passes" in pltpu.CompilerParams.__dataclass_fields__:
        params = dataclasses.replace(params, needs_layout_passes=False)
    return params


def _sc_gather(tables, idx, after):
    n = idx.shape[0]
    assert n % SC_WINDOW == 0
    nt = len(tables)

    def body(*refs):
        i_hbm = refs[nt]
        for t_hbm, o_hbm in zip(refs[:nt], refs[nt + 2:]):
            def gather_window(i_vmem, o_vmem, t_hbm=t_hbm):
                pltpu.sync_copy(t_hbm.at[i_vmem.at[0]], o_vmem)

            pltpu.emit_pipeline(
                gather_window, grid=(n // SC_WINDOW,),
                in_specs=[pl.BlockSpec((1, SC_WINDOW), lambda i: (0, i))],
                out_specs=[pl.BlockSpec((SC_WINDOW, t_hbm.shape[1]), lambda i: (i, 0))],
                core_axis_name=("core", "subcore"),
                dimension_semantics=(pltpu.PARALLEL,),
            )(i_hbm, o_hbm)

    out_type = tuple(jax.ShapeDtypeStruct((n, t.shape[1]), t.dtype) for t in tables)
    return pl.kernel(body, out_type=out_type, mesh=_sc_mesh(), name="sc_gather")(
        *tables, idx.reshape(1, n), after)


def _sc_slots(cls, rank, starts, n_slots):
    t = cls.shape[0]
    workers = SC_CORES * SC_SUBCORES
    slot_per, tok_per = n_slots // workers, t // workers
    assert n_slots % (workers * SC_LANES) == 0 and t % (workers * SC_LANES) == 0 and t & (t - 1) == 0
    assert n_slots % (SC_LANES * SC_UNROLL) == 0 and t % (SC_LANES * SC_UNROLL) == 0

    def body(cls_hbm, rank_hbm, starts_hbm, pos_hbm, slot_hbm, cls_v, rank_v, starts_v, pos_v, slot_v):
        wid = lax.axis_index("subcore") * SC_CORES + lax.axis_index("core")
        pltpu.sync_copy(cls_hbm, cls_v)
        pltpu.sync_copy(rank_hbm, rank_v)
        pltpu.sync_copy(starts_hbm, starts_v)

        lane = lax.iota(I32, SC_LANES)
        span = SC_LANES * SC_UNROLL

        @pl.loop(0, n_slots, step=span)
        def _(i):
            for u in range(SC_UNROLL):
                j = i + u * SC_LANES
                slot_v[pl.ds(j, SC_LANES)] = (lane + j) & (t - 1)

        @pl.loop(0, t, step=span)
        def _(i):
            for u in range(SC_UNROLL):
                j = i + u * SC_LANES
                at = pl.ds(j, SC_LANES)
                pos = plsc.load_gather(starts_v, [cls_v[at]]) + rank_v[at]
                pos_v[at] = pos
                plsc.store_scatter(slot_v, [pos], lane + j)

        tok_off = pl.multiple_of(wid * tok_per, SC_LANES)
        pltpu.sync_copy(pos_v.at[pl.ds(tok_off, tok_per)], pos_hbm.at[pl.ds(tok_off, tok_per)])
        slot_off = pl.multiple_of(wid * slot_per, SC_LANES)
        pltpu.sync_copy(slot_v.at[pl.ds(slot_off, slot_per)], slot_hbm.at[pl.ds(slot_off, slot_per)])

    return pl.kernel(
        body, mesh=_sc_mesh(), compiler_params=_sc_params(), name="sc_slots",
        out_type=(jax.ShapeDtypeStruct((t,), I32), jax.ShapeDtypeStruct((n_slots,), I32)),
        scratch_types=[pltpu.VMEM((t,), I32), pltpu.VMEM((t,), I32), pltpu.VMEM((CLASS_ROWS,), I32),
                       pltpu.VMEM((t,), I32), pltpu.VMEM((n_slots,), I32)],
    )(cls, rank, starts)


def _moe_experts(routed, wgu, wd, after):
    ha, hb, gates, idx, counts = routed
    t = ha.shape[0]
    n_slots = t + N_CLASSES * PAIR_TILE
    n_tiles = n_slots // PAIR_TILE

    cnt = counts[:, 0].astype(I32)
    padded = (cnt + PAIR_TILE - 1) // PAIR_TILE * PAIR_TILE
    ends = jnp.cumsum(padded)
    starts = ends - padded
    tile_cls = jnp.minimum(
        jnp.sum(ends[None, :N_CLASSES] <= (jnp.arange(n_tiles, dtype=I32) * PAIR_TILE)[:, None], axis=1),
        N_CLASSES - 1).astype(I32)
    first = (tile_cls // PAIRS_PER_GROUP) * EXPERTS_PER_GROUP
    tile_ea = first + jnp.asarray(PAIR_LO, I32)[tile_cls % PAIRS_PER_GROUP]
    tile_eb = first + jnp.asarray(PAIR_HI, I32)[tile_cls % PAIRS_PER_GROUP]
    used = (ends[N_CLASSES - 1:N_CLASSES] // PAIR_TILE).astype(I32)

    pos, slot_tok = _sc_slots(idx[0], idx[1], starts, n_slots)
    hsa, hsb, gate_s = _sc_gather((ha, hb, gates), slot_tok, after=wgu)

    row = lambda i, ea, eb, nu: (jnp.minimum(i, nu[0] - 1), 0)
    w_spec = lambda shape, which: pl.BlockSpec(
        (1,) + shape, lambda i, ea, eb, nu: ((ea, eb)[which][i], 0, 0))
    gu_shape, d_shape = (D_MODEL, 2 * D_EXPERT), (D_EXPERT, D_MODEL)
    ysa, ysb = pl.pallas_call(
        _moe_pair_kernel,
        grid_spec=pltpu.PrefetchScalarGridSpec(
            num_scalar_prefetch=3, grid=(n_tiles,),
            in_specs=[pl.BlockSpec((PAIR_TILE, PACK_W), row), pl.BlockSpec((PAIR_TILE, PACK_W), row),
                      pl.BlockSpec((PAIR_TILE, LANES), row),
                      w_spec(gu_shape, 0), w_spec(d_shape, 0), w_spec(gu_shape, 1), w_spec(d_shape, 1),
                      _ANY_SPEC],
            out_specs=[pl.BlockSpec((PAIR_TILE, PACK_W), row), pl.BlockSpec((PAIR_TILE, PACK_W), row)]),
        out_shape=[jax.ShapeDtypeStruct((n_slots, PACK_W), I32),
                   jax.ShapeDtypeStruct((n_slots, PACK_W), I32)],
        compiler_params=pltpu.CompilerParams(
            dimension_semantics=("arbitrary",), vmem_limit_bytes=VMEM_LIMIT),
        name="moe_pair",
    )(tile_ea, tile_eb, used, hsa, hsb, gate_s, wgu, wd, wgu, wd, after)
    return (ysa, ysb), pos


def _expert_cast(w_gate, w_up, w_down, layer, after):
    per = CAST_EXPERTS
    w_spec = lambda shape: pl.BlockSpec((None, per) + shape, lambda e: (layer, e, 0, 0))
    out = lambda shape: pl.BlockSpec((per,) + shape, lambda e: (e, 0, 0))
    return pl.pallas_call(
        _expert_cast_kernel,
        grid=(N_EXPERTS // per,),
        in_specs=[w_spec((D_MODEL, D_EXPERT)), w_spec((D_MODEL, D_EXPERT)), w_spec((D_EXPERT, D_MODEL)),
                  _ANY_SPEC],
        out_specs=[out((D_MODEL, 2 * D_EXPERT)), out((D_EXPERT, D_MODEL))],
        out_shape=[jax.ShapeDtypeStruct((N_EXPERTS, D_MODEL, 2 * D_EXPERT), BF16),
                   jax.ShapeDtypeStruct((N_EXPERTS, D_EXPERT, D_MODEL), BF16)],
        compiler_params=pltpu.CompilerParams(
            dimension_semantics=("arbitrary",), vmem_limit_bytes=VMEM_LIMIT),
        name="expert_cast",
    )(w_gate, w_up, w_down, after)


def _final_norm(x, moe_y, g, rows):
    t = x.shape[0]
    rows = min(rows, t)
    assert t % rows == 0
    tok = lambda i: (i, 0)
    return pl.pallas_call(
        _final_norm_kernel,
        grid=(t // rows,),
        in_specs=[pl.BlockSpec((rows, D_MODEL), tok), pl.BlockSpec((rows, PACK_W), tok),
                  pl.BlockSpec((rows, PACK_W), tok), _const_spec((1, D_MODEL))],
        out_specs=pl.BlockSpec((rows, D_MODEL), tok),
        out_shape=jax.ShapeDtypeStruct(x.shape, F32),
        compiler_params=pltpu.CompilerParams(
            dimension_semantics=("arbitrary",), vmem_limit_bytes=VMEM_LIMIT),
        name="final_norm",
    )(x, *moe_y, g.reshape(1, D_MODEL))


def _moe_dense(x, norm_g, router, wgu, wd, norm_final, layer, final_norm, rows, after):
    t = x.shape[0]
    assert t % rows == 0
    tok = lambda i, e: (i, 0)
    w_spec = lambda shape: pl.BlockSpec((1,) + shape, lambda i, e: (e, 0, 0))
    return pl.pallas_call(
        functools.partial(_moe_dense_kernel, final_norm=final_norm),
        grid=(t // rows, N_EXPERTS),
        in_specs=[pl.BlockSpec((rows, D_MODEL), tok), _layer_spec((None, 1, D_MODEL), layer),
                  *_router_specs(layer),
                  w_spec((D_MODEL, 2 * D_EXPERT)), w_spec((D_EXPERT, D_MODEL)),
                  _const_spec((1, D_MODEL)), _ANY_SPEC],
        out_specs=pl.BlockSpec((rows, D_MODEL), tok),
        out_shape=jax.ShapeDtypeStruct(x.shape, F32),
        scratch_shapes=[pltpu.VMEM((rows, D_MODEL), BF16),
                        pltpu.VMEM((rows, LANES), F32),
                        pltpu.VMEM((rows, D_MODEL), F32)],
        compiler_params=pltpu.CompilerParams(
            dimension_semantics=("arbitrary", "arbitrary"), vmem_limit_bytes=VMEM_LIMIT),
        name="moe_dense",
    )(x, norm_g, *router, wgu, wd, norm_final.reshape(1, D_MODEL), after)


def kernel(x_prompt, x_sample, cache_pool, state_ret, norm_mix, w_in, w_pool, pool_scale, ret_gn, w_out, norm_ffn, w_router_group, b_router_group, w_router_expert, b_router_expert, w_gate, w_up, w_down, norm_final):
    depth = norm_mix.shape[0]
    row = lambda a: a.reshape(depth, 1, a.shape[-1])
    mix_split = (row(norm_mix), *_split_weight(w_in), *_split_weight(w_pool), row(pool_scale),
                 row(ret_gn), *_split_weight(w_out))
    mix_hi = tuple(mix_split[i] for i in (0, 1, 3, 5, 6, 7))
    router = _router_weights(w_router_group, b_router_group, w_router_expert, b_router_expert)
    norm_ffn = row(norm_ffn)
    pool_prev = jnp.swapaxes(cache_pool, 1, 2)

    yp, ys = x_prompt, x_sample
    moe_p = None
    out_p = out_s = None
    for l in range(depth):
        yp, out_p, routed = _mixer_prompt(
            yp, moe_p, mix_split, (norm_ffn, *router), out_p, l, rows=512,
            precise_tail=PRECISE_TAIL_STEPS if l < depth - 1 else 0, after=ys)
        wgu, wd = _expert_cast(w_gate, w_up, w_down, l, after=routed[-1])
        ys, out_s = _mixer_sample(ys, pool_prev, state_ret, mix_hi, out_s, l, after=wgu)
        sorted_y, pos = _moe_experts(routed, wgu, wd, after=ys)
        ys = _moe_dense(ys.reshape(-1, D_MODEL), norm_ffn, router, wgu, wd, norm_final,
                        l, l == depth - 1, rows=1024, after=sorted_y[0]).reshape(ys.shape)
        moe_p = _sc_gather(sorted_y, pos, after=routed[-1])
    yp = _final_norm(yp.reshape(-1, D_MODEL), moe_p, norm_final, rows=2048).reshape(yp.shape)
    return (yp, ys, *out_p, jnp.swapaxes(out_s[0], 1, 2), out_s[1])
```

```python
import dataclasses
import functools
import itertools

import jax
import jax.numpy as jnp
import numpy as np
from jax import lax
from jax.experimental import pallas as pl
from jax.experimental.pallas import tpu as pltpu
from jax.experimental.pallas import tpu_sc as plsc

F32 = jnp.float32
BF16 = jnp.bfloat16
I32 = jnp.int32

D_MODEL = 1024
POOL_WIDTH = 512
POOL_WINDOWS = (2, 4, 8, 16)
POOL_GROUP_DIM = 128
POOL_BUF = 15
POOL_HIST = 16
RET_WIDTH = 512
RET_HEADS = 4
RET_HEAD_DIM = 128
RET_CHUNK = 128
ROPE_BASE = 10000.0
IN_WIDTH = POOL_WIDTH + 4 * RET_WIDTH
N_EXPERT_GROUPS = 4
EXPERTS_PER_GROUP = 4
N_EXPERTS = N_EXPERT_GROUPS * EXPERTS_PER_GROUP
D_EXPERT = 256
RMS_EPS = 1e-6
GN_EPS = 1e-5
PAST_LEN = 16384

LANES = 128
SUBLANES = 8
EXPERT_LANE0 = 8
GROUP_SHIFT = EXPERTS_PER_GROUP.bit_length() - 1
PAIRS = tuple(itertools.combinations(range(EXPERTS_PER_GROUP), 2))
PAIRS_PER_GROUP = len(PAIRS)
PAIR_LO, PAIR_HI = zip(*PAIRS)
N_CLASSES = N_EXPERT_GROUPS * PAIRS_PER_GROUP
CLASS_ROWS = 32
GATE_LO, GATE_HI = 0, 1
COUNT_BLOCK = 256
PAIR_TILE = 256
CAST_EXPERTS = 2
PACK_W = D_MODEL // 4
SC_CORES, SC_SUBCORES, SC_LANES = 2, 16, 16
SC_WINDOW = 128
SC_UNROLL = 8
PRECISE_TAIL_STEPS = 1
CHUNK = 128
SAMPLE_SEQS = 16
VMEM_LIMIT = 56 * 1024 * 1024


def _dot(a, b):
    return jnp.dot(a, b, preferred_element_type=F32)


def _dot_nt(a, b):
    return lax.dot_general(a, b, (((1,), (1,)), ((), ())), preferred_element_type=F32)


def _bf16_round_bits(u):
    return (u + jnp.uint32(0x7FFF) + ((u >> 16) & jnp.uint32(1))) & jnp.uint32(0xFFFF0000)


def _split(a):
    hi = pltpu.bitcast(_bf16_round_bits(pltpu.bitcast(a, jnp.uint32)), F32)
    return hi.astype(BF16), (a - hi).astype(BF16)


def _mm(a, b, precise, nt=False):
    dot = _dot_nt if nt else _dot
    if precise:
        b_hi, b_lo = b if isinstance(b, tuple) else _split(b)
        a_hi, a_lo = _split(a)
        return dot(a_hi, b_hi) + dot(a_lo, b_hi) + dot(a_hi, b_lo)
    return dot(a.astype(BF16), b[0] if isinstance(b, tuple) else b.astype(BF16))


def _rmsnorm(x, g):
    ms = jnp.mean(x * x, axis=-1, keepdims=True)
    return x * lax.rsqrt(ms + RMS_EPS) * g


def _pool_mix(ubuf, rows, t_first, n_prev, wpool_refs, pscale, precise=False, row0=0):
    ns = ubuf.shape[0]
    t = t_first + lax.broadcasted_iota(I32, (1, rows, POOL_GROUP_DIM), 1)
    base = POOL_HIST + row0
    outs = []
    for j, w in enumerate(POOL_WINDOWS):
        lanes = slice(j * POOL_GROUP_DIM, (j + 1) * POOL_GROUP_DIM)
        uj = ubuf[:, base:base + rows, lanes]
        acc = uj
        for i in range(1, w):
            acc = acc + ubuf[:, base - i:base - i + rows, lanes]
        cnt = jnp.minimum(w, n_prev + t + 1).astype(F32)
        d = (acc / cnt - uj).reshape(ns * rows, POOL_GROUP_DIM)
        outs.append(_mm(d, tuple(w[j] for w in wpool_refs), precise))
    return jnp.concatenate(outs, axis=-1) * pscale


def _rope(xh, cos2, sin2):
    return xh * cos2 + pltpu.roll(xh, RET_HEAD_DIM // 2, 1) * sin2


def _group_norm(o):
    mu = jnp.mean(o, axis=-1, keepdims=True)
    c = o - mu
    var = jnp.mean(c * c, axis=-1, keepdims=True)
    return c * lax.rsqrt(var + GN_EPS)


def _silu(x):
    return x * (1.0 / (1.0 + jnp.exp(-x)))


def _head(a, h):
    return a[:, h * RET_HEAD_DIM:(h + 1) * RET_HEAD_DIM]


def _qkvg(z):
    p, r = POOL_WIDTH, RET_WIDTH
    return z[:, p:p + r], z[:, p + r:p + 2 * r], z[:, p + 2 * r:p + 3 * r], z[:, p + 3 * r:p + 4 * r]


def _pack_bf16_pair(a, b):
    ua = pltpu.bitcast(a.astype(BF16).astype(F32), jnp.uint32)
    ub = pltpu.bitcast(b.astype(BF16).astype(F32), jnp.uint32)
    return pltpu.bitcast((ua >> 16) | (ub & jnp.uint32(0xFFFF0000)), I32)


def _unpack_bf16_pair(w):
    u = pltpu.bitcast(w, jnp.uint32)
    return pltpu.bitcast(u << 16, F32), pltpu.bitcast(u & jnp.uint32(0xFFFF0000), F32)


def _pack_row(y):
    q = PACK_W
    return _pack_bf16_pair(y[:, 0:q], y[:, q:2 * q]), _pack_bf16_pair(y[:, 2 * q:3 * q], y[:, 3 * q:])


def _unpack_row(wa, wb):
    return jnp.concatenate([*_unpack_bf16_pair(wa), *_unpack_bf16_pair(wb)], axis=-1)


def _zero_other_layers(ref, layer):
    for j in range(ref.shape[0]):
        if j != layer:
            ref[j] = jnp.zeros(ref.shape[1:], ref.dtype)


def _mixer_prompt_kernel(*refs, rows, steps, moe_in, precise_tail, layer, has_prev):
    (x_ref, ya_ref, yb_ref, cos_ref, sin_ref, ng_ref, win_hi_ref, win_lo_ref, wpool_hi_ref,
     wpool_lo_ref, pscale_ref, gn_ref, wout_hi_ref, wout_lo_ref, dmat_ref, qdec_ref, kdec_ref,
     cdec_ref) = refs[:18]
    route_in = refs[18:22]
    y_ref, tail_ref, sfin_ref, *route_out, ubuf, s_scr, ret_scr, z_scr, count_scr = (
        refs[23 + 2 * has_prev:])
    c = pl.program_id(1)

    @pl.when((pl.program_id(0) == 0) & (c == 0))
    def _():
        count_scr[...] = jnp.zeros_like(count_scr)
    if not has_prev:
        _zero_other_layers(tail_ref, layer)
        _zero_other_layers(sfin_ref, layer)
        tail_ref, sfin_ref = tail_ref.at[layer], sfin_ref.at[layer]

    @pl.when(c == 0)
    def _():
        ubuf[:, 0:POOL_HIST, :] = jnp.zeros((1, POOL_HIST, POOL_WIDTH), F32)
        s_scr[...] = jnp.zeros_like(s_scr)

    kv_cols = slice(POOL_WIDTH + RET_WIDTH, POOL_WIDTH + 3 * RET_WIDTH)

    def step(kv_precise, full_from):
        x = x_ref[0]
        if moe_in:
            x = x + _unpack_row(ya_ref[0], yb_ref[0])
        hn = _rmsnorm(x, ng_ref[...])
        hi, lo = _split(hn) if kv_precise else (hn.astype(BF16), None)
        z_scr[...] = _dot(hi, win_hi_ref[...])
        if kv_precise and full_from:
            z_scr[:full_from, kv_cols] += (_dot(lo[:full_from], win_hi_ref[:, kv_cols])
                                           + _dot(hi[:full_from], win_lo_ref[:, kv_cols]))
        if full_from < rows:
            z_scr[full_from:, :] += (_dot(lo[full_from:], win_hi_ref[...])
                                     + _dot(hi[full_from:], win_lo_ref[...]))

        ubuf[0, POOL_HIST:POOL_HIST + rows, :] = z_scr[:, :POOL_WIDTH]
        pool_w = (wpool_hi_ref, wpool_lo_ref)
        pool_parts = []
        if full_from:
            pool_parts.append(_pool_mix(ubuf, full_from, c * rows, 0, pool_w, pscale_ref[...]))
        if full_from < rows:
            pool_parts.append(_pool_mix(ubuf, rows - full_from, c * rows + full_from, 0, pool_w,
                                        pscale_ref[...], precise=True, row0=full_from))
        pool_out = jnp.concatenate(pool_parts, axis=0)
        tail_ref[...] = ubuf[:, rows + POOL_HIST - POOL_BUF:rows + POOL_HIST, :]
        ubuf[:, 0:POOL_HIST, :] = ubuf[:, rows:rows + POOL_HIST, :]

        scale = RET_HEAD_DIM ** -0.5
        for ci in range(rows // CHUNK):
            rs = slice(ci * CHUNK, (ci + 1) * CHUNK)
            full = ci * CHUNK >= full_from
            cos2 = cos_ref[rs, :]
            sin2 = sin_ref[rs, :]
            for h in range(RET_HEADS):
                col = lambda part: slice(POOL_WIDTH + part * RET_WIDTH + h * RET_HEAD_DIM,
                                         POOL_WIDTH + part * RET_WIDTH + (h + 1) * RET_HEAD_DIM)
                q = _rope(z_scr[rs, col(0)], cos2, sin2)
                k = _rope(z_scr[rs, col(1)], cos2, sin2) * scale
                v = z_scr[rs, col(2)]
                s_old = s_scr[h]
                scores = _mm(q, k, full, nt=True) * dmat_ref[h]
                qd = q * _head(qdec_ref[...], h)
                o = _mm(scores, v, full) + _mm(qd, s_old, full)
                kd = k * _head(kdec_ref[...], h)
                s_scr[h] = s_old * cdec_ref[h] + _mm(kd.T, v, kv_precise)
                on = _group_norm(o) * _head(gn_ref[...], h)
                ret_scr[rs, h * RET_HEAD_DIM:(h + 1) * RET_HEAD_DIM] = _silu(z_scr[rs, col(3)]) * on

        mix_in = jnp.concatenate([pool_out, ret_scr[...]], axis=-1)
        y_ref[0] = x + _dot(mix_in.astype(BF16), wout_hi_ref[...])
        if full_from < rows:
            m_hi, m_lo = _split(mix_in[full_from:])
            y_ref[0, full_from:, :] += _dot(m_lo, wout_hi_ref[...]) + _dot(m_hi, wout_lo_ref[...])
        sfin_ref[0] = s_scr[...]
        _route_tile(y_ref[0], full_from, *route_in, *route_out, count_scr)

    if precise_tail:
        pl.when(c < steps - precise_tail)(lambda: step(False, rows))
        if precise_tail > 1:
            pl.when((c >= steps - precise_tail) & (c < steps - 1))(lambda: step(True, rows))
        pl.when(c == steps - 1)(lambda: step(True, rows - CHUNK))
    else:
        step(False, rows)


def _mixer_sample_kernel(*refs, seq_len, layer, has_prev):
    (x_ref, prev_ref, s0_ref, cos_ref, sin_ref, ng_ref, win_ref, wpool_ref, pscale_ref, gn_ref,
     wout_ref, dmat_ref, qdec_ref, kdec_ref, cdec_ref) = refs[:15]
    y_ref, tail_ref, sfin_ref, ubuf, ret_scr = refs[16 + 2 * has_prev:]
    if not has_prev:
        _zero_other_layers(tail_ref, layer)
        _zero_other_layers(sfin_ref, layer)
        tail_ref, sfin_ref = tail_ref.at[layer], sfin_ref.at[layer]
    ns = SAMPLE_SEQS
    rows = ns * seq_len
    x = x_ref[...].reshape(rows, D_MODEL)
    hn = _rmsnorm(x, ng_ref[...]).astype(BF16)
    z = _dot(hn, win_ref[...])
    for j in range(POOL_BUF):
        ubuf[:, POOL_HIST - POOL_BUF + j, :] = prev_ref[j]
    ubuf[:, POOL_HIST:POOL_HIST + seq_len, :] = z[:, :POOL_WIDTH].reshape(ns, seq_len, POOL_WIDTH)
    pool_out = _pool_mix(ubuf, seq_len, 0, POOL_BUF, (wpool_ref,), pscale_ref[...])
    for j in range(POOL_BUF):
        tail_ref[j] = ubuf[:, seq_len + POOL_HIST - POOL_BUF + j, :]

    q_all, k_all, v_all, g_all = _qkvg(z)
    scale = RET_HEAD_DIM ** -0.5
    cos2 = cos_ref[...]
    sin2 = sin_ref[...]
    tok_seq = lax.broadcasted_iota(I32, (RET_HEAD_DIM, rows), 1) // seq_len
    for h in range(RET_HEADS):
        q = _rope(_head(q_all, h), cos2, sin2)
        k = _rope(_head(k_all, h), cos2, sin2) * scale
        vb = _head(v_all, h).astype(BF16)
        s_old = s0_ref[:, h]
        scores = _dot_nt(q.astype(BF16), k.astype(BF16)) * dmat_ref[h]
        qd = (q * _head(qdec_ref[...], h)).astype(BF16).reshape(ns, seq_len, RET_HEAD_DIM)
        o_state = jnp.einsum('bid,bde->bie', qd, s_old.astype(BF16), preferred_element_type=F32)
        o = _dot(scores.astype(BF16), vb) + o_state.reshape(rows, RET_HEAD_DIM)
        kdt = (k * _head(kdec_ref[...], h)).T
        lhs = jnp.concatenate(
            [jnp.where(tok_seq == b, kdt, 0.0).astype(BF16) for b in range(ns)], axis=0)
        upd = _dot(lhs, vb).reshape(ns, RET_HEAD_DIM, RET_HEAD_DIM)
        sfin_ref[:, h] = s_old * cdec_ref[h] + upd
        on = _group_norm(o) * _head(gn_ref[...], h)
        ret_scr[:, h * RET_HEAD_DIM:(h + 1) * RET_HEAD_DIM] = _silu(_head(g_all, h)) * on

    mix_in = jnp.concatenate([pool_out, ret_scr[...]], axis=-1).astype(BF16)
    y_ref[...] = (x + _dot(mix_in, wout_ref[...])).reshape(ns, seq_len, D_MODEL)


def _router_logits(hn, wr_cat_ref, wr_hi_ref, br_ref, precise_from=0):
    parts = []
    if precise_from:
        parts.append(_dot(hn[:precise_from].astype(BF16), wr_hi_ref[...]))
    if precise_from < hn.shape[0]:
        h_hi, h_lo = _split(hn[precise_from:])
        part = _dot(h_hi, wr_cat_ref[...])
        parts.append(part[:, :LANES] + part[:, LANES:] + _dot(h_lo, wr_hi_ref[...]))
    return jnp.concatenate(parts, axis=0) + br_ref[...]


def _select(group_lg, expert_lg, gidx, eidx, axis):
    red = dict(axis=axis, keepdims=True)
    neg = jnp.float32(-jnp.inf)
    gmax = jnp.max(group_lg, **red)
    g_sel = jnp.min(jnp.where(group_lg == gmax, gidx, N_EXPERT_GROUPS), **red)
    p_sel = 1.0 / jnp.sum(jnp.exp(group_lg - gmax), **red)
    emask = (eidx >> GROUP_SHIFT) == g_sel
    v1 = jnp.max(jnp.where(emask, expert_lg, neg), **red)
    i1 = jnp.min(jnp.where(emask & (expert_lg == v1), eidx, N_EXPERTS), **red)
    emask2 = emask & (eidx != i1)
    v2 = jnp.max(jnp.where(emask2, expert_lg, neg), **red)
    i2 = jnp.min(jnp.where(emask2 & (expert_lg == v2), eidx, N_EXPERTS), **red)
    e2 = jnp.exp(v2 - v1)
    return g_sel, i1, i2, p_sel / (1.0 + e2), p_sel * e2 / (1.0 + e2)


def _route_tile(x, precise_from, ng_ref, wr_cat_ref, wr_hi_ref, br_ref, ha_ref, hbg_ref, idx_ref,
                count_ref, carry_scr):
    rows = x.shape[0]
    hn = _rmsnorm(x, ng_ref[...])
    ha_ref[...], hbg_ref[:, :PACK_W] = _pack_row(hn)
    lgt = _router_logits(hn, wr_cat_ref, wr_hi_ref, br_ref, precise_from).T
    neg = jnp.float32(-jnp.inf)
    gidx = lax.broadcasted_iota(I32, (SUBLANES, rows), 0)
    eidx = lax.broadcasted_iota(I32, (N_EXPERTS, rows), 0)
    group_lg = jnp.where(gidx < N_EXPERT_GROUPS, lgt[0:SUBLANES], neg)
    expert_lg = lgt[EXPERT_LANE0:EXPERT_LANE0 + N_EXPERTS]
    g_sel, i1, i2, w1, w2 = _select(group_lg, expert_lg, gidx, eidx, 0)
    lo = jnp.minimum(i1, i2) - g_sel * EXPERTS_PER_GROUP
    hi = jnp.maximum(i1, i2) - g_sel * EXPERTS_PER_GROUP
    pair = ((lo * (2 * EXPERTS_PER_GROUP - 1 - lo)) >> 1) + (hi - lo - 1)
    cls = g_sel * PAIRS_PER_GROUP + pair
    w_lo = jnp.where(i1 < i2, w1, w2)
    w_hi = jnp.where(i1 < i2, w2, w1)

    crow = lax.broadcasted_iota(I32, (CLASS_ROWS, rows), 0)
    onehot = jnp.where(crow == cls, 1.0, 0.0)
    n_blk = rows // COUNT_BLOCK
    blocks = [onehot[:, j * COUNT_BLOCK:(j + 1) * COUNT_BLOCK] for j in range(n_blk)]
    r = lax.broadcasted_iota(I32, (COUNT_BLOCK, COUNT_BLOCK), 0)
    c = lax.broadcasted_iota(I32, (COUNT_BLOCK, COUNT_BLOCK), 1)
    upper = jnp.where(r < c, 1.0, 0.0).astype(BF16)
    within = _dot(jnp.concatenate(blocks, axis=0).astype(BF16), upper)
    carry = carry_scr[:, 0:1]
    ranks = []
    for j in range(n_blk):
        before = within[j * CLASS_ROWS:(j + 1) * CLASS_ROWS] + carry
        ranks.append(jnp.sum(blocks[j] * before, axis=0, keepdims=True))
        carry = carry + jnp.sum(blocks[j], axis=1, keepdims=True)
    rank = jnp.concatenate(ranks, axis=1)
    carry_scr[...] = jnp.broadcast_to(carry, carry_scr.shape)
    count_ref[...] = carry_scr[...]

    row8 = lax.broadcasted_iota(I32, (SUBLANES, rows), 0)
    idx_ref[...] = jnp.where(row8 == 0, cls, jnp.where(row8 == 1, rank.astype(I32), 0))
    rowl = lax.broadcasted_iota(I32, (LANES, rows), 0)
    gate_rec = jnp.where(rowl == GATE_LO, w_lo, jnp.where(rowl == GATE_HI, w_hi, 0.0)).T
    hbg_ref[:, PACK_W:] = pltpu.bitcast(gate_rec, I32)


def _expert_cast_kernel(wg_ref, wu_ref, wd_ref, after_ref, wgu_ref, wdb_ref):
    del after_ref
    wgu_ref[:, :, :D_EXPERT] = wg_ref[...].astype(BF16)
    wgu_ref[:, :, D_EXPERT:] = wu_ref[...].astype(BF16)
    wdb_ref[...] = wd_ref[...].astype(BF16)


def _moe_pair_kernel(ea_ref, eb_ref, used_ref, ha_ref, hbg_ref, wgu_a_ref, wd_a_ref,
                     wgu_b_ref, wd_b_ref, after_ref, ya_ref, yb_ref):
    del ea_ref, eb_ref, after_ref

    @pl.when(pl.program_id(0) < used_ref[0])
    def _():
        h = _unpack_row(ha_ref[...], hbg_ref[:, :PACK_W]).astype(BF16)
        gates = pltpu.bitcast(hbg_ref[:, PACK_W:], F32)
        y = None
        for wgu_ref, wd_ref, lane in ((wgu_a_ref, wd_a_ref, GATE_LO), (wgu_b_ref, wd_b_ref, GATE_HI)):
            gu = _dot(h, wgu_ref[0])
            act = _silu(gu[:, :D_EXPERT]) * gu[:, D_EXPERT:]
            part = _dot((act * gates[:, lane:lane + 1]).astype(BF16), wd_ref[0])
            y = part if y is None else y + part
        ya_ref[...], yb_ref[...] = _pack_row(y)


def _final_norm_kernel(x_ref, ya_ref, yb_ref, g_ref, o_ref):
    o_ref[...] = _rmsnorm(x_ref[...] + _unpack_row(ya_ref[...], yb_ref[...]), g_ref[...])


def _moe_dense_kernel(x_ref, ng_ref, wr_cat_ref, wr_hi_ref, br_ref, wgu_ref, wd_ref, nf_ref,
                      after_ref, y_ref, h_scr, gate_scr, acc_scr, *, final_norm):
    del after_ref
    e = pl.program_id(1)

    @pl.when(e == 0)
    def _():
        hn = _rmsnorm(x_ref[...], ng_ref[...])
        h_scr[...] = hn.astype(BF16)
        lg = _router_logits(hn, wr_cat_ref, wr_hi_ref, br_ref)
        lane = lax.broadcasted_iota(I32, lg.shape, 1)
        neg = jnp.float32(-jnp.inf)
        group_lg = jnp.where(lane < N_EXPERT_GROUPS, lg, neg)
        is_expert = (lane >= EXPERT_LANE0) & (lane < EXPERT_LANE0 + N_EXPERTS)
        _, i1, i2, w1, w2 = _select(group_lg, jnp.where(is_expert, lg, neg), lane,
                                    jnp.where(is_expert, lane - EXPERT_LANE0, N_EXPERTS), 1)
        gate_scr[...] = (jnp.where(lane == i1 + EXPERT_LANE0, w1, 0.0)
                         + jnp.where(lane == i2 + EXPERT_LANE0, w2, 0.0))
        acc_scr[...] = jnp.zeros_like(acc_scr)

    lane = lax.broadcasted_iota(I32, gate_scr.shape, 1)
    gate = jnp.sum(jnp.where(lane == EXPERT_LANE0 + e, gate_scr[...], 0.0), axis=-1, keepdims=True)
    gu = _dot(h_scr[...], wgu_ref[0])
    act = _silu(gu[:, :D_EXPERT]) * gu[:, D_EXPERT:]
    acc_scr[...] += _dot((act * gate).astype(BF16), wd_ref[0])

    @pl.when(e == N_EXPERTS - 1)
    def _():
        y = x_ref[...] + acc_scr[...]
        if final_norm:
            y = _rmsnorm(y, nf_ref[...])
        y_ref[...] = y


def _const_spec(shape):
    return pl.BlockSpec(shape, lambda *_: (0,) * len(shape))


def _resident_spec(shape):
    return pl.BlockSpec(shape, lambda *_: (0,) * len(shape), pipeline_mode=pl.Buffered(1))


_ANY_SPEC = pl.BlockSpec(memory_space=pl.ANY)


def _rope_tables(pos):
    half = RET_HEAD_DIM // 2
    inv = np.float32(ROPE_BASE) ** (-np.arange(half, dtype=np.float32) / np.float32(half))
    ang = pos.astype(np.float32)[:, None] * inv[None, :]
    cos, sin = np.cos(ang), np.sin(ang)
    return np.concatenate([cos, cos], -1), np.concatenate([-sin, sin], -1)


def _decay_consts(c, reps):
    f = np.float32
    log_g = np.log(f(1.0) - f(2.0) ** (f(-5.0) - np.arange(RET_HEADS, dtype=f)))
    i = np.arange(c, dtype=f)
    diff = i[:, None] - i[None, :]
    dmat = np.where(diff[None] >= 0, np.exp(np.maximum(diff, f(0))[None] * log_g[:, None, None]), f(0))
    qdec = np.exp((i + f(1))[None, :] * log_g[:, None])
    kdec = np.exp((f(c) - f(1) - i)[None, :] * log_g[:, None])
    cdec = np.exp(f(c) * log_g)
    dmat = np.einsum('ab,hij->haibj', np.eye(reps, dtype=f), dmat).reshape(RET_HEADS, reps * c, reps * c)
    lanes = lambda t: np.repeat(np.tile(t, (1, reps)).T, RET_HEAD_DIM, axis=1)
    cdec = np.broadcast_to(cdec[:, None, None], (RET_HEADS, 1, RET_HEAD_DIM))
    return tuple(np.ascontiguousarray(a, dtype=f) for a in (dmat, lanes(qdec), lanes(kdec), cdec))


def _split_weight(w):
    hi = lax.bitcast_convert_type(_bf16_round_bits(lax.bitcast_convert_type(w, jnp.uint32)), F32)
    return hi.astype(BF16), (w - hi).astype(BF16)


_W_IN_BLOCK = (None, D_MODEL, IN_WIDTH)
_W_POOL_BLOCK = (None, len(POOL_WINDOWS), POOL_GROUP_DIM, POOL_GROUP_DIM)
_W_OUT_BLOCK = (None, D_MODEL, D_MODEL)
_DECAY_SPECS = [
    _const_spec((RET_HEADS, CHUNK, CHUNK)), _const_spec((CHUNK, RET_WIDTH)),
    _const_spec((CHUNK, RET_WIDTH)), _const_spec((RET_HEADS, 1, RET_HEAD_DIM)),
]


def _layer_spec(block, layer, resident=False):
    index_map = lambda *_: (layer,) + (0,) * (len(block) - 1)
    if resident:
        return pl.BlockSpec(block, index_map, pipeline_mode=pl.Buffered(1))
    return pl.BlockSpec(block, index_map)


def _mixer_weight_specs(layer, split):
    n = 2 if split else 1
    row = lambda width: _layer_spec((None, 1, width), layer)
    return [row(D_MODEL), *[_layer_spec(_W_IN_BLOCK, layer, True)] * n,
            *[_layer_spec(_W_POOL_BLOCK, layer)] * n, row(POOL_WIDTH), row(RET_WIDTH),
            *[_layer_spec(_W_OUT_BLOCK, layer, True)] * n]


def _chain_out_spec(block, depth, layer, has_prev):
    zeros = (0,) * (len(block) - 1)
    if has_prev:
        return pl.BlockSpec((None,) + block, lambda i, *_: (layer, i) + zeros)
    return pl.BlockSpec((depth,) + block, lambda i, *_: (0, i) + zeros)


def _mixer_prompt(x, moe_y, weights, route_w, prev_out, layer, rows, precise_tail, after):
    b, l, _ = x.shape
    t = b * l
    depth = weights[0].shape[0]
    assert l % rows == 0 and rows % CHUNK == 0
    cos2, sin2 = _rope_tables(np.arange(l))
    decay = _decay_consts(RET_CHUNK, 1)
    steps = l // rows
    tok = lambda i, c: (i, c, 0)
    if moe_y is None:
        ya = yb = jnp.zeros((1, rows, PACK_W), I32)
        y_spec = _const_spec((1, rows, PACK_W))
    else:
        ya, yb = (t.reshape(b, l, PACK_W) for t in moe_y)
        y_spec = pl.BlockSpec((1, rows, PACK_W), tok)
    has_prev = prev_out is not None
    n_in = 23
    flat = lambda i, c: (i * steps + c, 0)
    y, tails, states, *routed = pl.pallas_call(
        functools.partial(_mixer_prompt_kernel, rows=rows, steps=steps, moe_in=moe_y is not None,
                          precise_tail=precise_tail, layer=layer, has_prev=has_prev),
        grid=(b, steps),
        in_specs=[pl.BlockSpec((1, rows, D_MODEL), tok), y_spec, y_spec,
                  pl.BlockSpec((rows, RET_HEAD_DIM), lambda i, c: (c, 0)),
                  pl.BlockSpec((rows, RET_HEAD_DIM), lambda i, c: (c, 0)),
                  *_mixer_weight_specs(layer, True), *_DECAY_SPECS,
                  _layer_spec((None, 1, D_MODEL), layer), *_router_specs(layer), _ANY_SPEC,
                  *([_ANY_SPEC, _ANY_SPEC] if has_prev else [])],
        out_specs=[pl.BlockSpec((1, rows, D_MODEL), tok),
                   _chain_out_spec((1, POOL_BUF, POOL_WIDTH), depth, layer, has_prev),
                   _chain_out_spec((1, RET_HEADS, RET_HEAD_DIM, RET_HEAD_DIM), depth, layer, has_prev),
                   pl.BlockSpec((rows, PACK_W), flat), pl.BlockSpec((rows, PACK_W + LANES), flat),
                   pl.BlockSpec((SUBLANES, rows), lambda i, c: (0, i * steps + c)),
                   _const_spec((CLASS_ROWS, LANES))],
        out_shape=[jax.ShapeDtypeStruct(x.shape, F32),
                   jax.ShapeDtypeStruct((depth, b, POOL_BUF, POOL_WIDTH), F32),
                   jax.ShapeDtypeStruct((depth, b, RET_HEADS, RET_HEAD_DIM, RET_HEAD_DIM), F32),
                   jax.ShapeDtypeStruct((t, PACK_W), I32),
                   jax.ShapeDtypeStruct((t, PACK_W + LANES), I32), jax.ShapeDtypeStruct((SUBLANES, t), I32),
                   jax.ShapeDtypeStruct((CLASS_ROWS, LANES), F32)],
        input_output_aliases={n_in: 1, n_in + 1: 2} if has_prev else {},
        scratch_shapes=[pltpu.VMEM((1, POOL_HIST + rows, POOL_WIDTH), F32),
                        pltpu.VMEM((RET_HEADS, RET_HEAD_DIM, RET_HEAD_DIM), F32),
                        pltpu.VMEM((rows, RET_WIDTH), F32),
                        pltpu.VMEM((rows, IN_WIDTH), F32),
                        pltpu.VMEM((CLASS_ROWS, LANES), F32)],
        compiler_params=pltpu.CompilerParams(
            dimension_semantics=("arbitrary", "arbitrary"), vmem_limit_bytes=VMEM_LIMIT),
        name="mixer_prompt",
    )(x, ya, yb, cos2, sin2, *weights, *decay, *route_w, after, *(prev_out if has_prev else ()))
    return y, (tails, states), tuple(routed)


def _mixer_sample(x, pool_prev, s0, weights, prev_out, layer, after):
    b, l, _ = x.shape
    depth = s0.shape[0]
    ns = SAMPLE_SEQS
    assert ns * l == CHUNK and b % ns == 0
    cos2, sin2 = _rope_tables(PAST_LEN + np.arange(l))
    cos2, sin2 = np.tile(cos2, (ns, 1)), np.tile(sin2, (ns, 1))
    decay = _decay_consts(l, ns)
    seq3 = lambda i: (i, 0, 0)
    state_block = (ns, RET_HEADS, RET_HEAD_DIM, RET_HEAD_DIM)
    tail_block = (POOL_BUF, ns, POOL_WIDTH)
    if prev_out is None:
        tail_out = pl.BlockSpec((depth,) + tail_block, lambda i: (0, 0, i, 0))
    else:
        tail_out = pl.BlockSpec((None,) + tail_block, lambda i: (layer, 0, i, 0))
    has_prev = prev_out is not None
    n_in = 16
    y, tails, states = pl.pallas_call(
        functools.partial(_mixer_sample_kernel, seq_len=l, layer=layer, has_prev=has_prev),
        grid=(b // ns,),
        in_specs=[pl.BlockSpec((ns, l, D_MODEL), seq3),
                  pl.BlockSpec((None,) + tail_block, lambda i: (layer, 0, i, 0)),
                  pl.BlockSpec((None,) + state_block, lambda i: (layer, i, 0, 0, 0)),
                  _const_spec((CHUNK, RET_HEAD_DIM)), _const_spec((CHUNK, RET_HEAD_DIM)),
                  *_mixer_weight_specs(layer, False), *_DECAY_SPECS, _ANY_SPEC,
                  *([_ANY_SPEC, _ANY_SPEC] if has_prev else [])],
        out_specs=[pl.BlockSpec((ns, l, D_MODEL), seq3), tail_out,
                   _chain_out_spec(state_block, depth, layer, has_prev)],
        out_shape=[jax.ShapeDtypeStruct(x.shape, F32),
                   jax.ShapeDtypeStruct((depth, POOL_BUF, b, POOL_WIDTH), F32),
                   jax.ShapeDtypeStruct(s0.shape, F32)],
        input_output_aliases={n_in: 1, n_in + 1: 2} if has_prev else {},
        scratch_shapes=[pltpu.VMEM((ns, POOL_HIST + l, POOL_WIDTH), F32),
                        pltpu.VMEM((CHUNK, RET_WIDTH), F32)],
        compiler_params=pltpu.CompilerParams(
            dimension_semantics=("arbitrary",), vmem_limit_bytes=VMEM_LIMIT),
        name="mixer_sample",
    )(x, pool_prev, s0, cos2, sin2, *weights, *decay, after, *(prev_out if has_prev else ()))
    return y, (tails, states)


def _router_weights(w_rg, b_rg, w_re, b_re):
    depth = w_rg.shape[0]
    gap = EXPERT_LANE0 - N_EXPERT_GROUPS
    rest = LANES - EXPERT_LANE0 - N_EXPERTS
    wr = jnp.concatenate([w_rg, jnp.zeros((depth, D_MODEL, gap), F32), w_re,
                          jnp.zeros((depth, D_MODEL, rest), F32)], axis=-1)
    br = jnp.concatenate([b_rg, jnp.zeros((depth, gap), F32), b_re, jnp.zeros((depth, rest), F32)],
                         axis=-1).reshape(depth, 1, LANES)
    wr_hi, wr_lo = _split_weight(wr)
    return jnp.concatenate([wr_hi, wr_lo], axis=-1), wr_hi, br


def _router_specs(layer):
    return [_layer_spec((None, D_MODEL, 2 * LANES), layer), _layer_spec((None, D_MODEL, LANES), layer),
            _layer_spec((None, 1, LANES), layer)]


def _sc_mesh():
    return plsc.VectorSubcoreMesh(core_axis_name="core", subcore_axis_name="subcore",
                                  num_cores=SC_CORES, num_subcores=SC_SUBCORES)


def _sc_params():
    params = pltpu.CompilerParams()
    if "needs_layout_passes" in pltpu.CompilerParams.__dataclass_fields__:
        params = dataclasses.replace(params, needs_layout_passes=False)
    return params


def _sc_gather(tables, idx, after):
    n = idx.shape[0]
    assert n % SC_WINDOW == 0
    nt = len(tables)

    def body(*refs):
        i_hbm = refs[nt]
        for t_hbm, o_hbm in zip(refs[:nt], refs[nt + 2:]):
            def gather_window(i_vmem, o_vmem, t_hbm=t_hbm):
                pltpu.sync_copy(t_hbm.at[i_vmem.at[0]], o_vmem)

            pltpu.emit_pipeline(
                gather_window, grid=(n // SC_WINDOW,),
                in_specs=[pl.BlockSpec((1, SC_WINDOW), lambda i: (0, i))],
                out_specs=[pl.BlockSpec((SC_WINDOW, t_hbm.shape[1]), lambda i: (i, 0))],
                core_axis_name=("core", "subcore"),
                dimension_semantics=(pltpu.PARALLEL,),
            )(i_hbm, o_hbm)

    out_type = tuple(jax.ShapeDtypeStruct((n, t.shape[1]), t.dtype) for t in tables)
    return pl.kernel(body, out_type=out_type, mesh=_sc_mesh(), name="sc_gather")(
        *tables, idx.reshape(1, n), after)


def _sc_slots(cls, rank, starts, n_slots):
    t = cls.shape[0]
    workers = SC_CORES * SC_SUBCORES
    slot_per, tok_per = n_slots // workers, t // workers
    assert n_slots % (workers * SC_LANES) == 0 and t % (workers * SC_LANES) == 0 and t & (t - 1) == 0
    assert n_slots % (SC_LANES * SC_UNROLL) == 0 and t % (SC_LANES * SC_UNROLL) == 0

    def body(cls_hbm, rank_hbm, starts_hbm, pos_hbm, slot_hbm, cls_v, rank_v, starts_v, pos_v, slot_v):
        wid = lax.axis_index("subcore") * SC_CORES + lax.axis_index("core")
        pltpu.sync_copy(cls_hbm, cls_v)
        pltpu.sync_copy(rank_hbm, rank_v)
        pltpu.sync_copy(starts_hbm, starts_v)

        lane = lax.iota(I32, SC_LANES)
        span = SC_LANES * SC_UNROLL

        @pl.loop(0, n_slots, step=span)
        def _(i):
            for u in range(SC_UNROLL):
                j = i + u * SC_LANES
                slot_v[pl.ds(j, SC_LANES)] = (lane + j) & (t - 1)

        @pl.loop(0, t, step=span)
        def _(i):
            for u in range(SC_UNROLL):
                j = i + u * SC_LANES
                at = pl.ds(j, SC_LANES)
                pos = plsc.load_gather(starts_v, [cls_v[at]]) + rank_v[at]
                pos_v[at] = pos
                plsc.store_scatter(slot_v, [pos], lane + j)

        tok_off = pl.multiple_of(wid * tok_per, SC_LANES)
        pltpu.sync_copy(pos_v.at[pl.ds(tok_off, tok_per)], pos_hbm.at[pl.ds(tok_off, tok_per)])
        slot_off = pl.multiple_of(wid * slot_per, SC_LANES)
        pltpu.sync_copy(slot_v.at[pl.ds(slot_off, slot_per)], slot_hbm.at[pl.ds(slot_off, slot_per)])

    return pl.kernel(
        body, mesh=_sc_mesh(), compiler_params=_sc_params(), name="sc_slots",
        out_type=(jax.ShapeDtypeStruct((t,), I32), jax.ShapeDtypeStruct((n_slots,), I32)),
        scratch_types=[pltpu.VMEM((t,), I32), pltpu.VMEM((t,), I32), pltpu.VMEM((CLASS_ROWS,), I32),
                       pltpu.VMEM((t,), I32), pltpu.VMEM((n_slots,), I32)],
    )(cls, rank, starts)


def _moe_experts(routed, wgu, wd, after):
    ha, hbg, idx, counts = routed
    t = ha.shape[0]
    n_slots = t + N_CLASSES * PAIR_TILE
    n_tiles = n_slots // PAIR_TILE

    cnt = counts[:, 0].astype(I32)
    padded = (cnt + PAIR_TILE - 1) // PAIR_TILE * PAIR_TILE
    ends = jnp.cumsum(padded)
    starts = ends - padded
    tile_cls = jnp.minimum(
        jnp.sum(ends[None, :N_CLASSES] <= (jnp.arange(n_tiles, dtype=I32) * PAIR_TILE)[:, None], axis=1),
        N_CLASSES - 1).astype(I32)
    first = (tile_cls // PAIRS_PER_GROUP) * EXPERTS_PER_GROUP
    tile_ea = first + jnp.asarray(PAIR_LO, I32)[tile_cls % PAIRS_PER_GROUP]
    tile_eb = first + jnp.asarray(PAIR_HI, I32)[tile_cls % PAIRS_PER_GROUP]
    used = (ends[N_CLASSES - 1:N_CLASSES] // PAIR_TILE).astype(I32)

    pos, slot_tok = _sc_slots(idx[0], idx[1], starts, n_slots)
    hsa, hsbg = _sc_gather((ha, hbg), slot_tok, after=wgu)

    row = lambda i, ea, eb, nu: (jnp.minimum(i, nu[0] - 1), 0)
    w_spec = lambda shape, which: pl.BlockSpec(
        (1,) + shape, lambda i, ea, eb, nu: ((ea, eb)[which][i], 0, 0))
    gu_shape, d_shape = (D_MODEL, 2 * D_EXPERT), (D_EXPERT, D_MODEL)
    ysa, ysb = pl.pallas_call(
        _moe_pair_kernel,
        grid_spec=pltpu.PrefetchScalarGridSpec(
            num_scalar_prefetch=3, grid=(n_tiles,),
            in_specs=[pl.BlockSpec((PAIR_TILE, PACK_W), row),
                      pl.BlockSpec((PAIR_TILE, PACK_W + LANES), row),
                      w_spec(gu_shape, 0), w_spec(d_shape, 0), w_spec(gu_shape, 1), w_spec(d_shape, 1),
                      _ANY_SPEC],
            out_specs=[pl.BlockSpec((PAIR_TILE, PACK_W), row), pl.BlockSpec((PAIR_TILE, PACK_W), row)]),
        out_shape=[jax.ShapeDtypeStruct((n_slots, PACK_W), I32),
                   jax.ShapeDtypeStruct((n_slots, PACK_W), I32)],
        compiler_params=pltpu.CompilerParams(
            dimension_semantics=("arbitrary",), vmem_limit_bytes=VMEM_LIMIT),
        name="moe_pair",
    )(tile_ea, tile_eb, used, hsa, hsbg, wgu, wd, wgu, wd, after)
    return (ysa, ysb), pos


def _expert_cast(w_gate, w_up, w_down, layer, after):
    per = CAST_EXPERTS
    w_spec = lambda shape: pl.BlockSpec((None, per) + shape, lambda e: (layer, e, 0, 0))
    out = lambda shape: pl.BlockSpec((per,) + shape, lambda e: (e, 0, 0))
    return pl.pallas_call(
        _expert_cast_kernel,
        grid=(N_EXPERTS // per,),
        in_specs=[w_spec((D_MODEL, D_EXPERT)), w_spec((D_MODEL, D_EXPERT)), w_spec((D_EXPERT, D_MODEL)),
                  _ANY_SPEC],
        out_specs=[out((D_MODEL, 2 * D_EXPERT)), out((D_EXPERT, D_MODEL))],
        out_shape=[jax.ShapeDtypeStruct((N_EXPERTS, D_MODEL, 2 * D_EXPERT), BF16),
                   jax.ShapeDtypeStruct((N_EXPERTS, D_EXPERT, D_MODEL), BF16)],
        compiler_params=pltpu.CompilerParams(
            dimension_semantics=("arbitrary",), vmem_limit_bytes=VMEM_LIMIT),
        name="expert_cast",
    )(w_gate, w_up, w_down, after)


def _final_norm(x, moe_y, g, rows):
    t = x.shape[0]
    rows = min(rows, t)
    assert t % rows == 0
    tok = lambda i: (i, 0)
    return pl.pallas_call(
        _final_norm_kernel,
        grid=(t // rows,),
        in_specs=[pl.BlockSpec((rows, D_MODEL), tok), pl.BlockSpec((rows, PACK_W), tok),
                  pl.BlockSpec((rows, PACK_W), tok), _const_spec((1, D_MODEL))],
        out_specs=pl.BlockSpec((rows, D_MODEL), tok),
        out_shape=jax.ShapeDtypeStruct(x.shape, F32),
        compiler_params=pltpu.CompilerParams(
            dimension_semantics=("arbitrary",), vmem_limit_bytes=VMEM_LIMIT),
        name="final_norm",
    )(x, *moe_y, g.reshape(1, D_MODEL))


def _moe_dense(x, norm_g, router, wgu, wd, norm_final, layer, final_norm, rows, after):
    t = x.shape[0]
    assert t % rows == 0
    tok = lambda i, e: (i, 0)
    w_spec = lambda shape: pl.BlockSpec((1,) + shape, lambda i, e: (e, 0, 0))
    return pl.pallas_call(
        functools.partial(_moe_dense_kernel, final_norm=final_norm),
        grid=(t // rows, N_EXPERTS),
        in_specs=[pl.BlockSpec((rows, D_MODEL), tok), _layer_spec((None, 1, D_MODEL), layer),
                  *_router_specs(layer),
                  w_spec((D_MODEL, 2 * D_EXPERT)), w_spec((D_EXPERT, D_MODEL)),
                  _const_spec((1, D_MODEL)), _ANY_SPEC],
        out_specs=pl.BlockSpec((rows, D_MODEL), tok),
        out_shape=jax.ShapeDtypeStruct(x.shape, F32),
        scratch_shapes=[pltpu.VMEM((rows, D_MODEL), BF16),
                        pltpu.VMEM((rows, LANES), F32),
                        pltpu.VMEM((rows, D_MODEL), F32)],
        compiler_params=pltpu.CompilerParams(
            dimension_semantics=("arbitrary", "arbitrary"), vmem_limit_bytes=VMEM_LIMIT),
        name="moe_dense",
    )(x, norm_g, *router, wgu, wd, norm_final.reshape(1, D_MODEL), after)


def kernel(x_prompt, x_sample, cache_pool, state_ret, norm_mix, w_in, w_pool, pool_scale, ret_gn, w_out, norm_ffn, w_router_group, b_router_group, w_router_expert, b_router_expert, w_gate, w_up, w_down, norm_final):
    depth = norm_mix.shape[0]
    row = lambda a: a.reshape(depth, 1, a.shape[-1])
    mix_split = (row(norm_mix), *_split_weight(w_in), *_split_weight(w_pool), row(pool_scale),
                 row(ret_gn), *_split_weight(w_out))
    mix_hi = tuple(mix_split[i] for i in (0, 1, 3, 5, 6, 7))
    router = _router_weights(w_router_group, b_router_group, w_router_expert, b_router_expert)
    norm_ffn = row(norm_ffn)
    pool_prev = jnp.swapaxes(cache_pool, 1, 2)

    yp, ys = x_prompt, x_sample
    moe_p = None
    out_p = out_s = None
    for l in range(depth):
        yp, out_p, routed = _mixer_prompt(
            yp, moe_p, mix_split, (norm_ffn, *router), out_p, l, rows=512,
            precise_tail=PRECISE_TAIL_STEPS if l < depth - 1 else 0, after=ys)
        wgu, wd = _expert_cast(w_gate, w_up, w_down, l, after=routed[-1])
        ys, out_s = _mixer_sample(ys, pool_prev, state_ret, mix_hi, out_s, l, after=wgu)
        sorted_y, pos = _moe_experts(routed, wgu, wd, after=ys)
        ys = _moe_dense(ys.reshape(-1, D_MODEL), norm_ffn, router, wgu, wd, norm_final,
                        l, l == depth - 1, rows=1024, after=sorted_y[0]).reshape(ys.shape)
        moe_p = _sc_gather(sorted_y, pos, after=routed[-1])
    yp = _final_norm(yp.reshape(-1, D_MODEL), moe_p, norm_final, rows=2048).reshape(yp.shape)
    return (yp, ys, *out_p, jnp.swapaxes(out_s[0], 1, 2), out_s[1])
```

```python
import dataclasses
import functools
import itertools

import jax
import jax.numpy as jnp
import numpy as np
from jax import lax
from jax.experimental import pallas as pl
from jax.experimental.pallas import tpu as pltpu
from jax.experimental.pallas import tpu_sc as plsc

F32 = jnp.float32
BF16 = jnp.bfloat16
I32 = jnp.int32

D_MODEL = 1024
POOL_WIDTH = 512
POOL_WINDOWS = (2, 4, 8, 16)
POOL_GROUP_DIM = 128
POOL_BUF = 15
POOL_HIST = 16
RET_WIDTH = 512
RET_HEADS = 4
RET_HEAD_DIM = 128
RET_CHUNK = 128
ROPE_BASE = 10000.0
IN_WIDTH = POOL_WIDTH + 4 * RET_WIDTH
N_EXPERT_GROUPS = 4
EXPERTS_PER_GROUP = 4
N_EXPERTS = N_EXPERT_GROUPS * EXPERTS_PER_GROUP
D_EXPERT = 256
RMS_EPS = 1e-6
GN_EPS = 1e-5
PAST_LEN = 16384

LANES = 128
SUBLANES = 8
EXPERT_LANE0 = 8
GROUP_SHIFT = EXPERTS_PER_GROUP.bit_length() - 1
PAIRS = tuple(itertools.combinations(range(EXPERTS_PER_GROUP), 2))
PAIRS_PER_GROUP = len(PAIRS)
PAIR_LO, PAIR_HI = zip(*PAIRS)
N_CLASSES = N_EXPERT_GROUPS * PAIRS_PER_GROUP
CLASS_ROWS = 32
GATE_LO, GATE_HI = 0, 1
COUNT_BLOCK = 256
PAIR_TILE = 256
CAST_EXPERTS = 2
PACK_W = D_MODEL // 4
SC_CORES, SC_SUBCORES, SC_LANES = 2, 16, 16
SC_WINDOW = 128
SC_UNROLL = 8
PRECISE_TAIL_STEPS = 2
CHUNK = 128
SAMPLE_SEQS = 16
VMEM_LIMIT = 56 * 1024 * 1024


def _dot(a, b):
    return jnp.dot(a, b, preferred_element_type=F32)


def _dot_nt(a, b):
    return lax.dot_general(a, b, (((1,), (1,)), ((), ())), preferred_element_type=F32)


def _bf16_round_bits(u):
    return (u + jnp.uint32(0x7FFF) + ((u >> 16) & jnp.uint32(1))) & jnp.uint32(0xFFFF0000)


def _split(a):
    hi = pltpu.bitcast(_bf16_round_bits(pltpu.bitcast(a, jnp.uint32)), F32)
    return hi.astype(BF16), (a - hi).astype(BF16)


def _mm(a, b, precise, nt=False):
    dot = _dot_nt if nt else _dot
    if precise:
        b_hi, b_lo = b if isinstance(b, tuple) else _split(b)
        a_hi, a_lo = _split(a)
        return dot(a_hi, b_hi) + dot(a_lo, b_hi) + dot(a_hi, b_lo)
    return dot(a.astype(BF16), b[0] if isinstance(b, tuple) else b.astype(BF16))


def _rmsnorm(x, g):
    ms = jnp.mean(x * x, axis=-1, keepdims=True)
    return x * lax.rsqrt(ms + RMS_EPS) * g


def _pool_mix(ubuf, rows, t_first, n_prev, wpool_refs, pscale, precise=False, row0=0):
    ns = ubuf.shape[0]
    t = t_first + lax.broadcasted_iota(I32, (1, rows, POOL_GROUP_DIM), 1)
    base = POOL_HIST + row0
    outs = []
    for j, w in enumerate(POOL_WINDOWS):
        lanes = slice(j * POOL_GROUP_DIM, (j + 1) * POOL_GROUP_DIM)
        uj = ubuf[:, base:base + rows, lanes]
        acc = uj
        for i in range(1, w):
            acc = acc + ubuf[:, base - i:base - i + rows, lanes]
        cnt = jnp.minimum(w, n_prev + t + 1).astype(F32)
        d = (acc / cnt - uj).reshape(ns * rows, POOL_GROUP_DIM)
        outs.append(_mm(d, tuple(w[j] for w in wpool_refs), precise))
    return jnp.concatenate(outs, axis=-1) * pscale


def _rope(xh, cos2, sin2):
    return xh * cos2 + pltpu.roll(xh, RET_HEAD_DIM // 2, 1) * sin2


def _group_norm(o):
    mu = jnp.mean(o, axis=-1, keepdims=True)
    c = o - mu
    var = jnp.mean(c * c, axis=-1, keepdims=True)
    return c * lax.rsqrt(var + GN_EPS)


def _silu(x):
    return x * (1.0 / (1.0 + jnp.exp(-x)))


def _head(a, h):
    return a[:, h * RET_HEAD_DIM:(h + 1) * RET_HEAD_DIM]


def _qkvg(z):
    p, r = POOL_WIDTH, RET_WIDTH
    return z[:, p:p + r], z[:, p + r:p + 2 * r], z[:, p + 2 * r:p + 3 * r], z[:, p + 3 * r:p + 4 * r]


def _pack_bf16_pair(a, b):
    ua = pltpu.bitcast(a.astype(BF16).astype(F32), jnp.uint32)
    ub = pltpu.bitcast(b.astype(BF16).astype(F32), jnp.uint32)
    return pltpu.bitcast((ua >> 16) | (ub & jnp.uint32(0xFFFF0000)), I32)


def _unpack_bf16_pair(w):
    u = pltpu.bitcast(w, jnp.uint32)
    return pltpu.bitcast(u << 16, F32), pltpu.bitcast(u & jnp.uint32(0xFFFF0000), F32)


def _pack_row(y):
    q = PACK_W
    return _pack_bf16_pair(y[:, 0:q], y[:, q:2 * q]), _pack_bf16_pair(y[:, 2 * q:3 * q], y[:, 3 * q:])


def _unpack_row(wa, wb):
    return jnp.concatenate([*_unpack_bf16_pair(wa), *_unpack_bf16_pair(wb)], axis=-1)


def _zero_other_layers(ref, layer):
    for j in range(ref.shape[0]):
        if j != layer:
            ref[j] = jnp.zeros(ref.shape[1:], ref.dtype)


def _mixer_prompt_kernel(*refs, rows, steps, moe_in, precise_tail, layer, has_prev):
    (x_ref, ya_ref, yb_ref, cos_ref, sin_ref, ng_ref, win_hi_ref, win_lo_ref, wpool_hi_ref,
     wpool_lo_ref, pscale_ref, gn_ref, wout_hi_ref, wout_lo_ref, dmat_ref, qdec_ref, kdec_ref,
     cdec_ref) = refs[:18]
    route_in = refs[18:22]
    y_ref, tail_ref, sfin_ref, *route_out, ubuf, s_scr, ret_scr, z_scr, count_scr = (
        refs[23 + 2 * has_prev:])
    c = pl.program_id(1)

    @pl.when((pl.program_id(0) == 0) & (c == 0))
    def _():
        count_scr[...] = jnp.zeros_like(count_scr)
    if not has_prev:
        _zero_other_layers(tail_ref, layer)
        _zero_other_layers(sfin_ref, layer)
        tail_ref, sfin_ref = tail_ref.at[layer], sfin_ref.at[layer]

    @pl.when(c == 0)
    def _():
        ubuf[:, 0:POOL_HIST, :] = jnp.zeros((1, POOL_HIST, POOL_WIDTH), F32)
        s_scr[...] = jnp.zeros_like(s_scr)

    kv_cols = slice(POOL_WIDTH + RET_WIDTH, POOL_WIDTH + 3 * RET_WIDTH)

    def step(kv_precise, full_from):
        x = x_ref[0]
        if moe_in:
            x = x + _unpack_row(ya_ref[0], yb_ref[0])
        hn = _rmsnorm(x, ng_ref[...])
        hi, lo = _split(hn) if kv_precise else (hn.astype(BF16), None)
        z_scr[...] = _dot(hi, win_hi_ref[...])
        if kv_precise and full_from:
            z_scr[:full_from, kv_cols] += (_dot(lo[:full_from], win_hi_ref[:, kv_cols])
                                           + _dot(hi[:full_from], win_lo_ref[:, kv_cols]))
        if full_from < rows:
            z_scr[full_from:, :] += (_dot(lo[full_from:], win_hi_ref[...])
                                     + _dot(hi[full_from:], win_lo_ref[...]))

        ubuf[0, POOL_HIST:POOL_HIST + rows, :] = z_scr[:, :POOL_WIDTH]
        pool_w = (wpool_hi_ref, wpool_lo_ref)
        pool_parts = []
        if full_from:
            pool_parts.append(_pool_mix(ubuf, full_from, c * rows, 0, pool_w, pscale_ref[...]))
        if full_from < rows:
            pool_parts.append(_pool_mix(ubuf, rows - full_from, c * rows + full_from, 0, pool_w,
                                        pscale_ref[...], precise=True, row0=full_from))
        pool_out = jnp.concatenate(pool_parts, axis=0)
        tail_ref[...] = ubuf[:, rows + POOL_HIST - POOL_BUF:rows + POOL_HIST, :]
        ubuf[:, 0:POOL_HIST, :] = ubuf[:, rows:rows + POOL_HIST, :]

        scale = RET_HEAD_DIM ** -0.5
        for ci in range(rows // CHUNK):
            rs = slice(ci * CHUNK, (ci + 1) * CHUNK)
            full = ci * CHUNK >= full_from
            cos2 = cos_ref[rs, :]
            sin2 = sin_ref[rs, :]
            for h in range(RET_HEADS):
                col = lambda part: slice(POOL_WIDTH + part * RET_WIDTH + h * RET_HEAD_DIM,
                                         POOL_WIDTH + part * RET_WIDTH + (h + 1) * RET_HEAD_DIM)
                q = _rope(z_scr[rs, col(0)], cos2, sin2)
                k = _rope(z_scr[rs, col(1)], cos2, sin2) * scale
                v = z_scr[rs, col(2)]
                s_old = s_scr[h]
                scores = _mm(q, k, full, nt=True) * dmat_ref[h]
                qd = q * _head(qdec_ref[...], h)
                o = _mm(scores, v, full) + _mm(qd, s_old, full)
                kd = k * _head(kdec_ref[...], h)
                s_scr[h] = s_old * cdec_ref[h] + _mm(kd.T, v, kv_precise)
                on = _group_norm(o) * _head(gn_ref[...], h)
                ret_scr[rs, h * RET_HEAD_DIM:(h + 1) * RET_HEAD_DIM] = _silu(z_scr[rs, col(3)]) * on

        mix_in = jnp.concatenate([pool_out, ret_scr[...]], axis=-1)
        y_ref[0] = x + _dot(mix_in.astype(BF16), wout_hi_ref[...])
        if full_from < rows:
            m_hi, m_lo = _split(mix_in[full_from:])
            y_ref[0, full_from:, :] += _dot(m_lo, wout_hi_ref[...]) + _dot(m_hi, wout_lo_ref[...])
        sfin_ref[0] = s_scr[...]
        _route_tile(y_ref[0], full_from, *route_in, *route_out, count_scr)

    if precise_tail:
        pl.when(c < steps - precise_tail)(lambda: step(False, rows))
        if precise_tail > 1:
            pl.when((c >= steps - precise_tail) & (c < steps - 1))(lambda: step(True, rows))
        pl.when(c == steps - 1)(lambda: step(True, rows - CHUNK))
    else:
        step(False, rows)


def _mixer_sample_kernel(*refs, seq_len, layer, has_prev):
    (x_ref, prev_ref, s0_ref, cos_ref, sin_ref, ng_ref, win_ref, wpool_ref, pscale_ref, gn_ref,
     wout_ref, dmat_ref, qdec_ref, kdec_ref, cdec_ref) = refs[:15]
    y_ref, tail_ref, sfin_ref, ubuf, ret_scr = refs[16 + 2 * has_prev:]
    if not has_prev:
        _zero_other_layers(tail_ref, layer)
        _zero_other_layers(sfin_ref, layer)
        tail_ref, sfin_ref = tail_ref.at[layer], sfin_ref.at[layer]
    ns = SAMPLE_SEQS
    rows = ns * seq_len
    x = x_ref[...].reshape(rows, D_MODEL)
    hn = _rmsnorm(x, ng_ref[...]).astype(BF16)
    z = _dot(hn, win_ref[...])
    for j in range(POOL_BUF):
        ubuf[:, POOL_HIST - POOL_BUF + j, :] = prev_ref[j]
    ubuf[:, POOL_HIST:POOL_HIST + seq_len, :] = z[:, :POOL_WIDTH].reshape(ns, seq_len, POOL_WIDTH)
    pool_out = _pool_mix(ubuf, seq_len, 0, POOL_BUF, (wpool_ref,), pscale_ref[...])
    for j in range(POOL_BUF):
        tail_ref[j] = ubuf[:, seq_len + POOL_HIST - POOL_BUF + j, :]

    q_all, k_all, v_all, g_all = _qkvg(z)
    scale = RET_HEAD_DIM ** -0.5
    cos2 = cos_ref[...]
    sin2 = sin_ref[...]
    tok_seq = lax.broadcasted_iota(I32, (RET_HEAD_DIM, rows), 1) // seq_len
    for h in range(RET_HEADS):
        q = _rope(_head(q_all, h), cos2, sin2)
        k = _rope(_head(k_all, h), cos2, sin2) * scale
        vb = _head(v_all, h).astype(BF16)
        s_old = s0_ref[:, h]
        scores = _dot_nt(q.astype(BF16), k.astype(BF16)) * dmat_ref[h]
        qd = (q * _head(qdec_ref[...], h)).astype(BF16).reshape(ns, seq_len, RET_HEAD_DIM)
        o_state = jnp.einsum('bid,bde->bie', qd, s_old.astype(BF16), preferred_element_type=F32)
        o = _dot(scores.astype(BF16), vb) + o_state.reshape(rows, RET_HEAD_DIM)
        kdt = (k * _head(kdec_ref[...], h)).T
        lhs = jnp.concatenate(
            [jnp.where(tok_seq == b, kdt, 0.0).astype(BF16) for b in range(ns)], axis=0)
        upd = _dot(lhs, vb).reshape(ns, RET_HEAD_DIM, RET_HEAD_DIM)
        sfin_ref[:, h] = s_old * cdec_ref[h] + upd
        on = _group_norm(o) * _head(gn_ref[...], h)
        ret_scr[:, h * RET_HEAD_DIM:(h + 1) * RET_HEAD_DIM] = _silu(_head(g_all, h)) * on

    mix_in = jnp.concatenate([pool_out, ret_scr[...]], axis=-1).astype(BF16)
    y_ref[...] = (x + _dot(mix_in, wout_ref[...])).reshape(ns, seq_len, D_MODEL)


def _router_logits(hn, wr_cat_ref, wr_hi_ref, br_ref, precise_from=0):
    parts = []
    if precise_from:
        parts.append(_dot(hn[:precise_from].astype(BF16), wr_hi_ref[...]))
    if precise_from < hn.shape[0]:
        h_hi, h_lo = _split(hn[precise_from:])
        part = _dot(h_hi, wr_cat_ref[...])
        parts.append(part[:, :LANES] + part[:, LANES:] + _dot(h_lo, wr_hi_ref[...]))
    return jnp.concatenate(parts, axis=0) + br_ref[...]


def _select(group_lg, expert_lg, gidx, eidx, axis):
    red = dict(axis=axis, keepdims=True)
    neg = jnp.float32(-jnp.inf)
    gmax = jnp.max(group_lg, **red)
    g_sel = jnp.min(jnp.where(group_lg == gmax, gidx, N_EXPERT_GROUPS), **red)
    p_sel = 1.0 / jnp.sum(jnp.exp(group_lg - gmax), **red)
    emask = (eidx >> GROUP_SHIFT) == g_sel
    v1 = jnp.max(jnp.where(emask, expert_lg, neg), **red)
    i1 = jnp.min(jnp.where(emask & (expert_lg == v1), eidx, N_EXPERTS), **red)
    emask2 = emask & (eidx != i1)
    v2 = jnp.max(jnp.where(emask2, expert_lg, neg), **red)
    i2 = jnp.min(jnp.where(emask2 & (expert_lg == v2), eidx, N_EXPERTS), **red)
    e2 = jnp.exp(v2 - v1)
    return g_sel, i1, i2, p_sel / (1.0 + e2), p_sel * e2 / (1.0 + e2)


def _route_tile(x, precise_from, ng_ref, wr_cat_ref, wr_hi_ref, br_ref, ha_ref, hbg_ref, idx_ref,
                count_ref, carry_scr):
    rows = x.shape[0]
    hn = _rmsnorm(x, ng_ref[...])
    ha_ref[...], hbg_ref[:, :PACK_W] = _pack_row(hn)
    lgt = _router_logits(hn, wr_cat_ref, wr_hi_ref, br_ref, precise_from).T
    neg = jnp.float32(-jnp.inf)
    gidx = lax.broadcasted_iota(I32, (SUBLANES, rows), 0)
    eidx = lax.broadcasted_iota(I32, (N_EXPERTS, rows), 0)
    group_lg = jnp.where(gidx < N_EXPERT_GROUPS, lgt[0:SUBLANES], neg)
    expert_lg = lgt[EXPERT_LANE0:EXPERT_LANE0 + N_EXPERTS]
    g_sel, i1, i2, w1, w2 = _select(group_lg, expert_lg, gidx, eidx, 0)
    lo = jnp.minimum(i1, i2) - g_sel * EXPERTS_PER_GROUP
    hi = jnp.maximum(i1, i2) - g_sel * EXPERTS_PER_GROUP
    pair = ((lo * (2 * EXPERTS_PER_GROUP - 1 - lo)) >> 1) + (hi - lo - 1)
    cls = g_sel * PAIRS_PER_GROUP + pair
    w_lo = jnp.where(i1 < i2, w1, w2)
    w_hi = jnp.where(i1 < i2, w2, w1)

    crow = lax.broadcasted_iota(I32, (CLASS_ROWS, rows), 0)
    onehot = jnp.where(crow == cls, 1.0, 0.0)
    n_blk = rows // COUNT_BLOCK
    blocks = [onehot[:, j * COUNT_BLOCK:(j + 1) * COUNT_BLOCK] for j in range(n_blk)]
    r = lax.broadcasted_iota(I32, (COUNT_BLOCK, COUNT_BLOCK), 0)
    c = lax.broadcasted_iota(I32, (COUNT_BLOCK, COUNT_BLOCK), 1)
    upper = jnp.where(r < c, 1.0, 0.0).astype(BF16)
    within = _dot(jnp.concatenate(blocks, axis=0).astype(BF16), upper)
    carry = carry_scr[:, 0:1]
    ranks = []
    for j in range(n_blk):
        before = within[j * CLASS_ROWS:(j + 1) * CLASS_ROWS] + carry
        ranks.append(jnp.sum(blocks[j] * before, axis=0, keepdims=True))
        carry = carry + jnp.sum(blocks[j], axis=1, keepdims=True)
    rank = jnp.concatenate(ranks, axis=1)
    carry_scr[...] = jnp.broadcast_to(carry, carry_scr.shape)
    count_ref[...] = carry_scr[...]

    row8 = lax.broadcasted_iota(I32, (SUBLANES, rows), 0)
    idx_ref[...] = jnp.where(row8 == 0, cls, jnp.where(row8 == 1, rank.astype(I32), 0))
    rowl = lax.broadcasted_iota(I32, (LANES, rows), 0)
    gate_rec = jnp.where(rowl == GATE_LO, w_lo, jnp.where(rowl == GATE_HI, w_hi, 0.0)).T
    hbg_ref[:, PACK_W:] = pltpu.bitcast(gate_rec, I32)


def _expert_cast_kernel(wg_ref, wu_ref, wd_ref, after_ref, wgu_ref, wdb_ref):
    del after_ref
    wgu_ref[:, :, :D_EXPERT] = wg_ref[...].astype(BF16)
    wgu_ref[:, :, D_EXPERT:] = wu_ref[...].astype(BF16)
    wdb_ref[...] = wd_ref[...].astype(BF16)


def _moe_pair_kernel(ea_ref, eb_ref, used_ref, ha_ref, hbg_ref, wgu_a_ref, wd_a_ref,
                     wgu_b_ref, wd_b_ref, after_ref, ya_ref, yb_ref):
    del ea_ref, eb_ref, after_ref

    @pl.when(pl.program_id(0) < used_ref[0])
    def _():
        h = _unpack_row(ha_ref[...], hbg_ref[:, :PACK_W]).astype(BF16)
        gates = pltpu.bitcast(hbg_ref[:, PACK_W:], F32)
        y = None
        for wgu_ref, wd_ref, lane in ((wgu_a_ref, wd_a_ref, GATE_LO), (wgu_b_ref, wd_b_ref, GATE_HI)):
            gu = _dot(h, wgu_ref[0])
            act = _silu(gu[:, :D_EXPERT]) * gu[:, D_EXPERT:]
            part = _dot((act * gates[:, lane:lane + 1]).astype(BF16), wd_ref[0])
            y = part if y is None else y + part
        ya_ref[...], yb_ref[...] = _pack_row(y)


def _final_norm_kernel(x_ref, ya_ref, yb_ref, g_ref, o_ref):
    o_ref[...] = _rmsnorm(x_ref[...] + _unpack_row(ya_ref[...], yb_ref[...]), g_ref[...])


def _moe_dense_kernel(x_ref, ng_ref, wr_cat_ref, wr_hi_ref, br_ref, wgu_ref, wd_ref, nf_ref,
                      after_ref, y_ref, h_scr, gate_scr, acc_scr, *, final_norm):
    del after_ref
    e = pl.program_id(1)

    @pl.when(e == 0)
    def _():
        hn = _rmsnorm(x_ref[...], ng_ref[...])
        h_scr[...] = hn.astype(BF16)
        lg = _router_logits(hn, wr_cat_ref, wr_hi_ref, br_ref)
        lane = lax.broadcasted_iota(I32, lg.shape, 1)
        neg = jnp.float32(-jnp.inf)
        group_lg = jnp.where(lane < N_EXPERT_GROUPS, lg, neg)
        is_expert = (lane >= EXPERT_LANE0) & (lane < EXPERT_LANE0 + N_EXPERTS)
        _, i1, i2, w1, w2 = _select(group_lg, jnp.where(is_expert, lg, neg), lane,
                                    jnp.where(is_expert, lane - EXPERT_LANE0, N_EXPERTS), 1)
        gate_scr[...] = (jnp.where(lane == i1 + EXPERT_LANE0, w1, 0.0)
                         + jnp.where(lane == i2 + EXPERT_LANE0, w2, 0.0))
        acc_scr[...] = jnp.zeros_like(acc_scr)

    lane = lax.broadcasted_iota(I32, gate_scr.shape, 1)
    gate = jnp.sum(jnp.where(lane == EXPERT_LANE0 + e, gate_scr[...], 0.0), axis=-1, keepdims=True)
    gu = _dot(h_scr[...], wgu_ref[0])
    act = _silu(gu[:, :D_EXPERT]) * gu[:, D_EXPERT:]
    acc_scr[...] += _dot((act * gate).astype(BF16), wd_ref[0])

    @pl.when(e == N_EXPERTS - 1)
    def _():
        y = x_ref[...] + acc_scr[...]
        if final_norm:
            y = _rmsnorm(y, nf_ref[...])
        y_ref[...] = y


def _const_spec(shape):
    return pl.BlockSpec(shape, lambda *_: (0,) * len(shape))


def _resident_spec(shape):
    return pl.BlockSpec(shape, lambda *_: (0,) * len(shape), pipeline_mode=pl.Buffered(1))


_ANY_SPEC = pl.BlockSpec(memory_space=pl.ANY)


def _rope_tables(pos):
    half = RET_HEAD_DIM // 2
    inv = np.float32(ROPE_BASE) ** (-np.arange(half, dtype=np.float32) / np.float32(half))
    ang = pos.astype(np.float32)[:, None] * inv[None, :]
    cos, sin = np.cos(ang), np.sin(ang)
    return np.concatenate([cos, cos], -1), np.concatenate([-sin, sin], -1)


def _decay_consts(c, reps):
    f = np.float32
    log_g = np.log(f(1.0) - f(2.0) ** (f(-5.0) - np.arange(RET_HEADS, dtype=f)))
    i = np.arange(c, dtype=f)
    diff = i[:, None] - i[None, :]
    dmat = np.where(diff[None] >= 0, np.exp(np.maximum(diff, f(0))[None] * log_g[:, None, None]), f(0))
    qdec = np.exp((i + f(1))[None, :] * log_g[:, None])
    kdec = np.exp((f(c) - f(1) - i)[None, :] * log_g[:, None])
    cdec = np.exp(f(c) * log_g)
    dmat = np.einsum('ab,hij->haibj', np.eye(reps, dtype=f), dmat).reshape(RET_HEADS, reps * c, reps * c)
    lanes = lambda t: np.repeat(np.tile(t, (1, reps)).T, RET_HEAD_DIM, axis=1)
    cdec = np.broadcast_to(cdec[:, None, None], (RET_HEADS, 1, RET_HEAD_DIM))
    return tuple(np.ascontiguousarray(a, dtype=f) for a in (dmat, lanes(qdec), lanes(kdec), cdec))


def _split_weight(w):
    hi = lax.bitcast_convert_type(_bf16_round_bits(lax.bitcast_convert_type(w, jnp.uint32)), F32)
    return hi.astype(BF16), (w - hi).astype(BF16)


_W_IN_BLOCK = (None, D_MODEL, IN_WIDTH)
_W_POOL_BLOCK = (None, len(POOL_WINDOWS), POOL_GROUP_DIM, POOL_GROUP_DIM)
_W_OUT_BLOCK = (None, D_MODEL, D_MODEL)
_DECAY_SPECS = [
    _const_spec((RET_HEADS, CHUNK, CHUNK)), _const_spec((CHUNK, RET_WIDTH)),
    _const_spec((CHUNK, RET_WIDTH)), _const_spec((RET_HEADS, 1, RET_HEAD_DIM)),
]


def _layer_spec(block, layer, resident=False):
    index_map = lambda *_: (layer,) + (0,) * (len(block) - 1)
    if resident:
        return pl.BlockSpec(block, index_map, pipeline_mode=pl.Buffered(1))
    return pl.BlockSpec(block, index_map)


def _mixer_weight_specs(layer, split):
    n = 2 if split else 1
    row = lambda width: _layer_spec((None, 1, width), layer)
    return [row(D_MODEL), *[_layer_spec(_W_IN_BLOCK, layer, True)] * n,
            *[_layer_spec(_W_POOL_BLOCK, layer)] * n, row(POOL_WIDTH), row(RET_WIDTH),
            *[_layer_spec(_W_OUT_BLOCK, layer, True)] * n]


def _chain_out_spec(block, depth, layer, has_prev):
    zeros = (0,) * (len(block) - 1)
    if has_prev:
        return pl.BlockSpec((None,) + block, lambda i, *_: (layer, i) + zeros)
    return pl.BlockSpec((depth,) + block, lambda i, *_: (0, i) + zeros)


def _mixer_prompt(x, moe_y, weights, route_w, prev_out, layer, rows, precise_tail, after):
    b, l, _ = x.shape
    t = b * l
    depth = weights[0].shape[0]
    assert l % rows == 0 and rows % CHUNK == 0
    cos2, sin2 = _rope_tables(np.arange(l))
    decay = _decay_consts(RET_CHUNK, 1)
    steps = l // rows
    tok = lambda i, c: (i, c, 0)
    if moe_y is None:
        ya = yb = jnp.zeros((1, rows, PACK_W), I32)
        y_spec = _const_spec((1, rows, PACK_W))
    else:
        ya, yb = (t.reshape(b, l, PACK_W) for t in moe_y)
        y_spec = pl.BlockSpec((1, rows, PACK_W), tok)
    has_prev = prev_out is not None
    n_in = 23
    flat = lambda i, c: (i * steps + c, 0)
    y, tails, states, *routed = pl.pallas_call(
        functools.partial(_mixer_prompt_kernel, rows=rows, steps=steps, moe_in=moe_y is not None,
                          precise_tail=precise_tail, layer=layer, has_prev=has_prev),
        grid=(b, steps),
        in_specs=[pl.BlockSpec((1, rows, D_MODEL), tok), y_spec, y_spec,
                  pl.BlockSpec((rows, RET_HEAD_DIM), lambda i, c: (c, 0)),
                  pl.BlockSpec((rows, RET_HEAD_DIM), lambda i, c: (c, 0)),
                  *_mixer_weight_specs(layer, True), *_DECAY_SPECS,
                  _layer_spec((None, 1, D_MODEL), layer), *_router_specs(layer), _ANY_SPEC,
                  *([_ANY_SPEC, _ANY_SPEC] if has_prev else [])],
        out_specs=[pl.BlockSpec((1, rows, D_MODEL), tok),
                   _chain_out_spec((1, POOL_BUF, POOL_WIDTH), depth, layer, has_prev),
                   _chain_out_spec((1, RET_HEADS, RET_HEAD_DIM, RET_HEAD_DIM), depth, layer, has_prev),
                   pl.BlockSpec((rows, PACK_W), flat), pl.BlockSpec((rows, PACK_W + LANES), flat),
                   pl.BlockSpec((SUBLANES, rows), lambda i, c: (0, i * steps + c)),
                   _const_spec((CLASS_ROWS, LANES))],
        out_shape=[jax.ShapeDtypeStruct(x.shape, F32),
                   jax.ShapeDtypeStruct((depth, b, POOL_BUF, POOL_WIDTH), F32),
                   jax.ShapeDtypeStruct((depth, b, RET_HEADS, RET_HEAD_DIM, RET_HEAD_DIM), F32),
                   jax.ShapeDtypeStruct((t, PACK_W), I32),
                   jax.ShapeDtypeStruct((t, PACK_W + LANES), I32), jax.ShapeDtypeStruct((SUBLANES, t), I32),
                   jax.ShapeDtypeStruct((CLASS_ROWS, LANES), F32)],
        input_output_aliases={n_in: 1, n_in + 1: 2} if has_prev else {},
        scratch_shapes=[pltpu.VMEM((1, POOL_HIST + rows, POOL_WIDTH), F32),
                        pltpu.VMEM((RET_HEADS, RET_HEAD_DIM, RET_HEAD_DIM), F32),
                        pltpu.VMEM((rows, RET_WIDTH), F32),
                        pltpu.VMEM((rows, IN_WIDTH), F32),
                        pltpu.VMEM((CLASS_ROWS, LANES), F32)],
        compiler_params=pltpu.CompilerParams(
            dimension_semantics=("arbitrary", "arbitrary"), vmem_limit_bytes=VMEM_LIMIT),
        name="mixer_prompt",
    )(x, ya, yb, cos2, sin2, *weights, *decay, *route_w, after, *(prev_out if has_prev else ()))
    return y, (tails, states), tuple(routed)


def _mixer_sample(x, pool_prev, s0, weights, prev_out, layer, after):
    b, l, _ = x.shape
    depth = s0.shape[0]
    ns = SAMPLE_SEQS
    assert ns * l == CHUNK and b % ns == 0
    cos2, sin2 = _rope_tables(PAST_LEN + np.arange(l))
    cos2, sin2 = np.tile(cos2, (ns, 1)), np.tile(sin2, (ns, 1))
    decay = _decay_consts(l, ns)
    seq3 = lambda i: (i, 0, 0)
    state_block = (ns, RET_HEADS, RET_HEAD_DIM, RET_HEAD_DIM)
    tail_block = (POOL_BUF, ns, POOL_WIDTH)
    if prev_out is None:
        tail_out = pl.BlockSpec((depth,) + tail_block, lambda i: (0, 0, i, 0))
    else:
        tail_out = pl.BlockSpec((None,) + tail_block, lambda i: (layer, 0, i, 0))
    has_prev = prev_out is not None
    n_in = 16
    y, tails, states = pl.pallas_call(
        functools.partial(_mixer_sample_kernel, seq_len=l, layer=layer, has_prev=has_prev),
        grid=(b // ns,),
        in_specs=[pl.BlockSpec((ns, l, D_MODEL), seq3),
                  pl.BlockSpec((None,) + tail_block, lambda i: (layer, 0, i, 0)),
                  pl.BlockSpec((None,) + state_block, lambda i: (layer, i, 0, 0, 0)),
                  _const_spec((CHUNK, RET_HEAD_DIM)), _const_spec((CHUNK, RET_HEAD_DIM)),
                  *_mixer_weight_specs(layer, False), *_DECAY_SPECS, _ANY_SPEC,
                  *([_ANY_SPEC, _ANY_SPEC] if has_prev else [])],
        out_specs=[pl.BlockSpec((ns, l, D_MODEL), seq3), tail_out,
                   _chain_out_spec(state_block, depth, layer, has_prev)],
        out_shape=[jax.ShapeDtypeStruct(x.shape, F32),
                   jax.ShapeDtypeStruct((depth, POOL_BUF, b, POOL_WIDTH), F32),
                   jax.ShapeDtypeStruct(s0.shape, F32)],
        input_output_aliases={n_in: 1, n_in + 1: 2} if has_prev else {},
        scratch_shapes=[pltpu.VMEM((ns, POOL_HIST + l, POOL_WIDTH), F32),
                        pltpu.VMEM((CHUNK, RET_WIDTH), F32)],
        compiler_params=pltpu.CompilerParams(
            dimension_semantics=("arbitrary",), vmem_limit_bytes=VMEM_LIMIT),
        name="mixer_sample",
    )(x, pool_prev, s0, cos2, sin2, *weights, *decay, after, *(prev_out if has_prev else ()))
    return y, (tails, states)


def _router_weights(w_rg, b_rg, w_re, b_re):
    depth = w_rg.shape[0]
    gap = EXPERT_LANE0 - N_EXPERT_GROUPS
    rest = LANES - EXPERT_LANE0 - N_EXPERTS
    wr = jnp.concatenate([w_rg, jnp.zeros((depth, D_MODEL, gap), F32), w_re,
                          jnp.zeros((depth, D_MODEL, rest), F32)], axis=-1)
    br = jnp.concatenate([b_rg, jnp.zeros((depth, gap), F32), b_re, jnp.zeros((depth, rest), F32)],
                         axis=-1).reshape(depth, 1, LANES)
    wr_hi, wr_lo = _split_weight(wr)
    return jnp.concatenate([wr_hi, wr_lo], axis=-1), wr_hi, br


def _router_specs(layer):
    return [_layer_spec((None, D_MODEL, 2 * LANES), layer), _layer_spec((None, D_MODEL, LANES), layer),
            _layer_spec((None, 1, LANES), layer)]


def _sc_mesh():
    return plsc.VectorSubcoreMesh(core_axis_name="core", subcore_axis_name="subcore",
                                  num_cores=SC_CORES, num_subcores=SC_SUBCORES)


def _sc_params():
    params = pltpu.CompilerParams()
    if "needs_layout_passes" in pltpu.CompilerParams.__dataclass_fields__:
        params = dataclasses.replace(params, needs_layout_passes=False)
    return params


def _sc_gather(tables, idx, after):
    n = idx.shape[0]
    assert n % SC_WINDOW == 0
    nt = len(tables)

    def body(*refs):
        i_hbm = refs[nt]
        for t_hbm, o_hbm in zip(refs[:nt], refs[nt + 2:]):
            def gather_window(i_vmem, o_vmem, t_hbm=t_hbm):
                pltpu.sync_copy(t_hbm.at[i_vmem.at[0]], o_vmem)

            pltpu.emit_pipeline(
                gather_window, grid=(n // SC_WINDOW,),
                in_specs=[pl.BlockSpec((1, SC_WINDOW), lambda i: (0, i))],
                out_specs=[pl.BlockSpec((SC_WINDOW, t_hbm.shape[1]), lambda i: (i, 0))],
                core_axis_name=("core", "subcore"),
                dimension_semantics=(pltpu.PARALLEL,),
            )(i_hbm, o_hbm)

    out_type = tuple(jax.ShapeDtypeStruct((n, t.shape[1]), t.dtype) for t in tables)
    return pl.kernel(body, out_type=out_type, mesh=_sc_mesh(), name="sc_gather")(
        *tables, idx.reshape(1, n), after)


def _sc_slots(cls, rank, starts, n_slots):
    t = cls.shape[0]
    workers = SC_CORES * SC_SUBCORES
    slot_per, tok_per = n_slots // workers, t // workers
    assert n_slots % (workers * SC_LANES) == 0 and t % (workers * SC_LANES) == 0 and t & (t - 1) == 0
    assert n_slots % (SC_LANES * SC_UNROLL) == 0 and t % (SC_LANES * SC_UNROLL) == 0

    def body(cls_hbm, rank_hbm, starts_hbm, pos_hbm, slot_hbm, cls_v, rank_v, starts_v, pos_v, slot_v):
        wid = lax.axis_index("subcore") * SC_CORES + lax.axis_index("core")
        pltpu.sync_copy(cls_hbm, cls_v)
        pltpu.sync_copy(rank_hbm, rank_v)
        pltpu.sync_copy(starts_hbm, starts_v)

        lane = lax.iota(I32, SC_LANES)
        span = SC_LANES * SC_UNROLL

        @pl.loop(0, n_slots, step=span)
        def _(i):
            for u in range(SC_UNROLL):
                j = i + u * SC_LANES
                slot_v[pl.ds(j, SC_LANES)] = (lane + j) & (t - 1)

        @pl.loop(0, t, step=span)
        def _(i):
            for u in range(SC_UNROLL):
                j = i + u * SC_LANES
                at = pl.ds(j, SC_LANES)
                pos = plsc.load_gather(starts_v, [cls_v[at]]) + rank_v[at]
                pos_v[at] = pos
                plsc.store_scatter(slot_v, [pos], lane + j)

        tok_off = pl.multiple_of(wid * tok_per, SC_LANES)
        pltpu.sync_copy(pos_v.at[pl.ds(tok_off, tok_per)], pos_hbm.at[pl.ds(tok_off, tok_per)])
        slot_off = pl.multiple_of(wid * slot_per, SC_LANES)
        pltpu.sync_copy(slot_v.at[pl.ds(slot_off, slot_per)], slot_hbm.at[pl.ds(slot_off, slot_per)])

    return pl.kernel(
        body, mesh=_sc_mesh(), compiler_params=_sc_params(), name="sc_slots",
        out_type=(jax.ShapeDtypeStruct((t,), I32), jax.ShapeDtypeStruct((n_slots,), I32)),
        scratch_types=[pltpu.VMEM((t,), I32), pltpu.VMEM((t,), I32), pltpu.VMEM((CLASS_ROWS,), I32),
                       pltpu.VMEM((t,), I32), pltpu.VMEM((n_slots,), I32)],
    )(cls, rank, starts)


def _moe_experts(routed, wgu, wd, after):
    ha, hbg, idx, counts = routed
    t = ha.shape[0]
    n_slots = t + N_CLASSES * PAIR_TILE
    n_tiles = n_slots // PAIR_TILE

    cnt = counts[:, 0].astype(I32)
    padded = (cnt + PAIR_TILE - 1) // PAIR_TILE * PAIR_TILE
    ends = jnp.cumsum(padded)
    starts = ends - padded
    tile_cls = jnp.minimum(
        jnp.sum(ends[None, :N_CLASSES] <= (jnp.arange(n_tiles, dtype=I32) * PAIR_TILE)[:, None], axis=1),
        N_CLASSES - 1).astype(I32)
    first = (tile_cls // PAIRS_PER_GROUP) * EXPERTS_PER_GROUP
    tile_ea = first + jnp.asarray(PAIR_LO, I32)[tile_cls % PAIRS_PER_GROUP]
    tile_eb = first + jnp.asarray(PAIR_HI, I32)[tile_cls % PAIRS_PER_GROUP]
    used = (ends[N_CLASSES - 1:N_CLASSES] // PAIR_TILE).astype(I32)

    pos, slot_tok = _sc_slots(idx[0], idx[1], starts, n_slots)
    hsa, hsbg = _sc_gather((ha, hbg), slot_tok, after=wgu)

    row = lambda i, ea, eb, nu: (jnp.minimum(i, nu[0] - 1), 0)
    w_spec = lambda shape, which: pl.BlockSpec(
        (1,) + shape, lambda i, ea, eb, nu: ((ea, eb)[which][i], 0, 0))
    gu_shape, d_shape = (D_MODEL, 2 * D_EXPERT), (D_EXPERT, D_MODEL)
    ysa, ysb = pl.pallas_call(
        _moe_pair_kernel,
        grid_spec=pltpu.PrefetchScalarGridSpec(
            num_scalar_prefetch=3, grid=(n_tiles,),
            in_specs=[pl.BlockSpec((PAIR_TILE, PACK_W), row),
                      pl.BlockSpec((PAIR_TILE, PACK_W + LANES), row),
                      w_spec(gu_shape, 0), w_spec(d_shape, 0), w_spec(gu_shape, 1), w_spec(d_shape, 1),
                      _ANY_SPEC],
            out_specs=[pl.BlockSpec((PAIR_TILE, PACK_W), row), pl.BlockSpec((PAIR_TILE, PACK_W), row)]),
        out_shape=[jax.ShapeDtypeStruct((n_slots, PACK_W), I32),
                   jax.ShapeDtypeStruct((n_slots, PACK_W), I32)],
        compiler_params=pltpu.CompilerParams(
            dimension_semantics=("arbitrary",), vmem_limit_bytes=VMEM_LIMIT),
        name="moe_pair",
    )(tile_ea, tile_eb, used, hsa, hsbg, wgu, wd, wgu, wd, after)
    return (ysa, ysb), pos


def _expert_cast(w_gate, w_up, w_down, layer, after):
    per = CAST_EXPERTS
    w_spec = lambda shape: pl.BlockSpec((None, per) + shape, lambda e: (layer, e, 0, 0))
    out = lambda shape: pl.BlockSpec((per,) + shape, lambda e: (e, 0, 0))
    return pl.pallas_call(
        _expert_cast_kernel,
        grid=(N_EXPERTS // per,),
        in_specs=[w_spec((D_MODEL, D_EXPERT)), w_spec((D_MODEL, D_EXPERT)), w_spec((D_EXPERT, D_MODEL)),
                  _ANY_SPEC],
        out_specs=[out((D_MODEL, 2 * D_EXPERT)), out((D_EXPERT, D_MODEL))],
        out_shape=[jax.ShapeDtypeStruct((N_EXPERTS, D_MODEL, 2 * D_EXPERT), BF16),
                   jax.ShapeDtypeStruct((N_EXPERTS, D_EXPERT, D_MODEL), BF16)],
        compiler_params=pltpu.CompilerParams(
            dimension_semantics=("arbitrary",), vmem_limit_bytes=VMEM_LIMIT),
        name="expert_cast",
    )(w_gate, w_up, w_down, after)


def _final_norm(x, moe_y, g, rows):
    t = x.shape[0]
    rows = min(rows, t)
    assert t % rows == 0
    tok = lambda i: (i, 0)
    return pl.pallas_call(
        _final_norm_kernel,
        grid=(t // rows,),
        in_specs=[pl.BlockSpec((rows, D_MODEL), tok), pl.BlockSpec((rows, PACK_W), tok),
                  pl.BlockSpec((rows, PACK_W), tok), _const_spec((1, D_MODEL))],
        out_specs=pl.BlockSpec((rows, D_MODEL), tok),
        out_shape=jax.ShapeDtypeStruct(x.shape, F32),
        compiler_params=pltpu.CompilerParams(
            dimension_semantics=("arbitrary",), vmem_limit_bytes=VMEM_LIMIT),
        name="final_norm",
    )(x, *moe_y, g.reshape(1, D_MODEL))


def _moe_dense(x, norm_g, router, wgu, wd, norm_final, layer, final_norm, rows, after):
    t = x.shape[0]
    assert t % rows == 0
    tok = lambda i, e: (i, 0)
    w_spec = lambda shape: pl.BlockSpec((1,) + shape, lambda i, e: (e, 0, 0))
    return pl.pallas_call(
        functools.partial(_moe_dense_kernel, final_norm=final_norm),
        grid=(t // rows, N_EXPERTS),
        in_specs=[pl.BlockSpec((rows, D_MODEL), tok), _layer_spec((None, 1, D_MODEL), layer),
                  *_router_specs(layer),
                  w_spec((D_MODEL, 2 * D_EXPERT)), w_spec((D_EXPERT, D_MODEL)),
                  _const_spec((1, D_MODEL)), _ANY_SPEC],
        out_specs=pl.BlockSpec((rows, D_MODEL), tok),
        out_shape=jax.ShapeDtypeStruct(x.shape, F32),
        scratch_shapes=[pltpu.VMEM((rows, D_MODEL), BF16),
                        pltpu.VMEM((rows, LANES), F32),
                        pltpu.VMEM((rows, D_MODEL), F32)],
        compiler_params=pltpu.CompilerParams(
            dimension_semantics=("arbitrary", "arbitrary"), vmem_limit_bytes=VMEM_LIMIT),
        name="moe_dense",
    )(x, norm_g, *router, wgu, wd, norm_final.reshape(1, D_MODEL), after)


def kernel(x_prompt, x_sample, cache_pool, state_ret, norm_mix, w_in, w_pool, pool_scale, ret_gn, w_out, norm_ffn, w_router_group, b_router_group, w_router_expert, b_router_expert, w_gate, w_up, w_down, norm_final):
    depth = norm_mix.shape[0]
    row = lambda a: a.reshape(depth, 1, a.shape[-1])
    mix_split = (row(norm_mix), *_split_weight(w_in), *_split_weight(w_pool), row(pool_scale),
                 row(ret_gn), *_split_weight(w_out))
    mix_hi = tuple(mix_split[i] for i in (0, 1, 3, 5, 6, 7))
    router = _router_weights(w_router_group, b_router_group, w_router_expert, b_router_expert)
    norm_ffn = row(norm_ffn)
    pool_prev = jnp.swapaxes(cache_pool, 1, 2)

    yp, ys = x_prompt, x_sample
    moe_p = None
    out_p = out_s = None
    for l in range(depth):
        yp, out_p, routed = _mixer_prompt(
            yp, moe_p, mix_split, (norm_ffn, *router), out_p, l, rows=512,
            precise_tail=PRECISE_TAIL_STEPS if l < depth - 1 else 0, after=ys)
        wgu, wd = _expert_cast(w_gate, w_up, w_down, l, after=routed[-1])
        ys, out_s = _mixer_sample(ys, pool_prev, state_ret, mix_hi, out_s, l, after=wgu)
        sorted_y, pos = _moe_experts(routed, wgu, wd, after=ys)
        ys = _moe_dense(ys.reshape(-1, D_MODEL), norm_ffn, router, wgu, wd, norm_final,
                        l, l == depth - 1, rows=1024, after=sorted_y[0]).reshape(ys.shape)
        moe_p = _sc_gather(sorted_y, pos, after=routed[-1])
    yp = _final_norm(yp.reshape(-1, D_MODEL), moe_p, norm_final, rows=2048).reshape(yp.shape)
    return (yp, ys, *out_p, jnp.swapaxes(out_s[0], 1, 2), out_s[1])
```

```python
import dataclasses
import functools
import itertools

import jax
import jax.numpy as jnp
import numpy as np
from jax import lax
from jax.experimental import pallas as pl
from jax.experimental.pallas import tpu as pltpu
from jax.experimental.pallas import tpu_sc as plsc

F32 = jnp.float32
BF16 = jnp.bfloat16
I32 = jnp.int32

D_MODEL = 1024
POOL_WIDTH = 512
POOL_WINDOWS = (2, 4, 8, 16)
POOL_GROUP_DIM = 128
POOL_BUF = 15
POOL_HIST = 16
RET_WIDTH = 512
RET_HEADS = 4
RET_HEAD_DIM = 128
RET_CHUNK = 128
ROPE_BASE = 10000.0
IN_WIDTH = POOL_WIDTH + 4 * RET_WIDTH
N_EXPERT_GROUPS = 4
EXPERTS_PER_GROUP = 4
N_EXPERTS = N_EXPERT_GROUPS * EXPERTS_PER_GROUP
D_EXPERT = 256
RMS_EPS = 1e-6
GN_EPS = 1e-5
PAST_LEN = 16384

LANES = 128
SUBLANES = 8
EXPERT_LANE0 = 8
GROUP_SHIFT = EXPERTS_PER_GROUP.bit_length() - 1
PAIRS = tuple(itertools.combinations(range(EXPERTS_PER_GROUP), 2))
PAIRS_PER_GROUP = len(PAIRS)
PAIR_LO, PAIR_HI = zip(*PAIRS)
N_CLASSES = N_EXPERT_GROUPS * PAIRS_PER_GROUP
CLASS_ROWS = 32
GATE_LO, GATE_HI = 0, 1
COUNT_BLOCK = 256
PAIR_TILE = 256
PACK_W = D_MODEL // 4
SC_CORES, SC_SUBCORES, SC_LANES = 2, 16, 16
SC_WINDOW = 128
SC_UNROLL = 8
PRECISE_TAIL_STEPS = 1
CHUNK = 128
SAMPLE_SEQS = 16
VMEM_LIMIT = 56 * 1024 * 1024


def _dot(a, b):
    return jnp.dot(a, b, preferred_element_type=F32)


def _dot_nt(a, b):
    return lax.dot_general(a, b, (((1,), (1,)), ((), ())), preferred_element_type=F32)


def _bf16_round_bits(u):
    return (u + jnp.uint32(0x7FFF) + ((u >> 16) & jnp.uint32(1))) & jnp.uint32(0xFFFF0000)


def _split(a):
    hi = pltpu.bitcast(_bf16_round_bits(pltpu.bitcast(a, jnp.uint32)), F32)
    return hi.astype(BF16), (a - hi).astype(BF16)


def _mm(a, b, precise, nt=False):
    dot = _dot_nt if nt else _dot
    if precise:
        b_hi, b_lo = b if isinstance(b, tuple) else _split(b)
        a_hi, a_lo = _split(a)
        return dot(a_hi, b_hi) + dot(a_lo, b_hi) + dot(a_hi, b_lo)
    return dot(a.astype(BF16), b[0] if isinstance(b, tuple) else b.astype(BF16))


def _rmsnorm(x, g):
    ms = jnp.mean(x * x, axis=-1, keepdims=True)
    return x * lax.rsqrt(ms + RMS_EPS) * g


def _pool_mix(ubuf, rows, t_first, n_prev, wpool_refs, pscale, precise=False, row0=0):
    ns = ubuf.shape[0]
    t = t_first + lax.broadcasted_iota(I32, (1, rows, POOL_GROUP_DIM), 1)
    base = POOL_HIST + row0
    outs = []
    for j, w in enumerate(POOL_WINDOWS):
        lanes = slice(j * POOL_GROUP_DIM, (j + 1) * POOL_GROUP_DIM)
        uj = ubuf[:, base:base + rows, lanes]
        acc = uj
        for i in range(1, w):
            acc = acc + ubuf[:, base - i:base - i + rows, lanes]
        cnt = jnp.minimum(w, n_prev + t + 1).astype(F32)
        d = (acc / cnt - uj).reshape(ns * rows, POOL_GROUP_DIM)
        outs.append(_mm(d, tuple(w[j] for w in wpool_refs), precise))
    return jnp.concatenate(outs, axis=-1) * pscale


def _rope(xh, cos2, sin2):
    return xh * cos2 + pltpu.roll(xh, RET_HEAD_DIM // 2, 1) * sin2


def _group_norm(o):
    mu = jnp.mean(o, axis=-1, keepdims=True)
    c = o - mu
    var = jnp.mean(c * c, axis=-1, keepdims=True)
    return c * lax.rsqrt(var + GN_EPS)


def _silu(x):
    return x * (1.0 / (1.0 + jnp.exp(-x)))


def _head(a, h):
    return a[:, h * RET_HEAD_DIM:(h + 1) * RET_HEAD_DIM]


def _qkvg(z):
    p, r = POOL_WIDTH, RET_WIDTH
    return z[:, p:p + r], z[:, p + r:p + 2 * r], z[:, p + 2 * r:p + 3 * r], z[:, p + 3 * r:p + 4 * r]


def _pack_bf16_pair(a, b):
    ua = pltpu.bitcast(a.astype(BF16).astype(F32), jnp.uint32)
    ub = pltpu.bitcast(b.astype(BF16).astype(F32), jnp.uint32)
    return pltpu.bitcast((ua >> 16) | (ub & jnp.uint32(0xFFFF0000)), I32)


def _unpack_bf16_pair(w):
    u = pltpu.bitcast(w, jnp.uint32)
    return pltpu.bitcast(u << 16, F32), pltpu.bitcast(u & jnp.uint32(0xFFFF0000), F32)


def _pack_row(y):
    q = PACK_W
    return _pack_bf16_pair(y[:, 0:q], y[:, q:2 * q]), _pack_bf16_pair(y[:, 2 * q:3 * q], y[:, 3 * q:])


def _unpack_row(wa, wb):
    return jnp.concatenate([*_unpack_bf16_pair(wa), *_unpack_bf16_pair(wb)], axis=-1)


def _zero_other_layers(ref, layer):
    for j in range(ref.shape[0]):
        if j != layer:
            ref[j] = jnp.zeros(ref.shape[1:], ref.dtype)


def _mixer_prompt_kernel(*refs, rows, steps, moe_in, precise_tail, layer, has_prev):
    (x_ref, ya_ref, yb_ref, cos_ref, sin_ref, ng_ref, win_hi_ref, win_lo_ref, wpool_hi_ref,
     wpool_lo_ref, pscale_ref, gn_ref, wout_hi_ref, wout_lo_ref, dmat_ref, qdec_ref, kdec_ref,
     cdec_ref) = refs[:18]
    route_in = refs[18:22]
    y_ref, tail_ref, sfin_ref, *route_out, ubuf, s_scr, ret_scr, z_scr, count_scr = (
        refs[23 + 2 * has_prev:])
    c = pl.program_id(1)

    @pl.when((pl.program_id(0) == 0) & (c == 0))
    def _():
        count_scr[...] = jnp.zeros_like(count_scr)
    if not has_prev:
        _zero_other_layers(tail_ref, layer)
        _zero_other_layers(sfin_ref, layer)
        tail_ref, sfin_ref = tail_ref.at[layer], sfin_ref.at[layer]

    @pl.when(c == 0)
    def _():
        ubuf[:, 0:POOL_HIST, :] = jnp.zeros((1, POOL_HIST, POOL_WIDTH), F32)
        s_scr[...] = jnp.zeros_like(s_scr)

    kv_cols = slice(POOL_WIDTH + RET_WIDTH, POOL_WIDTH + 3 * RET_WIDTH)

    def step(kv_precise, full_from):
        x = x_ref[0]
        if moe_in:
            x = x + _unpack_row(ya_ref[0], yb_ref[0])
        hn = _rmsnorm(x, ng_ref[...])
        hi, lo = _split(hn) if kv_precise else (hn.astype(BF16), None)
        z_scr[...] = _dot(hi, win_hi_ref[...])
        if kv_precise and full_from:
            z_scr[:full_from, kv_cols] += (_dot(lo[:full_from], win_hi_ref[:, kv_cols])
                                           + _dot(hi[:full_from], win_lo_ref[:, kv_cols]))
        if full_from < rows:
            z_scr[full_from:, :] += (_dot(lo[full_from:], win_hi_ref[...])
                                     + _dot(hi[full_from:], win_lo_ref[...]))

        ubuf[0, POOL_HIST:POOL_HIST + rows, :] = z_scr[:, :POOL_WIDTH]
        pool_w = (wpool_hi_ref, wpool_lo_ref)
        pool_parts = []
        if full_from:
            pool_parts.append(_pool_mix(ubuf, full_from, c * rows, 0, pool_w, pscale_ref[...]))
        if full_from < rows:
            pool_parts.append(_pool_mix(ubuf, rows - full_from, c * rows + full_from, 0, pool_w,
                                        pscale_ref[...], precise=True, row0=full_from))
        pool_out = jnp.concatenate(pool_parts, axis=0)
        tail_ref[...] = ubuf[:, rows + POOL_HIST - POOL_BUF:rows + POOL_HIST, :]
        ubuf[:, 0:POOL_HIST, :] = ubuf[:, rows:rows + POOL_HIST, :]

        scale = RET_HEAD_DIM ** -0.5
        for ci in range(rows // CHUNK):
            rs = slice(ci * CHUNK, (ci + 1) * CHUNK)
            full = ci * CHUNK >= full_from
            cos2 = cos_ref[rs, :]
            sin2 = sin_ref[rs, :]
            for h in range(RET_HEADS):
                col = lambda part: slice(POOL_WIDTH + part * RET_WIDTH + h * RET_HEAD_DIM,
                                         POOL_WIDTH + part * RET_WIDTH + (h + 1) * RET_HEAD_DIM)
                q = _rope(z_scr[rs, col(0)], cos2, sin2)
                k = _rope(z_scr[rs, col(1)], cos2, sin2) * scale
                v = z_scr[rs, col(2)]
                s_old = s_scr[h]
                scores = _mm(q, k, full, nt=True) * dmat_ref[h]
                qd = q * _head(qdec_ref[...], h)
                o = _mm(scores, v, full) + _mm(qd, s_old, full)
                kd = k * _head(kdec_ref[...], h)
                s_scr[h] = s_old * cdec_ref[h] + _mm(kd.T, v, kv_precise)
                on = _group_norm(o) * _head(gn_ref[...], h)
                ret_scr[rs, h * RET_HEAD_DIM:(h + 1) * RET_HEAD_DIM] = _silu(z_scr[rs, col(3)]) * on

        mix_in = jnp.concatenate([pool_out, ret_scr[...]], axis=-1)
        y_ref[0] = x + _dot(mix_in.astype(BF16), wout_hi_ref[...])
        if full_from < rows:
            m_hi, m_lo = _split(mix_in[full_from:])
            y_ref[0, full_from:, :] += _dot(m_lo, wout_hi_ref[...]) + _dot(m_hi, wout_lo_ref[...])
        sfin_ref[0] = s_scr[...]
        _route_tile(y_ref[0], full_from, *route_in, *route_out, count_scr)

    if precise_tail:
        pl.when(c < steps - precise_tail)(lambda: step(False, rows))
        if precise_tail > 1:
            pl.when((c >= steps - precise_tail) & (c < steps - 1))(lambda: step(True, rows))
        pl.when(c == steps - 1)(lambda: step(True, rows - CHUNK))
    else:
        step(False, rows)


def _mixer_sample_kernel(*refs, seq_len, layer, has_prev):
    (x_ref, prev_ref, s0_ref, cos_ref, sin_ref, ng_ref, win_ref, wpool_ref, pscale_ref, gn_ref,
     wout_ref, dmat_ref, qdec_ref, kdec_ref, cdec_ref) = refs[:15]
    y_ref, tail_ref, sfin_ref, ubuf, ret_scr = refs[16 + 2 * has_prev:]
    if not has_prev:
        _zero_other_layers(tail_ref, layer)
        _zero_other_layers(sfin_ref, layer)
        tail_ref, sfin_ref = tail_ref.at[layer], sfin_ref.at[layer]
    ns = SAMPLE_SEQS
    rows = ns * seq_len
    x = x_ref[...].reshape(rows, D_MODEL)
    hn = _rmsnorm(x, ng_ref[...]).astype(BF16)
    z = _dot(hn, win_ref[...])
    for j in range(POOL_BUF):
        ubuf[:, POOL_HIST - POOL_BUF + j, :] = prev_ref[j]
    ubuf[:, POOL_HIST:POOL_HIST + seq_len, :] = z[:, :POOL_WIDTH].reshape(ns, seq_len, POOL_WIDTH)
    pool_out = _pool_mix(ubuf, seq_len, 0, POOL_BUF, (wpool_ref,), pscale_ref[...])
    for j in range(POOL_BUF):
        tail_ref[j] = ubuf[:, seq_len + POOL_HIST - POOL_BUF + j, :]

    q_all, k_all, v_all, g_all = _qkvg(z)
    scale = RET_HEAD_DIM ** -0.5
    cos2 = cos_ref[...]
    sin2 = sin_ref[...]
    tok_seq = lax.broadcasted_iota(I32, (RET_HEAD_DIM, rows), 1) // seq_len
    for h in range(RET_HEADS):
        q = _rope(_head(q_all, h), cos2, sin2)
        k = _rope(_head(k_all, h), cos2, sin2) * scale
        vb = _head(v_all, h).astype(BF16)
        s_old = s0_ref[:, h]
        scores = _dot_nt(q.astype(BF16), k.astype(BF16)) * dmat_ref[h]
        qd = (q * _head(qdec_ref[...], h)).astype(BF16).reshape(ns, seq_len, RET_HEAD_DIM)
        o_state = jnp.einsum('bid,bde->bie', qd, s_old.astype(BF16), preferred_element_type=F32)
        o = _dot(scores.astype(BF16), vb) + o_state.reshape(rows, RET_HEAD_DIM)
        kdt = (k * _head(kdec_ref[...], h)).T
        lhs = jnp.concatenate(
            [jnp.where(tok_seq == b, kdt, 0.0).astype(BF16) for b in range(ns)], axis=0)
        upd = _dot(lhs, vb).reshape(ns, RET_HEAD_DIM, RET_HEAD_DIM)
        sfin_ref[:, h] = s_old * cdec_ref[h] + upd
        on = _group_norm(o) * _head(gn_ref[...], h)
        ret_scr[:, h * RET_HEAD_DIM:(h + 1) * RET_HEAD_DIM] = _silu(_head(g_all, h)) * on

    mix_in = jnp.concatenate([pool_out, ret_scr[...]], axis=-1).astype(BF16)
    y_ref[...] = (x + _dot(mix_in, wout_ref[...])).reshape(ns, seq_len, D_MODEL)


def _router_logits(hn, wr_cat_ref, wr_hi_ref, br_ref, precise_from=0):
    parts = []
    if precise_from:
        parts.append(_dot(hn[:precise_from].astype(BF16), wr_hi_ref[...]))
    if precise_from < hn.shape[0]:
        h_hi, h_lo = _split(hn[precise_from:])
        part = _dot(h_hi, wr_cat_ref[...])
        parts.append(part[:, :LANES] + part[:, LANES:] + _dot(h_lo, wr_hi_ref[...]))
    return jnp.concatenate(parts, axis=0) + br_ref[...]


def _select(group_lg, expert_lg, gidx, eidx, axis):
    red = dict(axis=axis, keepdims=True)
    neg = jnp.float32(-jnp.inf)
    gmax = jnp.max(group_lg, **red)
    g_sel = jnp.min(jnp.where(group_lg == gmax, gidx, N_EXPERT_GROUPS), **red)
    p_sel = 1.0 / jnp.sum(jnp.exp(group_lg - gmax), **red)
    emask = (eidx >> GROUP_SHIFT) == g_sel
    v1 = jnp.max(jnp.where(emask, expert_lg, neg), **red)
    i1 = jnp.min(jnp.where(emask & (expert_lg == v1), eidx, N_EXPERTS), **red)
    emask2 = emask & (eidx != i1)
    v2 = jnp.max(jnp.where(emask2, expert_lg, neg), **red)
    i2 = jnp.min(jnp.where(emask2 & (expert_lg == v2), eidx, N_EXPERTS), **red)
    e2 = jnp.exp(v2 - v1)
    return g_sel, i1, i2, p_sel / (1.0 + e2), p_sel * e2 / (1.0 + e2)


def _route_tile(x, precise_from, ng_ref, wr_cat_ref, wr_hi_ref, br_ref, ha_ref, hbg_ref, idx_ref,
                count_ref, carry_scr):
    rows = x.shape[0]
    hn = _rmsnorm(x, ng_ref[...])
    ha_ref[...], hbg_ref[:, :PACK_W] = _pack_row(hn)
    lgt = _router_logits(hn, wr_cat_ref, wr_hi_ref, br_ref, precise_from).T
    neg = jnp.float32(-jnp.inf)
    gidx = lax.broadcasted_iota(I32, (SUBLANES, rows), 0)
    eidx = lax.broadcasted_iota(I32, (N_EXPERTS, rows), 0)
    group_lg = jnp.where(gidx < N_EXPERT_GROUPS, lgt[0:SUBLANES], neg)
    expert_lg = lgt[EXPERT_LANE0:EXPERT_LANE0 + N_EXPERTS]
    g_sel, i1, i2, w1, w2 = _select(group_lg, expert_lg, gidx, eidx, 0)
    lo = jnp.minimum(i1, i2) - g_sel * EXPERTS_PER_GROUP
    hi = jnp.maximum(i1, i2) - g_sel * EXPERTS_PER_GROUP
    pair = ((lo * (2 * EXPERTS_PER_GROUP - 1 - lo)) >> 1) + (hi - lo - 1)
    cls = g_sel * PAIRS_PER_GROUP + pair
    w_lo = jnp.where(i1 < i2, w1, w2)
    w_hi = jnp.where(i1 < i2, w2, w1)

    crow = lax.broadcasted_iota(I32, (CLASS_ROWS, rows), 0)
    onehot = jnp.where(crow == cls, 1.0, 0.0)
    n_blk = rows // COUNT_BLOCK
    blocks = [onehot[:, j * COUNT_BLOCK:(j + 1) * COUNT_BLOCK] for j in range(n_blk)]
    r = lax.broadcasted_iota(I32, (COUNT_BLOCK, COUNT_BLOCK), 0)
    c = lax.broadcasted_iota(I32, (COUNT_BLOCK, COUNT_BLOCK), 1)
    upper = jnp.where(r < c, 1.0, 0.0).astype(BF16)
    within = _dot(jnp.concatenate(blocks, axis=0).astype(BF16), upper)
    carry = carry_scr[:, 0:1]
    ranks = []
    for j in range(n_blk):
        before = within[j * CLASS_ROWS:(j + 1) * CLASS_ROWS] + carry
        ranks.append(jnp.sum(blocks[j] * before, axis=0, keepdims=True))
        carry = carry + jnp.sum(blocks[j], axis=1, keepdims=True)
    rank = jnp.concatenate(ranks, axis=1)
    carry_scr[...] = jnp.broadcast_to(carry, carry_scr.shape)
    count_ref[...] = carry_scr[...]

    row8 = lax.broadcasted_iota(I32, (SUBLANES, rows), 0)
    idx_ref[...] = jnp.where(row8 == 0, cls, jnp.where(row8 == 1, rank.astype(I32), 0))
    rowl = lax.broadcasted_iota(I32, (LANES, rows), 0)
    gate_rec = jnp.where(rowl == GATE_LO, w_lo, jnp.where(rowl == GATE_HI, w_hi, 0.0)).T
    hbg_ref[:, PACK_W:] = pltpu.bitcast(gate_rec, I32)


def _moe_pair_kernel(ea_ref, eb_ref, used_ref, ha_ref, hbg_ref, wgu_a_ref, wd_a_ref,
                     wgu_b_ref, wd_b_ref, after_ref, ya_ref, yb_ref):
    del ea_ref, eb_ref, after_ref

    @pl.when(pl.program_id(0) < used_ref[0])
    def _():
        h = _unpack_row(ha_ref[...], hbg_ref[:, :PACK_W]).astype(BF16)
        gates = pltpu.bitcast(hbg_ref[:, PACK_W:], F32)
        y = None
        for wgu_ref, wd_ref, lane in ((wgu_a_ref, wd_a_ref, GATE_LO), (wgu_b_ref, wd_b_ref, GATE_HI)):
            gu = _dot(h, wgu_ref[0])
            act = _silu(gu[:, :D_EXPERT]) * gu[:, D_EXPERT:]
            part = _dot((act * gates[:, lane:lane + 1]).astype(BF16), wd_ref[0])
            y = part if y is None else y + part
        ya_ref[...], yb_ref[...] = _pack_row(y)


def _final_norm_kernel(x_ref, ya_ref, yb_ref, g_ref, o_ref):
    o_ref[...] = _rmsnorm(x_ref[...] + _unpack_row(ya_ref[...], yb_ref[...]), g_ref[...])


def _moe_dense_kernel(x_ref, ng_ref, wr_cat_ref, wr_hi_ref, br_ref, wg_ref, wu_ref, wdn_ref, nf_ref,
                      after_ref, y_ref, wgu_ref, wd_ref, h_scr, gate_scr, acc_scr, *, final_norm):
    del after_ref
    e = pl.program_id(1)

    @pl.when(e == 0)
    def _():
        hn = _rmsnorm(x_ref[...], ng_ref[...])
        h_scr[...] = hn.astype(BF16)
        lg = _router_logits(hn, wr_cat_ref, wr_hi_ref, br_ref)
        lane = lax.broadcasted_iota(I32, lg.shape, 1)
        neg = jnp.float32(-jnp.inf)
        group_lg = jnp.where(lane < N_EXPERT_GROUPS, lg, neg)
        is_expert = (lane >= EXPERT_LANE0) & (lane < EXPERT_LANE0 + N_EXPERTS)
        _, i1, i2, w1, w2 = _select(group_lg, jnp.where(is_expert, lg, neg), lane,
                                    jnp.where(is_expert, lane - EXPERT_LANE0, N_EXPERTS), 1)
        gate_scr[...] = (jnp.where(lane == i1 + EXPERT_LANE0, w1, 0.0)
                         + jnp.where(lane == i2 + EXPERT_LANE0, w2, 0.0))
        acc_scr[...] = jnp.zeros_like(acc_scr)

    lane = lax.broadcasted_iota(I32, gate_scr.shape, 1)
    gate = jnp.sum(jnp.where(lane == EXPERT_LANE0 + e, gate_scr[...], 0.0), axis=-1, keepdims=True)
    wgu_ref[0, :, :D_EXPERT] = wg_ref[0].astype(BF16)
    wgu_ref[0, :, D_EXPERT:] = wu_ref[0].astype(BF16)
    wd_ref[0] = wdn_ref[0].astype(BF16)
    gu = _dot(h_scr[...], wgu_ref[0])
    act = _silu(gu[:, :D_EXPERT]) * gu[:, D_EXPERT:]
    acc_scr[...] += _dot((act * gate).astype(BF16), wd_ref[0])

    @pl.when(e == N_EXPERTS - 1)
    def _():
        y = x_ref[...] + acc_scr[...]
        if final_norm:
            y = _rmsnorm(y, nf_ref[...])
        y_ref[...] = y


def _const_spec(shape):
    return pl.BlockSpec(shape, lambda *_: (0,) * len(shape))


_ANY_SPEC = pl.BlockSpec(memory_space=pl.ANY)


def _rope_tables(pos):
    half = RET_HEAD_DIM // 2
    inv = np.float32(ROPE_BASE) ** (-np.arange(half, dtype=np.float32) / np.float32(half))
    ang = pos.astype(np.float32)[:, None] * inv[None, :]
    cos, sin = np.cos(ang), np.sin(ang)
    return np.concatenate([cos, cos], -1), np.concatenate([-sin, sin], -1)


def _decay_consts(c, reps):
    f = np.float32
    log_g = np.log(f(1.0) - f(2.0) ** (f(-5.0) - np.arange(RET_HEADS, dtype=f)))
    i = np.arange(c, dtype=f)
    diff = i[:, None] - i[None, :]
    dmat = np.where(diff[None] >= 0, np.exp(np.maximum(diff, f(0))[None] * log_g[:, None, None]), f(0))
    qdec = np.exp((i + f(1))[None, :] * log_g[:, None])
    kdec = np.exp((f(c) - f(1) - i)[None, :] * log_g[:, None])
    cdec = np.exp(f(c) * log_g)
    dmat = np.einsum('ab,hij->haibj', np.eye(reps, dtype=f), dmat).reshape(RET_HEADS, reps * c, reps * c)
    lanes = lambda t: np.repeat(np.tile(t, (1, reps)).T, RET_HEAD_DIM, axis=1)
    cdec = np.broadcast_to(cdec[:, None, None], (RET_HEADS, 1, RET_HEAD_DIM))
    return tuple(np.ascontiguousarray(a, dtype=f) for a in (dmat, lanes(qdec), lanes(kdec), cdec))


def _split_weight(w):
    hi = lax.bitcast_convert_type(_bf16_round_bits(lax.bitcast_convert_type(w, jnp.uint32)), F32)
    return hi.astype(BF16), (w - hi).astype(BF16)


_W_IN_BLOCK = (None, D_MODEL, IN_WIDTH)
_W_POOL_BLOCK = (None, len(POOL_WINDOWS), POOL_GROUP_DIM, POOL_GROUP_DIM)
_W_OUT_BLOCK = (None, D_MODEL, D_MODEL)
_DECAY_SPECS = [
    _const_spec((RET_HEADS, CHUNK, CHUNK)), _const_spec((CHUNK, RET_WIDTH)),
    _const_spec((CHUNK, RET_WIDTH)), _const_spec((RET_HEADS, 1, RET_HEAD_DIM)),
]


def _layer_spec(block, layer, resident=False):
    index_map = lambda *_: (layer,) + (0,) * (len(block) - 1)
    if resident:
        return pl.BlockSpec(block, index_map, pipeline_mode=pl.Buffered(1))
    return pl.BlockSpec(block, index_map)


def _mixer_weight_specs(layer, split):
    n = 2 if split else 1
    row = lambda width: _layer_spec((None, 1, width), layer)
    return [row(D_MODEL), *[_layer_spec(_W_IN_BLOCK, layer, True)] * n,
            *[_layer_spec(_W_POOL_BLOCK, layer)] * n, row(POOL_WIDTH), row(RET_WIDTH),
            *[_layer_spec(_W_OUT_BLOCK, layer, True)] * n]


def _chain_out_spec(block, depth, layer, has_prev):
    zeros = (0,) * (len(block) - 1)
    if has_prev:
        return pl.BlockSpec((None,) + block, lambda i, *_: (layer, i) + zeros)
    return pl.BlockSpec((depth,) + block, lambda i, *_: (0, i) + zeros)


def _mixer_prompt(x, moe_y, weights, route_w, prev_out, layer, rows, precise_tail, after):
    b, l, _ = x.shape
    t = b * l
    depth = weights[0].shape[0]
    assert l % rows == 0 and rows % CHUNK == 0
    cos2, sin2 = _rope_tables(np.arange(l))
    decay = _decay_consts(RET_CHUNK, 1)
    steps = l // rows
    tok = lambda i, c: (i, c, 0)
    if moe_y is None:
        ya = yb = jnp.zeros((1, rows, PACK_W), I32)
        y_spec = _const_spec((1, rows, PACK_W))
    else:
        ya, yb = (t.reshape(b, l, PACK_W) for t in moe_y)
        y_spec = pl.BlockSpec((1, rows, PACK_W), tok)
    has_prev = prev_out is not None
    n_in = 23
    flat = lambda i, c: (i * steps + c, 0)
    y, tails, states, *routed = pl.pallas_call(
        functools.partial(_mixer_prompt_kernel, rows=rows, steps=steps, moe_in=moe_y is not None,
                          precise_tail=precise_tail, layer=layer, has_prev=has_prev),
        grid=(b, steps),
        in_specs=[pl.BlockSpec((1, rows, D_MODEL), tok), y_spec, y_spec,
                  pl.BlockSpec((rows, RET_HEAD_DIM), lambda i, c: (c, 0)),
                  pl.BlockSpec((rows, RET_HEAD_DIM), lambda i, c: (c, 0)),
                  *_mixer_weight_specs(layer, True), *_DECAY_SPECS,
                  _layer_spec((None, 1, D_MODEL), layer), *_router_specs(layer), _ANY_SPEC,
                  *([_ANY_SPEC, _ANY_SPEC] if has_prev else [])],
        out_specs=[pl.BlockSpec((1, rows, D_MODEL), tok),
                   _chain_out_spec((1, POOL_BUF, POOL_WIDTH), depth, layer, has_prev),
                   _chain_out_spec((1, RET_HEADS, RET_HEAD_DIM, RET_HEAD_DIM), depth, layer, has_prev),
                   pl.BlockSpec((rows, PACK_W), flat), pl.BlockSpec((rows, PACK_W + LANES), flat),
                   pl.BlockSpec((SUBLANES, rows), lambda i, c: (0, i * steps + c)),
                   _const_spec((CLASS_ROWS, LANES))],
        out_shape=[jax.ShapeDtypeStruct(x.shape, F32),
                   jax.ShapeDtypeStruct((depth, b, POOL_BUF, POOL_WIDTH), F32),
                   jax.ShapeDtypeStruct((depth, b, RET_HEADS, RET_HEAD_DIM, RET_HEAD_DIM), F32),
                   jax.ShapeDtypeStruct((t, PACK_W), I32),
                   jax.ShapeDtypeStruct((t, PACK_W + LANES), I32), jax.ShapeDtypeStruct((SUBLANES, t), I32),
                   jax.ShapeDtypeStruct((CLASS_ROWS, LANES), F32)],
        input_output_aliases={n_in: 1, n_in + 1: 2} if has_prev else {},
        scratch_shapes=[pltpu.VMEM((1, POOL_HIST + rows, POOL_WIDTH), F32),
                        pltpu.VMEM((RET_HEADS, RET_HEAD_DIM, RET_HEAD_DIM), F32),
                        pltpu.VMEM((rows, RET_WIDTH), F32),
                        pltpu.VMEM((rows, IN_WIDTH), F32),
                        pltpu.VMEM((CLASS_ROWS, LANES), F32)],
        compiler_params=pltpu.CompilerParams(
            dimension_semantics=("arbitrary", "arbitrary"), vmem_limit_bytes=VMEM_LIMIT),
        name="mixer_prompt",
    )(x, ya, yb, cos2, sin2, *weights, *decay, *route_w, after, *(prev_out if has_prev else ()))
    return y, (tails, states), tuple(routed)


def _mixer_sample(x, pool_prev, s0, weights, prev_out, layer, after):
    b, l, _ = x.shape
    depth = s0.shape[0]
    ns = SAMPLE_SEQS
    assert ns * l == CHUNK and b % ns == 0
    cos2, sin2 = _rope_tables(PAST_LEN + np.arange(l))
    cos2, sin2 = np.tile(cos2, (ns, 1)), np.tile(sin2, (ns, 1))
    decay = _decay_consts(l, ns)
    seq3 = lambda i: (i, 0, 0)
    state_block = (ns, RET_HEADS, RET_HEAD_DIM, RET_HEAD_DIM)
    tail_block = (POOL_BUF, ns, POOL_WIDTH)
    if prev_out is None:
        tail_out = pl.BlockSpec((depth,) + tail_block, lambda i: (0, 0, i, 0))
    else:
        tail_out = pl.BlockSpec((None,) + tail_block, lambda i: (layer, 0, i, 0))
    has_prev = prev_out is not None
    n_in = 16
    y, tails, states = pl.pallas_call(
        functools.partial(_mixer_sample_kernel, seq_len=l, layer=layer, has_prev=has_prev),
        grid=(b // ns,),
        in_specs=[pl.BlockSpec((ns, l, D_MODEL), seq3),
                  pl.BlockSpec((None,) + tail_block, lambda i: (layer, 0, i, 0)),
                  pl.BlockSpec((None,) + state_block, lambda i: (layer, i, 0, 0, 0)),
                  _const_spec((CHUNK, RET_HEAD_DIM)), _const_spec((CHUNK, RET_HEAD_DIM)),
                  *_mixer_weight_specs(layer, False), *_DECAY_SPECS, _ANY_SPEC,
                  *([_ANY_SPEC, _ANY_SPEC] if has_prev else [])],
        out_specs=[pl.BlockSpec((ns, l, D_MODEL), seq3), tail_out,
                   _chain_out_spec(state_block, depth, layer, has_prev)],
        out_shape=[jax.ShapeDtypeStruct(x.shape, F32),
                   jax.ShapeDtypeStruct((depth, POOL_BUF, b, POOL_WIDTH), F32),
                   jax.ShapeDtypeStruct(s0.shape, F32)],
        input_output_aliases={n_in: 1, n_in + 1: 2} if has_prev else {},
        scratch_shapes=[pltpu.VMEM((ns, POOL_HIST + l, POOL_WIDTH), F32),
                        pltpu.VMEM((CHUNK, RET_WIDTH), F32)],
        compiler_params=pltpu.CompilerParams(
            dimension_semantics=("arbitrary",), vmem_limit_bytes=VMEM_LIMIT),
        name="mixer_sample",
    )(x, pool_prev, s0, cos2, sin2, *weights, *decay, after, *(prev_out if has_prev else ()))
    return y, (tails, states)


def _router_weights(w_rg, b_rg, w_re, b_re):
    depth = w_rg.shape[0]
    gap = EXPERT_LANE0 - N_EXPERT_GROUPS
    rest = LANES - EXPERT_LANE0 - N_EXPERTS
    wr = jnp.concatenate([w_rg, jnp.zeros((depth, D_MODEL, gap), F32), w_re,
                          jnp.zeros((depth, D_MODEL, rest), F32)], axis=-1)
    br = jnp.concatenate([b_rg, jnp.zeros((depth, gap), F32), b_re, jnp.zeros((depth, rest), F32)],
                         axis=-1).reshape(depth, 1, LANES)
    wr_hi, wr_lo = _split_weight(wr)
    return jnp.concatenate([wr_hi, wr_lo], axis=-1), wr_hi, br


def _router_specs(layer):
    return [_layer_spec((None, D_MODEL, 2 * LANES), layer), _layer_spec((None, D_MODEL, LANES), layer),
            _layer_spec((None, 1, LANES), layer)]


def _sc_mesh():
    return plsc.VectorSubcoreMesh(core_axis_name="core", subcore_axis_name="subcore",
                                  num_cores=SC_CORES, num_subcores=SC_SUBCORES)


def _sc_params():
    params = pltpu.CompilerParams()
    if "needs_layout_passes" in pltpu.CompilerParams.__dataclass_fields__:
        params = dataclasses.replace(params, needs_layout_passes=False)
    return params


def _sc_gather(tables, idx, after):
    n = idx.shape[0]
    assert n % SC_WINDOW == 0
    nt = len(tables)

    def body(*refs):
        i_hbm = refs[nt]
        for t_hbm, o_hbm in zip(refs[:nt], refs[nt + 2:]):
            def gather_window(i_vmem, o_vmem, t_hbm=t_hbm):
                pltpu.sync_copy(t_hbm.at[i_vmem.at[0]], o_vmem)

            pltpu.emit_pipeline(
                gather_window, grid=(n // SC_WINDOW,),
                in_specs=[pl.BlockSpec((1, SC_WINDOW), lambda i: (0, i))],
                out_specs=[pl.BlockSpec((SC_WINDOW, t_hbm.shape[1]), lambda i: (i, 0))],
                core_axis_name=("core", "subcore"),
                dimension_semantics=(pltpu.PARALLEL,),
            )(i_hbm, o_hbm)

    out_type = tuple(jax.ShapeDtypeStruct((n, t.shape[1]), t.dtype) for t in tables)
    return pl.kernel(body, out_type=out_type, mesh=_sc_mesh(), name="sc_gather")(
        *tables, idx.reshape(1, n), after)


def _sc_slots(cls, rank, starts, n_slots):
    t = cls.shape[0]
    workers = SC_CORES * SC_SUBCORES
    slot_per, tok_per = n_slots // workers, t // workers
    assert n_slots % (workers * SC_LANES) == 0 and t % (workers * SC_LANES) == 0 and t & (t - 1) == 0
    assert n_slots % (SC_LANES * SC_UNROLL) == 0 and t % (SC_LANES * SC_UNROLL) == 0

    def body(cls_hbm, rank_hbm, starts_hbm, pos_hbm, slot_hbm, cls_v, rank_v, starts_v, pos_v, slot_v):
        wid = lax.axis_index("subcore") * SC_CORES + lax.axis_index("core")
        pltpu.sync_copy(cls_hbm, cls_v)
        pltpu.sync_copy(rank_hbm, rank_v)
        pltpu.sync_copy(starts_hbm, starts_v)

        lane = lax.iota(I32, SC_LANES)
        span = SC_LANES * SC_UNROLL

        @pl.loop(0, n_slots, step=span)
        def _(i):
            for u in range(SC_UNROLL):
                j = i + u * SC_LANES
                slot_v[pl.ds(j, SC_LANES)] = (lane + j) & (t - 1)

        @pl.loop(0, t, step=span)
        def _(i):
            for u in range(SC_UNROLL):
                j = i + u * SC_LANES
                at = pl.ds(j, SC_LANES)
                pos = plsc.load_gather(starts_v, [cls_v[at]]) + rank_v[at]
                pos_v[at] = pos
                plsc.store_scatter(slot_v, [pos], lane + j)

        tok_off = pl.multiple_of(wid * tok_per, SC_LANES)
        pltpu.sync_copy(pos_v.at[pl.ds(tok_off, tok_per)], pos_hbm.at[pl.ds(tok_off, tok_per)])
        slot_off = pl.multiple_of(wid * slot_per, SC_LANES)
        pltpu.sync_copy(slot_v.at[pl.ds(slot_off, slot_per)], slot_hbm.at[pl.ds(slot_off, slot_per)])

    return pl.kernel(
        body, mesh=_sc_mesh(), compiler_params=_sc_params(), name="sc_slots",
        out_type=(jax.ShapeDtypeStruct((t,), I32), jax.ShapeDtypeStruct((n_slots,), I32)),
        scratch_types=[pltpu.VMEM((t,), I32), pltpu.VMEM((t,), I32), pltpu.VMEM((CLASS_ROWS,), I32),
                       pltpu.VMEM((t,), I32), pltpu.VMEM((n_slots,), I32)],
    )(cls, rank, starts)


def _moe_dispatch(routed, after):
    ha, hbg, idx, counts = routed
    t = ha.shape[0]
    n_slots = t + N_CLASSES * PAIR_TILE
    n_tiles = n_slots // PAIR_TILE

    cnt = counts[:, 0].astype(I32)
    padded = (cnt + PAIR_TILE - 1) // PAIR_TILE * PAIR_TILE
    ends = jnp.cumsum(padded)
    starts = ends - padded
    tile_cls = jnp.minimum(
        jnp.sum(ends[None, :N_CLASSES] <= (jnp.arange(n_tiles, dtype=I32) * PAIR_TILE)[:, None], axis=1),
        N_CLASSES - 1).astype(I32)
    first = (tile_cls // PAIRS_PER_GROUP) * EXPERTS_PER_GROUP
    tile_ea = first + jnp.asarray(PAIR_LO, I32)[tile_cls % PAIRS_PER_GROUP]
    tile_eb = first + jnp.asarray(PAIR_HI, I32)[tile_cls % PAIRS_PER_GROUP]
    used = (ends[N_CLASSES - 1:N_CLASSES] // PAIR_TILE).astype(I32)

    pos, slot_tok = _sc_slots(idx[0], idx[1], starts, n_slots)
    hsa, hsbg = _sc_gather((ha, hbg), slot_tok, after=after)
    return (tile_ea, tile_eb, used, hsa, hsbg), pos, slot_tok


def _moe_pair(dispatched, wgu, wd, after):
    tile_ea, tile_eb, used, hsa, hsbg = dispatched
    n_slots = hsa.shape[0]
    n_tiles = n_slots // PAIR_TILE
    row = lambda i, ea, eb, nu: (jnp.minimum(i, nu[0] - 1), 0)
    w_spec = lambda shape, which: pl.BlockSpec(
        (1,) + shape, lambda i, ea, eb, nu: ((ea, eb)[which][i], 0, 0))
    gu_shape, d_shape = (D_MODEL, 2 * D_EXPERT), (D_EXPERT, D_MODEL)
    ysa, ysb = pl.pallas_call(
        _moe_pair_kernel,
        grid_spec=pltpu.PrefetchScalarGridSpec(
            num_scalar_prefetch=3, grid=(n_tiles,),
            in_specs=[pl.BlockSpec((PAIR_TILE, PACK_W), row),
                      pl.BlockSpec((PAIR_TILE, PACK_W + LANES), row),
                      w_spec(gu_shape, 0), w_spec(d_shape, 0), w_spec(gu_shape, 1), w_spec(d_shape, 1),
                      _ANY_SPEC],
            out_specs=[pl.BlockSpec((PAIR_TILE, PACK_W), row), pl.BlockSpec((PAIR_TILE, PACK_W), row)]),
        out_shape=[jax.ShapeDtypeStruct((n_slots, PACK_W), I32),
                   jax.ShapeDtypeStruct((n_slots, PACK_W), I32)],
        compiler_params=pltpu.CompilerParams(
            dimension_semantics=("arbitrary",), vmem_limit_bytes=VMEM_LIMIT),
        name="moe_pair",
    )(tile_ea, tile_eb, used, hsa, hsbg, wgu, wd, wgu, wd, after)
    return ysa, ysb


def _final_norm(x, moe_y, g, rows):
    t = x.shape[0]
    rows = min(rows, t)
    assert t % rows == 0
    tok = lambda i: (i, 0)
    return pl.pallas_call(
        _final_norm_kernel,
        grid=(t // rows,),
        in_specs=[pl.BlockSpec((rows, D_MODEL), tok), pl.BlockSpec((rows, PACK_W), tok),
                  pl.BlockSpec((rows, PACK_W), tok), _const_spec((1, D_MODEL))],
        out_specs=pl.BlockSpec((rows, D_MODEL), tok),
        out_shape=jax.ShapeDtypeStruct(x.shape, F32),
        compiler_params=pltpu.CompilerParams(
            dimension_semantics=("arbitrary",), vmem_limit_bytes=VMEM_LIMIT),
        name="final_norm",
    )(x, *moe_y, g.reshape(1, D_MODEL))


def _moe_dense(x, norm_g, router, w_gate, w_up, w_down, norm_final, layer, final_norm, rows, after):
    t = x.shape[0]
    assert t == rows
    tok = lambda i, e: (i, 0)
    w_spec = lambda shape: pl.BlockSpec((None, 1) + shape, lambda i, e: (layer, e, 0, 0))
    w_out = lambda shape: pl.BlockSpec((1,) + shape, lambda i, e: (e, 0, 0))
    return pl.pallas_call(
        functools.partial(_moe_dense_kernel, final_norm=final_norm),
        grid=(t // rows, N_EXPERTS),
        in_specs=[pl.BlockSpec((rows, D_MODEL), tok), _layer_spec((None, 1, D_MODEL), layer),
                  *_router_specs(layer),
                  w_spec((D_MODEL, D_EXPERT)), w_spec((D_MODEL, D_EXPERT)), w_spec((D_EXPERT, D_MODEL)),
                  _const_spec((1, D_MODEL)), _ANY_SPEC],
        out_specs=[pl.BlockSpec((rows, D_MODEL), tok), w_out((D_MODEL, 2 * D_EXPERT)),
                   w_out((D_EXPERT, D_MODEL))],
        out_shape=[jax.ShapeDtypeStruct(x.shape, F32),
                   jax.ShapeDtypeStruct((N_EXPERTS, D_MODEL, 2 * D_EXPERT), BF16),
                   jax.ShapeDtypeStruct((N_EXPERTS, D_EXPERT, D_MODEL), BF16)],
        scratch_shapes=[pltpu.VMEM((rows, D_MODEL), BF16),
                        pltpu.VMEM((rows, LANES), F32),
                        pltpu.VMEM((rows, D_MODEL), F32)],
        compiler_params=pltpu.CompilerParams(
            dimension_semantics=("arbitrary", "arbitrary"), vmem_limit_bytes=VMEM_LIMIT),
        name="moe_dense",
    )(x, norm_g, *router, w_gate, w_up, w_down, norm_final.reshape(1, D_MODEL), after)


def kernel(x_prompt, x_sample, cache_pool, state_ret, norm_mix, w_in, w_pool, pool_scale, ret_gn, w_out, norm_ffn, w_router_group, b_router_group, w_router_expert, b_router_expert, w_gate, w_up, w_down, norm_final):
    depth = norm_mix.shape[0]
    row = lambda a: a.reshape(depth, 1, a.shape[-1])
    mix_split = (row(norm_mix), *_split_weight(w_in), *_split_weight(w_pool), row(pool_scale),
                 row(ret_gn), *_split_weight(w_out))
    mix_hi = tuple(mix_split[i] for i in (0, 1, 3, 5, 6, 7))
    router = _router_weights(w_router_group, b_router_group, w_router_expert, b_router_expert)
    norm_ffn = row(norm_ffn)
    pool_prev = jnp.swapaxes(cache_pool, 1, 2)

    yp, ys = x_prompt, x_sample
    moe_p = None
    out_p = out_s = None
    ys_mix = None
    for l in range(depth):
        yp, out_p, routed = _mixer_prompt(
            yp, moe_p, mix_split, (norm_ffn, *router), out_p, l, rows=512,
            precise_tail=PRECISE_TAIL_STEPS if l < depth - 1 else 0,
            after=ys if ys_mix is None else ys_mix)
        if ys_mix is None:
            ys_mix, out_s = _mixer_sample(ys, pool_prev, state_ret, mix_hi, out_s, l, after=routed[-1])
        dispatched, pos, slot_tok = _moe_dispatch(routed, after=ys_mix if l == 0 else routed[-1])
        ys, wgu, wd = _moe_dense(ys_mix.reshape(-1, D_MODEL), norm_ffn, router, w_gate, w_up, w_down,
                                 norm_final, l, l == depth - 1, rows=ys_mix.shape[0] * ys_mix.shape[1],
                                 after=slot_tok)
        ys = ys.reshape(x_sample.shape)
        sorted_y = _moe_pair(dispatched, wgu, wd, after=ys)
        moe_p = _sc_gather(sorted_y, pos, after=routed[-1])
        if l + 1 < depth:
            ys_mix, out_s = _mixer_sample(ys, pool_prev, state_ret, mix_hi, out_s, l + 1,
                                          after=sorted_y[0])
    yp = _final_norm(yp.reshape(-1, D_MODEL), moe_p, norm_final, rows=2048).reshape(yp.shape)
    return (yp, ys, *out_p, jnp.swapaxes(out_s[0], 1, 2), out_s[1])
```

```python
import dataclasses
import functools
import itertools

import jax
import jax.numpy as jnp
import numpy as np
from jax import lax
from jax.experimental import pallas as pl
from jax.experimental.pallas import tpu as pltpu
from jax.experimental.pallas import tpu_sc as plsc

F32 = jnp.float32
BF16 = jnp.bfloat16
I32 = jnp.int32

D_MODEL = 1024
POOL_WIDTH = 512
POOL_WINDOWS = (2, 4, 8, 16)
POOL_GROUP_DIM = 128
POOL_BUF = 15
POOL_HIST = 16
RET_WIDTH = 512
RET_HEADS = 4
RET_HEAD_DIM = 128
RET_CHUNK = 128
ROPE_BASE = 10000.0
IN_WIDTH = POOL_WIDTH + 4 * RET_WIDTH
N_EXPERT_GROUPS = 4
EXPERTS_PER_GROUP = 4
N_EXPERTS = N_EXPERT_GROUPS * EXPERTS_PER_GROUP
D_EXPERT = 256
RMS_EPS = 1e-6
GN_EPS = 1e-5
PAST_LEN = 16384

LANES = 128
SUBLANES = 8
EXPERT_LANE0 = 8
GROUP_SHIFT = EXPERTS_PER_GROUP.bit_length() - 1
PAIRS = tuple(itertools.combinations(range(EXPERTS_PER_GROUP), 2))
PAIRS_PER_GROUP = len(PAIRS)
PAIR_LO, PAIR_HI = zip(*PAIRS)
N_CLASSES = N_EXPERT_GROUPS * PAIRS_PER_GROUP
CLASS_ROWS = 32
GATE_LO, GATE_HI = 0, 1
COUNT_BLOCK = 256
PAIR_TILE = 256
PACK_W = D_MODEL // 4
SC_CORES, SC_SUBCORES, SC_LANES = 2, 16, 16
SC_WINDOW = 128
SC_UNROLL = 8
PRECISE_TAIL_STEPS = 1
CHUNK = 128
SAMPLE_SEQS = 16
STATE_BUFFERS = 3
VMEM_LIMIT = 56 * 1024 * 1024


def _dot(a, b):
    return jnp.dot(a, b, preferred_element_type=F32)


def _dot_nt(a, b):
    return lax.dot_general(a, b, (((1,), (1,)), ((), ())), preferred_element_type=F32)


def _bf16_round_bits(u):
    return (u + jnp.uint32(0x7FFF) + ((u >> 16) & jnp.uint32(1))) & jnp.uint32(0xFFFF0000)


def _split(a):
    hi = pltpu.bitcast(_bf16_round_bits(pltpu.bitcast(a, jnp.uint32)), F32)
    return hi.astype(BF16), (a - hi).astype(BF16)


def _mm(a, b, precise, nt=False):
    dot = _dot_nt if nt else _dot
    if precise:
        b_hi, b_lo = b if isinstance(b, tuple) else _split(b)
        a_hi, a_lo = _split(a)
        return dot(a_hi, b_hi) + dot(a_lo, b_hi) + dot(a_hi, b_lo)
    return dot(a.astype(BF16), b[0] if isinstance(b, tuple) else b.astype(BF16))


def _rmsnorm(x, g):
    ms = jnp.mean(x * x, axis=-1, keepdims=True)
    return x * lax.rsqrt(ms + RMS_EPS) * g


def _pool_mix(ubuf, rows, t_first, n_prev, wpool_refs, pscale, precise=False, row0=0):
    ns = ubuf.shape[0]
    t = t_first + lax.broadcasted_iota(I32, (1, rows, POOL_GROUP_DIM), 1)
    base = POOL_HIST + row0
    outs = []
    for j, w in enumerate(POOL_WINDOWS):
        lanes = slice(j * POOL_GROUP_DIM, (j + 1) * POOL_GROUP_DIM)
        uj = ubuf[:, base:base + rows, lanes]
        acc = uj
        for i in range(1, w):
            acc = acc + ubuf[:, base - i:base - i + rows, lanes]
        cnt = jnp.minimum(w, n_prev + t + 1).astype(F32)
        d = (acc / cnt - uj).reshape(ns * rows, POOL_GROUP_DIM)
        outs.append(_mm(d, tuple(w[j] for w in wpool_refs), precise))
    return jnp.concatenate(outs, axis=-1) * pscale


def _rope(xh, cos2, sin2):
    return xh * cos2 + pltpu.roll(xh, RET_HEAD_DIM // 2, 1) * sin2


def _group_norm(o):
    mu = jnp.mean(o, axis=-1, keepdims=True)
    c = o - mu
    var = jnp.mean(c * c, axis=-1, keepdims=True)
    return c * lax.rsqrt(var + GN_EPS)


def _silu(x):
    return x * (1.0 / (1.0 + jnp.exp(-x)))


def _head(a, h):
    return a[:, h * RET_HEAD_DIM:(h + 1) * RET_HEAD_DIM]


def _qkvg(z):
    p, r = POOL_WIDTH, RET_WIDTH
    return z[:, p:p + r], z[:, p + r:p + 2 * r], z[:, p + 2 * r:p + 3 * r], z[:, p + 3 * r:p + 4 * r]


def _pack_bf16_pair(a, b):
    ua = pltpu.bitcast(a.astype(BF16).astype(F32), jnp.uint32)
    ub = pltpu.bitcast(b.astype(BF16).astype(F32), jnp.uint32)
    return pltpu.bitcast((ua >> 16) | (ub & jnp.uint32(0xFFFF0000)), I32)


def _unpack_bf16_pair(w):
    u = pltpu.bitcast(w, jnp.uint32)
    return pltpu.bitcast(u << 16, F32), pltpu.bitcast(u & jnp.uint32(0xFFFF0000), F32)


def _pack_row(y):
    q = PACK_W
    return _pack_bf16_pair(y[:, 0:q], y[:, q:2 * q]), _pack_bf16_pair(y[:, 2 * q:3 * q], y[:, 3 * q:])


def _unpack_row(wa, wb):
    return jnp.concatenate([*_unpack_bf16_pair(wa), *_unpack_bf16_pair(wb)], axis=-1)


def _zero_other_layers(ref, layer):
    for j in range(ref.shape[0]):
        if j != layer:
            ref[j] = jnp.zeros(ref.shape[1:], ref.dtype)


def _mixer_prompt_kernel(*refs, rows, steps, moe_in, precise_tail, layer, has_prev):
    (x_ref, ya_ref, yb_ref, cos_ref, sin_ref, ng_ref, win_hi_ref, win_lo_ref, wpool_hi_ref,
     wpool_lo_ref, pscale_ref, gn_ref, wout_hi_ref, wout_lo_ref, dmat_ref, qdec_ref, kdec_ref,
     cdec_ref) = refs[:18]
    route_in = refs[18:22]
    y_ref, tail_ref, sfin_ref, *route_out, ubuf, s_scr, ret_scr, z_scr, count_scr = (
        refs[23 + 2 * has_prev:])
    c = pl.program_id(1)

    @pl.when((pl.program_id(0) == 0) & (c == 0))
    def _():
        count_scr[...] = jnp.zeros_like(count_scr)
    if not has_prev:
        _zero_other_layers(tail_ref, layer)
        _zero_other_layers(sfin_ref, layer)
        tail_ref, sfin_ref = tail_ref.at[layer], sfin_ref.at[layer]

    @pl.when(c == 0)
    def _():
        ubuf[:, 0:POOL_HIST, :] = jnp.zeros((1, POOL_HIST, POOL_WIDTH), F32)
        s_scr[...] = jnp.zeros_like(s_scr)

    kv_cols = slice(POOL_WIDTH + RET_WIDTH, POOL_WIDTH + 3 * RET_WIDTH)

    def step(kv_precise, full_from):
        x = x_ref[0]
        if moe_in:
            x = x + _unpack_row(ya_ref[0], yb_ref[0])
        hn = _rmsnorm(x, ng_ref[...])
        hi, lo = _split(hn) if kv_precise else (hn.astype(BF16), None)
        z_scr[...] = _dot(hi, win_hi_ref[...])
        if kv_precise and full_from:
            z_scr[:full_from, kv_cols] += (_dot(lo[:full_from], win_hi_ref[:, kv_cols])
                                           + _dot(hi[:full_from], win_lo_ref[:, kv_cols]))
        if full_from < rows:
            z_scr[full_from:, :] += (_dot(lo[full_from:], win_hi_ref[...])
                                     + _dot(hi[full_from:], win_lo_ref[...]))

        ubuf[0, POOL_HIST:POOL_HIST + rows, :] = z_scr[:, :POOL_WIDTH]
        pool_w = (wpool_hi_ref, wpool_lo_ref)
        pool_parts = []
        if full_from:
            pool_parts.append(_pool_mix(ubuf, full_from, c * rows, 0, pool_w, pscale_ref[...]))
        if full_from < rows:
            pool_parts.append(_pool_mix(ubuf, rows - full_from, c * rows + full_from, 0, pool_w,
                                        pscale_ref[...], precise=True, row0=full_from))
        pool_out = jnp.concatenate(pool_parts, axis=0)
        tail_ref[...] = ubuf[:, rows + POOL_HIST - POOL_BUF:rows + POOL_HIST, :]
        ubuf[:, 0:POOL_HIST, :] = ubuf[:, rows:rows + POOL_HIST, :]

        scale = RET_HEAD_DIM ** -0.5
        for ci in range(rows // CHUNK):
            rs = slice(ci * CHUNK, (ci + 1) * CHUNK)
            full = ci * CHUNK >= full_from
            cos2 = cos_ref[rs, :]
            sin2 = sin_ref[rs, :]
            for h in range(RET_HEADS):
                col = lambda part: slice(POOL_WIDTH + part * RET_WIDTH + h * RET_HEAD_DIM,
                                         POOL_WIDTH + part * RET_WIDTH + (h + 1) * RET_HEAD_DIM)
                q = _rope(z_scr[rs, col(0)], cos2, sin2)
                k = _rope(z_scr[rs, col(1)], cos2, sin2) * scale
                v = z_scr[rs, col(2)]
                s_old = s_scr[h]
                scores = _mm(q, k, full, nt=True) * dmat_ref[h]
                qd = q * _head(qdec_ref[...], h)
                o = _mm(scores, v, full) + _mm(qd, s_old, full)
                kd = k * _head(kdec_ref[...], h)
                s_scr[h] = s_old * cdec_ref[h] + _mm(kd.T, v, kv_precise)
                on = _group_norm(o) * _head(gn_ref[...], h)
                ret_scr[rs, h * RET_HEAD_DIM:(h + 1) * RET_HEAD_DIM] = _silu(z_scr[rs, col(3)]) * on

        mix_in = jnp.concatenate([pool_out, ret_scr[...]], axis=-1)
        y_ref[0] = x + _dot(mix_in.astype(BF16), wout_hi_ref[...])
        if full_from < rows:
            m_hi, m_lo = _split(mix_in[full_from:])
            y_ref[0, full_from:, :] += _dot(m_lo, wout_hi_ref[...]) + _dot(m_hi, wout_lo_ref[...])
        sfin_ref[0] = s_scr[...]
        _route_tile(y_ref[0], full_from, *route_in, *route_out, count_scr)

    if precise_tail:
        pl.when(c < steps - precise_tail)(lambda: step(False, rows))
        if precise_tail > 1:
            pl.when((c >= steps - precise_tail) & (c < steps - 1))(lambda: step(True, rows))
        pl.when(c == steps - 1)(lambda: step(True, rows - CHUNK))
    else:
        step(False, rows)


def _mixer_sample_kernel(*refs, seq_len, layer, has_prev):
    (x_ref, prev_ref, s0_ref, cos_ref, sin_ref, ng_ref, win_ref, wpool_ref, pscale_ref, gn_ref,
     wout_ref, dmat_ref, qdec_ref, kdec_ref, cdec_ref) = refs[:15]
    y_ref, tail_ref, sfin_ref, ubuf, ret_scr, s_ring, s_sem = refs[16 + 2 * has_prev:]
    if not has_prev:
        _zero_other_layers(tail_ref, layer)
        _zero_other_layers(sfin_ref, layer)
        tail_ref, sfin_ref = tail_ref.at[layer], sfin_ref.at[layer]
    ns = SAMPLE_SEQS
    rows = ns * seq_len

    i, n_steps = pl.program_id(0), pl.num_programs(0)

    def state_copy(step):
        slot = step % STATE_BUFFERS
        return pltpu.make_async_copy(s0_ref.at[layer, pl.ds(step * ns, ns)], s_ring.at[slot],
                                     s_sem.at[slot])

    @pl.when(i == 0)
    def _():
        for step in range(STATE_BUFFERS - 1):
            state_copy(step).start()

    @pl.when(i + STATE_BUFFERS - 1 < n_steps)
    def _():
        state_copy(i + STATE_BUFFERS - 1).start()

    state_copy(i).wait()
    s_in = s_ring.at[i % STATE_BUFFERS]
    x = x_ref[...].reshape(rows, D_MODEL)
    hn = _rmsnorm(x, ng_ref[...]).astype(BF16)
    z = _dot(hn, win_ref[...])
    for j in range(POOL_BUF):
        ubuf[:, POOL_HIST - POOL_BUF + j, :] = prev_ref[j]
    ubuf[:, POOL_HIST:POOL_HIST + seq_len, :] = z[:, :POOL_WIDTH].reshape(ns, seq_len, POOL_WIDTH)
    pool_out = _pool_mix(ubuf, seq_len, 0, POOL_BUF, (wpool_ref,), pscale_ref[...])
    for j in range(POOL_BUF):
        tail_ref[j] = ubuf[:, seq_len + POOL_HIST - POOL_BUF + j, :]

    q_all, k_all, v_all, g_all = _qkvg(z)
    scale = RET_HEAD_DIM ** -0.5
    cos2 = cos_ref[...]
    sin2 = sin_ref[...]
    tok_seq = lax.broadcasted_iota(I32, (RET_HEAD_DIM, rows), 1) // seq_len
    for h in range(RET_HEADS):
        q = _rope(_head(q_all, h), cos2, sin2)
        k = _rope(_head(k_all, h), cos2, sin2) * scale
        vb = _head(v_all, h).astype(BF16)
        s_old = s_in[:, h]
        scores = _dot_nt(q.astype(BF16), k.astype(BF16)) * dmat_ref[h]
        qd = (q * _head(qdec_ref[...], h)).astype(BF16).reshape(ns, seq_len, RET_HEAD_DIM)
        o_state = jnp.einsum('bid,bde->bie', qd, s_old.astype(BF16), preferred_element_type=F32)
        o = _dot(scores.astype(BF16), vb) + o_state.reshape(rows, RET_HEAD_DIM)
        kdt = (k * _head(kdec_ref[...], h)).T
        lhs = jnp.concatenate(
            [jnp.where(tok_seq == b, kdt, 0.0).astype(BF16) for b in range(ns)], axis=0)
        upd = _dot(lhs, vb).reshape(ns, RET_HEAD_DIM, RET_HEAD_DIM)
        sfin_ref[:, h] = s_old * cdec_ref[h] + upd
        on = _group_norm(o) * _head(gn_ref[...], h)
        ret_scr[:, h * RET_HEAD_DIM:(h + 1) * RET_HEAD_DIM] = _silu(_head(g_all, h)) * on

    mix_in = jnp.concatenate([pool_out, ret_scr[...]], axis=-1).astype(BF16)
    y_ref[...] = (x + _dot(mix_in, wout_ref[...])).reshape(ns, seq_len, D_MODEL)


def _router_logits(hn, wr_cat_ref, wr_hi_ref, br_ref, precise_from=0):
    parts = []
    if precise_from:
        parts.append(_dot(hn[:precise_from].astype(BF16), wr_hi_ref[...]))
    if precise_from < hn.shape[0]:
        h_hi, h_lo = _split(hn[precise_from:])
        part = _dot(h_hi, wr_cat_ref[...])
        parts.append(part[:, :LANES] + part[:, LANES:] + _dot(h_lo, wr_hi_ref[...]))
    return jnp.concatenate(parts, axis=0) + br_ref[...]


def _select(group_lg, expert_lg, gidx, eidx, axis):
    red = dict(axis=axis, keepdims=True)
    neg = jnp.float32(-jnp.inf)
    gmax = jnp.max(group_lg, **red)
    g_sel = jnp.min(jnp.where(group_lg == gmax, gidx, N_EXPERT_GROUPS), **red)
    p_sel = 1.0 / jnp.sum(jnp.exp(group_lg - gmax), **red)
    emask = (eidx >> GROUP_SHIFT) == g_sel
    v1 = jnp.max(jnp.where(emask, expert_lg, neg), **red)
    i1 = jnp.min(jnp.where(emask & (expert_lg == v1), eidx, N_EXPERTS), **red)
    emask2 = emask & (eidx != i1)
    v2 = jnp.max(jnp.where(emask2, expert_lg, neg), **red)
    i2 = jnp.min(jnp.where(emask2 & (expert_lg == v2), eidx, N_EXPERTS), **red)
    e2 = jnp.exp(v2 - v1)
    return g_sel, i1, i2, p_sel / (1.0 + e2), p_sel * e2 / (1.0 + e2)


def _route_tile(x, precise_from, ng_ref, wr_cat_ref, wr_hi_ref, br_ref, ha_ref, hbg_ref, idx_ref,
                count_ref, carry_scr):
    rows = x.shape[0]
    hn = _rmsnorm(x, ng_ref[...])
    ha_ref[...], hbg_ref[:, :PACK_W] = _pack_row(hn)
    lgt = _router_logits(hn, wr_cat_ref, wr_hi_ref, br_ref, precise_from).T
    neg = jnp.float32(-jnp.inf)
    gidx = lax.broadcasted_iota(I32, (SUBLANES, rows), 0)
    eidx = lax.broadcasted_iota(I32, (N_EXPERTS, rows), 0)
    group_lg = jnp.where(gidx < N_EXPERT_GROUPS, lgt[0:SUBLANES], neg)
    expert_lg = lgt[EXPERT_LANE0:EXPERT_LANE0 + N_EXPERTS]
    g_sel, i1, i2, w1, w2 = _select(group_lg, expert_lg, gidx, eidx, 0)
    lo = jnp.minimum(i1, i2) - g_sel * EXPERTS_PER_GROUP
    hi = jnp.maximum(i1, i2) - g_sel * EXPERTS_PER_GROUP
    pair = ((lo * (2 * EXPERTS_PER_GROUP - 1 - lo)) >> 1) + (hi - lo - 1)
    cls = g_sel * PAIRS_PER_GROUP + pair
    w_lo = jnp.where(i1 < i2, w1, w2)
    w_hi = jnp.where(i1 < i2, w2, w1)

    crow = lax.broadcasted_iota(I32, (CLASS_ROWS, rows), 0)
    onehot = jnp.where(crow == cls, 1.0, 0.0)
    n_blk = rows // COUNT_BLOCK
    blocks = [onehot[:, j * COUNT_BLOCK:(j + 1) * COUNT_BLOCK] for j in range(n_blk)]
    r = lax.broadcasted_iota(I32, (COUNT_BLOCK, COUNT_BLOCK), 0)
    c = lax.broadcasted_iota(I32, (COUNT_BLOCK, COUNT_BLOCK), 1)
    upper = jnp.where(r < c, 1.0, 0.0).astype(BF16)
    within = _dot(jnp.concatenate(blocks, axis=0).astype(BF16), upper)
    carry = carry_scr[:, 0:1]
    ranks = []
    for j in range(n_blk):
        before = within[j * CLASS_ROWS:(j + 1) * CLASS_ROWS] + carry
        ranks.append(jnp.sum(blocks[j] * before, axis=0, keepdims=True))
        carry = carry + jnp.sum(blocks[j], axis=1, keepdims=True)
    rank = jnp.concatenate(ranks, axis=1)
    carry_scr[...] = jnp.broadcast_to(carry, carry_scr.shape)
    count_ref[...] = carry_scr[...]

    row8 = lax.broadcasted_iota(I32, (SUBLANES, rows), 0)
    idx_ref[...] = jnp.where(row8 == 0, cls, jnp.where(row8 == 1, rank.astype(I32), 0))
    rowl = lax.broadcasted_iota(I32, (LANES, rows), 0)
    gate_rec = jnp.where(rowl == GATE_LO, w_lo, jnp.where(rowl == GATE_HI, w_hi, 0.0)).T
    hbg_ref[:, PACK_W:] = pltpu.bitcast(gate_rec, I32)


def _moe_pair_kernel(ea_ref, eb_ref, used_ref, ha_ref, hbg_ref, wgu_a_ref, wd_a_ref,
                     wgu_b_ref, wd_b_ref, after_ref, ya_ref, yb_ref):
    del ea_ref, eb_ref, after_ref

    @pl.when(pl.program_id(0) < used_ref[0])
    def _():
        h = _unpack_row(ha_ref[...], hbg_ref[:, :PACK_W]).astype(BF16)
        gates = pltpu.bitcast(hbg_ref[:, PACK_W:], F32)
        y = None
        for wgu_ref, wd_ref, lane in ((wgu_a_ref, wd_a_ref, GATE_LO), (wgu_b_ref, wd_b_ref, GATE_HI)):
            gu = _dot(h, wgu_ref[0])
            act = _silu(gu[:, :D_EXPERT]) * gu[:, D_EXPERT:]
            part = _dot((act * gates[:, lane:lane + 1]).astype(BF16), wd_ref[0])
            y = part if y is None else y + part
        ya_ref[...], yb_ref[...] = _pack_row(y)


def _final_norm_kernel(x_ref, ya_ref, yb_ref, g_ref, o_ref):
    o_ref[...] = _rmsnorm(x_ref[...] + _unpack_row(ya_ref[...], yb_ref[...]), g_ref[...])


def _moe_dense_kernel(x_ref, ng_ref, wr_cat_ref, wr_hi_ref, br_ref, wg_ref, wu_ref, wdn_ref, nf_ref,
                      after_ref, y_ref, wgu_ref, wd_ref, h_scr, gate_scr, acc_scr, *, final_norm):
    del after_ref
    e = pl.program_id(1)

    @pl.when(e == 0)
    def _():
        hn = _rmsnorm(x_ref[...], ng_ref[...])
        h_scr[...] = hn.astype(BF16)
        lg = _router_logits(hn, wr_cat_ref, wr_hi_ref, br_ref)
        lane = lax.broadcasted_iota(I32, lg.shape, 1)
        neg = jnp.float32(-jnp.inf)
        group_lg = jnp.where(lane < N_EXPERT_GROUPS, lg, neg)
        is_expert = (lane >= EXPERT_LANE0) & (lane < EXPERT_LANE0 + N_EXPERTS)
        _, i1, i2, w1, w2 = _select(group_lg, jnp.where(is_expert, lg, neg), lane,
                                    jnp.where(is_expert, lane - EXPERT_LANE0, N_EXPERTS), 1)
        gate_scr[...] = (jnp.where(lane == i1 + EXPERT_LANE0, w1, 0.0)
                         + jnp.where(lane == i2 + EXPERT_LANE0, w2, 0.0))
        acc_scr[...] = jnp.zeros_like(acc_scr)

    lane = lax.broadcasted_iota(I32, gate_scr.shape, 1)
    gate = jnp.sum(jnp.where(lane == EXPERT_LANE0 + e, gate_scr[...], 0.0), axis=-1, keepdims=True)
    wgu_ref[0, :, :D_EXPERT] = wg_ref[0].astype(BF16)
    wgu_ref[0, :, D_EXPERT:] = wu_ref[0].astype(BF16)
    wd_ref[0] = wdn_ref[0].astype(BF16)
    gu = _dot(h_scr[...], wgu_ref[0])
    act = _silu(gu[:, :D_EXPERT]) * gu[:, D_EXPERT:]
    acc_scr[...] += _dot((act * gate).astype(BF16), wd_ref[0])

    @pl.when(e == N_EXPERTS - 1)
    def _():
        y = x_ref[...] + acc_scr[...]
        if final_norm:
            y = _rmsnorm(y, nf_ref[...])
        y_ref[...] = y


def _const_spec(shape):
    return pl.BlockSpec(shape, lambda *_: (0,) * len(shape))


_ANY_SPEC = pl.BlockSpec(memory_space=pl.ANY)


def _rope_tables(pos):
    half = RET_HEAD_DIM // 2
    inv = np.float32(ROPE_BASE) ** (-np.arange(half, dtype=np.float32) / np.float32(half))
    ang = pos.astype(np.float32)[:, None] * inv[None, :]
    cos, sin = np.cos(ang), np.sin(ang)
    return np.concatenate([cos, cos], -1), np.concatenate([-sin, sin], -1)


def _decay_consts(c, reps):
    f = np.float32
    log_g = np.log(f(1.0) - f(2.0) ** (f(-5.0) - np.arange(RET_HEADS, dtype=f)))
    i = np.arange(c, dtype=f)
    diff = i[:, None] - i[None, :]
    dmat = np.where(diff[None] >= 0, np.exp(np.maximum(diff, f(0))[None] * log_g[:, None, None]), f(0))
    qdec = np.exp((i + f(1))[None, :] * log_g[:, None])
    kdec = np.exp((f(c) - f(1) - i)[None, :] * log_g[:, None])
    cdec = np.exp(f(c) * log_g)
    dmat = np.einsum('ab,hij->haibj', np.eye(reps, dtype=f), dmat).reshape(RET_HEADS, reps * c, reps * c)
    lanes = lambda t: np.repeat(np.tile(t, (1, reps)).T, RET_HEAD_DIM, axis=1)
    cdec = np.broadcast_to(cdec[:, None, None], (RET_HEADS, 1, RET_HEAD_DIM))
    return tuple(np.ascontiguousarray(a, dtype=f) for a in (dmat, lanes(qdec), lanes(kdec), cdec))


def _split_weight(w):
    hi = lax.bitcast_convert_type(_bf16_round_bits(lax.bitcast_convert_type(w, jnp.uint32)), F32)
    return hi.astype(BF16), (w - hi).astype(BF16)


_W_IN_BLOCK = (None, D_MODEL, IN_WIDTH)
_W_POOL_BLOCK = (None, len(POOL_WINDOWS), POOL_GROUP_DIM, POOL_GROUP_DIM)
_W_OUT_BLOCK = (None, D_MODEL, D_MODEL)
_DECAY_SPECS = [
    _const_spec((RET_HEADS, CHUNK, CHUNK)), _const_spec((CHUNK, RET_WIDTH)),
    _const_spec((CHUNK, RET_WIDTH)), _const_spec((RET_HEADS, 1, RET_HEAD_DIM)),
]


def _layer_spec(block, layer, resident=False):
    index_map = lambda *_: (layer,) + (0,) * (len(block) - 1)
    if resident:
        return pl.BlockSpec(block, index_map, pipeline_mode=pl.Buffered(1))
    return pl.BlockSpec(block, index_map)


def _mixer_weight_specs(layer, split):
    n = 2 if split else 1
    row = lambda width: _layer_spec((None, 1, width), layer)
    return [row(D_MODEL), *[_layer_spec(_W_IN_BLOCK, layer, True)] * n,
            *[_layer_spec(_W_POOL_BLOCK, layer)] * n, row(POOL_WIDTH), row(RET_WIDTH),
            *[_layer_spec(_W_OUT_BLOCK, layer, True)] * n]


def _chain_out_spec(block, depth, layer, has_prev):
    zeros = (0,) * (len(block) - 1)
    if has_prev:
        return pl.BlockSpec((None,) + block, lambda i, *_: (layer, i) + zeros)
    return pl.BlockSpec((depth,) + block, lambda i, *_: (0, i) + zeros)


def _mixer_prompt(x, moe_y, weights, route_w, prev_out, layer, rows, precise_tail, after):
    b, l, _ = x.shape
    t = b * l
    depth = weights[0].shape[0]
    assert l % rows == 0 and rows % CHUNK == 0
    cos2, sin2 = _rope_tables(np.arange(l))
    decay = _decay_consts(RET_CHUNK, 1)
    steps = l // rows
    tok = lambda i, c: (i, c, 0)
    if moe_y is None:
        ya = yb = jnp.zeros((1, rows, PACK_W), I32)
        y_spec = _const_spec((1, rows, PACK_W))
    else:
        ya, yb = (t.reshape(b, l, PACK_W) for t in moe_y)
        y_spec = pl.BlockSpec((1, rows, PACK_W), tok)
    has_prev = prev_out is not None
    n_in = 23
    flat = lambda i, c: (i * steps + c, 0)
    y, tails, states, *routed = pl.pallas_call(
        functools.partial(_mixer_prompt_kernel, rows=rows, steps=steps, moe_in=moe_y is not None,
                          precise_tail=precise_tail, layer=layer, has_prev=has_prev),
        grid=(b, steps),
        in_specs=[pl.BlockSpec((1, rows, D_MODEL), tok), y_spec, y_spec,
                  pl.BlockSpec((rows, RET_HEAD_DIM), lambda i, c: (c, 0)),
                  pl.BlockSpec((rows, RET_HEAD_DIM), lambda i, c: (c, 0)),
                  *_mixer_weight_specs(layer, True), *_DECAY_SPECS,
                  _layer_spec((None, 1, D_MODEL), layer), *_router_specs(layer), _ANY_SPEC,
                  *([_ANY_SPEC, _ANY_SPEC] if has_prev else [])],
        out_specs=[pl.BlockSpec((1, rows, D_MODEL), tok),
                   _chain_out_spec((1, POOL_BUF, POOL_WIDTH), depth, layer, has_prev),
                   _chain_out_spec((1, RET_HEADS, RET_HEAD_DIM, RET_HEAD_DIM), depth, layer, has_prev),
                   pl.BlockSpec((rows, PACK_W), flat), pl.BlockSpec((rows, PACK_W + LANES), flat),
                   pl.BlockSpec((SUBLANES, rows), lambda i, c: (0, i * steps + c)),
                   _const_spec((CLASS_ROWS, LANES))],
        out_shape=[jax.ShapeDtypeStruct(x.shape, F32),
                   jax.ShapeDtypeStruct((depth, b, POOL_BUF, POOL_WIDTH), F32),
                   jax.ShapeDtypeStruct((depth, b, RET_HEADS, RET_HEAD_DIM, RET_HEAD_DIM), F32),
                   jax.ShapeDtypeStruct((t, PACK_W), I32),
                   jax.ShapeDtypeStruct((t, PACK_W + LANES), I32), jax.ShapeDtypeStruct((SUBLANES, t), I32),
                   jax.ShapeDtypeStruct((CLASS_ROWS, LANES), F32)],
        input_output_aliases={n_in: 1, n_in + 1: 2} if has_prev else {},
        scratch_shapes=[pltpu.VMEM((1, POOL_HIST + rows, POOL_WIDTH), F32),
                        pltpu.VMEM((RET_HEADS, RET_HEAD_DIM, RET_HEAD_DIM), F32),
                        pltpu.VMEM((rows, RET_WIDTH), F32),
                        pltpu.VMEM((rows, IN_WIDTH), F32),
                        pltpu.VMEM((CLASS_ROWS, LANES), F32)],
        compiler_params=pltpu.CompilerParams(
            dimension_semantics=("arbitrary", "arbitrary"), vmem_limit_bytes=VMEM_LIMIT),
        name="mixer_prompt",
    )(x, ya, yb, cos2, sin2, *weights, *decay, *route_w, after, *(prev_out if has_prev else ()))
    return y, (tails, states), tuple(routed)


def _mixer_sample(x, pool_prev, s0, weights, prev_out, layer, after):
    b, l, _ = x.shape
    depth = s0.shape[0]
    ns = SAMPLE_SEQS
    assert ns * l == CHUNK and b % ns == 0
    cos2, sin2 = _rope_tables(PAST_LEN + np.arange(l))
    cos2, sin2 = np.tile(cos2, (ns, 1)), np.tile(sin2, (ns, 1))
    decay = _decay_consts(l, ns)
    seq3 = lambda i: (i, 0, 0)
    state_block = (ns, RET_HEADS, RET_HEAD_DIM, RET_HEAD_DIM)
    tail_block = (POOL_BUF, ns, POOL_WIDTH)
    if prev_out is None:
        tail_out = pl.BlockSpec((depth,) + tail_block, lambda i: (0, 0, i, 0))
    else:
        tail_out = pl.BlockSpec((None,) + tail_block, lambda i: (layer, 0, i, 0))
    has_prev = prev_out is not None
    n_in = 16
    y, tails, states = pl.pallas_call(
        functools.partial(_mixer_sample_kernel, seq_len=l, layer=layer, has_prev=has_prev),
        grid=(b // ns,),
        in_specs=[pl.BlockSpec((ns, l, D_MODEL), seq3),
                  pl.BlockSpec((None,) + tail_block, lambda i: (layer, 0, i, 0)),
                  _ANY_SPEC,
                  _const_spec((CHUNK, RET_HEAD_DIM)), _const_spec((CHUNK, RET_HEAD_DIM)),
                  *_mixer_weight_specs(layer, False), *_DECAY_SPECS, _ANY_SPEC,
                  *([_ANY_SPEC, _ANY_SPEC] if has_prev else [])],
        out_specs=[pl.BlockSpec((ns, l, D_MODEL), seq3), tail_out,
                   _chain_out_spec(state_block, depth, layer, has_prev)],
        out_shape=[jax.ShapeDtypeStruct(x.shape, F32),
                   jax.ShapeDtypeStruct((depth, POOL_BUF, b, POOL_WIDTH), F32),
                   jax.ShapeDtypeStruct(s0.shape, F32)],
        input_output_aliases={n_in: 1, n_in + 1: 2} if has_prev else {},
        scratch_shapes=[pltpu.VMEM((ns, POOL_HIST + l, POOL_WIDTH), F32),
                        pltpu.VMEM((CHUNK, RET_WIDTH), F32),
                        pltpu.VMEM((STATE_BUFFERS,) + state_block, F32),
                        pltpu.SemaphoreType.DMA((STATE_BUFFERS,))],
        compiler_params=pltpu.CompilerParams(
            dimension_semantics=("arbitrary",), vmem_limit_bytes=VMEM_LIMIT),
        name="mixer_sample",
    )(x, pool_prev, s0, cos2, sin2, *weights, *decay, after, *(prev_out if has_prev else ()))
    return y, (tails, states)


def _router_weights(w_rg, b_rg, w_re, b_re):
    depth = w_rg.shape[0]
    gap = EXPERT_LANE0 - N_EXPERT_GROUPS
    rest = LANES - EXPERT_LANE0 - N_EXPERTS
    wr = jnp.concatenate([w_rg, jnp.zeros((depth, D_MODEL, gap), F32), w_re,
                          jnp.zeros((depth, D_MODEL, rest), F32)], axis=-1)
    br = jnp.concatenate([b_rg, jnp.zeros((depth, gap), F32), b_re, jnp.zeros((depth, rest), F32)],
                         axis=-1).reshape(depth, 1, LANES)
    wr_hi, wr_lo = _split_weight(wr)
    return jnp.concatenate([wr_hi, wr_lo], axis=-1), wr_hi, br


def _router_specs(layer):
    return [_layer_spec((None, D_MODEL, 2 * LANES), layer), _layer_spec((None, D_MODEL, LANES), layer),
            _layer_spec((None, 1, LANES), layer)]


def _sc_mesh():
    return plsc.VectorSubcoreMesh(core_axis_name="core", subcore_axis_name="subcore",
                                  num_cores=SC_CORES, num_subcores=SC_SUBCORES)


def _sc_params():
    params = pltpu.CompilerParams()
    if "needs_layout_passes" in pltpu.CompilerParams.__dataclass_fields__:
        params = dataclasses.replace(params, needs_layout_passes=False)
    return params


def _sc_gather(tables, idx, after):
    n = idx.shape[0]
    assert n % SC_WINDOW == 0
    nt = len(tables)

    def body(*refs):
        i_hbm = refs[nt]
        for t_hbm, o_hbm in zip(refs[:nt], refs[nt + 2:]):
            def gather_window(i_vmem, o_vmem, t_hbm=t_hbm):
                pltpu.sync_copy(t_hbm.at[i_vmem.at[0]], o_vmem)

            pltpu.emit_pipeline(
                gather_window, grid=(n // SC_WINDOW,),
                in_specs=[pl.BlockSpec((1, SC_WINDOW), lambda i: (0, i))],
                out_specs=[pl.BlockSpec((SC_WINDOW, t_hbm.shape[1]), lambda i: (i, 0))],
                core_axis_name=("core", "subcore"),
                dimension_semantics=(pltpu.PARALLEL,),
            )(i_hbm, o_hbm)

    out_type = tuple(jax.ShapeDtypeStruct((n, t.shape[1]), t.dtype) for t in tables)
    return pl.kernel(body, out_type=out_type, mesh=_sc_mesh(), name="sc_gather")(
        *tables, idx.reshape(1, n), after)


def _sc_slots(cls, rank, starts, n_slots):
    t = cls.shape[0]
    workers = SC_CORES * SC_SUBCORES
    slot_per, tok_per = n_slots // workers, t // workers
    assert n_slots % (workers * SC_LANES) == 0 and t % (workers * SC_LANES) == 0 and t & (t - 1) == 0
    assert n_slots % (SC_LANES * SC_UNROLL) == 0 and t % (SC_LANES * SC_UNROLL) == 0

    def body(cls_hbm, rank_hbm, starts_hbm, pos_hbm, slot_hbm, cls_v, rank_v, starts_v, pos_v, slot_v):
        wid = lax.axis_index("subcore") * SC_CORES + lax.axis_index("core")
        pltpu.sync_copy(cls_hbm, cls_v)
        pltpu.sync_copy(rank_hbm, rank_v)
        pltpu.sync_copy(starts_hbm, starts_v)

        lane = lax.iota(I32, SC_LANES)
        span = SC_LANES * SC_UNROLL

        @pl.loop(0, n_slots, step=span)
        def _(i):
            for u in range(SC_UNROLL):
                j = i + u * SC_LANES
                slot_v[pl.ds(j, SC_LANES)] = (lane + j) & (t - 1)

        @pl.loop(0, t, step=span)
        def _(i):
            for u in range(SC_UNROLL):
                j = i + u * SC_LANES
                at = pl.ds(j, SC_LANES)
                pos = plsc.load_gather(starts_v, [cls_v[at]]) + rank_v[at]
                pos_v[at] = pos
                plsc.store_scatter(slot_v, [pos], lane + j)

        tok_off = pl.multiple_of(wid * tok_per, SC_LANES)
        pltpu.sync_copy(pos_v.at[pl.ds(tok_off, tok_per)], pos_hbm.at[pl.ds(tok_off, tok_per)])
        slot_off = pl.multiple_of(wid * slot_per, SC_LANES)
        pltpu.sync_copy(slot_v.at[pl.ds(slot_off, slot_per)], slot_hbm.at[pl.ds(slot_off, slot_per)])

    return pl.kernel(
        body, mesh=_sc_mesh(), compiler_params=_sc_params(), name="sc_slots",
        out_type=(jax.ShapeDtypeStruct((t,), I32), jax.ShapeDtypeStruct((n_slots,), I32)),
        scratch_types=[pltpu.VMEM((t,), I32), pltpu.VMEM((t,), I32), pltpu.VMEM((CLASS_ROWS,), I32),
                       pltpu.VMEM((t,), I32), pltpu.VMEM((n_slots,), I32)],
    )(cls, rank, starts)


def _moe_dispatch(routed, after):
    ha, hbg, idx, counts = routed
    t = ha.shape[0]
    n_slots = t + N_CLASSES * PAIR_TILE
    n_tiles = n_slots // PAIR_TILE

    cnt = counts[:, 0].astype(I32)
    padded = (cnt + PAIR_TILE - 1) // PAIR_TILE * PAIR_TILE
    ends = jnp.cumsum(padded)
    starts = ends - padded
    tile_cls = jnp.minimum(
        jnp.sum(ends[None, :N_CLASSES] <= (jnp.arange(n_tiles, dtype=I32) * PAIR_TILE)[:, None], axis=1),
        N_CLASSES - 1).astype(I32)
    first = (tile_cls // PAIRS_PER_GROUP) * EXPERTS_PER_GROUP
    tile_ea = first + jnp.asarray(PAIR_LO, I32)[tile_cls % PAIRS_PER_GROUP]
    tile_eb = first + jnp.asarray(PAIR_HI, I32)[tile_cls % PAIRS_PER_GROUP]
    used = (ends[N_CLASSES - 1:N_CLASSES] // PAIR_TILE).astype(I32)

    pos, slot_tok = _sc_slots(idx[0], idx[1], starts, n_slots)
    hsa, hsbg = _sc_gather((ha, hbg), slot_tok, after=after)
    return (tile_ea, tile_eb, used, hsa, hsbg), pos, slot_tok


def _moe_pair(dispatched, wgu, wd, after):
    tile_ea, tile_eb, used, hsa, hsbg = dispatched
    n_slots = hsa.shape[0]
    n_tiles = n_slots // PAIR_TILE
    row = lambda i, ea, eb, nu: (jnp.minimum(i, nu[0] - 1), 0)
    w_spec = lambda shape, which: pl.BlockSpec(
        (1,) + shape, lambda i, ea, eb, nu: ((ea, eb)[which][i], 0, 0))
    gu_shape, d_shape = (D_MODEL, 2 * D_EXPERT), (D_EXPERT, D_MODEL)
    ysa, ysb = pl.pallas_call(
        _moe_pair_kernel,
        grid_spec=pltpu.PrefetchScalarGridSpec(
            num_scalar_prefetch=3, grid=(n_tiles,),
            in_specs=[pl.BlockSpec((PAIR_TILE, PACK_W), row),
                      pl.BlockSpec((PAIR_TILE, PACK_W + LANES), row),
                      w_spec(gu_shape, 0), w_spec(d_shape, 0), w_spec(gu_shape, 1), w_spec(d_shape, 1),
                      _ANY_SPEC],
            out_specs=[pl.BlockSpec((PAIR_TILE, PACK_W), row), pl.BlockSpec((PAIR_TILE, PACK_W), row)]),
        out_shape=[jax.ShapeDtypeStruct((n_slots, PACK_W), I32),
                   jax.ShapeDtypeStruct((n_slots, PACK_W), I32)],
        compiler_params=pltpu.CompilerParams(
            dimension_semantics=("arbitrary",), vmem_limit_bytes=VMEM_LIMIT),
        name="moe_pair",
    )(tile_ea, tile_eb, used, hsa, hsbg, wgu, wd, wgu, wd, after)
    return ysa, ysb


def _final_norm(x, moe_y, g, rows):
    t = x.shape[0]
    rows = min(rows, t)
    assert t % rows == 0
    tok = lambda i: (i, 0)
    return pl.pallas_call(
        _final_norm_kernel,
        grid=(t // rows,),
        in_specs=[pl.BlockSpec((rows, D_MODEL), tok), pl.BlockSpec((rows, PACK_W), tok),
                  pl.BlockSpec((rows, PACK_W), tok), _const_spec((1, D_MODEL))],
        out_specs=pl.BlockSpec((rows, D_MODEL), tok),
        out_shape=jax.ShapeDtypeStruct(x.shape, F32),
        compiler_params=pltpu.CompilerParams(
            dimension_semantics=("arbitrary",), vmem_limit_bytes=VMEM_LIMIT),
        name="final_norm",
    )(x, *moe_y, g.reshape(1, D_MODEL))


def _moe_dense(x, norm_g, router, w_gate, w_up, w_down, norm_final, layer, final_norm, rows, after):
    t = x.shape[0]
    assert t == rows
    tok = lambda i, e: (i, 0)
    w_spec = lambda shape: pl.BlockSpec((None, 1) + shape, lambda i, e: (layer, e, 0, 0))
    w_out = lambda shape: pl.BlockSpec((1,) + shape, lambda i, e: (e, 0, 0))
    return pl.pallas_call(
        functools.partial(_moe_dense_kernel, final_norm=final_norm),
        grid=(t // rows, N_EXPERTS),
        in_specs=[pl.BlockSpec((rows, D_MODEL), tok), _layer_spec((None, 1, D_MODEL), layer),
                  *_router_specs(layer),
                  w_spec((D_MODEL, D_EXPERT)), w_spec((D_MODEL, D_EXPERT)), w_spec((D_EXPERT, D_MODEL)),
                  _const_spec((1, D_MODEL)), _ANY_SPEC],
        out_specs=[pl.BlockSpec((rows, D_MODEL), tok), w_out((D_MODEL, 2 * D_EXPERT)),
                   w_out((D_EXPERT, D_MODEL))],
        out_shape=[jax.ShapeDtypeStruct(x.shape, F32),
                   jax.ShapeDtypeStruct((N_EXPERTS, D_MODEL, 2 * D_EXPERT), BF16),
                   jax.ShapeDtypeStruct((N_EXPERTS, D_EXPERT, D_MODEL), BF16)],
        scratch_shapes=[pltpu.VMEM((rows, D_MODEL), BF16),
                        pltpu.VMEM((rows, LANES), F32),
                        pltpu.VMEM((rows, D_MODEL), F32)],
        compiler_params=pltpu.CompilerParams(
            dimension_semantics=("arbitrary", "arbitrary"), vmem_limit_bytes=VMEM_LIMIT),
        name="moe_dense",
    )(x, norm_g, *router, w_gate, w_up, w_down, norm_final.reshape(1, D_MODEL), after)


def kernel(x_prompt, x_sample, cache_pool, state_ret, norm_mix, w_in, w_pool, pool_scale, ret_gn, w_out, norm_ffn, w_router_group, b_router_group, w_router_expert, b_router_expert, w_gate, w_up, w_down, norm_final):
    depth = norm_mix.shape[0]
    row = lambda a: a.reshape(depth, 1, a.shape[-1])
    mix_split = (row(norm_mix), *_split_weight(w_in), *_split_weight(w_pool), row(pool_scale),
                 row(ret_gn), *_split_weight(w_out))
    mix_hi = tuple(mix_split[i] for i in (0, 1, 3, 5, 6, 7))
    router = _router_weights(w_router_group, b_router_group, w_router_expert, b_router_expert)
    norm_ffn = row(norm_ffn)
    pool_prev = jnp.swapaxes(cache_pool, 1, 2)

    yp, ys = x_prompt, x_sample
    moe_p = None
    out_p = out_s = None
    ys_mix = None
    for l in range(depth):
        yp, out_p, routed = _mixer_prompt(
            yp, moe_p, mix_split, (norm_ffn, *router), out_p, l, rows=512,
            precise_tail=PRECISE_TAIL_STEPS if l < depth - 1 else 0,
            after=ys if ys_mix is None else ys_mix)
        if ys_mix is None:
            ys_mix, out_s = _mixer_sample(ys, pool_prev, state_ret, mix_hi, out_s, l, after=routed[-1])
        dispatched, pos, slot_tok = _moe_dispatch(routed, after=ys_mix if l == 0 else routed[-1])
        ys, wgu, wd = _moe_dense(ys_mix.reshape(-1, D_MODEL), norm_ffn, router, w_gate, w_up, w_down,
                                 norm_final, l, l == depth - 1, rows=ys_mix.shape[0] * ys_mix.shape[1],
                                 after=slot_tok)
        ys = ys.reshape(x_sample.shape)
        sorted_y = _moe_pair(dispatched, wgu, wd, after=ys)
        moe_p = _sc_gather(sorted_y, pos, after=routed[-1])
        if l + 1 < depth:
            ys_mix, out_s = _mixer_sample(ys, pool_prev, state_ret, mix_hi, out_s, l + 1,
                                          after=sorted_y[0])
    yp = _final_norm(yp.reshape(-1, D_MODEL), moe_p, norm_final, rows=2048).reshape(yp.shape)
    return (yp, ys, *out_p, jnp.swapaxes(out_s[0], 1, 2), out_s[1])
```

```python
import dataclasses
import functools
import itertools

import jax
import jax.numpy as jnp
import numpy as np
from jax import lax
from jax.experimental import pallas as pl
from jax.experimental.pallas import tpu as pltpu
from jax.experimental.pallas import tpu_sc as plsc

F32 = jnp.float32
BF16 = jnp.bfloat16
I32 = jnp.int32

D_MODEL = 1024
POOL_WIDTH = 512
POOL_WINDOWS = (2, 4, 8, 16)
POOL_GROUP_DIM = 128
POOL_BUF = 15
POOL_HIST = 16
RET_WIDTH = 512
RET_HEADS = 4
RET_HEAD_DIM = 128
RET_CHUNK = 128
ROPE_BASE = 10000.0
IN_WIDTH = POOL_WIDTH + 4 * RET_WIDTH
N_EXPERT_GROUPS = 4
EXPERTS_PER_GROUP = 4
N_EXPERTS = N_EXPERT_GROUPS * EXPERTS_PER_GROUP
D_EXPERT = 256
RMS_EPS = 1e-6
GN_EPS = 1e-5
PAST_LEN = 16384

LANES = 128
SUBLANES = 8
EXPERT_LANE0 = 8
GROUP_SHIFT = EXPERTS_PER_GROUP.bit_length() - 1
PAIRS = tuple(itertools.combinations(range(EXPERTS_PER_GROUP), 2))
PAIRS_PER_GROUP = len(PAIRS)
PAIR_LO, PAIR_HI = zip(*PAIRS)
N_CLASSES = N_EXPERT_GROUPS * PAIRS_PER_GROUP
CLASS_ROWS = 32
GATE_LO, GATE_HI = 0, 1
COUNT_BLOCK = 256
PAIR_TILE = 256
PACK_W = D_MODEL // 4
SC_CORES, SC_SUBCORES, SC_LANES = 2, 16, 16
SC_WINDOW = 128
SC_UNROLL = 8
PRECISE_TAIL_STEPS = 1
CHUNK = 128
SAMPLE_SEQS = 16
DENSE_HEAD = 4
VMEM_LIMIT = 56 * 1024 * 1024


def _dot(a, b):
    return jnp.dot(a, b, preferred_element_type=F32)


def _dot_nt(a, b):
    return lax.dot_general(a, b, (((1,), (1,)), ((), ())), preferred_element_type=F32)


def _bf16_round_bits(u):
    return (u + jnp.uint32(0x7FFF) + ((u >> 16) & jnp.uint32(1))) & jnp.uint32(0xFFFF0000)


def _split(a):
    hi = pltpu.bitcast(_bf16_round_bits(pltpu.bitcast(a, jnp.uint32)), F32)
    return hi.astype(BF16), (a - hi).astype(BF16)


def _mm(a, b, precise, nt=False):
    dot = _dot_nt if nt else _dot
    if precise:
        b_hi, b_lo = b if isinstance(b, tuple) else _split(b)
        a_hi, a_lo = _split(a)
        return dot(a_hi, b_hi) + dot(a_lo, b_hi) + dot(a_hi, b_lo)
    return dot(a.astype(BF16), b[0] if isinstance(b, tuple) else b.astype(BF16))


def _rmsnorm(x, g):
    ms = jnp.mean(x * x, axis=-1, keepdims=True)
    return x * lax.rsqrt(ms + RMS_EPS) * g


def _pool_mix(ubuf, rows, t_first, n_prev, wpool_refs, pscale, precise=False, row0=0):
    ns = ubuf.shape[0]
    t = t_first + lax.broadcasted_iota(I32, (1, rows, POOL_GROUP_DIM), 1)
    base = POOL_HIST + row0
    outs = []
    for j, w in enumerate(POOL_WINDOWS):
        lanes = slice(j * POOL_GROUP_DIM, (j + 1) * POOL_GROUP_DIM)
        uj = ubuf[:, base:base + rows, lanes]
        acc = uj
        for i in range(1, w):
            acc = acc + ubuf[:, base - i:base - i + rows, lanes]
        cnt = jnp.minimum(w, n_prev + t + 1).astype(F32)
        d = (acc / cnt - uj).reshape(ns * rows, POOL_GROUP_DIM)
        outs.append(_mm(d, tuple(w[j] for w in wpool_refs), precise))
    return jnp.concatenate(outs, axis=-1) * pscale


def _rope(xh, cos2, sin2):
    return xh * cos2 + pltpu.roll(xh, RET_HEAD_DIM // 2, 1) * sin2


def _group_norm(o):
    mu = jnp.mean(o, axis=-1, keepdims=True)
    c = o - mu
    var = jnp.mean(c * c, axis=-1, keepdims=True)
    return c * lax.rsqrt(var + GN_EPS)


def _silu(x):
    return x * (1.0 / (1.0 + jnp.exp(-x)))


def _head(a, h):
    return a[:, h * RET_HEAD_DIM:(h + 1) * RET_HEAD_DIM]


def _qkvg(z):
    p, r = POOL_WIDTH, RET_WIDTH
    return z[:, p:p + r], z[:, p + r:p + 2 * r], z[:, p + 2 * r:p + 3 * r], z[:, p + 3 * r:p + 4 * r]


def _pack_bf16_pair(a, b):
    ua = pltpu.bitcast(a.astype(BF16).astype(F32), jnp.uint32)
    ub = pltpu.bitcast(b.astype(BF16).astype(F32), jnp.uint32)
    return pltpu.bitcast((ua >> 16) | (ub & jnp.uint32(0xFFFF0000)), I32)


def _unpack_bf16_pair(w):
    u = pltpu.bitcast(w, jnp.uint32)
    return pltpu.bitcast(u << 16, F32), pltpu.bitcast(u & jnp.uint32(0xFFFF0000), F32)


def _pack_row(y):
    q = PACK_W
    return _pack_bf16_pair(y[:, 0:q], y[:, q:2 * q]), _pack_bf16_pair(y[:, 2 * q:3 * q], y[:, 3 * q:])


def _unpack_row(wa, wb):
    return jnp.concatenate([*_unpack_bf16_pair(wa), *_unpack_bf16_pair(wb)], axis=-1)


def _zero_other_layers(ref, layer):
    for j in range(ref.shape[0]):
        if j != layer:
            ref[j] = jnp.zeros(ref.shape[1:], ref.dtype)


def _mixer_prompt_kernel(*refs, rows, steps, moe_in, precise_tail, layer, has_prev):
    (x_ref, ya_ref, yb_ref, cos_ref, sin_ref, ng_ref, win_hi_ref, win_lo_ref, wpool_hi_ref,
     wpool_lo_ref, pscale_ref, gn_ref, wout_hi_ref, wout_lo_ref, dmat_ref, qdec_ref, kdec_ref,
     cdec_ref) = refs[:18]
    route_in = refs[18:22]
    y_ref, tail_ref, sfin_ref, *route_out, ubuf, s_scr, ret_scr, z_scr, count_scr = (
        refs[23 + 2 * has_prev:])
    c = pl.program_id(1)

    @pl.when((pl.program_id(0) == 0) & (c == 0))
    def _():
        count_scr[...] = jnp.zeros_like(count_scr)
    if not has_prev:
        _zero_other_layers(tail_ref, layer)
        _zero_other_layers(sfin_ref, layer)
        tail_ref, sfin_ref = tail_ref.at[layer], sfin_ref.at[layer]

    @pl.when(c == 0)
    def _():
        ubuf[:, 0:POOL_HIST, :] = jnp.zeros((1, POOL_HIST, POOL_WIDTH), F32)
        s_scr[...] = jnp.zeros_like(s_scr)

    kv_cols = slice(POOL_WIDTH + RET_WIDTH, POOL_WIDTH + 3 * RET_WIDTH)

    def step(kv_precise, full_from):
        x = x_ref[0]
        if moe_in:
            x = x + _unpack_row(ya_ref[0], yb_ref[0])
        hn = _rmsnorm(x, ng_ref[...])
        hi, lo = _split(hn) if kv_precise else (hn.astype(BF16), None)
        z_scr[...] = _dot(hi, win_hi_ref[...])
        if kv_precise and full_from:
            z_scr[:full_from, kv_cols] += (_dot(lo[:full_from], win_hi_ref[:, kv_cols])
                                           + _dot(hi[:full_from], win_lo_ref[:, kv_cols]))
        if full_from < rows:
            z_scr[full_from:, :] += (_dot(lo[full_from:], win_hi_ref[...])
                                     + _dot(hi[full_from:], win_lo_ref[...]))

        ubuf[0, POOL_HIST:POOL_HIST + rows, :] = z_scr[:, :POOL_WIDTH]
        pool_w = (wpool_hi_ref, wpool_lo_ref)
        pool_parts = []
        if full_from:
            pool_parts.append(_pool_mix(ubuf, full_from, c * rows, 0, pool_w, pscale_ref[...]))
        if full_from < rows:
            pool_parts.append(_pool_mix(ubuf, rows - full_from, c * rows + full_from, 0, pool_w,
                                        pscale_ref[...], precise=True, row0=full_from))
        pool_out = jnp.concatenate(pool_parts, axis=0)
        tail_ref[...] = ubuf[:, rows + POOL_HIST - POOL_BUF:rows + POOL_HIST, :]
        ubuf[:, 0:POOL_HIST, :] = ubuf[:, rows:rows + POOL_HIST, :]

        scale = RET_HEAD_DIM ** -0.5
        for ci in range(rows // CHUNK):
            rs = slice(ci * CHUNK, (ci + 1) * CHUNK)
            full = ci * CHUNK >= full_from
            cos2 = cos_ref[rs, :]
            sin2 = sin_ref[rs, :]
            for h in range(RET_HEADS):
                col = lambda part: slice(POOL_WIDTH + part * RET_WIDTH + h * RET_HEAD_DIM,
                                         POOL_WIDTH + part * RET_WIDTH + (h + 1) * RET_HEAD_DIM)
                q = _rope(z_scr[rs, col(0)], cos2, sin2)
                k = _rope(z_scr[rs, col(1)], cos2, sin2) * scale
                v = z_scr[rs, col(2)]
                s_old = s_scr[h]
                scores = _mm(q, k, full, nt=True) * dmat_ref[h]
                qd = q * _head(qdec_ref[...], h)
                o = _mm(scores, v, full) + _mm(qd, s_old, full)
                kd = k * _head(kdec_ref[...], h)
                s_scr[h] = s_old * cdec_ref[h] + _mm(kd.T, v, kv_precise)
                on = _group_norm(o) * _head(gn_ref[...], h)
                ret_scr[rs, h * RET_HEAD_DIM:(h + 1) * RET_HEAD_DIM] = _silu(z_scr[rs, col(3)]) * on

        mix_in = jnp.concatenate([pool_out, ret_scr[...]], axis=-1)
        y_ref[0] = x + _dot(mix_in.astype(BF16), wout_hi_ref[...])
        if full_from < rows:
            m_hi, m_lo = _split(mix_in[full_from:])
            y_ref[0, full_from:, :] += _dot(m_lo, wout_hi_ref[...]) + _dot(m_hi, wout_lo_ref[...])
        sfin_ref[0] = s_scr[...]
        _route_tile(y_ref[0], full_from, *route_in, *route_out, count_scr)

    if precise_tail:
        pl.when(c < steps - precise_tail)(lambda: step(False, rows))
        if precise_tail > 1:
            pl.when((c >= steps - precise_tail) & (c < steps - 1))(lambda: step(True, rows))
        pl.when(c == steps - 1)(lambda: step(True, rows - CHUNK))
    else:
        step(False, rows)


def _mixer_sample_kernel(*refs, seq_len, layer, has_prev):
    (x_ref, prev_ref, s0_ref, cos_ref, sin_ref, ng_ref, win_ref, wpool_ref, pscale_ref, gn_ref,
     wout_ref, dmat_ref, qdec_ref, kdec_ref, cdec_ref) = refs[:15]
    y_ref, tail_ref, sfin_ref, ubuf, ret_scr = refs[16 + 2 * has_prev:]
    if not has_prev:
        _zero_other_layers(tail_ref, layer)
        _zero_other_layers(sfin_ref, layer)
        tail_ref, sfin_ref = tail_ref.at[layer], sfin_ref.at[layer]
    ns = SAMPLE_SEQS
    rows = ns * seq_len
    x = x_ref[...].reshape(rows, D_MODEL)
    hn = _rmsnorm(x, ng_ref[...]).astype(BF16)
    z = _dot(hn, win_ref[...])
    for j in range(POOL_BUF):
        ubuf[:, POOL_HIST - POOL_BUF + j, :] = prev_ref[j]
    ubuf[:, POOL_HIST:POOL_HIST + seq_len, :] = z[:, :POOL_WIDTH].reshape(ns, seq_len, POOL_WIDTH)
    pool_out = _pool_mix(ubuf, seq_len, 0, POOL_BUF, (wpool_ref,), pscale_ref[...])
    for j in range(POOL_BUF):
        tail_ref[j] = ubuf[:, seq_len + POOL_HIST - POOL_BUF + j, :]

    q_all, k_all, v_all, g_all = _qkvg(z)
    scale = RET_HEAD_DIM ** -0.5
    cos2 = cos_ref[...]
    sin2 = sin_ref[...]
    tok_seq = lax.broadcasted_iota(I32, (RET_HEAD_DIM, rows), 1) // seq_len
    for h in range(RET_HEADS):
        q = _rope(_head(q_all, h), cos2, sin2)
        k = _rope(_head(k_all, h), cos2, sin2) * scale
        vb = _head(v_all, h).astype(BF16)
        s_old = s0_ref[:, h]
        scores = _dot_nt(q.astype(BF16), k.astype(BF16)) * dmat_ref[h]
        qd = (q * _head(qdec_ref[...], h)).astype(BF16).reshape(ns, seq_len, RET_HEAD_DIM)
        o_state = jnp.einsum('bid,bde->bie', qd, s_old.astype(BF16), preferred_element_type=F32)
        o = _dot(scores.astype(BF16), vb) + o_state.reshape(rows, RET_HEAD_DIM)
        kdt = (k * _head(kdec_ref[...], h)).T
        lhs = jnp.concatenate(
            [jnp.where(tok_seq == b, kdt, 0.0).astype(BF16) for b in range(ns)], axis=0)
        upd = _dot(lhs, vb).reshape(ns, RET_HEAD_DIM, RET_HEAD_DIM)
        sfin_ref[:, h] = s_old * cdec_ref[h] + upd
        on = _group_norm(o) * _head(gn_ref[...], h)
        ret_scr[:, h * RET_HEAD_DIM:(h + 1) * RET_HEAD_DIM] = _silu(_head(g_all, h)) * on

    mix_in = jnp.concatenate([pool_out, ret_scr[...]], axis=-1).astype(BF16)
    y_ref[...] = (x + _dot(mix_in, wout_ref[...])).reshape(ns, seq_len, D_MODEL)


def _router_logits(hn, wr_cat_ref, wr_hi_ref, br_ref, precise_from=0):
    parts = []
    if precise_from:
        parts.append(_dot(hn[:precise_from].astype(BF16), wr_hi_ref[...]))
    if precise_from < hn.shape[0]:
        h_hi, h_lo = _split(hn[precise_from:])
        part = _dot(h_hi, wr_cat_ref[...])
        parts.append(part[:, :LANES] + part[:, LANES:] + _dot(h_lo, wr_hi_ref[...]))
    return jnp.concatenate(parts, axis=0) + br_ref[...]


def _select(group_lg, expert_lg, gidx, eidx, axis):
    red = dict(axis=axis, keepdims=True)
    neg = jnp.float32(-jnp.inf)
    gmax = jnp.max(group_lg, **red)
    g_sel = jnp.min(jnp.where(group_lg == gmax, gidx, N_EXPERT_GROUPS), **red)
    p_sel = 1.0 / jnp.sum(jnp.exp(group_lg - gmax), **red)
    emask = (eidx >> GROUP_SHIFT) == g_sel
    v1 = jnp.max(jnp.where(emask, expert_lg, neg), **red)
    i1 = jnp.min(jnp.where(emask & (expert_lg == v1), eidx, N_EXPERTS), **red)
    emask2 = emask & (eidx != i1)
    v2 = jnp.max(jnp.where(emask2, expert_lg, neg), **red)
    i2 = jnp.min(jnp.where(emask2 & (expert_lg == v2), eidx, N_EXPERTS), **red)
    e2 = jnp.exp(v2 - v1)
    return g_sel, i1, i2, p_sel / (1.0 + e2), p_sel * e2 / (1.0 + e2)


def _route_tile(x, precise_from, ng_ref, wr_cat_ref, wr_hi_ref, br_ref, ha_ref, hbg_ref, idx_ref,
                count_ref, carry_scr):
    rows = x.shape[0]
    hn = _rmsnorm(x, ng_ref[...])
    ha_ref[...], hbg_ref[:, :PACK_W] = _pack_row(hn)
    lgt = _router_logits(hn, wr_cat_ref, wr_hi_ref, br_ref, precise_from).T
    neg = jnp.float32(-jnp.inf)
    gidx = lax.broadcasted_iota(I32, (SUBLANES, rows), 0)
    eidx = lax.broadcasted_iota(I32, (N_EXPERTS, rows), 0)
    group_lg = jnp.where(gidx < N_EXPERT_GROUPS, lgt[0:SUBLANES], neg)
    expert_lg = lgt[EXPERT_LANE0:EXPERT_LANE0 + N_EXPERTS]
    g_sel, i1, i2, w1, w2 = _select(group_lg, expert_lg, gidx, eidx, 0)
    lo = jnp.minimum(i1, i2) - g_sel * EXPERTS_PER_GROUP
    hi = jnp.maximum(i1, i2) - g_sel * EXPERTS_PER_GROUP
    pair = ((lo * (2 * EXPERTS_PER_GROUP - 1 - lo)) >> 1) + (hi - lo - 1)
    cls = g_sel * PAIRS_PER_GROUP + pair
    w_lo = jnp.where(i1 < i2, w1, w2)
    w_hi = jnp.where(i1 < i2, w2, w1)

    crow = lax.broadcasted_iota(I32, (CLASS_ROWS, rows), 0)
    onehot = jnp.where(crow == cls, 1.0, 0.0)
    n_blk = rows // COUNT_BLOCK
    blocks = [onehot[:, j * COUNT_BLOCK:(j + 1) * COUNT_BLOCK] for j in range(n_blk)]
    r = lax.broadcasted_iota(I32, (COUNT_BLOCK, COUNT_BLOCK), 0)
    c = lax.broadcasted_iota(I32, (COUNT_BLOCK, COUNT_BLOCK), 1)
    upper = jnp.where(r < c, 1.0, 0.0).astype(BF16)
    within = _dot(jnp.concatenate(blocks, axis=0).astype(BF16), upper)
    carry = carry_scr[:, 0:1]
    ranks = []
    for j in range(n_blk):
        before = within[j * CLASS_ROWS:(j + 1) * CLASS_ROWS] + carry
        ranks.append(jnp.sum(blocks[j] * before, axis=0, keepdims=True))
        carry = carry + jnp.sum(blocks[j], axis=1, keepdims=True)
    rank = jnp.concatenate(ranks, axis=1)
    carry_scr[...] = jnp.broadcast_to(carry, carry_scr.shape)
    count_ref[...] = carry_scr[...]

    row8 = lax.broadcasted_iota(I32, (SUBLANES, rows), 0)
    idx_ref[...] = jnp.where(row8 == 0, cls, jnp.where(row8 == 1, rank.astype(I32), 0))
    rowl = lax.broadcasted_iota(I32, (LANES, rows), 0)
    gate_rec = jnp.where(rowl == GATE_LO, w_lo, jnp.where(rowl == GATE_HI, w_hi, 0.0)).T
    hbg_ref[:, PACK_W:] = pltpu.bitcast(gate_rec, I32)


def _moe_pair_kernel(ea_ref, eb_ref, used_ref, ha_ref, hbg_ref, wgu_a_ref, wd_a_ref,
                     wgu_b_ref, wd_b_ref, after_ref, ya_ref, yb_ref):
    del ea_ref, eb_ref, after_ref

    @pl.when(pl.program_id(0) < used_ref[0])
    def _():
        h = _unpack_row(ha_ref[...], hbg_ref[:, :PACK_W]).astype(BF16)
        gates = pltpu.bitcast(hbg_ref[:, PACK_W:], F32)
        y = None
        for wgu_ref, wd_ref, lane in ((wgu_a_ref, wd_a_ref, GATE_LO), (wgu_b_ref, wd_b_ref, GATE_HI)):
            gu = _dot(h, wgu_ref[0])
            act = _silu(gu[:, :D_EXPERT]) * gu[:, D_EXPERT:]
            part = _dot((act * gates[:, lane:lane + 1]).astype(BF16), wd_ref[0])
            y = part if y is None else y + part
        ya_ref[...], yb_ref[...] = _pack_row(y)


def _final_norm_kernel(x_ref, ya_ref, yb_ref, g_ref, o_ref):
    o_ref[...] = _rmsnorm(x_ref[...] + _unpack_row(ya_ref[...], yb_ref[...]), g_ref[...])


def _moe_dense_kernel(*refs, final_norm, e0, first, last):
    x_ref, ng_ref, wr_cat_ref, wr_hi_ref, br_ref, wg_ref, wu_ref, wdn_ref, nf_ref = refs[:9]
    acc_in_ref = None if first else refs[10]
    y_ref, wgu_ref, wd_ref, h_scr, gate_scr, acc_scr = refs[10 + 3 * (not first):]
    e = pl.program_id(1)

    @pl.when(e == 0)
    def _():
        hn = _rmsnorm(x_ref[...], ng_ref[...])
        h_scr[...] = hn.astype(BF16)
        lg = _router_logits(hn, wr_cat_ref, wr_hi_ref, br_ref)
        lane = lax.broadcasted_iota(I32, lg.shape, 1)
        neg = jnp.float32(-jnp.inf)
        group_lg = jnp.where(lane < N_EXPERT_GROUPS, lg, neg)
        is_expert = (lane >= EXPERT_LANE0) & (lane < EXPERT_LANE0 + N_EXPERTS)
        _, i1, i2, w1, w2 = _select(group_lg, jnp.where(is_expert, lg, neg), lane,
                                    jnp.where(is_expert, lane - EXPERT_LANE0, N_EXPERTS), 1)
        gate_scr[...] = (jnp.where(lane == i1 + EXPERT_LANE0, w1, 0.0)
                         + jnp.where(lane == i2 + EXPERT_LANE0, w2, 0.0))
        acc_scr[...] = jnp.zeros_like(acc_scr) if first else acc_in_ref[...]

    lane = lax.broadcasted_iota(I32, gate_scr.shape, 1)
    gate = jnp.sum(jnp.where(lane == EXPERT_LANE0 + e0 + e, gate_scr[...], 0.0), axis=-1, keepdims=True)
    wgu_ref[0, :, :D_EXPERT] = wg_ref[0].astype(BF16)
    wgu_ref[0, :, D_EXPERT:] = wu_ref[0].astype(BF16)
    wd_ref[0] = wdn_ref[0].astype(BF16)
    gu = _dot(h_scr[...], wgu_ref[0])
    act = _silu(gu[:, :D_EXPERT]) * gu[:, D_EXPERT:]
    acc_scr[...] += _dot((act * gate).astype(BF16), wd_ref[0])

    @pl.when(e == pl.num_programs(1) - 1)
    def _():
        if not last:
            y_ref[...] = acc_scr[...]
            return
        y = x_ref[...] + acc_scr[...]
        if final_norm:
            y = _rmsnorm(y, nf_ref[...])
        y_ref[...] = y


def _const_spec(shape):
    return pl.BlockSpec(shape, lambda *_: (0,) * len(shape))


_ANY_SPEC = pl.BlockSpec(memory_space=pl.ANY)


def _rope_tables(pos):
    half = RET_HEAD_DIM // 2
    inv = np.float32(ROPE_BASE) ** (-np.arange(half, dtype=np.float32) / np.float32(half))
    ang = pos.astype(np.float32)[:, None] * inv[None, :]
    cos, sin = np.cos(ang), np.sin(ang)
    return np.concatenate([cos, cos], -1), np.concatenate([-sin, sin], -1)


def _decay_consts(c, reps):
    f = np.float32
    log_g = np.log(f(1.0) - f(2.0) ** (f(-5.0) - np.arange(RET_HEADS, dtype=f)))
    i = np.arange(c, dtype=f)
    diff = i[:, None] - i[None, :]
    dmat = np.where(diff[None] >= 0, np.exp(np.maximum(diff, f(0))[None] * log_g[:, None, None]), f(0))
    qdec = np.exp((i + f(1))[None, :] * log_g[:, None])
    kdec = np.exp((f(c) - f(1) - i)[None, :] * log_g[:, None])
    cdec = np.exp(f(c) * log_g)
    dmat = np.einsum('ab,hij->haibj', np.eye(reps, dtype=f), dmat).reshape(RET_HEADS, reps * c, reps * c)
    lanes = lambda t: np.repeat(np.tile(t, (1, reps)).T, RET_HEAD_DIM, axis=1)
    cdec = np.broadcast_to(cdec[:, None, None], (RET_HEADS, 1, RET_HEAD_DIM))
    return tuple(np.ascontiguousarray(a, dtype=f) for a in (dmat, lanes(qdec), lanes(kdec), cdec))


def _split_weight(w):
    hi = lax.bitcast_convert_type(_bf16_round_bits(lax.bitcast_convert_type(w, jnp.uint32)), F32)
    return hi.astype(BF16), (w - hi).astype(BF16)


_W_IN_BLOCK = (None, D_MODEL, IN_WIDTH)
_W_POOL_BLOCK = (None, len(POOL_WINDOWS), POOL_GROUP_DIM, POOL_GROUP_DIM)
_W_OUT_BLOCK = (None, D_MODEL, D_MODEL)
_DECAY_SPECS = [
    _const_spec((RET_HEADS, CHUNK, CHUNK)), _const_spec((CHUNK, RET_WIDTH)),
    _const_spec((CHUNK, RET_WIDTH)), _const_spec((RET_HEADS, 1, RET_HEAD_DIM)),
]


def _layer_spec(block, layer, resident=False):
    index_map = lambda *_: (layer,) + (0,) * (len(block) - 1)
    if resident:
        return pl.BlockSpec(block, index_map, pipeline_mode=pl.Buffered(1))
    return pl.BlockSpec(block, index_map)


def _mixer_weight_specs(layer, split):
    n = 2 if split else 1
    row = lambda width: _layer_spec((None, 1, width), layer)
    return [row(D_MODEL), *[_layer_spec(_W_IN_BLOCK, layer, True)] * n,
            *[_layer_spec(_W_POOL_BLOCK, layer)] * n, row(POOL_WIDTH), row(RET_WIDTH),
            *[_layer_spec(_W_OUT_BLOCK, layer, True)] * n]


def _chain_out_spec(block, depth, layer, has_prev):
    zeros = (0,) * (len(block) - 1)
    if has_prev:
        return pl.BlockSpec((None,) + block, lambda i, *_: (layer, i) + zeros)
    return pl.BlockSpec((depth,) + block, lambda i, *_: (0, i) + zeros)


def _mixer_prompt(x, moe_y, weights, route_w, prev_out, layer, rows, precise_tail, after):
    b, l, _ = x.shape
    t = b * l
    depth = weights[0].shape[0]
    assert l % rows == 0 and rows % CHUNK == 0
    cos2, sin2 = _rope_tables(np.arange(l))
    decay = _decay_consts(RET_CHUNK, 1)
    steps = l // rows
    tok = lambda i, c: (i, c, 0)
    if moe_y is None:
        ya = yb = jnp.zeros((1, rows, PACK_W), I32)
        y_spec = _const_spec((1, rows, PACK_W))
    else:
        ya, yb = (t.reshape(b, l, PACK_W) for t in moe_y)
        y_spec = pl.BlockSpec((1, rows, PACK_W), tok)
    has_prev = prev_out is not None
    n_in = 23
    flat = lambda i, c: (i * steps + c, 0)
    y, tails, states, *routed = pl.pallas_call(
        functools.partial(_mixer_prompt_kernel, rows=rows, steps=steps, moe_in=moe_y is not None,
                          precise_tail=precise_tail, layer=layer, has_prev=has_prev),
        grid=(b, steps),
        in_specs=[pl.BlockSpec((1, rows, D_MODEL), tok), y_spec, y_spec,
                  pl.BlockSpec((rows, RET_HEAD_DIM), lambda i, c: (c, 0)),
                  pl.BlockSpec((rows, RET_HEAD_DIM), lambda i, c: (c, 0)),
                  *_mixer_weight_specs(layer, True), *_DECAY_SPECS,
                  _layer_spec((None, 1, D_MODEL), layer), *_router_specs(layer), _ANY_SPEC,
                  *([_ANY_SPEC, _ANY_SPEC] if has_prev else [])],
        out_specs=[pl.BlockSpec((1, rows, D_MODEL), tok),
                   _chain_out_spec((1, POOL_BUF, POOL_WIDTH), depth, layer, has_prev),
                   _chain_out_spec((1, RET_HEADS, RET_HEAD_DIM, RET_HEAD_DIM), depth, layer, has_prev),
                   pl.BlockSpec((rows, PACK_W), flat), pl.BlockSpec((rows, PACK_W + LANES), flat),
                   pl.BlockSpec((SUBLANES, rows), lambda i, c: (0, i * steps + c)),
                   _const_spec((CLASS_ROWS, LANES))],
        out_shape=[jax.ShapeDtypeStruct(x.shape, F32),
                   jax.ShapeDtypeStruct((depth, b, POOL_BUF, POOL_WIDTH), F32),
                   jax.ShapeDtypeStruct((depth, b, RET_HEADS, RET_HEAD_DIM, RET_HEAD_DIM), F32),
                   jax.ShapeDtypeStruct((t, PACK_W), I32),
                   jax.ShapeDtypeStruct((t, PACK_W + LANES), I32), jax.ShapeDtypeStruct((SUBLANES, t), I32),
                   jax.ShapeDtypeStruct((CLASS_ROWS, LANES), F32)],
        input_output_aliases={n_in: 1, n_in + 1: 2} if has_prev else {},
        scratch_shapes=[pltpu.VMEM((1, POOL_HIST + rows, POOL_WIDTH), F32),
                        pltpu.VMEM((RET_HEADS, RET_HEAD_DIM, RET_HEAD_DIM), F32),
                        pltpu.VMEM((rows, RET_WIDTH), F32),
                        pltpu.VMEM((rows, IN_WIDTH), F32),
                        pltpu.VMEM((CLASS_ROWS, LANES), F32)],
        compiler_params=pltpu.CompilerParams(
            dimension_semantics=("arbitrary", "arbitrary"), vmem_limit_bytes=VMEM_LIMIT),
        name="mixer_prompt",
    )(x, ya, yb, cos2, sin2, *weights, *decay, *route_w, after, *(prev_out if has_prev else ()))
    return y, (tails, states), tuple(routed)


def _mixer_sample(x, pool_prev, s0, weights, prev_out, layer, after):
    b, l, _ = x.shape
    depth = s0.shape[0]
    ns = SAMPLE_SEQS
    assert ns * l == CHUNK and b % ns == 0
    cos2, sin2 = _rope_tables(PAST_LEN + np.arange(l))
    cos2, sin2 = np.tile(cos2, (ns, 1)), np.tile(sin2, (ns, 1))
    decay = _decay_consts(l, ns)
    seq3 = lambda i: (i, 0, 0)
    state_block = (ns, RET_HEADS, RET_HEAD_DIM, RET_HEAD_DIM)
    tail_block = (POOL_BUF, ns, POOL_WIDTH)
    if prev_out is None:
        tail_out = pl.BlockSpec((depth,) + tail_block, lambda i: (0, 0, i, 0))
    else:
        tail_out = pl.BlockSpec((None,) + tail_block, lambda i: (layer, 0, i, 0))
    has_prev = prev_out is not None
    n_in = 16
    y, tails, states = pl.pallas_call(
        functools.partial(_mixer_sample_kernel, seq_len=l, layer=layer, has_prev=has_prev),
        grid=(b // ns,),
        in_specs=[pl.BlockSpec((ns, l, D_MODEL), seq3),
                  pl.BlockSpec((None,) + tail_block, lambda i: (layer, 0, i, 0)),
                  pl.BlockSpec((None,) + state_block, lambda i: (layer, i, 0, 0, 0)),
                  _const_spec((CHUNK, RET_HEAD_DIM)), _const_spec((CHUNK, RET_HEAD_DIM)),
                  *_mixer_weight_specs(layer, False), *_DECAY_SPECS, _ANY_SPEC,
                  *([_ANY_SPEC, _ANY_SPEC] if has_prev else [])],
        out_specs=[pl.BlockSpec((ns, l, D_MODEL), seq3), tail_out,
                   _chain_out_spec(state_block, depth, layer, has_prev)],
        out_shape=[jax.ShapeDtypeStruct(x.shape, F32),
                   jax.ShapeDtypeStruct((depth, POOL_BUF, b, POOL_WIDTH), F32),
                   jax.ShapeDtypeStruct(s0.shape, F32)],
        input_output_aliases={n_in: 1, n_in + 1: 2} if has_prev else {},
        scratch_shapes=[pltpu.VMEM((ns, POOL_HIST + l, POOL_WIDTH), F32),
                        pltpu.VMEM((CHUNK, RET_WIDTH), F32)],
        compiler_params=pltpu.CompilerParams(
            dimension_semantics=("arbitrary",), vmem_limit_bytes=VMEM_LIMIT),
        name="mixer_sample",
    )(x, pool_prev, s0, cos2, sin2, *weights, *decay, after, *(prev_out if has_prev else ()))
    return y, (tails, states)


def _router_weights(w_rg, b_rg, w_re, b_re):
    depth = w_rg.shape[0]
    gap = EXPERT_LANE0 - N_EXPERT_GROUPS
    rest = LANES - EXPERT_LANE0 - N_EXPERTS
    wr = jnp.concatenate([w_rg, jnp.zeros((depth, D_MODEL, gap), F32), w_re,
                          jnp.zeros((depth, D_MODEL, rest), F32)], axis=-1)
    br = jnp.concatenate([b_rg, jnp.zeros((depth, gap), F32), b_re, jnp.zeros((depth, rest), F32)],
                         axis=-1).reshape(depth, 1, LANES)
    wr_hi, wr_lo = _split_weight(wr)
    return jnp.concatenate([wr_hi, wr_lo], axis=-1), wr_hi, br


def _router_specs(layer):
    return [_layer_spec((None, D_MODEL, 2 * LANES), layer), _layer_spec((None, D_MODEL, LANES), layer),
            _layer_spec((None, 1, LANES), layer)]


def _sc_mesh():
    return plsc.VectorSubcoreMesh(core_axis_name="core", subcore_axis_name="subcore",
                                  num_cores=SC_CORES, num_subcores=SC_SUBCORES)


def _sc_params():
    params = pltpu.CompilerParams()
    if "needs_layout_passes" in pltpu.CompilerParams.__dataclass_fields__:
        params = dataclasses.replace(params, needs_layout_passes=False)
    return params


def _sc_gather(tables, idx, after):
    n = idx.shape[0]
    assert n % SC_WINDOW == 0
    nt = len(tables)

    def body(*refs):
        i_hbm = refs[nt]
        for t_hbm, o_hbm in zip(refs[:nt], refs[nt + 2:]):
            def gather_window(i_vmem, o_vmem, t_hbm=t_hbm):
                pltpu.sync_copy(t_hbm.at[i_vmem.at[0]], o_vmem)

            pltpu.emit_pipeline(
                gather_window, grid=(n // SC_WINDOW,),
                in_specs=[pl.BlockSpec((1, SC_WINDOW), lambda i: (0, i))],
                out_specs=[pl.BlockSpec((SC_WINDOW, t_hbm.shape[1]), lambda i: (i, 0))],
                core_axis_name=("core", "subcore"),
                dimension_semantics=(pltpu.PARALLEL,),
            )(i_hbm, o_hbm)

    out_type = tuple(jax.ShapeDtypeStruct((n, t.shape[1]), t.dtype) for t in tables)
    return pl.kernel(body, out_type=out_type, mesh=_sc_mesh(), name="sc_gather")(
        *tables, idx.reshape(1, n), after)


def _sc_slots(cls, rank, starts, n_slots):
    t = cls.shape[0]
    workers = SC_CORES * SC_SUBCORES
    slot_per, tok_per = n_slots // workers, t // workers
    assert n_slots % (workers * SC_LANES) == 0 and t % (workers * SC_LANES) == 0 and t & (t - 1) == 0
    assert n_slots % (SC_LANES * SC_UNROLL) == 0 and t % (SC_LANES * SC_UNROLL) == 0

    def body(cls_hbm, rank_hbm, starts_hbm, pos_hbm, slot_hbm, cls_v, rank_v, starts_v, pos_v, slot_v):
        wid = lax.axis_index("subcore") * SC_CORES + lax.axis_index("core")
        pltpu.sync_copy(cls_hbm, cls_v)
        pltpu.sync_copy(rank_hbm, rank_v)
        pltpu.sync_copy(starts_hbm, starts_v)

        lane = lax.iota(I32, SC_LANES)
        span = SC_LANES * SC_UNROLL

        @pl.loop(0, n_slots, step=span)
        def _(i):
            for u in range(SC_UNROLL):
                j = i + u * SC_LANES
                slot_v[pl.ds(j, SC_LANES)] = (lane + j) & (t - 1)

        @pl.loop(0, t, step=span)
        def _(i):
            for u in range(SC_UNROLL):
                j = i + u * SC_LANES
                at = pl.ds(j, SC_LANES)
                pos = plsc.load_gather(starts_v, [cls_v[at]]) + rank_v[at]
                pos_v[at] = pos
                plsc.store_scatter(slot_v, [pos], lane + j)

        tok_off = pl.multiple_of(wid * tok_per, SC_LANES)
        pltpu.sync_copy(pos_v.at[pl.ds(tok_off, tok_per)], pos_hbm.at[pl.ds(tok_off, tok_per)])
        slot_off = pl.multiple_of(wid * slot_per, SC_LANES)
        pltpu.sync_copy(slot_v.at[pl.ds(slot_off, slot_per)], slot_hbm.at[pl.ds(slot_off, slot_per)])

    return pl.kernel(
        body, mesh=_sc_mesh(), compiler_params=_sc_params(), name="sc_slots",
        out_type=(jax.ShapeDtypeStruct((t,), I32), jax.ShapeDtypeStruct((n_slots,), I32)),
        scratch_types=[pltpu.VMEM((t,), I32), pltpu.VMEM((t,), I32), pltpu.VMEM((CLASS_ROWS,), I32),
                       pltpu.VMEM((t,), I32), pltpu.VMEM((n_slots,), I32)],
    )(cls, rank, starts)


def _moe_dispatch(routed, after):
    ha, hbg, idx, counts = routed
    t = ha.shape[0]
    n_slots = t + N_CLASSES * PAIR_TILE
    n_tiles = n_slots // PAIR_TILE

    cnt = counts[:, 0].astype(I32)
    padded = (cnt + PAIR_TILE - 1) // PAIR_TILE * PAIR_TILE
    ends = jnp.cumsum(padded)
    starts = ends - padded
    tile_cls = jnp.minimum(
        jnp.sum(ends[None, :N_CLASSES] <= (jnp.arange(n_tiles, dtype=I32) * PAIR_TILE)[:, None], axis=1),
        N_CLASSES - 1).astype(I32)
    first = (tile_cls // PAIRS_PER_GROUP) * EXPERTS_PER_GROUP
    tile_ea = first + jnp.asarray(PAIR_LO, I32)[tile_cls % PAIRS_PER_GROUP]
    tile_eb = first + jnp.asarray(PAIR_HI, I32)[tile_cls % PAIRS_PER_GROUP]
    used = (ends[N_CLASSES - 1:N_CLASSES] // PAIR_TILE).astype(I32)

    pos, slot_tok = _sc_slots(idx[0], idx[1], starts, n_slots)
    hsa, hsbg = _sc_gather((ha, hbg), slot_tok, after=after)
    return (tile_ea, tile_eb, used, hsa, hsbg), pos, slot_tok


def _moe_pair(dispatched, wgu, wd, after):
    tile_ea, tile_eb, used, hsa, hsbg = dispatched
    n_slots = hsa.shape[0]
    n_tiles = n_slots // PAIR_TILE
    row = lambda i, ea, eb, nu: (jnp.minimum(i, nu[0] - 1), 0)
    w_spec = lambda shape, which: pl.BlockSpec(
        (1,) + shape, lambda i, ea, eb, nu: ((ea, eb)[which][i], 0, 0))
    gu_shape, d_shape = (D_MODEL, 2 * D_EXPERT), (D_EXPERT, D_MODEL)
    ysa, ysb = pl.pallas_call(
        _moe_pair_kernel,
        grid_spec=pltpu.PrefetchScalarGridSpec(
            num_scalar_prefetch=3, grid=(n_tiles,),
            in_specs=[pl.BlockSpec((PAIR_TILE, PACK_W), row),
                      pl.BlockSpec((PAIR_TILE, PACK_W + LANES), row),
                      w_spec(gu_shape, 0), w_spec(d_shape, 0), w_spec(gu_shape, 1), w_spec(d_shape, 1),
                      _ANY_SPEC],
            out_specs=[pl.BlockSpec((PAIR_TILE, PACK_W), row), pl.BlockSpec((PAIR_TILE, PACK_W), row)]),
        out_shape=[jax.ShapeDtypeStruct((n_slots, PACK_W), I32),
                   jax.ShapeDtypeStruct((n_slots, PACK_W), I32)],
        compiler_params=pltpu.CompilerParams(
            dimension_semantics=("arbitrary",), vmem_limit_bytes=VMEM_LIMIT),
        name="moe_pair",
    )(tile_ea, tile_eb, used, hsa, hsbg, wgu, wd, wgu, wd, after)
    return ysa, ysb


def _final_norm(x, moe_y, g, rows):
    t = x.shape[0]
    rows = min(rows, t)
    assert t % rows == 0
    tok = lambda i: (i, 0)
    return pl.pallas_call(
        _final_norm_kernel,
        grid=(t // rows,),
        in_specs=[pl.BlockSpec((rows, D_MODEL), tok), pl.BlockSpec((rows, PACK_W), tok),
                  pl.BlockSpec((rows, PACK_W), tok), _const_spec((1, D_MODEL))],
        out_specs=pl.BlockSpec((rows, D_MODEL), tok),
        out_shape=jax.ShapeDtypeStruct(x.shape, F32),
        compiler_params=pltpu.CompilerParams(
            dimension_semantics=("arbitrary",), vmem_limit_bytes=VMEM_LIMIT),
        name="final_norm",
    )(x, *moe_y, g.reshape(1, D_MODEL))


def _moe_dense(x, norm_g, router, w_gate, w_up, w_down, norm_final, layer, final_norm, rows, after,
               experts=(0, N_EXPERTS), chain=None):
    t = x.shape[0]
    assert t == rows
    e0, e1 = experts
    first, last = chain is None, e1 == N_EXPERTS
    assert first == (e0 == 0)
    tok = lambda i, e: (i, 0)
    w_spec = lambda shape: pl.BlockSpec((None, 1) + shape, lambda i, e: (layer, e0 + e, 0, 0))
    w_out = lambda shape: pl.BlockSpec((1,) + shape, lambda i, e: (e0 + e, 0, 0))
    n_in = 10
    return pl.pallas_call(
        functools.partial(_moe_dense_kernel, final_norm=final_norm, e0=e0, first=first, last=last),
        grid=(t // rows, e1 - e0),
        in_specs=[pl.BlockSpec((rows, D_MODEL), tok), _layer_spec((None, 1, D_MODEL), layer),
                  *_router_specs(layer),
                  w_spec((D_MODEL, D_EXPERT)), w_spec((D_MODEL, D_EXPERT)), w_spec((D_EXPERT, D_MODEL)),
                  _const_spec((1, D_MODEL)), _ANY_SPEC,
                  *([] if first else [pl.BlockSpec((rows, D_MODEL), tok), _ANY_SPEC, _ANY_SPEC])],
        out_specs=[pl.BlockSpec((rows, D_MODEL), tok), w_out((D_MODEL, 2 * D_EXPERT)),
                   w_out((D_EXPERT, D_MODEL))],
        out_shape=[jax.ShapeDtypeStruct(x.shape, F32),
                   jax.ShapeDtypeStruct((N_EXPERTS, D_MODEL, 2 * D_EXPERT), BF16),
                   jax.ShapeDtypeStruct((N_EXPERTS, D_EXPERT, D_MODEL), BF16)],
        input_output_aliases={} if first else {n_in + 1: 1, n_in + 2: 2},
        scratch_shapes=[pltpu.VMEM((rows, D_MODEL), BF16),
                        pltpu.VMEM((rows, LANES), F32),
                        pltpu.VMEM((rows, D_MODEL), F32)],
        compiler_params=pltpu.CompilerParams(
            dimension_semantics=("arbitrary", "arbitrary"), vmem_limit_bytes=VMEM_LIMIT),
        name="moe_dense",
    )(x, norm_g, *router, w_gate, w_up, w_down, norm_final.reshape(1, D_MODEL), after,
      *(() if first else chain))


def kernel(x_prompt, x_sample, cache_pool, state_ret, norm_mix, w_in, w_pool, pool_scale, ret_gn, w_out, norm_ffn, w_router_group, b_router_group, w_router_expert, b_router_expert, w_gate, w_up, w_down, norm_final):
    depth = norm_mix.shape[0]
    row = lambda a: a.reshape(depth, 1, a.shape[-1])
    mix_split = (row(norm_mix), *_split_weight(w_in), *_split_weight(w_pool), row(pool_scale),
                 row(ret_gn), *_split_weight(w_out))
    mix_hi = tuple(mix_split[i] for i in (0, 1, 3, 5, 6, 7))
    router = _router_weights(w_router_group, b_router_group, w_router_expert, b_router_expert)
    norm_ffn = row(norm_ffn)
    pool_prev = jnp.swapaxes(cache_pool, 1, 2)

    yp, ys = x_prompt, x_sample
    moe_p = None
    out_p = out_s = None
    ys_mix = None
    for l in range(depth):
        yp, out_p, routed = _mixer_prompt(
            yp, moe_p, mix_split, (norm_ffn, *router), out_p, l, rows=512,
            precise_tail=PRECISE_TAIL_STEPS if l < depth - 1 else 0,
            after=ys if ys_mix is None else ys_mix)
        if ys_mix is None:
            ys_mix, out_s = _mixer_sample(ys, pool_prev, state_ret, mix_hi, out_s, l, after=routed[-1])
        dense = functools.partial(_moe_dense, ys_mix.reshape(-1, D_MODEL), norm_ffn, router, w_gate, w_up,
                                  w_down, norm_final, l, l == depth - 1,
                                  rows=ys_mix.shape[0] * ys_mix.shape[1])
        head = dense(after=routed[-1], experts=(0, DENSE_HEAD)) if l else None
        dispatched, pos, slot_tok = _moe_dispatch(routed, after=ys_mix if l == 0 else head[0])
        ys, wgu, wd = dense(after=slot_tok, experts=(DENSE_HEAD if l else 0, N_EXPERTS), chain=head)
        ys = ys.reshape(x_sample.shape)
        sorted_y = _moe_pair(dispatched, wgu, wd, after=ys)
        moe_p = _sc_gather(sorted_y, pos, after=routed[-1])
        if l + 1 < depth:
            ys_mix, out_s = _mixer_sample(ys, pool_prev, state_ret, mix_hi, out_s, l + 1,
                                          after=sorted_y[0])
    yp = _final_norm(yp.reshape(-1, D_MODEL), moe_p, norm_final, rows=2048).reshape(yp.shape)
    return (yp, ys, *out_p, jnp.swapaxes(out_s[0], 1, 2), out_s[1])
```

```python
import dataclasses
import functools
import itertools

import jax
import jax.numpy as jnp
import numpy as np
from jax import lax
from jax.experimental import pallas as pl
from jax.experimental.pallas import tpu as pltpu
from jax.experimental.pallas import tpu_sc as plsc

F32 = jnp.float32
BF16 = jnp.bfloat16
I32 = jnp.int32

D_MODEL = 1024
POOL_WIDTH = 512
POOL_WINDOWS = (2, 4, 8, 16)
POOL_GROUP_DIM = 128
POOL_BUF = 15
POOL_HIST = 16
RET_WIDTH = 512
RET_HEADS = 4
RET_HEAD_DIM = 128
RET_CHUNK = 128
ROPE_BASE = 10000.0
IN_WIDTH = POOL_WIDTH + 4 * RET_WIDTH
N_EXPERT_GROUPS = 4
EXPERTS_PER_GROUP = 4
N_EXPERTS = N_EXPERT_GROUPS * EXPERTS_PER_GROUP
D_EXPERT = 256
RMS_EPS = 1e-6
GN_EPS = 1e-5
PAST_LEN = 16384

LANES = 128
SUBLANES = 8
EXPERT_LANE0 = 8
GROUP_SHIFT = EXPERTS_PER_GROUP.bit_length() - 1
PAIRS = tuple(itertools.combinations(range(EXPERTS_PER_GROUP), 2))
PAIRS_PER_GROUP = len(PAIRS)
PAIR_LO, PAIR_HI = zip(*PAIRS)
N_CLASSES = N_EXPERT_GROUPS * PAIRS_PER_GROUP
CLASS_ROWS = 32
GATE_LO, GATE_HI = 0, 1
COUNT_BLOCK = 256
PAIR_TILE = 256
PACK_W = D_MODEL // 4
SC_CORES, SC_SUBCORES, SC_LANES = 2, 16, 16
SC_WINDOW = 128
SC_UNROLL = 8
PRECISE_TAIL_STEPS = 1
CHUNK = 128
SAMPLE_SEQS = 16
DENSE_HEAD = 4
FINAL_PARTS = 2
VMEM_LIMIT = 56 * 1024 * 1024


def _dot(a, b):
    return jnp.dot(a, b, preferred_element_type=F32)


def _dot_nt(a, b):
    return lax.dot_general(a, b, (((1,), (1,)), ((), ())), preferred_element_type=F32)


def _bf16_round_bits(u):
    return (u + jnp.uint32(0x7FFF) + ((u >> 16) & jnp.uint32(1))) & jnp.uint32(0xFFFF0000)


def _split(a):
    hi = pltpu.bitcast(_bf16_round_bits(pltpu.bitcast(a, jnp.uint32)), F32)
    return hi.astype(BF16), (a - hi).astype(BF16)


def _mm(a, b, precise, nt=False):
    dot = _dot_nt if nt else _dot
    if precise:
        b_hi, b_lo = b if isinstance(b, tuple) else _split(b)
        a_hi, a_lo = _split(a)
        return dot(a_hi, b_hi) + dot(a_lo, b_hi) + dot(a_hi, b_lo)
    return dot(a.astype(BF16), b[0] if isinstance(b, tuple) else b.astype(BF16))


def _rmsnorm(x, g):
    ms = jnp.mean(x * x, axis=-1, keepdims=True)
    return x * lax.rsqrt(ms + RMS_EPS) * g


def _pool_mix(ubuf, rows, t_first, n_prev, wpool_refs, pscale, precise=False, row0=0):
    ns = ubuf.shape[0]
    t = t_first + lax.broadcasted_iota(I32, (1, rows, POOL_GROUP_DIM), 1)
    base = POOL_HIST + row0
    outs = []
    for j, w in enumerate(POOL_WINDOWS):
        lanes = slice(j * POOL_GROUP_DIM, (j + 1) * POOL_GROUP_DIM)
        uj = ubuf[:, base:base + rows, lanes]
        acc = uj
        for i in range(1, w):
            acc = acc + ubuf[:, base - i:base - i + rows, lanes]
        cnt = jnp.minimum(w, n_prev + t + 1).astype(F32)
        d = (acc / cnt - uj).reshape(ns * rows, POOL_GROUP_DIM)
        outs.append(_mm(d, tuple(w[j] for w in wpool_refs), precise))
    return jnp.concatenate(outs, axis=-1) * pscale


def _rope(xh, cos2, sin2):
    return xh * cos2 + pltpu.roll(xh, RET_HEAD_DIM // 2, 1) * sin2


def _group_norm(o):
    mu = jnp.mean(o, axis=-1, keepdims=True)
    c = o - mu
    var = jnp.mean(c * c, axis=-1, keepdims=True)
    return c * lax.rsqrt(var + GN_EPS)


def _silu(x):
    return x * (1.0 / (1.0 + jnp.exp(-x)))


def _head(a, h):
    return a[:, h * RET_HEAD_DIM:(h + 1) * RET_HEAD_DIM]


def _qkvg(z):
    p, r = POOL_WIDTH, RET_WIDTH
    return z[:, p:p + r], z[:, p + r:p + 2 * r], z[:, p + 2 * r:p + 3 * r], z[:, p + 3 * r:p + 4 * r]


def _pack_bf16_pair(a, b):
    ua = pltpu.bitcast(a.astype(BF16).astype(F32), jnp.uint32)
    ub = pltpu.bitcast(b.astype(BF16).astype(F32), jnp.uint32)
    return pltpu.bitcast((ua >> 16) | (ub & jnp.uint32(0xFFFF0000)), I32)


def _unpack_bf16_pair(w):
    u = pltpu.bitcast(w, jnp.uint32)
    return pltpu.bitcast(u << 16, F32), pltpu.bitcast(u & jnp.uint32(0xFFFF0000), F32)


def _pack_row(y):
    q = PACK_W
    return _pack_bf16_pair(y[:, 0:q], y[:, q:2 * q]), _pack_bf16_pair(y[:, 2 * q:3 * q], y[:, 3 * q:])


def _unpack_row(wa, wb):
    return jnp.concatenate([*_unpack_bf16_pair(wa), *_unpack_bf16_pair(wb)], axis=-1)


def _zero_other_layers(ref, layer):
    for j in range(ref.shape[0]):
        if j != layer:
            ref[j] = jnp.zeros(ref.shape[1:], ref.dtype)


def _mixer_prompt_kernel(*refs, rows, steps, moe_in, precise_tail, layer, has_prev):
    (x_ref, ya_ref, yb_ref, cos_ref, sin_ref, ng_ref, win_hi_ref, win_lo_ref, wpool_hi_ref,
     wpool_lo_ref, pscale_ref, gn_ref, wout_hi_ref, wout_lo_ref, dmat_ref, qdec_ref, kdec_ref,
     cdec_ref) = refs[:18]
    route_in = refs[18:22]
    y_ref, tail_ref, sfin_ref, *route_out, ubuf, s_scr, ret_scr, z_scr, count_scr = (
        refs[23 + 2 * has_prev:])
    c = pl.program_id(1)

    @pl.when((pl.program_id(0) == 0) & (c == 0))
    def _():
        count_scr[...] = jnp.zeros_like(count_scr)
    if not has_prev:
        _zero_other_layers(tail_ref, layer)
        _zero_other_layers(sfin_ref, layer)
        tail_ref, sfin_ref = tail_ref.at[layer], sfin_ref.at[layer]

    @pl.when(c == 0)
    def _():
        ubuf[:, 0:POOL_HIST, :] = jnp.zeros((1, POOL_HIST, POOL_WIDTH), F32)
        s_scr[...] = jnp.zeros_like(s_scr)

    kv_cols = slice(POOL_WIDTH + RET_WIDTH, POOL_WIDTH + 3 * RET_WIDTH)

    def step(kv_precise, full_from):
        x = x_ref[0]
        if moe_in:
            x = x + _unpack_row(ya_ref[0], yb_ref[0])
        hn = _rmsnorm(x, ng_ref[...])
        hi, lo = _split(hn) if kv_precise else (hn.astype(BF16), None)
        z_scr[...] = _dot(hi, win_hi_ref[...])
        if kv_precise and full_from:
            z_scr[:full_from, kv_cols] += (_dot(lo[:full_from], win_hi_ref[:, kv_cols])
                                           + _dot(hi[:full_from], win_lo_ref[:, kv_cols]))
        if full_from < rows:
            z_scr[full_from:, :] += (_dot(lo[full_from:], win_hi_ref[...])
                                     + _dot(hi[full_from:], win_lo_ref[...]))

        ubuf[0, POOL_HIST:POOL_HIST + rows, :] = z_scr[:, :POOL_WIDTH]
        pool_w = (wpool_hi_ref, wpool_lo_ref)
        pool_parts = []
        if full_from:
            pool_parts.append(_pool_mix(ubuf, full_from, c * rows, 0, pool_w, pscale_ref[...]))
        if full_from < rows:
            pool_parts.append(_pool_mix(ubuf, rows - full_from, c * rows + full_from, 0, pool_w,
                                        pscale_ref[...], precise=True, row0=full_from))
        pool_out = jnp.concatenate(pool_parts, axis=0)
        tail_ref[...] = ubuf[:, rows + POOL_HIST - POOL_BUF:rows + POOL_HIST, :]
        ubuf[:, 0:POOL_HIST, :] = ubuf[:, rows:rows + POOL_HIST, :]

        scale = RET_HEAD_DIM ** -0.5
        for ci in range(rows // CHUNK):
            rs = slice(ci * CHUNK, (ci + 1) * CHUNK)
            full = ci * CHUNK >= full_from
            cos2 = cos_ref[rs, :]
            sin2 = sin_ref[rs, :]
            for h in range(RET_HEADS):
                col = lambda part: slice(POOL_WIDTH + part * RET_WIDTH + h * RET_HEAD_DIM,
                                         POOL_WIDTH + part * RET_WIDTH + (h + 1) * RET_HEAD_DIM)
                q = _rope(z_scr[rs, col(0)], cos2, sin2)
                k = _rope(z_scr[rs, col(1)], cos2, sin2) * scale
                v = z_scr[rs, col(2)]
                s_old = s_scr[h]
                scores = _mm(q, k, full, nt=True) * dmat_ref[h]
                qd = q * _head(qdec_ref[...], h)
                o = _mm(scores, v, full) + _mm(qd, s_old, full)
                kd = k * _head(kdec_ref[...], h)
                s_scr[h] = s_old * cdec_ref[h] + _mm(kd.T, v, kv_precise)
                on = _group_norm(o) * _head(gn_ref[...], h)
                ret_scr[rs, h * RET_HEAD_DIM:(h + 1) * RET_HEAD_DIM] = _silu(z_scr[rs, col(3)]) * on

        mix_in = jnp.concatenate([pool_out, ret_scr[...]], axis=-1)
        y_ref[0] = x + _dot(mix_in.astype(BF16), wout_hi_ref[...])
        if full_from < rows:
            m_hi, m_lo = _split(mix_in[full_from:])
            y_ref[0, full_from:, :] += _dot(m_lo, wout_hi_ref[...]) + _dot(m_hi, wout_lo_ref[...])
        sfin_ref[0] = s_scr[...]
        _route_tile(y_ref[0], full_from, *route_in, *route_out, count_scr)

    if precise_tail:
        pl.when(c < steps - precise_tail)(lambda: step(False, rows))
        if precise_tail > 1:
            pl.when((c >= steps - precise_tail) & (c < steps - 1))(lambda: step(True, rows))
        pl.when(c == steps - 1)(lambda: step(True, rows - CHUNK))
    else:
        step(False, rows)


def _mixer_sample_kernel(*refs, seq_len, layer, has_prev):
    (x_ref, prev_ref, s0_ref, cos_ref, sin_ref, ng_ref, win_ref, wpool_ref, pscale_ref, gn_ref,
     wout_ref, dmat_ref, qdec_ref, kdec_ref, cdec_ref) = refs[:15]
    y_ref, tail_ref, sfin_ref, ubuf, ret_scr = refs[16 + 2 * has_prev:]
    if not has_prev:
        _zero_other_layers(tail_ref, layer)
        _zero_other_layers(sfin_ref, layer)
        tail_ref, sfin_ref = tail_ref.at[layer], sfin_ref.at[layer]
    ns = SAMPLE_SEQS
    rows = ns * seq_len
    x = x_ref[...].reshape(rows, D_MODEL)
    hn = _rmsnorm(x, ng_ref[...]).astype(BF16)
    z = _dot(hn, win_ref[...])
    for j in range(POOL_BUF):
        ubuf[:, POOL_HIST - POOL_BUF + j, :] = prev_ref[j]
    ubuf[:, POOL_HIST:POOL_HIST + seq_len, :] = z[:, :POOL_WIDTH].reshape(ns, seq_len, POOL_WIDTH)
    pool_out = _pool_mix(ubuf, seq_len, 0, POOL_BUF, (wpool_ref,), pscale_ref[...])
    for j in range(POOL_BUF):
        tail_ref[j] = ubuf[:, seq_len + POOL_HIST - POOL_BUF + j, :]

    q_all, k_all, v_all, g_all = _qkvg(z)
    scale = RET_HEAD_DIM ** -0.5
    cos2 = cos_ref[...]
    sin2 = sin_ref[...]
    tok_seq = lax.broadcasted_iota(I32, (RET_HEAD_DIM, rows), 1) // seq_len
    for h in range(RET_HEADS):
        q = _rope(_head(q_all, h), cos2, sin2)
        k = _rope(_head(k_all, h), cos2, sin2) * scale
        vb = _head(v_all, h).astype(BF16)
        s_old = s0_ref[:, h]
        scores = _dot_nt(q.astype(BF16), k.astype(BF16)) * dmat_ref[h]
        qd = (q * _head(qdec_ref[...], h)).astype(BF16).reshape(ns, seq_len, RET_HEAD_DIM)
        o_state = jnp.einsum('bid,bde->bie', qd, s_old.astype(BF16), preferred_element_type=F32)
        o = _dot(scores.astype(BF16), vb) + o_state.reshape(rows, RET_HEAD_DIM)
        kdt = (k * _head(kdec_ref[...], h)).T
        lhs = jnp.concatenate(
            [jnp.where(tok_seq == b, kdt, 0.0).astype(BF16) for b in range(ns)], axis=0)
        upd = _dot(lhs, vb).reshape(ns, RET_HEAD_DIM, RET_HEAD_DIM)
        sfin_ref[:, h] = s_old * cdec_ref[h] + upd
        on = _group_norm(o) * _head(gn_ref[...], h)
        ret_scr[:, h * RET_HEAD_DIM:(h + 1) * RET_HEAD_DIM] = _silu(_head(g_all, h)) * on

    mix_in = jnp.concatenate([pool_out, ret_scr[...]], axis=-1).astype(BF16)
    y_ref[...] = (x + _dot(mix_in, wout_ref[...])).reshape(ns, seq_len, D_MODEL)


def _router_logits(hn, wr_cat_ref, wr_hi_ref, br_ref, precise_from=0):
    parts = []
    if precise_from:
        parts.append(_dot(hn[:precise_from].astype(BF16), wr_hi_ref[...]))
    if precise_from < hn.shape[0]:
        h_hi, h_lo = _split(hn[precise_from:])
        part = _dot(h_hi, wr_cat_ref[...])
        parts.append(part[:, :LANES] + part[:, LANES:] + _dot(h_lo, wr_hi_ref[...]))
    return jnp.concatenate(parts, axis=0) + br_ref[...]


def _select(group_lg, expert_lg, gidx, eidx, axis):
    red = dict(axis=axis, keepdims=True)
    neg = jnp.float32(-jnp.inf)
    gmax = jnp.max(group_lg, **red)
    g_sel = jnp.min(jnp.where(group_lg == gmax, gidx, N_EXPERT_GROUPS), **red)
    p_sel = 1.0 / jnp.sum(jnp.exp(group_lg - gmax), **red)
    emask = (eidx >> GROUP_SHIFT) == g_sel
    v1 = jnp.max(jnp.where(emask, expert_lg, neg), **red)
    i1 = jnp.min(jnp.where(emask & (expert_lg == v1), eidx, N_EXPERTS), **red)
    emask2 = emask & (eidx != i1)
    v2 = jnp.max(jnp.where(emask2, expert_lg, neg), **red)
    i2 = jnp.min(jnp.where(emask2 & (expert_lg == v2), eidx, N_EXPERTS), **red)
    e2 = jnp.exp(v2 - v1)
    return g_sel, i1, i2, p_sel / (1.0 + e2), p_sel * e2 / (1.0 + e2)


def _route_tile(x, precise_from, ng_ref, wr_cat_ref, wr_hi_ref, br_ref, ha_ref, hbg_ref, idx_ref,
                count_ref, carry_scr):
    rows = x.shape[0]
    hn = _rmsnorm(x, ng_ref[...])
    ha_ref[...], hbg_ref[:, :PACK_W] = _pack_row(hn)
    lgt = _router_logits(hn, wr_cat_ref, wr_hi_ref, br_ref, precise_from).T
    neg = jnp.float32(-jnp.inf)
    gidx = lax.broadcasted_iota(I32, (SUBLANES, rows), 0)
    eidx = lax.broadcasted_iota(I32, (N_EXPERTS, rows), 0)
    group_lg = jnp.where(gidx < N_EXPERT_GROUPS, lgt[0:SUBLANES], neg)
    expert_lg = lgt[EXPERT_LANE0:EXPERT_LANE0 + N_EXPERTS]
    g_sel, i1, i2, w1, w2 = _select(group_lg, expert_lg, gidx, eidx, 0)
    lo = jnp.minimum(i1, i2) - g_sel * EXPERTS_PER_GROUP
    hi = jnp.maximum(i1, i2) - g_sel * EXPERTS_PER_GROUP
    pair = ((lo * (2 * EXPERTS_PER_GROUP - 1 - lo)) >> 1) + (hi - lo - 1)
    cls = g_sel * PAIRS_PER_GROUP + pair
    w_lo = jnp.where(i1 < i2, w1, w2)
    w_hi = jnp.where(i1 < i2, w2, w1)

    crow = lax.broadcasted_iota(I32, (CLASS_ROWS, rows), 0)
    onehot = jnp.where(crow == cls, 1.0, 0.0)
    n_blk = rows // COUNT_BLOCK
    blocks = [onehot[:, j * COUNT_BLOCK:(j + 1) * COUNT_BLOCK] for j in range(n_blk)]
    r = lax.broadcasted_iota(I32, (COUNT_BLOCK, COUNT_BLOCK), 0)
    c = lax.broadcasted_iota(I32, (COUNT_BLOCK, COUNT_BLOCK), 1)
    upper = jnp.where(r < c, 1.0, 0.0).astype(BF16)
    within = _dot(jnp.concatenate(blocks, axis=0).astype(BF16), upper)
    carry = carry_scr[:, 0:1]
    ranks = []
    for j in range(n_blk):
        before = within[j * CLASS_ROWS:(j + 1) * CLASS_ROWS] + carry
        ranks.append(jnp.sum(blocks[j] * before, axis=0, keepdims=True))
        carry = carry + jnp.sum(blocks[j], axis=1, keepdims=True)
    rank = jnp.concatenate(ranks, axis=1)
    carry_scr[...] = jnp.broadcast_to(carry, carry_scr.shape)
    count_ref[...] = carry_scr[...]

    row8 = lax.broadcasted_iota(I32, (SUBLANES, rows), 0)
    idx_ref[...] = jnp.where(row8 == 0, cls, jnp.where(row8 == 1, rank.astype(I32), 0))
    rowl = lax.broadcasted_iota(I32, (LANES, rows), 0)
    gate_rec = jnp.where(rowl == GATE_LO, w_lo, jnp.where(rowl == GATE_HI, w_hi, 0.0)).T
    hbg_ref[:, PACK_W:] = pltpu.bitcast(gate_rec, I32)


def _moe_pair_kernel(ea_ref, eb_ref, used_ref, ha_ref, hbg_ref, wgu_a_ref, wd_a_ref,
                     wgu_b_ref, wd_b_ref, after_ref, ya_ref, yb_ref):
    del ea_ref, eb_ref, after_ref

    @pl.when(pl.program_id(0) < used_ref[0])
    def _():
        h = _unpack_row(ha_ref[...], hbg_ref[:, :PACK_W]).astype(BF16)
        gates = pltpu.bitcast(hbg_ref[:, PACK_W:], F32)
        y = None
        for wgu_ref, wd_ref, lane in ((wgu_a_ref, wd_a_ref, GATE_LO), (wgu_b_ref, wd_b_ref, GATE_HI)):
            gu = _dot(h, wgu_ref[0])
            act = _silu(gu[:, :D_EXPERT]) * gu[:, D_EXPERT:]
            part = _dot((act * gates[:, lane:lane + 1]).astype(BF16), wd_ref[0])
            y = part if y is None else y + part
        ya_ref[...], yb_ref[...] = _pack_row(y)


def _final_norm_kernel(*refs):
    x_ref, ya_ref, yb_ref, g_ref = refs[:4]
    o_ref = refs[-1]
    o_ref[...] = _rmsnorm(x_ref[...] + _unpack_row(ya_ref[...], yb_ref[...]), g_ref[...])


def _moe_dense_kernel(*refs, final_norm, e0, first, last):
    x_ref, ng_ref, wr_cat_ref, wr_hi_ref, br_ref, wg_ref, wu_ref, wdn_ref, nf_ref = refs[:9]
    acc_in_ref = None if first else refs[10]
    y_ref, wgu_ref, wd_ref, h_scr, gate_scr, acc_scr = refs[10 + 3 * (not first):]
    e = pl.program_id(1)

    @pl.when(e == 0)
    def _():
        hn = _rmsnorm(x_ref[...], ng_ref[...])
        h_scr[...] = hn.astype(BF16)
        lg = _router_logits(hn, wr_cat_ref, wr_hi_ref, br_ref)
        lane = lax.broadcasted_iota(I32, lg.shape, 1)
        neg = jnp.float32(-jnp.inf)
        group_lg = jnp.where(lane < N_EXPERT_GROUPS, lg, neg)
        is_expert = (lane >= EXPERT_LANE0) & (lane < EXPERT_LANE0 + N_EXPERTS)
        _, i1, i2, w1, w2 = _select(group_lg, jnp.where(is_expert, lg, neg), lane,
                                    jnp.where(is_expert, lane - EXPERT_LANE0, N_EXPERTS), 1)
        gate_scr[...] = (jnp.where(lane == i1 + EXPERT_LANE0, w1, 0.0)
                         + jnp.where(lane == i2 + EXPERT_LANE0, w2, 0.0))
        acc_scr[...] = jnp.zeros_like(acc_scr) if first else acc_in_ref[...]

    lane = lax.broadcasted_iota(I32, gate_scr.shape, 1)
    gate = jnp.sum(jnp.where(lane == EXPERT_LANE0 + e0 + e, gate_scr[...], 0.0), axis=-1, keepdims=True)
    wgu_ref[0, :, :D_EXPERT] = wg_ref[0].astype(BF16)
    wgu_ref[0, :, D_EXPERT:] = wu_ref[0].astype(BF16)
    wd_ref[0] = wdn_ref[0].astype(BF16)
    gu = _dot(h_scr[...], wgu_ref[0])
    act = _silu(gu[:, :D_EXPERT]) * gu[:, D_EXPERT:]
    acc_scr[...] += _dot((act * gate).astype(BF16), wd_ref[0])

    @pl.when(e == pl.num_programs(1) - 1)
    def _():
        if not last:
            y_ref[...] = acc_scr[...]
            return
        y = x_ref[...] + acc_scr[...]
        if final_norm:
            y = _rmsnorm(y, nf_ref[...])
        y_ref[...] = y


def _const_spec(shape):
    return pl.BlockSpec(shape, lambda *_: (0,) * len(shape))


_ANY_SPEC = pl.BlockSpec(memory_space=pl.ANY)


def _rope_tables(pos):
    half = RET_HEAD_DIM // 2
    inv = np.float32(ROPE_BASE) ** (-np.arange(half, dtype=np.float32) / np.float32(half))
    ang = pos.astype(np.float32)[:, None] * inv[None, :]
    cos, sin = np.cos(ang), np.sin(ang)
    return np.concatenate([cos, cos], -1), np.concatenate([-sin, sin], -1)


def _decay_consts(c, reps):
    f = np.float32
    log_g = np.log(f(1.0) - f(2.0) ** (f(-5.0) - np.arange(RET_HEADS, dtype=f)))
    i = np.arange(c, dtype=f)
    diff = i[:, None] - i[None, :]
    dmat = np.where(diff[None] >= 0, np.exp(np.maximum(diff, f(0))[None] * log_g[:, None, None]), f(0))
    qdec = np.exp((i + f(1))[None, :] * log_g[:, None])
    kdec = np.exp((f(c) - f(1) - i)[None, :] * log_g[:, None])
    cdec = np.exp(f(c) * log_g)
    dmat = np.einsum('ab,hij->haibj', np.eye(reps, dtype=f), dmat).reshape(RET_HEADS, reps * c, reps * c)
    lanes = lambda t: np.repeat(np.tile(t, (1, reps)).T, RET_HEAD_DIM, axis=1)
    cdec = np.broadcast_to(cdec[:, None, None], (RET_HEADS, 1, RET_HEAD_DIM))
    return tuple(np.ascontiguousarray(a, dtype=f) for a in (dmat, lanes(qdec), lanes(kdec), cdec))


def _split_weight(w):
    hi = lax.bitcast_convert_type(_bf16_round_bits(lax.bitcast_convert_type(w, jnp.uint32)), F32)
    return hi.astype(BF16), (w - hi).astype(BF16)


_W_IN_BLOCK = (None, D_MODEL, IN_WIDTH)
_W_POOL_BLOCK = (None, len(POOL_WINDOWS), POOL_GROUP_DIM, POOL_GROUP_DIM)
_W_OUT_BLOCK = (None, D_MODEL, D_MODEL)
_DECAY_SPECS = [
    _const_spec((RET_HEADS, CHUNK, CHUNK)), _const_spec((CHUNK, RET_WIDTH)),
    _const_spec((CHUNK, RET_WIDTH)), _const_spec((RET_HEADS, 1, RET_HEAD_DIM)),
]


def _layer_spec(block, layer, resident=False):
    index_map = lambda *_: (layer,) + (0,) * (len(block) - 1)
    if resident:
        return pl.BlockSpec(block, index_map, pipeline_mode=pl.Buffered(1))
    return pl.BlockSpec(block, index_map)


def _mixer_weight_specs(layer, split):
    n = 2 if split else 1
    row = lambda width: _layer_spec((None, 1, width), layer)
    return [row(D_MODEL), *[_layer_spec(_W_IN_BLOCK, layer, True)] * n,
            *[_layer_spec(_W_POOL_BLOCK, layer)] * n, row(POOL_WIDTH), row(RET_WIDTH),
            *[_layer_spec(_W_OUT_BLOCK, layer, True)] * n]


def _chain_out_spec(block, depth, layer, has_prev):
    zeros = (0,) * (len(block) - 1)
    if has_prev:
        return pl.BlockSpec((None,) + block, lambda i, *_: (layer, i) + zeros)
    return pl.BlockSpec((depth,) + block, lambda i, *_: (0, i) + zeros)


def _mixer_prompt(x, moe_y, weights, route_w, prev_out, layer, rows, precise_tail, after):
    b, l, _ = x.shape
    t = b * l
    depth = weights[0].shape[0]
    assert l % rows == 0 and rows % CHUNK == 0
    cos2, sin2 = _rope_tables(np.arange(l))
    decay = _decay_consts(RET_CHUNK, 1)
    steps = l // rows
    tok = lambda i, c: (i, c, 0)
    if moe_y is None:
        ya = yb = jnp.zeros((1, rows, PACK_W), I32)
        y_spec = _const_spec((1, rows, PACK_W))
    else:
        ya, yb = (t.reshape(b, l, PACK_W) for t in moe_y)
        y_spec = pl.BlockSpec((1, rows, PACK_W), tok)
    has_prev = prev_out is not None
    n_in = 23
    flat = lambda i, c: (i * steps + c, 0)
    y, tails, states, *routed = pl.pallas_call(
        functools.partial(_mixer_prompt_kernel, rows=rows, steps=steps, moe_in=moe_y is not None,
                          precise_tail=precise_tail, layer=layer, has_prev=has_prev),
        grid=(b, steps),
        in_specs=[pl.BlockSpec((1, rows, D_MODEL), tok), y_spec, y_spec,
                  pl.BlockSpec((rows, RET_HEAD_DIM), lambda i, c: (c, 0)),
                  pl.BlockSpec((rows, RET_HEAD_DIM), lambda i, c: (c, 0)),
                  *_mixer_weight_specs(layer, True), *_DECAY_SPECS,
                  _layer_spec((None, 1, D_MODEL), layer), *_router_specs(layer), _ANY_SPEC,
                  *([_ANY_SPEC, _ANY_SPEC] if has_prev else [])],
        out_specs=[pl.BlockSpec((1, rows, D_MODEL), tok),
                   _chain_out_spec((1, POOL_BUF, POOL_WIDTH), depth, layer, has_prev),
                   _chain_out_spec((1, RET_HEADS, RET_HEAD_DIM, RET_HEAD_DIM), depth, layer, has_prev),
                   pl.BlockSpec((rows, PACK_W), flat), pl.BlockSpec((rows, PACK_W + LANES), flat),
                   pl.BlockSpec((SUBLANES, rows), lambda i, c: (0, i * steps + c)),
                   _const_spec((CLASS_ROWS, LANES))],
        out_shape=[jax.ShapeDtypeStruct(x.shape, F32),
                   jax.ShapeDtypeStruct((depth, b, POOL_BUF, POOL_WIDTH), F32),
                   jax.ShapeDtypeStruct((depth, b, RET_HEADS, RET_HEAD_DIM, RET_HEAD_DIM), F32),
                   jax.ShapeDtypeStruct((t, PACK_W), I32),
                   jax.ShapeDtypeStruct((t, PACK_W + LANES), I32), jax.ShapeDtypeStruct((SUBLANES, t), I32),
                   jax.ShapeDtypeStruct((CLASS_ROWS, LANES), F32)],
        input_output_aliases={n_in: 1, n_in + 1: 2} if has_prev else {},
        scratch_shapes=[pltpu.VMEM((1, POOL_HIST + rows, POOL_WIDTH), F32),
                        pltpu.VMEM((RET_HEADS, RET_HEAD_DIM, RET_HEAD_DIM), F32),
                        pltpu.VMEM((rows, RET_WIDTH), F32),
                        pltpu.VMEM((rows, IN_WIDTH), F32),
                        pltpu.VMEM((CLASS_ROWS, LANES), F32)],
        compiler_params=pltpu.CompilerParams(
            dimension_semantics=("arbitrary", "arbitrary"), vmem_limit_bytes=VMEM_LIMIT),
        name="mixer_prompt",
    )(x, ya, yb, cos2, sin2, *weights, *decay, *route_w, after, *(prev_out if has_prev else ()))
    return y, (tails, states), tuple(routed)


def _mixer_sample(x, pool_prev, s0, weights, prev_out, layer, after):
    b, l, _ = x.shape
    depth = s0.shape[0]
    ns = SAMPLE_SEQS
    assert ns * l == CHUNK and b % ns == 0
    cos2, sin2 = _rope_tables(PAST_LEN + np.arange(l))
    cos2, sin2 = np.tile(cos2, (ns, 1)), np.tile(sin2, (ns, 1))
    decay = _decay_consts(l, ns)
    seq3 = lambda i: (i, 0, 0)
    state_block = (ns, RET_HEADS, RET_HEAD_DIM, RET_HEAD_DIM)
    tail_block = (POOL_BUF, ns, POOL_WIDTH)
    if prev_out is None:
        tail_out = pl.BlockSpec((depth,) + tail_block, lambda i: (0, 0, i, 0))
    else:
        tail_out = pl.BlockSpec((None,) + tail_block, lambda i: (layer, 0, i, 0))
    has_prev = prev_out is not None
    n_in = 16
    y, tails, states = pl.pallas_call(
        functools.partial(_mixer_sample_kernel, seq_len=l, layer=layer, has_prev=has_prev),
        grid=(b // ns,),
        in_specs=[pl.BlockSpec((ns, l, D_MODEL), seq3),
                  pl.BlockSpec((None,) + tail_block, lambda i: (layer, 0, i, 0)),
                  pl.BlockSpec((None,) + state_block, lambda i: (layer, i, 0, 0, 0)),
                  _const_spec((CHUNK, RET_HEAD_DIM)), _const_spec((CHUNK, RET_HEAD_DIM)),
                  *_mixer_weight_specs(layer, False), *_DECAY_SPECS, _ANY_SPEC,
                  *([_ANY_SPEC, _ANY_SPEC] if has_prev else [])],
        out_specs=[pl.BlockSpec((ns, l, D_MODEL), seq3), tail_out,
                   _chain_out_spec(state_block, depth, layer, has_prev)],
        out_shape=[jax.ShapeDtypeStruct(x.shape, F32),
                   jax.ShapeDtypeStruct((depth, POOL_BUF, b, POOL_WIDTH), F32),
                   jax.ShapeDtypeStruct(s0.shape, F32)],
        input_output_aliases={n_in: 1, n_in + 1: 2} if has_prev else {},
        scratch_shapes=[pltpu.VMEM((ns, POOL_HIST + l, POOL_WIDTH), F32),
                        pltpu.VMEM((CHUNK, RET_WIDTH), F32)],
        compiler_params=pltpu.CompilerParams(
            dimension_semantics=("arbitrary",), vmem_limit_bytes=VMEM_LIMIT),
        name="mixer_sample",
    )(x, pool_prev, s0, cos2, sin2, *weights, *decay, after, *(prev_out if has_prev else ()))
    return y, (tails, states)


def _router_weights(w_rg, b_rg, w_re, b_re):
    depth = w_rg.shape[0]
    gap = EXPERT_LANE0 - N_EXPERT_GROUPS
    rest = LANES - EXPERT_LANE0 - N_EXPERTS
    wr = jnp.concatenate([w_rg, jnp.zeros((depth, D_MODEL, gap), F32), w_re,
                          jnp.zeros((depth, D_MODEL, rest), F32)], axis=-1)
    br = jnp.concatenate([b_rg, jnp.zeros((depth, gap), F32), b_re, jnp.zeros((depth, rest), F32)],
                         axis=-1).reshape(depth, 1, LANES)
    wr_hi, wr_lo = _split_weight(wr)
    return jnp.concatenate([wr_hi, wr_lo], axis=-1), wr_hi, br


def _router_specs(layer):
    return [_layer_spec((None, D_MODEL, 2 * LANES), layer), _layer_spec((None, D_MODEL, LANES), layer),
            _layer_spec((None, 1, LANES), layer)]


def _sc_mesh():
    return plsc.VectorSubcoreMesh(core_axis_name="core", subcore_axis_name="subcore",
                                  num_cores=SC_CORES, num_subcores=SC_SUBCORES)


def _sc_params():
    params = pltpu.CompilerParams()
    if "needs_layout_passes" in pltpu.CompilerParams.__dataclass_fields__:
        params = dataclasses.replace(params, needs_layout_passes=False)
    return params


def _sc_gather(tables, idx, after):
    n = idx.shape[0]
    assert n % SC_WINDOW == 0
    nt = len(tables)

    def body(*refs):
        i_hbm = refs[nt]
        for t_hbm, o_hbm in zip(refs[:nt], refs[nt + 2:]):
            def gather_window(i_vmem, o_vmem, t_hbm=t_hbm):
                pltpu.sync_copy(t_hbm.at[i_vmem.at[0]], o_vmem)

            pltpu.emit_pipeline(
                gather_window, grid=(n // SC_WINDOW,),
                in_specs=[pl.BlockSpec((1, SC_WINDOW), lambda i: (0, i))],
                out_specs=[pl.BlockSpec((SC_WINDOW, t_hbm.shape[1]), lambda i: (i, 0))],
                core_axis_name=("core", "subcore"),
                dimension_semantics=(pltpu.PARALLEL,),
            )(i_hbm, o_hbm)

    out_type = tuple(jax.ShapeDtypeStruct((n, t.shape[1]), t.dtype) for t in tables)
    return pl.kernel(body, out_type=out_type, mesh=_sc_mesh(), name="sc_gather")(
        *tables, idx.reshape(1, n), after)


def _sc_slots(cls, rank, starts, n_slots):
    t = cls.shape[0]
    workers = SC_CORES * SC_SUBCORES
    slot_per, tok_per = n_slots // workers, t // workers
    assert n_slots % (workers * SC_LANES) == 0 and t % (workers * SC_LANES) == 0 and t & (t - 1) == 0
    assert n_slots % (SC_LANES * SC_UNROLL) == 0 and t % (SC_LANES * SC_UNROLL) == 0

    def body(cls_hbm, rank_hbm, starts_hbm, pos_hbm, slot_hbm, cls_v, rank_v, starts_v, pos_v, slot_v):
        wid = lax.axis_index("subcore") * SC_CORES + lax.axis_index("core")
        pltpu.sync_copy(cls_hbm, cls_v)
        pltpu.sync_copy(rank_hbm, rank_v)
        pltpu.sync_copy(starts_hbm, starts_v)

        lane = lax.iota(I32, SC_LANES)
        span = SC_LANES * SC_UNROLL

        @pl.loop(0, n_slots, step=span)
        def _(i):
            for u in range(SC_UNROLL):
                j = i + u * SC_LANES
                slot_v[pl.ds(j, SC_LANES)] = (lane + j) & (t - 1)

        @pl.loop(0, t, step=span)
        def _(i):
            for u in range(SC_UNROLL):
                j = i + u * SC_LANES
                at = pl.ds(j, SC_LANES)
                pos = plsc.load_gather(starts_v, [cls_v[at]]) + rank_v[at]
                pos_v[at] = pos
                plsc.store_scatter(slot_v, [pos], lane + j)

        tok_off = pl.multiple_of(wid * tok_per, SC_LANES)
        pltpu.sync_copy(pos_v.at[pl.ds(tok_off, tok_per)], pos_hbm.at[pl.ds(tok_off, tok_per)])
        slot_off = pl.multiple_of(wid * slot_per, SC_LANES)
        pltpu.sync_copy(slot_v.at[pl.ds(slot_off, slot_per)], slot_hbm.at[pl.ds(slot_off, slot_per)])

    return pl.kernel(
        body, mesh=_sc_mesh(), compiler_params=_sc_params(), name="sc_slots",
        out_type=(jax.ShapeDtypeStruct((t,), I32), jax.ShapeDtypeStruct((n_slots,), I32)),
        scratch_types=[pltpu.VMEM((t,), I32), pltpu.VMEM((t,), I32), pltpu.VMEM((CLASS_ROWS,), I32),
                       pltpu.VMEM((t,), I32), pltpu.VMEM((n_slots,), I32)],
    )(cls, rank, starts)


def _moe_dispatch(routed, after):
    ha, hbg, idx, counts = routed
    t = ha.shape[0]
    n_slots = t + N_CLASSES * PAIR_TILE
    n_tiles = n_slots // PAIR_TILE

    cnt = counts[:, 0].astype(I32)
    padded = (cnt + PAIR_TILE - 1) // PAIR_TILE * PAIR_TILE
    ends = jnp.cumsum(padded)
    starts = ends - padded
    tile_cls = jnp.minimum(
        jnp.sum(ends[None, :N_CLASSES] <= (jnp.arange(n_tiles, dtype=I32) * PAIR_TILE)[:, None], axis=1),
        N_CLASSES - 1).astype(I32)
    first = (tile_cls // PAIRS_PER_GROUP) * EXPERTS_PER_GROUP
    tile_ea = first + jnp.asarray(PAIR_LO, I32)[tile_cls % PAIRS_PER_GROUP]
    tile_eb = first + jnp.asarray(PAIR_HI, I32)[tile_cls % PAIRS_PER_GROUP]
    used = (ends[N_CLASSES - 1:N_CLASSES] // PAIR_TILE).astype(I32)

    pos, slot_tok = _sc_slots(idx[0], idx[1], starts, n_slots)
    hsa, hsbg = _sc_gather((ha, hbg), slot_tok, after=after)
    return (tile_ea, tile_eb, used, hsa, hsbg), pos, slot_tok


def _moe_pair(dispatched, wgu, wd, after):
    tile_ea, tile_eb, used, hsa, hsbg = dispatched
    n_slots = hsa.shape[0]
    n_tiles = n_slots // PAIR_TILE
    row = lambda i, ea, eb, nu: (jnp.minimum(i, nu[0] - 1), 0)
    w_spec = lambda shape, which: pl.BlockSpec(
        (1,) + shape, lambda i, ea, eb, nu: ((ea, eb)[which][i], 0, 0))
    gu_shape, d_shape = (D_MODEL, 2 * D_EXPERT), (D_EXPERT, D_MODEL)
    ysa, ysb = pl.pallas_call(
        _moe_pair_kernel,
        grid_spec=pltpu.PrefetchScalarGridSpec(
            num_scalar_prefetch=3, grid=(n_tiles,),
            in_specs=[pl.BlockSpec((PAIR_TILE, PACK_W), row),
                      pl.BlockSpec((PAIR_TILE, PACK_W + LANES), row),
                      w_spec(gu_shape, 0), w_spec(d_shape, 0), w_spec(gu_shape, 1), w_spec(d_shape, 1),
                      _ANY_SPEC],
            out_specs=[pl.BlockSpec((PAIR_TILE, PACK_W), row), pl.BlockSpec((PAIR_TILE, PACK_W), row)]),
        out_shape=[jax.ShapeDtypeStruct((n_slots, PACK_W), I32),
                   jax.ShapeDtypeStruct((n_slots, PACK_W), I32)],
        compiler_params=pltpu.CompilerParams(
            dimension_semantics=("arbitrary",), vmem_limit_bytes=VMEM_LIMIT),
        name="moe_pair",
    )(tile_ea, tile_eb, used, hsa, hsbg, wgu, wd, wgu, wd, after)
    return ysa, ysb


def _final_norm(x, moe_y, g, rows, part, prev):
    t = x.shape[0]
    k, n = part
    rows = min(rows, t // n)
    assert t % (n * rows) == 0
    steps = t // n // rows
    tok = lambda i: (k * steps + i, 0)
    sub = lambda i: (i, 0)
    has_prev = prev is not None
    return pl.pallas_call(
        _final_norm_kernel,
        grid=(steps,),
        in_specs=[pl.BlockSpec((rows, D_MODEL), tok), pl.BlockSpec((rows, PACK_W), sub),
                  pl.BlockSpec((rows, PACK_W), sub), _const_spec((1, D_MODEL)),
                  *([_ANY_SPEC] if has_prev else [])],
        out_specs=pl.BlockSpec((rows, D_MODEL), tok),
        out_shape=jax.ShapeDtypeStruct(x.shape, F32),
        input_output_aliases={4: 0} if has_prev else {},
        compiler_params=pltpu.CompilerParams(
            dimension_semantics=("arbitrary",), vmem_limit_bytes=VMEM_LIMIT),
        name="final_norm",
    )(x, *moe_y, g.reshape(1, D_MODEL), *([prev] if has_prev else []))


def _moe_dense(x, norm_g, router, w_gate, w_up, w_down, norm_final, layer, final_norm, rows, after,
               experts=(0, N_EXPERTS), chain=None):
    t = x.shape[0]
    assert t == rows
    e0, e1 = experts
    first, last = chain is None, e1 == N_EXPERTS
    assert first == (e0 == 0)
    tok = lambda i, e: (i, 0)
    w_spec = lambda shape: pl.BlockSpec((None, 1) + shape, lambda i, e: (layer, e0 + e, 0, 0))
    w_out = lambda shape: pl.BlockSpec((1,) + shape, lambda i, e: (e0 + e, 0, 0))
    n_in = 10
    return pl.pallas_call(
        functools.partial(_moe_dense_kernel, final_norm=final_norm, e0=e0, first=first, last=last),
        grid=(t // rows, e1 - e0),
        in_specs=[pl.BlockSpec((rows, D_MODEL), tok), _layer_spec((None, 1, D_MODEL), layer),
                  *_router_specs(layer),
                  w_spec((D_MODEL, D_EXPERT)), w_spec((D_MODEL, D_EXPERT)), w_spec((D_EXPERT, D_MODEL)),
                  _const_spec((1, D_MODEL)), _ANY_SPEC,
                  *([] if first else [pl.BlockSpec((rows, D_MODEL), tok), _ANY_SPEC, _ANY_SPEC])],
        out_specs=[pl.BlockSpec((rows, D_MODEL), tok), w_out((D_MODEL, 2 * D_EXPERT)),
                   w_out((D_EXPERT, D_MODEL))],
        out_shape=[jax.ShapeDtypeStruct(x.shape, F32),
                   jax.ShapeDtypeStruct((N_EXPERTS, D_MODEL, 2 * D_EXPERT), BF16),
                   jax.ShapeDtypeStruct((N_EXPERTS, D_EXPERT, D_MODEL), BF16)],
        input_output_aliases={} if first else {n_in + 1: 1, n_in + 2: 2},
        scratch_shapes=[pltpu.VMEM((rows, D_MODEL), BF16),
                        pltpu.VMEM((rows, LANES), F32),
                        pltpu.VMEM((rows, D_MODEL), F32)],
        compiler_params=pltpu.CompilerParams(
            dimension_semantics=("arbitrary", "arbitrary"), vmem_limit_bytes=VMEM_LIMIT),
        name="moe_dense",
    )(x, norm_g, *router, w_gate, w_up, w_down, norm_final.reshape(1, D_MODEL), after,
      *(() if first else chain))


def kernel(x_prompt, x_sample, cache_pool, state_ret, norm_mix, w_in, w_pool, pool_scale, ret_gn, w_out, norm_ffn, w_router_group, b_router_group, w_router_expert, b_router_expert, w_gate, w_up, w_down, norm_final):
    depth = norm_mix.shape[0]
    row = lambda a: a.reshape(depth, 1, a.shape[-1])
    mix_split = (row(norm_mix), *_split_weight(w_in), *_split_weight(w_pool), row(pool_scale),
                 row(ret_gn), *_split_weight(w_out))
    mix_hi = tuple(mix_split[i] for i in (0, 1, 3, 5, 6, 7))
    router = _router_weights(w_router_group, b_router_group, w_router_expert, b_router_expert)
    norm_ffn = row(norm_ffn)
    pool_prev = jnp.swapaxes(cache_pool, 1, 2)

    yp, ys = x_prompt, x_sample
    moe_p = None
    out_p = out_s = None
    ys_mix = None
    for l in range(depth):
        yp, out_p, routed = _mixer_prompt(
            yp, moe_p, mix_split, (norm_ffn, *router), out_p, l, rows=512,
            precise_tail=PRECISE_TAIL_STEPS if l < depth - 1 else 0,
            after=ys if ys_mix is None else ys_mix)
        if ys_mix is None:
            ys_mix, out_s = _mixer_sample(ys, pool_prev, state_ret, mix_hi, out_s, l, after=routed[-1])
        dense = functools.partial(_moe_dense, ys_mix.reshape(-1, D_MODEL), norm_ffn, router, w_gate, w_up,
                                  w_down, norm_final, l, l == depth - 1,
                                  rows=ys_mix.shape[0] * ys_mix.shape[1])
        head = dense(after=routed[-1], experts=(0, DENSE_HEAD)) if l else None
        dispatched, pos, slot_tok = _moe_dispatch(routed, after=ys_mix if l == 0 else head[0])
        ys, wgu, wd = dense(after=slot_tok, experts=(DENSE_HEAD if l else 0, N_EXPERTS), chain=head)
        ys = ys.reshape(x_sample.shape)
        sorted_y = _moe_pair(dispatched, wgu, wd, after=ys)
        if l + 1 < depth:
            moe_p = _sc_gather(sorted_y, pos, after=routed[-1])
            ys_mix, out_s = _mixer_sample(ys, pool_prev, state_ret, mix_hi, out_s, l + 1,
                                          after=sorted_y[0])
    yp_tok = yp.reshape(-1, D_MODEL)
    part = yp_tok.shape[0] // FINAL_PARTS
    out = None
    for k in range(FINAL_PARTS):
        moe_k = _sc_gather(sorted_y, pos[k * part:(k + 1) * part], after=routed[-1])
        out = _final_norm(yp_tok, moe_k, norm_final, 2048, (k, FINAL_PARTS), out)
    return (out.reshape(yp.shape), ys, *out_p, jnp.swapaxes(out_s[0], 1, 2), out_s[1])
```

```python
import dataclasses
import functools
import itertools

import jax
import jax.numpy as jnp
import numpy as np
from jax import lax
from jax.experimental import pallas as pl
from jax.experimental.pallas import tpu as pltpu
from jax.experimental.pallas import tpu_sc as plsc

F32 = jnp.float32
BF16 = jnp.bfloat16
I32 = jnp.int32

D_MODEL = 1024
POOL_WIDTH = 512
POOL_WINDOWS = (2, 4, 8, 16)
POOL_GROUP_DIM = 128
POOL_BUF = 15
POOL_HIST = 16
RET_WIDTH = 512
RET_HEADS = 4
RET_HEAD_DIM = 128
RET_CHUNK = 128
ROPE_BASE = 10000.0
IN_WIDTH = POOL_WIDTH + 4 * RET_WIDTH
N_EXPERT_GROUPS = 4
EXPERTS_PER_GROUP = 4
N_EXPERTS = N_EXPERT_GROUPS * EXPERTS_PER_GROUP
D_EXPERT = 256
RMS_EPS = 1e-6
GN_EPS = 1e-5
PAST_LEN = 16384

LANES = 128
SUBLANES = 8
EXPERT_LANE0 = 8
GROUP_SHIFT = EXPERTS_PER_GROUP.bit_length() - 1
PAIRS = tuple(itertools.combinations(range(EXPERTS_PER_GROUP), 2))
PAIRS_PER_GROUP = len(PAIRS)
PAIR_LO, PAIR_HI = zip(*PAIRS)
N_CLASSES = N_EXPERT_GROUPS * PAIRS_PER_GROUP
CLASS_ROWS = 32
GATE_LO, GATE_HI = 0, 1
COUNT_BLOCK = 256
PAIR_TILE = 256
PACK_W = D_MODEL // 4
SC_CORES, SC_SUBCORES, SC_LANES = 2, 16, 16
SC_WINDOW = 128
SC_UNROLL = 8
PRECISE_TAIL_STEPS = 1
CHUNK = 128
SAMPLE_SEQS = 16
SAMPLE_PARTS = 2
VMEM_LIMIT = 56 * 1024 * 1024


def _dot(a, b):
    return jnp.dot(a, b, preferred_element_type=F32)


def _dot_nt(a, b):
    return lax.dot_general(a, b, (((1,), (1,)), ((), ())), preferred_element_type=F32)


def _bf16_round_bits(u):
    return (u + jnp.uint32(0x7FFF) + ((u >> 16) & jnp.uint32(1))) & jnp.uint32(0xFFFF0000)


def _split(a):
    hi = pltpu.bitcast(_bf16_round_bits(pltpu.bitcast(a, jnp.uint32)), F32)
    return hi.astype(BF16), (a - hi).astype(BF16)


def _mm(a, b, precise, nt=False):
    dot = _dot_nt if nt else _dot
    if precise:
        b_hi, b_lo = b if isinstance(b, tuple) else _split(b)
        a_hi, a_lo = _split(a)
        return dot(a_hi, b_hi) + dot(a_lo, b_hi) + dot(a_hi, b_lo)
    return dot(a.astype(BF16), b[0] if isinstance(b, tuple) else b.astype(BF16))


def _rmsnorm(x, g):
    ms = jnp.mean(x * x, axis=-1, keepdims=True)
    return x * lax.rsqrt(ms + RMS_EPS) * g


def _pool_mix(ubuf, rows, t_first, n_prev, wpool_refs, pscale, precise=False, row0=0):
    ns = ubuf.shape[0]
    t = t_first + lax.broadcasted_iota(I32, (1, rows, POOL_GROUP_DIM), 1)
    base = POOL_HIST + row0
    outs = []
    for j, w in enumerate(POOL_WINDOWS):
        lanes = slice(j * POOL_GROUP_DIM, (j + 1) * POOL_GROUP_DIM)
        uj = ubuf[:, base:base + rows, lanes]
        acc = uj
        for i in range(1, w):
            acc = acc + ubuf[:, base - i:base - i + rows, lanes]
        cnt = jnp.minimum(w, n_prev + t + 1).astype(F32)
        d = (acc / cnt - uj).reshape(ns * rows, POOL_GROUP_DIM)
        outs.append(_mm(d, tuple(w[j] for w in wpool_refs), precise))
    return jnp.concatenate(outs, axis=-1) * pscale


def _rope(xh, cos2, sin2):
    return xh * cos2 + pltpu.roll(xh, RET_HEAD_DIM // 2, 1) * sin2


def _group_norm(o):
    mu = jnp.mean(o, axis=-1, keepdims=True)
    c = o - mu
    var = jnp.mean(c * c, axis=-1, keepdims=True)
    return c * lax.rsqrt(var + GN_EPS)


def _silu(x):
    return x * (1.0 / (1.0 + jnp.exp(-x)))


def _head(a, h):
    return a[:, h * RET_HEAD_DIM:(h + 1) * RET_HEAD_DIM]


def _qkvg(z):
    p, r = POOL_WIDTH, RET_WIDTH
    return z[:, p:p + r], z[:, p + r:p + 2 * r], z[:, p + 2 * r:p + 3 * r], z[:, p + 3 * r:p + 4 * r]


def _pack_bf16_pair(a, b):
    ua = pltpu.bitcast(a.astype(BF16).astype(F32), jnp.uint32)
    ub = pltpu.bitcast(b.astype(BF16).astype(F32), jnp.uint32)
    return pltpu.bitcast((ua >> 16) | (ub & jnp.uint32(0xFFFF0000)), I32)


def _unpack_bf16_pair(w):
    u = pltpu.bitcast(w, jnp.uint32)
    return pltpu.bitcast(u << 16, F32), pltpu.bitcast(u & jnp.uint32(0xFFFF0000), F32)


def _pack_row(y):
    q = PACK_W
    return _pack_bf16_pair(y[:, 0:q], y[:, q:2 * q]), _pack_bf16_pair(y[:, 2 * q:3 * q], y[:, 3 * q:])


def _unpack_row(wa, wb):
    return jnp.concatenate([*_unpack_bf16_pair(wa), *_unpack_bf16_pair(wb)], axis=-1)


def _zero_other_layers(ref, layer):
    for j in range(ref.shape[0]):
        if j != layer:
            ref[j] = jnp.zeros(ref.shape[1:], ref.dtype)


def _mixer_prompt_kernel(*refs, rows, steps, moe_in, precise_tail, layer, has_prev):
    (x_ref, ya_ref, yb_ref, cos_ref, sin_ref, ng_ref, win_hi_ref, win_lo_ref, wpool_hi_ref,
     wpool_lo_ref, pscale_ref, gn_ref, wout_hi_ref, wout_lo_ref, dmat_ref, qdec_ref, kdec_ref,
     cdec_ref) = refs[:18]
    route_in = refs[18:22]
    y_ref, tail_ref, sfin_ref, *route_out, ubuf, s_scr, ret_scr, z_scr, count_scr = (
        refs[23 + 2 * has_prev:])
    c = pl.program_id(1)

    @pl.when((pl.program_id(0) == 0) & (c == 0))
    def _():
        count_scr[...] = jnp.zeros_like(count_scr)
    if not has_prev:
        _zero_other_layers(tail_ref, layer)
        _zero_other_layers(sfin_ref, layer)
        tail_ref, sfin_ref = tail_ref.at[layer], sfin_ref.at[layer]

    @pl.when(c == 0)
    def _():
        ubuf[:, 0:POOL_HIST, :] = jnp.zeros((1, POOL_HIST, POOL_WIDTH), F32)
        s_scr[...] = jnp.zeros_like(s_scr)

    kv_cols = slice(POOL_WIDTH + RET_WIDTH, POOL_WIDTH + 3 * RET_WIDTH)

    def step(kv_precise, full_from):
        x = x_ref[0]
        if moe_in:
            x = x + _unpack_row(ya_ref[0], yb_ref[0])
        hn = _rmsnorm(x, ng_ref[...])
        hi, lo = _split(hn) if kv_precise else (hn.astype(BF16), None)
        z_scr[...] = _dot(hi, win_hi_ref[...])
        if kv_precise and full_from:
            z_scr[:full_from, kv_cols] += (_dot(lo[:full_from], win_hi_ref[:, kv_cols])
                                           + _dot(hi[:full_from], win_lo_ref[:, kv_cols]))
        if full_from < rows:
            z_scr[full_from:, :] += (_dot(lo[full_from:], win_hi_ref[...])
                                     + _dot(hi[full_from:], win_lo_ref[...]))

        ubuf[0, POOL_HIST:POOL_HIST + rows, :] = z_scr[:, :POOL_WIDTH]
        pool_w = (wpool_hi_ref, wpool_lo_ref)
        pool_parts = []
        if full_from:
            pool_parts.append(_pool_mix(ubuf, full_from, c * rows, 0, pool_w, pscale_ref[...]))
        if full_from < rows:
            pool_parts.append(_pool_mix(ubuf, rows - full_from, c * rows + full_from, 0, pool_w,
                                        pscale_ref[...], precise=True, row0=full_from))
        pool_out = jnp.concatenate(pool_parts, axis=0)
        tail_ref[...] = ubuf[:, rows + POOL_HIST - POOL_BUF:rows + POOL_HIST, :]
        ubuf[:, 0:POOL_HIST, :] = ubuf[:, rows:rows + POOL_HIST, :]

        scale = RET_HEAD_DIM ** -0.5
        for ci in range(rows // CHUNK):
            rs = slice(ci * CHUNK, (ci + 1) * CHUNK)
            full = ci * CHUNK >= full_from
            cos2 = cos_ref[rs, :]
            sin2 = sin_ref[rs, :]
            for h in range(RET_HEADS):
                col = lambda part: slice(POOL_WIDTH + part * RET_WIDTH + h * RET_HEAD_DIM,
                                         POOL_WIDTH + part * RET_WIDTH + (h + 1) * RET_HEAD_DIM)
                q = _rope(z_scr[rs, col(0)], cos2, sin2)
                k = _rope(z_scr[rs, col(1)], cos2, sin2) * scale
                v = z_scr[rs, col(2)]
                s_old = s_scr[h]
                scores = _mm(q, k, full, nt=True) * dmat_ref[h]
                qd = q * _head(qdec_ref[...], h)
                o = _mm(scores, v, full) + _mm(qd, s_old, full)
                kd = k * _head(kdec_ref[...], h)
                s_scr[h] = s_old * cdec_ref[h] + _mm(kd.T, v, kv_precise)
                on = _group_norm(o) * _head(gn_ref[...], h)
                ret_scr[rs, h * RET_HEAD_DIM:(h + 1) * RET_HEAD_DIM] = _silu(z_scr[rs, col(3)]) * on

        mix_in = jnp.concatenate([pool_out, ret_scr[...]], axis=-1)
        y_ref[0] = x + _dot(mix_in.astype(BF16), wout_hi_ref[...])
        if full_from < rows:
            m_hi, m_lo = _split(mix_in[full_from:])
            y_ref[0, full_from:, :] += _dot(m_lo, wout_hi_ref[...]) + _dot(m_hi, wout_lo_ref[...])
        sfin_ref[0] = s_scr[...]
        _route_tile(y_ref[0], full_from, *route_in, *route_out, count_scr)

    if precise_tail:
        pl.when(c < steps - precise_tail)(lambda: step(False, rows))
        if precise_tail > 1:
            pl.when((c >= steps - precise_tail) & (c < steps - 1))(lambda: step(True, rows))
        pl.when(c == steps - 1)(lambda: step(True, rows - CHUNK))
    else:
        step(False, rows)


def _mixer_sample_kernel(*refs, seq_len, layer, has_prev, n_alias):
    (x_ref, prev_ref, s0_ref, cos_ref, sin_ref, ng_ref, win_ref, wpool_ref, pscale_ref, gn_ref,
     wout_ref, dmat_ref, qdec_ref, kdec_ref, cdec_ref) = refs[:15]
    y_ref, tail_ref, sfin_ref, ubuf, ret_scr = refs[16 + n_alias:]
    if not has_prev:
        _zero_other_layers(tail_ref, layer)
        _zero_other_layers(sfin_ref, layer)
        tail_ref, sfin_ref = tail_ref.at[layer], sfin_ref.at[layer]
    ns = SAMPLE_SEQS
    rows = ns * seq_len
    x = x_ref[...].reshape(rows, D_MODEL)
    hn = _rmsnorm(x, ng_ref[...]).astype(BF16)
    z = _dot(hn, win_ref[...])
    for j in range(POOL_BUF):
        ubuf[:, POOL_HIST - POOL_BUF + j, :] = prev_ref[j]
    ubuf[:, POOL_HIST:POOL_HIST + seq_len, :] = z[:, :POOL_WIDTH].reshape(ns, seq_len, POOL_WIDTH)
    pool_out = _pool_mix(ubuf, seq_len, 0, POOL_BUF, (wpool_ref,), pscale_ref[...])
    for j in range(POOL_BUF):
        tail_ref[j] = ubuf[:, seq_len + POOL_HIST - POOL_BUF + j, :]

    q_all, k_all, v_all, g_all = _qkvg(z)
    scale = RET_HEAD_DIM ** -0.5
    cos2 = cos_ref[...]
    sin2 = sin_ref[...]
    tok_seq = lax.broadcasted_iota(I32, (RET_HEAD_DIM, rows), 1) // seq_len
    for h in range(RET_HEADS):
        q = _rope(_head(q_all, h), cos2, sin2)
        k = _rope(_head(k_all, h), cos2, sin2) * scale
        vb = _head(v_all, h).astype(BF16)
        s_old = s0_ref[:, h]
        scores = _dot_nt(q.astype(BF16), k.astype(BF16)) * dmat_ref[h]
        qd = (q * _head(qdec_ref[...], h)).astype(BF16).reshape(ns, seq_len, RET_HEAD_DIM)
        o_state = jnp.einsum('bid,bde->bie', qd, s_old.astype(BF16), preferred_element_type=F32)
        o = _dot(scores.astype(BF16), vb) + o_state.reshape(rows, RET_HEAD_DIM)
        kdt = (k * _head(kdec_ref[...], h)).T
        lhs = jnp.concatenate(
            [jnp.where(tok_seq == b, kdt, 0.0).astype(BF16) for b in range(ns)], axis=0)
        upd = _dot(lhs, vb).reshape(ns, RET_HEAD_DIM, RET_HEAD_DIM)
        sfin_ref[:, h] = s_old * cdec_ref[h] + upd
        on = _group_norm(o) * _head(gn_ref[...], h)
        ret_scr[:, h * RET_HEAD_DIM:(h + 1) * RET_HEAD_DIM] = _silu(_head(g_all, h)) * on

    mix_in = jnp.concatenate([pool_out, ret_scr[...]], axis=-1).astype(BF16)
    y_ref[...] = (x + _dot(mix_in, wout_ref[...])).reshape(ns, seq_len, D_MODEL)


def _router_logits(hn, wr_cat_ref, wr_hi_ref, br_ref, precise_from=0):
    parts = []
    if precise_from:
        parts.append(_dot(hn[:precise_from].astype(BF16), wr_hi_ref[...]))
    if precise_from < hn.shape[0]:
        h_hi, h_lo = _split(hn[precise_from:])
        part = _dot(h_hi, wr_cat_ref[...])
        parts.append(part[:, :LANES] + part[:, LANES:] + _dot(h_lo, wr_hi_ref[...]))
    return jnp.concatenate(parts, axis=0) + br_ref[...]


def _select(group_lg, expert_lg, gidx, eidx, axis):
    red = dict(axis=axis, keepdims=True)
    neg = jnp.float32(-jnp.inf)
    gmax = jnp.max(group_lg, **red)
    g_sel = jnp.min(jnp.where(group_lg == gmax, gidx, N_EXPERT_GROUPS), **red)
    p_sel = 1.0 / jnp.sum(jnp.exp(group_lg - gmax), **red)
    emask = (eidx >> GROUP_SHIFT) == g_sel
    v1 = jnp.max(jnp.where(emask, expert_lg, neg), **red)
    i1 = jnp.min(jnp.where(emask & (expert_lg == v1), eidx, N_EXPERTS), **red)
    emask2 = emask & (eidx != i1)
    v2 = jnp.max(jnp.where(emask2, expert_lg, neg), **red)
    i2 = jnp.min(jnp.where(emask2 & (expert_lg == v2), eidx, N_EXPERTS), **red)
    e2 = jnp.exp(v2 - v1)
    return g_sel, i1, i2, p_sel / (1.0 + e2), p_sel * e2 / (1.0 + e2)


def _route_tile(x, precise_from, ng_ref, wr_cat_ref, wr_hi_ref, br_ref, ha_ref, hbg_ref, idx_ref,
                count_ref, carry_scr):
    rows = x.shape[0]
    hn = _rmsnorm(x, ng_ref[...])
    ha_ref[...], hbg_ref[:, :PACK_W] = _pack_row(hn)
    lgt = _router_logits(hn, wr_cat_ref, wr_hi_ref, br_ref, precise_from).T
    neg = jnp.float32(-jnp.inf)
    gidx = lax.broadcasted_iota(I32, (SUBLANES, rows), 0)
    eidx = lax.broadcasted_iota(I32, (N_EXPERTS, rows), 0)
    group_lg = jnp.where(gidx < N_EXPERT_GROUPS, lgt[0:SUBLANES], neg)
    expert_lg = lgt[EXPERT_LANE0:EXPERT_LANE0 + N_EXPERTS]
    g_sel, i1, i2, w1, w2 = _select(group_lg, expert_lg, gidx, eidx, 0)
    lo = jnp.minimum(i1, i2) - g_sel * EXPERTS_PER_GROUP
    hi = jnp.maximum(i1, i2) - g_sel * EXPERTS_PER_GROUP
    pair = ((lo * (2 * EXPERTS_PER_GROUP - 1 - lo)) >> 1) + (hi - lo - 1)
    cls = g_sel * PAIRS_PER_GROUP + pair
    w_lo = jnp.where(i1 < i2, w1, w2)
    w_hi = jnp.where(i1 < i2, w2, w1)

    crow = lax.broadcasted_iota(I32, (CLASS_ROWS, rows), 0)
    onehot = jnp.where(crow == cls, 1.0, 0.0)
    n_blk = rows // COUNT_BLOCK
    blocks = [onehot[:, j * COUNT_BLOCK:(j + 1) * COUNT_BLOCK] for j in range(n_blk)]
    r = lax.broadcasted_iota(I32, (COUNT_BLOCK, COUNT_BLOCK), 0)
    c = lax.broadcasted_iota(I32, (COUNT_BLOCK, COUNT_BLOCK), 1)
    upper = jnp.where(r < c, 1.0, 0.0).astype(BF16)
    within = _dot(jnp.concatenate(blocks, axis=0).astype(BF16), upper)
    carry = carry_scr[:, 0:1]
    ranks = []
    for j in range(n_blk):
        before = within[j * CLASS_ROWS:(j + 1) * CLASS_ROWS] + carry
        ranks.append(jnp.sum(blocks[j] * before, axis=0, keepdims=True))
        carry = carry + jnp.sum(blocks[j], axis=1, keepdims=True)
    rank = jnp.concatenate(ranks, axis=1)
    carry_scr[...] = jnp.broadcast_to(carry, carry_scr.shape)
    count_ref[...] = carry_scr[...]

    row8 = lax.broadcasted_iota(I32, (SUBLANES, rows), 0)
    idx_ref[...] = jnp.where(row8 == 0, cls, jnp.where(row8 == 1, rank.astype(I32), 0))
    rowl = lax.broadcasted_iota(I32, (LANES, rows), 0)
    gate_rec = jnp.where(rowl == GATE_LO, w_lo, jnp.where(rowl == GATE_HI, w_hi, 0.0)).T
    hbg_ref[:, PACK_W:] = pltpu.bitcast(gate_rec, I32)


def _moe_pair_kernel(ea_ref, eb_ref, used_ref, ha_ref, hbg_ref, wgu_a_ref, wd_a_ref,
                     wgu_b_ref, wd_b_ref, after_ref, ya_ref, yb_ref):
    del ea_ref, eb_ref, after_ref

    @pl.when(pl.program_id(0) < used_ref[0])
    def _():
        h = _unpack_row(ha_ref[...], hbg_ref[:, :PACK_W]).astype(BF16)
        gates = pltpu.bitcast(hbg_ref[:, PACK_W:], F32)
        y = None
        for wgu_ref, wd_ref, lane in ((wgu_a_ref, wd_a_ref, GATE_LO), (wgu_b_ref, wd_b_ref, GATE_HI)):
            gu = _dot(h, wgu_ref[0])
            act = _silu(gu[:, :D_EXPERT]) * gu[:, D_EXPERT:]
            part = _dot((act * gates[:, lane:lane + 1]).astype(BF16), wd_ref[0])
            y = part if y is None else y + part
        ya_ref[...], yb_ref[...] = _pack_row(y)


def _final_norm_kernel(x_ref, ya_ref, yb_ref, g_ref, o_ref):
    o_ref[...] = _rmsnorm(x_ref[...] + _unpack_row(ya_ref[...], yb_ref[...]), g_ref[...])


def _moe_dense_kernel(*refs, final_norm, e0, first, last):
    x_ref, ng_ref, wr_cat_ref, wr_hi_ref, br_ref, wg_ref, wu_ref, wdn_ref, nf_ref = refs[:9]
    acc_in_ref = None if first else refs[10]
    y_ref, wgu_ref, wd_ref, h_scr, gate_scr, acc_scr = refs[10 + 3 * (not first):]
    e = pl.program_id(1)

    @pl.when(e == 0)
    def _():
        hn = _rmsnorm(x_ref[...], ng_ref[...])
        h_scr[...] = hn.astype(BF16)
        lg = _router_logits(hn, wr_cat_ref, wr_hi_ref, br_ref)
        lane = lax.broadcasted_iota(I32, lg.shape, 1)
        neg = jnp.float32(-jnp.inf)
        group_lg = jnp.where(lane < N_EXPERT_GROUPS, lg, neg)
        is_expert = (lane >= EXPERT_LANE0) & (lane < EXPERT_LANE0 + N_EXPERTS)
        _, i1, i2, w1, w2 = _select(group_lg, jnp.where(is_expert, lg, neg), lane,
                                    jnp.where(is_expert, lane - EXPERT_LANE0, N_EXPERTS), 1)
        gate_scr[...] = (jnp.where(lane == i1 + EXPERT_LANE0, w1, 0.0)
                         + jnp.where(lane == i2 + EXPERT_LANE0, w2, 0.0))
        acc_scr[...] = jnp.zeros_like(acc_scr) if first else acc_in_ref[...]

    lane = lax.broadcasted_iota(I32, gate_scr.shape, 1)
    gate = jnp.sum(jnp.where(lane == EXPERT_LANE0 + e0 + e, gate_scr[...], 0.0), axis=-1, keepdims=True)
    wgu_ref[0, :, :D_EXPERT] = wg_ref[0].astype(BF16)
    wgu_ref[0, :, D_EXPERT:] = wu_ref[0].astype(BF16)
    wd_ref[0] = wdn_ref[0].astype(BF16)
    gu = _dot(h_scr[...], wgu_ref[0])
    act = _silu(gu[:, :D_EXPERT]) * gu[:, D_EXPERT:]
    acc_scr[...] += _dot((act * gate).astype(BF16), wd_ref[0])

    @pl.when(e == pl.num_programs(1) - 1)
    def _():
        if not last:
            y_ref[...] = acc_scr[...]
            return
        y = x_ref[...] + acc_scr[...]
        if final_norm:
            y = _rmsnorm(y, nf_ref[...])
        y_ref[...] = y


def _const_spec(shape):
    return pl.BlockSpec(shape, lambda *_: (0,) * len(shape))


_ANY_SPEC = pl.BlockSpec(memory_space=pl.ANY)


def _rope_tables(pos):
    half = RET_HEAD_DIM // 2
    inv = np.float32(ROPE_BASE) ** (-np.arange(half, dtype=np.float32) / np.float32(half))
    ang = pos.astype(np.float32)[:, None] * inv[None, :]
    cos, sin = np.cos(ang), np.sin(ang)
    return np.concatenate([cos, cos], -1), np.concatenate([-sin, sin], -1)


def _decay_consts(c, reps):
    f = np.float32
    log_g = np.log(f(1.0) - f(2.0) ** (f(-5.0) - np.arange(RET_HEADS, dtype=f)))
    i = np.arange(c, dtype=f)
    diff = i[:, None] - i[None, :]
    dmat = np.where(diff[None] >= 0, np.exp(np.maximum(diff, f(0))[None] * log_g[:, None, None]), f(0))
    qdec = np.exp((i + f(1))[None, :] * log_g[:, None])
    kdec = np.exp((f(c) - f(1) - i)[None, :] * log_g[:, None])
    cdec = np.exp(f(c) * log_g)
    dmat = np.einsum('ab,hij->haibj', np.eye(reps, dtype=f), dmat).reshape(RET_HEADS, reps * c, reps * c)
    lanes = lambda t: np.repeat(np.tile(t, (1, reps)).T, RET_HEAD_DIM, axis=1)
    cdec = np.broadcast_to(cdec[:, None, None], (RET_HEADS, 1, RET_HEAD_DIM))
    return tuple(np.ascontiguousarray(a, dtype=f) for a in (dmat, lanes(qdec), lanes(kdec), cdec))


def _split_weight(w):
    hi = lax.bitcast_convert_type(_bf16_round_bits(lax.bitcast_convert_type(w, jnp.uint32)), F32)
    return hi.astype(BF16), (w - hi).astype(BF16)


_W_IN_BLOCK = (None, D_MODEL, IN_WIDTH)
_W_POOL_BLOCK = (None, len(POOL_WINDOWS), POOL_GROUP_DIM, POOL_GROUP_DIM)
_W_OUT_BLOCK = (None, D_MODEL, D_MODEL)
_DECAY_SPECS = [
    _const_spec((RET_HEADS, CHUNK, CHUNK)), _const_spec((CHUNK, RET_WIDTH)),
    _const_spec((CHUNK, RET_WIDTH)), _const_spec((RET_HEADS, 1, RET_HEAD_DIM)),
]


def _layer_spec(block, layer, resident=False):
    index_map = lambda *_: (layer,) + (0,) * (len(block) - 1)
    if resident:
        return pl.BlockSpec(block, index_map, pipeline_mode=pl.Buffered(1))
    return pl.BlockSpec(block, index_map)


def _mixer_weight_specs(layer, split):
    n = 2 if split else 1
    row = lambda width: _layer_spec((None, 1, width), layer)
    return [row(D_MODEL), *[_layer_spec(_W_IN_BLOCK, layer, True)] * n,
            *[_layer_spec(_W_POOL_BLOCK, layer)] * n, row(POOL_WIDTH), row(RET_WIDTH),
            *[_layer_spec(_W_OUT_BLOCK, layer, True)] * n]


def _chain_out_spec(block, depth, layer, has_prev):
    zeros = (0,) * (len(block) - 1)
    if has_prev:
        return pl.BlockSpec((None,) + block, lambda i, *_: (layer, i) + zeros)
    return pl.BlockSpec((depth,) + block, lambda i, *_: (0, i) + zeros)


def _mixer_prompt(x, moe_y, weights, route_w, prev_out, layer, rows, precise_tail, after):
    b, l, _ = x.shape
    t = b * l
    depth = weights[0].shape[0]
    assert l % rows == 0 and rows % CHUNK == 0
    cos2, sin2 = _rope_tables(np.arange(l))
    decay = _decay_consts(RET_CHUNK, 1)
    steps = l // rows
    tok = lambda i, c: (i, c, 0)
    if moe_y is None:
        ya = yb = jnp.zeros((1, rows, PACK_W), I32)
        y_spec = _const_spec((1, rows, PACK_W))
    else:
        ya, yb = (t.reshape(b, l, PACK_W) for t in moe_y)
        y_spec = pl.BlockSpec((1, rows, PACK_W), tok)
    has_prev = prev_out is not None
    n_in = 23
    flat = lambda i, c: (i * steps + c, 0)
    y, tails, states, *routed = pl.pallas_call(
        functools.partial(_mixer_prompt_kernel, rows=rows, steps=steps, moe_in=moe_y is not None,
                          precise_tail=precise_tail, layer=layer, has_prev=has_prev),
        grid=(b, steps),
        in_specs=[pl.BlockSpec((1, rows, D_MODEL), tok), y_spec, y_spec,
                  pl.BlockSpec((rows, RET_HEAD_DIM), lambda i, c: (c, 0)),
                  pl.BlockSpec((rows, RET_HEAD_DIM), lambda i, c: (c, 0)),
                  *_mixer_weight_specs(layer, True), *_DECAY_SPECS,
                  _layer_spec((None, 1, D_MODEL), layer), *_router_specs(layer), _ANY_SPEC,
                  *([_ANY_SPEC, _ANY_SPEC] if has_prev else [])],
        out_specs=[pl.BlockSpec((1, rows, D_MODEL), tok),
                   _chain_out_spec((1, POOL_BUF, POOL_WIDTH), depth, layer, has_prev),
                   _chain_out_spec((1, RET_HEADS, RET_HEAD_DIM, RET_HEAD_DIM), depth, layer, has_prev),
                   pl.BlockSpec((rows, PACK_W), flat), pl.BlockSpec((rows, PACK_W + LANES), flat),
                   pl.BlockSpec((SUBLANES, rows), lambda i, c: (0, i * steps + c)),
                   _const_spec((CLASS_ROWS, LANES))],
        out_shape=[jax.ShapeDtypeStruct(x.shape, F32),
                   jax.ShapeDtypeStruct((depth, b, POOL_BUF, POOL_WIDTH), F32),
                   jax.ShapeDtypeStruct((depth, b, RET_HEADS, RET_HEAD_DIM, RET_HEAD_DIM), F32),
                   jax.ShapeDtypeStruct((t, PACK_W), I32),
                   jax.ShapeDtypeStruct((t, PACK_W + LANES), I32), jax.ShapeDtypeStruct((SUBLANES, t), I32),
                   jax.ShapeDtypeStruct((CLASS_ROWS, LANES), F32)],
        input_output_aliases={n_in: 1, n_in + 1: 2} if has_prev else {},
        scratch_shapes=[pltpu.VMEM((1, POOL_HIST + rows, POOL_WIDTH), F32),
                        pltpu.VMEM((RET_HEADS, RET_HEAD_DIM, RET_HEAD_DIM), F32),
                        pltpu.VMEM((rows, RET_WIDTH), F32),
                        pltpu.VMEM((rows, IN_WIDTH), F32),
                        pltpu.VMEM((CLASS_ROWS, LANES), F32)],
        compiler_params=pltpu.CompilerParams(
            dimension_semantics=("arbitrary", "arbitrary"), vmem_limit_bytes=VMEM_LIMIT),
        name="mixer_prompt",
    )(x, ya, yb, cos2, sin2, *weights, *decay, *route_w, after, *(prev_out if has_prev else ()))
    return y, (tails, states), tuple(routed)


def _mixer_sample(x, pool_prev, s0, weights, prev_out, layer, after, part=(0, 1), prev_y=None):
    b, l, _ = x.shape
    depth = s0.shape[0]
    ns = SAMPLE_SEQS
    k, n = part
    assert ns * l == CHUNK and b % (ns * n) == 0
    steps = b // ns // n
    off = k * steps
    cos2, sin2 = _rope_tables(PAST_LEN + np.arange(l))
    cos2, sin2 = np.tile(cos2, (ns, 1)), np.tile(sin2, (ns, 1))
    decay = _decay_consts(l, ns)
    seq3 = lambda i: (i + off, 0, 0)
    state_block = (ns, RET_HEADS, RET_HEAD_DIM, RET_HEAD_DIM)
    tail_block = (POOL_BUF, ns, POOL_WIDTH)
    has_prev = prev_out is not None
    assert has_prev or n == 1
    if has_prev:
        tail_out = pl.BlockSpec((None,) + tail_block, lambda i: (layer, 0, i + off, 0))
        state_out = pl.BlockSpec((None,) + state_block, lambda i: (layer, i + off, 0, 0, 0))
    else:
        tail_out = pl.BlockSpec((depth,) + tail_block, lambda i: (0, 0, i, 0))
        state_out = _chain_out_spec(state_block, depth, layer, False)
    aliased = [*(prev_out if has_prev else ()), *(() if prev_y is None else (prev_y,))]
    n_in = 16
    aliases = {n_in: 1, n_in + 1: 2} if has_prev else {}
    if prev_y is not None:
        aliases[n_in + 2] = 0
    y, tails, states = pl.pallas_call(
        functools.partial(_mixer_sample_kernel, seq_len=l, layer=layer, has_prev=has_prev,
                          n_alias=len(aliased)),
        grid=(steps,),
        in_specs=[pl.BlockSpec((ns, l, D_MODEL), seq3),
                  pl.BlockSpec((None,) + tail_block, lambda i: (layer, 0, i + off, 0)),
                  pl.BlockSpec((None,) + state_block, lambda i: (layer, i + off, 0, 0, 0)),
                  _const_spec((CHUNK, RET_HEAD_DIM)), _const_spec((CHUNK, RET_HEAD_DIM)),
                  *_mixer_weight_specs(layer, False), *_DECAY_SPECS, _ANY_SPEC,
                  *[_ANY_SPEC] * len(aliased)],
        out_specs=[pl.BlockSpec((ns, l, D_MODEL), seq3), tail_out, state_out],
        out_shape=[jax.ShapeDtypeStruct(x.shape, F32),
                   jax.ShapeDtypeStruct((depth, POOL_BUF, b, POOL_WIDTH), F32),
                   jax.ShapeDtypeStruct(s0.shape, F32)],
        input_output_aliases=aliases,
        scratch_shapes=[pltpu.VMEM((ns, POOL_HIST + l, POOL_WIDTH), F32),
                        pltpu.VMEM((CHUNK, RET_WIDTH), F32)],
        compiler_params=pltpu.CompilerParams(
            dimension_semantics=("arbitrary",), vmem_limit_bytes=VMEM_LIMIT),
        name="mixer_sample",
    )(x, pool_prev, s0, cos2, sin2, *weights, *decay, after, *aliased)
    return y, (tails, states)


def _router_weights(w_rg, b_rg, w_re, b_re):
    depth = w_rg.shape[0]
    gap = EXPERT_LANE0 - N_EXPERT_GROUPS
    rest = LANES - EXPERT_LANE0 - N_EXPERTS
    wr = jnp.concatenate([w_rg, jnp.zeros((depth, D_MODEL, gap), F32), w_re,
                          jnp.zeros((depth, D_MODEL, rest), F32)], axis=-1)
    br = jnp.concatenate([b_rg, jnp.zeros((depth, gap), F32), b_re, jnp.zeros((depth, rest), F32)],
                         axis=-1).reshape(depth, 1, LANES)
    wr_hi, wr_lo = _split_weight(wr)
    return jnp.concatenate([wr_hi, wr_lo], axis=-1), wr_hi, br


def _router_specs(layer):
    return [_layer_spec((None, D_MODEL, 2 * LANES), layer), _layer_spec((None, D_MODEL, LANES), layer),
            _layer_spec((None, 1, LANES), layer)]


def _sc_mesh():
    return plsc.VectorSubcoreMesh(core_axis_name="core", subcore_axis_name="subcore",
                                  num_cores=SC_CORES, num_subcores=SC_SUBCORES)


def _sc_params():
    params = pltpu.CompilerParams()
    if "needs_layout_passes" in pltpu.CompilerParams.__dataclass_fields__:
        params = dataclasses.replace(params, needs_layout_passes=False)
    return params


def _sc_gather(tables, idx, after):
    n = idx.shape[0]
    assert n % SC_WINDOW == 0
    nt = len(tables)

    def body(*refs):
        i_hbm = refs[nt]
        for t_hbm, o_hbm in zip(refs[:nt], refs[nt + 2:]):
            def gather_window(i_vmem, o_vmem, t_hbm=t_hbm):
                pltpu.sync_copy(t_hbm.at[i_vmem.at[0]], o_vmem)

            pltpu.emit_pipeline(
                gather_window, grid=(n // SC_WINDOW,),
                in_specs=[pl.BlockSpec((1, SC_WINDOW), lambda i: (0, i))],
                out_specs=[pl.BlockSpec((SC_WINDOW, t_hbm.shape[1]), lambda i: (i, 0))],
                core_axis_name=("core", "subcore"),
                dimension_semantics=(pltpu.PARALLEL,),
            )(i_hbm, o_hbm)

    out_type = tuple(jax.ShapeDtypeStruct((n, t.shape[1]), t.dtype) for t in tables)
    return pl.kernel(body, out_type=out_type, mesh=_sc_mesh(), name="sc_gather")(
        *tables, idx.reshape(1, n), after)


def _sc_slots(cls, rank, starts, n_slots):
    t = cls.shape[0]
    workers = SC_CORES * SC_SUBCORES
    slot_per, tok_per = n_slots // workers, t // workers
    assert n_slots % (workers * SC_LANES) == 0 and t % (workers * SC_LANES) == 0 and t & (t - 1) == 0
    assert n_slots % (SC_LANES * SC_UNROLL) == 0 and t % (SC_LANES * SC_UNROLL) == 0

    def body(cls_hbm, rank_hbm, starts_hbm, pos_hbm, slot_hbm, cls_v, rank_v, starts_v, pos_v, slot_v):
        wid = lax.axis_index("subcore") * SC_CORES + lax.axis_index("core")
        pltpu.sync_copy(cls_hbm, cls_v)
        pltpu.sync_copy(rank_hbm, rank_v)
        pltpu.sync_copy(starts_hbm, starts_v)

        lane = lax.iota(I32, SC_LANES)
        span = SC_LANES * SC_UNROLL

        @pl.loop(0, n_slots, step=span)
        def _(i):
            for u in range(SC_UNROLL):
                j = i + u * SC_LANES
                slot_v[pl.ds(j, SC_LANES)] = (lane + j) & (t - 1)

        @pl.loop(0, t, step=span)
        def _(i):
            for u in range(SC_UNROLL):
                j = i + u * SC_LANES
                at = pl.ds(j, SC_LANES)
                pos = plsc.load_gather(starts_v, [cls_v[at]]) + rank_v[at]
                pos_v[at] = pos
                plsc.store_scatter(slot_v, [pos], lane + j)

        tok_off = pl.multiple_of(wid * tok_per, SC_LANES)
        pltpu.sync_copy(pos_v.at[pl.ds(tok_off, tok_per)], pos_hbm.at[pl.ds(tok_off, tok_per)])
        slot_off = pl.multiple_of(wid * slot_per, SC_LANES)
        pltpu.sync_copy(slot_v.at[pl.ds(slot_off, slot_per)], slot_hbm.at[pl.ds(slot_off, slot_per)])

    return pl.kernel(
        body, mesh=_sc_mesh(), compiler_params=_sc_params(), name="sc_slots",
        out_type=(jax.ShapeDtypeStruct((t,), I32), jax.ShapeDtypeStruct((n_slots,), I32)),
        scratch_types=[pltpu.VMEM((t,), I32), pltpu.VMEM((t,), I32), pltpu.VMEM((CLASS_ROWS,), I32),
                       pltpu.VMEM((t,), I32), pltpu.VMEM((n_slots,), I32)],
    )(cls, rank, starts)


def _moe_dispatch(routed, after):
    ha, hbg, idx, counts = routed
    t = ha.shape[0]
    n_slots = t + N_CLASSES * PAIR_TILE
    n_tiles = n_slots // PAIR_TILE

    cnt = counts[:, 0].astype(I32)
    padded = (cnt + PAIR_TILE - 1) // PAIR_TILE * PAIR_TILE
    ends = jnp.cumsum(padded)
    starts = ends - padded
    tile_cls = jnp.minimum(
        jnp.sum(ends[None, :N_CLASSES] <= (jnp.arange(n_tiles, dtype=I32) * PAIR_TILE)[:, None], axis=1),
        N_CLASSES - 1).astype(I32)
    first = (tile_cls // PAIRS_PER_GROUP) * EXPERTS_PER_GROUP
    tile_ea = first + jnp.asarray(PAIR_LO, I32)[tile_cls % PAIRS_PER_GROUP]
    tile_eb = first + jnp.asarray(PAIR_HI, I32)[tile_cls % PAIRS_PER_GROUP]
    used = (ends[N_CLASSES - 1:N_CLASSES] // PAIR_TILE).astype(I32)

    pos, slot_tok = _sc_slots(idx[0], idx[1], starts, n_slots)
    hsa, hsbg = _sc_gather((ha, hbg), slot_tok, after=after)
    return (tile_ea, tile_eb, used, hsa, hsbg), pos, slot_tok


def _moe_pair(dispatched, wgu, wd, after):
    tile_ea, tile_eb, used, hsa, hsbg = dispatched
    n_slots = hsa.shape[0]
    n_tiles = n_slots // PAIR_TILE
    row = lambda i, ea, eb, nu: (jnp.minimum(i, nu[0] - 1), 0)
    w_spec = lambda shape, which: pl.BlockSpec(
        (1,) + shape, lambda i, ea, eb, nu: ((ea, eb)[which][i], 0, 0))
    gu_shape, d_shape = (D_MODEL, 2 * D_EXPERT), (D_EXPERT, D_MODEL)
    ysa, ysb = pl.pallas_call(
        _moe_pair_kernel,
        grid_spec=pltpu.PrefetchScalarGridSpec(
            num_scalar_prefetch=3, grid=(n_tiles,),
            in_specs=[pl.BlockSpec((PAIR_TILE, PACK_W), row),
                      pl.BlockSpec((PAIR_TILE, PACK_W + LANES), row),
                      w_spec(gu_shape, 0), w_spec(d_shape, 0), w_spec(gu_shape, 1), w_spec(d_shape, 1),
                      _ANY_SPEC],
            out_specs=[pl.BlockSpec((PAIR_TILE, PACK_W), row), pl.BlockSpec((PAIR_TILE, PACK_W), row)]),
        out_shape=[jax.ShapeDtypeStruct((n_slots, PACK_W), I32),
                   jax.ShapeDtypeStruct((n_slots, PACK_W), I32)],
        compiler_params=pltpu.CompilerParams(
            dimension_semantics=("arbitrary",), vmem_limit_bytes=VMEM_LIMIT),
        name="moe_pair",
    )(tile_ea, tile_eb, used, hsa, hsbg, wgu, wd, wgu, wd, after)
    return ysa, ysb


def _final_norm(x, moe_y, g, rows):
    t = x.shape[0]
    rows = min(rows, t)
    assert t % rows == 0
    tok = lambda i: (i, 0)
    return pl.pallas_call(
        _final_norm_kernel,
        grid=(t // rows,),
        in_specs=[pl.BlockSpec((rows, D_MODEL), tok), pl.BlockSpec((rows, PACK_W), tok),
                  pl.BlockSpec((rows, PACK_W), tok), _const_spec((1, D_MODEL))],
        out_specs=pl.BlockSpec((rows, D_MODEL), tok),
        out_shape=jax.ShapeDtypeStruct(x.shape, F32),
        compiler_params=pltpu.CompilerParams(
            dimension_semantics=("arbitrary",), vmem_limit_bytes=VMEM_LIMIT),
        name="final_norm",
    )(x, *moe_y, g.reshape(1, D_MODEL))


def _moe_dense(x, norm_g, router, w_gate, w_up, w_down, norm_final, layer, final_norm, rows, after,
               experts=(0, N_EXPERTS), chain=None):
    t = x.shape[0]
    assert t == rows
    e0, e1 = experts
    first, last = chain is None, e1 == N_EXPERTS
    assert first == (e0 == 0)
    tok = lambda i, e: (i, 0)
    w_spec = lambda shape: pl.BlockSpec((None, 1) + shape, lambda i, e: (layer, e0 + e, 0, 0))
    w_out = lambda shape: pl.BlockSpec((1,) + shape, lambda i, e: (e0 + e, 0, 0))
    n_in = 10
    return pl.pallas_call(
        functools.partial(_moe_dense_kernel, final_norm=final_norm, e0=e0, first=first, last=last),
        grid=(t // rows, e1 - e0),
        in_specs=[pl.BlockSpec((rows, D_MODEL), tok), _layer_spec((None, 1, D_MODEL), layer),
                  *_router_specs(layer),
                  w_spec((D_MODEL, D_EXPERT)), w_spec((D_MODEL, D_EXPERT)), w_spec((D_EXPERT, D_MODEL)),
                  _const_spec((1, D_MODEL)), _ANY_SPEC,
                  *([] if first else [pl.BlockSpec((rows, D_MODEL), tok), _ANY_SPEC, _ANY_SPEC])],
        out_specs=[pl.BlockSpec((rows, D_MODEL), tok), w_out((D_MODEL, 2 * D_EXPERT)),
                   w_out((D_EXPERT, D_MODEL))],
        out_shape=[jax.ShapeDtypeStruct(x.shape, F32),
                   jax.ShapeDtypeStruct((N_EXPERTS, D_MODEL, 2 * D_EXPERT), BF16),
                   jax.ShapeDtypeStruct((N_EXPERTS, D_EXPERT, D_MODEL), BF16)],
        input_output_aliases={} if first else {n_in + 1: 1, n_in + 2: 2},
        scratch_shapes=[pltpu.VMEM((rows, D_MODEL), BF16),
                        pltpu.VMEM((rows, LANES), F32),
                        pltpu.VMEM((rows, D_MODEL), F32)],
        compiler_params=pltpu.CompilerParams(
            dimension_semantics=("arbitrary", "arbitrary"), vmem_limit_bytes=VMEM_LIMIT),
        name="moe_dense",
    )(x, norm_g, *router, w_gate, w_up, w_down, norm_final.reshape(1, D_MODEL), after,
      *(() if first else chain))


def kernel(x_prompt, x_sample, cache_pool, state_ret, norm_mix, w_in, w_pool, pool_scale, ret_gn, w_out, norm_ffn, w_router_group, b_router_group, w_router_expert, b_router_expert, w_gate, w_up, w_down, norm_final):
    depth = norm_mix.shape[0]
    row = lambda a: a.reshape(depth, 1, a.shape[-1])
    mix_split = (row(norm_mix), *_split_weight(w_in), *_split_weight(w_pool), row(pool_scale),
                 row(ret_gn), *_split_weight(w_out))
    mix_hi = tuple(mix_split[i] for i in (0, 1, 3, 5, 6, 7))
    router = _router_weights(w_router_group, b_router_group, w_router_expert, b_router_expert)
    norm_ffn = row(norm_ffn)
    pool_prev = jnp.swapaxes(cache_pool, 1, 2)

    yp, ys = x_prompt, x_sample
    moe_p = None
    out_p = out_s = None
    ys_head = None
    for l in range(depth):
        yp, out_p, routed = _mixer_prompt(
            yp, moe_p, mix_split, (norm_ffn, *router), out_p, l, rows=512,
            precise_tail=PRECISE_TAIL_STEPS if l < depth - 1 else 0,
            after=ys if ys_head is None else ys_head)
        ys_mix, out_s = _mixer_sample(ys, pool_prev, state_ret, mix_hi, out_s, l, after=routed[-1],
                                      part=(1, SAMPLE_PARTS) if l else (0, 1), prev_y=ys_head)
        dispatched, pos, slot_tok = _moe_dispatch(routed, after=ys_mix)
        ys, wgu, wd = _moe_dense(ys_mix.reshape(-1, D_MODEL), norm_ffn, router, w_gate, w_up, w_down,
                                 norm_final, l, l == depth - 1, rows=ys_mix.shape[0] * ys_mix.shape[1],
                                 after=slot_tok)
        ys = ys.reshape(x_sample.shape)
        sorted_y = _moe_pair(dispatched, wgu, wd, after=ys)
        moe_p = _sc_gather(sorted_y, pos, after=routed[-1])
        if l + 1 < depth:
            ys_head, out_s = _mixer_sample(ys, pool_prev, state_ret, mix_hi, out_s, l + 1,
                                           after=sorted_y[0], part=(0, SAMPLE_PARTS))
    yp = _final_norm(yp.reshape(-1, D_MODEL), moe_p, norm_final, rows=2048).reshape(yp.shape)
    return (yp, ys, *out_p, jnp.swapaxes(out_s[0], 1, 2), out_s[1])
```

```python
import dataclasses
import functools
import itertools

import jax
import jax.numpy as jnp
import numpy as np
from jax import lax
from jax.experimental import pallas as pl
from jax.experimental.pallas import tpu as pltpu
from jax.experimental.pallas import tpu_sc as plsc

F32 = jnp.float32
BF16 = jnp.bfloat16
I32 = jnp.int32

D_MODEL = 1024
POOL_WIDTH = 512
POOL_WINDOWS = (2, 4, 8, 16)
POOL_GROUP_DIM = 128
POOL_BUF = 15
POOL_HIST = 16
RET_WIDTH = 512
RET_HEADS = 4
RET_HEAD_DIM = 128
RET_CHUNK = 128
ROPE_BASE = 10000.0
IN_WIDTH = POOL_WIDTH + 4 * RET_WIDTH
N_EXPERT_GROUPS = 4
EXPERTS_PER_GROUP = 4
N_EXPERTS = N_EXPERT_GROUPS * EXPERTS_PER_GROUP
D_EXPERT = 256
RMS_EPS = 1e-6
GN_EPS = 1e-5
PAST_LEN = 16384

LANES = 128
SUBLANES = 8
EXPERT_LANE0 = 8
GROUP_SHIFT = EXPERTS_PER_GROUP.bit_length() - 1
PAIRS = tuple(itertools.combinations(range(EXPERTS_PER_GROUP), 2))
PAIRS_PER_GROUP = len(PAIRS)
PAIR_LO, PAIR_HI = zip(*PAIRS)
N_CLASSES = N_EXPERT_GROUPS * PAIRS_PER_GROUP
CLASS_ROWS = 32
GATE_LO, GATE_HI = 0, 1
COUNT_BLOCK = 256
PAIR_TILE = 256
PACK_W = D_MODEL // 4
SC_CORES, SC_SUBCORES, SC_LANES = 2, 16, 16
SC_WINDOW = 128
SC_UNROLL = 8
PRECISE_TAIL_STEPS = 1
CHUNK = 128
SAMPLE_SEQS = 16
SAMPLE_HEAD = 0.625
VMEM_LIMIT = 56 * 1024 * 1024


def _dot(a, b):
    return jnp.dot(a, b, preferred_element_type=F32)


def _dot_nt(a, b):
    return lax.dot_general(a, b, (((1,), (1,)), ((), ())), preferred_element_type=F32)


def _bf16_round_bits(u):
    return (u + jnp.uint32(0x7FFF) + ((u >> 16) & jnp.uint32(1))) & jnp.uint32(0xFFFF0000)


def _split(a):
    hi = pltpu.bitcast(_bf16_round_bits(pltpu.bitcast(a, jnp.uint32)), F32)
    return hi.astype(BF16), (a - hi).astype(BF16)


def _mm(a, b, precise, nt=False):
    dot = _dot_nt if nt else _dot
    if precise:
        b_hi, b_lo = b if isinstance(b, tuple) else _split(b)
        a_hi, a_lo = _split(a)
        return dot(a_hi, b_hi) + dot(a_lo, b_hi) + dot(a_hi, b_lo)
    return dot(a.astype(BF16), b[0] if isinstance(b, tuple) else b.astype(BF16))


def _rmsnorm(x, g):
    ms = jnp.mean(x * x, axis=-1, keepdims=True)
    return x * lax.rsqrt(ms + RMS_EPS) * g


def _pool_mix(ubuf, rows, t_first, n_prev, wpool_refs, pscale, precise=False, row0=0):
    ns = ubuf.shape[0]
    t = t_first + lax.broadcasted_iota(I32, (1, rows, POOL_GROUP_DIM), 1)
    base = POOL_HIST + row0
    outs = []
    for j, w in enumerate(POOL_WINDOWS):
        lanes = slice(j * POOL_GROUP_DIM, (j + 1) * POOL_GROUP_DIM)
        uj = ubuf[:, base:base + rows, lanes]
        acc = uj
        for i in range(1, w):
            acc = acc + ubuf[:, base - i:base - i + rows, lanes]
        cnt = jnp.minimum(w, n_prev + t + 1).astype(F32)
        d = (acc / cnt - uj).reshape(ns * rows, POOL_GROUP_DIM)
        outs.append(_mm(d, tuple(w[j] for w in wpool_refs), precise))
    return jnp.concatenate(outs, axis=-1) * pscale


def _rope(xh, cos2, sin2):
    return xh * cos2 + pltpu.roll(xh, RET_HEAD_DIM // 2, 1) * sin2


def _group_norm(o):
    mu = jnp.mean(o, axis=-1, keepdims=True)
    c = o - mu
    var = jnp.mean(c * c, axis=-1, keepdims=True)
    return c * lax.rsqrt(var + GN_EPS)


def _silu(x):
    return x * (1.0 / (1.0 + jnp.exp(-x)))


def _head(a, h):
    return a[:, h * RET_HEAD_DIM:(h + 1) * RET_HEAD_DIM]


def _qkvg(z):
    p, r = POOL_WIDTH, RET_WIDTH
    return z[:, p:p + r], z[:, p + r:p + 2 * r], z[:, p + 2 * r:p + 3 * r], z[:, p + 3 * r:p + 4 * r]


def _pack_bf16_pair(a, b):
    ua = pltpu.bitcast(a.astype(BF16).astype(F32), jnp.uint32)
    ub = pltpu.bitcast(b.astype(BF16).astype(F32), jnp.uint32)
    return pltpu.bitcast((ua >> 16) | (ub & jnp.uint32(0xFFFF0000)), I32)


def _unpack_bf16_pair(w):
    u = pltpu.bitcast(w, jnp.uint32)
    return pltpu.bitcast(u << 16, F32), pltpu.bitcast(u & jnp.uint32(0xFFFF0000), F32)


def _pack_row(y):
    q = PACK_W
    return _pack_bf16_pair(y[:, 0:q], y[:, q:2 * q]), _pack_bf16_pair(y[:, 2 * q:3 * q], y[:, 3 * q:])


def _unpack_row(wa, wb):
    return jnp.concatenate([*_unpack_bf16_pair(wa), *_unpack_bf16_pair(wb)], axis=-1)


def _zero_other_layers(ref, layer):
    for j in range(ref.shape[0]):
        if j != layer:
            ref[j] = jnp.zeros(ref.shape[1:], ref.dtype)


def _mixer_prompt_kernel(*refs, rows, steps, moe_in, precise_tail, layer, has_prev):
    (x_ref, ya_ref, yb_ref, cos_ref, sin_ref, ng_ref, win_hi_ref, win_lo_ref, wpool_hi_ref,
     wpool_lo_ref, pscale_ref, gn_ref, wout_hi_ref, wout_lo_ref, dmat_ref, qdec_ref, kdec_ref,
     cdec_ref) = refs[:18]
    route_in = refs[18:22]
    y_ref, tail_ref, sfin_ref, *route_out, ubuf, s_scr, ret_scr, z_scr, count_scr = (
        refs[23 + 2 * has_prev:])
    c = pl.program_id(1)

    @pl.when((pl.program_id(0) == 0) & (c == 0))
    def _():
        count_scr[...] = jnp.zeros_like(count_scr)
    if not has_prev:
        _zero_other_layers(tail_ref, layer)
        _zero_other_layers(sfin_ref, layer)
        tail_ref, sfin_ref = tail_ref.at[layer], sfin_ref.at[layer]

    @pl.when(c == 0)
    def _():
        ubuf[:, 0:POOL_HIST, :] = jnp.zeros((1, POOL_HIST, POOL_WIDTH), F32)
        s_scr[...] = jnp.zeros_like(s_scr)

    kv_cols = slice(POOL_WIDTH + RET_WIDTH, POOL_WIDTH + 3 * RET_WIDTH)

    def step(kv_precise, full_from):
        x = x_ref[0]
        if moe_in:
            x = x + _unpack_row(ya_ref[0], yb_ref[0])
        hn = _rmsnorm(x, ng_ref[...])
        hi, lo = _split(hn) if kv_precise else (hn.astype(BF16), None)
        z_scr[...] = _dot(hi, win_hi_ref[...])
        if kv_precise and full_from:
            z_scr[:full_from, kv_cols] += (_dot(lo[:full_from], win_hi_ref[:, kv_cols])
                                           + _dot(hi[:full_from], win_lo_ref[:, kv_cols]))
        if full_from < rows:
            z_scr[full_from:, :] += (_dot(lo[full_from:], win_hi_ref[...])
                                     + _dot(hi[full_from:], win_lo_ref[...]))

        ubuf[0, POOL_HIST:POOL_HIST + rows, :] = z_scr[:, :POOL_WIDTH]
        pool_w = (wpool_hi_ref, wpool_lo_ref)
        pool_parts = []
        if full_from:
            pool_parts.append(_pool_mix(ubuf, full_from, c * rows, 0, pool_w, pscale_ref[...]))
        if full_from < rows:
            pool_parts.append(_pool_mix(ubuf, rows - full_from, c * rows + full_from, 0, pool_w,
                                        pscale_ref[...], precise=True, row0=full_from))
        pool_out = jnp.concatenate(pool_parts, axis=0)
        tail_ref[...] = ubuf[:, rows + POOL_HIST - POOL_BUF:rows + POOL_HIST, :]
        ubuf[:, 0:POOL_HIST, :] = ubuf[:, rows:rows + POOL_HIST, :]

        scale = RET_HEAD_DIM ** -0.5
        for ci in range(rows // CHUNK):
            rs = slice(ci * CHUNK, (ci + 1) * CHUNK)
            full = ci * CHUNK >= full_from
            cos2 = cos_ref[rs, :]
            sin2 = sin_ref[rs, :]
            for h in range(RET_HEADS):
                col = lambda part: slice(POOL_WIDTH + part * RET_WIDTH + h * RET_HEAD_DIM,
                                         POOL_WIDTH + part * RET_WIDTH + (h + 1) * RET_HEAD_DIM)
                q = _rope(z_scr[rs, col(0)], cos2, sin2)
                k = _rope(z_scr[rs, col(1)], cos2, sin2) * scale
                v = z_scr[rs, col(2)]
                s_old = s_scr[h]
                scores = _mm(q, k, full, nt=True) * dmat_ref[h]
                qd = q * _head(qdec_ref[...], h)
                o = _mm(scores, v, full) + _mm(qd, s_old, full)
                kd = k * _head(kdec_ref[...], h)
                s_scr[h] = s_old * cdec_ref[h] + _mm(kd.T, v, kv_precise)
                on = _group_norm(o) * _head(gn_ref[...], h)
                ret_scr[rs, h * RET_HEAD_DIM:(h + 1) * RET_HEAD_DIM] = _silu(z_scr[rs, col(3)]) * on

        mix_in = jnp.concatenate([pool_out, ret_scr[...]], axis=-1)
        y_ref[0] = x + _dot(mix_in.astype(BF16), wout_hi_ref[...])
        if full_from < rows:
            m_hi, m_lo = _split(mix_in[full_from:])
            y_ref[0, full_from:, :] += _dot(m_lo, wout_hi_ref[...]) + _dot(m_hi, wout_lo_ref[...])
        sfin_ref[0] = s_scr[...]
        _route_tile(y_ref[0], full_from, *route_in, *route_out, count_scr)

    if precise_tail:
        pl.when(c < steps - precise_tail)(lambda: step(False, rows))
        if precise_tail > 1:
            pl.when((c >= steps - precise_tail) & (c < steps - 1))(lambda: step(True, rows))
        pl.when(c == steps - 1)(lambda: step(True, rows - CHUNK))
    else:
        step(False, rows)


def _mixer_sample_kernel(*refs, seq_len, layer, has_prev, n_alias):
    (x_ref, prev_ref, s0_ref, cos_ref, sin_ref, ng_ref, win_ref, wpool_ref, pscale_ref, gn_ref,
     wout_ref, dmat_ref, qdec_ref, kdec_ref, cdec_ref) = refs[:15]
    y_ref, tail_ref, sfin_ref, ubuf, ret_scr = refs[16 + n_alias:]
    if not has_prev:
        _zero_other_layers(tail_ref, layer)
        _zero_other_layers(sfin_ref, layer)
        tail_ref, sfin_ref = tail_ref.at[layer], sfin_ref.at[layer]
    ns = SAMPLE_SEQS
    rows = ns * seq_len
    x = x_ref[...].reshape(rows, D_MODEL)
    hn = _rmsnorm(x, ng_ref[...]).astype(BF16)
    z = _dot(hn, win_ref[...])
    for j in range(POOL_BUF):
        ubuf[:, POOL_HIST - POOL_BUF + j, :] = prev_ref[j]
    ubuf[:, POOL_HIST:POOL_HIST + seq_len, :] = z[:, :POOL_WIDTH].reshape(ns, seq_len, POOL_WIDTH)
    pool_out = _pool_mix(ubuf, seq_len, 0, POOL_BUF, (wpool_ref,), pscale_ref[...])
    for j in range(POOL_BUF):
        tail_ref[j] = ubuf[:, seq_len + POOL_HIST - POOL_BUF + j, :]

    q_all, k_all, v_all, g_all = _qkvg(z)
    scale = RET_HEAD_DIM ** -0.5
    cos2 = cos_ref[...]
    sin2 = sin_ref[...]
    tok_seq = lax.broadcasted_iota(I32, (RET_HEAD_DIM, rows), 1) // seq_len
    for h in range(RET_HEADS):
        q = _rope(_head(q_all, h), cos2, sin2)
        k = _rope(_head(k_all, h), cos2, sin2) * scale
        vb = _head(v_all, h).astype(BF16)
        s_old = s0_ref[:, h]
        scores = _dot_nt(q.astype(BF16), k.astype(BF16)) * dmat_ref[h]
        qd = (q * _head(qdec_ref[...], h)).astype(BF16).reshape(ns, seq_len, RET_HEAD_DIM)
        o_state = jnp.einsum('bid,bde->bie', qd, s_old.astype(BF16), preferred_element_type=F32)
        o = _dot(scores.astype(BF16), vb) + o_state.reshape(rows, RET_HEAD_DIM)
        kdt = (k * _head(kdec_ref[...], h)).T
        lhs = jnp.concatenate(
            [jnp.where(tok_seq == b, kdt, 0.0).astype(BF16) for b in range(ns)], axis=0)
        upd = _dot(lhs, vb).reshape(ns, RET_HEAD_DIM, RET_HEAD_DIM)
        sfin_ref[:, h] = s_old * cdec_ref[h] + upd
        on = _group_norm(o) * _head(gn_ref[...], h)
        ret_scr[:, h * RET_HEAD_DIM:(h + 1) * RET_HEAD_DIM] = _silu(_head(g_all, h)) * on

    mix_in = jnp.concatenate([pool_out, ret_scr[...]], axis=-1).astype(BF16)
    y_ref[...] = (x + _dot(mix_in, wout_ref[...])).reshape(ns, seq_len, D_MODEL)


def _router_logits(hn, wr_cat_ref, wr_hi_ref, br_ref, precise_from=0):
    parts = []
    if precise_from:
        parts.append(_dot(hn[:precise_from].astype(BF16), wr_hi_ref[...]))
    if precise_from < hn.shape[0]:
        h_hi, h_lo = _split(hn[precise_from:])
        part = _dot(h_hi, wr_cat_ref[...])
        parts.append(part[:, :LANES] + part[:, LANES:] + _dot(h_lo, wr_hi_ref[...]))
    return jnp.concatenate(parts, axis=0) + br_ref[...]


def _select(group_lg, expert_lg, gidx, eidx, axis):
    red = dict(axis=axis, keepdims=True)
    neg = jnp.float32(-jnp.inf)
    gmax = jnp.max(group_lg, **red)
    g_sel = jnp.min(jnp.where(group_lg == gmax, gidx, N_EXPERT_GROUPS), **red)
    p_sel = 1.0 / jnp.sum(jnp.exp(group_lg - gmax), **red)
    emask = (eidx >> GROUP_SHIFT) == g_sel
    v1 = jnp.max(jnp.where(emask, expert_lg, neg), **red)
    i1 = jnp.min(jnp.where(emask & (expert_lg == v1), eidx, N_EXPERTS), **red)
    emask2 = emask & (eidx != i1)
    v2 = jnp.max(jnp.where(emask2, expert_lg, neg), **red)
    i2 = jnp.min(jnp.where(emask2 & (expert_lg == v2), eidx, N_EXPERTS), **red)
    e2 = jnp.exp(v2 - v1)
    return g_sel, i1, i2, p_sel / (1.0 + e2), p_sel * e2 / (1.0 + e2)


def _route_tile(x, precise_from, ng_ref, wr_cat_ref, wr_hi_ref, br_ref, ha_ref, hbg_ref, idx_ref,
                count_ref, carry_scr):
    rows = x.shape[0]
    hn = _rmsnorm(x, ng_ref[...])
    ha_ref[...], hbg_ref[:, :PACK_W] = _pack_row(hn)
    lgt = _router_logits(hn, wr_cat_ref, wr_hi_ref, br_ref, precise_from).T
    neg = jnp.float32(-jnp.inf)
    gidx = lax.broadcasted_iota(I32, (SUBLANES, rows), 0)
    eidx = lax.broadcasted_iota(I32, (N_EXPERTS, rows), 0)
    group_lg = jnp.where(gidx < N_EXPERT_GROUPS, lgt[0:SUBLANES], neg)
    expert_lg = lgt[EXPERT_LANE0:EXPERT_LANE0 + N_EXPERTS]
    g_sel, i1, i2, w1, w2 = _select(group_lg, expert_lg, gidx, eidx, 0)
    lo = jnp.minimum(i1, i2) - g_sel * EXPERTS_PER_GROUP
    hi = jnp.maximum(i1, i2) - g_sel * EXPERTS_PER_GROUP
    pair = ((lo * (2 * EXPERTS_PER_GROUP - 1 - lo)) >> 1) + (hi - lo - 1)
    cls = g_sel * PAIRS_PER_GROUP + pair
    w_lo = jnp.where(i1 < i2, w1, w2)
    w_hi = jnp.where(i1 < i2, w2, w1)

    crow = lax.broadcasted_iota(I32, (CLASS_ROWS, rows), 0)
    onehot = jnp.where(crow == cls, 1.0, 0.0)
    n_blk = rows // COUNT_BLOCK
    blocks = [onehot[:, j * COUNT_BLOCK:(j + 1) * COUNT_BLOCK] for j in range(n_blk)]
    r = lax.broadcasted_iota(I32, (COUNT_BLOCK, COUNT_BLOCK), 0)
    c = lax.broadcasted_iota(I32, (COUNT_BLOCK, COUNT_BLOCK), 1)
    upper = jnp.where(r < c, 1.0, 0.0).astype(BF16)
    within = _dot(jnp.concatenate(blocks, axis=0).astype(BF16), upper)
    carry = carry_scr[:, 0:1]
    ranks = []
    for j in range(n_blk):
        before = within[j * CLASS_ROWS:(j + 1) * CLASS_ROWS] + carry
        ranks.append(jnp.sum(blocks[j] * before, axis=0, keepdims=True))
        carry = carry + jnp.sum(blocks[j], axis=1, keepdims=True)
    rank = jnp.concatenate(ranks, axis=1)
    carry_scr[...] = jnp.broadcast_to(carry, carry_scr.shape)
    count_ref[...] = carry_scr[...]

    row8 = lax.broadcasted_iota(I32, (SUBLANES, rows), 0)
    idx_ref[...] = jnp.where(row8 == 0, cls, jnp.where(row8 == 1, rank.astype(I32), 0))
    rowl = lax.broadcasted_iota(I32, (LANES, rows), 0)
    gate_rec = jnp.where(rowl == GATE_LO, w_lo, jnp.where(rowl == GATE_HI, w_hi, 0.0)).T
    hbg_ref[:, PACK_W:] = pltpu.bitcast(gate_rec, I32)


def _moe_pair_kernel(ea_ref, eb_ref, used_ref, ha_ref, hbg_ref, wgu_a_ref, wd_a_ref,
                     wgu_b_ref, wd_b_ref, after_ref, ya_ref, yb_ref):
    del ea_ref, eb_ref, after_ref

    @pl.when(pl.program_id(0) < used_ref[0])
    def _():
        h = _unpack_row(ha_ref[...], hbg_ref[:, :PACK_W]).astype(BF16)
        gates = pltpu.bitcast(hbg_ref[:, PACK_W:], F32)
        y = None
        for wgu_ref, wd_ref, lane in ((wgu_a_ref, wd_a_ref, GATE_LO), (wgu_b_ref, wd_b_ref, GATE_HI)):
            gu = _dot(h, wgu_ref[0])
            act = _silu(gu[:, :D_EXPERT]) * gu[:, D_EXPERT:]
            part = _dot((act * gates[:, lane:lane + 1]).astype(BF16), wd_ref[0])
            y = part if y is None else y + part
        ya_ref[...], yb_ref[...] = _pack_row(y)


def _final_norm_kernel(x_ref, ya_ref, yb_ref, g_ref, o_ref):
    o_ref[...] = _rmsnorm(x_ref[...] + _unpack_row(ya_ref[...], yb_ref[...]), g_ref[...])


def _moe_dense_kernel(*refs, final_norm, e0, first, last):
    x_ref, ng_ref, wr_cat_ref, wr_hi_ref, br_ref, wg_ref, wu_ref, wdn_ref, nf_ref = refs[:9]
    acc_in_ref = None if first else refs[10]
    y_ref, wgu_ref, wd_ref, h_scr, gate_scr, acc_scr = refs[10 + 3 * (not first):]
    e = pl.program_id(1)

    @pl.when(e == 0)
    def _():
        hn = _rmsnorm(x_ref[...], ng_ref[...])
        h_scr[...] = hn.astype(BF16)
        lg = _router_logits(hn, wr_cat_ref, wr_hi_ref, br_ref)
        lane = lax.broadcasted_iota(I32, lg.shape, 1)
        neg = jnp.float32(-jnp.inf)
        group_lg = jnp.where(lane < N_EXPERT_GROUPS, lg, neg)
        is_expert = (lane >= EXPERT_LANE0) & (lane < EXPERT_LANE0 + N_EXPERTS)
        _, i1, i2, w1, w2 = _select(group_lg, jnp.where(is_expert, lg, neg), lane,
                                    jnp.where(is_expert, lane - EXPERT_LANE0, N_EXPERTS), 1)
        gate_scr[...] = (jnp.where(lane == i1 + EXPERT_LANE0, w1, 0.0)
                         + jnp.where(lane == i2 + EXPERT_LANE0, w2, 0.0))
        acc_scr[...] = jnp.zeros_like(acc_scr) if first else acc_in_ref[...]

    lane = lax.broadcasted_iota(I32, gate_scr.shape, 1)
    gate = jnp.sum(jnp.where(lane == EXPERT_LANE0 + e0 + e, gate_scr[...], 0.0), axis=-1, keepdims=True)
    wgu_ref[0, :, :D_EXPERT] = wg_ref[0].astype(BF16)
    wgu_ref[0, :, D_EXPERT:] = wu_ref[0].astype(BF16)
    wd_ref[0] = wdn_ref[0].astype(BF16)
    gu = _dot(h_scr[...], wgu_ref[0])
    act = _silu(gu[:, :D_EXPERT]) * gu[:, D_EXPERT:]
    acc_scr[...] += _dot((act * gate).astype(BF16), wd_ref[0])

    @pl.when(e == pl.num_programs(1) - 1)
    def _():
        if not last:
            y_ref[...] = acc_scr[...]
            return
        y = x_ref[...] + acc_scr[...]
        if final_norm:
            y = _rmsnorm(y, nf_ref[...])
        y_ref[...] = y


def _const_spec(shape):
    return pl.BlockSpec(shape, lambda *_: (0,) * len(shape))


_ANY_SPEC = pl.BlockSpec(memory_space=pl.ANY)


def _rope_tables(pos):
    half = RET_HEAD_DIM // 2
    inv = np.float32(ROPE_BASE) ** (-np.arange(half, dtype=np.float32) / np.float32(half))
    ang = pos.astype(np.float32)[:, None] * inv[None, :]
    cos, sin = np.cos(ang), np.sin(ang)
    return np.concatenate([cos, cos], -1), np.concatenate([-sin, sin], -1)


def _decay_consts(c, reps):
    f = np.float32
    log_g = np.log(f(1.0) - f(2.0) ** (f(-5.0) - np.arange(RET_HEADS, dtype=f)))
    i = np.arange(c, dtype=f)
    diff = i[:, None] - i[None, :]
    dmat = np.where(diff[None] >= 0, np.exp(np.maximum(diff, f(0))[None] * log_g[:, None, None]), f(0))
    qdec = np.exp((i + f(1))[None, :] * log_g[:, None])
    kdec = np.exp((f(c) - f(1) - i)[None, :] * log_g[:, None])
    cdec = np.exp(f(c) * log_g)
    dmat = np.einsum('ab,hij->haibj', np.eye(reps, dtype=f), dmat).reshape(RET_HEADS, reps * c, reps * c)
    lanes = lambda t: np.repeat(np.tile(t, (1, reps)).T, RET_HEAD_DIM, axis=1)
    cdec = np.broadcast_to(cdec[:, None, None], (RET_HEADS, 1, RET_HEAD_DIM))
    return tuple(np.ascontiguousarray(a, dtype=f) for a in (dmat, lanes(qdec), lanes(kdec), cdec))


def _split_weight(w):
    hi = lax.bitcast_convert_type(_bf16_round_bits(lax.bitcast_convert_type(w, jnp.uint32)), F32)
    return hi.astype(BF16), (w - hi).astype(BF16)


_W_IN_BLOCK = (None, D_MODEL, IN_WIDTH)
_W_POOL_BLOCK = (None, len(POOL_WINDOWS), POOL_GROUP_DIM, POOL_GROUP_DIM)
_W_OUT_BLOCK = (None, D_MODEL, D_MODEL)
_DECAY_SPECS = [
    _const_spec((RET_HEADS, CHUNK, CHUNK)), _const_spec((CHUNK, RET_WIDTH)),
    _const_spec((CHUNK, RET_WIDTH)), _const_spec((RET_HEADS, 1, RET_HEAD_DIM)),
]


def _layer_spec(block, layer, resident=False):
    index_map = lambda *_: (layer,) + (0,) * (len(block) - 1)
    if resident:
        return pl.BlockSpec(block, index_map, pipeline_mode=pl.Buffered(1))
    return pl.BlockSpec(block, index_map)


def _mixer_weight_specs(layer, split):
    n = 2 if split else 1
    row = lambda width: _layer_spec((None, 1, width), layer)
    return [row(D_MODEL), *[_layer_spec(_W_IN_BLOCK, layer, True)] * n,
            *[_layer_spec(_W_POOL_BLOCK, layer)] * n, row(POOL_WIDTH), row(RET_WIDTH),
            *[_layer_spec(_W_OUT_BLOCK, layer, True)] * n]


def _chain_out_spec(block, depth, layer, has_prev):
    zeros = (0,) * (len(block) - 1)
    if has_prev:
        return pl.BlockSpec((None,) + block, lambda i, *_: (layer, i) + zeros)
    return pl.BlockSpec((depth,) + block, lambda i, *_: (0, i) + zeros)


def _mixer_prompt(x, moe_y, weights, route_w, prev_out, layer, rows, precise_tail, after):
    b, l, _ = x.shape
    t = b * l
    depth = weights[0].shape[0]
    assert l % rows == 0 and rows % CHUNK == 0
    cos2, sin2 = _rope_tables(np.arange(l))
    decay = _decay_consts(RET_CHUNK, 1)
    steps = l // rows
    tok = lambda i, c: (i, c, 0)
    if moe_y is None:
        ya = yb = jnp.zeros((1, rows, PACK_W), I32)
        y_spec = _const_spec((1, rows, PACK_W))
    else:
        ya, yb = (t.reshape(b, l, PACK_W) for t in moe_y)
        y_spec = pl.BlockSpec((1, rows, PACK_W), tok)
    has_prev = prev_out is not None
    n_in = 23
    flat = lambda i, c: (i * steps + c, 0)
    y, tails, states, *routed = pl.pallas_call(
        functools.partial(_mixer_prompt_kernel, rows=rows, steps=steps, moe_in=moe_y is not None,
                          precise_tail=precise_tail, layer=layer, has_prev=has_prev),
        grid=(b, steps),
        in_specs=[pl.BlockSpec((1, rows, D_MODEL), tok), y_spec, y_spec,
                  pl.BlockSpec((rows, RET_HEAD_DIM), lambda i, c: (c, 0)),
                  pl.BlockSpec((rows, RET_HEAD_DIM), lambda i, c: (c, 0)),
                  *_mixer_weight_specs(layer, True), *_DECAY_SPECS,
                  _layer_spec((None, 1, D_MODEL), layer), *_router_specs(layer), _ANY_SPEC,
                  *([_ANY_SPEC, _ANY_SPEC] if has_prev else [])],
        out_specs=[pl.BlockSpec((1, rows, D_MODEL), tok),
                   _chain_out_spec((1, POOL_BUF, POOL_WIDTH), depth, layer, has_prev),
                   _chain_out_spec((1, RET_HEADS, RET_HEAD_DIM, RET_HEAD_DIM), depth, layer, has_prev),
                   pl.BlockSpec((rows, PACK_W), flat), pl.BlockSpec((rows, PACK_W + LANES), flat),
                   pl.BlockSpec((SUBLANES, rows), lambda i, c: (0, i * steps + c)),
                   _const_spec((CLASS_ROWS, LANES))],
        out_shape=[jax.ShapeDtypeStruct(x.shape, F32),
                   jax.ShapeDtypeStruct((depth, b, POOL_BUF, POOL_WIDTH), F32),
                   jax.ShapeDtypeStruct((depth, b, RET_HEADS, RET_HEAD_DIM, RET_HEAD_DIM), F32),
                   jax.ShapeDtypeStruct((t, PACK_W), I32),
                   jax.ShapeDtypeStruct((t, PACK_W + LANES), I32), jax.ShapeDtypeStruct((SUBLANES, t), I32),
                   jax.ShapeDtypeStruct((CLASS_ROWS, LANES), F32)],
        input_output_aliases={n_in: 1, n_in + 1: 2} if has_prev else {},
        scratch_shapes=[pltpu.VMEM((1, POOL_HIST + rows, POOL_WIDTH), F32),
                        pltpu.VMEM((RET_HEADS, RET_HEAD_DIM, RET_HEAD_DIM), F32),
                        pltpu.VMEM((rows, RET_WIDTH), F32),
                        pltpu.VMEM((rows, IN_WIDTH), F32),
                        pltpu.VMEM((CLASS_ROWS, LANES), F32)],
        compiler_params=pltpu.CompilerParams(
            dimension_semantics=("arbitrary", "arbitrary"), vmem_limit_bytes=VMEM_LIMIT),
        name="mixer_prompt",
    )(x, ya, yb, cos2, sin2, *weights, *decay, *route_w, after, *(prev_out if has_prev else ()))
    return y, (tails, states), tuple(routed)


def _mixer_sample(x, pool_prev, s0, weights, prev_out, layer, after, part=None, prev_y=None):
    b, l, _ = x.shape
    depth = s0.shape[0]
    ns = SAMPLE_SEQS
    assert ns * l == CHUNK and b % ns == 0
    off, end = part or (0, b // ns)
    assert 0 <= off < end <= b // ns
    steps = end - off
    n = 1 if steps == b // ns else 2
    cos2, sin2 = _rope_tables(PAST_LEN + np.arange(l))
    cos2, sin2 = np.tile(cos2, (ns, 1)), np.tile(sin2, (ns, 1))
    decay = _decay_consts(l, ns)
    seq3 = lambda i: (i + off, 0, 0)
    state_block = (ns, RET_HEADS, RET_HEAD_DIM, RET_HEAD_DIM)
    tail_block = (POOL_BUF, ns, POOL_WIDTH)
    has_prev = prev_out is not None
    assert has_prev or n == 1
    if has_prev:
        tail_out = pl.BlockSpec((None,) + tail_block, lambda i: (layer, 0, i + off, 0))
        state_out = pl.BlockSpec((None,) + state_block, lambda i: (layer, i + off, 0, 0, 0))
    else:
        tail_out = pl.BlockSpec((depth,) + tail_block, lambda i: (0, 0, i, 0))
        state_out = _chain_out_spec(state_block, depth, layer, False)
    aliased = [*(prev_out if has_prev else ()), *(() if prev_y is None else (prev_y,))]
    n_in = 16
    aliases = {n_in: 1, n_in + 1: 2} if has_prev else {}
    if prev_y is not None:
        aliases[n_in + 2] = 0
    y, tails, states = pl.pallas_call(
        functools.partial(_mixer_sample_kernel, seq_len=l, layer=layer, has_prev=has_prev,
                          n_alias=len(aliased)),
        grid=(steps,),
        in_specs=[pl.BlockSpec((ns, l, D_MODEL), seq3),
                  pl.BlockSpec((None,) + tail_block, lambda i: (layer, 0, i + off, 0)),
                  pl.BlockSpec((None,) + state_block, lambda i: (layer, i + off, 0, 0, 0)),
                  _const_spec((CHUNK, RET_HEAD_DIM)), _const_spec((CHUNK, RET_HEAD_DIM)),
                  *_mixer_weight_specs(layer, False), *_DECAY_SPECS, _ANY_SPEC,
                  *[_ANY_SPEC] * len(aliased)],
        out_specs=[pl.BlockSpec((ns, l, D_MODEL), seq3), tail_out, state_out],
        out_shape=[jax.ShapeDtypeStruct(x.shape, F32),
                   jax.ShapeDtypeStruct((depth, POOL_BUF, b, POOL_WIDTH), F32),
                   jax.ShapeDtypeStruct(s0.shape, F32)],
        input_output_aliases=aliases,
        scratch_shapes=[pltpu.VMEM((ns, POOL_HIST + l, POOL_WIDTH), F32),
                        pltpu.VMEM((CHUNK, RET_WIDTH), F32)],
        compiler_params=pltpu.CompilerParams(
            dimension_semantics=("arbitrary",), vmem_limit_bytes=VMEM_LIMIT),
        name="mixer_sample",
    )(x, pool_prev, s0, cos2, sin2, *weights, *decay, after, *aliased)
    return y, (tails, states)


def _router_weights(w_rg, b_rg, w_re, b_re):
    depth = w_rg.shape[0]
    gap = EXPERT_LANE0 - N_EXPERT_GROUPS
    rest = LANES - EXPERT_LANE0 - N_EXPERTS
    wr = jnp.concatenate([w_rg, jnp.zeros((depth, D_MODEL, gap), F32), w_re,
                          jnp.zeros((depth, D_MODEL, rest), F32)], axis=-1)
    br = jnp.concatenate([b_rg, jnp.zeros((depth, gap), F32), b_re, jnp.zeros((depth, rest), F32)],
                         axis=-1).reshape(depth, 1, LANES)
    wr_hi, wr_lo = _split_weight(wr)
    return jnp.concatenate([wr_hi, wr_lo], axis=-1), wr_hi, br


def _router_specs(layer):
    return [_layer_spec((None, D_MODEL, 2 * LANES), layer), _layer_spec((None, D_MODEL, LANES), layer),
            _layer_spec((None, 1, LANES), layer)]


def _sc_mesh():
    return plsc.VectorSubcoreMesh(core_axis_name="core", subcore_axis_name="subcore",
                                  num_cores=SC_CORES, num_subcores=SC_SUBCORES)


def _sc_params():
    params = pltpu.CompilerParams()
    if "needs_layout_passes" in pltpu.CompilerParams.__dataclass_fields__:
        params = dataclasses.replace(params, needs_layout_passes=False)
    return params


def _sc_gather(tables, idx, after):
    n = idx.shape[0]
    assert n % SC_WINDOW == 0
    nt = len(tables)

    def body(*refs):
        i_hbm = refs[nt]
        for t_hbm, o_hbm in zip(refs[:nt], refs[nt + 2:]):
            def gather_window(i_vmem, o_vmem, t_hbm=t_hbm):
                pltpu.sync_copy(t_hbm.at[i_vmem.at[0]], o_vmem)

            pltpu.emit_pipeline(
                gather_window, grid=(n // SC_WINDOW,),
                in_specs=[pl.BlockSpec((1, SC_WINDOW), lambda i: (0, i))],
                out_specs=[pl.BlockSpec((SC_WINDOW, t_hbm.shape[1]), lambda i: (i, 0))],
                core_axis_name=("core", "subcore"),
                dimension_semantics=(pltpu.PARALLEL,),
            )(i_hbm, o_hbm)

    out_type = tuple(jax.ShapeDtypeStruct((n, t.shape[1]), t.dtype) for t in tables)
    return pl.kernel(body, out_type=out_type, mesh=_sc_mesh(), name="sc_gather")(
        *tables, idx.reshape(1, n), after)


def _sc_slots(cls, rank, starts, n_slots):
    t = cls.shape[0]
    workers = SC_CORES * SC_SUBCORES
    slot_per, tok_per = n_slots // workers, t // workers
    assert n_slots % (workers * SC_LANES) == 0 and t % (workers * SC_LANES) == 0 and t & (t - 1) == 0
    assert n_slots % (SC_LANES * SC_UNROLL) == 0 and t % (SC_LANES * SC_UNROLL) == 0

    def body(cls_hbm, rank_hbm, starts_hbm, pos_hbm, slot_hbm, cls_v, rank_v, starts_v, pos_v, slot_v):
        wid = lax.axis_index("subcore") * SC_CORES + lax.axis_index("core")
        pltpu.sync_copy(cls_hbm, cls_v)
        pltpu.sync_copy(rank_hbm, rank_v)
        pltpu.sync_copy(starts_hbm, starts_v)

        lane = lax.iota(I32, SC_LANES)
        span = SC_LANES * SC_UNROLL

        @pl.loop(0, n_slots, step=span)
        def _(i):
            for u in range(SC_UNROLL):
                j = i + u * SC_LANES
                slot_v[pl.ds(j, SC_LANES)] = (lane + j) & (t - 1)

        @pl.loop(0, t, step=span)
        def _(i):
            for u in range(SC_UNROLL):
                j = i + u * SC_LANES
                at = pl.ds(j, SC_LANES)
                pos = plsc.load_gather(starts_v, [cls_v[at]]) + rank_v[at]
                pos_v[at] = pos
                plsc.store_scatter(slot_v, [pos], lane + j)

        tok_off = pl.multiple_of(wid * tok_per, SC_LANES)
        pltpu.sync_copy(pos_v.at[pl.ds(tok_off, tok_per)], pos_hbm.at[pl.ds(tok_off, tok_per)])
        slot_off = pl.multiple_of(wid * slot_per, SC_LANES)
        pltpu.sync_copy(slot_v.at[pl.ds(slot_off, slot_per)], slot_hbm.at[pl.ds(slot_off, slot_per)])

    return pl.kernel(
        body, mesh=_sc_mesh(), compiler_params=_sc_params(), name="sc_slots",
        out_type=(jax.ShapeDtypeStruct((t,), I32), jax.ShapeDtypeStruct((n_slots,), I32)),
        scratch_types=[pltpu.VMEM((t,), I32), pltpu.VMEM((t,), I32), pltpu.VMEM((CLASS_ROWS,), I32),
                       pltpu.VMEM((t,), I32), pltpu.VMEM((n_slots,), I32)],
    )(cls, rank, starts)


def _moe_dispatch(routed, after):
    ha, hbg, idx, counts = routed
    t = ha.shape[0]
    n_slots = t + N_CLASSES * PAIR_TILE
    n_tiles = n_slots // PAIR_TILE

    cnt = counts[:, 0].astype(I32)
    padded = (cnt + PAIR_TILE - 1) // PAIR_TILE * PAIR_TILE
    ends = jnp.cumsum(padded)
    starts = ends - padded
    tile_cls = jnp.minimum(
        jnp.sum(ends[None, :N_CLASSES] <= (jnp.arange(n_tiles, dtype=I32) * PAIR_TILE)[:, None], axis=1),
        N_CLASSES - 1).astype(I32)
    first = (tile_cls // PAIRS_PER_GROUP) * EXPERTS_PER_GROUP
    tile_ea = first + jnp.asarray(PAIR_LO, I32)[tile_cls % PAIRS_PER_GROUP]
    tile_eb = first + jnp.asarray(PAIR_HI, I32)[tile_cls % PAIRS_PER_GROUP]
    used = (ends[N_CLASSES - 1:N_CLASSES] // PAIR_TILE).astype(I32)

    pos, slot_tok = _sc_slots(idx[0], idx[1], starts, n_slots)
    hsa, hsbg = _sc_gather((ha, hbg), slot_tok, after=after)
    return (tile_ea, tile_eb, used, hsa, hsbg), pos, slot_tok


def _moe_pair(dispatched, wgu, wd, after):
    tile_ea, tile_eb, used, hsa, hsbg = dispatched
    n_slots = hsa.shape[0]
    n_tiles = n_slots // PAIR_TILE
    row = lambda i, ea, eb, nu: (jnp.minimum(i, nu[0] - 1), 0)
    w_spec = lambda shape, which: pl.BlockSpec(
        (1,) + shape, lambda i, ea, eb, nu: ((ea, eb)[which][i], 0, 0))
    gu_shape, d_shape = (D_MODEL, 2 * D_EXPERT), (D_EXPERT, D_MODEL)
    ysa, ysb = pl.pallas_call(
        _moe_pair_kernel,
        grid_spec=pltpu.PrefetchScalarGridSpec(
            num_scalar_prefetch=3, grid=(n_tiles,),
            in_specs=[pl.BlockSpec((PAIR_TILE, PACK_W), row),
                      pl.BlockSpec((PAIR_TILE, PACK_W + LANES), row),
                      w_spec(gu_shape, 0), w_spec(d_shape, 0), w_spec(gu_shape, 1), w_spec(d_shape, 1),
                      _ANY_SPEC],
            out_specs=[pl.BlockSpec((PAIR_TILE, PACK_W), row), pl.BlockSpec((PAIR_TILE, PACK_W), row)]),
        out_shape=[jax.ShapeDtypeStruct((n_slots, PACK_W), I32),
                   jax.ShapeDtypeStruct((n_slots, PACK_W), I32)],
        compiler_params=pltpu.CompilerParams(
            dimension_semantics=("arbitrary",), vmem_limit_bytes=VMEM_LIMIT),
        name="moe_pair",
    )(tile_ea, tile_eb, used, hsa, hsbg, wgu, wd, wgu, wd, after)
    return ysa, ysb


def _final_norm(x, moe_y, g, rows):
    t = x.shape[0]
    rows = min(rows, t)
    assert t % rows == 0
    tok = lambda i: (i, 0)
    return pl.pallas_call(
        _final_norm_kernel,
        grid=(t // rows,),
        in_specs=[pl.BlockSpec((rows, D_MODEL), tok), pl.BlockSpec((rows, PACK_W), tok),
                  pl.BlockSpec((rows, PACK_W), tok), _const_spec((1, D_MODEL))],
        out_specs=pl.BlockSpec((rows, D_MODEL), tok),
        out_shape=jax.ShapeDtypeStruct(x.shape, F32),
        compiler_params=pltpu.CompilerParams(
            dimension_semantics=("arbitrary",), vmem_limit_bytes=VMEM_LIMIT),
        name="final_norm",
    )(x, *moe_y, g.reshape(1, D_MODEL))


def _moe_dense(x, norm_g, router, w_gate, w_up, w_down, norm_final, layer, final_norm, rows, after,
               experts=(0, N_EXPERTS), chain=None):
    t = x.shape[0]
    assert t == rows
    e0, e1 = experts
    first, last = chain is None, e1 == N_EXPERTS
    assert first == (e0 == 0)
    tok = lambda i, e: (i, 0)
    w_spec = lambda shape: pl.BlockSpec((None, 1) + shape, lambda i, e: (layer, e0 + e, 0, 0))
    w_out = lambda shape: pl.BlockSpec((1,) + shape, lambda i, e: (e0 + e, 0, 0))
    n_in = 10
    return pl.pallas_call(
        functools.partial(_moe_dense_kernel, final_norm=final_norm, e0=e0, first=first, last=last),
        grid=(t // rows, e1 - e0),
        in_specs=[pl.BlockSpec((rows, D_MODEL), tok), _layer_spec((None, 1, D_MODEL), layer),
                  *_router_specs(layer),
                  w_spec((D_MODEL, D_EXPERT)), w_spec((D_MODEL, D_EXPERT)), w_spec((D_EXPERT, D_MODEL)),
                  _const_spec((1, D_MODEL)), _ANY_SPEC,
                  *([] if first else [pl.BlockSpec((rows, D_MODEL), tok), _ANY_SPEC, _ANY_SPEC])],
        out_specs=[pl.BlockSpec((rows, D_MODEL), tok), w_out((D_MODEL, 2 * D_EXPERT)),
                   w_out((D_EXPERT, D_MODEL))],
        out_shape=[jax.ShapeDtypeStruct(x.shape, F32),
                   jax.ShapeDtypeStruct((N_EXPERTS, D_MODEL, 2 * D_EXPERT), BF16),
                   jax.ShapeDtypeStruct((N_EXPERTS, D_EXPERT, D_MODEL), BF16)],
        input_output_aliases={} if first else {n_in + 1: 1, n_in + 2: 2},
        scratch_shapes=[pltpu.VMEM((rows, D_MODEL), BF16),
                        pltpu.VMEM((rows, LANES), F32),
                        pltpu.VMEM((rows, D_MODEL), F32)],
        compiler_params=pltpu.CompilerParams(
            dimension_semantics=("arbitrary", "arbitrary"), vmem_limit_bytes=VMEM_LIMIT),
        name="moe_dense",
    )(x, norm_g, *router, w_gate, w_up, w_down, norm_final.reshape(1, D_MODEL), after,
      *(() if first else chain))


def kernel(x_prompt, x_sample, cache_pool, state_ret, norm_mix, w_in, w_pool, pool_scale, ret_gn, w_out, norm_ffn, w_router_group, b_router_group, w_router_expert, b_router_expert, w_gate, w_up, w_down, norm_final):
    depth = norm_mix.shape[0]
    row = lambda a: a.reshape(depth, 1, a.shape[-1])
    mix_split = (row(norm_mix), *_split_weight(w_in), *_split_weight(w_pool), row(pool_scale),
                 row(ret_gn), *_split_weight(w_out))
    mix_hi = tuple(mix_split[i] for i in (0, 1, 3, 5, 6, 7))
    router = _router_weights(w_router_group, b_router_group, w_router_expert, b_router_expert)
    norm_ffn = row(norm_ffn)
    pool_prev = jnp.swapaxes(cache_pool, 1, 2)

    yp, ys = x_prompt, x_sample
    moe_p = None
    out_p = out_s = None
    ys_head = None
    s_blocks = x_sample.shape[0] // SAMPLE_SEQS
    s_head = max(1, min(s_blocks - 1, round(s_blocks * SAMPLE_HEAD)))
    for l in range(depth):
        yp, out_p, routed = _mixer_prompt(
            yp, moe_p, mix_split, (norm_ffn, *router), out_p, l, rows=512,
            precise_tail=PRECISE_TAIL_STEPS if l < depth - 1 else 0,
            after=ys if ys_head is None else ys_head)
        ys_mix, out_s = _mixer_sample(ys, pool_prev, state_ret, mix_hi, out_s, l, after=routed[-1],
                                      part=(s_head, s_blocks) if l else None, prev_y=ys_head)
        dispatched, pos, slot_tok = _moe_dispatch(routed, after=ys_mix)
        ys, wgu, wd = _moe_dense(ys_mix.reshape(-1, D_MODEL), norm_ffn, router, w_gate, w_up, w_down,
                                 norm_final, l, l == depth - 1, rows=ys_mix.shape[0] * ys_mix.shape[1],
                                 after=slot_tok)
        ys = ys.reshape(x_sample.shape)
        sorted_y = _moe_pair(dispatched, wgu, wd, after=ys)
        moe_p = _sc_gather(sorted_y, pos, after=routed[-1])
        if l + 1 < depth:
            ys_head, out_s = _mixer_sample(ys, pool_prev, state_ret, mix_hi, out_s, l + 1,
                                           after=sorted_y[0], part=(0, s_head))
    yp = _final_norm(yp.reshape(-1, D_MODEL), moe_p, norm_final, rows=2048).reshape(yp.shape)
    return (yp, ys, *out_p, jnp.swapaxes(out_s[0], 1, 2), out_s[1])
```
